```python
import math
import jax
import jax.numpy as jnp
from jax import lax
import numpy as np

D_MODEL = 2048
BATCH = 8
SEQ = 8192
DEPTH = 2

D_MIX = D_MODEL
N_MIXERS = 4
GROUP_WIDTH = D_MIX // N_MIXERS

S5_CH_PER_GROUP = 16
S5_GROUPS = GROUP_WIDTH // S5_CH_PER_GROUP
S5_STATE = 64
S5_STEP_MIN = 1e-3
S5_STEP_MAX = 1e-1

SGU_CHUNK = 128
SGU_HEADS = 8
SGU_HEAD_DIM = GROUP_WIDTH // SGU_HEADS

POOL_WINDOWS = (2, 4, 8, 16)
POOL_GROUPS = len(POOL_WINDOWS)
POOL_GROUP_DIM = GROUP_WIDTH // POOL_GROUPS

DN_HEAD_DIM = 128
DN_HEADS = GROUP_WIDTH // DN_HEAD_DIM
DN_CONV = 4
DN_CHUNK = 64
DN_DT_MIN = 1e-3
DN_DT_MAX = 1e-1

D_FF = 4 * D_MODEL

LN_EPS = 1e-5
RMS_EPS = 1e-6
L2_EPS = 1e-6
DEEPNORM_ALPHA = (2 * DEPTH) ** 0.25
DEEPNORM_BETA = (8 * DEPTH) ** -0.25

S5_OFF = 0
SGU_OFF = S5_OFF + GROUP_WIDTH
POOL_OFF = SGU_OFF + 2 * GROUP_WIDTH
DN_QKV_OFF = POOL_OFF + GROUP_WIDTH
DN_GATE_OFF = DN_QKV_OFF + 3 * GROUP_WIDTH
DN_A_OFF = DN_GATE_OFF + GROUP_WIDTH
DN_B_OFF = DN_A_OFF + DN_HEADS
IN_COLS = DN_B_OFF + DN_HEADS

kernel_name = 'hybrid_s5_sgu_pool_gdn_deepnorm'


def _layer_norm(x, g, b):
    xf = x.astype(jnp.float32)
    mu = jnp.mean(xf, axis=-1, keepdims=True)
    var = jnp.mean(jnp.square(xf - mu), axis=-1, keepdims=True)
    return (xf - mu) * lax.rsqrt(var + LN_EPS) * g.astype(jnp.float32) + b.astype(jnp.float32)


def _rms_norm(x, g):
    return x * lax.rsqrt(jnp.mean(jnp.square(x), axis=-1, keepdims=True) + RMS_EPS) * g.astype(jnp.float32)


def _l2_normalize(x):
    return x * lax.rsqrt(jnp.sum(jnp.square(x), axis=-1, keepdims=True) + L2_EPS)


def _linear_recurrence_combine(left, right):
    a_l, b_l = left
    a_r, b_r = right
    return a_l * a_r, a_r * b_l + b_r


def _s5_mixer(u, lam_re, lam_im, log_step, b_re, b_im, c_re, c_im, d, glu_w, glu_b):
    bsz, l, _ = u.shape
    uf = u.astype(jnp.float32).reshape(bsz, l, S5_GROUPS, S5_CH_PER_GROUP)
    lam = lax.complex(lam_re.astype(jnp.float32), lam_im.astype(jnp.float32))
    step = jnp.exp(log_step.astype(jnp.float32))[:, None]
    lam_bar = jnp.exp(lam * step)
    b_mat = lax.complex(b_re.astype(jnp.float32), b_im.astype(jnp.float32))
    b_bar = ((lam_bar - 1.0) / lam)[:, :, None] * b_mat
    bu = jnp.einsum('blgh,gph->blgp', uf.astype(jnp.complex64), b_bar)
    a = jnp.broadcast_to(lam_bar, bu.shape)
    _, states = lax.associative_scan(_linear_recurrence_combine, (a, bu), axis=1)
    c_mat = lax.complex(c_re.astype(jnp.float32), c_im.astype(jnp.float32))
    y = jnp.real(jnp.einsum('blgp,ghp->blgh', states, c_mat)) + d.astype(jnp.float32) * uf
    y = jax.nn.gelu(y.reshape(bsz, l, GROUP_WIDTH))
    return y * jax.nn.sigmoid(y @ glu_w.astype(jnp.float32) + glu_b.astype(jnp.float32))


def _sgu_mixer(z, norm_g, norm_b, w_s, b_s):
    z = jax.nn.gelu(z.astype(jnp.float32))
    u, v = jnp.split(z, 2, axis=-1)
    v = _layer_norm(v, norm_g, norm_b)
    bsz, l, _ = v.shape
    n = l // SGU_CHUNK
    v = v.reshape(bsz, n, SGU_CHUNK, SGU_HEADS, SGU_HEAD_DIM)
    causal = jnp.tril(jnp.ones((SGU_CHUNK, SGU_CHUNK), dtype=bool))
    w = jnp.where(causal, w_s.astype(jnp.float32), 0.0)
    bias = b_s.astype(jnp.float32).T[None, None, :, :, None]
    mixed = jnp.einsum('hts,bnshd->bnthd', w, v) + bias
    return u * mixed.reshape(bsz, l, GROUP_WIDTH)


def _pool_mixer(p, w_pool, scale):
    pf = p.astype(jnp.float32)
    bsz, l, _ = pf.shape
    groups = pf.reshape(bsz, l, POOL_GROUPS, POOL_GROUP_DIM)
    csum = jnp.cumsum(groups, axis=1)
    pos = jnp.arange(l)
    outs = []
    for gi, win in enumerate(POOL_WINDOWS):
        cs = csum[:, :, gi]
        prev = jnp.pad(cs, ((0, 0), (win, 0), (0, 0)))[:, :l]
        count = jnp.minimum(pos + 1, win).astype(jnp.float32)[None, :, None]
        outs.append((cs - prev) / count - groups[:, :, gi])
    pooled = jnp.stack(outs, axis=2)
    mixed = jnp.einsum('blgc,gcd->blgd', pooled, w_pool.astype(jnp.float32))
    return mixed.reshape(bsz, l, GROUP_WIDTH) * scale.astype(jnp.float32)


def _chunk_gated_delta_rule(q, k, v, g, beta):
    bsz, l, h, dk = q.shape
    dv = v.shape[-1]
    c = DN_CHUNK
    n = l // c

    def chunk4(t):
        return t.reshape(bsz, n, c, h, t.shape[-1]).transpose(0, 3, 1, 2, 4)

    def chunk3(t):
        return t.reshape(bsz, n, c, h).transpose(0, 3, 1, 2)

    q, k, v = chunk4(q), chunk4(k), chunk4(v)
    g, beta = chunk3(g), chunk3(beta)
    gc = jnp.cumsum(g, axis=-1)
    causal = jnp.tril(jnp.ones((c, c), dtype=bool))
    strict = jnp.tril(jnp.ones((c, c), dtype=bool), k=-1)
    decay = jnp.exp(jnp.where(causal, gc[..., :, None] - gc[..., None, :], -jnp.inf))
    k_beta = k * beta[..., None]
    v_beta = v * beta[..., None]
    eye = jnp.eye(c, dtype=jnp.float32)
    kk = jnp.einsum('bhncd,bhnsd->bhncs', k_beta, k) * decay
    a_mat = jnp.where(strict, kk, 0.0) + eye
    t_mat = lax.linalg.triangular_solve(a_mat, jnp.broadcast_to(eye, a_mat.shape),
                                        left_side=True, lower=True, unit_diagonal=True)
    u = jnp.einsum('bhncs,bhnsd->bhncd', t_mat, v_beta)
    w = jnp.einsum('bhncs,bhnsd->bhncd', t_mat, k_beta * jnp.exp(gc)[..., None])
    qk = jnp.einsum('bhncd,bhnsd->bhncs', q, k) * decay
    q_g = q * jnp.exp(gc)[..., None]
    k_tail = k * jnp.exp(gc[..., -1:] - gc)[..., None]
    g_last = jnp.exp(gc[..., -1])

    def step(state, inp):
        u_c, w_c, qg_c, qk_c, kt_c, gl_c = inp
        v_new = u_c - jnp.einsum('bhck,bhkv->bhcv', w_c, state)
        o_c = jnp.einsum('bhck,bhkv->bhcv', qg_c, state) + jnp.einsum('bhcs,bhsv->bhcv', qk_c, v_new)
        state = state * gl_c[..., None, None] + jnp.einsum('bhck,bhcv->bhkv', kt_c, v_new)
        return state, o_c

    xs = (jnp.moveaxis(u, 2, 0), jnp.moveaxis(w, 2, 0), jnp.moveaxis(q_g, 2, 0),
          jnp.moveaxis(qk, 2, 0), jnp.moveaxis(k_tail, 2, 0), jnp.moveaxis(g_last, 2, 0))
    state0 = jnp.zeros((bsz, h, dk, dv), jnp.float32)
    _, o = lax.scan(step, state0, xs)
    return o.transpose(1, 0, 3, 2, 4).reshape(bsz, l, h, dv)


def _deltanet_mixer(qkv, gate, a_logit, b_logit, conv_w, a_log, dt_bias, norm_g):
    bsz, l, _ = qkv.shape
    x3 = qkv.astype(jnp.float32)
    conv = lax.conv_general_dilated(x3, conv_w.astype(jnp.float32)[:, None, :], window_strides=(1,),
                                    padding=[(DN_CONV - 1, 0)], dimension_numbers=('NWC', 'WIO', 'NWC'),
                                    feature_group_count=3 * GROUP_WIDTH)
    x3 = jax.nn.silu(conv)
    q, k, v = jnp.split(x3, 3, axis=-1)
    q = _l2_normalize(q.reshape(bsz, l, DN_HEADS, DN_HEAD_DIM)) * (DN_HEAD_DIM ** -0.5)
    k = _l2_normalize(k.reshape(bsz, l, DN_HEADS, DN_HEAD_DIM))
    v = v.reshape(bsz, l, DN_HEADS, DN_HEAD_DIM)
    g = -jnp.exp(a_log.astype(jnp.float32)) * jax.nn.softplus(a_logit.astype(jnp.float32) + dt_bias.astype(jnp.float32))
    beta = jax.nn.sigmoid(b_logit.astype(jnp.float32))
    o = _chunk_gated_delta_rule(q, k, v, g, beta)
    gate = gate.astype(jnp.float32).reshape(bsz, l, DN_HEADS, DN_HEAD_DIM)
    o = _rms_norm(o, norm_g) * jax.nn.silu(gate)
    return o.reshape(bsz, l, GROUP_WIDTH)


def _fwd_setup_inputs(seed: int = 0) -> dict:
    key = jax.random.key(seed)
    ks = jax.random.split(key, 32)
    f32 = jnp.float32
    nl = DEPTH

    def nrm(k, shape, scale):
        return jax.random.normal(k, shape, f32) * scale

    x = nrm(ks[0], (BATCH, SEQ, D_MODEL), 1.0)
    w_in = nrm(ks[1], (nl, D_MODEL, IN_COLS), D_MODEL ** -0.5)
    n_idx = jnp.arange(S5_STATE, dtype=f32)
    s5_lambda_re = -0.5 + nrm(ks[2], (nl, S5_GROUPS, S5_STATE), 0.01)
    s5_lambda_im = math.pi * n_idx + nrm(ks[3], (nl, S5_GROUPS, S5_STATE), 0.01)
    s5_log_step = jax.random.uniform(ks[4], (nl, S5_GROUPS), f32, math.log(S5_STEP_MIN), math.log(S5_STEP_MAX))
    s5_b_re = nrm(ks[5], (nl, S5_GROUPS, S5_STATE, S5_CH_PER_GROUP), (2 * S5_CH_PER_GROUP) ** -0.5)
    s5_b_im = nrm(ks[6], (nl, S5_GROUPS, S5_STATE, S5_CH_PER_GROUP), (2 * S5_CH_PER_GROUP) ** -0.5)
    s5_c_re = nrm(ks[7], (nl, S5_GROUPS, S5_CH_PER_GROUP, S5_STATE), (2 * S5_STATE) ** -0.5)
    s5_c_im = nrm(ks[8], (nl, S5_GROUPS, S5_CH_PER_GROUP, S5_STATE), (2 * S5_STATE) ** -0.5)
    s5_d = nrm(ks[9], (nl, S5_GROUPS, S5_CH_PER_GROUP), 1.0)
    s5_glu_w = nrm(ks[10], (nl, GROUP_WIDTH, GROUP_WIDTH), GROUP_WIDTH ** -0.5)
    s5_glu_b = nrm(ks[11], (nl, GROUP_WIDTH), 0.01)
    sgu_norm_g = 1.0 + nrm(ks[12], (nl, GROUP_WIDTH), 0.02)
    sgu_norm_b = nrm(ks[13], (nl, GROUP_WIDTH), 0.02)
    sgu_w = nrm(ks[14], (nl, SGU_HEADS, SGU_CHUNK, SGU_CHUNK), SGU_CHUNK ** -0.5)
    sgu_b = 1.0 + nrm(ks[15], (nl, SGU_HEADS, SGU_CHUNK), 0.02)
    pool_w = nrm(ks[16], (nl, POOL_GROUPS, POOL_GROUP_DIM, POOL_GROUP_DIM), POOL_GROUP_DIM ** -0.5)
    pool_scale = 1.0 + nrm(ks[17], (nl, GROUP_WIDTH), 0.02)
    dn_conv_w = nrm(ks[18], (nl, DN_CONV, 3 * GROUP_WIDTH), DN_CONV ** -0.5)
    dn_a_log = jnp.log(jax.random.uniform(ks[19], (nl, DN_HEADS), f32, 1.0, 16.0))
    dt = jnp.exp(jax.random.uniform(ks[20], (nl, DN_HEADS), f32, math.log(DN_DT_MIN), math.log(DN_DT_MAX)))
    dn_dt_bias = dt + jnp.log(-jnp.expm1(-dt))
    dn_norm_g = 1.0 + nrm(ks[21], (nl, DN_HEAD_DIM), 0.02)
    w_out = nrm(ks[22], (nl, D_MIX, D_MODEL), D_MIX ** -0.5) * DEEPNORM_BETA
    ln1_g = 1.0 + nrm(ks[23], (nl, D_MODEL), 0.02)
    ln1_b = nrm(ks[24], (nl, D_MODEL), 0.02)
    w_up = nrm(ks[25], (nl, D_MODEL, D_FF), D_MODEL ** -0.5)
    w_down = nrm(ks[26], (nl, D_FF, D_MODEL), D_FF ** -0.5) * DEEPNORM_BETA
    ln2_g = 1.0 + nrm(ks[27], (nl, D_MODEL), 0.02)
    ln2_b = nrm(ks[28], (nl, D_MODEL), 0.02)
    return {'x': x, 'w_in': w_in, 's5_lambda_re': s5_lambda_re, 's5_lambda_im': s5_lambda_im,
            's5_log_step': s5_log_step, 's5_b_re': s5_b_re, 's5_b_im': s5_b_im, 's5_c_re': s5_c_re,
            's5_c_im': s5_c_im, 's5_d': s5_d, 's5_glu_w': s5_glu_w, 's5_glu_b': s5_glu_b,
            'sgu_norm_g': sgu_norm_g, 'sgu_norm_b': sgu_norm_b, 'sgu_w': sgu_w, 'sgu_b': sgu_b,
            'pool_w': pool_w, 'pool_scale': pool_scale, 'dn_conv_w': dn_conv_w, 'dn_a_log': dn_a_log,
            'dn_dt_bias': dn_dt_bias, 'dn_norm_g': dn_norm_g, 'w_out': w_out, 'ln1_g': ln1_g,
            'ln1_b': ln1_b, 'w_up': w_up, 'w_down': w_down, 'ln2_g': ln2_g, 'ln2_b': ln2_b}


def _fwd_reference(x, w_in, s5_lambda_re, s5_lambda_im, s5_log_step, s5_b_re, s5_b_im, s5_c_re, s5_c_im,
              s5_d, s5_glu_w, s5_glu_b, sgu_norm_g, sgu_norm_b, sgu_w, sgu_b, pool_w, pool_scale,
              dn_conv_w, dn_a_log, dn_dt_bias, dn_norm_g, w_out, ln1_g, ln1_b, w_up, w_down,
              ln2_g, ln2_b):
    for i in range(DEPTH):
        proj = x @ w_in[i]
        y_s5 = _s5_mixer(proj[..., S5_OFF:SGU_OFF], s5_lambda_re[i], s5_lambda_im[i], s5_log_step[i],
                         s5_b_re[i], s5_b_im[i], s5_c_re[i], s5_c_im[i], s5_d[i], s5_glu_w[i], s5_glu_b[i])
        y_sgu = _sgu_mixer(proj[..., SGU_OFF:POOL_OFF], sgu_norm_g[i], sgu_norm_b[i], sgu_w[i], sgu_b[i])
        y_pool = _pool_mixer(proj[..., POOL_OFF:DN_QKV_OFF], pool_w[i], pool_scale[i])
        y_dn = _deltanet_mixer(proj[..., DN_QKV_OFF:DN_GATE_OFF], proj[..., DN_GATE_OFF:DN_A_OFF],
                               proj[..., DN_A_OFF:DN_B_OFF], proj[..., DN_B_OFF:IN_COLS],
                               dn_conv_w[i], dn_a_log[i], dn_dt_bias[i], dn_norm_g[i])
        mixed = jnp.concatenate([y_s5, y_sgu, y_pool, y_dn], axis=-1).astype(x.dtype)
        x = _layer_norm(DEEPNORM_ALPHA * x + mixed @ w_out[i], ln1_g[i], ln1_b[i]).astype(x.dtype)
        hidden = jnp.square(jax.nn.relu(x @ w_up[i]))
        x = _layer_norm(DEEPNORM_ALPHA * x + hidden @ w_down[i], ln2_g[i], ln2_b[i]).astype(x.dtype)
    return x


import jax as _jax
import jax.numpy as _jnp

TWIN_FORMAT = 'train_step'
FWD_PARAMS = ['x', 'w_in', 's5_lambda_re', 's5_lambda_im', 's5_log_step', 's5_b_re', 's5_b_im', 's5_c_re', 's5_c_im', 's5_d', 's5_glu_w', 's5_glu_b', 'sgu_norm_g', 'sgu_norm_b', 'sgu_w', 'sgu_b', 'pool_w', 'pool_scale', 'dn_conv_w', 'dn_a_log', 'dn_dt_bias', 'dn_norm_g', 'w_out', 'ln1_g', 'ln1_b', 'w_up', 'w_down', 'ln2_g', 'ln2_b']
TWIN_WEIGHTS = ['w_in', 's5_lambda_re', 's5_lambda_im', 's5_log_step', 's5_b_re', 's5_b_im', 's5_c_re', 's5_c_im', 's5_d', 's5_glu_w', 's5_glu_b', 'sgu_norm_g', 'sgu_norm_b', 'sgu_w', 'sgu_b', 'pool_w', 'pool_scale', 'dn_conv_w', 'dn_a_log', 'dn_dt_bias', 'dn_norm_g', 'w_out', 'ln1_g', 'ln1_b', 'w_up', 'w_down', 'ln2_g', 'ln2_b']
TWIN_DIFF_INPUT = 'x'
TWIN_INPUTS = ['x', 'w_in', 's5_lambda_re', 's5_lambda_im', 's5_log_step', 's5_b_re', 's5_b_im', 's5_c_re', 's5_c_im', 's5_d', 's5_glu_w', 's5_glu_b', 'sgu_norm_g', 'sgu_norm_b', 'sgu_w', 'sgu_b', 'pool_w', 'pool_scale', 'dn_conv_w', 'dn_a_log', 'dn_dt_bias', 'dn_norm_g', 'w_out', 'ln1_g', 'ln1_b', 'w_up', 'w_down', 'ln2_g', 'ln2_b', 'loss_target', 'm_w_in', 'm_s5_lambda_re', 'm_s5_lambda_im', 'm_s5_log_step', 'm_s5_b_re', 'm_s5_b_im', 'm_s5_c_re', 'm_s5_c_im', 'm_s5_d', 'm_s5_glu_w', 'm_s5_glu_b', 'm_sgu_norm_g', 'm_sgu_norm_b', 'm_sgu_w', 'm_sgu_b', 'm_pool_w', 'm_pool_scale', 'm_dn_conv_w', 'm_dn_a_log', 'm_dn_dt_bias', 'm_dn_norm_g', 'm_w_out', 'm_ln1_g', 'm_ln1_b', 'm_w_up', 'm_w_down', 'm_ln2_g', 'm_ln2_b', 'v_w_in', 'v_s5_lambda_re', 'v_s5_lambda_im', 'v_s5_log_step', 'v_s5_b_re', 'v_s5_b_im', 'v_s5_c_re', 'v_s5_c_im', 'v_s5_d', 'v_s5_glu_w', 'v_s5_glu_b', 'v_sgu_norm_g', 'v_sgu_norm_b', 'v_sgu_w', 'v_sgu_b', 'v_pool_w', 'v_pool_scale', 'v_dn_conv_w', 'v_dn_a_log', 'v_dn_dt_bias', 'v_dn_norm_g', 'v_w_out', 'v_ln1_g', 'v_ln1_b', 'v_w_up', 'v_w_down', 'v_ln2_g', 'v_ln2_b']
TWIN_OUTPUTS = ['loss', 'grad_x', 'grad_w_in', 'grad_s5_lambda_re', 'grad_s5_lambda_im', 'grad_s5_log_step', 'grad_s5_b_re', 'grad_s5_b_im', 'grad_s5_c_re', 'grad_s5_c_im', 'grad_s5_d', 'grad_s5_glu_w', 'grad_s5_glu_b', 'grad_sgu_norm_g', 'grad_sgu_norm_b', 'grad_sgu_w', 'grad_sgu_b', 'grad_pool_w', 'grad_pool_scale', 'grad_dn_conv_w', 'grad_dn_a_log', 'grad_dn_dt_bias', 'grad_dn_norm_g', 'grad_w_out', 'grad_ln1_g', 'grad_ln1_b', 'grad_w_up', 'grad_w_down', 'grad_ln2_g', 'grad_ln2_b', 'delta_w_in', 'delta_s5_lambda_re', 'delta_s5_lambda_im', 'delta_s5_log_step', 'delta_s5_b_re', 'delta_s5_b_im', 'delta_s5_c_re', 'delta_s5_c_im', 'delta_s5_d', 'delta_s5_glu_w', 'delta_s5_glu_b', 'delta_sgu_norm_g', 'delta_sgu_norm_b', 'delta_sgu_w', 'delta_sgu_b', 'delta_pool_w', 'delta_pool_scale', 'delta_dn_conv_w', 'delta_dn_a_log', 'delta_dn_dt_bias', 'delta_dn_norm_g', 'delta_w_out', 'delta_ln1_g', 'delta_ln1_b', 'delta_w_up', 'delta_w_down', 'delta_ln2_g', 'delta_ln2_b', 'new_m_w_in', 'new_m_s5_lambda_re', 'new_m_s5_lambda_im', 'new_m_s5_log_step', 'new_m_s5_b_re', 'new_m_s5_b_im', 'new_m_s5_c_re', 'new_m_s5_c_im', 'new_m_s5_d', 'new_m_s5_glu_w', 'new_m_s5_glu_b', 'new_m_sgu_norm_g', 'new_m_sgu_norm_b', 'new_m_sgu_w', 'new_m_sgu_b', 'new_m_pool_w', 'new_m_pool_scale', 'new_m_dn_conv_w', 'new_m_dn_a_log', 'new_m_dn_dt_bias', 'new_m_dn_norm_g', 'new_m_w_out', 'new_m_ln1_g', 'new_m_ln1_b', 'new_m_w_up', 'new_m_w_down', 'new_m_ln2_g', 'new_m_ln2_b', 'new_v_w_in', 'new_v_s5_lambda_re', 'new_v_s5_lambda_im', 'new_v_s5_log_step', 'new_v_s5_b_re', 'new_v_s5_b_im', 'new_v_s5_c_re', 'new_v_s5_c_im', 'new_v_s5_d', 'new_v_s5_glu_w', 'new_v_s5_glu_b', 'new_v_sgu_norm_g', 'new_v_sgu_norm_b', 'new_v_sgu_w', 'new_v_sgu_b', 'new_v_pool_w', 'new_v_pool_scale', 'new_v_dn_conv_w', 'new_v_dn_a_log', 'new_v_dn_dt_bias', 'new_v_dn_norm_g', 'new_v_w_out', 'new_v_ln1_g', 'new_v_ln1_b', 'new_v_w_up', 'new_v_w_down', 'new_v_ln2_g', 'new_v_ln2_b']
TWIN_LEAF_KINDS = {'loss': 'loss', 'grad_x': 'grad_x', 'grad_w_in': 'grad_w', 'grad_s5_lambda_re': 'grad_w', 'grad_s5_lambda_im': 'grad_w', 'grad_s5_log_step': 'grad_w', 'grad_s5_b_re': 'grad_w', 'grad_s5_b_im': 'grad_w', 'grad_s5_c_re': 'grad_w', 'grad_s5_c_im': 'grad_w', 'grad_s5_d': 'grad_w', 'grad_s5_glu_w': 'grad_w', 'grad_s5_glu_b': 'grad_w', 'grad_sgu_norm_g': 'grad_w', 'grad_sgu_norm_b': 'grad_w', 'grad_sgu_w': 'grad_w', 'grad_sgu_b': 'grad_w', 'grad_pool_w': 'grad_w', 'grad_pool_scale': 'grad_w', 'grad_dn_conv_w': 'grad_w', 'grad_dn_a_log': 'grad_w', 'grad_dn_dt_bias': 'grad_w', 'grad_dn_norm_g': 'grad_w', 'grad_w_out': 'grad_w', 'grad_ln1_g': 'grad_w', 'grad_ln1_b': 'grad_w', 'grad_w_up': 'grad_w', 'grad_w_down': 'grad_w', 'grad_ln2_g': 'grad_w', 'grad_ln2_b': 'grad_w', 'delta_w_in': 'delta_w', 'delta_s5_lambda_re': 'delta_w', 'delta_s5_lambda_im': 'delta_w', 'delta_s5_log_step': 'delta_w', 'delta_s5_b_re': 'delta_w', 'delta_s5_b_im': 'delta_w', 'delta_s5_c_re': 'delta_w', 'delta_s5_c_im': 'delta_w', 'delta_s5_d': 'delta_w', 'delta_s5_glu_w': 'delta_w', 'delta_s5_glu_b': 'delta_w', 'delta_sgu_norm_g': 'delta_w', 'delta_sgu_norm_b': 'delta_w', 'delta_sgu_w': 'delta_w', 'delta_sgu_b': 'delta_w', 'delta_pool_w': 'delta_w', 'delta_pool_scale': 'delta_w', 'delta_dn_conv_w': 'delta_w', 'delta_dn_a_log': 'delta_w', 'delta_dn_dt_bias': 'delta_w', 'delta_dn_norm_g': 'delta_w', 'delta_w_out': 'delta_w', 'delta_ln1_g': 'delta_w', 'delta_ln1_b': 'delta_w', 'delta_w_up': 'delta_w', 'delta_w_down': 'delta_w', 'delta_ln2_g': 'delta_w', 'delta_ln2_b': 'delta_w', 'new_m_w_in': 'new_m', 'new_m_s5_lambda_re': 'new_m', 'new_m_s5_lambda_im': 'new_m', 'new_m_s5_log_step': 'new_m', 'new_m_s5_b_re': 'new_m', 'new_m_s5_b_im': 'new_m', 'new_m_s5_c_re': 'new_m', 'new_m_s5_c_im': 'new_m', 'new_m_s5_d': 'new_m', 'new_m_s5_glu_w': 'new_m', 'new_m_s5_glu_b': 'new_m', 'new_m_sgu_norm_g': 'new_m', 'new_m_sgu_norm_b': 'new_m', 'new_m_sgu_w': 'new_m', 'new_m_sgu_b': 'new_m', 'new_m_pool_w': 'new_m', 'new_m_pool_scale': 'new_m', 'new_m_dn_conv_w': 'new_m', 'new_m_dn_a_log': 'new_m', 'new_m_dn_dt_bias': 'new_m', 'new_m_dn_norm_g': 'new_m', 'new_m_w_out': 'new_m', 'new_m_ln1_g': 'new_m', 'new_m_ln1_b': 'new_m', 'new_m_w_up': 'new_m', 'new_m_w_down': 'new_m', 'new_m_ln2_g': 'new_m', 'new_m_ln2_b': 'new_m', 'new_v_w_in': 'new_v', 'new_v_s5_lambda_re': 'new_v', 'new_v_s5_lambda_im': 'new_v', 'new_v_s5_log_step': 'new_v', 'new_v_s5_b_re': 'new_v', 'new_v_s5_b_im': 'new_v', 'new_v_s5_c_re': 'new_v', 'new_v_s5_c_im': 'new_v', 'new_v_s5_d': 'new_v', 'new_v_s5_glu_w': 'new_v', 'new_v_s5_glu_b': 'new_v', 'new_v_sgu_norm_g': 'new_v', 'new_v_sgu_norm_b': 'new_v', 'new_v_sgu_w': 'new_v', 'new_v_sgu_b': 'new_v', 'new_v_pool_w': 'new_v', 'new_v_pool_scale': 'new_v', 'new_v_dn_conv_w': 'new_v', 'new_v_dn_a_log': 'new_v', 'new_v_dn_dt_bias': 'new_v', 'new_v_dn_norm_g': 'new_v', 'new_v_w_out': 'new_v', 'new_v_ln1_g': 'new_v', 'new_v_ln1_b': 'new_v', 'new_v_w_up': 'new_v', 'new_v_w_down': 'new_v', 'new_v_ln2_g': 'new_v', 'new_v_ln2_b': 'new_v'}


def _forward(args):
    return _fwd_reference(*[args[k] for k in FWD_PARAMS])


def _output_shape():
    def fwd():
        inp = _fwd_setup_inputs(0)
        return _fwd_reference(*[inp[k] for k in FWD_PARAMS])
    out = _jax.eval_shape(fwd)
    return out.shape, out.dtype

N_MICROBATCH = 1
ADAM_LR = 0.001
ADAM_B1 = 0.9
ADAM_B2 = 0.999
ADAM_EPS = 1e-08
ADAM_WD = 0.01
ADAM_STEP = 10
PER_EXAMPLE_BATCH_AXIS = {'x': 0, 'loss_target': 0}
SHARED_INPUTS = []
_WEIGHT_DTYPES = {'w_in': _jnp.float32, 's5_lambda_re': _jnp.float32, 's5_lambda_im': _jnp.float32, 's5_log_step': _jnp.float32, 's5_b_re': _jnp.float32, 's5_b_im': _jnp.float32, 's5_c_re': _jnp.float32, 's5_c_im': _jnp.float32, 's5_d': _jnp.float32, 's5_glu_w': _jnp.float32, 's5_glu_b': _jnp.float32, 'sgu_norm_g': _jnp.float32, 'sgu_norm_b': _jnp.float32, 'sgu_w': _jnp.float32, 'sgu_b': _jnp.float32, 'pool_w': _jnp.float32, 'pool_scale': _jnp.float32, 'dn_conv_w': _jnp.float32, 'dn_a_log': _jnp.float32, 'dn_dt_bias': _jnp.float32, 'dn_norm_g': _jnp.float32, 'w_out': _jnp.float32, 'ln1_g': _jnp.float32, 'ln1_b': _jnp.float32, 'w_up': _jnp.float32, 'w_down': _jnp.float32, 'ln2_g': _jnp.float32, 'ln2_b': _jnp.float32}
MOMENT_SCALE = {'w_in': 2.699847e-02, 's5_lambda_re': 1.203760e-03, 's5_lambda_im': 9.280270e-04, 's5_log_step': 5.844442e-01, 's5_b_re': 7.103059e-04, 's5_b_im': 5.869705e-04, 's5_c_re': 1.319728e-03, 's5_c_im': 1.330170e-03, 's5_d': 3.064262e-02, 's5_glu_w': 5.529463e-03, 's5_glu_b': 1.200738e-02, 'sgu_norm_g': 2.124209e-02, 'sgu_norm_b': 2.147994e-02, 'sgu_w': 1.478416e-02, 'sgu_b': 2.145724e-02, 'pool_w': 3.934434e-02, 'pool_scale': 3.870619e-02, 'dn_conv_w': 2.238091e-02, 'dn_a_log': 1.317832e-01, 'dn_dt_bias': 1.277604e-01, 'dn_norm_g': 5.664997e-02, 'w_out': 7.579972e-02, 'ln1_g': 8.131170e-01, 'ln1_b': 5.172824e-01, 'w_up': 3.010973e-02, 'w_down': 1.712417e-01, 'ln2_g': 2.273334e+01, 'ln2_b': 5.391098e+00}


def _to_microbatches(a, axis):
    t = _jnp.moveaxis(a, axis, 0)
    t = t.reshape((N_MICROBATCH, t.shape[0] // N_MICROBATCH) + t.shape[1:])
    return _jnp.moveaxis(t, 1, axis + 1)


def setup_inputs(seed: int = 0) -> dict:
    inp = _fwd_setup_inputs(seed)
    key = _jax.random.fold_in(_jax.random.key(seed), 7919)
    shape, _ = _output_shape()
    out = dict(inp)
    out["loss_target"] = _jax.random.normal(_jax.random.fold_in(key, 0), shape, _jnp.float32)
    for i, name in enumerate(TWIN_WEIGHTS):
        w = inp[name].astype(_jnp.float32)
        if MOMENT_SCALE is None:
            s = _jnp.sqrt(_jnp.mean(_jnp.square(w)) + 1e-30)
        else:
            s = MOMENT_SCALE[name]
        km, kv = _jax.random.split(_jax.random.fold_in(key, i + 1))
        out[name] = w
        out["m_" + name] = s * _jax.random.normal(km, w.shape, _jnp.float32)
        out["v_" + name] = (s * s) * _jax.random.uniform(kv, w.shape, _jnp.float32, 0.5, 1.5)
    if N_MICROBATCH > 1:
        for name, axis in PER_EXAMPLE_BATCH_AXIS.items():
            out[name] = _to_microbatches(out[name], axis)
    return {'x': out['x'], 'w_in': out['w_in'], 's5_lambda_re': out['s5_lambda_re'], 's5_lambda_im': out['s5_lambda_im'], 's5_log_step': out['s5_log_step'], 's5_b_re': out['s5_b_re'], 's5_b_im': out['s5_b_im'], 's5_c_re': out['s5_c_re'], 's5_c_im': out['s5_c_im'], 's5_d': out['s5_d'], 's5_glu_w': out['s5_glu_w'], 's5_glu_b': out['s5_glu_b'], 'sgu_norm_g': out['sgu_norm_g'], 'sgu_norm_b': out['sgu_norm_b'], 'sgu_w': out['sgu_w'], 'sgu_b': out['sgu_b'], 'pool_w': out['pool_w'], 'pool_scale': out['pool_scale'], 'dn_conv_w': out['dn_conv_w'], 'dn_a_log': out['dn_a_log'], 'dn_dt_bias': out['dn_dt_bias'], 'dn_norm_g': out['dn_norm_g'], 'w_out': out['w_out'], 'ln1_g': out['ln1_g'], 'ln1_b': out['ln1_b'], 'w_up': out['w_up'], 'w_down': out['w_down'], 'ln2_g': out['ln2_g'], 'ln2_b': out['ln2_b'], 'loss_target': out['loss_target'], 'm_w_in': out['m_w_in'], 'm_s5_lambda_re': out['m_s5_lambda_re'], 'm_s5_lambda_im': out['m_s5_lambda_im'], 'm_s5_log_step': out['m_s5_log_step'], 'm_s5_b_re': out['m_s5_b_re'], 'm_s5_b_im': out['m_s5_b_im'], 'm_s5_c_re': out['m_s5_c_re'], 'm_s5_c_im': out['m_s5_c_im'], 'm_s5_d': out['m_s5_d'], 'm_s5_glu_w': out['m_s5_glu_w'], 'm_s5_glu_b': out['m_s5_glu_b'], 'm_sgu_norm_g': out['m_sgu_norm_g'], 'm_sgu_norm_b': out['m_sgu_norm_b'], 'm_sgu_w': out['m_sgu_w'], 'm_sgu_b': out['m_sgu_b'], 'm_pool_w': out['m_pool_w'], 'm_pool_scale': out['m_pool_scale'], 'm_dn_conv_w': out['m_dn_conv_w'], 'm_dn_a_log': out['m_dn_a_log'], 'm_dn_dt_bias': out['m_dn_dt_bias'], 'm_dn_norm_g': out['m_dn_norm_g'], 'm_w_out': out['m_w_out'], 'm_ln1_g': out['m_ln1_g'], 'm_ln1_b': out['m_ln1_b'], 'm_w_up': out['m_w_up'], 'm_w_down': out['m_w_down'], 'm_ln2_g': out['m_ln2_g'], 'm_ln2_b': out['m_ln2_b'], 'v_w_in': out['v_w_in'], 'v_s5_lambda_re': out['v_s5_lambda_re'], 'v_s5_lambda_im': out['v_s5_lambda_im'], 'v_s5_log_step': out['v_s5_log_step'], 'v_s5_b_re': out['v_s5_b_re'], 'v_s5_b_im': out['v_s5_b_im'], 'v_s5_c_re': out['v_s5_c_re'], 'v_s5_c_im': out['v_s5_c_im'], 'v_s5_d': out['v_s5_d'], 'v_s5_glu_w': out['v_s5_glu_w'], 'v_s5_glu_b': out['v_s5_glu_b'], 'v_sgu_norm_g': out['v_sgu_norm_g'], 'v_sgu_norm_b': out['v_sgu_norm_b'], 'v_sgu_w': out['v_sgu_w'], 'v_sgu_b': out['v_sgu_b'], 'v_pool_w': out['v_pool_w'], 'v_pool_scale': out['v_pool_scale'], 'v_dn_conv_w': out['v_dn_conv_w'], 'v_dn_a_log': out['v_dn_a_log'], 'v_dn_dt_bias': out['v_dn_dt_bias'], 'v_dn_norm_g': out['v_dn_norm_g'], 'v_w_out': out['v_w_out'], 'v_ln1_g': out['v_ln1_g'], 'v_ln1_b': out['v_ln1_b'], 'v_w_up': out['v_w_up'], 'v_w_down': out['v_w_down'], 'v_ln2_g': out['v_ln2_g'], 'v_ln2_b': out['v_ln2_b']}


def _loss(weights, diff, rest, loss_target):
    with _jax.named_scope("forward"):
        args = {**rest, TWIN_DIFF_INPUT: diff, **{k: w.astype(_WEIGHT_DTYPES[k]) for k, w in weights.items()}}
        y = _forward(args)
    with _jax.named_scope("loss_head"):
        err = _jnp.square(y.astype(_jnp.float32) - loss_target)
        return 0.5 * _jnp.sum(_jnp.mean(err, axis=-1)) if err.ndim else 0.5 * err


def _adamw(w, g, m, v):
    m = ADAM_B1 * m + (1.0 - ADAM_B1) * g
    v = ADAM_B2 * v + (1.0 - ADAM_B2) * _jnp.square(g)
    m_hat = m / (1.0 - ADAM_B1 ** ADAM_STEP)
    v_hat = v / (1.0 - ADAM_B2 ** ADAM_STEP)
    delta = -ADAM_LR * (m_hat / (_jnp.sqrt(v_hat) + ADAM_EPS) + ADAM_WD * w)
    return delta, m, v


def reference(x, w_in, s5_lambda_re, s5_lambda_im, s5_log_step, s5_b_re, s5_b_im, s5_c_re, s5_c_im, s5_d, s5_glu_w, s5_glu_b, sgu_norm_g, sgu_norm_b, sgu_w, sgu_b, pool_w, pool_scale, dn_conv_w, dn_a_log, dn_dt_bias, dn_norm_g, w_out, ln1_g, ln1_b, w_up, w_down, ln2_g, ln2_b, loss_target, m_w_in, m_s5_lambda_re, m_s5_lambda_im, m_s5_log_step, m_s5_b_re, m_s5_b_im, m_s5_c_re, m_s5_c_im, m_s5_d, m_s5_glu_w, m_s5_glu_b, m_sgu_norm_g, m_sgu_norm_b, m_sgu_w, m_sgu_b, m_pool_w, m_pool_scale, m_dn_conv_w, m_dn_a_log, m_dn_dt_bias, m_dn_norm_g, m_w_out, m_ln1_g, m_ln1_b, m_w_up, m_w_down, m_ln2_g, m_ln2_b, v_w_in, v_s5_lambda_re, v_s5_lambda_im, v_s5_log_step, v_s5_b_re, v_s5_b_im, v_s5_c_re, v_s5_c_im, v_s5_d, v_s5_glu_w, v_s5_glu_b, v_sgu_norm_g, v_sgu_norm_b, v_sgu_w, v_sgu_b, v_pool_w, v_pool_scale, v_dn_conv_w, v_dn_a_log, v_dn_dt_bias, v_dn_norm_g, v_w_out, v_ln1_g, v_ln1_b, v_w_up, v_w_down, v_ln2_g, v_ln2_b):
    given = dict(x=x, w_in=w_in, s5_lambda_re=s5_lambda_re, s5_lambda_im=s5_lambda_im, s5_log_step=s5_log_step, s5_b_re=s5_b_re, s5_b_im=s5_b_im, s5_c_re=s5_c_re, s5_c_im=s5_c_im, s5_d=s5_d, s5_glu_w=s5_glu_w, s5_glu_b=s5_glu_b, sgu_norm_g=sgu_norm_g, sgu_norm_b=sgu_norm_b, sgu_w=sgu_w, sgu_b=sgu_b, pool_w=pool_w, pool_scale=pool_scale, dn_conv_w=dn_conv_w, dn_a_log=dn_a_log, dn_dt_bias=dn_dt_bias, dn_norm_g=dn_norm_g, w_out=w_out, ln1_g=ln1_g, ln1_b=ln1_b, w_up=w_up, w_down=w_down, ln2_g=ln2_g, ln2_b=ln2_b, loss_target=loss_target, m_w_in=m_w_in, m_s5_lambda_re=m_s5_lambda_re, m_s5_lambda_im=m_s5_lambda_im, m_s5_log_step=m_s5_log_step, m_s5_b_re=m_s5_b_re, m_s5_b_im=m_s5_b_im, m_s5_c_re=m_s5_c_re, m_s5_c_im=m_s5_c_im, m_s5_d=m_s5_d, m_s5_glu_w=m_s5_glu_w, m_s5_glu_b=m_s5_glu_b, m_sgu_norm_g=m_sgu_norm_g, m_sgu_norm_b=m_sgu_norm_b, m_sgu_w=m_sgu_w, m_sgu_b=m_sgu_b, m_pool_w=m_pool_w, m_pool_scale=m_pool_scale, m_dn_conv_w=m_dn_conv_w, m_dn_a_log=m_dn_a_log, m_dn_dt_bias=m_dn_dt_bias, m_dn_norm_g=m_dn_norm_g, m_w_out=m_w_out, m_ln1_g=m_ln1_g, m_ln1_b=m_ln1_b, m_w_up=m_w_up, m_w_down=m_w_down, m_ln2_g=m_ln2_g, m_ln2_b=m_ln2_b, v_w_in=v_w_in, v_s5_lambda_re=v_s5_lambda_re, v_s5_lambda_im=v_s5_lambda_im, v_s5_log_step=v_s5_log_step, v_s5_b_re=v_s5_b_re, v_s5_b_im=v_s5_b_im, v_s5_c_re=v_s5_c_re, v_s5_c_im=v_s5_c_im, v_s5_d=v_s5_d, v_s5_glu_w=v_s5_glu_w, v_s5_glu_b=v_s5_glu_b, v_sgu_norm_g=v_sgu_norm_g, v_sgu_norm_b=v_sgu_norm_b, v_sgu_w=v_sgu_w, v_sgu_b=v_sgu_b, v_pool_w=v_pool_w, v_pool_scale=v_pool_scale, v_dn_conv_w=v_dn_conv_w, v_dn_a_log=v_dn_a_log, v_dn_dt_bias=v_dn_dt_bias, v_dn_norm_g=v_dn_norm_g, v_w_out=v_w_out, v_ln1_g=v_ln1_g, v_ln1_b=v_ln1_b, v_w_up=v_w_up, v_w_down=v_w_down, v_ln2_g=v_ln2_g, v_ln2_b=v_ln2_b)
    weights = {n: given[n] for n in TWIN_WEIGHTS}
    shared = {n: given[n] for n in SHARED_INPUTS}
    per_example = {n: given[n] for n in ['x']}
    grad_fn = _jax.value_and_grad(_loss, argnums=(0, 1))

    def one_microbatch(ex, loss_target):
        ex = dict(ex)
        diff = ex.pop(TWIN_DIFF_INPUT)
        return grad_fn(weights, diff, {**shared, **ex}, loss_target)

    if N_MICROBATCH == 1:
        loss, (grad_w, grad_x) = one_microbatch(per_example, given["loss_target"])
    else:
        def body(carry, xs):
            loss_sum, grad_sum = carry
            l_k, (gw_k, gx_k) = one_microbatch(xs[0], xs[1])
            with _jax.named_scope("update"):
                return (loss_sum + l_k, _jax.tree.map(_jnp.add, grad_sum, gw_k)), gx_k

        init = (_jnp.zeros((), _jnp.float32), _jax.tree.map(_jnp.zeros_like, weights))
        (loss, grad_w), grad_x = _jax.lax.scan(body, init, (per_example, given["loss_target"]))
    with _jax.named_scope("update"):
        delta_w, new_m, new_v = {}, {}, {}
        for n in TWIN_WEIGHTS:
            delta_w[n], new_m[n], new_v[n] = _adamw(weights[n], grad_w[n], given["m_" + n], given["v_" + n])
    return (loss, grad_x, *[grad_w[n] for n in TWIN_WEIGHTS], *[delta_w[n] for n in TWIN_WEIGHTS],
            *[new_m[n] for n in TWIN_WEIGHTS], *[new_v[n] for n in TWIN_WEIGHTS])
```

```python
import functools
import math

import jax
import jax.numpy as jnp
from jax import lax
from jax.experimental import pallas as pl
from jax.experimental.pallas import tpu as pltpu

F32 = jnp.float32
BF16 = jnp.bfloat16
MXU_DTYPE = jnp.bfloat16
HI = lax.Precision.HIGHEST

N_DEV = 8
D_MODEL = 2048
DEPTH = 2
GROUP_WIDTH = 512
S5_GROUPS, S5_CH, S5_STATE = 32, 16, 64
S5_NS = S5_GROUPS * S5_STATE
SGU_CHUNK, SGU_HEADS = 128, 8
POOL_WINDOWS = (2, 4, 8, 16)
DN_HEADS, DN_HEAD_DIM, DN_CONV, DN_CHUNK = 4, 128, 4, 64
D_FF = 4 * D_MODEL
LN_EPS, RMS_EPS, L2_EPS = 1e-5, 1e-6, 1e-6
ALPHA = (2 * DEPTH) ** 0.25
MAIN_COLS = 4096
AB_PAD = 128
ADAM_LR, ADAM_B1, ADAM_B2, ADAM_EPS, ADAM_WD, ADAM_STEP = 0.001, 0.9, 0.999, 1e-08, 0.01, 10
VMEM_LIMIT = 56 * 1024 * 1024
C_GELU = math.sqrt(2.0 / math.pi)


def _params(sem=None):
    return pltpu.CompilerParams(dimension_semantics=sem, vmem_limit_bytes=VMEM_LIMIT)


def _gelu(x):
    return 0.5 * x * (1.0 + jnp.tanh(C_GELU * (x + 0.044715 * x * x * x)))


def _gelu_grad(x):
    t = jnp.tanh(C_GELU * (x + 0.044715 * x * x * x))
    return 0.5 * (1.0 + t) + 0.5 * x * (1.0 - t * t) * C_GELU * (1.0 + 3.0 * 0.044715 * x * x)


def _sigmoid(x):
    return 1.0 / (1.0 + jnp.exp(-x))


def _silu(x):
    return x * _sigmoid(x)


def _silu_grad(x):
    s = _sigmoid(x)
    return s * (1.0 + x * (1.0 - s))


def _softplus(x):
    z = jnp.exp(-jnp.abs(x))
    small = z * (1.0 - z * (0.5 - z * (1.0 / 3.0)))
    return jnp.maximum(x, 0.0) + jnp.where(z < 1e-2, small, jnp.log(1.0 + z))


def _mx(x):
    return x.astype(MXU_DTYPE)


def _dot(a, b, dims="nn", precision=None):
    cd = {"nn": ((1,), (0,)), "nt": ((1,), (1,)), "tn": ((0,), (0,))}[dims]
    return lax.dot_general(a, b, (cd, ((), ())), preferred_element_type=F32, precision=precision)


def _mdot(a, b, dims="nn"):
    return _dot(_mx(a), _mx(b), dims)


def _matmul(a, b, *, mode, tm, tn, tk, out_dtype, name, a_fn=None, extras=(), epi=None, a_cols=None):
    a_shape = a.shape if a_cols is None else (a.shape[0], a_cols)
    if mode == "nn":
        (m, k), n = a_shape, b.shape[1]
    elif mode == "nt":
        (m, k), n = a_shape, b.shape[0]
    else:
        (k, m), n = a_shape, b.shape[1]
    tm, tn, tk = min(tm, m), min(tn, n), min(tk, k)
    assert m % tm == 0 and n % tn == 0 and k % tk == 0, (name, a.shape, b.shape, tm, tn, tk)
    nk = k // tk
    n_ex = len(extras)

    def body(a_ref, b_ref, *rest):
        ex_refs, o_ref, acc = rest[:n_ex], rest[n_ex], rest[n_ex + 1]
        kk = pl.program_id(2)

        @pl.when(kk == 0)
        def _():
            acc[...] = jnp.zeros_like(acc)

        av = a_ref[...]
        if a_fn is not None:
            av = a_fn(av)
        acc[...] += _dot(_mx(av), _mx(b_ref[...]), mode)

        @pl.when(kk == nk - 1)
        def _():
            r = acc[...]
            if epi is not None:
                r = epi(r, *[e[...] for e in ex_refs])
            o_ref[...] = r.astype(out_dtype)

    a_spec = pl.BlockSpec((tk, tm), lambda i, j, kk: (kk, i)) if mode == "tn" else pl.BlockSpec((tm, tk), lambda i, j, kk: (i, kk))
    b_spec = pl.BlockSpec((tn, tk), lambda i, j, kk: (j, kk)) if mode == "nt" else pl.BlockSpec((tk, tn), lambda i, j, kk: (kk, j))
    ex_specs = [pl.BlockSpec((tm if bs[0] is None else bs[0], tn if bs[1] is None else bs[1]),
                             functools.partial(lambda i, j, kk, f: f(i, j), f=im)) for (_, bs, im) in extras]
    return pl.pallas_call(
        body,
        name=name,
        grid=(m // tm, n // tn, nk),
        in_specs=[a_spec, b_spec, *ex_specs],
        out_specs=pl.BlockSpec((tm, tn), lambda i, j, kk: (i, j)),
        out_shape=jax.ShapeDtypeStruct((m, n), out_dtype),
        scratch_shapes=[pltpu.VMEM((tm, tn), F32)],
        compiler_params=_params(("parallel", "parallel", "arbitrary")),
    )(a, b, *[e[0] for e in extras])


def _ln_fwd(x, y, g, b, name):
    l, d = x.shape
    tl = 256

    def body(x_ref, y_ref, g_ref, b_ref, h_ref, o_ref):
        h = ALPHA * x_ref[...] + y_ref[...]
        mu = jnp.mean(h, axis=-1, keepdims=True)
        c = h - mu
        var = jnp.mean(c * c, axis=-1, keepdims=True)
        h_ref[...] = h
        o_ref[...] = c * lax.rsqrt(var + LN_EPS) * g_ref[...] + b_ref[...]

    row = pl.BlockSpec((tl, d), lambda i: (i, 0))
    vec = pl.BlockSpec((1, d), lambda i: (0, 0))
    return pl.pallas_call(
        body, name=name, grid=(l // tl,), in_specs=[row, row, vec, vec], out_specs=[row, row],
        out_shape=[jax.ShapeDtypeStruct((l, d), F32)] * 2, compiler_params=_params(("parallel",)),
    )(x, y, g, b)


def _ln_bwd(dout, h, g, name):
    l, d = h.shape
    tl = 256

    def body(do_ref, h_ref, g_ref, dh_ref, dg_ref, db_ref):
        @pl.when(pl.program_id(0) == 0)
        def _():
            dg_ref[...] = jnp.zeros_like(dg_ref)
            db_ref[...] = jnp.zeros_like(db_ref)

        hv, do = h_ref[...], do_ref[...]
        mu = jnp.mean(hv, axis=-1, keepdims=True)
        c = hv - mu
        r = lax.rsqrt(jnp.mean(c * c, axis=-1, keepdims=True) + LN_EPS)
        xh = c * r
        dxh = do * g_ref[...]
        m1 = jnp.mean(dxh, axis=-1, keepdims=True)
        m2 = jnp.mean(dxh * xh, axis=-1, keepdims=True)
        dh_ref[...] = r * (dxh - m1 - xh * m2)
        dg_ref[...] += jnp.sum(do * xh, axis=0, keepdims=True)
        db_ref[...] += jnp.sum(do, axis=0, keepdims=True)

    row = pl.BlockSpec((tl, d), lambda i: (i, 0))
    vec = pl.BlockSpec((1, d), lambda i: (0, 0))
    return pl.pallas_call(
        body, name=name, grid=(l // tl,), in_specs=[row, row, vec], out_specs=[row, vec, vec],
        out_shape=[jax.ShapeDtypeStruct((l, d), F32), jax.ShapeDtypeStruct((1, d), F32), jax.ShapeDtypeStruct((1, d), F32)],
        compiler_params=_params(("arbitrary",)),
    )(dout, h, g)


def _loss_head(y, target):
    l, d = y.shape
    tl = 256

    def body(y_ref, t_ref, loss_ref, dy_ref):
        @pl.when(pl.program_id(0) == 0)
        def _():
            loss_ref[...] = jnp.zeros_like(loss_ref)

        e = y_ref[...] - t_ref[...]
        dy_ref[...] = e * (1.0 / d)
        s = jnp.sum(jnp.sum(e * e, axis=1, keepdims=True), axis=0, keepdims=True)
        loss_ref[...] += s * (0.5 / d)

    row = pl.BlockSpec((tl, d), lambda i: (i, 0))
    return pl.pallas_call(
        body, name="loss_head", grid=(l // tl,), in_specs=[row, row],
        out_specs=[pl.BlockSpec((1, 1), lambda i: (0, 0)), row],
        out_shape=[jax.ShapeDtypeStruct((1, 1), F32), jax.ShapeDtypeStruct((l, d), F32)],
        compiler_params=_params(("arbitrary",)),
    )(y, target)


def _s5_discretize(lam_re, lam_im, log_step, b_re, b_im):
    step = jnp.exp(log_step)[:, None]
    e = jnp.exp(lam_re * step)
    lbr, lbi = e * jnp.cos(lam_im * step), e * jnp.sin(lam_im * step)
    den = lam_re * lam_re + lam_im * lam_im
    qr = ((lbr - 1.0) * lam_re + lbi * lam_im) / den
    qi = (lbi * lam_re - (lbr - 1.0) * lam_im) / den
    bbr = qr[:, :, None] * b_re - qi[:, :, None] * b_im
    bbi = qr[:, :, None] * b_im + qi[:, :, None] * b_re
    return lbr, lbi, bbr, bbi


def _s5_dense(bbr, bbi, c_re, c_im):
    eye = jnp.eye(S5_GROUPS, dtype=F32)
    def bd(t):
        return jnp.einsum("gph,gk->ghkp", t, eye).reshape(GROUP_WIDTH, S5_NS)
    def cd(t):
        return jnp.einsum("ghp,gk->gpkh", t, eye).reshape(S5_NS, GROUP_WIDTH)
    return jnp.concatenate([bd(bbr), bd(bbi)], axis=1), jnp.concatenate([cd(c_re), -cd(c_im)], axis=0)


def _s5_undense_b(dbmat):
    eye = jnp.eye(S5_GROUPS, dtype=F32)[:, None, :, None]
    def ex(t):
        return jnp.sum(t.reshape(S5_GROUPS, S5_CH, S5_GROUPS, S5_STATE) * eye, axis=2).transpose(0, 2, 1)
    return ex(dbmat[:, :S5_NS]), ex(dbmat[:, S5_NS:])


def _s5_undense_c(dcmat):
    eye = jnp.eye(S5_GROUPS, dtype=F32)[:, None, :, None]
    def ex(t):
        return jnp.sum(t.reshape(S5_GROUPS, S5_STATE, S5_GROUPS, S5_CH) * eye, axis=2).transpose(0, 2, 1)
    return ex(dcmat[:S5_NS]), -ex(dcmat[S5_NS:])


def _s5_scan(bu, lam, *, reverse, h=None, name):
    l, w = bu.shape
    ns = w // 2
    tl = 256
    nb = l // tl
    with_h = h is not None

    def body(*refs):
        if with_h:
            bu_ref, lam_ref, h_ref, o_ref, dl_ref, carry = refs
        else:
            bu_ref, lam_ref, o_ref, carry = refs

        @pl.when(pl.program_id(0) == 0)
        def _():
            carry[...] = jnp.zeros_like(carry)
            if with_h:
                dl_ref[...] = jnp.zeros_like(dl_ref)

        lr, li = lam_ref[:, :ns], lam_ref[:, ns:]

        def step(t, c):
            row = (tl - 1 - t) if reverse else t
            cr, ci = c[0], c[1]
            out = c[2:]
            if with_h:
                hr, hi = h_ref[pl.ds(row, 1), :ns], h_ref[pl.ds(row, 1), ns:]
                out = (out[0] + cr * hr + ci * hi, out[1] + ci * hr - cr * hi)
            nr = lr * cr - li * ci + bu_ref[pl.ds(row, 1), :ns]
            ni = lr * ci + li * cr + bu_ref[pl.ds(row, 1), ns:]
            o_ref[pl.ds(row, 1), :ns] = nr
            o_ref[pl.ds(row, 1), ns:] = ni
            return (nr, ni) + tuple(out)

        init = (carry[:, :ns], carry[:, ns:])
        if with_h:
            init = init + (dl_ref[:, :ns], dl_ref[:, ns:])
        fin = lax.fori_loop(0, tl, step, init)
        carry[:, :ns] = fin[0]
        carry[:, ns:] = fin[1]
        if with_h:
            dl_ref[:, :ns] = fin[2]
            dl_ref[:, ns:] = fin[3]

    idx = (lambda i: (nb - 1 - i, 0)) if reverse else (lambda i: (i, 0))
    row = pl.BlockSpec((tl, w), idx)
    vec = pl.BlockSpec((1, w), lambda i: (0, 0))
    in_specs = [row, vec] + ([row] if with_h else [])
    out_specs = [row] + ([vec] if with_h else [])
    out_shape = [jax.ShapeDtypeStruct((l, w), F32)] + ([jax.ShapeDtypeStruct((1, w), F32)] if with_h else [])
    res = pl.pallas_call(
        body, name=name, grid=(nb,), in_specs=in_specs, out_specs=out_specs, out_shape=out_shape,
        scratch_shapes=[pltpu.VMEM((1, w), F32)], compiler_params=_params(("arbitrary",)),
    )(*([bu, lam] + ([h] if with_h else [])))
    return res if with_h else res[0]


def _s5_glu_fwd(y, glu_w, glu_b, name):
    l, d = y.shape
    tl = min(512, l)

    def body(y_ref, w_ref, b_ref, o_ref):
        yg = _gelu(y_ref[...])
        z = _mdot(yg, w_ref[...]) + b_ref[...]
        o_ref[...] = (yg * _sigmoid(z)).astype(o_ref.dtype)

    return pl.pallas_call(
        body, name=name, grid=(l // tl,),
        in_specs=[pl.BlockSpec((tl, d), lambda i: (i, 0)), pl.BlockSpec((d, d), lambda i: (0, 0)), pl.BlockSpec((1, d), lambda i: (0, 0))],
        out_specs=pl.BlockSpec((tl, d), lambda i: (i, 0)), out_shape=jax.ShapeDtypeStruct((l, d), BF16),
        compiler_params=_params(("parallel",)),
    )(y, glu_w, glu_b)


def _s5_glu_bwd(dmixed, y, proj, glu_w, glu_b, name):
    l, d = y.shape
    tl = min(512, l)

    def body(do_ref, y_ref, u_ref, w_ref, b_ref, dy_ref, dz_ref, yg_ref, db_ref, dd_ref):
        @pl.when(pl.program_id(0) == 0)
        def _():
            db_ref[...] = jnp.zeros_like(db_ref)
            dd_ref[...] = jnp.zeros_like(dd_ref)

        yv, do = y_ref[...], do_ref[...]
        yg = _gelu(yv)
        gate = _sigmoid(_mdot(yg, w_ref[...]) + b_ref[...])
        dz = do * yg * gate * (1.0 - gate)
        dyg = do * gate + _mdot(dz, w_ref[...], "nt")
        dy = dyg * _gelu_grad(yv)
        dy_ref[...] = dy
        dz_ref[...] = dz.astype(dz_ref.dtype)
        yg_ref[...] = yg.astype(yg_ref.dtype)
        db_ref[...] += jnp.sum(dz, axis=0, keepdims=True)
        dd_ref[...] += jnp.sum(dy * u_ref[...], axis=0, keepdims=True)

    row = pl.BlockSpec((tl, d), lambda i: (i, 0))
    vec = pl.BlockSpec((1, d), lambda i: (0, 0))
    return pl.pallas_call(
        body, name=name, grid=(l // tl,),
        in_specs=[row, row, row, pl.BlockSpec((d, d), lambda i: (0, 0)), vec],
        out_specs=[row, row, row, vec, vec],
        out_shape=[jax.ShapeDtypeStruct((l, d), F32), jax.ShapeDtypeStruct((l, d), BF16), jax.ShapeDtypeStruct((l, d), BF16),
                   jax.ShapeDtypeStruct((1, d), F32), jax.ShapeDtypeStruct((1, d), F32)],
        compiler_params=_params(("arbitrary",)),
    )(dmixed, y, proj, glu_w, glu_b)


def _sgu_pair(w_ref, x, j, dims):
    lo = lax.broadcasted_iota(jnp.int32, x.shape, 1) < (GROUP_WIDTH // SGU_HEADS)
    xb = _mx(x)
    r0 = _dot(w_ref[2 * j], xb, dims)
    r1 = _dot(w_ref[2 * j + 1], xb, dims)
    return jnp.where(lo, r0, r1)


def _sgu_norm(v, g, b):
    mu = jnp.mean(v, axis=-1, keepdims=True)
    c = v - mu
    r = lax.rsqrt(jnp.mean(c * c, axis=-1, keepdims=True) + LN_EPS)
    return c * r, r


def _sgu_fwd(proj, norm_g, norm_b, wm, bfull, name):
    l = proj.shape[0]
    tl = 256
    gw = GROUP_WIDTH

    def body(zu_ref, zv_ref, g_ref, b_ref, w_ref, bf_ref, o_ref):
        for c in range(tl // SGU_CHUNK):
            rows = slice(c * SGU_CHUNK, (c + 1) * SGU_CHUNK)
            u = _gelu(zu_ref[rows, :])
            vh, _ = _sgu_norm(_gelu(zv_ref[rows, :]), None, None)
            vn = vh * g_ref[...] + b_ref[...]
            for j in range(gw // 128):
                cols = slice(j * 128, (j + 1) * 128)
                mixed = _sgu_pair(w_ref, vn[:, cols], j, "nn") + bf_ref[:, cols]
                o_ref[rows, cols] = (u[:, cols] * mixed).astype(o_ref.dtype)

    vec = pl.BlockSpec((1, gw), lambda i: (0, 0))
    return pl.pallas_call(
        body, name=name, grid=(l // tl,),
        in_specs=[pl.BlockSpec((tl, gw), lambda i: (i, 1)), pl.BlockSpec((tl, gw), lambda i: (i, 2)), vec, vec,
                  pl.BlockSpec((SGU_HEADS, SGU_CHUNK, SGU_CHUNK), lambda i: (0, 0, 0)), pl.BlockSpec((SGU_CHUNK, gw), lambda i: (0, 0))],
        out_specs=pl.BlockSpec((tl, gw), lambda i: (i, 0)), out_shape=jax.ShapeDtypeStruct((l, gw), BF16),
        compiler_params=_params(("parallel",)),
    )(proj, proj, norm_g, norm_b, wm, bfull)


def _sgu_bwd(dmixed, proj, norm_g, norm_b, wm, bfull, name):
    l = proj.shape[0]
    tl = 256
    gw = GROUP_WIDTH
    hd = gw // SGU_HEADS

    def body(do_ref, zu_ref, zv_ref, g_ref, b_ref, w_ref, bf_ref, dzu_ref, dzv_ref, dw_ref, dbf_ref, dg_ref, dnb_ref):
        @pl.when(pl.program_id(0) == 0)
        def _():
            dw_ref[...] = jnp.zeros_like(dw_ref)
            dbf_ref[...] = jnp.zeros_like(dbf_ref)
            dg_ref[...] = jnp.zeros_like(dg_ref)
            dnb_ref[...] = jnp.zeros_like(dnb_ref)

        for c in range(tl // SGU_CHUNK):
            rows = slice(c * SGU_CHUNK, (c + 1) * SGU_CHUNK)
            zu, zv, do = zu_ref[rows, :], zv_ref[rows, :], do_ref[rows, :]
            u = _gelu(zu)
            vh, r = _sgu_norm(_gelu(zv), None, None)
            vn = vh * g_ref[...] + b_ref[...]
            dvn_parts, mixed_parts = [], []
            for j in range(gw // 128):
                cols = slice(j * 128, (j + 1) * 128)
                vb = vn[:, cols]
                mixed_parts.append(_sgu_pair(w_ref, vb, j, "nn") + bf_ref[:, cols])
                dm = do[:, cols] * u[:, cols]
                dvn_parts.append(_sgu_pair(w_ref, dm, j, "tn"))
                lo = lax.broadcasted_iota(jnp.int32, dm.shape, 1) < hd
                dw_ref[2 * j] += _mdot(jnp.where(lo, dm, 0.0), vb, "nt")
                dw_ref[2 * j + 1] += _mdot(jnp.where(lo, 0.0, dm), vb, "nt")
                dbf_ref[:, cols] += dm
            mixed = jnp.concatenate(mixed_parts, axis=1)
            dvn = jnp.concatenate(dvn_parts, axis=1)
            dzu_ref[rows, :] = (do * mixed * _gelu_grad(zu)).astype(dzu_ref.dtype)
            dg_ref[...] += jnp.sum(dvn * vh, axis=0, keepdims=True)
            dnb_ref[...] += jnp.sum(dvn, axis=0, keepdims=True)
            dvh = dvn * g_ref[...]
            m1 = jnp.mean(dvh, axis=-1, keepdims=True)
            m2 = jnp.mean(dvh * vh, axis=-1, keepdims=True)
            dv = r * (dvh - m1 - vh * m2)
            dzv_ref[rows, :] = (dv * _gelu_grad(zv)).astype(dzv_ref.dtype)

    vec = pl.BlockSpec((1, gw), lambda i: (0, 0))
    row = pl.BlockSpec((tl, gw), lambda i: (i, 0))
    wspec = pl.BlockSpec((SGU_HEADS, SGU_CHUNK, SGU_CHUNK), lambda i: (0, 0, 0))
    bspec = pl.BlockSpec((SGU_CHUNK, gw), lambda i: (0, 0))
    return pl.pallas_call(
        body, name=name, grid=(l // tl,),
        in_specs=[pl.BlockSpec((tl, gw), lambda i: (i, 1)), pl.BlockSpec((tl, gw), lambda i: (i, 1)), pl.BlockSpec((tl, gw), lambda i: (i, 2)),
                  vec, vec, wspec, bspec],
        out_specs=[row, row, wspec, bspec, vec, vec],
        out_shape=[jax.ShapeDtypeStruct((l, gw), BF16), jax.ShapeDtypeStruct((l, gw), BF16),
                   jax.ShapeDtypeStruct((SGU_HEADS, SGU_CHUNK, SGU_CHUNK), F32), jax.ShapeDtypeStruct((SGU_CHUNK, gw), F32),
                   jax.ShapeDtypeStruct((1, gw), F32), jax.ShapeDtypeStruct((1, gw), F32)],
        compiler_params=_params(("arbitrary",)),
    )(dmixed, proj, proj, norm_g, norm_b, wm, bfull)


HALO = 16


def _window_sums(ext, n_rows, forward):
    def sh(x, k):
        return pltpu.roll(x, (n_rows - k) if forward else k, axis=0)
    s2 = ext + sh(ext, 1)
    s4 = s2 + sh(s2, 2)
    s8 = s4 + sh(s4, 4)
    s16 = s8 + sh(s8, 8)
    return (s2, s4, s8, s16)


def _pool_fwd(proj, pool_w, scale, name):
    l = proj.shape[0]
    tl = 256
    gw = GROUP_WIDTH
    pg = gw // len(POOL_WINDOWS)

    def body(x_ref, halo_ref, w_ref, s_ref, o_ref, p_ref):
        i = pl.program_id(0)
        x = x_ref[...]
        halo = jnp.where(i > 0, halo_ref[...], 0.0)
        ext = jnp.concatenate([halo, x], axis=0)
        sums = _window_sums(ext, tl + HALO, False)
        t = i * tl + lax.broadcasted_iota(jnp.int32, (tl, pg), 0)
        for gi, win in enumerate(POOL_WINDOWS):
            cols = slice(gi * pg, (gi + 1) * pg)
            cnt = jnp.minimum(t + 1, win).astype(F32)
            pooled = sums[gi][HALO:, cols] / cnt - x[:, cols]
            p_ref[:, cols] = pooled
            o_ref[:, cols] = (_mdot(pooled, w_ref[gi]) * s_ref[:, cols]).astype(o_ref.dtype)

    row = pl.BlockSpec((tl, gw), lambda i: (i, 0))
    return pl.pallas_call(
        body, name=name, grid=(l // tl,),
        in_specs=[pl.BlockSpec((tl, gw), lambda i: (i, 3)),
                  pl.BlockSpec((HALO, gw), lambda i: (jnp.maximum(i * (tl // HALO) - 1, 0), 3)),
                  pl.BlockSpec((len(POOL_WINDOWS), pg, pg), lambda i: (0, 0, 0)), pl.BlockSpec((1, gw), lambda i: (0, 0))],
        out_specs=[row, row], out_shape=[jax.ShapeDtypeStruct((l, gw), BF16), jax.ShapeDtypeStruct((l, gw), F32)],
        compiler_params=_params(("parallel",)),
    )(proj, proj, pool_w, scale)


def _pool_bwd_map(dmixed, pooled, pool_w, scale, name):
    l, gw = pooled.shape
    tl = 256
    ng = len(POOL_WINDOWS)
    pg = gw // ng

    def body(do_ref, p_ref, w_ref, s_ref, dp_ref, dw_ref, ds_ref):
        @pl.when(pl.program_id(0) == 0)
        def _():
            dw_ref[...] = jnp.zeros_like(dw_ref)
            ds_ref[...] = jnp.zeros_like(ds_ref)

        for gi in range(ng):
            cols = slice(gi * pg, (gi + 1) * pg)
            do, pooled_g = do_ref[:, cols], p_ref[:, cols]
            mixed = _mdot(pooled_g, w_ref[gi])
            ds_ref[:, cols] += jnp.sum(do * mixed, axis=0, keepdims=True)
            dm = do * s_ref[:, cols]
            dw_ref[gi] += _mdot(pooled_g, dm, "tn")
            dp_ref[:, cols] = _mdot(dm, w_ref[gi], "nt")

    row = pl.BlockSpec((tl, gw), lambda i: (i, 0))
    wspec = pl.BlockSpec((ng, pg, pg), lambda i: (0, 0, 0))
    vec = pl.BlockSpec((1, gw), lambda i: (0, 0))
    return pl.pallas_call(
        body, name=name, grid=(l // tl,),
        in_specs=[pl.BlockSpec((tl, gw), lambda i: (i, 2)), row, wspec, vec], out_specs=[row, wspec, vec],
        out_shape=[jax.ShapeDtypeStruct((l, gw), F32), jax.ShapeDtypeStruct((ng, pg, pg), F32), jax.ShapeDtypeStruct((1, gw), F32)],
        compiler_params=_params(("arbitrary",)),
    )(dmixed, pooled, pool_w, scale)


def _pool_bwd_window(dpooled, name):
    l, gw = dpooled.shape
    tl = 256
    nb = l // tl
    pg = gw // len(POOL_WINDOWS)

    def body(d_ref, halo_ref, o_ref):
        i = pl.program_id(0)
        d = d_ref[...]
        halo = jnp.where(i < nb - 1, halo_ref[...], 0.0)
        ext = jnp.concatenate([d, halo], axis=0)
        t = i * tl + lax.broadcasted_iota(jnp.int32, (tl + HALO, pg), 0)
        for gi, win in enumerate(POOL_WINDOWS):
            cols = slice(gi * pg, (gi + 1) * pg)
            cnt = jnp.minimum(t + 1, win).astype(F32)
            sums = _window_sums(ext[:, cols] / cnt, tl + HALO, True)
            o_ref[:, cols] = (sums[gi][:tl, :] - d[:, cols]).astype(o_ref.dtype)

    row = pl.BlockSpec((tl, gw), lambda i: (i, 0))
    return pl.pallas_call(
        body, name=name, grid=(nb,),
        in_specs=[row, pl.BlockSpec((HALO, gw), lambda i: (jnp.minimum((i + 1) * (tl // HALO), l // HALO - 1), 0))],
        out_specs=row, out_shape=jax.ShapeDtypeStruct((l, gw), BF16), compiler_params=_params(("parallel",)),
    )(dpooled, dpooled)


CONV_HALO = 8
QKV_BLK = 4


def _head_sums(x):
    parts = []
    for hd in range(DN_HEADS):
        s = jnp.sum(x[:, hd * DN_HEAD_DIM:(hd + 1) * DN_HEAD_DIM], axis=-1, keepdims=True)
        parts.append(jnp.broadcast_to(s, (x.shape[0], DN_HEAD_DIM)))
    return jnp.concatenate(parts, axis=1)


def _gdn_pre_fwd(proj, proj_ab, conv_w, a_log, dt_bias, name):
    l = proj.shape[0]
    tl = 256
    gw = GROUP_WIDTH

    def body(xq, xk, xv, hq, hk, hv, w_ref, ab_ref, al_ref, dt_ref, qn_ref, kn_ref, v_ref, cq_ref, ck_ref, cv_ref, gb_ref):
        i = pl.program_id(0)
        for p, (x_ref, h_ref, c_ref) in enumerate(((xq, hq, cq_ref), (xk, hk, ck_ref), (xv, hv, cv_ref))):
            ext = jnp.concatenate([jnp.where(i > 0, h_ref[...], 0.0), x_ref[...]], axis=0)
            conv = jnp.zeros((tl, gw), F32)
            for j in range(DN_CONV):
                k = DN_CONV - 1 - j
                shifted = ext if k == 0 else pltpu.roll(ext, k, axis=0)
                conv = conv + shifted[CONV_HALO:, :] * w_ref[j:j + 1, p * gw:(p + 1) * gw]
            c_ref[...] = conv
            s = _silu(conv)
            if p == 2:
                v_ref[...] = s
            else:
                r = lax.rsqrt(_head_sums(s * s) + L2_EPS)
                (qn_ref if p == 0 else kn_ref)[...] = s * r * (DN_HEAD_DIM ** -0.5 if p == 0 else 1.0)
        ab = ab_ref[...]
        lane = lax.broadcasted_iota(jnp.int32, ab.shape, 1)
        g = -jnp.exp(al_ref[...]) * _softplus(ab + dt_ref[...])
        gb_ref[...] = jnp.where(lane < DN_HEADS, g, _sigmoid(ab))

    def xs(b):
        return pl.BlockSpec((tl, gw), lambda i: (i, b))

    def hs(b):
        return pl.BlockSpec((CONV_HALO, gw), lambda i: (jnp.maximum(i * (tl // CONV_HALO) - 1, 0), b))

    row = pl.BlockSpec((tl, gw), lambda i: (i, 0))
    abrow = pl.BlockSpec((tl, AB_PAD), lambda i: (i, 0))
    abvec = pl.BlockSpec((1, AB_PAD), lambda i: (0, 0))
    return pl.pallas_call(
        body, name=name, grid=(l // tl,),
        in_specs=[xs(QKV_BLK), xs(QKV_BLK + 1), xs(QKV_BLK + 2), hs(QKV_BLK), hs(QKV_BLK + 1), hs(QKV_BLK + 2),
                  pl.BlockSpec((DN_CONV, 3 * gw), lambda i: (0, 0)), abrow, abvec, abvec],
        out_specs=[row] * 6 + [abrow],
        out_shape=[jax.ShapeDtypeStruct((l, gw), F32)] * 6 + [jax.ShapeDtypeStruct((l, AB_PAD), F32)],
        compiler_params=_params(("parallel",)),
    )(proj, proj, proj, proj, proj, proj, conv_w, proj_ab, a_log, dt_bias)


def _gdn_pre_bwd(dq, dk, dv, cq, ck, cv, dgb, gb, proj_ab, a_log, dt_bias, name):
    l, gw = cq.shape
    tl = 256

    def body(dq_ref, dk_ref, dv_ref, cq_ref, ck_ref, cv_ref, dgb_ref, gb_ref, ab_ref, al_ref, dt_ref,
             dcq_ref, dck_ref, dcv_ref, dab_ref, dal_ref, ddt_ref):
        @pl.when(pl.program_id(0) == 0)
        def _():
            dal_ref[...] = jnp.zeros_like(dal_ref)
            ddt_ref[...] = jnp.zeros_like(ddt_ref)

        for p, (d_ref, c_ref, o_ref) in enumerate(((dq_ref, cq_ref, dcq_ref), (dk_ref, ck_ref, dck_ref), (dv_ref, cv_ref, dcv_ref))):
            c, d = c_ref[...], d_ref[...]
            if p == 2:
                ds = d
            else:
                s = _silu(c)
                r = lax.rsqrt(_head_sums(s * s) + L2_EPS)
                ds = (DN_HEAD_DIM ** -0.5 if p == 0 else 1.0) * r * (d - s * r * r * _head_sums(d * s))
            o_ref[...] = ds * _silu_grad(c)
        ab, dgb_v, gb_v = ab_ref[...], dgb_ref[...], gb_ref[...]
        lane = lax.broadcasted_iota(jnp.int32, ab.shape, 1)
        is_g = lane < DN_HEADS
        dpre = dgb_v * (-jnp.exp(al_ref[...])) * _sigmoid(ab + dt_ref[...])
        dab_ref[...] = jnp.where(is_g, dpre, dgb_v * gb_v * (1.0 - gb_v)).astype(dab_ref.dtype)
        dal_ref[...] += jnp.sum(jnp.where(is_g, dgb_v * gb_v, 0.0), axis=0, keepdims=True)
        ddt_ref[...] += jnp.sum(jnp.where(is_g, dpre, 0.0), axis=0, keepdims=True)

    row = pl.BlockSpec((tl, gw), lambda i: (i, 0))
    abrow = pl.BlockSpec((tl, AB_PAD), lambda i: (i, 0))
    abvec = pl.BlockSpec((1, AB_PAD), lambda i: (0, 0))
    return pl.pallas_call(
        body, name=name, grid=(l // tl,),
        in_specs=[row] * 6 + [abrow, abrow, abrow, abvec, abvec],
        out_specs=[row, row, row, abrow, abvec, abvec],
        out_shape=[jax.ShapeDtypeStruct((l, gw), F32)] * 3 + [jax.ShapeDtypeStruct((l, AB_PAD), BF16),
                   jax.ShapeDtypeStruct((1, AB_PAD), F32), jax.ShapeDtypeStruct((1, AB_PAD), F32)],
        compiler_params=_params(("arbitrary",)),
    )(dq, dk, dv, cq, ck, cv, dgb, gb, proj_ab, a_log, dt_bias)


def _conv_bwd(dc, proj, col_blk, w_part, name):
    l, gw = dc.shape
    tl = 256
    nb = l // tl

    def body(dc_ref, halo_ref, x_ref, w_ref, dx_ref, dw_ref):
        i = pl.program_id(0)

        @pl.when(i == 0)
        def _():
            dw_ref[...] = jnp.zeros_like(dw_ref)

        ext = jnp.concatenate([dc_ref[...], jnp.where(i < nb - 1, halo_ref[...], 0.0)], axis=0)
        x = x_ref[...]
        dx = jnp.zeros((tl, gw), F32)
        rid = lax.broadcasted_iota(jnp.int32, (8, gw), 0)
        dw = jnp.zeros((8, gw), F32)
        for j in range(DN_CONV):
            k = DN_CONV - 1 - j
            shifted = (ext if k == 0 else pltpu.roll(ext, tl + CONV_HALO - k, axis=0))[:tl, :]
            dx = dx + shifted * w_ref[j:j + 1, :]
            dw = dw + jnp.where(rid == j, jnp.sum(x * shifted, axis=0, keepdims=True), 0.0)
        dx_ref[...] = dx.astype(dx_ref.dtype)
        dw_ref[...] += dw

    row = pl.BlockSpec((tl, gw), lambda i: (i, 0))
    return pl.pallas_call(
        body, name=name, grid=(nb,),
        in_specs=[row, pl.BlockSpec((CONV_HALO, gw), lambda i: (jnp.minimum((i + 1) * (tl // CONV_HALO), l // CONV_HALO - 1), 0)),
                  pl.BlockSpec((tl, gw), lambda i: (i, col_blk)), pl.BlockSpec((DN_CONV, gw), lambda i: (0, 0))],
        out_specs=[row, pl.BlockSpec((8, gw), lambda i: (0, 0))],
        out_shape=[jax.ShapeDtypeStruct((l, gw), BF16), jax.ShapeDtypeStruct((8, gw), F32)],
        compiler_params=_params(("arbitrary",)),
    )(dc, dc, proj, w_part)


def _chunk_terms(q, k, v, grow, gcol, beta):
    c = DN_CHUNK
    ii = lax.broadcasted_iota(jnp.int32, (c, c), 0)
    jj = lax.broadcasted_iota(jnp.int32, (c, c), 1)
    tril, strict = ii >= jj, ii > jj
    gc_col = jnp.sum(jnp.where(tril, grow, 0.0), axis=1, keepdims=True)
    gc_row = jnp.sum(jnp.where(ii <= jj, gcol, 0.0), axis=0, keepdims=True)
    dec = jnp.exp(jnp.where(tril, gc_col - gc_row, -1e30))
    kb, vb = k * beta, v * beta
    kk = _mdot(kb, k, "nt")
    a = jnp.where(strict, kk * dec, 0.0)
    eye = jnp.where(ii == jj, 1.0, 0.0)
    t = eye - a
    p = _dot(a, a, precision=HI)
    for it in range(5):
        t = t + _dot(t, p, precision=HI)
        if it < 4:
            p = _dot(p, p, precision=HI)
    eg = jnp.exp(gc_col)
    gc_last = gc_col[c - 1:c, :]
    kbg = kb * eg
    u = _mdot(t, vb)
    w = _mdot(t, kbg)
    qk0 = _mdot(q, k, "nt")
    qk = jnp.where(tril, qk0 * dec, 0.0)
    qg = q * eg
    e2 = jnp.exp(gc_last - gc_col)
    kt = k * e2
    gl = jnp.exp(gc_last)
    return dict(ii=ii, jj=jj, tril=tril, strict=strict, dec=dec, kb=kb, vb=vb, kk=kk, t=t, eg=eg, kbg=kbg, u=u, w=w,
                qk0=qk0, qk=qk, qg=qg, e2=e2, kt=kt, gl=gl)


def _gdn_specs(n_chunks, reverse):
    c, hd = DN_CHUNK, DN_HEAD_DIM
    ch = (lambda n: n_chunks - 1 - n) if reverse else (lambda n: n)
    blk = pl.BlockSpec((c, hd), lambda h, n: (ch(n), h))
    col = pl.BlockSpec((1, c, 1), lambda h, n: (h * n_chunks + ch(n), 0, 0))
    rowv = pl.BlockSpec((1, 1, c), lambda h, n: (h * n_chunks + ch(n), 0, 0))
    st = pl.BlockSpec((1, hd, hd), lambda h, n: (h * n_chunks + ch(n), 0, 0))
    return blk, col, rowv, st


def _gdn_core_fwd(qn, kn, v, g_row, g_col, b_col, name):
    l = qn.shape[0]
    n_chunks = l // DN_CHUNK
    hd = DN_HEAD_DIM
    blk, col, rowv, st = _gdn_specs(n_chunks, False)

    def body(q_ref, k_ref, v_ref, gr_ref, gc_ref, bc_ref, o_ref, s_ref, state):
        @pl.when(pl.program_id(1) == 0)
        def _():
            state[...] = jnp.zeros_like(state)

        x = _chunk_terms(q_ref[...], k_ref[...], v_ref[...], gr_ref[0], gc_ref[0], bc_ref[0])
        s = state[...]
        s_ref[0] = s
        v_new = x["u"] - _mdot(x["w"], s)
        o_ref[...] = _mdot(x["qg"], s) + _mdot(x["qk"], v_new)
        state[...] = s * x["gl"] + _mdot(x["kt"], v_new, "tn")

    return pl.pallas_call(
        body, name=name, grid=(DN_HEADS, n_chunks), in_specs=[blk, blk, blk, rowv, col, col], out_specs=[blk, st],
        out_shape=[jax.ShapeDtypeStruct((l, GROUP_WIDTH), F32), jax.ShapeDtypeStruct((DN_HEADS * n_chunks, hd, hd), F32)],
        scratch_shapes=[pltpu.VMEM((hd, hd), F32)], compiler_params=_params(("arbitrary", "arbitrary")),
    )(qn, kn, v, g_row, g_col, b_col)


def _gdn_core_bwd(do, qn, kn, v, g_row, g_col, b_col, states, name):
    l = qn.shape[0]
    n_chunks = l // DN_CHUNK
    hd, c = DN_HEAD_DIM, DN_CHUNK
    blk, col, rowv, st = _gdn_specs(n_chunks, True)

    def body(do_ref, q_ref, k_ref, v_ref, gr_ref, gc_ref, bc_ref, s_ref, dq_ref, dk_ref, dv_ref, dg_ref, db_ref, dstate):
        @pl.when(pl.program_id(1) == 0)
        def _():
            dstate[...] = jnp.zeros_like(dstate)

        q, k, vv, beta = q_ref[...], k_ref[...], v_ref[...], bc_ref[0]
        x = _chunk_terms(q, k, vv, gr_ref[0], gc_ref[0], beta)
        ii, jj, tril, strict = x["ii"], x["jj"], x["tril"], x["strict"]
        s, ds_new, dout = s_ref[0], dstate[...], do_ref[...]
        v_new = x["u"] - _mdot(x["w"], s)
        dqg = _mdot(dout, s, "nt")
        dqk = jnp.where(tril, _mdot(dout, v_new, "nt"), 0.0)
        dvn = _mdot(x["qk"], dout, "tn") + _mdot(x["kt"], ds_new)
        dkt = _mdot(v_new, ds_new, "nt")
        dgl = jnp.sum(jnp.sum(ds_new * s, axis=1, keepdims=True), axis=0, keepdims=True)
        dw = -_mdot(dvn, s, "nt")
        dstate[...] = _mdot(x["qg"], dout, "tn") + x["gl"] * ds_new - _mdot(x["w"], dvn, "tn")
        t = x["t"]
        dt = _mdot(dvn, x["vb"], "nt") + _mdot(dw, x["kbg"], "nt")
        dvb = _mdot(t, dvn, "tn")
        dkbg = _mdot(t, dw, "tn")
        tt_dt = _dot(t, dt, "tn", precision=HI)
        da = jnp.where(strict, -_dot(tt_dt, t, "nt", precision=HI), 0.0)
        dkk = da * x["dec"]
        dqk0 = dqk * x["dec"]
        e = (da * x["kk"] + dqk * x["qk0"]) * x["dec"]
        dkb = _mdot(dkk, k) + dkbg * x["eg"]
        dk = _mdot(dkk, x["kb"], "tn") + _mdot(dqk0, q, "tn") + dkt * x["e2"] + dkb * beta
        dq = _mdot(dqk0, k) + dqg * x["eg"]
        s_kt = jnp.sum(dkt * x["kt"], axis=1, keepdims=True)
        dgc_c = (jnp.sum(e, axis=1, keepdims=True) + jnp.sum(dqg * x["qg"], axis=1, keepdims=True) - s_kt
                 + jnp.sum(dkbg * x["kbg"], axis=1, keepdims=True))
        dgc_last = jnp.sum(s_kt, axis=0, keepdims=True) + dgl * x["gl"]
        rid = lax.broadcasted_iota(jnp.int32, (c, 1), 0)
        dgc_c = dgc_c + jnp.where(rid == c - 1, dgc_last, 0.0)
        dgc_r = jnp.sum(jnp.where(ii == jj, dgc_c, 0.0), axis=0, keepdims=True) - jnp.sum(e, axis=0, keepdims=True)
        dg_ref[0] = jnp.sum(jnp.where(jj >= ii, dgc_r, 0.0), axis=1, keepdims=True)
        db_ref[0] = jnp.sum(dkb * k, axis=1, keepdims=True) + jnp.sum(dvb * vv, axis=1, keepdims=True)
        dq_ref[...] = dq
        dk_ref[...] = dk
        dv_ref[...] = dvb * beta

    return pl.pallas_call(
        body, name=name, grid=(DN_HEADS, n_chunks), in_specs=[blk, blk, blk, blk, rowv, col, col, st],
        out_specs=[blk, blk, blk, col, col],
        out_shape=[jax.ShapeDtypeStruct((l, GROUP_WIDTH), F32)] * 3 + [jax.ShapeDtypeStruct((DN_HEADS * n_chunks, c, 1), F32)] * 2,
        scratch_shapes=[pltpu.VMEM((hd, hd), F32)], compiler_params=_params(("arbitrary", "arbitrary")),
    )(do, qn, kn, v, g_row, g_col, b_col, states)


def _gdn_post_fwd(o, proj, norm_g4, name):
    l, gw = o.shape
    tl = min(512, l)

    def body(o_ref, gate_ref, g_ref, out_ref):
        ov = o_ref[...]
        r = lax.rsqrt(_head_sums(ov * ov) * (1.0 / DN_HEAD_DIM) + RMS_EPS)
        out_ref[...] = (ov * r * g_ref[...] * _silu(gate_ref[...])).astype(out_ref.dtype)

    row = pl.BlockSpec((tl, gw), lambda i: (i, 0))
    return pl.pallas_call(
        body, name=name, grid=(l // tl,),
        in_specs=[row, pl.BlockSpec((tl, gw), lambda i: (i, 7)), pl.BlockSpec((1, gw), lambda i: (0, 0))],
        out_specs=row, out_shape=jax.ShapeDtypeStruct((l, gw), BF16), compiler_params=_params(("parallel",)),
    )(o, proj, norm_g4)


def _gdn_post_bwd(dmixed, o, proj, norm_g4, name):
    l, gw = o.shape
    tl = min(512, l)

    def body(d_ref, o_ref, gate_ref, g_ref, do_ref, dgate_ref, dng_ref):
        @pl.when(pl.program_id(0) == 0)
        def _():
            dng_ref[...] = jnp.zeros_like(dng_ref)

        ov, gate, d = o_ref[...], gate_ref[...], d_ref[...]
        r = lax.rsqrt(_head_sums(ov * ov) * (1.0 / DN_HEAD_DIM) + RMS_EPS)
        oh = ov * r
        sg = _silu(gate)
        dgate_ref[...] = (d * oh * g_ref[...] * _silu_grad(gate)).astype(dgate_ref.dtype)
        dng_ref[...] += jnp.sum(d * sg * oh, axis=0, keepdims=True)
        doh = d * g_ref[...] * sg
        do_ref[...] = r * (doh - oh * _head_sums(doh * oh) * (1.0 / DN_HEAD_DIM))

    row = pl.BlockSpec((tl, gw), lambda i: (i, 0))
    vec = pl.BlockSpec((1, gw), lambda i: (0, 0))
    return pl.pallas_call(
        body, name=name, grid=(l // tl,),
        in_specs=[pl.BlockSpec((tl, gw), lambda i: (i, 3)), row, pl.BlockSpec((tl, gw), lambda i: (i, 7)), vec],
        out_specs=[row, row, vec],
        out_shape=[jax.ShapeDtypeStruct((l, gw), F32), jax.ShapeDtypeStruct((l, gw), BF16), jax.ShapeDtypeStruct((1, gw), F32)],
        compiler_params=_params(("arbitrary",)),
    )(dmixed, o, proj, norm_g4)


def _gdn_vectors(gb, n_chunks):
    def col(t):
        return t.T.reshape(DN_HEADS * n_chunks, DN_CHUNK, 1)
    g, beta = gb[:, :DN_HEADS], gb[:, DN_HEADS:2 * DN_HEADS]
    return g.T.reshape(DN_HEADS * n_chunks, 1, DN_CHUNK), col(g), col(beta)


def _layer_fwd(x, w, li):
    l = x.shape[0]
    nm = f"l{li}_"
    proj = _matmul(x, w["w_main"], mode="nn", tm=1024, tn=1024, tk=512, out_dtype=F32, name=nm + "proj")
    proj_ab = _matmul(x, w["w_ab"], mode="nn", tm=1024, tn=AB_PAD, tk=2048, out_dtype=F32, name=nm + "proj_ab")
    bu = _matmul(proj, w["s5_bmat"], mode="nn", tm=1024, tn=1024, tk=GROUP_WIDTH, out_dtype=F32, name=nm + "s5_bu",
                 a_cols=GROUP_WIDTH)
    hs = _s5_scan(bu, w["s5_lam"], reverse=False, name=nm + "s5_scan")
    y = _matmul(hs, w["s5_cmat"], mode="nn", tm=1024, tn=GROUP_WIDTH, tk=1024, out_dtype=F32, name=nm + "s5_y",
                extras=[(proj, (None, GROUP_WIDTH), lambda i, j: (i, 0)), (w["s5_d"], (1, GROUP_WIDTH), lambda i, j: (0, 0))],
                epi=lambda acc, u, d: acc + d * u)
    m_s5 = _s5_glu_fwd(y, w["s5_glu_w"], w["s5_glu_b"], nm + "s5_glu")
    m_sgu = _sgu_fwd(proj, w["sgu_norm_g"], w["sgu_norm_b"], w["sgu_wm"], w["sgu_bfull"], nm + "sgu")
    m_pool, pooled = _pool_fwd(proj, w["pool_w"], w["pool_scale"], nm + "pool")
    qn, kn, v, cq, ck, cv, gb = _gdn_pre_fwd(proj, proj_ab, w["dn_conv_w"], w["dn_a_log"], w["dn_dt_bias"], nm + "gdn_pre")
    g_row, g_col, b_col = _gdn_vectors(gb, l // DN_CHUNK)
    o, states = _gdn_core_fwd(qn, kn, v, g_row, g_col, b_col, nm + "gdn_core")
    m_dn = _gdn_post_fwd(o, proj, w["dn_norm_g4"], nm + "gdn_post")
    mixed = jnp.concatenate([m_s5, m_sgu, m_pool, m_dn], axis=1)
    y1 = _matmul(mixed, w["w_out"], mode="nn", tm=1024, tn=1024, tk=512, out_dtype=F32, name=nm + "out_proj")
    h1, x1 = _ln_fwd(x, y1, w["ln1_g"], w["ln1_b"], nm + "ln1")
    r = _matmul(x1, w["w_up"], mode="nn", tm=1024, tn=1024, tk=512, out_dtype=BF16, name=nm + "up",
                epi=lambda acc: jnp.maximum(acc, 0.0))
    y2 = _matmul(r, w["w_down"], mode="nn", tm=1024, tn=1024, tk=512, out_dtype=F32, name=nm + "down",
                 a_fn=lambda a: a * a)
    h2, x2 = _ln_fwd(x1, y2, w["ln2_g"], w["ln2_b"], nm + "ln2")
    saved = dict(x=x, proj=proj, proj_ab=proj_ab, hs=hs, y=y, pooled=pooled, qn=qn, kn=kn, v=v, cq=cq, ck=ck, cv=cv, gb=gb,
                 g_row=g_row, g_col=g_col, b_col=b_col, o=o, states=states, mixed=mixed, h1=h1, x1=x1, r=r, h2=h2)
    return x2, saved


def _layer_bwd(dx2, s, w, li):
    nm = f"l{li}b_"
    l = dx2.shape[0]
    gw = GROUP_WIDTH
    g = {}
    dh2, g["ln2_g"], g["ln2_b"] = _ln_bwd(dx2, s["h2"], w["ln2_g"], nm + "ln2")
    g["w_down"] = _matmul(s["r"], dh2, mode="tn", tm=1024, tn=1024, tk=512, out_dtype=F32, name=nm + "dw_down", a_fn=lambda a: a * a)
    dpre = _matmul(dh2, w["w_down"], mode="nt", tm=1024, tn=1024, tk=512, out_dtype=BF16, name=nm + "dpre",
                   extras=[(s["r"], (None, None), lambda i, j: (i, j))], epi=lambda acc, r: acc * 2.0 * r.astype(F32))
    g["w_up"] = _matmul(s["x1"], dpre, mode="tn", tm=1024, tn=1024, tk=512, out_dtype=F32, name=nm + "dw_up")
    dx1 = _matmul(dpre, w["w_up"], mode="nt", tm=1024, tn=1024, tk=512, out_dtype=F32, name=nm + "dx1",
                  extras=[(dh2, (None, None), lambda i, j: (i, j))], epi=lambda acc, e: acc + ALPHA * e)
    dh1, g["ln1_g"], g["ln1_b"] = _ln_bwd(dx1, s["h1"], w["ln1_g"], nm + "ln1")
    g["w_out"] = _matmul(s["mixed"], dh1, mode="tn", tm=1024, tn=1024, tk=512, out_dtype=F32, name=nm + "dw_out")
    dmixed = _matmul(dh1, w["w_out"], mode="nt", tm=1024, tn=1024, tk=512, out_dtype=F32, name=nm + "dmixed")
    proj, proj_ab = s["proj"], s["proj_ab"]
    dy, dz, yg, g["s5_glu_b"], g["s5_d"] = _s5_glu_bwd(dmixed, s["y"], proj, w["s5_glu_w"], w["s5_glu_b"], nm + "s5_glu")
    g["s5_glu_w"] = _matmul(yg, dz, mode="tn", tm=gw, tn=gw, tk=1024, out_dtype=F32, name=nm + "dw_glu")
    dhs = _matmul(dy, w["s5_cmat"], mode="nt", tm=1024, tn=1024, tk=gw, out_dtype=F32, name=nm + "s5_dh")
    adj, g["s5_lam"] = _s5_scan(dhs, w["s5_lam_conj"], reverse=True, h=s["hs"], name=nm + "s5_scan")
    g["s5_cmat"] = _matmul(s["hs"], dy, mode="tn", tm=1024, tn=gw, tk=1024, out_dtype=F32, name=nm + "s5_dc")
    g["s5_bmat"] = _matmul(proj, adj, mode="tn", tm=gw, tn=1024, tk=1024, out_dtype=F32, name=nm + "s5_db", a_cols=gw)
    du_s5 = _matmul(adj, w["s5_bmat"], mode="nt", tm=1024, tn=gw, tk=1024, out_dtype=BF16, name=nm + "s5_du",
                    extras=[(dy, (None, gw), lambda i, j: (i, 0)), (w["s5_d"], (1, gw), lambda i, j: (0, 0))],
                    epi=lambda acc, dyv, d: acc + d * dyv)
    dzu, dzv, g["sgu_w"], g["sgu_bfull"], g["sgu_norm_g"], g["sgu_norm_b"] = _sgu_bwd(
        dmixed, proj, w["sgu_norm_g"], w["sgu_norm_b"], w["sgu_wm"], w["sgu_bfull"], nm + "sgu")
    dpooled, g["pool_w"], g["pool_scale"] = _pool_bwd_map(dmixed, s["pooled"], w["pool_w"], w["pool_scale"], nm + "pool_map")
    dp = _pool_bwd_window(dpooled, nm + "pool_win")
    do, dgate, g["dn_norm_g4"] = _gdn_post_bwd(dmixed, s["o"], proj, w["dn_norm_g4"], nm + "gdn_post")
    dq, dk, dv, dg_col, db_col = _gdn_core_bwd(do, s["qn"], s["kn"], s["v"], s["g_row"], s["g_col"], s["b_col"], s["states"], nm + "gdn_core")
    def uncol(t):
        return t.reshape(DN_HEADS, l).T
    dgb = jnp.concatenate([uncol(dg_col), uncol(db_col), jnp.zeros((l, AB_PAD - 2 * DN_HEADS), F32)], axis=1)
    dcq, dck, dcv, dab, g["dn_a_log"], g["dn_dt_bias"] = _gdn_pre_bwd(
        dq, dk, dv, s["cq"], s["ck"], s["cv"], dgb, s["gb"], proj_ab, w["dn_a_log"], w["dn_dt_bias"], nm + "gdn_pre")
    dxs, dws = [], []
    for p, dc in enumerate((dcq, dck, dcv)):
        dxp, dwp = _conv_bwd(dc, proj, QKV_BLK + p, w["dn_conv_w"][:, p * gw:(p + 1) * gw], nm + f"conv{p}")
        dxs.append(dxp)
        dws.append(dwp[:DN_CONV])
    g["dn_conv_w"] = jnp.concatenate(dws, axis=1)
    dproj = jnp.concatenate([du_s5, dzu, dzv, dp] + dxs + [dgate], axis=1)
    x = s["x"]
    g["w_main"] = _matmul(x, dproj, mode="tn", tm=1024, tn=1024, tk=512, out_dtype=F32, name=nm + "dw_main")
    g["w_ab"] = _matmul(x, dab, mode="tn", tm=1024, tn=AB_PAD, tk=1024, out_dtype=F32, name=nm + "dw_ab")
    dx_ab = _matmul(dab, w["w_ab"], mode="nt", tm=1024, tn=1024, tk=AB_PAD, out_dtype=F32, name=nm + "dx_ab",
                    extras=[(dh1, (None, None), lambda i, j: (i, j))], epi=lambda acc, e: acc + ALPHA * e)
    dx = _matmul(dproj, w["w_main"], mode="nt", tm=1024, tn=1024, tk=512, out_dtype=F32, name=nm + "dx",
                 extras=[(dx_ab, (None, None), lambda i, j: (i, j))], epi=lambda acc, e: acc + e)
    return dx, g


SMALL = ("s5_lambda_re", "s5_lambda_im", "s5_log_step", "s5_b_re", "s5_b_im", "s5_c_re", "s5_c_im", "s5_d", "s5_glu_b",
         "sgu_norm_g", "sgu_norm_b", "sgu_w", "sgu_b", "pool_w", "pool_scale", "dn_a_log", "dn_dt_bias", "dn_norm_g",
         "ln1_g", "ln1_b", "ln2_g", "ln2_b")
SHARDED = ("w_in", "s5_glu_w", "dn_conv_w", "w_out", "w_up", "w_down")


def _pad_lanes(v, width=AB_PAD):
    return jnp.pad(v.reshape(1, -1), ((0, 0), (0, width - v.size)))


def _prep_layer(p):
    mx = MXU_DTYPE
    lbr, lbi, bbr, bbi = _s5_discretize(p["s5_lambda_re"], p["s5_lambda_im"], p["s5_log_step"], p["s5_b_re"], p["s5_b_im"])
    bmat, cmat = _s5_dense(bbr, bbi, p["s5_c_re"], p["s5_c_im"])
    causal = jnp.tril(jnp.ones((SGU_CHUNK, SGU_CHUNK), F32))
    w_in = p["w_in"]
    return dict(
        w_main=w_in[:, :MAIN_COLS], w_ab=jnp.pad(w_in[:, MAIN_COLS:], ((0, 0), (0, AB_PAD - (w_in.shape[1] - MAIN_COLS)))),
        s5_bmat=bmat.astype(mx), s5_cmat=cmat.astype(mx),
        s5_lam=jnp.concatenate([lbr.reshape(1, -1), lbi.reshape(1, -1)], axis=1),
        s5_lam_conj=jnp.concatenate([lbr.reshape(1, -1), -lbi.reshape(1, -1)], axis=1),
        s5_d=p["s5_d"].reshape(1, -1), s5_glu_w=p["s5_glu_w"], s5_glu_b=p["s5_glu_b"].reshape(1, -1),
        sgu_norm_g=p["sgu_norm_g"].reshape(1, -1), sgu_norm_b=p["sgu_norm_b"].reshape(1, -1),
        sgu_wm=(p["sgu_w"] * causal).astype(mx), sgu_bfull=jnp.repeat(p["sgu_b"].T, GROUP_WIDTH // SGU_HEADS, axis=1),
        pool_w=p["pool_w"].astype(mx), pool_scale=p["pool_scale"].reshape(1, -1),
        dn_conv_w=p["dn_conv_w"], dn_a_log=_pad_lanes(p["dn_a_log"]), dn_dt_bias=_pad_lanes(p["dn_dt_bias"]),
        dn_norm_g4=jnp.tile(p["dn_norm_g"].reshape(1, -1), (1, DN_HEADS)),
        w_out=p["w_out"], ln1_g=p["ln1_g"].reshape(1, -1), ln1_b=p["ln1_b"].reshape(1, -1),
        w_up=p["w_up"], w_down=p["w_down"], ln2_g=p["ln2_g"].reshape(1, -1), ln2_b=p["ln2_b"].reshape(1, -1),
    )


def _unprep_grads(g, p):
    causal = jnp.tril(jnp.ones((SGU_CHUNK, SGU_CHUNK), F32))
    dbbr, dbbi = _s5_undense_b(g["s5_bmat"])
    dc_re, dc_im = _s5_undense_c(g["s5_cmat"])
    dlbr, dlbi = g["s5_lam"][0, :S5_NS].reshape(S5_GROUPS, S5_STATE), g["s5_lam"][0, S5_NS:].reshape(S5_GROUPS, S5_STATE)
    _, vjp = jax.vjp(_s5_discretize, p["s5_lambda_re"], p["s5_lambda_im"], p["s5_log_step"], p["s5_b_re"], p["s5_b_im"])
    d_lre, d_lim, d_step, d_bre, d_bim = vjp((dlbr, dlbi, dbbr, dbbi))
    hd = GROUP_WIDTH // SGU_HEADS
    return dict(
        w_in=jnp.concatenate([g["w_main"], g["w_ab"][:, :2 * DN_HEADS]], axis=1),
        s5_lambda_re=d_lre, s5_lambda_im=d_lim, s5_log_step=d_step, s5_b_re=d_bre, s5_b_im=d_bim, s5_c_re=dc_re, s5_c_im=dc_im,
        s5_d=g["s5_d"].reshape(S5_GROUPS, S5_CH), s5_glu_w=g["s5_glu_w"], s5_glu_b=g["s5_glu_b"].reshape(-1),
        sgu_norm_g=g["sgu_norm_g"].reshape(-1), sgu_norm_b=g["sgu_norm_b"].reshape(-1), sgu_w=g["sgu_w"] * causal,
        sgu_b=jnp.sum(g["sgu_bfull"].reshape(SGU_CHUNK, SGU_HEADS, hd), axis=2).T,
        pool_w=g["pool_w"], pool_scale=g["pool_scale"].reshape(-1), dn_conv_w=g["dn_conv_w"],
        dn_a_log=g["dn_a_log"][0, :DN_HEADS], dn_dt_bias=g["dn_dt_bias"][0, :DN_HEADS],
        dn_norm_g=jnp.sum(g["dn_norm_g4"].reshape(DN_HEADS, DN_HEAD_DIM), axis=0),
        w_out=g["w_out"], ln1_g=g["ln1_g"].reshape(-1), ln1_b=g["ln1_b"].reshape(-1),
        w_up=g["w_up"], w_down=g["w_down"], ln2_g=g["ln2_g"].reshape(-1), ln2_b=g["ln2_b"].reshape(-1),
    )


def _local_step(x, target, params):
    layers = [{k: v[i] for k, v in params.items()} for i in range(DEPTH)]
    preps = [_prep_layer(p) for p in layers]
    saved = []
    h = x
    for i in range(DEPTH):
        h, s = _layer_fwd(h, preps[i], i)
        saved.append(s)
    loss, dh = _loss_head(h, target)
    grads = [None] * DEPTH
    for i in reversed(range(DEPTH)):
        dh, g = _layer_bwd(dh, saved[i], preps[i], i)
        grads[i] = _unprep_grads(g, layers[i])
    return loss, dh, {k: jnp.stack([grads[i][k] for i in range(DEPTH)]) for k in grads[0]}


MESH_AXES = ("x", "y", "c")
OFFSETS = [(dx, dy, dc) for dx in (0, 1) for dy in (0, 1) for dc in (0, 1)][1:]


def _me_and_peers():
    x, y, c = (lax.axis_index(a) for a in MESH_AXES)
    def flip(v, d):
        return 1 - v if d else v
    peers = [(flip(x, dx), flip(y, dy), flip(c, dc)) for dx, dy, dc in OFFSETS]
    def idx(p):
        return 4 * p[0] + 2 * p[1] + p[2]
    return idx((x, y, c)), peers, [idx(p) for p in peers]


def _exchange(arrays, *, scatter, name):
    n = len(arrays)
    np_ = len(OFFSETS)

    def body(*refs):
        ins, outs = refs[:n], refs[n:2 * n]
        send_sems, recv_sems, local_sems = refs[2 * n:]
        me, peers, peer_idx = _me_and_peers()
        local, sends = [], []
        for k in range(n):
            cp = pltpu.make_async_copy(ins[k].at[me] if scatter else ins[k], outs[k].at[me], local_sems.at[k])
            cp.start()
            local.append(cp)
            for d in range(np_):
                rc = pltpu.make_async_remote_copy(
                    src_ref=ins[k].at[peer_idx[d]] if scatter else ins[k], dst_ref=outs[k].at[me],
                    send_sem=send_sems.at[k, d], recv_sem=recv_sems.at[k, d],
                    device_id=peers[d], device_id_type=pl.DeviceIdType.MESH)
                rc.start()
                sends.append(rc)
        for k in range(n):
            for d in range(np_):
                pltpu.make_async_remote_copy(
                    src_ref=ins[k].at[peer_idx[d]] if scatter else ins[k], dst_ref=outs[k].at[peer_idx[d]],
                    send_sem=send_sems.at[k, d], recv_sem=recv_sems.at[k, d],
                    device_id=peers[d], device_id_type=pl.DeviceIdType.MESH).wait_recv()
        for rc in sends:
            rc.wait_send()
        for cp in local:
            cp.wait()

    hbm = pl.BlockSpec(memory_space=pltpu.HBM)
    out_shape = [jax.ShapeDtypeStruct(a.shape if scatter else (N_DEV,) + a.shape, a.dtype) for a in arrays]
    return pl.pallas_call(
        body, name=name, in_specs=[hbm] * n, out_specs=[hbm] * n, out_shape=out_shape,
        scratch_shapes=[pltpu.SemaphoreType.DMA((n, np_)), pltpu.SemaphoreType.DMA((n, np_)), pltpu.SemaphoreType.DMA((n,))],
    )(*arrays)


def _adamw(w, gparts, m, v, name):
    r, c = w.shape
    lanes = -(-c // 128) * 128
    tr = r
    while tr * lanes * 4 * N_DEV > (4 << 20) and tr % 16 == 0:
        tr //= 2

    def body(w_ref, g_ref, m_ref, v_ref, go_ref, d_ref, mo_ref, vo_ref):
        g = g_ref[0].astype(F32)
        for s in range(1, N_DEV):
            g = g + g_ref[s].astype(F32)
        mn = ADAM_B1 * m_ref[...] + (1.0 - ADAM_B1) * g
        vn = ADAM_B2 * v_ref[...] + (1.0 - ADAM_B2) * g * g
        m_hat = mn / (1.0 - ADAM_B1 ** ADAM_STEP)
        v_hat = vn / (1.0 - ADAM_B2 ** ADAM_STEP)
        go_ref[...] = g
        d_ref[...] = -ADAM_LR * (m_hat / (jnp.sqrt(v_hat) + ADAM_EPS) + ADAM_WD * w_ref[...])
        mo_ref[...] = mn
        vo_ref[...] = vn

    row = pl.BlockSpec((tr, c), lambda i: (i, 0))
    return pl.pallas_call(
        body, name=name, grid=(r // tr,), in_specs=[row, pl.BlockSpec((N_DEV, tr, c), lambda i: (0, i, 0)), row, row],
        out_specs=[row] * 4, out_shape=[jax.ShapeDtypeStruct((r, c), F32)] * 4, compiler_params=_params(("parallel",)),
    )(w, gparts, m, v)


def _ungather(name, t):
    if name in ("w_in", "dn_conv_w", "w_up"):
        return jnp.transpose(t, (1, 2, 0, 3)).reshape(t.shape[1], t.shape[2], N_DEV * t.shape[3])
    return jnp.transpose(t, (1, 0, 2, 3)).reshape(t.shape[1], N_DEV * t.shape[2], t.shape[3])


def _to_slabs(name, g):
    dpt, a, b = g.shape
    if name in ("w_in", "dn_conv_w", "w_up"):
        return jnp.transpose(g.reshape(dpt, a, N_DEV, b // N_DEV), (2, 0, 1, 3))
    return jnp.transpose(g.reshape(dpt, N_DEV, a // N_DEV, b), (1, 0, 2, 3))


PACK_LANES = 128
PACK_ROWS_MULT = 1024


def _pack(vals):
    flat = jnp.concatenate([v.reshape(-1) for v in vals])
    rows = -(-flat.size // (PACK_LANES * PACK_ROWS_MULT)) * PACK_ROWS_MULT
    return jnp.pad(flat, (0, rows * PACK_LANES - flat.size)).reshape(rows, PACK_LANES)


def _unpack(packed, like):
    flat, out, off = packed.reshape(-1), [], 0
    for v in like:
        out.append(flat[off:off + v.size].reshape(v.shape))
        off += v.size
    return out


def kernel(x, w_in, s5_lambda_re, s5_lambda_im, s5_log_step, s5_b_re, s5_b_im, s5_c_re, s5_c_im, s5_d, s5_glu_w, s5_glu_b, sgu_norm_g, sgu_norm_b, sgu_w, sgu_b, pool_w, pool_scale, dn_conv_w, dn_a_log, dn_dt_bias, dn_norm_g, w_out, ln1_g, ln1_b, w_up, w_down, ln2_g, ln2_b, loss_target, m_w_in, m_s5_lambda_re, m_s5_lambda_im, m_s5_log_step, m_s5_b_re, m_s5_b_im, m_s5_c_re, m_s5_c_im, m_s5_d, m_s5_glu_w, m_s5_glu_b, m_sgu_norm_g, m_sgu_norm_b, m_sgu_w, m_sgu_b, m_pool_w, m_pool_scale, m_dn_conv_w, m_dn_a_log, m_dn_dt_bias, m_dn_norm_g, m_w_out, m_ln1_g, m_ln1_b, m_w_up, m_w_down, m_ln2_g, m_ln2_b, v_w_in, v_s5_lambda_re, v_s5_lambda_im, v_s5_log_step, v_s5_b_re, v_s5_b_im, v_s5_c_re, v_s5_c_im, v_s5_d, v_s5_glu_w, v_s5_glu_b, v_sgu_norm_g, v_sgu_norm_b, v_sgu_w, v_sgu_b, v_pool_w, v_pool_scale, v_dn_conv_w, v_dn_a_log, v_dn_dt_bias, v_dn_norm_g, v_w_out, v_ln1_g, v_ln1_b, v_w_up, v_w_down, v_ln2_g, v_ln2_b):
    names = ("w_in", "s5_lambda_re", "s5_lambda_im", "s5_log_step", "s5_b_re", "s5_b_im", "s5_c_re", "s5_c_im", "s5_d", "s5_glu_w",
             "s5_glu_b", "sgu_norm_g", "sgu_norm_b", "sgu_w", "sgu_b", "pool_w", "pool_scale", "dn_conv_w", "dn_a_log", "dn_dt_bias",
             "dn_norm_g", "w_out", "ln1_g", "ln1_b", "w_up", "w_down", "ln2_g", "ln2_b")
    env = locals()
    w = {n: env[n] for n in names}
    m = {n: env["m_" + n] for n in names}
    v = {n: env["v_" + n] for n in names}

    wire = {n: (w[n] if n == "dn_conv_w" else w[n].astype(MXU_DTYPE)) for n in SHARDED}
    gathered = _exchange([wire[n] for n in SHARDED], scatter=False, name="gather_weights")
    full = {n: w[n] for n in SMALL}
    full.update({n: _ungather(n, t) for n, t in zip(SHARDED, gathered)})

    loss, grad_x, grads = _local_step(x[0], loss_target[0], full)

    slabs = _exchange([_to_slabs(n, grads[n]) for n in SHARDED], scatter=True, name="scatter_grads")
    small_parts = _exchange([_pack([grads[n] for n in SMALL])], scatter=False, name="gather_small_grads")[0]

    g_out, d_out, m_out, v_out = {}, {}, {}, {}
    for n, parts in zip(SHARDED, slabs):
        shp = w[n].shape
        two_d = (shp[0] * shp[1], shp[2])
        res = _adamw(w[n].reshape(two_d), parts.reshape((N_DEV,) + two_d), m[n].reshape(two_d), v[n].reshape(two_d), "adamw_" + n)
        g_out[n], d_out[n], m_out[n], v_out[n] = (t.reshape(shp) for t in res)
    like = [w[n] for n in SMALL]
    res = _adamw(_pack(like), small_parts, _pack([m[n] for n in SMALL]), _pack([v[n] for n in SMALL]), "adamw_small")
    for dst, packed in zip((g_out, d_out, m_out, v_out), res):
        dst.update(dict(zip(SMALL, _unpack(packed, like))))

    total = lax.psum(loss[0, 0], MESH_AXES)
    return (total, grad_x[None], *[g_out[n] for n in names], *[d_out[n] for n in names],
            *[m_out[n] for n in names], *[v_out[n] for n in names])
```

```python
import functools
import math

import jax
import jax.numpy as jnp
from jax import lax
from jax.experimental import pallas as pl
from jax.experimental.pallas import tpu as pltpu

F32 = jnp.float32
BF16 = jnp.bfloat16
MXU_DTYPE = jnp.bfloat16
HI = lax.Precision.HIGHEST

N_DEV = 8
D_MODEL = 2048
DEPTH = 2
GROUP_WIDTH = 512
S5_GROUPS, S5_CH, S5_STATE = 32, 16, 64
S5_NS = S5_GROUPS * S5_STATE
SGU_CHUNK, SGU_HEADS = 128, 8
POOL_WINDOWS = (2, 4, 8, 16)
DN_HEADS, DN_HEAD_DIM, DN_CONV, DN_CHUNK = 4, 128, 4, 64
D_FF = 4 * D_MODEL
LN_EPS, RMS_EPS, L2_EPS = 1e-5, 1e-6, 1e-6
ALPHA = (2 * DEPTH) ** 0.25
MAIN_COLS = 4096
AB_PAD = 128
ADAM_LR, ADAM_B1, ADAM_B2, ADAM_EPS, ADAM_WD, ADAM_STEP = 0.001, 0.9, 0.999, 1e-08, 0.01, 10
VMEM_LIMIT = 56 * 1024 * 1024
C_GELU = math.sqrt(2.0 / math.pi)


def _params(sem=None):
    return pltpu.CompilerParams(dimension_semantics=sem, vmem_limit_bytes=VMEM_LIMIT)


def _gelu(x):
    return 0.5 * x * (1.0 + jnp.tanh(C_GELU * (x + 0.044715 * x * x * x)))


def _gelu_grad(x):
    t = jnp.tanh(C_GELU * (x + 0.044715 * x * x * x))
    return 0.5 * (1.0 + t) + 0.5 * x * (1.0 - t * t) * C_GELU * (1.0 + 3.0 * 0.044715 * x * x)


def _sigmoid(x):
    return 1.0 / (1.0 + jnp.exp(-x))


def _silu(x):
    return x * _sigmoid(x)


def _silu_grad(x):
    s = _sigmoid(x)
    return s * (1.0 + x * (1.0 - s))


def _softplus(x):
    z = jnp.exp(-jnp.abs(x))
    small = z * (1.0 - z * (0.5 - z * (1.0 / 3.0)))
    return jnp.maximum(x, 0.0) + jnp.where(z < 1e-2, small, jnp.log(1.0 + z))


def _mx(x):
    return x.astype(MXU_DTYPE)


def _dot(a, b, dims="nn", precision=None):
    cd = {"nn": ((1,), (0,)), "nt": ((1,), (1,)), "tn": ((0,), (0,))}[dims]
    return lax.dot_general(a, b, (cd, ((), ())), preferred_element_type=F32, precision=precision)


def _mdot(a, b, dims="nn"):
    return _dot(_mx(a), _mx(b), dims)


MESH_AXES = ("x", "y", "c")
OFFSETS = [(dx, dy, dc) for dx in (0, 1) for dy in (0, 1) for dc in (0, 1)][1:]


def _me_and_peers():
    x, y, c = (lax.axis_index(a) for a in MESH_AXES)
    def flip(v, d):
        return 1 - v if d else v
    peers = [(flip(x, dx), flip(y, dy), flip(c, dc)) for dx, dy, dc in OFFSETS]
    def idx(p):
        return 4 * p[0] + 2 * p[1] + p[2]
    return idx((x, y, c)), peers, [idx(p) for p in peers]


class _Comm:
    def __init__(self, ops):
        self.arrays = [a for a, _ in ops]
        self.scatter = [s for _, s in ops]
        self.n = n = len(ops)
        hbm = pl.BlockSpec(memory_space=pltpu.HBM)
        self.in_specs, self.out_specs = [hbm] * n, [hbm] * n
        self.out_shape = [jax.ShapeDtypeStruct(a.shape if s else (N_DEV,) + a.shape, a.dtype) for a, s in ops]
        npeer = len(OFFSETS)
        self.scratch = [pltpu.SemaphoreType.DMA((n, npeer)), pltpu.SemaphoreType.DMA((n, npeer)), pltpu.SemaphoreType.DMA((n,))]

    def _copies(self, ins, outs, sems):
        send_sems, recv_sems, local_sems = sems
        me, peers, peer_idx = _me_and_peers()
        local, sends, recvs = [], [], []
        for k in range(self.n):
            sc = self.scatter[k]
            local.append(pltpu.make_async_copy(ins[k].at[me] if sc else ins[k], outs[k].at[me], local_sems.at[k]))
            for d in range(len(OFFSETS)):
                src = ins[k].at[peer_idx[d]] if sc else ins[k]
                common = dict(send_sem=send_sems.at[k, d], recv_sem=recv_sems.at[k, d], device_id=peers[d],
                              device_id_type=pl.DeviceIdType.MESH)
                sends.append(pltpu.make_async_remote_copy(src_ref=src, dst_ref=outs[k].at[me], **common))
                recvs.append(pltpu.make_async_remote_copy(src_ref=src, dst_ref=outs[k].at[peer_idx[d]], **common))
        return local, sends, recvs

    def start(self, ins, outs, sems):
        local, sends, _ = self._copies(ins, outs, sems)
        for cp in local + sends:
            cp.start()

    def wait(self, ins, outs, sems):
        local, sends, recvs = self._copies(ins, outs, sems)
        for cp in recvs:
            cp.wait_recv()
        for cp in sends:
            cp.wait_send()
        for cp in local:
            cp.wait()


def _exchange(ops, name):
    cm = _Comm(ops)

    def body(*refs):
        ins, outs, sems = refs[:cm.n], refs[cm.n:2 * cm.n], refs[2 * cm.n:]
        cm.start(ins, outs, sems)
        cm.wait(ins, outs, sems)

    return pl.pallas_call(body, name=name, in_specs=cm.in_specs, out_specs=cm.out_specs, out_shape=cm.out_shape,
                          scratch_shapes=cm.scratch)(*cm.arrays)


def _matmul(a, b, *, mode, tm, tn, tk, out_dtype, name, a_fn=None, extras=(), epi=None, a_cols=None,
            b_slab=None, out_slab=None, comm=None):
    a_shape = a.shape if a_cols is None else (a.shape[0], a_cols)
    b_shape = b.shape if b_slab is None else (b.shape[1], N_DEV * b_slab)
    if mode == "nn":
        (m, k), n = a_shape, b_shape[1]
    elif mode == "nt":
        (m, k), n = a_shape, b_shape[0]
    else:
        (k, m), n = a_shape, b_shape[1]
    tm, tn, tk = min(tm, m), min(tn, n), min(tk, k)
    assert m % tm == 0 and n % tn == 0 and k % tk == 0, (name, a.shape, b.shape, tm, tn, tk)
    gi, gj, nk = m // tm, n // tn, k // tk
    n_ex = len(extras)
    cm = _Comm(comm) if comm else None
    nc = cm.n if cm else 0

    def body(a_ref, b_ref, *rest):
        ex_refs, rest = rest[:n_ex], rest[n_ex:]
        c_ins, o_ref, c_outs, acc, sems = rest[:nc], rest[nc], rest[nc + 1:2 * nc + 1], rest[2 * nc + 1], rest[2 * nc + 2:]
        i, j, kk = pl.program_id(0), pl.program_id(1), pl.program_id(2)
        if cm:
            @pl.when((i == 0) & (j == 0) & (kk == 0))
            def _():
                cm.start(c_ins, c_outs, sems)

        @pl.when(kk == 0)
        def _():
            acc[...] = jnp.zeros_like(acc)

        av = a_ref[...]
        if a_fn is not None:
            av = a_fn(av)
        acc[...] += _dot(_mx(av), _mx(b_ref[...]), mode)

        @pl.when(kk == nk - 1)
        def _():
            r = acc[...]
            if epi is not None:
                r = epi(r, *[e[...] for e in ex_refs])
            o_ref[...] = r.astype(out_dtype)

        if cm:
            @pl.when((i == gi - 1) & (j == gj - 1) & (kk == nk - 1))
            def _():
                cm.wait(c_ins, c_outs, sems)

    a_spec = pl.BlockSpec((tk, tm), lambda i, j, kk: (kk, i)) if mode == "tn" else pl.BlockSpec((tm, tk), lambda i, j, kk: (i, kk))
    if b_slab is None:
        b_spec = pl.BlockSpec((tn, tk), lambda i, j, kk: (j, kk)) if mode == "nt" else pl.BlockSpec((tk, tn), lambda i, j, kk: (kk, j))
    elif mode == "nt":
        assert b_slab % tk == 0
        b_spec = pl.BlockSpec((None, tn, tk), lambda i, j, kk: ((kk * tk) // b_slab, j, ((kk * tk) % b_slab) // tk))
    else:
        assert b_slab % tn == 0
        b_spec = pl.BlockSpec((None, tk, tn), lambda i, j, kk: ((j * tn) // b_slab, kk, ((j * tn) % b_slab) // tn))
    if out_slab is None:
        o_spec, o_shape = pl.BlockSpec((tm, tn), lambda i, j, kk: (i, j)), jax.ShapeDtypeStruct((m, n), out_dtype)
    else:
        assert out_slab % tn == 0 and n == N_DEV * out_slab
        o_spec = pl.BlockSpec((None, tm, tn), lambda i, j, kk: ((j * tn) // out_slab, i, ((j * tn) % out_slab) // tn))
        o_shape = jax.ShapeDtypeStruct((N_DEV, m, out_slab), out_dtype)
    ex_specs = [pl.BlockSpec((tm if bs[0] is None else bs[0], tn if bs[1] is None else bs[1]),
                             functools.partial(lambda i, j, kk, f: f(i, j), f=im)) for (_, bs, im) in extras]
    res = pl.pallas_call(
        body,
        name=name,
        grid=(gi, gj, nk),
        in_specs=[a_spec, b_spec, *ex_specs] + (cm.in_specs if cm else []),
        out_specs=[o_spec] + (cm.out_specs if cm else []),
        out_shape=[o_shape] + (cm.out_shape if cm else []),
        scratch_shapes=[pltpu.VMEM((tm, tn), F32)] + (cm.scratch if cm else []),
        compiler_params=_params(("arbitrary",) * 3 if cm else ("parallel", "parallel", "arbitrary")),
    )(a, b, *[e[0] for e in extras], *(cm.arrays if cm else []))
    return (res[0], res[1:]) if cm else res[0]


def _ln_fwd(x, y, g, b, name):
    l, d = x.shape
    tl = 256

    def body(x_ref, y_ref, g_ref, b_ref, h_ref, o_ref):
        h = ALPHA * x_ref[...] + y_ref[...]
        mu = jnp.mean(h, axis=-1, keepdims=True)
        c = h - mu
        var = jnp.mean(c * c, axis=-1, keepdims=True)
        h_ref[...] = h
        o_ref[...] = c * lax.rsqrt(var + LN_EPS) * g_ref[...] + b_ref[...]

    row = pl.BlockSpec((tl, d), lambda i: (i, 0))
    vec = pl.BlockSpec((1, d), lambda i: (0, 0))
    return pl.pallas_call(
        body, name=name, grid=(l // tl,), in_specs=[row, row, vec, vec], out_specs=[row, row],
        out_shape=[jax.ShapeDtypeStruct((l, d), F32)] * 2, compiler_params=_params(("parallel",)),
    )(x, y, g, b)


def _ln_bwd(dout, h, g, name):
    l, d = h.shape
    tl = 256

    def body(do_ref, h_ref, g_ref, dh_ref, dg_ref, db_ref):
        @pl.when(pl.program_id(0) == 0)
        def _():
            dg_ref[...] = jnp.zeros_like(dg_ref)
            db_ref[...] = jnp.zeros_like(db_ref)

        hv, do = h_ref[...], do_ref[...]
        mu = jnp.mean(hv, axis=-1, keepdims=True)
        c = hv - mu
        r = lax.rsqrt(jnp.mean(c * c, axis=-1, keepdims=True) + LN_EPS)
        xh = c * r
        dxh = do * g_ref[...]
        m1 = jnp.mean(dxh, axis=-1, keepdims=True)
        m2 = jnp.mean(dxh * xh, axis=-1, keepdims=True)
        dh_ref[...] = r * (dxh - m1 - xh * m2)
        dg_ref[...] += jnp.sum(do * xh, axis=0, keepdims=True)
        db_ref[...] += jnp.sum(do, axis=0, keepdims=True)

    row = pl.BlockSpec((tl, d), lambda i: (i, 0))
    vec = pl.BlockSpec((1, d), lambda i: (0, 0))
    return pl.pallas_call(
        body, name=name, grid=(l // tl,), in_specs=[row, row, vec], out_specs=[row, vec, vec],
        out_shape=[jax.ShapeDtypeStruct((l, d), F32), jax.ShapeDtypeStruct((1, d), F32), jax.ShapeDtypeStruct((1, d), F32)],
        compiler_params=_params(("arbitrary",)),
    )(dout, h, g)


def _loss_head(y, target):
    l, d = y.shape
    tl = 256

    def body(y_ref, t_ref, loss_ref, dy_ref):
        @pl.when(pl.program_id(0) == 0)
        def _():
            loss_ref[...] = jnp.zeros_like(loss_ref)

        e = y_ref[...] - t_ref[...]
        dy_ref[...] = e * (1.0 / d)
        s = jnp.sum(jnp.sum(e * e, axis=1, keepdims=True), axis=0, keepdims=True)
        loss_ref[...] += s * (0.5 / d)

    row = pl.BlockSpec((tl, d), lambda i: (i, 0))
    return pl.pallas_call(
        body, name="loss_head", grid=(l // tl,), in_specs=[row, row],
        out_specs=[pl.BlockSpec((1, 1), lambda i: (0, 0)), row],
        out_shape=[jax.ShapeDtypeStruct((1, 1), F32), jax.ShapeDtypeStruct((l, d), F32)],
        compiler_params=_params(("arbitrary",)),
    )(y, target)


def _s5_discretize(lam_re, lam_im, log_step, b_re, b_im):
    step = jnp.exp(log_step)[:, None]
    e = jnp.exp(lam_re * step)
    lbr, lbi = e * jnp.cos(lam_im * step), e * jnp.sin(lam_im * step)
    den = lam_re * lam_re + lam_im * lam_im
    qr = ((lbr - 1.0) * lam_re + lbi * lam_im) / den
    qi = (lbi * lam_re - (lbr - 1.0) * lam_im) / den
    bbr = qr[:, :, None] * b_re - qi[:, :, None] * b_im
    bbi = qr[:, :, None] * b_im + qi[:, :, None] * b_re
    return lbr, lbi, bbr, bbi


def _s5_dense(bbr, bbi, c_re, c_im):
    eye = jnp.eye(S5_GROUPS, dtype=F32)
    def bd(t):
        return jnp.einsum("gph,gk->ghkp", t, eye).reshape(GROUP_WIDTH, S5_NS)
    def cd(t):
        return jnp.einsum("ghp,gk->gpkh", t, eye).reshape(S5_NS, GROUP_WIDTH)
    return jnp.concatenate([bd(bbr), bd(bbi)], axis=1), jnp.concatenate([cd(c_re), -cd(c_im)], axis=0)


def _s5_undense_b(dbmat):
    eye = jnp.eye(S5_GROUPS, dtype=F32)[:, None, :, None]
    def ex(t):
        return jnp.sum(t.reshape(S5_GROUPS, S5_CH, S5_GROUPS, S5_STATE) * eye, axis=2).transpose(0, 2, 1)
    return ex(dbmat[:, :S5_NS]), ex(dbmat[:, S5_NS:])


def _s5_undense_c(dcmat):
    eye = jnp.eye(S5_GROUPS, dtype=F32)[:, None, :, None]
    def ex(t):
        return jnp.sum(t.reshape(S5_GROUPS, S5_STATE, S5_GROUPS, S5_CH) * eye, axis=2).transpose(0, 2, 1)
    return ex(dcmat[:S5_NS]), -ex(dcmat[S5_NS:])


def _s5_scan(bu, lam, *, reverse, h=None, name):
    l, w = bu.shape
    ns = w // 2
    tl = 256
    nb = l // tl
    with_h = h is not None

    def body(*refs):
        if with_h:
            bu_ref, lam_ref, h_ref, o_ref, dl_ref, carry = refs
        else:
            bu_ref, lam_ref, o_ref, carry = refs

        @pl.when(pl.program_id(0) == 0)
        def _():
            carry[...] = jnp.zeros_like(carry)
            if with_h:
                dl_ref[...] = jnp.zeros_like(dl_ref)

        lr, li = lam_ref[:, :ns], lam_ref[:, ns:]

        def step(t, c):
            row = (tl - 1 - t) if reverse else t
            cr, ci = c[0], c[1]
            out = c[2:]
            if with_h:
                hr, hi = h_ref[pl.ds(row, 1), :ns], h_ref[pl.ds(row, 1), ns:]
                out = (out[0] + cr * hr + ci * hi, out[1] + ci * hr - cr * hi)
            nr = lr * cr - li * ci + bu_ref[pl.ds(row, 1), :ns]
            ni = lr * ci + li * cr + bu_ref[pl.ds(row, 1), ns:]
            o_ref[pl.ds(row, 1), :ns] = nr
            o_ref[pl.ds(row, 1), ns:] = ni
            return (nr, ni) + tuple(out)

        init = (carry[:, :ns], carry[:, ns:])
        if with_h:
            init = init + (dl_ref[:, :ns], dl_ref[:, ns:])
        fin = lax.fori_loop(0, tl, step, init)
        carry[:, :ns] = fin[0]
        carry[:, ns:] = fin[1]
        if with_h:
            dl_ref[:, :ns] = fin[2]
            dl_ref[:, ns:] = fin[3]

    idx = (lambda i: (nb - 1 - i, 0)) if reverse else (lambda i: (i, 0))
    row = pl.BlockSpec((tl, w), idx)
    vec = pl.BlockSpec((1, w), lambda i: (0, 0))
    in_specs = [row, vec] + ([row] if with_h else [])
    out_specs = [row] + ([vec] if with_h else [])
    out_shape = [jax.ShapeDtypeStruct((l, w), F32)] + ([jax.ShapeDtypeStruct((1, w), F32)] if with_h else [])
    res = pl.pallas_call(
        body, name=name, grid=(nb,), in_specs=in_specs, out_specs=out_specs, out_shape=out_shape,
        scratch_shapes=[pltpu.VMEM((1, w), F32)], compiler_params=_params(("arbitrary",)),
    )(*([bu, lam] + ([h] if with_h else [])))
    return res if with_h else res[0]


def _s5_glu_fwd(y, glu_w, glu_b, name):
    l, d = y.shape
    tl = min(512, l)

    def body(y_ref, w_ref, b_ref, o_ref):
        yg = _gelu(y_ref[...])
        z = _mdot(yg, w_ref[...]) + b_ref[...]
        o_ref[...] = (yg * _sigmoid(z)).astype(o_ref.dtype)

    return pl.pallas_call(
        body, name=name, grid=(l // tl,),
        in_specs=[pl.BlockSpec((tl, d), lambda i: (i, 0)), pl.BlockSpec((d, d), lambda i: (0, 0)), pl.BlockSpec((1, d), lambda i: (0, 0))],
        out_specs=pl.BlockSpec((tl, d), lambda i: (i, 0)), out_shape=jax.ShapeDtypeStruct((l, d), BF16),
        compiler_params=_params(("parallel",)),
    )(y, glu_w, glu_b)


def _s5_glu_bwd(dmixed, y, proj, glu_w, glu_b, name):
    l, d = y.shape
    tl = min(512, l)

    def body(do_ref, y_ref, u_ref, w_ref, b_ref, dy_ref, dz_ref, yg_ref, db_ref, dd_ref):
        @pl.when(pl.program_id(0) == 0)
        def _():
            db_ref[...] = jnp.zeros_like(db_ref)
            dd_ref[...] = jnp.zeros_like(dd_ref)

        yv, do = y_ref[...], do_ref[...]
        yg = _gelu(yv)
        gate = _sigmoid(_mdot(yg, w_ref[...]) + b_ref[...])
        dz = do * yg * gate * (1.0 - gate)
        dyg = do * gate + _mdot(dz, w_ref[...], "nt")
        dy = dyg * _gelu_grad(yv)
        dy_ref[...] = dy
        dz_ref[...] = dz.astype(dz_ref.dtype)
        yg_ref[...] = yg.astype(yg_ref.dtype)
        db_ref[...] += jnp.sum(dz, axis=0, keepdims=True)
        dd_ref[...] += jnp.sum(dy * u_ref[...], axis=0, keepdims=True)

    row = pl.BlockSpec((tl, d), lambda i: (i, 0))
    vec = pl.BlockSpec((1, d), lambda i: (0, 0))
    return pl.pallas_call(
        body, name=name, grid=(l // tl,),
        in_specs=[row, row, row, pl.BlockSpec((d, d), lambda i: (0, 0)), vec],
        out_specs=[row, row, row, vec, vec],
        out_shape=[jax.ShapeDtypeStruct((l, d), F32), jax.ShapeDtypeStruct((l, d), BF16), jax.ShapeDtypeStruct((l, d), BF16),
                   jax.ShapeDtypeStruct((1, d), F32), jax.ShapeDtypeStruct((1, d), F32)],
        compiler_params=_params(("arbitrary",)),
    )(dmixed, y, proj, glu_w, glu_b)


def _sgu_pair(w_ref, x, j, dims):
    lo = lax.broadcasted_iota(jnp.int32, x.shape, 1) < (GROUP_WIDTH // SGU_HEADS)
    xb = _mx(x)
    r0 = _dot(w_ref[2 * j], xb, dims)
    r1 = _dot(w_ref[2 * j + 1], xb, dims)
    return jnp.where(lo, r0, r1)


def _sgu_norm(v, g, b):
    mu = jnp.mean(v, axis=-1, keepdims=True)
    c = v - mu
    r = lax.rsqrt(jnp.mean(c * c, axis=-1, keepdims=True) + LN_EPS)
    return c * r, r


def _sgu_fwd(proj, norm_g, norm_b, wm, bfull, name):
    l = proj.shape[0]
    tl = 256
    gw = GROUP_WIDTH

    def body(zu_ref, zv_ref, g_ref, b_ref, w_ref, bf_ref, o_ref):
        for c in range(tl // SGU_CHUNK):
            rows = slice(c * SGU_CHUNK, (c + 1) * SGU_CHUNK)
            u = _gelu(zu_ref[rows, :])
            vh, _ = _sgu_norm(_gelu(zv_ref[rows, :]), None, None)
            vn = vh * g_ref[...] + b_ref[...]
            for j in range(gw // 128):
                cols = slice(j * 128, (j + 1) * 128)
                mixed = _sgu_pair(w_ref, vn[:, cols], j, "nn") + bf_ref[:, cols]
                o_ref[rows, cols] = (u[:, cols] * mixed).astype(o_ref.dtype)

    vec = pl.BlockSpec((1, gw), lambda i: (0, 0))
    return pl.pallas_call(
        body, name=name, grid=(l // tl,),
        in_specs=[pl.BlockSpec((tl, gw), lambda i: (i, 1)), pl.BlockSpec((tl, gw), lambda i: (i, 2)), vec, vec,
                  pl.BlockSpec((SGU_HEADS, SGU_CHUNK, SGU_CHUNK), lambda i: (0, 0, 0)), pl.BlockSpec((SGU_CHUNK, gw), lambda i: (0, 0))],
        out_specs=pl.BlockSpec((tl, gw), lambda i: (i, 0)), out_shape=jax.ShapeDtypeStruct((l, gw), BF16),
        compiler_params=_params(("parallel",)),
    )(proj, proj, norm_g, norm_b, wm, bfull)


def _sgu_bwd(dmixed, proj, norm_g, norm_b, wm, bfull, name):
    l = proj.shape[0]
    tl = 256
    gw = GROUP_WIDTH
    hd = gw // SGU_HEADS

    def body(do_ref, zu_ref, zv_ref, g_ref, b_ref, w_ref, bf_ref, dzu_ref, dzv_ref, dw_ref, dbf_ref, dg_ref, dnb_ref):
        @pl.when(pl.program_id(0) == 0)
        def _():
            dw_ref[...] = jnp.zeros_like(dw_ref)
            dbf_ref[...] = jnp.zeros_like(dbf_ref)
            dg_ref[...] = jnp.zeros_like(dg_ref)
            dnb_ref[...] = jnp.zeros_like(dnb_ref)

        for c in range(tl // SGU_CHUNK):
            rows = slice(c * SGU_CHUNK, (c + 1) * SGU_CHUNK)
            zu, zv, do = zu_ref[rows, :], zv_ref[rows, :], do_ref[rows, :]
            u = _gelu(zu)
            vh, r = _sgu_norm(_gelu(zv), None, None)
            vn = vh * g_ref[...] + b_ref[...]
            dvn_parts, mixed_parts = [], []
            for j in range(gw // 128):
                cols = slice(j * 128, (j + 1) * 128)
                vb = vn[:, cols]
                mixed_parts.append(_sgu_pair(w_ref, vb, j, "nn") + bf_ref[:, cols])
                dm = do[:, cols] * u[:, cols]
                dvn_parts.append(_sgu_pair(w_ref, dm, j, "tn"))
                lo = lax.broadcasted_iota(jnp.int32, dm.shape, 1) < hd
                dw_ref[2 * j] += _mdot(jnp.where(lo, dm, 0.0), vb, "nt")
                dw_ref[2 * j + 1] += _mdot(jnp.where(lo, 0.0, dm), vb, "nt")
                dbf_ref[:, cols] += dm
            mixed = jnp.concatenate(mixed_parts, axis=1)
            dvn = jnp.concatenate(dvn_parts, axis=1)
            dzu_ref[rows, :] = (do * mixed * _gelu_grad(zu)).astype(dzu_ref.dtype)
            dg_ref[...] += jnp.sum(dvn * vh, axis=0, keepdims=True)
            dnb_ref[...] += jnp.sum(dvn, axis=0, keepdims=True)
            dvh = dvn * g_ref[...]
            m1 = jnp.mean(dvh, axis=-1, keepdims=True)
            m2 = jnp.mean(dvh * vh, axis=-1, keepdims=True)
            dv = r * (dvh - m1 - vh * m2)
            dzv_ref[rows, :] = (dv * _gelu_grad(zv)).astype(dzv_ref.dtype)

    vec = pl.BlockSpec((1, gw), lambda i: (0, 0))
    row = pl.BlockSpec((tl, gw), lambda i: (i, 0))
    wspec = pl.BlockSpec((SGU_HEADS, SGU_CHUNK, SGU_CHUNK), lambda i: (0, 0, 0))
    bspec = pl.BlockSpec((SGU_CHUNK, gw), lambda i: (0, 0))
    return pl.pallas_call(
        body, name=name, grid=(l // tl,),
        in_specs=[pl.BlockSpec((tl, gw), lambda i: (i, 1)), pl.BlockSpec((tl, gw), lambda i: (i, 1)), pl.BlockSpec((tl, gw), lambda i: (i, 2)),
                  vec, vec, wspec, bspec],
        out_specs=[row, row, wspec, bspec, vec, vec],
        out_shape=[jax.ShapeDtypeStruct((l, gw), BF16), jax.ShapeDtypeStruct((l, gw), BF16),
                   jax.ShapeDtypeStruct((SGU_HEADS, SGU_CHUNK, SGU_CHUNK), F32), jax.ShapeDtypeStruct((SGU_CHUNK, gw), F32),
                   jax.ShapeDtypeStruct((1, gw), F32), jax.ShapeDtypeStruct((1, gw), F32)],
        compiler_params=_params(("arbitrary",)),
    )(dmixed, proj, proj, norm_g, norm_b, wm, bfull)


HALO = 16


def _window_sums(ext, n_rows, forward):
    def sh(x, k):
        return pltpu.roll(x, (n_rows - k) if forward else k, axis=0)
    s2 = ext + sh(ext, 1)
    s4 = s2 + sh(s2, 2)
    s8 = s4 + sh(s4, 4)
    s16 = s8 + sh(s8, 8)
    return (s2, s4, s8, s16)


def _pool_fwd(proj, pool_w, scale, name):
    l = proj.shape[0]
    tl = 256
    gw = GROUP_WIDTH
    pg = gw // len(POOL_WINDOWS)

    def body(x_ref, halo_ref, w_ref, s_ref, o_ref, p_ref):
        i = pl.program_id(0)
        x = x_ref[...]
        halo = jnp.where(i > 0, halo_ref[...], 0.0)
        ext = jnp.concatenate([halo, x], axis=0)
        sums = _window_sums(ext, tl + HALO, False)
        t = i * tl + lax.broadcasted_iota(jnp.int32, (tl, pg), 0)
        for gi, win in enumerate(POOL_WINDOWS):
            cols = slice(gi * pg, (gi + 1) * pg)
            cnt = jnp.minimum(t + 1, win).astype(F32)
            pooled = sums[gi][HALO:, cols] / cnt - x[:, cols]
            p_ref[:, cols] = pooled
            o_ref[:, cols] = (_mdot(pooled, w_ref[gi]) * s_ref[:, cols]).astype(o_ref.dtype)

    row = pl.BlockSpec((tl, gw), lambda i: (i, 0))
    return pl.pallas_call(
        body, name=name, grid=(l // tl,),
        in_specs=[pl.BlockSpec((tl, gw), lambda i: (i, 3)),
                  pl.BlockSpec((HALO, gw), lambda i: (jnp.maximum(i * (tl // HALO) - 1, 0), 3)),
                  pl.BlockSpec((len(POOL_WINDOWS), pg, pg), lambda i: (0, 0, 0)), pl.BlockSpec((1, gw), lambda i: (0, 0))],
        out_specs=[row, row], out_shape=[jax.ShapeDtypeStruct((l, gw), BF16), jax.ShapeDtypeStruct((l, gw), F32)],
        compiler_params=_params(("parallel",)),
    )(proj, proj, pool_w, scale)


def _pool_bwd_map(dmixed, pooled, pool_w, scale, name):
    l, gw = pooled.shape
    tl = 256
    ng = len(POOL_WINDOWS)
    pg = gw // ng

    def body(do_ref, p_ref, w_ref, s_ref, dp_ref, dw_ref, ds_ref):
        @pl.when(pl.program_id(0) == 0)
        def _():
            dw_ref[...] = jnp.zeros_like(dw_ref)
            ds_ref[...] = jnp.zeros_like(ds_ref)

        for gi in range(ng):
            cols = slice(gi * pg, (gi + 1) * pg)
            do, pooled_g = do_ref[:, cols], p_ref[:, cols]
            mixed = _mdot(pooled_g, w_ref[gi])
            ds_ref[:, cols] += jnp.sum(do * mixed, axis=0, keepdims=True)
            dm = do * s_ref[:, cols]
            dw_ref[gi] += _mdot(pooled_g, dm, "tn")
            dp_ref[:, cols] = _mdot(dm, w_ref[gi], "nt")

    row = pl.BlockSpec((tl, gw), lambda i: (i, 0))
    wspec = pl.BlockSpec((ng, pg, pg), lambda i: (0, 0, 0))
    vec = pl.BlockSpec((1, gw), lambda i: (0, 0))
    return pl.pallas_call(
        body, name=name, grid=(l // tl,),
        in_specs=[pl.BlockSpec((tl, gw), lambda i: (i, 2)), row, wspec, vec], out_specs=[row, wspec, vec],
        out_shape=[jax.ShapeDtypeStruct((l, gw), F32), jax.ShapeDtypeStruct((ng, pg, pg), F32), jax.ShapeDtypeStruct((1, gw), F32)],
        compiler_params=_params(("arbitrary",)),
    )(dmixed, pooled, pool_w, scale)


def _pool_bwd_window(dpooled, name):
    l, gw = dpooled.shape
    tl = 256
    nb = l // tl
    pg = gw // len(POOL_WINDOWS)

    def body(d_ref, halo_ref, o_ref):
        i = pl.program_id(0)
        d = d_ref[...]
        halo = jnp.where(i < nb - 1, halo_ref[...], 0.0)
        ext = jnp.concatenate([d, halo], axis=0)
        t = i * tl + lax.broadcasted_iota(jnp.int32, (tl + HALO, pg), 0)
        for gi, win in enumerate(POOL_WINDOWS):
            cols = slice(gi * pg, (gi + 1) * pg)
            cnt = jnp.minimum(t + 1, win).astype(F32)
            sums = _window_sums(ext[:, cols] / cnt, tl + HALO, True)
            o_ref[:, cols] = (sums[gi][:tl, :] - d[:, cols]).astype(o_ref.dtype)

    row = pl.BlockSpec((tl, gw), lambda i: (i, 0))
    return pl.pallas_call(
        body, name=name, grid=(nb,),
        in_specs=[row, pl.BlockSpec((HALO, gw), lambda i: (jnp.minimum((i + 1) * (tl // HALO), l // HALO - 1), 0))],
        out_specs=row, out_shape=jax.ShapeDtypeStruct((l, gw), BF16), compiler_params=_params(("parallel",)),
    )(dpooled, dpooled)


CONV_HALO = 8
QKV_BLK = 4


def _head_sums(x):
    parts = []
    for hd in range(DN_HEADS):
        s = jnp.sum(x[:, hd * DN_HEAD_DIM:(hd + 1) * DN_HEAD_DIM], axis=-1, keepdims=True)
        parts.append(jnp.broadcast_to(s, (x.shape[0], DN_HEAD_DIM)))
    return jnp.concatenate(parts, axis=1)


def _gdn_pre_fwd(proj, proj_ab, conv_w, a_log, dt_bias, name):
    l = proj.shape[0]
    tl = 256
    gw = GROUP_WIDTH

    def body(xq, xk, xv, hq, hk, hv, w_ref, ab_ref, al_ref, dt_ref, qn_ref, kn_ref, v_ref, cq_ref, ck_ref, cv_ref, gb_ref):
        i = pl.program_id(0)
        for p, (x_ref, h_ref, c_ref) in enumerate(((xq, hq, cq_ref), (xk, hk, ck_ref), (xv, hv, cv_ref))):
            ext = jnp.concatenate([jnp.where(i > 0, h_ref[...], 0.0), x_ref[...]], axis=0)
            conv = jnp.zeros((tl, gw), F32)
            for j in range(DN_CONV):
                k = DN_CONV - 1 - j
                shifted = ext if k == 0 else pltpu.roll(ext, k, axis=0)
                conv = conv + shifted[CONV_HALO:, :] * w_ref[j:j + 1, p * gw:(p + 1) * gw]
            c_ref[...] = conv
            s = _silu(conv)
            if p == 2:
                v_ref[...] = s
            else:
                r = lax.rsqrt(_head_sums(s * s) + L2_EPS)
                (qn_ref if p == 0 else kn_ref)[...] = s * r * (DN_HEAD_DIM ** -0.5 if p == 0 else 1.0)
        ab = ab_ref[...]
        lane = lax.broadcasted_iota(jnp.int32, ab.shape, 1)
        g = -jnp.exp(al_ref[...]) * _softplus(ab + dt_ref[...])
        gb_ref[...] = jnp.where(lane < DN_HEADS, g, _sigmoid(ab))

    def xs(b):
        return pl.BlockSpec((tl, gw), lambda i: (i, b))

    def hs(b):
        return pl.BlockSpec((CONV_HALO, gw), lambda i: (jnp.maximum(i * (tl // CONV_HALO) - 1, 0), b))

    row = pl.BlockSpec((tl, gw), lambda i: (i, 0))
    abrow = pl.BlockSpec((tl, AB_PAD), lambda i: (i, 0))
    abvec = pl.BlockSpec((1, AB_PAD), lambda i: (0, 0))
    return pl.pallas_call(
        body, name=name, grid=(l // tl,),
        in_specs=[xs(QKV_BLK), xs(QKV_BLK + 1), xs(QKV_BLK + 2), hs(QKV_BLK), hs(QKV_BLK + 1), hs(QKV_BLK + 2),
                  pl.BlockSpec((DN_CONV, 3 * gw), lambda i: (0, 0)), abrow, abvec, abvec],
        out_specs=[row] * 6 + [abrow],
        out_shape=[jax.ShapeDtypeStruct((l, gw), F32)] * 6 + [jax.ShapeDtypeStruct((l, AB_PAD), F32)],
        compiler_params=_params(("parallel",)),
    )(proj, proj, proj, proj, proj, proj, conv_w, proj_ab, a_log, dt_bias)


def _gdn_pre_bwd(dq, dk, dv, cq, ck, cv, dgb, gb, proj_ab, a_log, dt_bias, name):
    l, gw = cq.shape
    tl = 256

    def body(dq_ref, dk_ref, dv_ref, cq_ref, ck_ref, cv_ref, dgb_ref, gb_ref, ab_ref, al_ref, dt_ref,
             dcq_ref, dck_ref, dcv_ref, dab_ref, dal_ref, ddt_ref):
        @pl.when(pl.program_id(0) == 0)
        def _():
            dal_ref[...] = jnp.zeros_like(dal_ref)
            ddt_ref[...] = jnp.zeros_like(ddt_ref)

        for p, (d_ref, c_ref, o_ref) in enumerate(((dq_ref, cq_ref, dcq_ref), (dk_ref, ck_ref, dck_ref), (dv_ref, cv_ref, dcv_ref))):
            c, d = c_ref[...], d_ref[...]
            if p == 2:
                ds = d
            else:
                s = _silu(c)
                r = lax.rsqrt(_head_sums(s * s) + L2_EPS)
                ds = (DN_HEAD_DIM ** -0.5 if p == 0 else 1.0) * r * (d - s * r * r * _head_sums(d * s))
            o_ref[...] = ds * _silu_grad(c)
        ab, dgb_v, gb_v = ab_ref[...], dgb_ref[...], gb_ref[...]
        lane = lax.broadcasted_iota(jnp.int32, ab.shape, 1)
        is_g = lane < DN_HEADS
        dpre = dgb_v * (-jnp.exp(al_ref[...])) * _sigmoid(ab + dt_ref[...])
        dab_ref[...] = jnp.where(is_g, dpre, dgb_v * gb_v * (1.0 - gb_v)).astype(dab_ref.dtype)
        dal_ref[...] += jnp.sum(jnp.where(is_g, dgb_v * gb_v, 0.0), axis=0, keepdims=True)
        ddt_ref[...] += jnp.sum(jnp.where(is_g, dpre, 0.0), axis=0, keepdims=True)

    row = pl.BlockSpec((tl, gw), lambda i: (i, 0))
    abrow = pl.BlockSpec((tl, AB_PAD), lambda i: (i, 0))
    abvec = pl.BlockSpec((1, AB_PAD), lambda i: (0, 0))
    return pl.pallas_call(
        body, name=name, grid=(l // tl,),
        in_specs=[row] * 6 + [abrow, abrow, abrow, abvec, abvec],
        out_specs=[row, row, row, abrow, abvec, abvec],
        out_shape=[jax.ShapeDtypeStruct((l, gw), F32)] * 3 + [jax.ShapeDtypeStruct((l, AB_PAD), BF16),
                   jax.ShapeDtypeStruct((1, AB_PAD), F32), jax.ShapeDtypeStruct((1, AB_PAD), F32)],
        compiler_params=_params(("arbitrary",)),
    )(dq, dk, dv, cq, ck, cv, dgb, gb, proj_ab, a_log, dt_bias)


def _conv_bwd(dc, proj, col_blk, w_part, name):
    l, gw = dc.shape
    tl = 256
    nb = l // tl

    def body(dc_ref, halo_ref, x_ref, w_ref, dx_ref, dw_ref):
        i = pl.program_id(0)

        @pl.when(i == 0)
        def _():
            dw_ref[...] = jnp.zeros_like(dw_ref)

        ext = jnp.concatenate([dc_ref[...], jnp.where(i < nb - 1, halo_ref[...], 0.0)], axis=0)
        x = x_ref[...]
        dx = jnp.zeros((tl, gw), F32)
        rid = lax.broadcasted_iota(jnp.int32, (8, gw), 0)
        dw = jnp.zeros((8, gw), F32)
        for j in range(DN_CONV):
            k = DN_CONV - 1 - j
            shifted = (ext if k == 0 else pltpu.roll(ext, tl + CONV_HALO - k, axis=0))[:tl, :]
            dx = dx + shifted * w_ref[j:j + 1, :]
            dw = dw + jnp.where(rid == j, jnp.sum(x * shifted, axis=0, keepdims=True), 0.0)
        dx_ref[...] = dx.astype(dx_ref.dtype)
        dw_ref[...] += dw

    row = pl.BlockSpec((tl, gw), lambda i: (i, 0))
    return pl.pallas_call(
        body, name=name, grid=(nb,),
        in_specs=[row, pl.BlockSpec((CONV_HALO, gw), lambda i: (jnp.minimum((i + 1) * (tl // CONV_HALO), l // CONV_HALO - 1), 0)),
                  pl.BlockSpec((tl, gw), lambda i: (i, col_blk)), pl.BlockSpec((DN_CONV, gw), lambda i: (0, 0))],
        out_specs=[row, pl.BlockSpec((8, gw), lambda i: (0, 0))],
        out_shape=[jax.ShapeDtypeStruct((l, gw), BF16), jax.ShapeDtypeStruct((8, gw), F32)],
        compiler_params=_params(("arbitrary",)),
    )(dc, dc, proj, w_part)


def _chunk_terms(q, k, v, grow, gcol, beta):
    c = DN_CHUNK
    ii = lax.broadcasted_iota(jnp.int32, (c, c), 0)
    jj = lax.broadcasted_iota(jnp.int32, (c, c), 1)
    tril, strict = ii >= jj, ii > jj
    gc_col = jnp.sum(jnp.where(tril, grow, 0.0), axis=1, keepdims=True)
    gc_row = jnp.sum(jnp.where(ii <= jj, gcol, 0.0), axis=0, keepdims=True)
    dec = jnp.exp(jnp.where(tril, gc_col - gc_row, -1e30))
    kb, vb = k * beta, v * beta
    kk = _mdot(kb, k, "nt")
    a = jnp.where(strict, kk * dec, 0.0)
    eye = jnp.where(ii == jj, 1.0, 0.0)
    t = eye - a
    p = _dot(a, a, precision=HI)
    for it in range(5):
        t = t + _dot(t, p, precision=HI)
        if it < 4:
            p = _dot(p, p, precision=HI)
    eg = jnp.exp(gc_col)
    gc_last = gc_col[c - 1:c, :]
    kbg = kb * eg
    u = _mdot(t, vb)
    w = _mdot(t, kbg)
    qk0 = _mdot(q, k, "nt")
    qk = jnp.where(tril, qk0 * dec, 0.0)
    qg = q * eg
    e2 = jnp.exp(gc_last - gc_col)
    kt = k * e2
    gl = jnp.exp(gc_last)
    return dict(ii=ii, jj=jj, tril=tril, strict=strict, dec=dec, kb=kb, vb=vb, kk=kk, t=t, eg=eg, kbg=kbg, u=u, w=w,
                qk0=qk0, qk=qk, qg=qg, e2=e2, kt=kt, gl=gl)


def _gdn_specs(n_chunks, reverse):
    c, hd, nh = DN_CHUNK, DN_HEAD_DIM, DN_HEADS
    ch = (lambda n: n_chunks - 1 - n) if reverse else (lambda n: n)
    blk = pl.BlockSpec((c, GROUP_WIDTH), lambda n: (ch(n), 0))
    col = pl.BlockSpec((1, nh, c, 1), lambda n: (ch(n), 0, 0, 0))
    rowv = pl.BlockSpec((1, nh, 1, c), lambda n: (ch(n), 0, 0, 0))
    st = pl.BlockSpec((1, nh, hd, hd), lambda n: (ch(n), 0, 0, 0))
    return blk, col, rowv, st


def _gdn_core_fwd(qn, kn, v, g_row, g_col, b_col, name, comm=None):
    l = qn.shape[0]
    n_chunks = l // DN_CHUNK
    hd, nh = DN_HEAD_DIM, DN_HEADS
    blk, col, rowv, st = _gdn_specs(n_chunks, False)
    cm = _Comm(comm) if comm else None
    nc = cm.n if cm else 0

    def body(q_ref, k_ref, v_ref, gr_ref, gc_ref, bc_ref, *rest):
        c_ins, o_ref, s_ref, c_outs, state, sems = rest[:nc], rest[nc], rest[nc + 1], rest[nc + 2:2 * nc + 2], rest[2 * nc + 2], rest[2 * nc + 3:]

        @pl.when(pl.program_id(0) == 0)
        def _():
            state[...] = jnp.zeros_like(state)
            if cm:
                cm.start(c_ins, c_outs, sems)

        for h in range(nh):
            cols = slice(h * hd, (h + 1) * hd)
            x = _chunk_terms(q_ref[:, cols], k_ref[:, cols], v_ref[:, cols], gr_ref[0, h], gc_ref[0, h], bc_ref[0, h])
            s = state[h]
            s_ref[0, h] = s
            v_new = x["u"] - _mdot(x["w"], s)
            o_ref[:, cols] = _mdot(x["qg"], s) + _mdot(x["qk"], v_new)
            state[h] = s * x["gl"] + _mdot(x["kt"], v_new, "tn")

        if cm:
            @pl.when(pl.program_id(0) == n_chunks - 1)
            def _():
                cm.wait(c_ins, c_outs, sems)

    res = pl.pallas_call(
        body, name=name, grid=(n_chunks,), in_specs=[blk, blk, blk, rowv, col, col] + (cm.in_specs if cm else []),
        out_specs=[blk, st] + (cm.out_specs if cm else []),
        out_shape=[jax.ShapeDtypeStruct((l, GROUP_WIDTH), F32), jax.ShapeDtypeStruct((n_chunks, nh, hd, hd), F32)]
        + (cm.out_shape if cm else []),
        scratch_shapes=[pltpu.VMEM((nh, hd, hd), F32)] + (cm.scratch if cm else []), compiler_params=_params(("arbitrary",)),
    )(qn, kn, v, g_row, g_col, b_col, *(cm.arrays if cm else []))
    return res[0], res[1], res[2:]


def _gdn_core_bwd(do, qn, kn, v, g_row, g_col, b_col, states, name):
    l = qn.shape[0]
    n_chunks = l // DN_CHUNK
    hd, c = DN_HEAD_DIM, DN_CHUNK
    blk, col, rowv, st = _gdn_specs(n_chunks, True)

    def body(do_ref, q_ref, k_ref, v_ref, gr_ref, gc_ref, bc_ref, s_ref, dq_ref, dk_ref, dv_ref, dg_ref, db_ref, dstate):
        @pl.when(pl.program_id(0) == 0)
        def _():
            dstate[...] = jnp.zeros_like(dstate)

        for h in range(DN_HEADS):
            head(h, do_ref, q_ref, k_ref, v_ref, gr_ref, gc_ref, bc_ref, s_ref, dq_ref, dk_ref, dv_ref, dg_ref, db_ref, dstate)

    def head(h, do_ref, q_ref, k_ref, v_ref, gr_ref, gc_ref, bc_ref, s_ref, dq_ref, dk_ref, dv_ref, dg_ref, db_ref, dstate):
        cols = slice(h * hd, (h + 1) * hd)
        q, k, vv, beta = q_ref[:, cols], k_ref[:, cols], v_ref[:, cols], bc_ref[0, h]
        x = _chunk_terms(q, k, vv, gr_ref[0, h], gc_ref[0, h], beta)
        ii, jj, tril, strict = x["ii"], x["jj"], x["tril"], x["strict"]
        s, ds_new, dout = s_ref[0, h], dstate[h], do_ref[:, cols]
        v_new = x["u"] - _mdot(x["w"], s)
        dqg = _mdot(dout, s, "nt")
        dqk = jnp.where(tril, _mdot(dout, v_new, "nt"), 0.0)
        dvn = _mdot(x["qk"], dout, "tn") + _mdot(x["kt"], ds_new)
        dkt = _mdot(v_new, ds_new, "nt")
        dgl = jnp.sum(jnp.sum(ds_new * s, axis=1, keepdims=True), axis=0, keepdims=True)
        dw = -_mdot(dvn, s, "nt")
        dstate[h] = _mdot(x["qg"], dout, "tn") + x["gl"] * ds_new - _mdot(x["w"], dvn, "tn")
        t = x["t"]
        dt = _mdot(dvn, x["vb"], "nt") + _mdot(dw, x["kbg"], "nt")
        dvb = _mdot(t, dvn, "tn")
        dkbg = _mdot(t, dw, "tn")
        tt_dt = _dot(t, dt, "tn", precision=HI)
        da = jnp.where(strict, -_dot(tt_dt, t, "nt", precision=HI), 0.0)
        dkk = da * x["dec"]
        dqk0 = dqk * x["dec"]
        e = (da * x["kk"] + dqk * x["qk0"]) * x["dec"]
        dkb = _mdot(dkk, k) + dkbg * x["eg"]
        dk = _mdot(dkk, x["kb"], "tn") + _mdot(dqk0, q, "tn") + dkt * x["e2"] + dkb * beta
        dq = _mdot(dqk0, k) + dqg * x["eg"]
        s_kt = jnp.sum(dkt * x["kt"], axis=1, keepdims=True)
        dgc_c = (jnp.sum(e, axis=1, keepdims=True) + jnp.sum(dqg * x["qg"], axis=1, keepdims=True) - s_kt
                 + jnp.sum(dkbg * x["kbg"], axis=1, keepdims=True))
        dgc_last = jnp.sum(s_kt, axis=0, keepdims=True) + dgl * x["gl"]
        rid = lax.broadcasted_iota(jnp.int32, (c, 1), 0)
        dgc_c = dgc_c + jnp.where(rid == c - 1, dgc_last, 0.0)
        dgc_r = jnp.sum(jnp.where(ii == jj, dgc_c, 0.0), axis=0, keepdims=True) - jnp.sum(e, axis=0, keepdims=True)
        dg_ref[0, h] = jnp.sum(jnp.where(jj >= ii, dgc_r, 0.0), axis=1, keepdims=True)
        db_ref[0, h] = jnp.sum(dkb * k, axis=1, keepdims=True) + jnp.sum(dvb * vv, axis=1, keepdims=True)
        dq_ref[:, cols] = dq
        dk_ref[:, cols] = dk
        dv_ref[:, cols] = dvb * beta

    return pl.pallas_call(
        body, name=name, grid=(n_chunks,), in_specs=[blk, blk, blk, blk, rowv, col, col, st],
        out_specs=[blk, blk, blk, col, col],
        out_shape=[jax.ShapeDtypeStruct((l, GROUP_WIDTH), F32)] * 3 + [jax.ShapeDtypeStruct((n_chunks, DN_HEADS, c, 1), F32)] * 2,
        scratch_shapes=[pltpu.VMEM((DN_HEADS, hd, hd), F32)], compiler_params=_params(("arbitrary",)),
    )(do, qn, kn, v, g_row, g_col, b_col, states)


def _gdn_post_fwd(o, proj, norm_g4, name):
    l, gw = o.shape
    tl = min(512, l)

    def body(o_ref, gate_ref, g_ref, out_ref):
        ov = o_ref[...]
        r = lax.rsqrt(_head_sums(ov * ov) * (1.0 / DN_HEAD_DIM) + RMS_EPS)
        out_ref[...] = (ov * r * g_ref[...] * _silu(gate_ref[...])).astype(out_ref.dtype)

    row = pl.BlockSpec((tl, gw), lambda i: (i, 0))
    return pl.pallas_call(
        body, name=name, grid=(l // tl,),
        in_specs=[row, pl.BlockSpec((tl, gw), lambda i: (i, 7)), pl.BlockSpec((1, gw), lambda i: (0, 0))],
        out_specs=row, out_shape=jax.ShapeDtypeStruct((l, gw), BF16), compiler_params=_params(("parallel",)),
    )(o, proj, norm_g4)


def _gdn_post_bwd(dmixed, o, proj, norm_g4, name):
    l, gw = o.shape
    tl = min(512, l)

    def body(d_ref, o_ref, gate_ref, g_ref, do_ref, dgate_ref, dng_ref):
        @pl.when(pl.program_id(0) == 0)
        def _():
            dng_ref[...] = jnp.zeros_like(dng_ref)

        ov, gate, d = o_ref[...], gate_ref[...], d_ref[...]
        r = lax.rsqrt(_head_sums(ov * ov) * (1.0 / DN_HEAD_DIM) + RMS_EPS)
        oh = ov * r
        sg = _silu(gate)
        dgate_ref[...] = (d * oh * g_ref[...] * _silu_grad(gate)).astype(dgate_ref.dtype)
        dng_ref[...] += jnp.sum(d * sg * oh, axis=0, keepdims=True)
        doh = d * g_ref[...] * sg
        do_ref[...] = r * (doh - oh * _head_sums(doh * oh) * (1.0 / DN_HEAD_DIM))

    row = pl.BlockSpec((tl, gw), lambda i: (i, 0))
    vec = pl.BlockSpec((1, gw), lambda i: (0, 0))
    return pl.pallas_call(
        body, name=name, grid=(l // tl,),
        in_specs=[pl.BlockSpec((tl, gw), lambda i: (i, 3)), row, pl.BlockSpec((tl, gw), lambda i: (i, 7)), vec],
        out_specs=[row, row, vec],
        out_shape=[jax.ShapeDtypeStruct((l, gw), F32), jax.ShapeDtypeStruct((l, gw), BF16), jax.ShapeDtypeStruct((1, gw), F32)],
        compiler_params=_params(("arbitrary",)),
    )(dmixed, o, proj, norm_g4)


def _gdn_vectors(gb, n_chunks):
    def chunked(t):
        return jnp.transpose(t.reshape(n_chunks, DN_CHUNK, DN_HEADS), (0, 2, 1))
    g, beta = chunked(gb[:, :DN_HEADS]), chunked(gb[:, DN_HEADS:2 * DN_HEADS])
    return g[:, :, None, :], g[..., None], beta[..., None]


def _run(hosts, name, fn):
    h = hosts.get(name)
    if h is None:
        return fn(None)
    res, outs = fn(h[0]())
    h[1](outs)
    return res


def _layer_fwd(x, w, li, hosts):
    l = x.shape[0]
    nm = f"l{li}_"
    proj = _run(hosts, nm + "proj", lambda ops: _matmul(
        x, w["w_main"], mode="nn", tm=1024, tn=1024, tk=512, out_dtype=F32, name=nm + "proj", comm=ops))
    proj_ab = _matmul(x, w["w_ab"], mode="nn", tm=1024, tn=AB_PAD, tk=2048, out_dtype=F32, name=nm + "proj_ab")
    bu = _matmul(proj, w["s5_bmat"], mode="nn", tm=1024, tn=1024, tk=GROUP_WIDTH, out_dtype=F32, name=nm + "s5_bu",
                 a_cols=GROUP_WIDTH)
    hs = _s5_scan(bu, w["s5_lam"], reverse=False, name=nm + "s5_scan")
    y = _matmul(hs, w["s5_cmat"], mode="nn", tm=1024, tn=GROUP_WIDTH, tk=1024, out_dtype=F32, name=nm + "s5_y",
                extras=[(proj, (None, GROUP_WIDTH), lambda i, j: (i, 0)), (w["s5_d"], (1, GROUP_WIDTH), lambda i, j: (0, 0))],
                epi=lambda acc, u, d: acc + d * u)
    m_s5 = _s5_glu_fwd(y, w["s5_glu_w"], w["s5_glu_b"], nm + "s5_glu")
    m_sgu = _sgu_fwd(proj, w["sgu_norm_g"], w["sgu_norm_b"], w["sgu_wm"], w["sgu_bfull"], nm + "sgu")
    m_pool, pooled = _pool_fwd(proj, w["pool_w"], w["pool_scale"], nm + "pool")
    qn, kn, v, cq, ck, cv, gb = _gdn_pre_fwd(proj, proj_ab, w["dn_conv_w"], w["dn_a_log"], w["dn_dt_bias"], nm + "gdn_pre")
    g_row, g_col, b_col = _gdn_vectors(gb, l // DN_CHUNK)
    def core(ops):
        o_, st_, outs = _gdn_core_fwd(qn, kn, v, g_row, g_col, b_col, nm + "gdn_core", comm=ops)
        return ((o_, st_), outs) if ops else (o_, st_)
    o, states = _run(hosts, nm + "gdn_core", core)
    m_dn = _gdn_post_fwd(o, proj, w["dn_norm_g4"], nm + "gdn_post")
    mixed = jnp.concatenate([m_s5, m_sgu, m_pool, m_dn], axis=1)
    y1 = _matmul(mixed, w["w_out"], mode="nn", tm=1024, tn=1024, tk=512, out_dtype=F32, name=nm + "out_proj")
    h1, x1 = _ln_fwd(x, y1, w["ln1_g"], w["ln1_b"], nm + "ln1")
    r = _run(hosts, nm + "up", lambda ops: _matmul(
        x1, w["w_up"], mode="nn", tm=1024, tn=1024, tk=512, out_dtype=BF16, name=nm + "up",
        epi=lambda acc: jnp.maximum(acc, 0.0), b_slab=w["w_up"].shape[2], comm=ops))
    y2 = _run(hosts, nm + "down", lambda ops: _matmul(
        r, w["w_down"], mode="nn", tm=1024, tn=1024, tk=512, out_dtype=F32, name=nm + "down", a_fn=lambda a: a * a, comm=ops))
    h2, x2 = _ln_fwd(x1, y2, w["ln2_g"], w["ln2_b"], nm + "ln2")
    saved = dict(x=x, proj=proj, proj_ab=proj_ab, hs=hs, y=y, pooled=pooled, qn=qn, kn=kn, v=v, cq=cq, ck=ck, cv=cv, gb=gb,
                 g_row=g_row, g_col=g_col, b_col=b_col, o=o, states=states, mixed=mixed, h1=h1, x1=x1, r=r, h2=h2)
    return x2, saved


def _layer_bwd(dx2, s, w, li, hosts, g):
    nm = f"l{li}b_"
    l = dx2.shape[0]
    gw = GROUP_WIDTH
    wire = MXU_DTYPE
    dh2, g["ln2_g"], g["ln2_b"] = _ln_bwd(dx2, s["h2"], w["ln2_g"], nm + "ln2")
    g["w_down"] = _run(hosts, nm + "dw_down", lambda ops: _matmul(
        s["r"], dh2, mode="tn", tm=1024, tn=1024, tk=512, out_dtype=wire, name=nm + "dw_down", a_fn=lambda a: a * a,
        comm=ops)).reshape(N_DEV, D_FF // N_DEV, D_MODEL)
    dpre = _run(hosts, nm + "dpre", lambda ops: _matmul(
        dh2, w["w_down"], mode="nt", tm=1024, tn=1024, tk=512, out_dtype=BF16, name=nm + "dpre",
        extras=[(s["r"], (None, None), lambda i, j: (i, j))], epi=lambda acc, r: acc * 2.0 * r.astype(F32), comm=ops))
    g["w_up"] = _matmul(s["x1"], dpre, mode="tn", tm=1024, tn=1024, tk=512, out_dtype=wire, name=nm + "dw_up",
                        out_slab=D_FF // N_DEV)
    dx1 = _run(hosts, nm + "dx1", lambda ops: _matmul(
        dpre, w["w_up"], mode="nt", tm=1024, tn=1024, tk=512, out_dtype=F32, name=nm + "dx1",
        extras=[(dh2, (None, None), lambda i, j: (i, j))], epi=lambda acc, e: acc + ALPHA * e,
        b_slab=w["w_up"].shape[2], comm=ops))
    dh1, g["ln1_g"], g["ln1_b"] = _ln_bwd(dx1, s["h1"], w["ln1_g"], nm + "ln1")
    g["w_out"] = _matmul(s["mixed"], dh1, mode="tn", tm=1024, tn=1024, tk=512, out_dtype=wire,
                         name=nm + "dw_out").reshape(N_DEV, D_MODEL // N_DEV, D_MODEL)
    dmixed = _run(hosts, nm + "dmixed", lambda ops: _matmul(
        dh1, w["w_out"], mode="nt", tm=1024, tn=1024, tk=512, out_dtype=F32, name=nm + "dmixed", comm=ops))
    proj, proj_ab = s["proj"], s["proj_ab"]
    dy, dz, yg, g["s5_glu_b"], g["s5_d"] = _s5_glu_bwd(dmixed, s["y"], proj, w["s5_glu_w"], w["s5_glu_b"], nm + "s5_glu")
    g["s5_glu_w"] = _matmul(yg, dz, mode="tn", tm=gw, tn=gw, tk=1024, out_dtype=wire,
                            name=nm + "dw_glu").reshape(N_DEV, gw // N_DEV, gw)
    dhs = _matmul(dy, w["s5_cmat"], mode="nt", tm=1024, tn=1024, tk=gw, out_dtype=F32, name=nm + "s5_dh")
    adj, g["s5_lam"] = _s5_scan(dhs, w["s5_lam_conj"], reverse=True, h=s["hs"], name=nm + "s5_scan")
    g["s5_cmat"] = _matmul(s["hs"], dy, mode="tn", tm=1024, tn=gw, tk=1024, out_dtype=F32, name=nm + "s5_dc")
    g["s5_bmat"] = _matmul(proj, adj, mode="tn", tm=gw, tn=1024, tk=1024, out_dtype=F32, name=nm + "s5_db", a_cols=gw)
    du_s5 = _matmul(adj, w["s5_bmat"], mode="nt", tm=1024, tn=gw, tk=1024, out_dtype=BF16, name=nm + "s5_du",
                    extras=[(dy, (None, gw), lambda i, j: (i, 0)), (w["s5_d"], (1, gw), lambda i, j: (0, 0))],
                    epi=lambda acc, dyv, d: acc + d * dyv)
    dzu, dzv, g["sgu_w"], g["sgu_bfull"], g["sgu_norm_g"], g["sgu_norm_b"] = _sgu_bwd(
        dmixed, proj, w["sgu_norm_g"], w["sgu_norm_b"], w["sgu_wm"], w["sgu_bfull"], nm + "sgu")
    dpooled, g["pool_w"], g["pool_scale"] = _pool_bwd_map(dmixed, s["pooled"], w["pool_w"], w["pool_scale"], nm + "pool_map")
    dp = _pool_bwd_window(dpooled, nm + "pool_win")
    do, dgate, g["dn_norm_g4"] = _gdn_post_bwd(dmixed, s["o"], proj, w["dn_norm_g4"], nm + "gdn_post")
    dq, dk, dv, dg_col, db_col = _gdn_core_bwd(do, s["qn"], s["kn"], s["v"], s["g_row"], s["g_col"], s["b_col"], s["states"], nm + "gdn_core")
    def uncol(t):
        return jnp.transpose(t[..., 0], (0, 2, 1)).reshape(l, DN_HEADS)
    dgb = jnp.concatenate([uncol(dg_col), uncol(db_col), jnp.zeros((l, AB_PAD - 2 * DN_HEADS), F32)], axis=1)
    dcq, dck, dcv, dab, g["dn_a_log"], g["dn_dt_bias"] = _gdn_pre_bwd(
        dq, dk, dv, s["cq"], s["ck"], s["cv"], dgb, s["gb"], proj_ab, w["dn_a_log"], w["dn_dt_bias"], nm + "gdn_pre")
    dxs, dws = [], []
    for p, dc in enumerate((dcq, dck, dcv)):
        dxp, dwp = _conv_bwd(dc, proj, QKV_BLK + p, w["dn_conv_w"][:, p * gw:(p + 1) * gw], nm + f"conv{p}")
        dxs.append(dxp)
        dws.append(dwp[:DN_CONV])
    dconv = jnp.concatenate(dws, axis=1)
    g["dn_conv_w"] = jnp.transpose(dconv.reshape(DN_CONV, N_DEV, 3 * gw // N_DEV), (1, 0, 2))
    dproj = jnp.concatenate([du_s5, dzu, dzv, dp] + dxs + [dgate], axis=1)
    x = s["x"]
    dw_main = _matmul(x, dproj, mode="tn", tm=1024, tn=1024, tk=512, out_dtype=wire, name=nm + "dw_main")
    dw_ab = _matmul(x, dab, mode="tn", tm=1024, tn=AB_PAD, tk=1024, out_dtype=wire, name=nm + "dw_ab")
    dw_in = jnp.concatenate([dw_main, dw_ab[:, :2 * DN_HEADS]], axis=1)
    g["w_in"] = jnp.transpose(dw_in.reshape(D_MODEL, N_DEV, dw_in.shape[1] // N_DEV), (1, 0, 2))
    dx_ab = _matmul(dab, w["w_ab"], mode="nt", tm=1024, tn=1024, tk=AB_PAD, out_dtype=F32, name=nm + "dx_ab",
                    extras=[(dh1, (None, None), lambda i, j: (i, j))], epi=lambda acc, e: acc + ALPHA * e)
    dx = _matmul(dproj, w["w_main"], mode="nt", tm=1024, tn=1024, tk=512, out_dtype=F32, name=nm + "dx",
                 extras=[(dx_ab, (None, None), lambda i, j: (i, j))], epi=lambda acc, e: acc + e)
    return dx


SMALL = ("s5_lambda_re", "s5_lambda_im", "s5_log_step", "s5_b_re", "s5_b_im", "s5_c_re", "s5_c_im", "s5_d", "s5_glu_b",
         "sgu_norm_g", "sgu_norm_b", "sgu_w", "sgu_b", "pool_w", "pool_scale", "dn_a_log", "dn_dt_bias", "dn_norm_g",
         "ln1_g", "ln1_b", "ln2_g", "ln2_b")
SHARDED = ("w_in", "s5_glu_w", "dn_conv_w", "w_out", "w_up", "w_down")


def _pad_lanes(v, width=AB_PAD):
    return jnp.pad(v.reshape(1, -1), ((0, 0), (0, width - v.size)))


def _prep_small(p):
    mx = MXU_DTYPE
    lbr, lbi, bbr, bbi = _s5_discretize(p["s5_lambda_re"], p["s5_lambda_im"], p["s5_log_step"], p["s5_b_re"], p["s5_b_im"])
    bmat, cmat = _s5_dense(bbr, bbi, p["s5_c_re"], p["s5_c_im"])
    causal = jnp.tril(jnp.ones((SGU_CHUNK, SGU_CHUNK), F32))
    return dict(
        s5_bmat=bmat.astype(mx), s5_cmat=cmat.astype(mx),
        s5_lam=jnp.concatenate([lbr.reshape(1, -1), lbi.reshape(1, -1)], axis=1),
        s5_lam_conj=jnp.concatenate([lbr.reshape(1, -1), -lbi.reshape(1, -1)], axis=1),
        s5_d=p["s5_d"].reshape(1, -1), s5_glu_b=p["s5_glu_b"].reshape(1, -1),
        sgu_norm_g=p["sgu_norm_g"].reshape(1, -1), sgu_norm_b=p["sgu_norm_b"].reshape(1, -1),
        sgu_wm=(p["sgu_w"] * causal).astype(mx), sgu_bfull=jnp.repeat(p["sgu_b"].T, GROUP_WIDTH // SGU_HEADS, axis=1),
        pool_w=p["pool_w"].astype(mx), pool_scale=p["pool_scale"].reshape(1, -1),
        dn_a_log=_pad_lanes(p["dn_a_log"]), dn_dt_bias=_pad_lanes(p["dn_dt_bias"]),
        dn_norm_g4=jnp.tile(p["dn_norm_g"].reshape(1, -1), (1, DN_HEADS)),
        ln1_g=p["ln1_g"].reshape(1, -1), ln1_b=p["ln1_b"].reshape(1, -1),
        ln2_g=p["ln2_g"].reshape(1, -1), ln2_b=p["ln2_b"].reshape(1, -1),
    )


def _weight_views(name, t):
    if name == "w_in":
        w_in = jnp.transpose(t, (1, 0, 2)).reshape(t.shape[1], N_DEV * t.shape[2])
        pad = AB_PAD - (w_in.shape[1] - MAIN_COLS)
        return dict(w_main=w_in[:, :MAIN_COLS], w_ab=jnp.pad(w_in[:, MAIN_COLS:], ((0, 0), (0, pad))))
    if name == "dn_conv_w":
        return dict(dn_conv_w=jnp.transpose(t, (1, 0, 2)).reshape(t.shape[1], N_DEV * t.shape[2]))
    if name == "w_up":
        return dict(w_up=t)
    return {name: t.reshape(N_DEV * t.shape[1], t.shape[2])}


def _unprep_grads(g, p):
    causal = jnp.tril(jnp.ones((SGU_CHUNK, SGU_CHUNK), F32))
    dbbr, dbbi = _s5_undense_b(g["s5_bmat"])
    dc_re, dc_im = _s5_undense_c(g["s5_cmat"])
    dlbr, dlbi = g["s5_lam"][0, :S5_NS].reshape(S5_GROUPS, S5_STATE), g["s5_lam"][0, S5_NS:].reshape(S5_GROUPS, S5_STATE)
    _, vjp = jax.vjp(_s5_discretize, p["s5_lambda_re"], p["s5_lambda_im"], p["s5_log_step"], p["s5_b_re"], p["s5_b_im"])
    d_lre, d_lim, d_step, d_bre, d_bim = vjp((dlbr, dlbi, dbbr, dbbi))
    hd = GROUP_WIDTH // SGU_HEADS
    return dict(
        s5_lambda_re=d_lre, s5_lambda_im=d_lim, s5_log_step=d_step, s5_b_re=d_bre, s5_b_im=d_bim, s5_c_re=dc_re, s5_c_im=dc_im,
        s5_d=g["s5_d"].reshape(S5_GROUPS, S5_CH), s5_glu_b=g["s5_glu_b"].reshape(-1),
        sgu_norm_g=g["sgu_norm_g"].reshape(-1), sgu_norm_b=g["sgu_norm_b"].reshape(-1), sgu_w=g["sgu_w"] * causal,
        sgu_b=jnp.sum(g["sgu_bfull"].reshape(SGU_CHUNK, SGU_HEADS, hd), axis=2).T,
        pool_w=g["pool_w"], pool_scale=g["pool_scale"].reshape(-1),
        dn_a_log=g["dn_a_log"][0, :DN_HEADS], dn_dt_bias=g["dn_dt_bias"][0, :DN_HEADS],
        dn_norm_g=jnp.sum(g["dn_norm_g4"].reshape(DN_HEADS, DN_HEAD_DIM), axis=0),
        ln1_g=g["ln1_g"].reshape(-1), ln1_b=g["ln1_b"].reshape(-1), ln2_g=g["ln2_g"].reshape(-1), ln2_b=g["ln2_b"].reshape(-1),
    )


def _local_step(x, target, ops, small, fwd_hosts, bwd_hosts, grads):
    saved = []
    h = x
    for i in range(DEPTH):
        h, s = _layer_fwd(h, ops[i], i, fwd_hosts)
        saved.append(s)
    loss, dh = _loss_head(h, target)
    for i in reversed(range(DEPTH)):
        dh = _layer_bwd(dh, saved[i], ops[i], i, bwd_hosts, grads[i])
        grads[i]["small"] = _unprep_grads(grads[i], small[i])
    return loss, dh


def _adamw(w, gparts, m, v, name):
    rr, c = w.shape
    ng = len(gparts)
    r = rr // ng
    lanes = -(-c // 128) * 128
    tr = r
    while tr * lanes * 4 * N_DEV > (4 << 20) and tr % 16 == 0:
        tr //= 2
    nb = r // tr

    def body(w_ref, *rest):
        g_refs, (m_ref, v_ref, go_ref, d_ref, mo_ref, vo_ref) = rest[:ng], rest[ng:]
        layer = pl.program_id(0)
        g = jnp.zeros(m_ref.shape, F32)
        for li in range(ng):
            gl = g_refs[li][0].astype(F32)
            for s in range(1, N_DEV):
                gl = gl + g_refs[li][s].astype(F32)
            g = jnp.where(layer == li, gl, g)
        mn = ADAM_B1 * m_ref[...] + (1.0 - ADAM_B1) * g
        vn = ADAM_B2 * v_ref[...] + (1.0 - ADAM_B2) * g * g
        m_hat = mn / (1.0 - ADAM_B1 ** ADAM_STEP)
        v_hat = vn / (1.0 - ADAM_B2 ** ADAM_STEP)
        go_ref[...] = g
        d_ref[...] = -ADAM_LR * (m_hat / (jnp.sqrt(v_hat) + ADAM_EPS) + ADAM_WD * w_ref[...])
        mo_ref[...] = mn
        vo_ref[...] = vn

    row = pl.BlockSpec((tr, c), lambda li, i: (li * nb + i, 0))
    part_specs = [pl.BlockSpec((N_DEV, tr, c), functools.partial(lambda li, i, k: (0, jnp.where(li == k, i, 0), 0), k=k))
                  for k in range(ng)]
    return pl.pallas_call(
        body, name=name, grid=(ng, nb), in_specs=[row] + part_specs + [row, row],
        out_specs=[row] * 4, out_shape=[jax.ShapeDtypeStruct((rr, c), F32)] * 4, compiler_params=_params(("arbitrary", "arbitrary")),
    )(w, *gparts, m, v)


PACK_LANES = 128
PACK_ROWS = 4096


def _pack(vals):
    rows = []
    for t in vals:
        flat = t.reshape(-1)
        n_rows = -(-flat.size // PACK_LANES)
        rows.append(jnp.pad(flat, (0, n_rows * PACK_LANES - flat.size)).reshape(n_rows, PACK_LANES))
    used = sum(r.shape[0] for r in rows)
    assert used <= PACK_ROWS, used
    return jnp.concatenate(rows + [jnp.zeros((PACK_ROWS - used, PACK_LANES), F32)], axis=0)


def _unpack(packed, like):
    out, off = [], 0
    for t in like:
        n_rows = -(-t.size // PACK_LANES)
        out.append(packed[off:off + n_rows].reshape(-1)[:t.size].reshape(t.shape))
        off += n_rows
    return out


def kernel(x, w_in, s5_lambda_re, s5_lambda_im, s5_log_step, s5_b_re, s5_b_im, s5_c_re, s5_c_im, s5_d, s5_glu_w, s5_glu_b, sgu_norm_g, sgu_norm_b, sgu_w, sgu_b, pool_w, pool_scale, dn_conv_w, dn_a_log, dn_dt_bias, dn_norm_g, w_out, ln1_g, ln1_b, w_up, w_down, ln2_g, ln2_b, loss_target, m_w_in, m_s5_lambda_re, m_s5_lambda_im, m_s5_log_step, m_s5_b_re, m_s5_b_im, m_s5_c_re, m_s5_c_im, m_s5_d, m_s5_glu_w, m_s5_glu_b, m_sgu_norm_g, m_sgu_norm_b, m_sgu_w, m_sgu_b, m_pool_w, m_pool_scale, m_dn_conv_w, m_dn_a_log, m_dn_dt_bias, m_dn_norm_g, m_w_out, m_ln1_g, m_ln1_b, m_w_up, m_w_down, m_ln2_g, m_ln2_b, v_w_in, v_s5_lambda_re, v_s5_lambda_im, v_s5_log_step, v_s5_b_re, v_s5_b_im, v_s5_c_re, v_s5_c_im, v_s5_d, v_s5_glu_w, v_s5_glu_b, v_sgu_norm_g, v_sgu_norm_b, v_sgu_w, v_sgu_b, v_pool_w, v_pool_scale, v_dn_conv_w, v_dn_a_log, v_dn_dt_bias, v_dn_norm_g, v_w_out, v_ln1_g, v_ln1_b, v_w_up, v_w_down, v_ln2_g, v_ln2_b):
    names = ("w_in", "s5_lambda_re", "s5_lambda_im", "s5_log_step", "s5_b_re", "s5_b_im", "s5_c_re", "s5_c_im", "s5_d", "s5_glu_w",
             "s5_glu_b", "sgu_norm_g", "sgu_norm_b", "sgu_w", "sgu_b", "pool_w", "pool_scale", "dn_conv_w", "dn_a_log", "dn_dt_bias",
             "dn_norm_g", "w_out", "ln1_g", "ln1_b", "w_up", "w_down", "ln2_g", "ln2_b")
    env = locals()
    w = {n: env[n] for n in names}
    m = {n: env["m_" + n] for n in names}
    v = {n: env["v_" + n] for n in names}

    wire = [{n: (w[n][i] if n == "dn_conv_w" else w[n][i].astype(MXU_DTYPE)) for n in SHARDED} for i in range(DEPTH)]
    small = [{n: w[n][i] for n in SMALL} for i in range(DEPTH)]
    ops = [_prep_small(small[i]) for i in range(DEPTH)]
    grads = [{} for _ in range(DEPTH)]
    recv = [{} for _ in range(DEPTH)]
    first = ("w_in", "s5_glu_w", "dn_conv_w", "w_out")

    def gather(layer, group):
        def take(outs):
            for n, t in zip(group, outs):
                ops[layer].update(_weight_views(n, t))
        return (lambda: [(wire[layer][n], False) for n in group]), take

    def scatter(layer, group, with_small=False):
        def make():
            sends = [(grads[layer][n], True) for n in group]
            if with_small:
                sends.append((_pack([grads[layer]["small"][n] for n in SMALL]), False))
            return sends
        def take(outs):
            recv[layer].update(dict(zip(group + (("small",) if with_small else ()), outs)))
        return make, take

    make, take = gather(0, first)
    take(_exchange(make(), "gather_first"))
    fwd_hosts = {"l0_proj": gather(0, ("w_up",)), "l0_gdn_core": gather(0, ("w_down",)), "l0_up": gather(1, first),
                 "l0_down": gather(1, ("w_up",)), "l1_gdn_core": gather(1, ("w_down",))}
    late = ("w_in", "s5_glu_w", "dn_conv_w")
    bwd_hosts = {"l1b_dpre": scatter(1, ("w_down",)), "l1b_dx1": scatter(1, ("w_up",)), "l1b_dmixed": scatter(1, ("w_out",)),
                 "l0b_dw_down": scatter(1, late, with_small=True),
                 "l0b_dpre": scatter(0, ("w_down",)), "l0b_dx1": scatter(0, ("w_up",)), "l0b_dmixed": scatter(0, ("w_out",))}
    loss, grad_x = _local_step(x[0], loss_target[0], ops, small, fwd_hosts, bwd_hosts, grads)
    make, take = scatter(0, late, with_small=True)
    take(_exchange(make(), "scatter_last"))

    g_out, d_out, m_out, v_out = {}, {}, {}, {}
    for n in SHARDED:
        shp = w[n].shape
        two_d = (shp[0] * shp[1], shp[2])
        parts = [recv[i][n] for i in range(DEPTH)]
        if shp[1] % 8:
            parts = [jnp.concatenate(parts, axis=1)]
        res = _adamw(w[n].reshape(two_d), parts, m[n].reshape(two_d), v[n].reshape(two_d), "adamw_" + n)
        g_out[n], d_out[n], m_out[n], v_out[n] = (t.reshape(shp) for t in res)
    def packed(src):
        return jnp.concatenate([_pack([src[n][i] for n in SMALL]) for i in range(DEPTH)], axis=0)
    res = _adamw(packed(w), [recv[i]["small"] for i in range(DEPTH)], packed(m), packed(v), "adamw_small")
    like = [w[n][0] for n in SMALL]
    for dst, pk in zip((g_out, d_out, m_out, v_out), res):
        per_layer = [_unpack(pk[i * PACK_ROWS:(i + 1) * PACK_ROWS], like) for i in range(DEPTH)]
        dst.update({n: jnp.stack([per_layer[i][k] for i in range(DEPTH)]) for k, n in enumerate(SMALL)})

    total = lax.psum(loss[0, 0], MESH_AXES)
    return (total, grad_x[None], *[g_out[n] for n in names], *[d_out[n] for n in names],
            *[m_out[n] for n in names], *[v_out[n] for n in names])
```

```python
import functools
import math

import jax
import jax.numpy as jnp
from jax import lax
from jax.experimental import pallas as pl
from jax.experimental.pallas import tpu as pltpu

F32 = jnp.float32
BF16 = jnp.bfloat16
MXU_DTYPE = jnp.bfloat16
HI = lax.Precision.HIGHEST

N_DEV = 8
D_MODEL = 2048
DEPTH = 2
GROUP_WIDTH = 512
S5_GROUPS, S5_CH, S5_STATE = 32, 16, 64
S5_NS = S5_GROUPS * S5_STATE
SGU_CHUNK, SGU_HEADS = 128, 8
POOL_WINDOWS = (2, 4, 8, 16)
DN_HEADS, DN_HEAD_DIM, DN_CONV, DN_CHUNK = 4, 128, 4, 64
D_FF = 4 * D_MODEL
LN_EPS, RMS_EPS, L2_EPS = 1e-5, 1e-6, 1e-6
ALPHA = (2 * DEPTH) ** 0.25
MAIN_COLS = 4096
AB_PAD = 128
ADAM_LR, ADAM_B1, ADAM_B2, ADAM_EPS, ADAM_WD, ADAM_STEP = 0.001, 0.9, 0.999, 1e-08, 0.01, 10
VMEM_LIMIT = 56 * 1024 * 1024
C_GELU = math.sqrt(2.0 / math.pi)


def _params(sem=None):
    return pltpu.CompilerParams(dimension_semantics=sem, vmem_limit_bytes=VMEM_LIMIT)


def _gelu(x):
    return 0.5 * x * (1.0 + jnp.tanh(C_GELU * (x + 0.044715 * x * x * x)))


def _gelu_grad(x):
    t = jnp.tanh(C_GELU * (x + 0.044715 * x * x * x))
    return 0.5 * (1.0 + t) + 0.5 * x * (1.0 - t * t) * C_GELU * (1.0 + 3.0 * 0.044715 * x * x)


def _sigmoid(x):
    return 1.0 / (1.0 + jnp.exp(-x))


def _silu(x):
    return x * _sigmoid(x)


def _silu_grad(x):
    s = _sigmoid(x)
    return s * (1.0 + x * (1.0 - s))


def _softplus(x):
    z = jnp.exp(-jnp.abs(x))
    small = z * (1.0 - z * (0.5 - z * (1.0 / 3.0)))
    return jnp.maximum(x, 0.0) + jnp.where(z < 1e-2, small, jnp.log(1.0 + z))


def _mx(x):
    return x.astype(MXU_DTYPE)


def _dot(a, b, dims="nn", precision=None):
    cd = {"nn": ((1,), (0,)), "nt": ((1,), (1,)), "tn": ((0,), (0,))}[dims]
    return lax.dot_general(a, b, (cd, ((), ())), preferred_element_type=F32, precision=precision)


def _mdot(a, b, dims="nn"):
    return _dot(_mx(a), _mx(b), dims)


MESH_AXES = ("x", "y", "c")
OFFSETS = [(dx, dy, dc) for dx in (0, 1) for dy in (0, 1) for dc in (0, 1)][1:]


def _me_and_peers():
    x, y, c = (lax.axis_index(a) for a in MESH_AXES)
    def flip(v, d):
        return 1 - v if d else v
    peers = [(flip(x, dx), flip(y, dy), flip(c, dc)) for dx, dy, dc in OFFSETS]
    def idx(p):
        return 4 * p[0] + 2 * p[1] + p[2]
    return idx((x, y, c)), peers, [idx(p) for p in peers]


class _Comm:
    def __init__(self, ops):
        self.arrays = [a for a, _ in ops]
        self.scatter = [s for _, s in ops]
        self.n = n = len(ops)
        hbm = pl.BlockSpec(memory_space=pltpu.HBM)
        self.in_specs, self.out_specs = [hbm] * n, [hbm] * n
        self.out_shape = [jax.ShapeDtypeStruct(a.shape if s else (N_DEV,) + a.shape, a.dtype) for a, s in ops]
        npeer = len(OFFSETS)
        self.scratch = [pltpu.SemaphoreType.DMA((n, npeer)), pltpu.SemaphoreType.DMA((n, npeer)), pltpu.SemaphoreType.DMA((n,))]

    def _copies(self, ins, outs, sems, arrivals):
        send_sems, recv_sems, local_sems = sems
        me, peers, peer_idx = _me_and_peers()
        local, sends, recvs = [], [], []
        for k in range(self.n):
            sc = self.scatter[k]
            local.append(pltpu.make_async_copy(ins[k].at[me] if sc else ins[k], outs[k].at[me], local_sems.at[k]))
            for d in range(len(OFFSETS)):
                src = ins[k].at[peer_idx[d]] if sc else ins[k]
                common = dict(send_sem=send_sems.at[k, d], recv_sem=recv_sems.at[k, d], device_id=peers[d],
                              device_id_type=pl.DeviceIdType.MESH)
                sends.append(pltpu.make_async_remote_copy(src_ref=src, dst_ref=outs[k].at[me], **common))
                if arrivals:
                    recvs.append(pltpu.make_async_remote_copy(src_ref=src, dst_ref=outs[k].at[peer_idx[d]], **common))
        return local, sends, recvs

    def start(self, ins, outs, sems):
        local, sends, _ = self._copies(ins, outs, sems, False)
        for cp in local + sends:
            cp.start()

    def wait(self, ins, outs, sems):
        local, sends, recvs = self._copies(ins, outs, sems, True)
        for cp in recvs:
            cp.wait_recv()
        for cp in sends:
            cp.wait_send()
        for cp in local:
            cp.wait()


def _exchange(ops, name):
    cm = _Comm(ops)

    def body(*refs):
        ins, outs, sems = refs[:cm.n], refs[cm.n:2 * cm.n], refs[2 * cm.n:]
        cm.start(ins, outs, sems)
        cm.wait(ins, outs, sems)

    return pl.pallas_call(body, name=name, in_specs=cm.in_specs, out_specs=cm.out_specs, out_shape=cm.out_shape,
                          scratch_shapes=cm.scratch)(*cm.arrays)


def _matmul(a, b, *, mode, tm, tn, tk, out_dtype, name, a_fn=None, extras=(), epi=None, a_cols=None,
            b_slab=None, out_slab=None, comm=None):
    a_shape = a.shape if a_cols is None else (a.shape[0], a_cols)
    b_shape = b.shape if b_slab is None else (b.shape[1], N_DEV * b_slab)
    if mode == "nn":
        (m, k), n = a_shape, b_shape[1]
    elif mode == "nt":
        (m, k), n = a_shape, b_shape[0]
    else:
        (k, m), n = a_shape, b_shape[1]
    tm, tn, tk = min(tm, m), min(tn, n), min(tk, k)
    if b_slab is not None:
        tn, tk = (tn, min(tk, b_slab)) if mode == "nt" else (min(tn, b_slab), tk)
    assert m % tm == 0 and n % tn == 0 and k % tk == 0, (name, a.shape, b.shape, tm, tn, tk)
    gi, gj, nk = m // tm, n // tn, k // tk
    n_ex = len(extras)
    cm = _Comm(comm) if comm else None
    nc = cm.n if cm else 0

    def body(a_ref, b_ref, *rest):
        ex_refs, rest = rest[:n_ex], rest[n_ex:]
        c_ins, o_ref, c_outs, acc, sems = rest[:nc], rest[nc], rest[nc + 1:2 * nc + 1], rest[2 * nc + 1], rest[2 * nc + 2:]
        i, j, kk = pl.program_id(0), pl.program_id(1), pl.program_id(2)
        if cm:
            @pl.when((i == 0) & (j == 0) & (kk == 0))
            def _():
                cm.start(c_ins, c_outs, sems)

        av = a_ref[...]
        if a_fn is not None:
            av = a_fn(av)
        part = _dot(_mx(av), _mx(b_ref[...]), mode)

        def finish(r):
            if epi is not None:
                r = epi(r, *[e[...] for e in ex_refs])
            o_ref[...] = r.astype(out_dtype)

        if nk == 1:
            finish(part)
        else:
            @pl.when(kk == 0)
            def _():
                acc[...] = part

            @pl.when((kk > 0) & (kk < nk - 1))
            def _():
                acc[...] += part

            @pl.when(kk == nk - 1)
            def _():
                finish(acc[...] + part)

        if cm:
            @pl.when((i == gi - 1) & (j == gj - 1) & (kk == nk - 1))
            def _():
                cm.wait(c_ins, c_outs, sems)

    a_spec = pl.BlockSpec((tk, tm), lambda i, j, kk: (kk, i)) if mode == "tn" else pl.BlockSpec((tm, tk), lambda i, j, kk: (i, kk))
    if b_slab is None:
        b_spec = pl.BlockSpec((tn, tk), lambda i, j, kk: (j, kk)) if mode == "nt" else pl.BlockSpec((tk, tn), lambda i, j, kk: (kk, j))
    elif mode == "nt":
        assert b_slab % tk == 0
        b_spec = pl.BlockSpec((None, tn, tk), lambda i, j, kk: ((kk * tk) // b_slab, j, ((kk * tk) % b_slab) // tk))
    else:
        assert b_slab % tn == 0
        b_spec = pl.BlockSpec((None, tk, tn), lambda i, j, kk: ((j * tn) // b_slab, kk, ((j * tn) % b_slab) // tn))
    if out_slab is None:
        o_spec, o_shape = pl.BlockSpec((tm, tn), lambda i, j, kk: (i, j)), jax.ShapeDtypeStruct((m, n), out_dtype)
    else:
        assert out_slab % tn == 0 and n == N_DEV * out_slab
        o_spec = pl.BlockSpec((None, tm, tn), lambda i, j, kk: ((j * tn) // out_slab, i, ((j * tn) % out_slab) // tn))
        o_shape = jax.ShapeDtypeStruct((N_DEV, m, out_slab), out_dtype)
    ex_specs = [pl.BlockSpec((tm if bs[0] is None else bs[0], tn if bs[1] is None else bs[1]),
                             functools.partial(lambda i, j, kk, f: f(i, j), f=im)) for (_, bs, im) in extras]
    res = pl.pallas_call(
        body,
        name=name,
        grid=(gi, gj, nk),
        in_specs=[a_spec, b_spec, *ex_specs] + (cm.in_specs if cm else []),
        out_specs=[o_spec] + (cm.out_specs if cm else []),
        out_shape=[o_shape] + (cm.out_shape if cm else []),
        scratch_shapes=[pltpu.VMEM((tm, tn) if nk > 1 else (8, 128), F32)] + (cm.scratch if cm else []),
        compiler_params=_params(("arbitrary",) * 3 if cm else ("parallel", "parallel", "arbitrary")),
    )(a, b, *[e[0] for e in extras], *(cm.arrays if cm else []))
    return (res[0], res[1:]) if cm else res[0]


def _ln_fwd(x, y, g, b, name):
    l, d = x.shape
    tl = 256

    def body(x_ref, y_ref, g_ref, b_ref, h_ref, o_ref, om_ref):
        h = ALPHA * x_ref[...] + y_ref[...]
        mu = jnp.mean(h, axis=-1, keepdims=True)
        c = h - mu
        var = jnp.mean(c * c, axis=-1, keepdims=True)
        h_ref[...] = h
        out = c * lax.rsqrt(var + LN_EPS) * g_ref[...] + b_ref[...]
        o_ref[...] = out
        om_ref[...] = out.astype(om_ref.dtype)

    row = pl.BlockSpec((tl, d), lambda i: (i, 0))
    vec = pl.BlockSpec((1, d), lambda i: (0, 0))
    return pl.pallas_call(
        body, name=name, grid=(l // tl,), in_specs=[row, row, vec, vec], out_specs=[row, row, row],
        out_shape=[jax.ShapeDtypeStruct((l, d), F32)] * 2 + [jax.ShapeDtypeStruct((l, d), MXU_DTYPE)],
        compiler_params=_params(("parallel",)),
    )(x, y, g, b)


def _ln_bwd(dout, h, g, name):
    l, d = h.shape
    tl = 256

    def body(do_ref, h_ref, g_ref, dh_ref, dhm_ref, dg_ref, db_ref):
        @pl.when(pl.program_id(0) == 0)
        def _():
            dg_ref[...] = jnp.zeros_like(dg_ref)
            db_ref[...] = jnp.zeros_like(db_ref)

        hv, do = h_ref[...], do_ref[...]
        mu = jnp.mean(hv, axis=-1, keepdims=True)
        c = hv - mu
        r = lax.rsqrt(jnp.mean(c * c, axis=-1, keepdims=True) + LN_EPS)
        xh = c * r
        dxh = do * g_ref[...]
        m1 = jnp.mean(dxh, axis=-1, keepdims=True)
        m2 = jnp.mean(dxh * xh, axis=-1, keepdims=True)
        dh = r * (dxh - m1 - xh * m2)
        dh_ref[...] = dh
        dhm_ref[...] = dh.astype(dhm_ref.dtype)
        dg_ref[...] += jnp.sum(do * xh, axis=0, keepdims=True)
        db_ref[...] += jnp.sum(do, axis=0, keepdims=True)

    row = pl.BlockSpec((tl, d), lambda i: (i, 0))
    vec = pl.BlockSpec((1, d), lambda i: (0, 0))
    return pl.pallas_call(
        body, name=name, grid=(l // tl,), in_specs=[row, row, vec], out_specs=[row, row, vec, vec],
        out_shape=[jax.ShapeDtypeStruct((l, d), F32), jax.ShapeDtypeStruct((l, d), MXU_DTYPE),
                   jax.ShapeDtypeStruct((1, d), F32), jax.ShapeDtypeStruct((1, d), F32)],
        compiler_params=_params(("arbitrary",)),
    )(dout, h, g)


def _loss_head(y, target):
    l, d = y.shape
    tl = 256

    def body(y_ref, t_ref, loss_ref, dy_ref):
        @pl.when(pl.program_id(0) == 0)
        def _():
            loss_ref[...] = jnp.zeros_like(loss_ref)

        e = y_ref[...] - t_ref[...]
        dy_ref[...] = e * (1.0 / d)
        s = jnp.sum(jnp.sum(e * e, axis=1, keepdims=True), axis=0, keepdims=True)
        loss_ref[...] += s * (0.5 / d)

    row = pl.BlockSpec((tl, d), lambda i: (i, 0))
    return pl.pallas_call(
        body, name="loss_head", grid=(l // tl,), in_specs=[row, row],
        out_specs=[pl.BlockSpec((1, 1), lambda i: (0, 0)), row],
        out_shape=[jax.ShapeDtypeStruct((1, 1), F32), jax.ShapeDtypeStruct((l, d), F32)],
        compiler_params=_params(("arbitrary",)),
    )(y, target)


def _s5_discretize(lam_re, lam_im, log_step, b_re, b_im):
    step = jnp.exp(log_step)[:, None]
    e = jnp.exp(lam_re * step)
    lbr, lbi = e * jnp.cos(lam_im * step), e * jnp.sin(lam_im * step)
    den = lam_re * lam_re + lam_im * lam_im
    qr = ((lbr - 1.0) * lam_re + lbi * lam_im) / den
    qi = (lbi * lam_re - (lbr - 1.0) * lam_im) / den
    bbr = qr[:, :, None] * b_re - qi[:, :, None] * b_im
    bbi = qr[:, :, None] * b_im + qi[:, :, None] * b_re
    return lbr, lbi, bbr, bbi


def _s5_dense(bbr, bbi, c_re, c_im):
    eye = jnp.eye(S5_GROUPS, dtype=F32)
    def bd(t):
        return jnp.einsum("gph,gk->ghkp", t, eye).reshape(GROUP_WIDTH, S5_NS)
    def cd(t):
        return jnp.einsum("ghp,gk->gpkh", t, eye).reshape(S5_NS, GROUP_WIDTH)
    return jnp.concatenate([bd(bbr), bd(bbi)], axis=1), jnp.concatenate([cd(c_re), -cd(c_im)], axis=0)


def _s5_undense_b(dbmat):
    eye = jnp.eye(S5_GROUPS, dtype=F32)[:, None, :, None]
    def ex(t):
        return jnp.sum(t.reshape(S5_GROUPS, S5_CH, S5_GROUPS, S5_STATE) * eye, axis=2).transpose(0, 2, 1)
    return ex(dbmat[:, :S5_NS]), ex(dbmat[:, S5_NS:])


def _s5_undense_c(dcmat):
    eye = jnp.eye(S5_GROUPS, dtype=F32)[:, None, :, None]
    def ex(t):
        return jnp.sum(t.reshape(S5_GROUPS, S5_STATE, S5_GROUPS, S5_CH) * eye, axis=2).transpose(0, 2, 1)
    return ex(dcmat[:S5_NS]), -ex(dcmat[S5_NS:])


def _s5_scan(bu, lam, *, reverse, h=None, name):
    l, w = bu.shape
    ns = w // 2
    tl = 256
    nb = l // tl
    with_h = h is not None

    def body(*refs):
        if with_h:
            bu_ref, lam_ref, h_ref, o_ref, dl_ref, carry = refs
        else:
            bu_ref, lam_ref, o_ref, carry = refs

        @pl.when(pl.program_id(0) == 0)
        def _():
            carry[...] = jnp.zeros_like(carry)
            if with_h:
                dl_ref[...] = jnp.zeros_like(dl_ref)

        lr, li = lam_ref[:, :ns], lam_ref[:, ns:]

        def step(t, c):
            row = (tl - 1 - t) if reverse else t
            cr, ci = c[0], c[1]
            out = c[2:]
            if with_h:
                hr, hi = h_ref[pl.ds(row, 1), :ns], h_ref[pl.ds(row, 1), ns:]
                out = (out[0] + cr * hr + ci * hi, out[1] + ci * hr - cr * hi)
            nr = lr * cr - li * ci + bu_ref[pl.ds(row, 1), :ns]
            ni = lr * ci + li * cr + bu_ref[pl.ds(row, 1), ns:]
            o_ref[pl.ds(row, 1), :ns] = nr
            o_ref[pl.ds(row, 1), ns:] = ni
            return (nr, ni) + tuple(out)

        init = (carry[:, :ns], carry[:, ns:])
        if with_h:
            init = init + (dl_ref[:, :ns], dl_ref[:, ns:])
        fin = lax.fori_loop(0, tl, step, init)
        carry[:, :ns] = fin[0]
        carry[:, ns:] = fin[1]
        if with_h:
            dl_ref[:, :ns] = fin[2]
            dl_ref[:, ns:] = fin[3]

    idx = (lambda i: (nb - 1 - i, 0)) if reverse else (lambda i: (i, 0))
    row = pl.BlockSpec((tl, w), idx)
    vec = pl.BlockSpec((1, w), lambda i: (0, 0))
    in_specs = [row, vec] + ([row] if with_h else [])
    out_specs = [row] + ([vec] if with_h else [])
    out_shape = [jax.ShapeDtypeStruct((l, w), F32)] + ([jax.ShapeDtypeStruct((1, w), F32)] if with_h else [])
    res = pl.pallas_call(
        body, name=name, grid=(nb,), in_specs=in_specs, out_specs=out_specs, out_shape=out_shape,
        scratch_shapes=[pltpu.VMEM((1, w), F32)], compiler_params=_params(("arbitrary",)),
    )(*([bu, lam] + ([h] if with_h else [])))
    return res if with_h else res[0]


def _s5_glu_fwd(y, glu_w, glu_b, name):
    l, d = y.shape
    tl = min(512, l)

    def body(y_ref, w_ref, b_ref, o_ref):
        yg = _gelu(y_ref[...])
        z = _mdot(yg, w_ref[...]) + b_ref[...]
        o_ref[...] = (yg * _sigmoid(z)).astype(o_ref.dtype)

    return pl.pallas_call(
        body, name=name, grid=(l // tl,),
        in_specs=[pl.BlockSpec((tl, d), lambda i: (i, 0)), pl.BlockSpec((d, d), lambda i: (0, 0)), pl.BlockSpec((1, d), lambda i: (0, 0))],
        out_specs=pl.BlockSpec((tl, d), lambda i: (i, 0)), out_shape=jax.ShapeDtypeStruct((l, d), BF16),
        compiler_params=_params(("parallel",)),
    )(y, glu_w, glu_b)


def _s5_glu_bwd(dmixed, y, proj, glu_w, glu_b, name):
    l, d = y.shape
    tl = min(512, l)

    def body(do_ref, y_ref, u_ref, w_ref, b_ref, dy_ref, dz_ref, yg_ref, db_ref, dd_ref):
        @pl.when(pl.program_id(0) == 0)
        def _():
            db_ref[...] = jnp.zeros_like(db_ref)
            dd_ref[...] = jnp.zeros_like(dd_ref)

        yv, do = y_ref[...], do_ref[...]
        yg = _gelu(yv)
        gate = _sigmoid(_mdot(yg, w_ref[...]) + b_ref[...])
        dz = do * yg * gate * (1.0 - gate)
        dyg = do * gate + _mdot(dz, w_ref[...], "nt")
        dy = dyg * _gelu_grad(yv)
        dy_ref[...] = dy
        dz_ref[...] = dz.astype(dz_ref.dtype)
        yg_ref[...] = yg.astype(yg_ref.dtype)
        db_ref[...] += jnp.sum(dz, axis=0, keepdims=True)
        dd_ref[...] += jnp.sum(dy * u_ref[...], axis=0, keepdims=True)

    row = pl.BlockSpec((tl, d), lambda i: (i, 0))
    vec = pl.BlockSpec((1, d), lambda i: (0, 0))
    return pl.pallas_call(
        body, name=name, grid=(l // tl,),
        in_specs=[row, row, row, pl.BlockSpec((d, d), lambda i: (0, 0)), vec],
        out_specs=[row, row, row, vec, vec],
        out_shape=[jax.ShapeDtypeStruct((l, d), F32), jax.ShapeDtypeStruct((l, d), BF16), jax.ShapeDtypeStruct((l, d), BF16),
                   jax.ShapeDtypeStruct((1, d), F32), jax.ShapeDtypeStruct((1, d), F32)],
        compiler_params=_params(("arbitrary",)),
    )(dmixed, y, proj, glu_w, glu_b)


def _sgu_pair(w_ref, x, j, dims):
    lo = lax.broadcasted_iota(jnp.int32, x.shape, 1) < (GROUP_WIDTH // SGU_HEADS)
    xb = _mx(x)
    r0 = _dot(w_ref[2 * j], xb, dims)
    r1 = _dot(w_ref[2 * j + 1], xb, dims)
    return jnp.where(lo, r0, r1)


def _sgu_norm(v, g, b):
    mu = jnp.mean(v, axis=-1, keepdims=True)
    c = v - mu
    r = lax.rsqrt(jnp.mean(c * c, axis=-1, keepdims=True) + LN_EPS)
    return c * r, r


def _sgu_fwd(proj, norm_g, norm_b, wm, bfull, name):
    l = proj.shape[0]
    tl = 256
    gw = GROUP_WIDTH

    def body(zu_ref, zv_ref, g_ref, b_ref, w_ref, bf_ref, o_ref):
        for c in range(tl // SGU_CHUNK):
            rows = slice(c * SGU_CHUNK, (c + 1) * SGU_CHUNK)
            u = _gelu(zu_ref[rows, :])
            vh, _ = _sgu_norm(_gelu(zv_ref[rows, :]), None, None)
            vn = vh * g_ref[...] + b_ref[...]
            for j in range(gw // 128):
                cols = slice(j * 128, (j + 1) * 128)
                mixed = _sgu_pair(w_ref, vn[:, cols], j, "nn") + bf_ref[:, cols]
                o_ref[rows, cols] = (u[:, cols] * mixed).astype(o_ref.dtype)

    vec = pl.BlockSpec((1, gw), lambda i: (0, 0))
    return pl.pallas_call(
        body, name=name, grid=(l // tl,),
        in_specs=[pl.BlockSpec((tl, gw), lambda i: (i, 1)), pl.BlockSpec((tl, gw), lambda i: (i, 2)), vec, vec,
                  pl.BlockSpec((SGU_HEADS, SGU_CHUNK, SGU_CHUNK), lambda i: (0, 0, 0)), pl.BlockSpec((SGU_CHUNK, gw), lambda i: (0, 0))],
        out_specs=pl.BlockSpec((tl, gw), lambda i: (i, 0)), out_shape=jax.ShapeDtypeStruct((l, gw), BF16),
        compiler_params=_params(("parallel",)),
    )(proj, proj, norm_g, norm_b, wm, bfull)


def _sgu_bwd(dmixed, proj, norm_g, norm_b, wm, bfull, name):
    l = proj.shape[0]
    tl = 256
    gw = GROUP_WIDTH
    hd = gw // SGU_HEADS

    def body(do_ref, zu_ref, zv_ref, g_ref, b_ref, w_ref, bf_ref, dzu_ref, dzv_ref, dw_ref, dbf_ref, dg_ref, dnb_ref):
        @pl.when(pl.program_id(0) == 0)
        def _():
            dw_ref[...] = jnp.zeros_like(dw_ref)
            dbf_ref[...] = jnp.zeros_like(dbf_ref)
            dg_ref[...] = jnp.zeros_like(dg_ref)
            dnb_ref[...] = jnp.zeros_like(dnb_ref)

        for c in range(tl // SGU_CHUNK):
            rows = slice(c * SGU_CHUNK, (c + 1) * SGU_CHUNK)
            zu, zv, do = zu_ref[rows, :], zv_ref[rows, :], do_ref[rows, :]
            u = _gelu(zu)
            vh, r = _sgu_norm(_gelu(zv), None, None)
            vn = vh * g_ref[...] + b_ref[...]
            dvn_parts, mixed_parts = [], []
            for j in range(gw // 128):
                cols = slice(j * 128, (j + 1) * 128)
                vb = vn[:, cols]
                mixed_parts.append(_sgu_pair(w_ref, vb, j, "nn") + bf_ref[:, cols])
                dm = do[:, cols] * u[:, cols]
                dvn_parts.append(_sgu_pair(w_ref, dm, j, "tn"))
                lo = lax.broadcasted_iota(jnp.int32, dm.shape, 1) < hd
                dw_ref[2 * j] += _mdot(jnp.where(lo, dm, 0.0), vb, "nt")
                dw_ref[2 * j + 1] += _mdot(jnp.where(lo, 0.0, dm), vb, "nt")
                dbf_ref[:, cols] += dm
            mixed = jnp.concatenate(mixed_parts, axis=1)
            dvn = jnp.concatenate(dvn_parts, axis=1)
            dzu_ref[rows, :] = (do * mixed * _gelu_grad(zu)).astype(dzu_ref.dtype)
            dg_ref[...] += jnp.sum(dvn * vh, axis=0, keepdims=True)
            dnb_ref[...] += jnp.sum(dvn, axis=0, keepdims=True)
            dvh = dvn * g_ref[...]
            m1 = jnp.mean(dvh, axis=-1, keepdims=True)
            m2 = jnp.mean(dvh * vh, axis=-1, keepdims=True)
            dv = r * (dvh - m1 - vh * m2)
            dzv_ref[rows, :] = (dv * _gelu_grad(zv)).astype(dzv_ref.dtype)

    vec = pl.BlockSpec((1, gw), lambda i: (0, 0))
    row = pl.BlockSpec((tl, gw), lambda i: (i, 0))
    wspec = pl.BlockSpec((SGU_HEADS, SGU_CHUNK, SGU_CHUNK), lambda i: (0, 0, 0))
    bspec = pl.BlockSpec((SGU_CHUNK, gw), lambda i: (0, 0))
    return pl.pallas_call(
        body, name=name, grid=(l // tl,),
        in_specs=[pl.BlockSpec((tl, gw), lambda i: (i, 1)), pl.BlockSpec((tl, gw), lambda i: (i, 1)), pl.BlockSpec((tl, gw), lambda i: (i, 2)),
                  vec, vec, wspec, bspec],
        out_specs=[row, row, wspec, bspec, vec, vec],
        out_shape=[jax.ShapeDtypeStruct((l, gw), BF16), jax.ShapeDtypeStruct((l, gw), BF16),
                   jax.ShapeDtypeStruct((SGU_HEADS, SGU_CHUNK, SGU_CHUNK), F32), jax.ShapeDtypeStruct((SGU_CHUNK, gw), F32),
                   jax.ShapeDtypeStruct((1, gw), F32), jax.ShapeDtypeStruct((1, gw), F32)],
        compiler_params=_params(("arbitrary",)),
    )(dmixed, proj, proj, norm_g, norm_b, wm, bfull)


HALO = 16


def _window_sums(ext, n_rows, forward):
    def sh(x, k):
        return pltpu.roll(x, (n_rows - k) if forward else k, axis=0)
    s2 = ext + sh(ext, 1)
    s4 = s2 + sh(s2, 2)
    s8 = s4 + sh(s4, 4)
    s16 = s8 + sh(s8, 8)
    return (s2, s4, s8, s16)


def _pool_fwd(proj, pool_w, scale, name):
    l = proj.shape[0]
    tl = 256
    gw = GROUP_WIDTH
    pg = gw // len(POOL_WINDOWS)

    def body(x_ref, halo_ref, w_ref, s_ref, o_ref, p_ref):
        i = pl.program_id(0)
        x = x_ref[...]
        halo = jnp.where(i > 0, halo_ref[...], 0.0)
        ext = jnp.concatenate([halo, x], axis=0)
        sums = _window_sums(ext, tl + HALO, False)
        t = i * tl + lax.broadcasted_iota(jnp.int32, (tl, pg), 0)
        for gi, win in enumerate(POOL_WINDOWS):
            cols = slice(gi * pg, (gi + 1) * pg)
            cnt = jnp.minimum(t + 1, win).astype(F32)
            pooled = sums[gi][HALO:, cols] / cnt - x[:, cols]
            p_ref[:, cols] = pooled
            o_ref[:, cols] = (_mdot(pooled, w_ref[gi]) * s_ref[:, cols]).astype(o_ref.dtype)

    row = pl.BlockSpec((tl, gw), lambda i: (i, 0))
    return pl.pallas_call(
        body, name=name, grid=(l // tl,),
        in_specs=[pl.BlockSpec((tl, gw), lambda i: (i, 3)),
                  pl.BlockSpec((HALO, gw), lambda i: (jnp.maximum(i * (tl // HALO) - 1, 0), 3)),
                  pl.BlockSpec((len(POOL_WINDOWS), pg, pg), lambda i: (0, 0, 0)), pl.BlockSpec((1, gw), lambda i: (0, 0))],
        out_specs=[row, row], out_shape=[jax.ShapeDtypeStruct((l, gw), BF16), jax.ShapeDtypeStruct((l, gw), F32)],
        compiler_params=_params(("parallel",)),
    )(proj, proj, pool_w, scale)


def _pool_bwd_map(dmixed, pooled, pool_w, scale, name):
    l, gw = pooled.shape
    tl = 256
    ng = len(POOL_WINDOWS)
    pg = gw // ng

    def body(do_ref, p_ref, w_ref, s_ref, dp_ref, dw_ref, ds_ref):
        @pl.when(pl.program_id(0) == 0)
        def _():
            dw_ref[...] = jnp.zeros_like(dw_ref)
            ds_ref[...] = jnp.zeros_like(ds_ref)

        for gi in range(ng):
            cols = slice(gi * pg, (gi + 1) * pg)
            do, pooled_g = do_ref[:, cols], p_ref[:, cols]
            mixed = _mdot(pooled_g, w_ref[gi])
            ds_ref[:, cols] += jnp.sum(do * mixed, axis=0, keepdims=True)
            dm = do * s_ref[:, cols]
            dw_ref[gi] += _mdot(pooled_g, dm, "tn")
            dp_ref[:, cols] = _mdot(dm, w_ref[gi], "nt")

    row = pl.BlockSpec((tl, gw), lambda i: (i, 0))
    wspec = pl.BlockSpec((ng, pg, pg), lambda i: (0, 0, 0))
    vec = pl.BlockSpec((1, gw), lambda i: (0, 0))
    return pl.pallas_call(
        body, name=name, grid=(l // tl,),
        in_specs=[pl.BlockSpec((tl, gw), lambda i: (i, 2)), row, wspec, vec], out_specs=[row, wspec, vec],
        out_shape=[jax.ShapeDtypeStruct((l, gw), F32), jax.ShapeDtypeStruct((ng, pg, pg), F32), jax.ShapeDtypeStruct((1, gw), F32)],
        compiler_params=_params(("arbitrary",)),
    )(dmixed, pooled, pool_w, scale)


def _pool_bwd_window(dpooled, name):
    l, gw = dpooled.shape
    tl = 256
    nb = l // tl
    pg = gw // len(POOL_WINDOWS)

    def body(d_ref, halo_ref, o_ref):
        i = pl.program_id(0)
        d = d_ref[...]
        halo = jnp.where(i < nb - 1, halo_ref[...], 0.0)
        ext = jnp.concatenate([d, halo], axis=0)
        t = i * tl + lax.broadcasted_iota(jnp.int32, (tl + HALO, pg), 0)
        for gi, win in enumerate(POOL_WINDOWS):
            cols = slice(gi * pg, (gi + 1) * pg)
            cnt = jnp.minimum(t + 1, win).astype(F32)
            sums = _window_sums(ext[:, cols] / cnt, tl + HALO, True)
            o_ref[:, cols] = (sums[gi][:tl, :] - d[:, cols]).astype(o_ref.dtype)

    row = pl.BlockSpec((tl, gw), lambda i: (i, 0))
    return pl.pallas_call(
        body, name=name, grid=(nb,),
        in_specs=[row, pl.BlockSpec((HALO, gw), lambda i: (jnp.minimum((i + 1) * (tl // HALO), l // HALO - 1), 0))],
        out_specs=row, out_shape=jax.ShapeDtypeStruct((l, gw), BF16), compiler_params=_params(("parallel",)),
    )(dpooled, dpooled)


CONV_HALO = 8
QKV_BLK = 4


def _head_sums(x):
    parts = []
    for hd in range(DN_HEADS):
        s = jnp.sum(x[:, hd * DN_HEAD_DIM:(hd + 1) * DN_HEAD_DIM], axis=-1, keepdims=True)
        parts.append(jnp.broadcast_to(s, (x.shape[0], DN_HEAD_DIM)))
    return jnp.concatenate(parts, axis=1)


def _gdn_pre_fwd(proj, proj_ab, conv_w, a_log, dt_bias, name):
    l = proj.shape[0]
    tl = 256
    gw = GROUP_WIDTH

    def body(xq, xk, xv, hq, hk, hv, w_ref, ab_ref, al_ref, dt_ref, qn_ref, kn_ref, v_ref, cq_ref, ck_ref, cv_ref, gb_ref):
        i = pl.program_id(0)
        for p, (x_ref, h_ref, c_ref) in enumerate(((xq, hq, cq_ref), (xk, hk, ck_ref), (xv, hv, cv_ref))):
            ext = jnp.concatenate([jnp.where(i > 0, h_ref[...], 0.0), x_ref[...]], axis=0)
            conv = jnp.zeros((tl, gw), F32)
            for j in range(DN_CONV):
                k = DN_CONV - 1 - j
                shifted = ext if k == 0 else pltpu.roll(ext, k, axis=0)
                conv = conv + shifted[CONV_HALO:, :] * w_ref[j:j + 1, p * gw:(p + 1) * gw]
            c_ref[...] = conv
            s = _silu(conv)
            if p == 2:
                v_ref[...] = s
            else:
                r = lax.rsqrt(_head_sums(s * s) + L2_EPS)
                (qn_ref if p == 0 else kn_ref)[...] = s * r * (DN_HEAD_DIM ** -0.5 if p == 0 else 1.0)
        ab = ab_ref[...]
        lane = lax.broadcasted_iota(jnp.int32, ab.shape, 1)
        g = -jnp.exp(al_ref[...]) * _softplus(ab + dt_ref[...])
        gb_ref[...] = jnp.where(lane < DN_HEADS, g, _sigmoid(ab))

    def xs(b):
        return pl.BlockSpec((tl, gw), lambda i: (i, b))

    def hs(b):
        return pl.BlockSpec((CONV_HALO, gw), lambda i: (jnp.maximum(i * (tl // CONV_HALO) - 1, 0), b))

    row = pl.BlockSpec((tl, gw), lambda i: (i, 0))
    abrow = pl.BlockSpec((tl, AB_PAD), lambda i: (i, 0))
    abvec = pl.BlockSpec((1, AB_PAD), lambda i: (0, 0))
    return pl.pallas_call(
        body, name=name, grid=(l // tl,),
        in_specs=[xs(QKV_BLK), xs(QKV_BLK + 1), xs(QKV_BLK + 2), hs(QKV_BLK), hs(QKV_BLK + 1), hs(QKV_BLK + 2),
                  pl.BlockSpec((DN_CONV, 3 * gw), lambda i: (0, 0)), abrow, abvec, abvec],
        out_specs=[row] * 6 + [abrow],
        out_shape=[jax.ShapeDtypeStruct((l, gw), F32)] * 6 + [jax.ShapeDtypeStruct((l, AB_PAD), F32)],
        compiler_params=_params(("parallel",)),
    )(proj, proj, proj, proj, proj, proj, conv_w, proj_ab, a_log, dt_bias)


def _gdn_pre_bwd(dq, dk, dv, cq, ck, cv, dgb, gb, proj_ab, a_log, dt_bias, name):
    l, gw = cq.shape
    tl = 256

    def body(dq_ref, dk_ref, dv_ref, cq_ref, ck_ref, cv_ref, dgb_ref, gb_ref, ab_ref, al_ref, dt_ref,
             dcq_ref, dck_ref, dcv_ref, dab_ref, dal_ref, ddt_ref):
        @pl.when(pl.program_id(0) == 0)
        def _():
            dal_ref[...] = jnp.zeros_like(dal_ref)
            ddt_ref[...] = jnp.zeros_like(ddt_ref)

        for p, (d_ref, c_ref, o_ref) in enumerate(((dq_ref, cq_ref, dcq_ref), (dk_ref, ck_ref, dck_ref), (dv_ref, cv_ref, dcv_ref))):
            c, d = c_ref[...], d_ref[...]
            if p == 2:
                ds = d
            else:
                s = _silu(c)
                r = lax.rsqrt(_head_sums(s * s) + L2_EPS)
                ds = (DN_HEAD_DIM ** -0.5 if p == 0 else 1.0) * r * (d - s * r * r * _head_sums(d * s))
            o_ref[...] = ds * _silu_grad(c)
        ab, dgb_v, gb_v = ab_ref[...], dgb_ref[...], gb_ref[...]
        lane = lax.broadcasted_iota(jnp.int32, ab.shape, 1)
        is_g = lane < DN_HEADS
        dpre = dgb_v * (-jnp.exp(al_ref[...])) * _sigmoid(ab + dt_ref[...])
        dab_ref[...] = jnp.where(is_g, dpre, dgb_v * gb_v * (1.0 - gb_v)).astype(dab_ref.dtype)
        dal_ref[...] += jnp.sum(jnp.where(is_g, dgb_v * gb_v, 0.0), axis=0, keepdims=True)
        ddt_ref[...] += jnp.sum(jnp.where(is_g, dpre, 0.0), axis=0, keepdims=True)

    row = pl.BlockSpec((tl, gw), lambda i: (i, 0))
    abrow = pl.BlockSpec((tl, AB_PAD), lambda i: (i, 0))
    abvec = pl.BlockSpec((1, AB_PAD), lambda i: (0, 0))
    return pl.pallas_call(
        body, name=name, grid=(l // tl,),
        in_specs=[row] * 6 + [abrow, abrow, abrow, abvec, abvec],
        out_specs=[row, row, row, abrow, abvec, abvec],
        out_shape=[jax.ShapeDtypeStruct((l, gw), F32)] * 3 + [jax.ShapeDtypeStruct((l, AB_PAD), BF16),
                   jax.ShapeDtypeStruct((1, AB_PAD), F32), jax.ShapeDtypeStruct((1, AB_PAD), F32)],
        compiler_params=_params(("arbitrary",)),
    )(dq, dk, dv, cq, ck, cv, dgb, gb, proj_ab, a_log, dt_bias)


def _conv_bwd(dc, proj, col_blk, w_part, name):
    l, gw = dc.shape
    tl = 256
    nb = l // tl

    def body(dc_ref, halo_ref, x_ref, w_ref, dx_ref, dw_ref):
        i = pl.program_id(0)

        @pl.when(i == 0)
        def _():
            dw_ref[...] = jnp.zeros_like(dw_ref)

        ext = jnp.concatenate([dc_ref[...], jnp.where(i < nb - 1, halo_ref[...], 0.0)], axis=0)
        x = x_ref[...]
        dx = jnp.zeros((tl, gw), F32)
        rid = lax.broadcasted_iota(jnp.int32, (8, gw), 0)
        dw = jnp.zeros((8, gw), F32)
        for j in range(DN_CONV):
            k = DN_CONV - 1 - j
            shifted = (ext if k == 0 else pltpu.roll(ext, tl + CONV_HALO - k, axis=0))[:tl, :]
            dx = dx + shifted * w_ref[j:j + 1, :]
            dw = dw + jnp.where(rid == j, jnp.sum(x * shifted, axis=0, keepdims=True), 0.0)
        dx_ref[...] = dx.astype(dx_ref.dtype)
        dw_ref[...] += dw

    row = pl.BlockSpec((tl, gw), lambda i: (i, 0))
    return pl.pallas_call(
        body, name=name, grid=(nb,),
        in_specs=[row, pl.BlockSpec((CONV_HALO, gw), lambda i: (jnp.minimum((i + 1) * (tl // CONV_HALO), l // CONV_HALO - 1), 0)),
                  pl.BlockSpec((tl, gw), lambda i: (i, col_blk)), pl.BlockSpec((DN_CONV, gw), lambda i: (0, 0))],
        out_specs=[row, pl.BlockSpec((8, gw), lambda i: (0, 0))],
        out_shape=[jax.ShapeDtypeStruct((l, gw), BF16), jax.ShapeDtypeStruct((8, gw), F32)],
        compiler_params=_params(("arbitrary",)),
    )(dc, dc, proj, w_part)


TERMS_CHUNKS = 4


def _bdot(a, b, dims="nn", precision=None):
    cd = {"nn": ((2,), (1,)), "nt": ((2,), (2,)), "tn": ((1,), (1,))}[dims]
    return lax.dot_general(a, b, (cd, ((0,), (0,))), preferred_element_type=F32, precision=precision)


def _bmdot(a, b, dims="nn"):
    return _bdot(_mx(a), _mx(b), dims)


def _wy_terms(q, k, v, gcol, beta, t=None):
    c = DN_CHUNK
    ii = lax.broadcasted_iota(jnp.int32, (1, c, c), 1)
    jj = lax.broadcasted_iota(jnp.int32, (1, c, c), 2)
    tril, strict = ii >= jj, ii > jj
    grow = jnp.sum(jnp.where(ii == jj, gcol, 0.0), axis=1, keepdims=True)
    gc_col = jnp.sum(jnp.where(tril, grow, 0.0), axis=2, keepdims=True)
    gc_row = jnp.sum(jnp.where(ii <= jj, gcol, 0.0), axis=1, keepdims=True)
    dec = jnp.exp(jnp.where(tril, gc_col - gc_row, -1e30))
    kb, vb = k * beta, v * beta
    kk = _bmdot(kb, k, "nt")
    if t is None:
        a = jnp.where(strict, kk * dec, 0.0)
        t = jnp.where(ii == jj, 1.0, 0.0) - a
        p = _bdot(a, a, precision=HI)
        for it in range(5):
            t = t + _bdot(t, p, precision=HI)
            if it < 4:
                p = _bdot(p, p, precision=HI)
    eg = jnp.exp(gc_col)
    gc_last = gc_col[:, c - 1:c, :]
    kbg = kb * eg
    qk0 = _bmdot(q, k, "nt")
    e2 = jnp.exp(gc_last - gc_col)
    return dict(ii=ii, jj=jj, tril=tril, strict=strict, dec=dec, kb=kb, vb=vb, kk=kk, t=t, eg=eg, kbg=kbg,
                qk0=qk0, qk=jnp.where(tril, qk0 * dec, 0.0), qg=q * eg, e2=e2, kt=k * e2, gl=jnp.exp(gc_last))


def _to_heads(x, g):
    return jnp.concatenate([x[:, h * DN_HEAD_DIM:(h + 1) * DN_HEAD_DIM].reshape(g, DN_CHUNK, DN_HEAD_DIM)
                            for h in range(DN_HEADS)], axis=0)


def _from_heads(t, ref, g):
    for h in range(DN_HEADS):
        ref[:, h * DN_HEAD_DIM:(h + 1) * DN_HEAD_DIM] = t[h * g:(h + 1) * g].reshape(g * DN_CHUNK, DN_HEAD_DIM).astype(ref.dtype)


def _head_columns(gbv, first_lane, g):
    lane = lax.broadcasted_iota(jnp.int32, gbv.shape, 1)
    return jnp.concatenate([jnp.sum(jnp.where(lane == first_lane + h, gbv, 0.0), axis=1, keepdims=True).reshape(g, DN_CHUNK, 1)
                            for h in range(DN_HEADS)], axis=0)


def _gdn_terms_fwd(qn, kn, v, gb, name):
    l = qn.shape[0]
    n_chunks = l // DN_CHUNK
    g = min(TERMS_CHUNKS, n_chunks)
    rows, c, nh = g * DN_CHUNK, DN_CHUNK, DN_HEADS

    def body(q_ref, k_ref, v_ref, gb_ref, u_ref, w_ref, qg_ref, kt_ref, qk_ref, t_ref, gl_ref):
        gbv = gb_ref[...]
        x = _wy_terms(_to_heads(q_ref[...], g), _to_heads(k_ref[...], g), _to_heads(v_ref[...], g),
                      _head_columns(gbv, 0, g), _head_columns(gbv, nh, g))
        _from_heads(_bmdot(x["t"], x["vb"]), u_ref, g)
        _from_heads(_bmdot(x["t"], x["kbg"]), w_ref, g)
        _from_heads(x["qg"], qg_ref, g)
        _from_heads(x["kt"], kt_ref, g)
        for h in range(nh):
            qk_ref[:, h] = x["qk"][h * g:(h + 1) * g]
            t_ref[:, h] = x["t"][h * g:(h + 1) * g]
            gl_ref[:, h] = jnp.broadcast_to(x["gl"][h * g:(h + 1) * g], (g, 1, 128))

    row = pl.BlockSpec((rows, GROUP_WIDTH), lambda i: (i, 0))
    sq = pl.BlockSpec((g, nh, c, c), lambda i: (i, 0, 0, 0))
    glb = pl.BlockSpec((g, nh, 1, 128), lambda i: (i, 0, 0, 0))
    return pl.pallas_call(
        body, name=name, grid=(n_chunks // g,), in_specs=[row, row, row, pl.BlockSpec((rows, AB_PAD), lambda i: (i, 0))],
        out_specs=[row] * 4 + [sq, sq, glb],
        out_shape=[jax.ShapeDtypeStruct((l, GROUP_WIDTH), F32)] * 4 + [jax.ShapeDtypeStruct((n_chunks, nh, c, c), F32)] * 2
        + [jax.ShapeDtypeStruct((n_chunks, nh, 1, 128), F32)],
        compiler_params=_params(("parallel",)),
    )(qn, kn, v, gb)


def _rec_specs(n_chunks, reverse):
    c, hd, nh = DN_CHUNK, DN_HEAD_DIM, DN_HEADS
    ch = (lambda n: n_chunks - 1 - n) if reverse else (lambda n: n)
    return (pl.BlockSpec((c, GROUP_WIDTH), lambda n: (ch(n), 0)), pl.BlockSpec((1, nh, c, c), lambda n: (ch(n), 0, 0, 0)),
            pl.BlockSpec((1, nh, 1, 128), lambda n: (ch(n), 0, 0, 0)), pl.BlockSpec((1, nh, hd, hd), lambda n: (ch(n), 0, 0, 0)))


def _gdn_rec_fwd(u, w, qg, kt, qk, gl, name):
    l = u.shape[0]
    n_chunks = l // DN_CHUNK
    hd, nh = DN_HEAD_DIM, DN_HEADS
    blk, sq, glb, st = _rec_specs(n_chunks, False)
    heads = range(nh)

    def body(u_ref, w_ref, qg_ref, kt_ref, qk_ref, gl_ref, o_ref, vn_ref, s_ref, state):
        @pl.when(pl.program_id(0) == 0)
        def _():
            state[...] = jnp.zeros_like(state)

        def cols(h):
            return slice(h * hd, (h + 1) * hd)
        s = [state[h] for h in heads]
        ws = [_mdot(w_ref[:, cols(h)], s[h]) for h in heads]
        vn = [u_ref[:, cols(h)] - ws[h] for h in heads]
        kv = [_mdot(kt_ref[:, cols(h)], vn[h], "tn") for h in heads]
        for h in heads:
            state[h] = s[h] * gl_ref[0, h] + kv[h]
        o1 = [_mdot(qg_ref[:, cols(h)], s[h]) for h in heads]
        o2 = [_mdot(qk_ref[0, h], vn[h]) for h in heads]
        for h in heads:
            s_ref[0, h] = s[h]
            o_ref[:, cols(h)] = o1[h] + o2[h]
            vn_ref[:, cols(h)] = vn[h]

    return pl.pallas_call(
        body, name=name, grid=(n_chunks,), in_specs=[blk, blk, blk, blk, sq, glb], out_specs=[blk, blk, st],
        out_shape=[jax.ShapeDtypeStruct((l, GROUP_WIDTH), F32)] * 2 + [jax.ShapeDtypeStruct((n_chunks, nh, hd, hd), F32)],
        scratch_shapes=[pltpu.VMEM((nh, hd, hd), F32)], compiler_params=_params(("arbitrary",)),
    )(u, w, qg, kt, qk, gl)


def _gdn_rec_bwd(do, w, qg, kt, vn, qk, gl, states, name):
    l = do.shape[0]
    n_chunks = l // DN_CHUNK
    hd, nh, c = DN_HEAD_DIM, DN_HEADS, DN_CHUNK
    blk, sq, glb, st = _rec_specs(n_chunks, True)
    heads = range(nh)

    def body(do_ref, w_ref, qg_ref, kt_ref, vn_ref, qk_ref, gl_ref, s_ref, dvn_ref, dw_ref, dkt_ref, dqg_ref, dqk_ref, dgl_ref, dstate):
        @pl.when(pl.program_id(0) == 0)
        def _():
            dstate[...] = jnp.zeros_like(dstate)

        def cols(h):
            return slice(h * hd, (h + 1) * hd)
        tril = lax.broadcasted_iota(jnp.int32, (c, c), 0) >= lax.broadcasted_iota(jnp.int32, (c, c), 1)
        ds = [dstate[h] for h in heads]
        dout = [do_ref[:, cols(h)] for h in heads]
        a1 = [_mdot(qk_ref[0, h], dout[h], "tn") for h in heads]
        a2 = [_mdot(kt_ref[:, cols(h)], ds[h]) for h in heads]
        dvn = [a1[h] + a2[h] for h in heads]
        b1 = [_mdot(qg_ref[:, cols(h)], dout[h], "tn") for h in heads]
        b2 = [_mdot(w_ref[:, cols(h)], dvn[h], "tn") for h in heads]
        for h in heads:
            dstate[h] = b1[h] + gl_ref[0, h] * ds[h] - b2[h]
        for h in heads:
            s, vnew = s_ref[0, h], vn_ref[:, cols(h)]
            dvn_ref[:, cols(h)] = dvn[h]
            dw_ref[:, cols(h)] = -_mdot(dvn[h], s, "nt")
            dkt_ref[:, cols(h)] = _mdot(vnew, ds[h], "nt")
            dqg_ref[:, cols(h)] = _mdot(dout[h], s, "nt")
            dqk_ref[0, h] = jnp.where(tril, _mdot(dout[h], vnew, "nt"), 0.0)
            dgl = jnp.sum(jnp.sum(ds[h] * s, axis=1, keepdims=True), axis=0, keepdims=True)
            dgl_ref[0, h] = jnp.broadcast_to(dgl, (1, 128))

    return pl.pallas_call(
        body, name=name, grid=(n_chunks,), in_specs=[blk] * 5 + [sq, glb, st], out_specs=[blk] * 4 + [sq, glb],
        out_shape=[jax.ShapeDtypeStruct((l, GROUP_WIDTH), F32)] * 4 + [jax.ShapeDtypeStruct((n_chunks, nh, c, c), F32),
                                                                       jax.ShapeDtypeStruct((n_chunks, nh, 1, 128), F32)],
        scratch_shapes=[pltpu.VMEM((nh, hd, hd), F32)], compiler_params=_params(("arbitrary",)),
    )(do, w, qg, kt, vn, qk, gl, states)


def _gdn_terms_bwd(qn, kn, v, gb, t_inv, dvn, dw, dkt, dqg, dqk, dgl, name):
    l = qn.shape[0]
    n_chunks = l // DN_CHUNK
    g = min(TERMS_CHUNKS, n_chunks)
    rows, c, nh = g * DN_CHUNK, DN_CHUNK, DN_HEADS

    def body(q_ref, k_ref, v_ref, gb_ref, t_ref, dvn_ref, dw_ref, dkt_ref, dqg_ref, dqk_ref, dgl_ref, dq_ref, dk_ref, dv_ref, dgb_ref):
        gbv = gb_ref[...]
        q, k, vv = _to_heads(q_ref[...], g), _to_heads(k_ref[...], g), _to_heads(v_ref[...], g)
        beta = _head_columns(gbv, nh, g)
        t = jnp.concatenate([t_ref[:, h] for h in range(nh)], axis=0)
        x = _wy_terms(q, k, vv, _head_columns(gbv, 0, g), beta, t=t)
        ii, jj, strict = x["ii"], x["jj"], x["strict"]
        du, dwv, dktv, dqgv = (_to_heads(r[...], g) for r in (dvn_ref, dw_ref, dkt_ref, dqg_ref))
        dqkv = jnp.concatenate([dqk_ref[:, h] for h in range(nh)], axis=0)
        dglv = jnp.concatenate([dgl_ref[:, h] for h in range(nh)], axis=0)[:, :, 0:1]
        dt = _bmdot(du, x["vb"], "nt") + _bmdot(dwv, x["kbg"], "nt")
        dvb = _bmdot(t, du, "tn")
        dkbg = _bmdot(t, dwv, "tn")
        tt_dt = _bdot(t, dt, "tn", precision=HI)
        da = jnp.where(strict, -_bdot(tt_dt, t, "nt", precision=HI), 0.0)
        dkk = da * x["dec"]
        dqk0 = dqkv * x["dec"]
        e = (da * x["kk"] + dqkv * x["qk0"]) * x["dec"]
        dkb = _bmdot(dkk, k) + dkbg * x["eg"]
        dk = _bmdot(dkk, x["kb"], "tn") + _bmdot(dqk0, q, "tn") + dktv * x["e2"] + dkb * beta
        dq = _bmdot(dqk0, k) + dqgv * x["eg"]
        s_kt = jnp.sum(dktv * x["kt"], axis=2, keepdims=True)
        dgc_c = (jnp.sum(e, axis=2, keepdims=True) + jnp.sum(dqgv * x["qg"], axis=2, keepdims=True) - s_kt
                 + jnp.sum(dkbg * x["kbg"], axis=2, keepdims=True))
        dgc_last = jnp.sum(s_kt, axis=1, keepdims=True) + dglv * x["gl"]
        rid = lax.broadcasted_iota(jnp.int32, (1, c, 1), 1)
        dgc_c = dgc_c + jnp.where(rid == c - 1, dgc_last, 0.0)
        dgc_r = jnp.sum(jnp.where(ii == jj, dgc_c, 0.0), axis=1, keepdims=True) - jnp.sum(e, axis=1, keepdims=True)
        dg = jnp.sum(jnp.where(jj >= ii, dgc_r, 0.0), axis=2, keepdims=True)
        dbeta = jnp.sum(dkb * k, axis=2, keepdims=True) + jnp.sum(dvb * vv, axis=2, keepdims=True)
        _from_heads(dq, dq_ref, g)
        _from_heads(dk, dk_ref, g)
        _from_heads(dvb * beta, dv_ref, g)
        lane = lax.broadcasted_iota(jnp.int32, gbv.shape, 1)
        dgb = jnp.zeros(gbv.shape, F32)
        for h in range(nh):
            dgb = dgb + jnp.where(lane == h, dg[h * g:(h + 1) * g].reshape(rows, 1), 0.0)
            dgb = dgb + jnp.where(lane == nh + h, dbeta[h * g:(h + 1) * g].reshape(rows, 1), 0.0)
        dgb_ref[...] = dgb

    row = pl.BlockSpec((rows, GROUP_WIDTH), lambda i: (i, 0))
    abrow = pl.BlockSpec((rows, AB_PAD), lambda i: (i, 0))
    sq = pl.BlockSpec((g, nh, c, c), lambda i: (i, 0, 0, 0))
    glb = pl.BlockSpec((g, nh, 1, 128), lambda i: (i, 0, 0, 0))
    return pl.pallas_call(
        body, name=name, grid=(n_chunks // g,), in_specs=[row, row, row, abrow, sq, row, row, row, row, sq, glb],
        out_specs=[row, row, row, abrow],
        out_shape=[jax.ShapeDtypeStruct((l, GROUP_WIDTH), F32)] * 3 + [jax.ShapeDtypeStruct((l, AB_PAD), F32)],
        compiler_params=_params(("parallel",)),
    )(qn, kn, v, gb, t_inv, dvn, dw, dkt, dqg, dqk, dgl)


def _gdn_post_fwd(o, proj, norm_g4, name):
    l, gw = o.shape
    tl = min(512, l)

    def body(o_ref, gate_ref, g_ref, out_ref):
        ov = o_ref[...]
        r = lax.rsqrt(_head_sums(ov * ov) * (1.0 / DN_HEAD_DIM) + RMS_EPS)
        out_ref[...] = (ov * r * g_ref[...] * _silu(gate_ref[...])).astype(out_ref.dtype)

    row = pl.BlockSpec((tl, gw), lambda i: (i, 0))
    return pl.pallas_call(
        body, name=name, grid=(l // tl,),
        in_specs=[row, pl.BlockSpec((tl, gw), lambda i: (i, 7)), pl.BlockSpec((1, gw), lambda i: (0, 0))],
        out_specs=row, out_shape=jax.ShapeDtypeStruct((l, gw), BF16), compiler_params=_params(("parallel",)),
    )(o, proj, norm_g4)


def _gdn_post_bwd(dmixed, o, proj, norm_g4, name):
    l, gw = o.shape
    tl = min(512, l)

    def body(d_ref, o_ref, gate_ref, g_ref, do_ref, dgate_ref, dng_ref):
        @pl.when(pl.program_id(0) == 0)
        def _():
            dng_ref[...] = jnp.zeros_like(dng_ref)

        ov, gate, d = o_ref[...], gate_ref[...], d_ref[...]
        r = lax.rsqrt(_head_sums(ov * ov) * (1.0 / DN_HEAD_DIM) + RMS_EPS)
        oh = ov * r
        sg = _silu(gate)
        dgate_ref[...] = (d * oh * g_ref[...] * _silu_grad(gate)).astype(dgate_ref.dtype)
        dng_ref[...] += jnp.sum(d * sg * oh, axis=0, keepdims=True)
        doh = d * g_ref[...] * sg
        do_ref[...] = r * (doh - oh * _head_sums(doh * oh) * (1.0 / DN_HEAD_DIM))

    row = pl.BlockSpec((tl, gw), lambda i: (i, 0))
    vec = pl.BlockSpec((1, gw), lambda i: (0, 0))
    return pl.pallas_call(
        body, name=name, grid=(l // tl,),
        in_specs=[pl.BlockSpec((tl, gw), lambda i: (i, 3)), row, pl.BlockSpec((tl, gw), lambda i: (i, 7)), vec],
        out_specs=[row, row, vec],
        out_shape=[jax.ShapeDtypeStruct((l, gw), F32), jax.ShapeDtypeStruct((l, gw), BF16), jax.ShapeDtypeStruct((1, gw), F32)],
        compiler_params=_params(("arbitrary",)),
    )(dmixed, o, proj, norm_g4)


def _run(hosts, name, fn):
    h = hosts.get(name)
    if h is None:
        return fn(None)
    res, outs = fn(h[0]())
    h[1](outs)
    return res


def _layer_fwd(x, xm, w, li, hosts):
    l = x.shape[0]
    nm = f"l{li}_"
    proj = _run(hosts, nm + "proj", lambda ops: _matmul(
        xm, w["w_main"], mode="nn", tm=1024, tn=1024, tk=2048,out_dtype=F32, name=nm + "proj", comm=ops))
    proj_ab = _matmul(xm, w["w_ab"], mode="nn", tm=1024, tn=AB_PAD, tk=2048, out_dtype=F32, name=nm + "proj_ab")
    bu = _matmul(proj, w["s5_bmat"], mode="nn", tm=1024, tn=1024, tk=GROUP_WIDTH, out_dtype=F32, name=nm + "s5_bu",
                 a_cols=GROUP_WIDTH)
    hs = _s5_scan(bu, w["s5_lam"], reverse=False, name=nm + "s5_scan")
    y = _matmul(hs, w["s5_cmat"], mode="nn", tm=1024, tn=GROUP_WIDTH, tk=1024, out_dtype=F32, name=nm + "s5_y",
                extras=[(proj, (None, GROUP_WIDTH), lambda i, j: (i, 0)), (w["s5_d"], (1, GROUP_WIDTH), lambda i, j: (0, 0))],
                epi=lambda acc, u, d: acc + d * u)
    m_s5 = _s5_glu_fwd(y, w["s5_glu_w"], w["s5_glu_b"], nm + "s5_glu")
    m_sgu = _sgu_fwd(proj, w["sgu_norm_g"], w["sgu_norm_b"], w["sgu_wm"], w["sgu_bfull"], nm + "sgu")
    m_pool, pooled = _pool_fwd(proj, w["pool_w"], w["pool_scale"], nm + "pool")
    qn, kn, v, cq, ck, cv, gb = _gdn_pre_fwd(proj, proj_ab, w["dn_conv_w"], w["dn_a_log"], w["dn_dt_bias"], nm + "gdn_pre")
    u, wy, qg, kt, qk, t_inv, gl = _gdn_terms_fwd(qn, kn, v, gb, nm + "gdn_terms")
    o, vn, states = _gdn_rec_fwd(u, wy, qg, kt, qk, gl, nm + "gdn_rec")
    m_dn = _gdn_post_fwd(o, proj, w["dn_norm_g4"], nm + "gdn_post")
    mixed = jnp.concatenate([m_s5, m_sgu, m_pool, m_dn], axis=1)
    y1 = _matmul(mixed, w["w_out"], mode="nn", tm=1024, tn=1024, tk=1024,out_dtype=F32, name=nm + "out_proj")
    h1, x1, x1m = _ln_fwd(x, y1, w["ln1_g"], w["ln1_b"], nm + "ln1")
    r = _run(hosts, nm + "up", lambda ops: _matmul(
        x1m, w["w_up"], mode="nn", tm=1024, tn=1024, tk=2048,out_dtype=BF16, name=nm + "up",
        epi=lambda acc: jnp.maximum(acc, 0.0), b_slab=w["w_up"].shape[2], comm=ops))
    y2 = _run(hosts, nm + "down", lambda ops: _matmul(
        r, w["w_down"], mode="nn", tm=1024, tn=1024, tk=2048,out_dtype=F32, name=nm + "down", a_fn=lambda a: a * a, comm=ops))
    h2, x2, x2m = _ln_fwd(x1, y2, w["ln2_g"], w["ln2_b"], nm + "ln2")
    saved = dict(xm=xm, proj=proj, proj_ab=proj_ab, hs=hs, y=y, pooled=pooled, qn=qn, kn=kn, v=v, cq=cq, ck=ck, cv=cv, gb=gb,
                 wy=wy, qg=qg, kt=kt, qk=qk, t_inv=t_inv, gl=gl, vn=vn, o=o, states=states, mixed=mixed, h1=h1, x1m=x1m,
                 r=r, h2=h2)
    return x2, x2m, saved


def _layer_bwd(dx2, s, w, small, li, hosts, g):
    nm = f"l{li}b_"
    l = dx2.shape[0]
    gw = GROUP_WIDTH
    wire = MXU_DTYPE
    dh2, dh2m, g["ln2_g"], g["ln2_b"] = _ln_bwd(dx2, s["h2"], w["ln2_g"], nm + "ln2")
    g["w_down"] = _run(hosts, nm + "dw_down", lambda ops: _matmul(
        s["r"], dh2m, mode="tn", tm=1024, tn=1024, tk=2048,out_dtype=wire, name=nm + "dw_down", a_fn=lambda a: a * a,
        comm=ops)).reshape(N_DEV, D_FF // N_DEV, D_MODEL)
    dpre = _run(hosts, nm + "dpre", lambda ops: _matmul(
        dh2m, w["w_down"], mode="nt", tm=1024, tn=1024, tk=2048,out_dtype=BF16, name=nm + "dpre",
        extras=[(s["r"], (None, None), lambda i, j: (i, j))], epi=lambda acc, r: acc * 2.0 * r.astype(F32), comm=ops))
    g["w_up"] = _matmul(s["x1m"], dpre, mode="tn", tm=1024, tn=1024, tk=2048,out_dtype=wire, name=nm + "dw_up",
                        out_slab=D_FF // N_DEV)
    dx1 = _run(hosts, nm + "dx1", lambda ops: _matmul(
        dpre, w["w_up"], mode="nt", tm=1024, tn=1024, tk=2048,out_dtype=F32, name=nm + "dx1",
        extras=[(dh2, (None, None), lambda i, j: (i, j))], epi=lambda acc, e: acc + ALPHA * e,
        b_slab=w["w_up"].shape[2], comm=ops))
    dh1, dh1m, g["ln1_g"], g["ln1_b"] = _ln_bwd(dx1, s["h1"], w["ln1_g"], nm + "ln1")
    g["w_out"] = _matmul(s["mixed"], dh1m, mode="tn", tm=1024, tn=1024, tk=2048,out_dtype=wire,
                         name=nm + "dw_out").reshape(N_DEV, D_MODEL // N_DEV, D_MODEL)
    dmixed = _run(hosts, nm + "dmixed", lambda ops: _matmul(
        dh1m, w["w_out"], mode="nt", tm=1024, tn=1024, tk=2048,out_dtype=F32, name=nm + "dmixed", comm=ops))
    proj, proj_ab = s["proj"], s["proj_ab"]
    dy, dz, yg, g["s5_glu_b"], g["s5_d"] = _s5_glu_bwd(dmixed, s["y"], proj, w["s5_glu_w"], w["s5_glu_b"], nm + "s5_glu")
    g["s5_glu_w"] = _matmul(yg, dz, mode="tn", tm=gw, tn=gw, tk=1024, out_dtype=wire,
                            name=nm + "dw_glu").reshape(N_DEV, gw // N_DEV, gw)
    dhs = _matmul(dy, w["s5_cmat"], mode="nt", tm=1024, tn=1024, tk=gw, out_dtype=F32, name=nm + "s5_dh")
    adj, g["s5_lam"] = _s5_scan(dhs, w["s5_lam_conj"], reverse=True, h=s["hs"], name=nm + "s5_scan")
    g["s5_cmat"] = _matmul(s["hs"], dy, mode="tn", tm=1024, tn=gw, tk=1024, out_dtype=F32, name=nm + "s5_dc")
    g["s5_bmat"] = _matmul(proj, adj, mode="tn", tm=gw, tn=1024, tk=1024, out_dtype=F32, name=nm + "s5_db", a_cols=gw)
    du_s5 = _matmul(adj, w["s5_bmat"], mode="nt", tm=1024, tn=gw, tk=1024, out_dtype=BF16, name=nm + "s5_du",
                    extras=[(dy, (None, gw), lambda i, j: (i, 0)), (w["s5_d"], (1, gw), lambda i, j: (0, 0))],
                    epi=lambda acc, dyv, d: acc + d * dyv)
    dzu, dzv, g["sgu_w"], g["sgu_bfull"], g["sgu_norm_g"], g["sgu_norm_b"] = _sgu_bwd(
        dmixed, proj, w["sgu_norm_g"], w["sgu_norm_b"], w["sgu_wm"], w["sgu_bfull"], nm + "sgu")
    dpooled, g["pool_w"], g["pool_scale"] = _pool_bwd_map(dmixed, s["pooled"], w["pool_w"], w["pool_scale"], nm + "pool_map")
    dp = _pool_bwd_window(dpooled, nm + "pool_win")
    do, dgate, g["dn_norm_g4"] = _gdn_post_bwd(dmixed, s["o"], proj, w["dn_norm_g4"], nm + "gdn_post")
    dvn, dwy, dkt, dqg, dqk, dgl = _gdn_rec_bwd(do, s["wy"], s["qg"], s["kt"], s["vn"], s["qk"], s["gl"], s["states"], nm + "gdn_rec")
    dq, dk, dv, dgb = _gdn_terms_bwd(s["qn"], s["kn"], s["v"], s["gb"], s["t_inv"], dvn, dwy, dkt, dqg, dqk, dgl, nm + "gdn_terms")
    dcq, dck, dcv, dab, g["dn_a_log"], g["dn_dt_bias"] = _gdn_pre_bwd(
        dq, dk, dv, s["cq"], s["ck"], s["cv"], dgb, s["gb"], proj_ab, w["dn_a_log"], w["dn_dt_bias"], nm + "gdn_pre")
    dxs, dws = [], []
    for p, dc in enumerate((dcq, dck, dcv)):
        dxp, dwp = _conv_bwd(dc, proj, QKV_BLK + p, w["dn_conv_w"][:, p * gw:(p + 1) * gw], nm + f"conv{p}")
        dxs.append(dxp)
        dws.append(dwp[:DN_CONV])
    dconv = jnp.concatenate(dws, axis=1)
    g["dn_conv_w"] = jnp.transpose(dconv.reshape(DN_CONV, N_DEV, 3 * gw // N_DEV), (1, 0, 2))
    dproj = jnp.concatenate([du_s5, dzu, dzv, dp] + dxs + [dgate], axis=1)
    xm = s["xm"]
    g["small"] = _unprep_grads(g, small)
    dw_main = _run(hosts, nm + "dw_main", lambda ops: _matmul(
        xm, dproj, mode="tn", tm=1024, tn=1024, tk=2048,out_dtype=wire, name=nm + "dw_main", comm=ops))
    dw_ab = _matmul(xm, dab, mode="tn", tm=1024, tn=AB_PAD, tk=1024, out_dtype=wire, name=nm + "dw_ab")
    dw_in = jnp.concatenate([dw_main, dw_ab[:, :2 * DN_HEADS]], axis=1)
    g["w_in"] = jnp.transpose(dw_in.reshape(D_MODEL, N_DEV, dw_in.shape[1] // N_DEV), (1, 0, 2))
    dx_ab = _matmul(dab, w["w_ab"], mode="nt", tm=1024, tn=1024, tk=AB_PAD, out_dtype=F32, name=nm + "dx_ab",
                    extras=[(dh1, (None, None), lambda i, j: (i, j))], epi=lambda acc, e: acc + ALPHA * e)
    return _run(hosts, nm + "dx", lambda ops: _matmul(
        dproj, w["w_main"], mode="nt", tm=1024, tn=1024, tk=2048,out_dtype=F32, name=nm + "dx",
        extras=[(dx_ab, (None, None), lambda i, j: (i, j))], epi=lambda acc, e: acc + e, comm=ops))


SMALL = ("s5_lambda_re", "s5_lambda_im", "s5_log_step", "s5_b_re", "s5_b_im", "s5_c_re", "s5_c_im", "s5_d", "s5_glu_b",
         "sgu_norm_g", "sgu_norm_b", "sgu_w", "sgu_b", "pool_w", "pool_scale", "dn_a_log", "dn_dt_bias", "dn_norm_g",
         "ln1_g", "ln1_b", "ln2_g", "ln2_b")
SHARDED = ("w_in", "s5_glu_w", "dn_conv_w", "w_out", "w_up", "w_down")


def _pad_lanes(v, width=AB_PAD):
    return jnp.pad(v.reshape(1, -1), ((0, 0), (0, width - v.size)))


def _prep_small(p):
    mx = MXU_DTYPE
    lbr, lbi, bbr, bbi = _s5_discretize(p["s5_lambda_re"], p["s5_lambda_im"], p["s5_log_step"], p["s5_b_re"], p["s5_b_im"])
    bmat, cmat = _s5_dense(bbr, bbi, p["s5_c_re"], p["s5_c_im"])
    causal = jnp.tril(jnp.ones((SGU_CHUNK, SGU_CHUNK), F32))
    return dict(
        s5_bmat=bmat.astype(mx), s5_cmat=cmat.astype(mx),
        s5_lam=jnp.concatenate([lbr.reshape(1, -1), lbi.reshape(1, -1)], axis=1),
        s5_lam_conj=jnp.concatenate([lbr.reshape(1, -1), -lbi.reshape(1, -1)], axis=1),
        s5_d=p["s5_d"].reshape(1, -1), s5_glu_b=p["s5_glu_b"].reshape(1, -1),
        sgu_norm_g=p["sgu_norm_g"].reshape(1, -1), sgu_norm_b=p["sgu_norm_b"].reshape(1, -1),
        sgu_wm=(p["sgu_w"] * causal).astype(mx), sgu_bfull=jnp.repeat(p["sgu_b"].T, GROUP_WIDTH // SGU_HEADS, axis=1),
        pool_w=p["pool_w"].astype(mx), pool_scale=p["pool_scale"].reshape(1, -1),
        dn_a_log=_pad_lanes(p["dn_a_log"]), dn_dt_bias=_pad_lanes(p["dn_dt_bias"]),
        dn_norm_g4=jnp.tile(p["dn_norm_g"].reshape(1, -1), (1, DN_HEADS)),
        ln1_g=p["ln1_g"].reshape(1, -1), ln1_b=p["ln1_b"].reshape(1, -1),
        ln2_g=p["ln2_g"].reshape(1, -1), ln2_b=p["ln2_b"].reshape(1, -1),
    )


def _weight_views(name, t):
    if name == "w_in":
        w_in = jnp.transpose(t, (1, 0, 2)).reshape(t.shape[1], N_DEV * t.shape[2])
        pad = AB_PAD - (w_in.shape[1] - MAIN_COLS)
        return dict(w_main=w_in[:, :MAIN_COLS], w_ab=jnp.pad(w_in[:, MAIN_COLS:], ((0, 0), (0, pad))))
    if name == "dn_conv_w":
        return dict(dn_conv_w=jnp.transpose(t, (1, 0, 2)).reshape(t.shape[1], N_DEV * t.shape[2]))
    if name == "w_up":
        return dict(w_up=t)
    return {name: t.reshape(N_DEV * t.shape[1], t.shape[2])}


def _unprep_grads(g, p):
    causal = jnp.tril(jnp.ones((SGU_CHUNK, SGU_CHUNK), F32))
    dbbr, dbbi = _s5_undense_b(g["s5_bmat"])
    dc_re, dc_im = _s5_undense_c(g["s5_cmat"])
    dlbr, dlbi = g["s5_lam"][0, :S5_NS].reshape(S5_GROUPS, S5_STATE), g["s5_lam"][0, S5_NS:].reshape(S5_GROUPS, S5_STATE)
    _, vjp = jax.vjp(_s5_discretize, p["s5_lambda_re"], p["s5_lambda_im"], p["s5_log_step"], p["s5_b_re"], p["s5_b_im"])
    d_lre, d_lim, d_step, d_bre, d_bim = vjp((dlbr, dlbi, dbbr, dbbi))
    hd = GROUP_WIDTH // SGU_HEADS
    return dict(
        s5_lambda_re=d_lre, s5_lambda_im=d_lim, s5_log_step=d_step, s5_b_re=d_bre, s5_b_im=d_bim, s5_c_re=dc_re, s5_c_im=dc_im,
        s5_d=g["s5_d"].reshape(S5_GROUPS, S5_CH), s5_glu_b=g["s5_glu_b"].reshape(-1),
        sgu_norm_g=g["sgu_norm_g"].reshape(-1), sgu_norm_b=g["sgu_norm_b"].reshape(-1), sgu_w=g["sgu_w"] * causal,
        sgu_b=jnp.sum(g["sgu_bfull"].reshape(SGU_CHUNK, SGU_HEADS, hd), axis=2).T,
        pool_w=g["pool_w"], pool_scale=g["pool_scale"].reshape(-1),
        dn_a_log=g["dn_a_log"][0, :DN_HEADS], dn_dt_bias=g["dn_dt_bias"][0, :DN_HEADS],
        dn_norm_g=jnp.sum(g["dn_norm_g4"].reshape(DN_HEADS, DN_HEAD_DIM), axis=0),
        ln1_g=g["ln1_g"].reshape(-1), ln1_b=g["ln1_b"].reshape(-1), ln2_g=g["ln2_g"].reshape(-1), ln2_b=g["ln2_b"].reshape(-1),
    )


def _local_step(x, target, ops, small, fwd_hosts, bwd_hosts, grads):
    saved = []
    h, hm = x, x.astype(MXU_DTYPE)
    for i in range(DEPTH):
        h, hm, s = _layer_fwd(h, hm, ops[i], i, fwd_hosts)
        saved.append(s)
    loss, dh = _loss_head(h, target)
    for i in reversed(range(DEPTH)):
        dh = _layer_bwd(dh, saved[i], ops[i], small[i], i, bwd_hosts, grads[i])
    return loss, dh


def _adamw(w, gparts, m, v, name):
    rr, c = w.shape
    ng = len(gparts)
    r = rr // ng
    lanes = -(-c // 128) * 128
    tr = r
    while tr * lanes * 4 * N_DEV > (4 << 20) and tr % 16 == 0:
        tr //= 2
    nb = r // tr

    def body(w_ref, *rest):
        g_refs, (m_ref, v_ref, go_ref, d_ref, mo_ref, vo_ref) = rest[:ng], rest[ng:]
        layer = pl.program_id(0)
        g = jnp.zeros(m_ref.shape, F32)
        for li in range(ng):
            gl = g_refs[li][0].astype(F32)
            for s in range(1, N_DEV):
                gl = gl + g_refs[li][s].astype(F32)
            g = jnp.where(layer == li, gl, g)
        mn = ADAM_B1 * m_ref[...] + (1.0 - ADAM_B1) * g
        vn = ADAM_B2 * v_ref[...] + (1.0 - ADAM_B2) * g * g
        m_hat = mn / (1.0 - ADAM_B1 ** ADAM_STEP)
        v_hat = vn / (1.0 - ADAM_B2 ** ADAM_STEP)
        go_ref[...] = g
        d_ref[...] = -ADAM_LR * (m_hat / (jnp.sqrt(v_hat) + ADAM_EPS) + ADAM_WD * w_ref[...])
        mo_ref[...] = mn
        vo_ref[...] = vn

    row = pl.BlockSpec((tr, c), lambda li, i: (li * nb + i, 0))
    part_specs = [pl.BlockSpec((N_DEV, tr, c), functools.partial(lambda li, i, k: (0, jnp.where(li == k, i, 0), 0), k=k))
                  for k in range(ng)]
    return pl.pallas_call(
        body, name=name, grid=(ng, nb), in_specs=[row] + part_specs + [row, row],
        out_specs=[row] * 4, out_shape=[jax.ShapeDtypeStruct((rr, c), F32)] * 4, compiler_params=_params(("arbitrary", "arbitrary")),
    )(w, *gparts, m, v)


PACK_LANES = 128
PACK_ROWS = 4096


def _pack(vals):
    rows = []
    for t in vals:
        flat = t.reshape(-1)
        n_rows = -(-flat.size // PACK_LANES)
        rows.append(jnp.pad(flat, (0, n_rows * PACK_LANES - flat.size)).reshape(n_rows, PACK_LANES))
    used = sum(r.shape[0] for r in rows)
    assert used <= PACK_ROWS, used
    return jnp.concatenate(rows + [jnp.zeros((PACK_ROWS - used, PACK_LANES), F32)], axis=0)


def _unpack(packed, like):
    out, off = [], 0
    for t in like:
        n_rows = -(-t.size // PACK_LANES)
        out.append(packed[off:off + n_rows].reshape(-1)[:t.size].reshape(t.shape))
        off += n_rows
    return out


def kernel(x, w_in, s5_lambda_re, s5_lambda_im, s5_log_step, s5_b_re, s5_b_im, s5_c_re, s5_c_im, s5_d, s5_glu_w, s5_glu_b, sgu_norm_g, sgu_norm_b, sgu_w, sgu_b, pool_w, pool_scale, dn_conv_w, dn_a_log, dn_dt_bias, dn_norm_g, w_out, ln1_g, ln1_b, w_up, w_down, ln2_g, ln2_b, loss_target, m_w_in, m_s5_lambda_re, m_s5_lambda_im, m_s5_log_step, m_s5_b_re, m_s5_b_im, m_s5_c_re, m_s5_c_im, m_s5_d, m_s5_glu_w, m_s5_glu_b, m_sgu_norm_g, m_sgu_norm_b, m_sgu_w, m_sgu_b, m_pool_w, m_pool_scale, m_dn_conv_w, m_dn_a_log, m_dn_dt_bias, m_dn_norm_g, m_w_out, m_ln1_g, m_ln1_b, m_w_up, m_w_down, m_ln2_g, m_ln2_b, v_w_in, v_s5_lambda_re, v_s5_lambda_im, v_s5_log_step, v_s5_b_re, v_s5_b_im, v_s5_c_re, v_s5_c_im, v_s5_d, v_s5_glu_w, v_s5_glu_b, v_sgu_norm_g, v_sgu_norm_b, v_sgu_w, v_sgu_b, v_pool_w, v_pool_scale, v_dn_conv_w, v_dn_a_log, v_dn_dt_bias, v_dn_norm_g, v_w_out, v_ln1_g, v_ln1_b, v_w_up, v_w_down, v_ln2_g, v_ln2_b):
    names = ("w_in", "s5_lambda_re", "s5_lambda_im", "s5_log_step", "s5_b_re", "s5_b_im", "s5_c_re", "s5_c_im", "s5_d", "s5_glu_w",
             "s5_glu_b", "sgu_norm_g", "sgu_norm_b", "sgu_w", "sgu_b", "pool_w", "pool_scale", "dn_conv_w", "dn_a_log", "dn_dt_bias",
             "dn_norm_g", "w_out", "ln1_g", "ln1_b", "w_up", "w_down", "ln2_g", "ln2_b")
    env = locals()
    w = {n: env[n] for n in names}
    m = {n: env["m_" + n] for n in names}
    v = {n: env["v_" + n] for n in names}

    wire = [{n: (w[n][i] if n == "dn_conv_w" else w[n][i].astype(MXU_DTYPE)) for n in SHARDED} for i in range(DEPTH)]
    small = [{n: w[n][i] for n in SMALL} for i in range(DEPTH)]
    ops = [_prep_small(small[i]) for i in range(DEPTH)]
    grads = [{} for _ in range(DEPTH)]
    recv = [{} for _ in range(DEPTH)]
    first = ("w_in", "s5_glu_w", "dn_conv_w", "w_out")

    def gather(layer, group):
        def take(outs):
            for n, t in zip(group, outs):
                ops[layer].update(_weight_views(n, t))
        return (lambda: [(wire[layer][n], False) for n in group]), take

    def scatter(layer, group, with_small=False):
        def make():
            sends = [(grads[layer][n], True) for n in group]
            if with_small:
                sends.append((_pack([grads[layer]["small"][n] for n in SMALL]), False))
            return sends
        def take(outs):
            recv[layer].update(dict(zip(group + (("small",) if with_small else ()), outs)))
        return make, take

    make, take = gather(0, first)
    take(_exchange(make(), "gather_first"))
    fwd_hosts = {"l0_proj": gather(0, ("w_up",)), "l0_up": gather(0, ("w_down",)), "l0_down": gather(1, first),
                 "l1_proj": gather(1, ("w_up",)), "l1_up": gather(1, ("w_down",))}
    late = ("w_in", "s5_glu_w", "dn_conv_w")
    bwd_hosts = {"l1b_dpre": scatter(1, ("w_down",)), "l1b_dx1": scatter(1, ("w_up",)), "l1b_dmixed": scatter(1, ("w_out",)),
                 "l0b_dw_down": scatter(1, late, with_small=True),
                 "l0b_dpre": scatter(0, ("w_down",)), "l0b_dx1": scatter(0, ("w_up",)), "l0b_dmixed": scatter(0, ("w_out",)),
                 "l0b_dw_main": scatter(0, ("s5_glu_w", "dn_conv_w"), with_small=True), "l0b_dx": scatter(0, ("w_in",))}
    loss, grad_x = _local_step(x[0], loss_target[0], ops, small, fwd_hosts, bwd_hosts, grads)

    g_out, d_out, m_out, v_out = {}, {}, {}, {}
    for n in SHARDED:
        shp = w[n].shape
        two_d = (shp[0] * shp[1], shp[2])
        parts = [recv[i][n] for i in range(DEPTH)]
        if shp[1] % 8:
            parts = [jnp.concatenate(parts, axis=1)]
        res = _adamw(w[n].reshape(two_d), parts, m[n].reshape(two_d), v[n].reshape(two_d), "adamw_" + n)
        g_out[n], d_out[n], m_out[n], v_out[n] = (t.reshape(shp) for t in res)
    def packed(src):
        return jnp.concatenate([_pack([src[n][i] for n in SMALL]) for i in range(DEPTH)], axis=0)
    res = _adamw(packed(w), [recv[i]["small"] for i in range(DEPTH)], packed(m), packed(v), "adamw_small")
    like = [w[n][0] for n in SMALL]
    for dst, pk in zip((g_out, d_out, m_out, v_out), res):
        per_layer = [_unpack(pk[i * PACK_ROWS:(i + 1) * PACK_ROWS], like) for i in range(DEPTH)]
        dst.update({n: jnp.stack([per_layer[i][k] for i in range(DEPTH)]) for k, n in enumerate(SMALL)})

    total = lax.psum(loss[0, 0], MESH_AXES)
    return (total, grad_x[None], *[g_out[n] for n in names], *[d_out[n] for n in names],
            *[m_out[n] for n in names], *[v_out[n] for n in names])
```

```python
import functools
import math

import jax
import jax.numpy as jnp
from jax import lax
from jax.experimental import pallas as pl
from jax.experimental.pallas import tpu as pltpu

F32 = jnp.float32
BF16 = jnp.bfloat16
MXU_DTYPE = jnp.bfloat16
HI = lax.Precision.HIGHEST

N_DEV = 8
D_MODEL = 2048
DEPTH = 2
GROUP_WIDTH = 512
S5_GROUPS, S5_CH, S5_STATE = 32, 16, 64
S5_NS = S5_GROUPS * S5_STATE
SGU_CHUNK, SGU_HEADS = 128, 8
POOL_WINDOWS = (2, 4, 8, 16)
DN_HEADS, DN_HEAD_DIM, DN_CONV, DN_CHUNK = 4, 128, 4, 64
D_FF = 4 * D_MODEL
LN_EPS, RMS_EPS, L2_EPS = 1e-5, 1e-6, 1e-6
ALPHA = (2 * DEPTH) ** 0.25
MAIN_COLS = 4096
AB_PAD = 128
ADAM_LR, ADAM_B1, ADAM_B2, ADAM_EPS, ADAM_WD, ADAM_STEP = 0.001, 0.9, 0.999, 1e-08, 0.01, 10
VMEM_LIMIT = 56 * 1024 * 1024
C_GELU = math.sqrt(2.0 / math.pi)


def _params(sem=None):
    return pltpu.CompilerParams(dimension_semantics=sem, vmem_limit_bytes=VMEM_LIMIT)


def _gelu(x):
    return 0.5 * x * (1.0 + jnp.tanh(C_GELU * (x + 0.044715 * x * x * x)))


def _gelu_grad(x):
    t = jnp.tanh(C_GELU * (x + 0.044715 * x * x * x))
    return 0.5 * (1.0 + t) + 0.5 * x * (1.0 - t * t) * C_GELU * (1.0 + 3.0 * 0.044715 * x * x)


def _sigmoid(x):
    return 1.0 / (1.0 + jnp.exp(-x))


def _silu(x):
    return x * _sigmoid(x)


def _silu_grad(x):
    s = _sigmoid(x)
    return s * (1.0 + x * (1.0 - s))


def _softplus(x):
    z = jnp.exp(-jnp.abs(x))
    small = z * (1.0 - z * (0.5 - z * (1.0 / 3.0)))
    return jnp.maximum(x, 0.0) + jnp.where(z < 1e-2, small, jnp.log(1.0 + z))


def _mx(x):
    return x.astype(MXU_DTYPE)


def _dot(a, b, dims="nn", precision=None):
    cd = {"nn": ((1,), (0,)), "nt": ((1,), (1,)), "tn": ((0,), (0,))}[dims]
    return lax.dot_general(a, b, (cd, ((), ())), preferred_element_type=F32, precision=precision)


def _mdot(a, b, dims="nn"):
    return _dot(_mx(a), _mx(b), dims)


MESH_AXES = ("x", "y", "c")
OFFSETS = [(dx, dy, dc) for dx in (0, 1) for dy in (0, 1) for dc in (0, 1)][1:]


def _me_and_peers():
    x, y, c = (lax.axis_index(a) for a in MESH_AXES)
    def flip(v, d):
        return 1 - v if d else v
    peers = [(flip(x, dx), flip(y, dy), flip(c, dc)) for dx, dy, dc in OFFSETS]
    def idx(p):
        return 4 * p[0] + 2 * p[1] + p[2]
    return idx((x, y, c)), peers, [idx(p) for p in peers]


SIBLING = OFFSETS.index((0, 0, 1))
SAME_CORE = [OFFSETS.index(f) for f in ((0, 1, 0), (1, 0, 0), (1, 1, 0))]


class _Comm:
    def __init__(self, ops):
        self.arrays = [a for a, _ in ops]
        self.scatter = [s for _, s in ops]
        self.n = n = len(ops)
        hbm = pl.BlockSpec(memory_space=pltpu.HBM)
        self.in_specs, self.out_specs = [hbm] * n, [hbm] * n
        self.out_shape = [jax.ShapeDtypeStruct(a.shape if s else (N_DEV,) + a.shape, a.dtype) for a, s in ops]
        npeer = len(OFFSETS)
        self.scratch = [pltpu.SemaphoreType.DMA((n, npeer)), pltpu.SemaphoreType.DMA((n, npeer)), pltpu.SemaphoreType.DMA((n,))]

    def _plan(self, ins, outs, sems, waiting):
        send_sems, recv_sems, local_sems = sems
        me, peers, peer_idx = _me_and_peers()

        def remote(k, d, src, dst, to):
            return pltpu.make_async_remote_copy(src_ref=src, dst_ref=dst, send_sem=send_sems.at[k, d], recv_sem=recv_sems.at[k, d],
                                                device_id=to, device_id_type=pl.DeviceIdType.MESH)
        plan = []
        for k in range(self.n):
            every = range(len(OFFSETS))
            if self.scatter[k]:
                local = pltpu.make_async_copy(ins[k].at[me], outs[k].at[me], local_sems.at[k])
                pushes = [remote(k, d, ins[k].at[peer_idx[d]], outs[k].at[me], peers[d]) for d in every]
                onward = []
            else:
                local = pltpu.make_async_copy(ins[k], outs[k].at[me], local_sems.at[k])
                pushes = [remote(k, d, ins[k], outs[k].at[me], peers[d]) for d in [SIBLING] + SAME_CORE]
                onward = SAME_CORE
            passed, arrivals = [], {}
            if waiting:
                passed = [(d, remote(k, d + 1, outs[k].at[peer_idx[d]], outs[k].at[peer_idx[d]], peers[SIBLING])) for d in onward]
                arrivals = {d: remote(k, d, outs[k].at[peer_idx[d]], outs[k].at[peer_idx[d]], peers[d]) for d in every}
            plan.append((local, pushes, passed, arrivals))
        return plan

    def start(self, ins, outs, sems):
        for local, pushes, _, _ in self._plan(ins, outs, sems, False):
            local.start()
            for cp in pushes:
                cp.start()

    def wait(self, ins, outs, sems):
        plan = self._plan(ins, outs, sems, True)
        for _, _, passed, arrivals in plan:
            for d, onward in passed:
                arrivals.pop(d).wait_recv()
                onward.start()
        for local, pushes, passed, arrivals in plan:
            for cp in arrivals.values():
                cp.wait_recv()
            for cp in pushes + [onward for _, onward in passed]:
                cp.wait_send()
            local.wait()


def _exchange(ops, name):
    cm = _Comm(ops)

    def body(*refs):
        ins, outs, sems = refs[:cm.n], refs[cm.n:2 * cm.n], refs[2 * cm.n:]
        cm.start(ins, outs, sems)
        cm.wait(ins, outs, sems)

    return pl.pallas_call(body, name=name, in_specs=cm.in_specs, out_specs=cm.out_specs, out_shape=cm.out_shape,
                          scratch_shapes=cm.scratch)(*cm.arrays)


def _matmul(a, b, *, mode, tm, tn, tk, out_dtype, name, a_fn=None, extras=(), epi=None, a_cols=None,
            b_slab=None, out_slab=None, comm=None):
    a_shape = a.shape if a_cols is None else (a.shape[0], a_cols)
    b_shape = b.shape if b_slab is None else (b.shape[1], N_DEV * b_slab)
    if mode == "nn":
        (m, k), n = a_shape, b_shape[1]
    elif mode == "nt":
        (m, k), n = a_shape, b_shape[0]
    else:
        (k, m), n = a_shape, b_shape[1]
    tm, tn, tk = min(tm, m), min(tn, n), min(tk, k)
    if b_slab is not None:
        tn, tk = (tn, min(tk, b_slab)) if mode == "nt" else (min(tn, b_slab), tk)
    assert m % tm == 0 and n % tn == 0 and k % tk == 0, (name, a.shape, b.shape, tm, tn, tk)
    gi, gj, nk = m // tm, n // tn, k // tk
    n_ex = len(extras)
    cm = _Comm(comm) if comm else None
    nc = cm.n if cm else 0

    def body(a_ref, b_ref, *rest):
        ex_refs, rest = rest[:n_ex], rest[n_ex:]
        c_ins, o_ref, c_outs, acc, sems = rest[:nc], rest[nc], rest[nc + 1:2 * nc + 1], rest[2 * nc + 1], rest[2 * nc + 2:]
        i, j, kk = pl.program_id(0), pl.program_id(1), pl.program_id(2)
        if cm:
            @pl.when((i == 0) & (j == 0) & (kk == 0))
            def _():
                cm.start(c_ins, c_outs, sems)

        av = a_ref[...]
        if a_fn is not None:
            av = a_fn(av)
        part = _dot(_mx(av), _mx(b_ref[...]), mode)

        def finish(r):
            if epi is not None:
                r = epi(r, *[e[...] for e in ex_refs])
            o_ref[...] = r.astype(out_dtype)

        if nk == 1:
            finish(part)
        else:
            @pl.when(kk == 0)
            def _():
                acc[...] = part

            @pl.when((kk > 0) & (kk < nk - 1))
            def _():
                acc[...] += part

            @pl.when(kk == nk - 1)
            def _():
                finish(acc[...] + part)

        if cm:
            @pl.when((i == gi - 1) & (j == gj - 1) & (kk == nk - 1))
            def _():
                cm.wait(c_ins, c_outs, sems)

    a_spec = pl.BlockSpec((tk, tm), lambda i, j, kk: (kk, i)) if mode == "tn" else pl.BlockSpec((tm, tk), lambda i, j, kk: (i, kk))
    if b_slab is None:
        b_spec = pl.BlockSpec((tn, tk), lambda i, j, kk: (j, kk)) if mode == "nt" else pl.BlockSpec((tk, tn), lambda i, j, kk: (kk, j))
    elif mode == "nt":
        assert b_slab % tk == 0
        b_spec = pl.BlockSpec((None, tn, tk), lambda i, j, kk: ((kk * tk) // b_slab, j, ((kk * tk) % b_slab) // tk))
    else:
        assert b_slab % tn == 0
        b_spec = pl.BlockSpec((None, tk, tn), lambda i, j, kk: ((j * tn) // b_slab, kk, ((j * tn) % b_slab) // tn))
    if out_slab is None:
        o_spec, o_shape = pl.BlockSpec((tm, tn), lambda i, j, kk: (i, j)), jax.ShapeDtypeStruct((m, n), out_dtype)
    else:
        assert out_slab % tn == 0 and n == N_DEV * out_slab
        o_spec = pl.BlockSpec((None, tm, tn), lambda i, j, kk: ((j * tn) // out_slab, i, ((j * tn) % out_slab) // tn))
        o_shape = jax.ShapeDtypeStruct((N_DEV, m, out_slab), out_dtype)
    ex_specs = [pl.BlockSpec((tm if bs[0] is None else bs[0], tn if bs[1] is None else bs[1]),
                             functools.partial(lambda i, j, kk, f: f(i, j), f=im)) for (_, bs, im) in extras]
    res = pl.pallas_call(
        body,
        name=name,
        grid=(gi, gj, nk),
        in_specs=[a_spec, b_spec, *ex_specs] + (cm.in_specs if cm else []),
        out_specs=[o_spec] + (cm.out_specs if cm else []),
        out_shape=[o_shape] + (cm.out_shape if cm else []),
        scratch_shapes=[pltpu.VMEM((tm, tn) if nk > 1 else (8, 128), F32)] + (cm.scratch if cm else []),
        compiler_params=_params(("arbitrary",) * 3 if cm else ("parallel", "parallel", "arbitrary")),
    )(a, b, *[e[0] for e in extras], *(cm.arrays if cm else []))
    return (res[0], res[1:]) if cm else res[0]


def _ln_fwd(x, y, g, b, name):
    l, d = x.shape
    tl = 256

    def body(x_ref, y_ref, g_ref, b_ref, h_ref, o_ref, om_ref):
        h = ALPHA * x_ref[...] + y_ref[...]
        mu = jnp.mean(h, axis=-1, keepdims=True)
        c = h - mu
        var = jnp.mean(c * c, axis=-1, keepdims=True)
        h_ref[...] = h
        out = c * lax.rsqrt(var + LN_EPS) * g_ref[...] + b_ref[...]
        o_ref[...] = out
        om_ref[...] = out.astype(om_ref.dtype)

    row = pl.BlockSpec((tl, d), lambda i: (i, 0))
    vec = pl.BlockSpec((1, d), lambda i: (0, 0))
    return pl.pallas_call(
        body, name=name, grid=(l // tl,), in_specs=[row, row, vec, vec], out_specs=[row, row, row],
        out_shape=[jax.ShapeDtypeStruct((l, d), F32)] * 2 + [jax.ShapeDtypeStruct((l, d), MXU_DTYPE)],
        compiler_params=_params(("parallel",)),
    )(x, y, g, b)


def _ln_bwd(dout, h, g, name):
    l, d = h.shape
    tl = 256

    def body(do_ref, h_ref, g_ref, dh_ref, dhm_ref, dg_ref, db_ref):
        @pl.when(pl.program_id(0) == 0)
        def _():
            dg_ref[...] = jnp.zeros_like(dg_ref)
            db_ref[...] = jnp.zeros_like(db_ref)

        hv, do = h_ref[...], do_ref[...]
        mu = jnp.mean(hv, axis=-1, keepdims=True)
        c = hv - mu
        r = lax.rsqrt(jnp.mean(c * c, axis=-1, keepdims=True) + LN_EPS)
        xh = c * r
        dxh = do * g_ref[...]
        m1 = jnp.mean(dxh, axis=-1, keepdims=True)
        m2 = jnp.mean(dxh * xh, axis=-1, keepdims=True)
        dh = r * (dxh - m1 - xh * m2)
        dh_ref[...] = dh
        dhm_ref[...] = dh.astype(dhm_ref.dtype)
        dg_ref[...] += jnp.sum(do * xh, axis=0, keepdims=True)
        db_ref[...] += jnp.sum(do, axis=0, keepdims=True)

    row = pl.BlockSpec((tl, d), lambda i: (i, 0))
    vec = pl.BlockSpec((1, d), lambda i: (0, 0))
    return pl.pallas_call(
        body, name=name, grid=(l // tl,), in_specs=[row, row, vec], out_specs=[row, row, vec, vec],
        out_shape=[jax.ShapeDtypeStruct((l, d), F32), jax.ShapeDtypeStruct((l, d), MXU_DTYPE),
                   jax.ShapeDtypeStruct((1, d), F32), jax.ShapeDtypeStruct((1, d), F32)],
        compiler_params=_params(("arbitrary",)),
    )(dout, h, g)


def _loss_head(y, target):
    l, d = y.shape
    tl = 256

    def body(y_ref, t_ref, loss_ref, dy_ref):
        @pl.when(pl.program_id(0) == 0)
        def _():
            loss_ref[...] = jnp.zeros_like(loss_ref)

        e = y_ref[...] - t_ref[...]
        dy_ref[...] = e * (1.0 / d)
        s = jnp.sum(jnp.sum(e * e, axis=1, keepdims=True), axis=0, keepdims=True)
        loss_ref[...] += s * (0.5 / d)

    row = pl.BlockSpec((tl, d), lambda i: (i, 0))
    return pl.pallas_call(
        body, name="loss_head", grid=(l // tl,), in_specs=[row, row],
        out_specs=[pl.BlockSpec((1, 1), lambda i: (0, 0)), row],
        out_shape=[jax.ShapeDtypeStruct((1, 1), F32), jax.ShapeDtypeStruct((l, d), F32)],
        compiler_params=_params(("arbitrary",)),
    )(y, target)


def _s5_discretize(lam_re, lam_im, log_step, b_re, b_im):
    step = jnp.exp(log_step)[:, None]
    e = jnp.exp(lam_re * step)
    lbr, lbi = e * jnp.cos(lam_im * step), e * jnp.sin(lam_im * step)
    den = lam_re * lam_re + lam_im * lam_im
    qr = ((lbr - 1.0) * lam_re + lbi * lam_im) / den
    qi = (lbi * lam_re - (lbr - 1.0) * lam_im) / den
    bbr = qr[:, :, None] * b_re - qi[:, :, None] * b_im
    bbi = qr[:, :, None] * b_im + qi[:, :, None] * b_re
    return lbr, lbi, bbr, bbi


S5_TILES, S5_SLABS = 4, 8
S5_TILE_W, S5_SLAB_W = GROUP_WIDTH // S5_TILES, S5_NS // S5_TILES
S5_GPT = S5_GROUPS // S5_TILES


def _s5_compact(bbr, bbi, c_re, c_im):
    eye = jnp.eye(S5_GPT, dtype=F32)
    def bd(t):
        return jnp.einsum("tgph,gk->tghkp", t.reshape(S5_TILES, S5_GPT, S5_STATE, S5_CH), eye).reshape(S5_TILES, S5_TILE_W, S5_SLAB_W)
    def cd(t):
        return jnp.einsum("tghp,gk->tgpkh", t.reshape(S5_TILES, S5_GPT, S5_CH, S5_STATE), eye).reshape(S5_TILES, S5_SLAB_W, S5_TILE_W)
    return jnp.concatenate([bd(bbr), bd(bbi)], axis=0), jnp.concatenate([cd(c_re), -cd(c_im)], axis=0)


def _s5_uncompact_b(db):
    eye = jnp.eye(S5_GPT, dtype=F32)[None, :, None, :, None]
    def ex(t):
        d = jnp.sum(t.reshape(S5_TILES, S5_GPT, S5_CH, S5_GPT, S5_STATE) * eye, axis=3)
        return jnp.transpose(d, (0, 1, 3, 2)).reshape(S5_GROUPS, S5_STATE, S5_CH)
    return ex(db[:S5_TILES]), ex(db[S5_TILES:])


def _s5_uncompact_c(dc):
    eye = jnp.eye(S5_GPT, dtype=F32)[None, :, None, :, None]
    def ex(t):
        d = jnp.sum(t.reshape(S5_TILES, S5_GPT, S5_STATE, S5_GPT, S5_CH) * eye, axis=3)
        return jnp.transpose(d, (0, 1, 3, 2)).reshape(S5_GROUPS, S5_CH, S5_STATE)
    return ex(dc[:S5_TILES]), -ex(dc[S5_TILES:])


def _s5_matmul(a, b, *, kind, out_dtype, name, extras=(), epi=None):
    nt, ns, tw, sw = S5_TILES, S5_SLABS, S5_TILE_W, S5_SLAB_W
    l = a.shape[0]
    n_ex = len(extras)

    if kind in ("expand", "expand_t"):
        tm = min(1024, l)
        mode = "nt" if kind == "expand_t" else "nn"

        def body(a_ref, b_ref, o_ref):
            o_ref[...] = _dot(_mx(a_ref[...]), _mx(b_ref[...]), mode).astype(out_dtype)

        return pl.pallas_call(
            body, name=name, grid=(l // tm, ns),
            in_specs=[pl.BlockSpec((tm, tw), lambda i, j: (i, j % nt)), pl.BlockSpec((None,) + b.shape[1:], lambda i, j: (j, 0, 0))],
            out_specs=pl.BlockSpec((tm, sw), lambda i, j: (i, j)), out_shape=jax.ShapeDtypeStruct((l, ns * sw), out_dtype),
            compiler_params=_params(("parallel", "parallel")),
        )(a, b)

    if kind in ("reduce", "reduce_t"):
        tm = min(1024, l)
        mode = "nt" if kind == "reduce_t" else "nn"

        def body(a_ref, b_ref, *rest):
            ex_refs, o_ref, acc = rest[:n_ex], rest[n_ex], rest[n_ex + 1]
            part = _dot(_mx(a_ref[...]), _mx(b_ref[...]), mode)

            @pl.when(pl.program_id(2) == 0)
            def _():
                acc[...] = part

            @pl.when(pl.program_id(2) == 1)
            def _():
                r = acc[...] + part
                if epi is not None:
                    r = epi(r, *[e[...] for e in ex_refs])
                o_ref[...] = r.astype(out_dtype)

        ex_specs = [pl.BlockSpec((tm, tw), lambda i, t, kk: (i, t)) if k == "tile" else pl.BlockSpec((1, tw), lambda i, t, kk: (0, t))
                    for _, k in extras]
        return pl.pallas_call(
            body, name=name, grid=(l // tm, nt, 2),
            in_specs=[pl.BlockSpec((tm, sw), lambda i, t, kk: (i, kk * nt + t)),
                      pl.BlockSpec((None,) + b.shape[1:], lambda i, t, kk: (kk * nt + t, 0, 0))] + ex_specs,
            out_specs=pl.BlockSpec((tm, tw), lambda i, t, kk: (i, t)), out_shape=jax.ShapeDtypeStruct((l, nt * tw), out_dtype),
            scratch_shapes=[pltpu.VMEM((tm, tw), F32)], compiler_params=_params(("parallel", "parallel", "arbitrary")),
        )(a, b, *[e for e, _ in extras])

    tk = min(1024, l)
    nk = l // tk
    slab = pl.BlockSpec((tk, sw), lambda j, kk: (kk, j))
    tile = pl.BlockSpec((tk, tw), lambda j, kk: (kk, j % nt))
    in_specs, o_blk = ([slab, tile], (sw, tw)) if kind == "grad_c" else ([tile, slab], (tw, sw))

    def body(a_ref, b_ref, o_ref):
        part = _dot(_mx(a_ref[...]), _mx(b_ref[...]), "tn")

        @pl.when(pl.program_id(1) == 0)
        def _():
            o_ref[...] = part

        @pl.when(pl.program_id(1) > 0)
        def _():
            o_ref[...] += part

    return pl.pallas_call(
        body, name=name, grid=(ns, nk), in_specs=in_specs, out_specs=pl.BlockSpec((None,) + o_blk, lambda j, kk: (j, 0, 0)),
        out_shape=jax.ShapeDtypeStruct((ns,) + o_blk, F32), compiler_params=_params(("parallel", "arbitrary")),
    )(a, b)


def _s5_scan(bu, lam, *, reverse, h=None, name):
    l, w = bu.shape
    ns = w // 2
    tl = 256
    nb = l // tl
    with_h = h is not None

    def body(*refs):
        if with_h:
            bu_ref, lam_ref, h_ref, o_ref, dl_ref, carry = refs
        else:
            bu_ref, lam_ref, o_ref, carry = refs

        @pl.when(pl.program_id(0) == 0)
        def _():
            carry[...] = jnp.zeros_like(carry)
            if with_h:
                dl_ref[...] = jnp.zeros_like(dl_ref)

        lr, li = lam_ref[:, :ns], lam_ref[:, ns:]

        def step(t, c):
            row = (tl - 1 - t) if reverse else t
            cr, ci = c[0], c[1]
            out = c[2:]
            if with_h:
                hr, hi = h_ref[pl.ds(row, 1), :ns], h_ref[pl.ds(row, 1), ns:]
                out = (out[0] + cr * hr + ci * hi, out[1] + ci * hr - cr * hi)
            nr = lr * cr - li * ci + bu_ref[pl.ds(row, 1), :ns]
            ni = lr * ci + li * cr + bu_ref[pl.ds(row, 1), ns:]
            o_ref[pl.ds(row, 1), :ns] = nr
            o_ref[pl.ds(row, 1), ns:] = ni
            return (nr, ni) + tuple(out)

        init = (carry[:, :ns], carry[:, ns:])
        if with_h:
            init = init + (dl_ref[:, :ns], dl_ref[:, ns:])
        fin = lax.fori_loop(0, tl, step, init)
        carry[:, :ns] = fin[0]
        carry[:, ns:] = fin[1]
        if with_h:
            dl_ref[:, :ns] = fin[2]
            dl_ref[:, ns:] = fin[3]

    idx = (lambda i: (nb - 1 - i, 0)) if reverse else (lambda i: (i, 0))
    row = pl.BlockSpec((tl, w), idx)
    vec = pl.BlockSpec((1, w), lambda i: (0, 0))
    in_specs = [row, vec] + ([row] if with_h else [])
    out_specs = [row] + ([vec] if with_h else [])
    out_shape = [jax.ShapeDtypeStruct((l, w), F32)] + ([jax.ShapeDtypeStruct((1, w), F32)] if with_h else [])
    res = pl.pallas_call(
        body, name=name, grid=(nb,), in_specs=in_specs, out_specs=out_specs, out_shape=out_shape,
        scratch_shapes=[pltpu.VMEM((1, w), F32)], compiler_params=_params(("arbitrary",)),
    )(*([bu, lam] + ([h] if with_h else [])))
    return res if with_h else res[0]


def _s5_glu_fwd(y, glu_w, glu_b, name):
    l, d = y.shape
    tl = min(512, l)

    def body(y_ref, w_ref, b_ref, o_ref):
        yg = _gelu(y_ref[...])
        z = _mdot(yg, w_ref[...]) + b_ref[...]
        o_ref[...] = (yg * _sigmoid(z)).astype(o_ref.dtype)

    return pl.pallas_call(
        body, name=name, grid=(l // tl,),
        in_specs=[pl.BlockSpec((tl, d), lambda i: (i, 0)), pl.BlockSpec((d, d), lambda i: (0, 0)), pl.BlockSpec((1, d), lambda i: (0, 0))],
        out_specs=pl.BlockSpec((tl, d), lambda i: (i, 0)), out_shape=jax.ShapeDtypeStruct((l, d), BF16),
        compiler_params=_params(("parallel",)),
    )(y, glu_w, glu_b)


def _s5_glu_bwd(dmixed, y, proj, glu_w, glu_b, name):
    l, d = y.shape
    tl = min(512, l)

    def body(do_ref, y_ref, u_ref, w_ref, b_ref, dy_ref, dz_ref, yg_ref, db_ref, dd_ref):
        @pl.when(pl.program_id(0) == 0)
        def _():
            db_ref[...] = jnp.zeros_like(db_ref)
            dd_ref[...] = jnp.zeros_like(dd_ref)

        yv, do = y_ref[...], do_ref[...]
        yg = _gelu(yv)
        gate = _sigmoid(_mdot(yg, w_ref[...]) + b_ref[...])
        dz = do * yg * gate * (1.0 - gate)
        dyg = do * gate + _mdot(dz, w_ref[...], "nt")
        dy = dyg * _gelu_grad(yv)
        dy_ref[...] = dy
        dz_ref[...] = dz.astype(dz_ref.dtype)
        yg_ref[...] = yg.astype(yg_ref.dtype)
        db_ref[...] += jnp.sum(dz, axis=0, keepdims=True)
        dd_ref[...] += jnp.sum(dy * u_ref[...], axis=0, keepdims=True)

    row = pl.BlockSpec((tl, d), lambda i: (i, 0))
    vec = pl.BlockSpec((1, d), lambda i: (0, 0))
    return pl.pallas_call(
        body, name=name, grid=(l // tl,),
        in_specs=[row, row, row, pl.BlockSpec((d, d), lambda i: (0, 0)), vec],
        out_specs=[row, row, row, vec, vec],
        out_shape=[jax.ShapeDtypeStruct((l, d), F32), jax.ShapeDtypeStruct((l, d), BF16), jax.ShapeDtypeStruct((l, d), BF16),
                   jax.ShapeDtypeStruct((1, d), F32), jax.ShapeDtypeStruct((1, d), F32)],
        compiler_params=_params(("arbitrary",)),
    )(dmixed, y, proj, glu_w, glu_b)


def _sgu_pair(w_ref, x, j, dims):
    lo = lax.broadcasted_iota(jnp.int32, x.shape, 1) < (GROUP_WIDTH // SGU_HEADS)
    xb = _mx(x)
    r0 = _dot(w_ref[2 * j], xb, dims)
    r1 = _dot(w_ref[2 * j + 1], xb, dims)
    return jnp.where(lo, r0, r1)


def _sgu_norm(v, g, b):
    mu = jnp.mean(v, axis=-1, keepdims=True)
    c = v - mu
    r = lax.rsqrt(jnp.mean(c * c, axis=-1, keepdims=True) + LN_EPS)
    return c * r, r


def _sgu_fwd(proj, norm_g, norm_b, wm, bfull, name):
    l = proj.shape[0]
    tl = 256
    gw = GROUP_WIDTH

    def body(zu_ref, zv_ref, g_ref, b_ref, w_ref, bf_ref, o_ref):
        for c in range(tl // SGU_CHUNK):
            rows = slice(c * SGU_CHUNK, (c + 1) * SGU_CHUNK)
            u = _gelu(zu_ref[rows, :])
            vh, _ = _sgu_norm(_gelu(zv_ref[rows, :]), None, None)
            vn = vh * g_ref[...] + b_ref[...]
            for j in range(gw // 128):
                cols = slice(j * 128, (j + 1) * 128)
                mixed = _sgu_pair(w_ref, vn[:, cols], j, "nn") + bf_ref[:, cols]
                o_ref[rows, cols] = (u[:, cols] * mixed).astype(o_ref.dtype)

    vec = pl.BlockSpec((1, gw), lambda i: (0, 0))
    return pl.pallas_call(
        body, name=name, grid=(l // tl,),
        in_specs=[pl.BlockSpec((tl, gw), lambda i: (i, 1)), pl.BlockSpec((tl, gw), lambda i: (i, 2)), vec, vec,
                  pl.BlockSpec((SGU_HEADS, SGU_CHUNK, SGU_CHUNK), lambda i: (0, 0, 0)), pl.BlockSpec((SGU_CHUNK, gw), lambda i: (0, 0))],
        out_specs=pl.BlockSpec((tl, gw), lambda i: (i, 0)), out_shape=jax.ShapeDtypeStruct((l, gw), BF16),
        compiler_params=_params(("parallel",)),
    )(proj, proj, norm_g, norm_b, wm, bfull)


def _sgu_bwd(dmixed, proj, norm_g, norm_b, wm, bfull, name):
    l = proj.shape[0]
    tl = 256
    gw = GROUP_WIDTH
    hd = gw // SGU_HEADS

    def body(do_ref, zu_ref, zv_ref, g_ref, b_ref, w_ref, bf_ref, dzu_ref, dzv_ref, dw_ref, dbf_ref, dg_ref, dnb_ref):
        @pl.when(pl.program_id(0) == 0)
        def _():
            dw_ref[...] = jnp.zeros_like(dw_ref)
            dbf_ref[...] = jnp.zeros_like(dbf_ref)
            dg_ref[...] = jnp.zeros_like(dg_ref)
            dnb_ref[...] = jnp.zeros_like(dnb_ref)

        for c in range(tl // SGU_CHUNK):
            rows = slice(c * SGU_CHUNK, (c + 1) * SGU_CHUNK)
            zu, zv, do = zu_ref[rows, :], zv_ref[rows, :], do_ref[rows, :]
            u = _gelu(zu)
            vh, r = _sgu_norm(_gelu(zv), None, None)
            vn = vh * g_ref[...] + b_ref[...]
            dvn_parts, mixed_parts = [], []
            for j in range(gw // 128):
                cols = slice(j * 128, (j + 1) * 128)
                vb = vn[:, cols]
                mixed_parts.append(_sgu_pair(w_ref, vb, j, "nn") + bf_ref[:, cols])
                dm = do[:, cols] * u[:, cols]
                dvn_parts.append(_sgu_pair(w_ref, dm, j, "tn"))
                lo = lax.broadcasted_iota(jnp.int32, dm.shape, 1) < hd
                dw_ref[2 * j] += _mdot(jnp.where(lo, dm, 0.0), vb, "nt")
                dw_ref[2 * j + 1] += _mdot(jnp.where(lo, 0.0, dm), vb, "nt")
                dbf_ref[:, cols] += dm
            mixed = jnp.concatenate(mixed_parts, axis=1)
            dvn = jnp.concatenate(dvn_parts, axis=1)
            dzu_ref[rows, :] = (do * mixed * _gelu_grad(zu)).astype(dzu_ref.dtype)
            dg_ref[...] += jnp.sum(dvn * vh, axis=0, keepdims=True)
            dnb_ref[...] += jnp.sum(dvn, axis=0, keepdims=True)
            dvh = dvn * g_ref[...]
            m1 = jnp.mean(dvh, axis=-1, keepdims=True)
            m2 = jnp.mean(dvh * vh, axis=-1, keepdims=True)
            dv = r * (dvh - m1 - vh * m2)
            dzv_ref[rows, :] = (dv * _gelu_grad(zv)).astype(dzv_ref.dtype)

    vec = pl.BlockSpec((1, gw), lambda i: (0, 0))
    row = pl.BlockSpec((tl, gw), lambda i: (i, 0))
    wspec = pl.BlockSpec((SGU_HEADS, SGU_CHUNK, SGU_CHUNK), lambda i: (0, 0, 0))
    bspec = pl.BlockSpec((SGU_CHUNK, gw), lambda i: (0, 0))
    return pl.pallas_call(
        body, name=name, grid=(l // tl,),
        in_specs=[pl.BlockSpec((tl, gw), lambda i: (i, 1)), pl.BlockSpec((tl, gw), lambda i: (i, 1)), pl.BlockSpec((tl, gw), lambda i: (i, 2)),
                  vec, vec, wspec, bspec],
        out_specs=[row, row, wspec, bspec, vec, vec],
        out_shape=[jax.ShapeDtypeStruct((l, gw), BF16), jax.ShapeDtypeStruct((l, gw), BF16),
                   jax.ShapeDtypeStruct((SGU_HEADS, SGU_CHUNK, SGU_CHUNK), F32), jax.ShapeDtypeStruct((SGU_CHUNK, gw), F32),
                   jax.ShapeDtypeStruct((1, gw), F32), jax.ShapeDtypeStruct((1, gw), F32)],
        compiler_params=_params(("arbitrary",)),
    )(dmixed, proj, proj, norm_g, norm_b, wm, bfull)


HALO = 16


def _window_sums(ext, n_rows, forward):
    def sh(x, k):
        return pltpu.roll(x, (n_rows - k) if forward else k, axis=0)
    s2 = ext + sh(ext, 1)
    s4 = s2 + sh(s2, 2)
    s8 = s4 + sh(s4, 4)
    s16 = s8 + sh(s8, 8)
    return (s2, s4, s8, s16)


def _pool_fwd(proj, pool_w, scale, name):
    l = proj.shape[0]
    tl = 256
    gw = GROUP_WIDTH
    pg = gw // len(POOL_WINDOWS)

    def body(x_ref, halo_ref, w_ref, s_ref, o_ref, p_ref):
        i = pl.program_id(0)
        x = x_ref[...]
        halo = jnp.where(i > 0, halo_ref[...], 0.0)
        ext = jnp.concatenate([halo, x], axis=0)
        sums = _window_sums(ext, tl + HALO, False)
        t = i * tl + lax.broadcasted_iota(jnp.int32, (tl, pg), 0)
        for gi, win in enumerate(POOL_WINDOWS):
            cols = slice(gi * pg, (gi + 1) * pg)
            cnt = jnp.minimum(t + 1, win).astype(F32)
            pooled = sums[gi][HALO:, cols] / cnt - x[:, cols]
            p_ref[:, cols] = pooled
            o_ref[:, cols] = (_mdot(pooled, w_ref[gi]) * s_ref[:, cols]).astype(o_ref.dtype)

    row = pl.BlockSpec((tl, gw), lambda i: (i, 0))
    return pl.pallas_call(
        body, name=name, grid=(l // tl,),
        in_specs=[pl.BlockSpec((tl, gw), lambda i: (i, 3)),
                  pl.BlockSpec((HALO, gw), lambda i: (jnp.maximum(i * (tl // HALO) - 1, 0), 3)),
                  pl.BlockSpec((len(POOL_WINDOWS), pg, pg), lambda i: (0, 0, 0)), pl.BlockSpec((1, gw), lambda i: (0, 0))],
        out_specs=[row, row], out_shape=[jax.ShapeDtypeStruct((l, gw), BF16), jax.ShapeDtypeStruct((l, gw), F32)],
        compiler_params=_params(("parallel",)),
    )(proj, proj, pool_w, scale)


def _pool_bwd_map(dmixed, pooled, pool_w, scale, name):
    l, gw = pooled.shape
    tl = 256
    ng = len(POOL_WINDOWS)
    pg = gw // ng

    def body(do_ref, p_ref, w_ref, s_ref, dp_ref, dw_ref, ds_ref):
        @pl.when(pl.program_id(0) == 0)
        def _():
            dw_ref[...] = jnp.zeros_like(dw_ref)
            ds_ref[...] = jnp.zeros_like(ds_ref)

        for gi in range(ng):
            cols = slice(gi * pg, (gi + 1) * pg)
            do, pooled_g = do_ref[:, cols], p_ref[:, cols]
            mixed = _mdot(pooled_g, w_ref[gi])
            ds_ref[:, cols] += jnp.sum(do * mixed, axis=0, keepdims=True)
            dm = do * s_ref[:, cols]
            dw_ref[gi] += _mdot(pooled_g, dm, "tn")
            dp_ref[:, cols] = _mdot(dm, w_ref[gi], "nt")

    row = pl.BlockSpec((tl, gw), lambda i: (i, 0))
    wspec = pl.BlockSpec((ng, pg, pg), lambda i: (0, 0, 0))
    vec = pl.BlockSpec((1, gw), lambda i: (0, 0))
    return pl.pallas_call(
        body, name=name, grid=(l // tl,),
        in_specs=[pl.BlockSpec((tl, gw), lambda i: (i, 2)), row, wspec, vec], out_specs=[row, wspec, vec],
        out_shape=[jax.ShapeDtypeStruct((l, gw), F32), jax.ShapeDtypeStruct((ng, pg, pg), F32), jax.ShapeDtypeStruct((1, gw), F32)],
        compiler_params=_params(("arbitrary",)),
    )(dmixed, pooled, pool_w, scale)


def _pool_bwd_window(dpooled, name):
    l, gw = dpooled.shape
    tl = 256
    nb = l // tl
    pg = gw // len(POOL_WINDOWS)

    def body(d_ref, halo_ref, o_ref):
        i = pl.program_id(0)
        d = d_ref[...]
        halo = jnp.where(i < nb - 1, halo_ref[...], 0.0)
        ext = jnp.concatenate([d, halo], axis=0)
        t = i * tl + lax.broadcasted_iota(jnp.int32, (tl + HALO, pg), 0)
        for gi, win in enumerate(POOL_WINDOWS):
            cols = slice(gi * pg, (gi + 1) * pg)
            cnt = jnp.minimum(t + 1, win).astype(F32)
            sums = _window_sums(ext[:, cols] / cnt, tl + HALO, True)
            o_ref[:, cols] = (sums[gi][:tl, :] - d[:, cols]).astype(o_ref.dtype)

    row = pl.BlockSpec((tl, gw), lambda i: (i, 0))
    return pl.pallas_call(
        body, name=name, grid=(nb,),
        in_specs=[row, pl.BlockSpec((HALO, gw), lambda i: (jnp.minimum((i + 1) * (tl // HALO), l // HALO - 1), 0))],
        out_specs=row, out_shape=jax.ShapeDtypeStruct((l, gw), BF16), compiler_params=_params(("parallel",)),
    )(dpooled, dpooled)


CONV_HALO = 8
QKV_BLK = 4


def _head_sums(x):
    parts = []
    for hd in range(DN_HEADS):
        s = jnp.sum(x[:, hd * DN_HEAD_DIM:(hd + 1) * DN_HEAD_DIM], axis=-1, keepdims=True)
        parts.append(jnp.broadcast_to(s, (x.shape[0], DN_HEAD_DIM)))
    return jnp.concatenate(parts, axis=1)


def _gdn_pre_fwd(proj, proj_ab, conv_w, a_log, dt_bias, name):
    l = proj.shape[0]
    tl = 256
    gw = GROUP_WIDTH

    def body(xq, xk, xv, hq, hk, hv, w_ref, ab_ref, al_ref, dt_ref, qn_ref, kn_ref, v_ref, cq_ref, ck_ref, cv_ref, gb_ref):
        i = pl.program_id(0)
        for p, (x_ref, h_ref, c_ref) in enumerate(((xq, hq, cq_ref), (xk, hk, ck_ref), (xv, hv, cv_ref))):
            ext = jnp.concatenate([jnp.where(i > 0, h_ref[...], 0.0), x_ref[...]], axis=0)
            conv = jnp.zeros((tl, gw), F32)
            for j in range(DN_CONV):
                k = DN_CONV - 1 - j
                shifted = ext if k == 0 else pltpu.roll(ext, k, axis=0)
                conv = conv + shifted[CONV_HALO:, :] * w_ref[j:j + 1, p * gw:(p + 1) * gw]
            c_ref[...] = conv
            s = _silu(conv)
            if p == 2:
                v_ref[...] = s
            else:
                r = lax.rsqrt(_head_sums(s * s) + L2_EPS)
                (qn_ref if p == 0 else kn_ref)[...] = s * r * (DN_HEAD_DIM ** -0.5 if p == 0 else 1.0)
        ab = ab_ref[...]
        lane = lax.broadcasted_iota(jnp.int32, ab.shape, 1)
        g = -jnp.exp(al_ref[...]) * _softplus(ab + dt_ref[...])
        gb_ref[...] = jnp.where(lane < DN_HEADS, g, _sigmoid(ab))

    def xs(b):
        return pl.BlockSpec((tl, gw), lambda i: (i, b))

    def hs(b):
        return pl.BlockSpec((CONV_HALO, gw), lambda i: (jnp.maximum(i * (tl // CONV_HALO) - 1, 0), b))

    row = pl.BlockSpec((tl, gw), lambda i: (i, 0))
    abrow = pl.BlockSpec((tl, AB_PAD), lambda i: (i, 0))
    abvec = pl.BlockSpec((1, AB_PAD), lambda i: (0, 0))
    return pl.pallas_call(
        body, name=name, grid=(l // tl,),
        in_specs=[xs(QKV_BLK), xs(QKV_BLK + 1), xs(QKV_BLK + 2), hs(QKV_BLK), hs(QKV_BLK + 1), hs(QKV_BLK + 2),
                  pl.BlockSpec((DN_CONV, 3 * gw), lambda i: (0, 0)), abrow, abvec, abvec],
        out_specs=[row] * 6 + [abrow],
        out_shape=[jax.ShapeDtypeStruct((l, gw), F32)] * 6 + [jax.ShapeDtypeStruct((l, AB_PAD), F32)],
        compiler_params=_params(("parallel",)),
    )(proj, proj, proj, proj, proj, proj, conv_w, proj_ab, a_log, dt_bias)


def _gdn_pre_bwd(dq, dk, dv, cq, ck, cv, dgb, gb, proj_ab, a_log, dt_bias, name):
    l, gw = cq.shape
    tl = 256

    def body(dq_ref, dk_ref, dv_ref, cq_ref, ck_ref, cv_ref, dgb_ref, gb_ref, ab_ref, al_ref, dt_ref,
             dcq_ref, dck_ref, dcv_ref, dab_ref, dal_ref, ddt_ref):
        @pl.when(pl.program_id(0) == 0)
        def _():
            dal_ref[...] = jnp.zeros_like(dal_ref)
            ddt_ref[...] = jnp.zeros_like(ddt_ref)

        for p, (d_ref, c_ref, o_ref) in enumerate(((dq_ref, cq_ref, dcq_ref), (dk_ref, ck_ref, dck_ref), (dv_ref, cv_ref, dcv_ref))):
            c, d = c_ref[...], d_ref[...]
            if p == 2:
                ds = d
            else:
                s = _silu(c)
                r = lax.rsqrt(_head_sums(s * s) + L2_EPS)
                ds = (DN_HEAD_DIM ** -0.5 if p == 0 else 1.0) * r * (d - s * r * r * _head_sums(d * s))
            o_ref[...] = ds * _silu_grad(c)
        ab, dgb_v, gb_v = ab_ref[...], dgb_ref[...], gb_ref[...]
        lane = lax.broadcasted_iota(jnp.int32, ab.shape, 1)
        is_g = lane < DN_HEADS
        dpre = dgb_v * (-jnp.exp(al_ref[...])) * _sigmoid(ab + dt_ref[...])
        dab_ref[...] = jnp.where(is_g, dpre, dgb_v * gb_v * (1.0 - gb_v)).astype(dab_ref.dtype)
        dal_ref[...] += jnp.sum(jnp.where(is_g, dgb_v * gb_v, 0.0), axis=0, keepdims=True)
        ddt_ref[...] += jnp.sum(jnp.where(is_g, dpre, 0.0), axis=0, keepdims=True)

    row = pl.BlockSpec((tl, gw), lambda i: (i, 0))
    abrow = pl.BlockSpec((tl, AB_PAD), lambda i: (i, 0))
    abvec = pl.BlockSpec((1, AB_PAD), lambda i: (0, 0))
    return pl.pallas_call(
        body, name=name, grid=(l // tl,),
        in_specs=[row] * 6 + [abrow, abrow, abrow, abvec, abvec],
        out_specs=[row, row, row, abrow, abvec, abvec],
        out_shape=[jax.ShapeDtypeStruct((l, gw), F32)] * 3 + [jax.ShapeDtypeStruct((l, AB_PAD), BF16),
                   jax.ShapeDtypeStruct((1, AB_PAD), F32), jax.ShapeDtypeStruct((1, AB_PAD), F32)],
        compiler_params=_params(("arbitrary",)),
    )(dq, dk, dv, cq, ck, cv, dgb, gb, proj_ab, a_log, dt_bias)


def _conv_bwd(dc, proj, col_blk, w_part, name):
    l, gw = dc.shape
    tl = 256
    nb = l // tl

    def body(dc_ref, halo_ref, x_ref, w_ref, dx_ref, dw_ref):
        i = pl.program_id(0)

        @pl.when(i == 0)
        def _():
            dw_ref[...] = jnp.zeros_like(dw_ref)

        ext = jnp.concatenate([dc_ref[...], jnp.where(i < nb - 1, halo_ref[...], 0.0)], axis=0)
        x = x_ref[...]
        dx = jnp.zeros((tl, gw), F32)
        rid = lax.broadcasted_iota(jnp.int32, (8, gw), 0)
        dw = jnp.zeros((8, gw), F32)
        for j in range(DN_CONV):
            k = DN_CONV - 1 - j
            shifted = (ext if k == 0 else pltpu.roll(ext, tl + CONV_HALO - k, axis=0))[:tl, :]
            dx = dx + shifted * w_ref[j:j + 1, :]
            dw = dw + jnp.where(rid == j, jnp.sum(x * shifted, axis=0, keepdims=True), 0.0)
        dx_ref[...] = dx.astype(dx_ref.dtype)
        dw_ref[...] += dw

    row = pl.BlockSpec((tl, gw), lambda i: (i, 0))
    return pl.pallas_call(
        body, name=name, grid=(nb,),
        in_specs=[row, pl.BlockSpec((CONV_HALO, gw), lambda i: (jnp.minimum((i + 1) * (tl // CONV_HALO), l // CONV_HALO - 1), 0)),
                  pl.BlockSpec((tl, gw), lambda i: (i, col_blk)), pl.BlockSpec((DN_CONV, gw), lambda i: (0, 0))],
        out_specs=[row, pl.BlockSpec((8, gw), lambda i: (0, 0))],
        out_shape=[jax.ShapeDtypeStruct((l, gw), BF16), jax.ShapeDtypeStruct((8, gw), F32)],
        compiler_params=_params(("arbitrary",)),
    )(dc, dc, proj, w_part)


TERMS_CHUNKS = 4


def _bdot(a, b, dims="nn", precision=None):
    cd = {"nn": ((2,), (1,)), "nt": ((2,), (2,)), "tn": ((1,), (1,))}[dims]
    return lax.dot_general(a, b, (cd, ((0,), (0,))), preferred_element_type=F32, precision=precision)


def _bmdot(a, b, dims="nn"):
    return _bdot(_mx(a), _mx(b), dims)


def _wy_terms(q, k, v, gcol, beta, t=None):
    c = DN_CHUNK
    ii = lax.broadcasted_iota(jnp.int32, (1, c, c), 1)
    jj = lax.broadcasted_iota(jnp.int32, (1, c, c), 2)
    tril, strict = ii >= jj, ii > jj
    grow = jnp.sum(jnp.where(ii == jj, gcol, 0.0), axis=1, keepdims=True)
    gc_col = jnp.sum(jnp.where(tril, grow, 0.0), axis=2, keepdims=True)
    gc_row = jnp.sum(jnp.where(ii <= jj, gcol, 0.0), axis=1, keepdims=True)
    dec = jnp.exp(jnp.where(tril, gc_col - gc_row, -1e30))
    kb, vb = k * beta, v * beta
    kk = _bmdot(kb, k, "nt")
    if t is None:
        a = jnp.where(strict, kk * dec, 0.0)
        d = jnp.where((ii >> 3) == (jj >> 3), a, 0.0)
        t = jnp.where(ii == jj, 1.0, 0.0) - d
        p = _bdot(d, d, precision=HI)
        t = t + _bdot(t, p, precision=HI)
        t = t + _bdot(t, _bdot(p, p, precision=HI), precision=HI)
        for sh in (3, 4, 5):
            below = ((ii >> (sh + 1)) == (jj >> (sh + 1))) & ((ii >> sh) > (jj >> sh))
            t = t - _bdot(t, _bdot(jnp.where(below, a, 0.0), t, precision=HI), precision=HI)
    eg = jnp.exp(gc_col)
    gc_last = gc_col[:, c - 1:c, :]
    kbg = kb * eg
    qk0 = _bmdot(q, k, "nt")
    e2 = jnp.exp(gc_last - gc_col)
    return dict(ii=ii, jj=jj, tril=tril, strict=strict, dec=dec, kb=kb, vb=vb, kk=kk, t=t, eg=eg, kbg=kbg,
                qk0=qk0, qk=jnp.where(tril, qk0 * dec, 0.0), qg=q * eg, e2=e2, kt=k * e2, gl=jnp.exp(gc_last))


def _to_heads(x, g):
    return jnp.concatenate([x[:, h * DN_HEAD_DIM:(h + 1) * DN_HEAD_DIM].reshape(g, DN_CHUNK, DN_HEAD_DIM)
                            for h in range(DN_HEADS)], axis=0)


def _from_heads(t, ref, g):
    for h in range(DN_HEADS):
        ref[:, h * DN_HEAD_DIM:(h + 1) * DN_HEAD_DIM] = t[h * g:(h + 1) * g].reshape(g * DN_CHUNK, DN_HEAD_DIM).astype(ref.dtype)


def _head_columns(gbv, first_lane, g):
    lane = lax.broadcasted_iota(jnp.int32, gbv.shape, 1)
    return jnp.concatenate([jnp.sum(jnp.where(lane == first_lane + h, gbv, 0.0), axis=1, keepdims=True).reshape(g, DN_CHUNK, 1)
                            for h in range(DN_HEADS)], axis=0)


def _gdn_terms_fwd(qn, kn, v, gb, name):
    l = qn.shape[0]
    n_chunks = l // DN_CHUNK
    g = min(TERMS_CHUNKS, n_chunks)
    rows, c, nh = g * DN_CHUNK, DN_CHUNK, DN_HEADS

    def body(q_ref, k_ref, v_ref, gb_ref, u_ref, w_ref, qg_ref, kt_ref, qk_ref, t_ref, gl_ref):
        gbv = gb_ref[...]
        x = _wy_terms(_to_heads(q_ref[...], g), _to_heads(k_ref[...], g), _to_heads(v_ref[...], g),
                      _head_columns(gbv, 0, g), _head_columns(gbv, nh, g))
        _from_heads(_bmdot(x["t"], x["vb"]), u_ref, g)
        _from_heads(_bmdot(x["t"], x["kbg"]), w_ref, g)
        _from_heads(x["qg"], qg_ref, g)
        _from_heads(x["kt"], kt_ref, g)
        for h in range(nh):
            qk_ref[:, h] = x["qk"][h * g:(h + 1) * g]
            t_ref[:, h] = x["t"][h * g:(h + 1) * g]
            gl_ref[:, h] = jnp.broadcast_to(x["gl"][h * g:(h + 1) * g], (g, 1, 128))

    row = pl.BlockSpec((rows, GROUP_WIDTH), lambda i: (i, 0))
    sq = pl.BlockSpec((g, nh, c, c), lambda i: (i, 0, 0, 0))
    glb = pl.BlockSpec((g, nh, 1, 128), lambda i: (i, 0, 0, 0))
    return pl.pallas_call(
        body, name=name, grid=(n_chunks // g,), in_specs=[row, row, row, pl.BlockSpec((rows, AB_PAD), lambda i: (i, 0))],
        out_specs=[row] * 4 + [sq, sq, glb],
        out_shape=[jax.ShapeDtypeStruct((l, GROUP_WIDTH), F32)] * 4 + [jax.ShapeDtypeStruct((n_chunks, nh, c, c), F32)] * 2
        + [jax.ShapeDtypeStruct((n_chunks, nh, 1, 128), F32)],
        compiler_params=_params(("parallel",)),
    )(qn, kn, v, gb)


def _rec_specs(n_chunks, reverse):
    c, hd, nh = DN_CHUNK, DN_HEAD_DIM, DN_HEADS
    ch = (lambda n: n_chunks - 1 - n) if reverse else (lambda n: n)
    return (pl.BlockSpec((c, GROUP_WIDTH), lambda n: (ch(n), 0)), pl.BlockSpec((1, nh, c, c), lambda n: (ch(n), 0, 0, 0)),
            pl.BlockSpec((1, nh, 1, 128), lambda n: (ch(n), 0, 0, 0)), pl.BlockSpec((1, nh, hd, hd), lambda n: (ch(n), 0, 0, 0)))


def _gdn_rec_fwd(u, w, qg, kt, qk, gl, name):
    l = u.shape[0]
    n_chunks = l // DN_CHUNK
    hd, nh = DN_HEAD_DIM, DN_HEADS
    blk, sq, glb, st = _rec_specs(n_chunks, False)
    heads = range(nh)

    def body(u_ref, w_ref, qg_ref, kt_ref, qk_ref, gl_ref, o_ref, vn_ref, s_ref, state):
        @pl.when(pl.program_id(0) == 0)
        def _():
            state[...] = jnp.zeros_like(state)

        def cols(h):
            return slice(h * hd, (h + 1) * hd)
        s = [state[h] for h in heads]
        ws = [_mdot(w_ref[:, cols(h)], s[h]) for h in heads]
        vn = [u_ref[:, cols(h)] - ws[h] for h in heads]
        kv = [_mdot(kt_ref[:, cols(h)], vn[h], "tn") for h in heads]
        for h in heads:
            state[h] = s[h] * gl_ref[0, h] + kv[h]
        o1 = [_mdot(qg_ref[:, cols(h)], s[h]) for h in heads]
        o2 = [_mdot(qk_ref[0, h], vn[h]) for h in heads]
        for h in heads:
            s_ref[0, h] = s[h]
            o_ref[:, cols(h)] = o1[h] + o2[h]
            vn_ref[:, cols(h)] = vn[h]

    return pl.pallas_call(
        body, name=name, grid=(n_chunks,), in_specs=[blk, blk, blk, blk, sq, glb], out_specs=[blk, blk, st],
        out_shape=[jax.ShapeDtypeStruct((l, GROUP_WIDTH), F32)] * 2 + [jax.ShapeDtypeStruct((n_chunks, nh, hd, hd), F32)],
        scratch_shapes=[pltpu.VMEM((nh, hd, hd), F32)], compiler_params=_params(("arbitrary",)),
    )(u, w, qg, kt, qk, gl)


def _gdn_rec_bwd(do, w, qg, kt, vn, qk, gl, states, name):
    l = do.shape[0]
    n_chunks = l // DN_CHUNK
    hd, nh, c = DN_HEAD_DIM, DN_HEADS, DN_CHUNK
    blk, sq, glb, st = _rec_specs(n_chunks, True)
    heads = range(nh)

    def body(do_ref, w_ref, qg_ref, kt_ref, vn_ref, qk_ref, gl_ref, s_ref, dvn_ref, dw_ref, dkt_ref, dqg_ref, dqk_ref, dgl_ref, dstate):
        @pl.when(pl.program_id(0) == 0)
        def _():
            dstate[...] = jnp.zeros_like(dstate)

        def cols(h):
            return slice(h * hd, (h + 1) * hd)
        tril = lax.broadcasted_iota(jnp.int32, (c, c), 0) >= lax.broadcasted_iota(jnp.int32, (c, c), 1)
        ds = [dstate[h] for h in heads]
        dout = [do_ref[:, cols(h)] for h in heads]
        a1 = [_mdot(qk_ref[0, h], dout[h], "tn") for h in heads]
        a2 = [_mdot(kt_ref[:, cols(h)], ds[h]) for h in heads]
        dvn = [a1[h] + a2[h] for h in heads]
        b1 = [_mdot(qg_ref[:, cols(h)], dout[h], "tn") for h in heads]
        b2 = [_mdot(w_ref[:, cols(h)], dvn[h], "tn") for h in heads]
        for h in heads:
            dstate[h] = b1[h] + gl_ref[0, h] * ds[h] - b2[h]
        for h in heads:
            s, vnew = s_ref[0, h], vn_ref[:, cols(h)]
            dvn_ref[:, cols(h)] = dvn[h]
            dw_ref[:, cols(h)] = -_mdot(dvn[h], s, "nt")
            dkt_ref[:, cols(h)] = _mdot(vnew, ds[h], "nt")
            dqg_ref[:, cols(h)] = _mdot(dout[h], s, "nt")
            dqk_ref[0, h] = jnp.where(tril, _mdot(dout[h], vnew, "nt"), 0.0)
            dgl = jnp.sum(jnp.sum(ds[h] * s, axis=1, keepdims=True), axis=0, keepdims=True)
            dgl_ref[0, h] = jnp.broadcast_to(dgl, (1, 128))

    return pl.pallas_call(
        body, name=name, grid=(n_chunks,), in_specs=[blk] * 5 + [sq, glb, st], out_specs=[blk] * 4 + [sq, glb],
        out_shape=[jax.ShapeDtypeStruct((l, GROUP_WIDTH), F32)] * 4 + [jax.ShapeDtypeStruct((n_chunks, nh, c, c), F32),
                                                                       jax.ShapeDtypeStruct((n_chunks, nh, 1, 128), F32)],
        scratch_shapes=[pltpu.VMEM((nh, hd, hd), F32)], compiler_params=_params(("arbitrary",)),
    )(do, w, qg, kt, vn, qk, gl, states)


def _gdn_terms_bwd(qn, kn, v, gb, t_inv, dvn, dw, dkt, dqg, dqk, dgl, name):
    l = qn.shape[0]
    n_chunks = l // DN_CHUNK
    g = min(TERMS_CHUNKS, n_chunks)
    rows, c, nh = g * DN_CHUNK, DN_CHUNK, DN_HEADS

    def body(q_ref, k_ref, v_ref, gb_ref, t_ref, dvn_ref, dw_ref, dkt_ref, dqg_ref, dqk_ref, dgl_ref, dq_ref, dk_ref, dv_ref, dgb_ref):
        gbv = gb_ref[...]
        q, k, vv = _to_heads(q_ref[...], g), _to_heads(k_ref[...], g), _to_heads(v_ref[...], g)
        beta = _head_columns(gbv, nh, g)
        t = jnp.concatenate([t_ref[:, h] for h in range(nh)], axis=0)
        x = _wy_terms(q, k, vv, _head_columns(gbv, 0, g), beta, t=t)
        ii, jj, strict = x["ii"], x["jj"], x["strict"]
        du, dwv, dktv, dqgv = (_to_heads(r[...], g) for r in (dvn_ref, dw_ref, dkt_ref, dqg_ref))
        dqkv = jnp.concatenate([dqk_ref[:, h] for h in range(nh)], axis=0)
        dglv = jnp.concatenate([dgl_ref[:, h] for h in range(nh)], axis=0)[:, :, 0:1]
        dt = _bmdot(du, x["vb"], "nt") + _bmdot(dwv, x["kbg"], "nt")
        dvb = _bmdot(t, du, "tn")
        dkbg = _bmdot(t, dwv, "tn")
        tt_dt = _bdot(t, dt, "tn", precision=HI)
        da = jnp.where(strict, -_bdot(tt_dt, t, "nt", precision=HI), 0.0)
        dkk = da * x["dec"]
        dqk0 = dqkv * x["dec"]
        e = (da * x["kk"] + dqkv * x["qk0"]) * x["dec"]
        dkb = _bmdot(dkk, k) + dkbg * x["eg"]
        dk = _bmdot(dkk, x["kb"], "tn") + _bmdot(dqk0, q, "tn") + dktv * x["e2"] + dkb * beta
        dq = _bmdot(dqk0, k) + dqgv * x["eg"]
        s_kt = jnp.sum(dktv * x["kt"], axis=2, keepdims=True)
        dgc_c = (jnp.sum(e, axis=2, keepdims=True) + jnp.sum(dqgv * x["qg"], axis=2, keepdims=True) - s_kt
                 + jnp.sum(dkbg * x["kbg"], axis=2, keepdims=True))
        dgc_last = jnp.sum(s_kt, axis=1, keepdims=True) + dglv * x["gl"]
        rid = lax.broadcasted_iota(jnp.int32, (1, c, 1), 1)
        dgc_c = dgc_c + jnp.where(rid == c - 1, dgc_last, 0.0)
        dgc_r = jnp.sum(jnp.where(ii == jj, dgc_c, 0.0), axis=1, keepdims=True) - jnp.sum(e, axis=1, keepdims=True)
        dg = jnp.sum(jnp.where(jj >= ii, dgc_r, 0.0), axis=2, keepdims=True)
        dbeta = jnp.sum(dkb * k, axis=2, keepdims=True) + jnp.sum(dvb * vv, axis=2, keepdims=True)
        _from_heads(dq, dq_ref, g)
        _from_heads(dk, dk_ref, g)
        _from_heads(dvb * beta, dv_ref, g)
        lane = lax.broadcasted_iota(jnp.int32, gbv.shape, 1)
        dgb = jnp.zeros(gbv.shape, F32)
        for h in range(nh):
            dgb = dgb + jnp.where(lane == h, dg[h * g:(h + 1) * g].reshape(rows, 1), 0.0)
            dgb = dgb + jnp.where(lane == nh + h, dbeta[h * g:(h + 1) * g].reshape(rows, 1), 0.0)
        dgb_ref[...] = dgb

    row = pl.BlockSpec((rows, GROUP_WIDTH), lambda i: (i, 0))
    abrow = pl.BlockSpec((rows, AB_PAD), lambda i: (i, 0))
    sq = pl.BlockSpec((g, nh, c, c), lambda i: (i, 0, 0, 0))
    glb = pl.BlockSpec((g, nh, 1, 128), lambda i: (i, 0, 0, 0))
    return pl.pallas_call(
        body, name=name, grid=(n_chunks // g,), in_specs=[row, row, row, abrow, sq, row, row, row, row, sq, glb],
        out_specs=[row, row, row, abrow],
        out_shape=[jax.ShapeDtypeStruct((l, GROUP_WIDTH), F32)] * 3 + [jax.ShapeDtypeStruct((l, AB_PAD), F32)],
        compiler_params=_params(("parallel",)),
    )(qn, kn, v, gb, t_inv, dvn, dw, dkt, dqg, dqk, dgl)


def _gdn_post_fwd(o, proj, norm_g4, name):
    l, gw = o.shape
    tl = min(512, l)

    def body(o_ref, gate_ref, g_ref, out_ref):
        ov = o_ref[...]
        r = lax.rsqrt(_head_sums(ov * ov) * (1.0 / DN_HEAD_DIM) + RMS_EPS)
        out_ref[...] = (ov * r * g_ref[...] * _silu(gate_ref[...])).astype(out_ref.dtype)

    row = pl.BlockSpec((tl, gw), lambda i: (i, 0))
    return pl.pallas_call(
        body, name=name, grid=(l // tl,),
        in_specs=[row, pl.BlockSpec((tl, gw), lambda i: (i, 7)), pl.BlockSpec((1, gw), lambda i: (0, 0))],
        out_specs=row, out_shape=jax.ShapeDtypeStruct((l, gw), BF16), compiler_params=_params(("parallel",)),
    )(o, proj, norm_g4)


def _gdn_post_bwd(dmixed, o, proj, norm_g4, name):
    l, gw = o.shape
    tl = min(512, l)

    def body(d_ref, o_ref, gate_ref, g_ref, do_ref, dgate_ref, dng_ref):
        @pl.when(pl.program_id(0) == 0)
        def _():
            dng_ref[...] = jnp.zeros_like(dng_ref)

        ov, gate, d = o_ref[...], gate_ref[...], d_ref[...]
        r = lax.rsqrt(_head_sums(ov * ov) * (1.0 / DN_HEAD_DIM) + RMS_EPS)
        oh = ov * r
        sg = _silu(gate)
        dgate_ref[...] = (d * oh * g_ref[...] * _silu_grad(gate)).astype(dgate_ref.dtype)
        dng_ref[...] += jnp.sum(d * sg * oh, axis=0, keepdims=True)
        doh = d * g_ref[...] * sg
        do_ref[...] = r * (doh - oh * _head_sums(doh * oh) * (1.0 / DN_HEAD_DIM))

    row = pl.BlockSpec((tl, gw), lambda i: (i, 0))
    vec = pl.BlockSpec((1, gw), lambda i: (0, 0))
    return pl.pallas_call(
        body, name=name, grid=(l // tl,),
        in_specs=[pl.BlockSpec((tl, gw), lambda i: (i, 3)), row, pl.BlockSpec((tl, gw), lambda i: (i, 7)), vec],
        out_specs=[row, row, vec],
        out_shape=[jax.ShapeDtypeStruct((l, gw), F32), jax.ShapeDtypeStruct((l, gw), BF16), jax.ShapeDtypeStruct((1, gw), F32)],
        compiler_params=_params(("arbitrary",)),
    )(dmixed, o, proj, norm_g4)


def _run(hosts, name, fn):
    h = hosts.get(name)
    if h is None:
        return fn(None)
    res, outs = fn(h[0]())
    h[1](outs)
    return res


def _layer_fwd(x, xm, w, li, hosts):
    l = x.shape[0]
    nm = f"l{li}_"
    proj = _run(hosts, nm + "proj", lambda ops: _matmul(
        xm, w["w_main"], mode="nn", tm=1024, tn=1024, tk=2048,out_dtype=F32, name=nm + "proj", comm=ops))
    proj_ab = _matmul(xm, w["w_ab"], mode="nn", tm=1024, tn=AB_PAD, tk=2048, out_dtype=F32, name=nm + "proj_ab")
    bu = _s5_matmul(proj, w["s5_b"], kind="expand", out_dtype=F32, name=nm + "s5_bu")
    hs = _s5_scan(bu, w["s5_lam"], reverse=False, name=nm + "s5_scan")
    y = _s5_matmul(hs, w["s5_c"], kind="reduce", out_dtype=F32, name=nm + "s5_y",
                   extras=[(proj, "tile"), (w["s5_d"], "vec")], epi=lambda acc, u, d: acc + d * u)
    m_s5 = _s5_glu_fwd(y, w["s5_glu_w"], w["s5_glu_b"], nm + "s5_glu")
    m_sgu = _sgu_fwd(proj, w["sgu_norm_g"], w["sgu_norm_b"], w["sgu_wm"], w["sgu_bfull"], nm + "sgu")
    m_pool, pooled = _pool_fwd(proj, w["pool_w"], w["pool_scale"], nm + "pool")
    qn, kn, v, cq, ck, cv, gb = _gdn_pre_fwd(proj, proj_ab, w["dn_conv_w"], w["dn_a_log"], w["dn_dt_bias"], nm + "gdn_pre")
    u, wy, qg, kt, qk, t_inv, gl = _gdn_terms_fwd(qn, kn, v, gb, nm + "gdn_terms")
    o, vn, states = _gdn_rec_fwd(u, wy, qg, kt, qk, gl, nm + "gdn_rec")
    m_dn = _gdn_post_fwd(o, proj, w["dn_norm_g4"], nm + "gdn_post")
    mixed = jnp.concatenate([m_s5, m_sgu, m_pool, m_dn], axis=1)
    y1 = _matmul(mixed, w["w_out"], mode="nn", tm=1024, tn=1024, tk=2048, out_dtype=F32, name=nm + "out_proj")
    h1, x1, x1m = _ln_fwd(x, y1, w["ln1_g"], w["ln1_b"], nm + "ln1")
    r = _run(hosts, nm + "up", lambda ops: _matmul(
        x1m, w["w_up"], mode="nn", tm=1024, tn=1024, tk=2048,out_dtype=BF16, name=nm + "up",
        epi=lambda acc: jnp.maximum(acc, 0.0), b_slab=w["w_up"].shape[2], comm=ops))
    y2 = _run(hosts, nm + "down", lambda ops: _matmul(
        r, w["w_down"], mode="nn", tm=1024, tn=1024, tk=2048,out_dtype=F32, name=nm + "down", a_fn=lambda a: a * a, comm=ops))
    h2, x2, x2m = _ln_fwd(x1, y2, w["ln2_g"], w["ln2_b"], nm + "ln2")
    saved = dict(xm=xm, proj=proj, proj_ab=proj_ab, hs=hs, y=y, pooled=pooled, qn=qn, kn=kn, v=v, cq=cq, ck=ck, cv=cv, gb=gb,
                 wy=wy, qg=qg, kt=kt, qk=qk, t_inv=t_inv, gl=gl, vn=vn, o=o, states=states, mixed=mixed, h1=h1, x1m=x1m,
                 r=r, h2=h2)
    return x2, x2m, saved


def _layer_bwd(dx2, s, w, small, li, hosts, g):
    nm = f"l{li}b_"
    l = dx2.shape[0]
    gw = GROUP_WIDTH
    wire = MXU_DTYPE
    dh2, dh2m, g["ln2_g"], g["ln2_b"] = _ln_bwd(dx2, s["h2"], w["ln2_g"], nm + "ln2")
    g["w_down"] = _run(hosts, nm + "dw_down", lambda ops: _matmul(
        s["r"], dh2m, mode="tn", tm=1024, tn=1024, tk=2048,out_dtype=wire, name=nm + "dw_down", a_fn=lambda a: a * a,
        comm=ops)).reshape(N_DEV, D_FF // N_DEV, D_MODEL)
    dpre = _run(hosts, nm + "dpre", lambda ops: _matmul(
        dh2m, w["w_down"], mode="nt", tm=1024, tn=1024, tk=2048,out_dtype=BF16, name=nm + "dpre",
        extras=[(s["r"], (None, None), lambda i, j: (i, j))], epi=lambda acc, r: acc * 2.0 * r.astype(F32), comm=ops))
    g["w_up"] = _matmul(s["x1m"], dpre, mode="tn", tm=1024, tn=1024, tk=2048,out_dtype=wire, name=nm + "dw_up",
                        out_slab=D_FF // N_DEV)
    dx1 = _run(hosts, nm + "dx1", lambda ops: _matmul(
        dpre, w["w_up"], mode="nt", tm=1024, tn=1024, tk=2048,out_dtype=F32, name=nm + "dx1",
        extras=[(dh2, (None, None), lambda i, j: (i, j))], epi=lambda acc, e: acc + ALPHA * e,
        b_slab=w["w_up"].shape[2], comm=ops))
    dh1, dh1m, g["ln1_g"], g["ln1_b"] = _ln_bwd(dx1, s["h1"], w["ln1_g"], nm + "ln1")
    g["w_out"] = _matmul(s["mixed"], dh1m, mode="tn", tm=1024, tn=1024, tk=2048,out_dtype=wire,
                         name=nm + "dw_out").reshape(N_DEV, D_MODEL // N_DEV, D_MODEL)
    dmixed = _run(hosts, nm + "dmixed", lambda ops: _matmul(
        dh1m, w["w_out"], mode="nt", tm=1024, tn=1024, tk=2048,out_dtype=F32, name=nm + "dmixed", comm=ops))
    proj, proj_ab = s["proj"], s["proj_ab"]
    dy, dz, yg, g["s5_glu_b"], g["s5_d"] = _s5_glu_bwd(dmixed, s["y"], proj, w["s5_glu_w"], w["s5_glu_b"], nm + "s5_glu")
    g["s5_glu_w"] = _matmul(yg, dz, mode="tn", tm=gw, tn=gw, tk=1024, out_dtype=wire,
                            name=nm + "dw_glu").reshape(N_DEV, gw // N_DEV, gw)
    dhs = _s5_matmul(dy, w["s5_c"], kind="expand_t", out_dtype=F32, name=nm + "s5_dh")
    adj, g["s5_lam"] = _s5_scan(dhs, w["s5_lam_conj"], reverse=True, h=s["hs"], name=nm + "s5_scan")
    g["s5_c"] = _s5_matmul(s["hs"], dy, kind="grad_c", out_dtype=F32, name=nm + "s5_dc")
    g["s5_b"] = _s5_matmul(proj, adj, kind="grad_b", out_dtype=F32, name=nm + "s5_db")
    du_s5 = _s5_matmul(adj, w["s5_b"], kind="reduce_t", out_dtype=BF16, name=nm + "s5_du",
                       extras=[(dy, "tile"), (w["s5_d"], "vec")], epi=lambda acc, dyv, d: acc + d * dyv)
    dzu, dzv, g["sgu_w"], g["sgu_bfull"], g["sgu_norm_g"], g["sgu_norm_b"] = _sgu_bwd(
        dmixed, proj, w["sgu_norm_g"], w["sgu_norm_b"], w["sgu_wm"], w["sgu_bfull"], nm + "sgu")
    dpooled, g["pool_w"], g["pool_scale"] = _pool_bwd_map(dmixed, s["pooled"], w["pool_w"], w["pool_scale"], nm + "pool_map")
    dp = _pool_bwd_window(dpooled, nm + "pool_win")
    do, dgate, g["dn_norm_g4"] = _gdn_post_bwd(dmixed, s["o"], proj, w["dn_norm_g4"], nm + "gdn_post")
    dvn, dwy, dkt, dqg, dqk, dgl = _gdn_rec_bwd(do, s["wy"], s["qg"], s["kt"], s["vn"], s["qk"], s["gl"], s["states"], nm + "gdn_rec")
    dq, dk, dv, dgb = _gdn_terms_bwd(s["qn"], s["kn"], s["v"], s["gb"], s["t_inv"], dvn, dwy, dkt, dqg, dqk, dgl, nm + "gdn_terms")
    dcq, dck, dcv, dab, g["dn_a_log"], g["dn_dt_bias"] = _gdn_pre_bwd(
        dq, dk, dv, s["cq"], s["ck"], s["cv"], dgb, s["gb"], proj_ab, w["dn_a_log"], w["dn_dt_bias"], nm + "gdn_pre")
    dxs, dws = [], []
    for p, dc in enumerate((dcq, dck, dcv)):
        dxp, dwp = _conv_bwd(dc, proj, QKV_BLK + p, w["dn_conv_w"][:, p * gw:(p + 1) * gw], nm + f"conv{p}")
        dxs.append(dxp)
        dws.append(dwp)
    dconv = jnp.concatenate(dws, axis=1)
    g["dn_conv_w"] = jnp.transpose(dconv.reshape(dconv.shape[0], N_DEV, 3 * gw // N_DEV), (1, 0, 2))
    dproj = jnp.concatenate([du_s5, dzu, dzv, dp] + dxs + [dgate], axis=1)
    xm = s["xm"]
    g["small"] = _unprep_grads(g, small)
    dw_main = _run(hosts, nm + "dw_main", lambda ops: _matmul(
        xm, dproj, mode="tn", tm=1024, tn=1024, tk=2048,out_dtype=wire, name=nm + "dw_main", comm=ops))
    dw_ab = _matmul(xm, dab, mode="tn", tm=1024, tn=AB_PAD, tk=1024, out_dtype=wire, name=nm + "dw_ab")
    dw_in = jnp.concatenate([dw_main, dw_ab[:, :2 * DN_HEADS]], axis=1)
    g["w_in"] = jnp.transpose(dw_in.reshape(D_MODEL, N_DEV, dw_in.shape[1] // N_DEV), (1, 0, 2))
    dx_ab = _matmul(dab, w["w_ab"], mode="nt", tm=1024, tn=1024, tk=AB_PAD, out_dtype=F32, name=nm + "dx_ab",
                    extras=[(dh1, (None, None), lambda i, j: (i, j))], epi=lambda acc, e: acc + ALPHA * e)
    return _run(hosts, nm + "dx", lambda ops: _matmul(
        dproj, w["w_main"], mode="nt", tm=1024, tn=1024, tk=2048,out_dtype=F32, name=nm + "dx",
        extras=[(dx_ab, (None, None), lambda i, j: (i, j))], epi=lambda acc, e: acc + e, comm=ops))


SMALL = ("s5_lambda_re", "s5_lambda_im", "s5_log_step", "s5_b_re", "s5_b_im", "s5_c_re", "s5_c_im", "s5_d", "s5_glu_b",
         "sgu_norm_g", "sgu_norm_b", "sgu_w", "sgu_b", "pool_w", "pool_scale", "dn_a_log", "dn_dt_bias", "dn_norm_g",
         "ln1_g", "ln1_b", "ln2_g", "ln2_b")
SHARDED = ("w_in", "s5_glu_w", "dn_conv_w", "w_out", "w_up", "w_down")


def _pad_lanes(v, width=AB_PAD):
    return jnp.pad(v.reshape(1, -1), ((0, 0), (0, width - v.size)))


def _prep_small(p):
    mx = MXU_DTYPE
    lbr, lbi, bbr, bbi = _s5_discretize(p["s5_lambda_re"], p["s5_lambda_im"], p["s5_log_step"], p["s5_b_re"], p["s5_b_im"])
    b_compact, c_compact = _s5_compact(bbr, bbi, p["s5_c_re"], p["s5_c_im"])
    causal = jnp.tril(jnp.ones((SGU_CHUNK, SGU_CHUNK), F32))
    return dict(
        s5_b=b_compact.astype(mx), s5_c=c_compact.astype(mx),
        s5_lam=jnp.concatenate([lbr.reshape(1, -1), lbi.reshape(1, -1)], axis=1),
        s5_lam_conj=jnp.concatenate([lbr.reshape(1, -1), -lbi.reshape(1, -1)], axis=1),
        s5_d=p["s5_d"].reshape(1, -1), s5_glu_b=p["s5_glu_b"].reshape(1, -1),
        sgu_norm_g=p["sgu_norm_g"].reshape(1, -1), sgu_norm_b=p["sgu_norm_b"].reshape(1, -1),
        sgu_wm=(p["sgu_w"] * causal).astype(mx), sgu_bfull=jnp.repeat(p["sgu_b"].T, GROUP_WIDTH // SGU_HEADS, axis=1),
        pool_w=p["pool_w"].astype(mx), pool_scale=p["pool_scale"].reshape(1, -1),
        dn_a_log=_pad_lanes(p["dn_a_log"]), dn_dt_bias=_pad_lanes(p["dn_dt_bias"]),
        dn_norm_g4=jnp.tile(p["dn_norm_g"].reshape(1, -1), (1, DN_HEADS)),
        ln1_g=p["ln1_g"].reshape(1, -1), ln1_b=p["ln1_b"].reshape(1, -1),
        ln2_g=p["ln2_g"].reshape(1, -1), ln2_b=p["ln2_b"].reshape(1, -1),
    )


def _weight_views(name, t):
    if name == "w_in":
        w_in = jnp.transpose(t, (1, 0, 2)).reshape(t.shape[1], N_DEV * t.shape[2])
        pad = AB_PAD - (w_in.shape[1] - MAIN_COLS)
        return dict(w_main=w_in[:, :MAIN_COLS], w_ab=jnp.pad(w_in[:, MAIN_COLS:], ((0, 0), (0, pad))))
    if name == "dn_conv_w":
        return dict(dn_conv_w=jnp.transpose(t, (1, 0, 2)).reshape(t.shape[1], N_DEV * t.shape[2]))
    if name == "w_up":
        return dict(w_up=t)
    return {name: t.reshape(N_DEV * t.shape[1], t.shape[2])}


def _unprep_grads(g, p):
    causal = jnp.tril(jnp.ones((SGU_CHUNK, SGU_CHUNK), F32))
    dbbr, dbbi = _s5_uncompact_b(g["s5_b"])
    dc_re, dc_im = _s5_uncompact_c(g["s5_c"])
    dlbr, dlbi = g["s5_lam"][0, :S5_NS].reshape(S5_GROUPS, S5_STATE), g["s5_lam"][0, S5_NS:].reshape(S5_GROUPS, S5_STATE)
    _, vjp = jax.vjp(_s5_discretize, p["s5_lambda_re"], p["s5_lambda_im"], p["s5_log_step"], p["s5_b_re"], p["s5_b_im"])
    d_lre, d_lim, d_step, d_bre, d_bim = vjp((dlbr, dlbi, dbbr, dbbi))
    hd = GROUP_WIDTH // SGU_HEADS
    return dict(
        s5_lambda_re=d_lre, s5_lambda_im=d_lim, s5_log_step=d_step, s5_b_re=d_bre, s5_b_im=d_bim, s5_c_re=dc_re, s5_c_im=dc_im,
        s5_d=g["s5_d"].reshape(S5_GROUPS, S5_CH), s5_glu_b=g["s5_glu_b"].reshape(-1),
        sgu_norm_g=g["sgu_norm_g"].reshape(-1), sgu_norm_b=g["sgu_norm_b"].reshape(-1), sgu_w=g["sgu_w"] * causal,
        sgu_b=jnp.sum(g["sgu_bfull"].reshape(SGU_CHUNK, SGU_HEADS, hd), axis=2).T,
        pool_w=g["pool_w"], pool_scale=g["pool_scale"].reshape(-1),
        dn_a_log=g["dn_a_log"][0, :DN_HEADS], dn_dt_bias=g["dn_dt_bias"][0, :DN_HEADS],
        dn_norm_g=jnp.sum(g["dn_norm_g4"].reshape(DN_HEADS, DN_HEAD_DIM), axis=0),
        ln1_g=g["ln1_g"].reshape(-1), ln1_b=g["ln1_b"].reshape(-1), ln2_g=g["ln2_g"].reshape(-1), ln2_b=g["ln2_b"].reshape(-1),
    )


def _local_step(x, target, ops, small, fwd_hosts, bwd_hosts, grads):
    saved = []
    h, hm = x, x.astype(MXU_DTYPE)
    for i in range(DEPTH):
        h, hm, s = _layer_fwd(h, hm, ops[i], i, fwd_hosts)
        saved.append(s)
    loss, dh = _loss_head(h, target)
    for i in reversed(range(DEPTH)):
        dh = _layer_bwd(dh, saved[i], ops[i], small[i], i, bwd_hosts, grads[i])
    return loss, dh


def _adamw(w, gparts, m, v, name):
    rr, c = w.shape
    ng = len(gparts)
    r = rr // ng
    lanes = -(-c // 128) * 128
    tr = r
    while tr * lanes * 4 * N_DEV > (4 << 20) and tr % 16 == 0:
        tr //= 2
    nb = r // tr

    def body(w_ref, *rest):
        g_refs, (m_ref, v_ref, go_ref, d_ref, mo_ref, vo_ref) = rest[:ng], rest[ng:]
        layer = pl.program_id(0)
        g = jnp.zeros(m_ref.shape, F32)
        for li in range(ng):
            gl = g_refs[li][0].astype(F32)
            for s in range(1, N_DEV):
                gl = gl + g_refs[li][s].astype(F32)
            g = jnp.where(layer == li, gl, g)
        mn = ADAM_B1 * m_ref[...] + (1.0 - ADAM_B1) * g
        vn = ADAM_B2 * v_ref[...] + (1.0 - ADAM_B2) * g * g
        m_hat = mn / (1.0 - ADAM_B1 ** ADAM_STEP)
        v_hat = vn / (1.0 - ADAM_B2 ** ADAM_STEP)
        go_ref[...] = g
        d_ref[...] = -ADAM_LR * (m_hat / (jnp.sqrt(v_hat) + ADAM_EPS) + ADAM_WD * w_ref[...])
        mo_ref[...] = mn
        vo_ref[...] = vn

    row = pl.BlockSpec((tr, c), lambda li, i: (li * nb + i, 0))
    part_specs = [pl.BlockSpec((N_DEV, tr, c), functools.partial(lambda li, i, k: (0, jnp.where(li == k, i, 0), 0), k=k))
                  for k in range(ng)]
    return pl.pallas_call(
        body, name=name, grid=(ng, nb), in_specs=[row] + part_specs + [row, row],
        out_specs=[row] * 4, out_shape=[jax.ShapeDtypeStruct((rr, c), F32)] * 4, compiler_params=_params(("arbitrary", "arbitrary")),
    )(w, *gparts, m, v)


PACK_LANES = 128
PACK_ROWS = 4096


PACK_TILE = 8 * PACK_LANES


def _pack_rows(t):
    return -(-t.size // PACK_TILE) * 8


def _pack(vals):
    rows = []
    for t in vals:
        flat = t.reshape(-1)
        n_rows = _pack_rows(t)
        rows.append(jnp.pad(flat, (0, n_rows * PACK_LANES - flat.size)).reshape(n_rows, PACK_LANES))
    used = sum(r.shape[0] for r in rows)
    assert used <= PACK_ROWS, used
    return jnp.concatenate(rows + [jnp.zeros((PACK_ROWS - used, PACK_LANES), F32)], axis=0)


def _unpack(packed, like):
    out, off = [], 0
    for t in like:
        n_rows = _pack_rows(t)
        out.append(packed[off:off + n_rows].reshape(-1)[:t.size].reshape(t.shape))
        off += n_rows
    return out


def kernel(x, w_in, s5_lambda_re, s5_lambda_im, s5_log_step, s5_b_re, s5_b_im, s5_c_re, s5_c_im, s5_d, s5_glu_w, s5_glu_b, sgu_norm_g, sgu_norm_b, sgu_w, sgu_b, pool_w, pool_scale, dn_conv_w, dn_a_log, dn_dt_bias, dn_norm_g, w_out, ln1_g, ln1_b, w_up, w_down, ln2_g, ln2_b, loss_target, m_w_in, m_s5_lambda_re, m_s5_lambda_im, m_s5_log_step, m_s5_b_re, m_s5_b_im, m_s5_c_re, m_s5_c_im, m_s5_d, m_s5_glu_w, m_s5_glu_b, m_sgu_norm_g, m_sgu_norm_b, m_sgu_w, m_sgu_b, m_pool_w, m_pool_scale, m_dn_conv_w, m_dn_a_log, m_dn_dt_bias, m_dn_norm_g, m_w_out, m_ln1_g, m_ln1_b, m_w_up, m_w_down, m_ln2_g, m_ln2_b, v_w_in, v_s5_lambda_re, v_s5_lambda_im, v_s5_log_step, v_s5_b_re, v_s5_b_im, v_s5_c_re, v_s5_c_im, v_s5_d, v_s5_glu_w, v_s5_glu_b, v_sgu_norm_g, v_sgu_norm_b, v_sgu_w, v_sgu_b, v_pool_w, v_pool_scale, v_dn_conv_w, v_dn_a_log, v_dn_dt_bias, v_dn_norm_g, v_w_out, v_ln1_g, v_ln1_b, v_w_up, v_w_down, v_ln2_g, v_ln2_b):
    names = ("w_in", "s5_lambda_re", "s5_lambda_im", "s5_log_step", "s5_b_re", "s5_b_im", "s5_c_re", "s5_c_im", "s5_d", "s5_glu_w",
             "s5_glu_b", "sgu_norm_g", "sgu_norm_b", "sgu_w", "sgu_b", "pool_w", "pool_scale", "dn_conv_w", "dn_a_log", "dn_dt_bias",
             "dn_norm_g", "w_out", "ln1_g", "ln1_b", "w_up", "w_down", "ln2_g", "ln2_b")
    env = locals()
    w = {n: env[n] for n in names}
    m = {n: env["m_" + n] for n in names}
    v = {n: env["v_" + n] for n in names}

    wire = [{n: (w[n][i] if n == "dn_conv_w" else w[n][i].astype(MXU_DTYPE)) for n in SHARDED} for i in range(DEPTH)]
    small = [{n: w[n][i] for n in SMALL} for i in range(DEPTH)]
    ops = [_prep_small(small[i]) for i in range(DEPTH)]
    grads = [{} for _ in range(DEPTH)]
    recv = [{} for _ in range(DEPTH)]
    first = ("w_in", "s5_glu_w", "dn_conv_w", "w_out")

    def gather(layer, group):
        def take(outs):
            for n, t in zip(group, outs):
                ops[layer].update(_weight_views(n, t))
        return (lambda: [(wire[layer][n], False) for n in group]), take

    def scatter(layer, group, with_small=False):
        def make():
            sends = [(grads[layer][n], True) for n in group]
            if with_small:
                sends.append((_pack([grads[layer]["small"][n] for n in SMALL]), False))
            return sends
        def take(outs):
            recv[layer].update(dict(zip(group + (("small",) if with_small else ()), outs)))
        return make, take

    make, take = gather(0, first)
    take(_exchange(make(), "gather_first"))
    fwd_hosts = {"l0_proj": gather(0, ("w_up",)), "l0_up": gather(0, ("w_down",)), "l0_down": gather(1, first),
                 "l1_proj": gather(1, ("w_up",)), "l1_up": gather(1, ("w_down",))}
    late = ("w_in", "s5_glu_w", "dn_conv_w")
    bwd_hosts = {"l1b_dpre": scatter(1, ("w_down",)), "l1b_dx1": scatter(1, ("w_up",)), "l1b_dmixed": scatter(1, ("w_out",)),
                 "l0b_dw_down": scatter(1, late, with_small=True),
                 "l0b_dpre": scatter(0, ("w_down",)), "l0b_dx1": scatter(0, ("w_up",)), "l0b_dmixed": scatter(0, ("w_out",)),
                 "l0b_dw_main": scatter(0, ("s5_glu_w", "dn_conv_w"), with_small=True), "l0b_dx": scatter(0, ("w_in",))}
    loss, grad_x = _local_step(x[0], loss_target[0], ops, small, fwd_hosts, bwd_hosts, grads)

    g_out, d_out, m_out, v_out = {}, {}, {}, {}
    for n in SHARDED:
        shp = w[n].shape
        pad = (-shp[1]) % 8
        def rows(t):
            return jnp.pad(t, ((0, 0), (0, pad), (0, 0))).reshape(shp[0] * (shp[1] + pad), shp[2])
        res = _adamw(rows(w[n]), [recv[i][n] for i in range(DEPTH)], rows(m[n]), rows(v[n]), "adamw_" + n)
        g_out[n], d_out[n], m_out[n], v_out[n] = (t.reshape(shp[0], shp[1] + pad, shp[2])[:, :shp[1]] for t in res)
    def packed(src):
        return jnp.concatenate([_pack([src[n][i] for n in SMALL]) for i in range(DEPTH)], axis=0)
    res = _adamw(packed(w), [recv[i]["small"] for i in range(DEPTH)], packed(m), packed(v), "adamw_small")
    like = [w[n][0] for n in SMALL]
    for dst, pk in zip((g_out, d_out, m_out, v_out), res):
        per_layer = [_unpack(pk[i * PACK_ROWS:(i + 1) * PACK_ROWS], like) for i in range(DEPTH)]
        dst.update({n: jnp.stack([per_layer[i][k] for i in range(DEPTH)]) for k, n in enumerate(SMALL)})

    total = lax.psum(loss[0, 0], MESH_AXES)
    return (total, grad_x[None], *[g_out[n] for n in names], *[d_out[n] for n in names],
            *[m_out[n] for n in names], *[v_out[n] for n in names])
```

```python
import functools
import math

import jax
import jax.numpy as jnp
from jax import lax
from jax.experimental import pallas as pl
from jax.experimental.pallas import tpu as pltpu

F32 = jnp.float32
BF16 = jnp.bfloat16
MXU_DTYPE = jnp.bfloat16
HI = lax.Precision.HIGHEST

N_DEV = 8
D_MODEL = 2048
DEPTH = 2
GROUP_WIDTH = 512
S5_GROUPS, S5_CH, S5_STATE = 32, 16, 64
S5_NS = S5_GROUPS * S5_STATE
SGU_CHUNK, SGU_HEADS = 128, 8
POOL_WINDOWS = (2, 4, 8, 16)
DN_HEADS, DN_HEAD_DIM, DN_CONV, DN_CHUNK = 4, 128, 4, 64
D_FF = 4 * D_MODEL
LN_EPS, RMS_EPS, L2_EPS = 1e-5, 1e-6, 1e-6
ALPHA = (2 * DEPTH) ** 0.25
MAIN_COLS = 4096
AB_PAD = 128
ADAM_LR, ADAM_B1, ADAM_B2, ADAM_EPS, ADAM_WD, ADAM_STEP = 0.001, 0.9, 0.999, 1e-08, 0.01, 10
VMEM_LIMIT = 56 * 1024 * 1024
C_GELU = math.sqrt(2.0 / math.pi)


def _params(sem=None):
    return pltpu.CompilerParams(dimension_semantics=sem, vmem_limit_bytes=VMEM_LIMIT)


def _gelu(x):
    return 0.5 * x * (1.0 + jnp.tanh(C_GELU * (x + 0.044715 * x * x * x)))


def _gelu_grad(x):
    t = jnp.tanh(C_GELU * (x + 0.044715 * x * x * x))
    return 0.5 * (1.0 + t) + 0.5 * x * (1.0 - t * t) * C_GELU * (1.0 + 3.0 * 0.044715 * x * x)


def _sigmoid(x):
    return 1.0 / (1.0 + jnp.exp(-x))


def _silu(x):
    return x * _sigmoid(x)


def _silu_grad(x):
    s = _sigmoid(x)
    return s * (1.0 + x * (1.0 - s))


def _softplus(x):
    z = jnp.exp(-jnp.abs(x))
    small = z * (1.0 - z * (0.5 - z * (1.0 / 3.0)))
    return jnp.maximum(x, 0.0) + jnp.where(z < 1e-2, small, jnp.log(1.0 + z))


def _mx(x):
    return x.astype(MXU_DTYPE)


def _dot(a, b, dims="nn", precision=None):
    cd = {"nn": ((1,), (0,)), "nt": ((1,), (1,)), "tn": ((0,), (0,))}[dims]
    return lax.dot_general(a, b, (cd, ((), ())), preferred_element_type=F32, precision=precision)


def _mdot(a, b, dims="nn"):
    return _dot(_mx(a), _mx(b), dims)


MESH_AXES = ("x", "y", "c")
OFFSETS = [(dx, dy, dc) for dx in (0, 1) for dy in (0, 1) for dc in (0, 1)][1:]


def _me_and_peers():
    x, y, c = (lax.axis_index(a) for a in MESH_AXES)
    def flip(v, d):
        return 1 - v if d else v
    peers = [(flip(x, dx), flip(y, dy), flip(c, dc)) for dx, dy, dc in OFFSETS]
    def idx(p):
        return 4 * p[0] + 2 * p[1] + p[2]
    return idx((x, y, c)), peers, [idx(p) for p in peers]


SIBLING = OFFSETS.index((0, 0, 1))
SAME_CORE = [OFFSETS.index(f) for f in ((0, 1, 0), (1, 0, 0), (1, 1, 0))]


class _Comm:
    def __init__(self, ops):
        self.arrays = [a for a, _ in ops]
        self.scatter = [s for _, s in ops]
        self.n = n = len(ops)
        hbm = pl.BlockSpec(memory_space=pltpu.HBM)
        self.in_specs, self.out_specs = [hbm] * n, [hbm] * n
        self.out_shape = [jax.ShapeDtypeStruct(a.shape if s else (N_DEV,) + a.shape, a.dtype) for a, s in ops]
        npeer = len(OFFSETS)
        self.scratch = [pltpu.SemaphoreType.DMA((n, npeer)), pltpu.SemaphoreType.DMA((n, npeer)), pltpu.SemaphoreType.DMA((n,))]

    def _plan(self, ins, outs, sems, waiting):
        send_sems, recv_sems, local_sems = sems
        me, peers, peer_idx = _me_and_peers()

        def remote(k, d, src, dst, to):
            return pltpu.make_async_remote_copy(src_ref=src, dst_ref=dst, send_sem=send_sems.at[k, d], recv_sem=recv_sems.at[k, d],
                                                device_id=to, device_id_type=pl.DeviceIdType.MESH)
        plan = []
        for k in range(self.n):
            every = range(len(OFFSETS))
            if self.scatter[k]:
                local = pltpu.make_async_copy(ins[k].at[me], outs[k].at[me], local_sems.at[k])
                pushes = [remote(k, d, ins[k].at[peer_idx[d]], outs[k].at[me], peers[d]) for d in every]
                onward = []
            else:
                local = pltpu.make_async_copy(ins[k], outs[k].at[me], local_sems.at[k])
                pushes = [remote(k, d, ins[k], outs[k].at[me], peers[d]) for d in [SIBLING] + SAME_CORE]
                onward = SAME_CORE
            passed, arrivals = [], {}
            if waiting:
                passed = [(d, remote(k, d + 1, outs[k].at[peer_idx[d]], outs[k].at[peer_idx[d]], peers[SIBLING])) for d in onward]
                arrivals = {d: remote(k, d, outs[k].at[peer_idx[d]], outs[k].at[peer_idx[d]], peers[d]) for d in every}
            plan.append((local, pushes, passed, arrivals))
        return plan

    def start(self, ins, outs, sems):
        for local, pushes, _, _ in self._plan(ins, outs, sems, False):
            local.start()
            for cp in pushes:
                cp.start()

    def wait(self, ins, outs, sems):
        plan = self._plan(ins, outs, sems, True)
        for _, _, passed, arrivals in plan:
            for d, onward in passed:
                arrivals.pop(d).wait_recv()
                onward.start()
        for local, pushes, passed, arrivals in plan:
            for cp in arrivals.values():
                cp.wait_recv()
            for cp in pushes + [onward for _, onward in passed]:
                cp.wait_send()
            local.wait()


def _exchange(ops, name):
    cm = _Comm(ops)

    def body(*refs):
        ins, outs, sems = refs[:cm.n], refs[cm.n:2 * cm.n], refs[2 * cm.n:]
        cm.start(ins, outs, sems)
        cm.wait(ins, outs, sems)

    return pl.pallas_call(body, name=name, in_specs=cm.in_specs, out_specs=cm.out_specs, out_shape=cm.out_shape,
                          scratch_shapes=cm.scratch)(*cm.arrays)


def _matmul(a, b, *, mode, tm, tn, tk, out_dtype, name, a_fn=None, extras=(), epi=None, a_cols=None,
            b_slab=None, out_slab=None, comm=None):
    a_shape = a.shape if a_cols is None else (a.shape[0], a_cols)
    b_shape = b.shape if b_slab is None else (b.shape[1], N_DEV * b_slab)
    if mode == "nn":
        (m, k), n = a_shape, b_shape[1]
    elif mode == "nt":
        (m, k), n = a_shape, b_shape[0]
    else:
        (k, m), n = a_shape, b_shape[1]
    tm, tn, tk = min(tm, m), min(tn, n), min(tk, k)
    if b_slab is not None:
        tn, tk = (tn, min(tk, b_slab)) if mode == "nt" else (min(tn, b_slab), tk)
    assert m % tm == 0 and n % tn == 0 and k % tk == 0, (name, a.shape, b.shape, tm, tn, tk)
    gi, gj, nk = m // tm, n // tn, k // tk
    n_ex = len(extras)
    cm = _Comm(comm) if comm else None
    nc = cm.n if cm else 0

    def body(a_ref, b_ref, *rest):
        ex_refs, rest = rest[:n_ex], rest[n_ex:]
        c_ins, o_ref, c_outs, acc, sems = rest[:nc], rest[nc], rest[nc + 1:2 * nc + 1], rest[2 * nc + 1], rest[2 * nc + 2:]
        i, j, kk = pl.program_id(0), pl.program_id(1), pl.program_id(2)
        if cm:
            @pl.when((i == 0) & (j == 0) & (kk == 0))
            def _():
                cm.start(c_ins, c_outs, sems)

        av = a_ref[...]
        if a_fn is not None:
            av = a_fn(av)
        part = _dot(_mx(av), _mx(b_ref[...]), mode)

        def finish(r):
            if epi is not None:
                r = epi(r, *[e[...] for e in ex_refs])
            o_ref[...] = r.astype(out_dtype)

        if nk == 1:
            finish(part)
        else:
            @pl.when(kk == 0)
            def _():
                acc[...] = part

            @pl.when((kk > 0) & (kk < nk - 1))
            def _():
                acc[...] += part

            @pl.when(kk == nk - 1)
            def _():
                finish(acc[...] + part)

        if cm:
            @pl.when((i == gi - 1) & (j == gj - 1) & (kk == nk - 1))
            def _():
                cm.wait(c_ins, c_outs, sems)

    a_spec = pl.BlockSpec((tk, tm), lambda i, j, kk: (kk, i)) if mode == "tn" else pl.BlockSpec((tm, tk), lambda i, j, kk: (i, kk))
    if b_slab is None:
        b_spec = pl.BlockSpec((tn, tk), lambda i, j, kk: (j, kk)) if mode == "nt" else pl.BlockSpec((tk, tn), lambda i, j, kk: (kk, j))
    elif mode == "nt":
        assert b_slab % tk == 0
        b_spec = pl.BlockSpec((None, tn, tk), lambda i, j, kk: ((kk * tk) // b_slab, j, ((kk * tk) % b_slab) // tk))
    else:
        assert b_slab % tn == 0
        b_spec = pl.BlockSpec((None, tk, tn), lambda i, j, kk: ((j * tn) // b_slab, kk, ((j * tn) % b_slab) // tn))
    if out_slab is None:
        o_spec, o_shape = pl.BlockSpec((tm, tn), lambda i, j, kk: (i, j)), jax.ShapeDtypeStruct((m, n), out_dtype)
    else:
        assert out_slab % tn == 0 and n == N_DEV * out_slab
        o_spec = pl.BlockSpec((None, tm, tn), lambda i, j, kk: ((j * tn) // out_slab, i, ((j * tn) % out_slab) // tn))
        o_shape = jax.ShapeDtypeStruct((N_DEV, m, out_slab), out_dtype)
    ex_specs = [pl.BlockSpec((tm if bs[0] is None else bs[0], tn if bs[1] is None else bs[1]),
                             functools.partial(lambda i, j, kk, f: f(i, j), f=im)) for (_, bs, im) in extras]
    res = pl.pallas_call(
        body,
        name=name,
        grid=(gi, gj, nk),
        in_specs=[a_spec, b_spec, *ex_specs] + (cm.in_specs if cm else []),
        out_specs=[o_spec] + (cm.out_specs if cm else []),
        out_shape=[o_shape] + (cm.out_shape if cm else []),
        scratch_shapes=[pltpu.VMEM((tm, tn) if nk > 1 else (8, 128), F32)] + (cm.scratch if cm else []),
        compiler_params=_params(("arbitrary",) * 3 if cm else ("parallel", "parallel", "arbitrary")),
    )(a, b, *[e[0] for e in extras], *(cm.arrays if cm else []))
    return (res[0], res[1:]) if cm else res[0]


def _ln_fwd(x, y, g, b, name):
    l, d = x.shape
    tl = 256

    def body(x_ref, y_ref, g_ref, b_ref, h_ref, o_ref, om_ref):
        h = ALPHA * x_ref[...] + y_ref[...]
        mu = jnp.mean(h, axis=-1, keepdims=True)
        c = h - mu
        var = jnp.mean(c * c, axis=-1, keepdims=True)
        h_ref[...] = h
        out = c * lax.rsqrt(var + LN_EPS) * g_ref[...] + b_ref[...]
        o_ref[...] = out
        om_ref[...] = out.astype(om_ref.dtype)

    row = pl.BlockSpec((tl, d), lambda i: (i, 0))
    vec = pl.BlockSpec((1, d), lambda i: (0, 0))
    return pl.pallas_call(
        body, name=name, grid=(l // tl,), in_specs=[row, row, vec, vec], out_specs=[row, row, row],
        out_shape=[jax.ShapeDtypeStruct((l, d), F32)] * 2 + [jax.ShapeDtypeStruct((l, d), MXU_DTYPE)],
        compiler_params=_params(("parallel",)),
    )(x, y, g, b)


def _ln_bwd(dout, h, g, name):
    l, d = h.shape
    tl = 256

    def body(do_ref, h_ref, g_ref, dh_ref, dhm_ref, dg_ref, db_ref):
        @pl.when(pl.program_id(0) == 0)
        def _():
            dg_ref[...] = jnp.zeros_like(dg_ref)
            db_ref[...] = jnp.zeros_like(db_ref)

        hv, do = h_ref[...], do_ref[...]
        mu = jnp.mean(hv, axis=-1, keepdims=True)
        c = hv - mu
        r = lax.rsqrt(jnp.mean(c * c, axis=-1, keepdims=True) + LN_EPS)
        xh = c * r
        dxh = do * g_ref[...]
        m1 = jnp.mean(dxh, axis=-1, keepdims=True)
        m2 = jnp.mean(dxh * xh, axis=-1, keepdims=True)
        dh = r * (dxh - m1 - xh * m2)
        dh_ref[...] = dh
        dhm_ref[...] = dh.astype(dhm_ref.dtype)
        dg_ref[...] += jnp.sum(do * xh, axis=0, keepdims=True)
        db_ref[...] += jnp.sum(do, axis=0, keepdims=True)

    row = pl.BlockSpec((tl, d), lambda i: (i, 0))
    vec = pl.BlockSpec((1, d), lambda i: (0, 0))
    return pl.pallas_call(
        body, name=name, grid=(l // tl,), in_specs=[row, row, vec], out_specs=[row, row, vec, vec],
        out_shape=[jax.ShapeDtypeStruct((l, d), F32), jax.ShapeDtypeStruct((l, d), MXU_DTYPE),
                   jax.ShapeDtypeStruct((1, d), F32), jax.ShapeDtypeStruct((1, d), F32)],
        compiler_params=_params(("arbitrary",)),
    )(dout, h, g)


def _loss_head(y, target):
    l, d = y.shape
    tl = 256

    def body(y_ref, t_ref, loss_ref, dy_ref):
        @pl.when(pl.program_id(0) == 0)
        def _():
            loss_ref[...] = jnp.zeros_like(loss_ref)

        e = y_ref[...] - t_ref[...]
        dy_ref[...] = e * (1.0 / d)
        s = jnp.sum(jnp.sum(e * e, axis=1, keepdims=True), axis=0, keepdims=True)
        loss_ref[...] += s * (0.5 / d)

    row = pl.BlockSpec((tl, d), lambda i: (i, 0))
    return pl.pallas_call(
        body, name="loss_head", grid=(l // tl,), in_specs=[row, row],
        out_specs=[pl.BlockSpec((1, 1), lambda i: (0, 0)), row],
        out_shape=[jax.ShapeDtypeStruct((1, 1), F32), jax.ShapeDtypeStruct((l, d), F32)],
        compiler_params=_params(("arbitrary",)),
    )(y, target)


def _s5_discretize(lam_re, lam_im, log_step, b_re, b_im):
    step = jnp.exp(log_step)[:, None]
    e = jnp.exp(lam_re * step)
    lbr, lbi = e * jnp.cos(lam_im * step), e * jnp.sin(lam_im * step)
    den = lam_re * lam_re + lam_im * lam_im
    qr = ((lbr - 1.0) * lam_re + lbi * lam_im) / den
    qi = (lbi * lam_re - (lbr - 1.0) * lam_im) / den
    bbr = qr[:, :, None] * b_re - qi[:, :, None] * b_im
    bbi = qr[:, :, None] * b_im + qi[:, :, None] * b_re
    return lbr, lbi, bbr, bbi


S5_TILES, S5_SLABS = 4, 8
S5_TILE_W, S5_SLAB_W = GROUP_WIDTH // S5_TILES, S5_NS // S5_TILES
S5_GPT = S5_GROUPS // S5_TILES


def _s5_compact(bbr, bbi, c_re, c_im):
    eye = jnp.eye(S5_GPT, dtype=F32)
    def bd(t):
        return jnp.einsum("tgph,gk->tghkp", t.reshape(S5_TILES, S5_GPT, S5_STATE, S5_CH), eye).reshape(S5_TILES, S5_TILE_W, S5_SLAB_W)
    def cd(t):
        return jnp.einsum("tghp,gk->tgpkh", t.reshape(S5_TILES, S5_GPT, S5_CH, S5_STATE), eye).reshape(S5_TILES, S5_SLAB_W, S5_TILE_W)
    return jnp.concatenate([bd(bbr), bd(bbi)], axis=0), jnp.concatenate([cd(c_re), -cd(c_im)], axis=0)


def _s5_uncompact_b(db):
    eye = jnp.eye(S5_GPT, dtype=F32)[None, :, None, :, None]
    def ex(t):
        d = jnp.sum(t.reshape(S5_TILES, S5_GPT, S5_CH, S5_GPT, S5_STATE) * eye, axis=3)
        return jnp.transpose(d, (0, 1, 3, 2)).reshape(S5_GROUPS, S5_STATE, S5_CH)
    return ex(db[:S5_TILES]), ex(db[S5_TILES:])


def _s5_uncompact_c(dc):
    eye = jnp.eye(S5_GPT, dtype=F32)[None, :, None, :, None]
    def ex(t):
        d = jnp.sum(t.reshape(S5_TILES, S5_GPT, S5_STATE, S5_GPT, S5_CH) * eye, axis=3)
        return jnp.transpose(d, (0, 1, 3, 2)).reshape(S5_GROUPS, S5_CH, S5_STATE)
    return ex(dc[:S5_TILES]), -ex(dc[S5_TILES:])


S5_ROWS = 256


def _s5_tile(j):
    t = j % S5_TILES
    return slice(t * S5_TILE_W, (t + 1) * S5_TILE_W)


def _s5_slab(j):
    return slice(j * S5_SLAB_W, (j + 1) * S5_SLAB_W)


def _s5_recur(src, lam_ref, carry, emit, n_rows, reverse, extra=()):
    ns = S5_NS
    lr, li = lam_ref[:, :ns], lam_ref[:, ns:]

    def step(t, c):
        row = (n_rows - 1 - t) if reverse else t
        cr, ci = c[0], c[1]
        nr = lr * cr - li * ci + src[pl.ds(row, 1), :ns]
        ni = lr * ci + li * cr + src[pl.ds(row, 1), ns:]
        return (nr, ni) + tuple(emit(row, nr, ni, cr, ci, c[2:]))

    fin = lax.fori_loop(0, n_rows, step, (carry[:, :ns], carry[:, ns:]) + tuple(extra))
    carry[:, :ns] = fin[0]
    carry[:, ns:] = fin[1]
    return fin[2:]


def _s5_fwd(proj, b, c, lam, d, name):
    l = proj.shape[0]
    tl = min(S5_ROWS, l)
    w = 2 * S5_NS

    def body(u_ref, b_ref, c_ref, lam_ref, d_ref, hs_ref, y_ref, bu, carry):
        @pl.when(pl.program_id(0) == 0)
        def _():
            carry[...] = jnp.zeros_like(carry)

        u = u_ref[...]
        um = _mx(u)
        for j in range(S5_SLABS):
            bu[:, _s5_slab(j)] = _dot(um[:, _s5_tile(j)], b_ref[j])

        def emit(row, nr, ni, cr, ci, extra):
            hs_ref[pl.ds(row, 1), :S5_NS] = nr
            hs_ref[pl.ds(row, 1), S5_NS:] = ni
            return extra

        _s5_recur(bu, lam_ref, carry, emit, tl, False)
        for t in range(S5_TILES):
            acc = _dot(_mx(hs_ref[:, _s5_slab(t)]), c_ref[t]) + _dot(_mx(hs_ref[:, _s5_slab(S5_TILES + t)]), c_ref[S5_TILES + t])
            y_ref[:, _s5_tile(t)] = acc + d_ref[:, _s5_tile(t)] * u[:, _s5_tile(t)]

    row = lambda width: pl.BlockSpec((tl, width), lambda i: (i, 0))
    full = lambda a: pl.BlockSpec(a.shape, lambda i: (0,) * a.ndim)
    return pl.pallas_call(
        body, name=name, grid=(l // tl,), in_specs=[row(GROUP_WIDTH), full(b), full(c), full(lam), full(d)],
        out_specs=[row(w), row(GROUP_WIDTH)],
        out_shape=[jax.ShapeDtypeStruct((l, w), F32), jax.ShapeDtypeStruct((l, GROUP_WIDTH), F32)],
        scratch_shapes=[pltpu.VMEM((tl, w), F32), pltpu.VMEM((1, w), F32)], compiler_params=_params(("arbitrary",)),
    )(proj, b, c, lam, d)


def _s5_bwd(dy, hs, proj, b, c, lam_conj, d, name):
    l = dy.shape[0]
    tl = min(S5_ROWS, l)
    nb = l // tl
    w = 2 * S5_NS

    def body(dy_ref, hs_ref, u_ref, b_ref, c_ref, lam_ref, d_ref, du_ref, db_ref, dc_ref, dl_ref, dh, adj, carry):
        @pl.when(pl.program_id(0) == 0)
        def _():
            carry[...] = jnp.zeros_like(carry)
            db_ref[...] = jnp.zeros_like(db_ref)
            dc_ref[...] = jnp.zeros_like(dc_ref)
            dl_ref[...] = jnp.zeros_like(dl_ref)

        dyv = dy_ref[...]
        dym, um = _mx(dyv), _mx(u_ref[...])
        for j in range(S5_SLABS):
            dh[:, _s5_slab(j)] = _dot(dym[:, _s5_tile(j)], c_ref[j], "nt")

        def emit(row, nr, ni, cr, ci, extra):
            adj[pl.ds(row, 1), :S5_NS] = nr
            adj[pl.ds(row, 1), S5_NS:] = ni
            hr, hi = hs_ref[pl.ds(row, 1), :S5_NS], hs_ref[pl.ds(row, 1), S5_NS:]
            return extra[0] + cr * hr + ci * hi, extra[1] + ci * hr - cr * hi

        dl = _s5_recur(dh, lam_ref, carry, emit, tl, True, extra=(dl_ref[:, :S5_NS], dl_ref[:, S5_NS:]))
        dl_ref[:, :S5_NS] = dl[0]
        dl_ref[:, S5_NS:] = dl[1]
        for t in range(S5_TILES):
            acc = (_dot(_mx(adj[:, _s5_slab(t)]), b_ref[t], "nt")
                   + _dot(_mx(adj[:, _s5_slab(S5_TILES + t)]), b_ref[S5_TILES + t], "nt"))
            du_ref[:, _s5_tile(t)] = (acc + d_ref[:, _s5_tile(t)] * dyv[:, _s5_tile(t)]).astype(du_ref.dtype)
        for j in range(S5_SLABS):
            dc_ref[j] += _dot(_mx(hs_ref[:, _s5_slab(j)]), dym[:, _s5_tile(j)], "tn")
            db_ref[j] += _dot(um[:, _s5_tile(j)], _mx(adj[:, _s5_slab(j)]), "tn")

    row = lambda width: pl.BlockSpec((tl, width), lambda i: (nb - 1 - i, 0))
    full = lambda a: pl.BlockSpec(a.shape, lambda i: (0,) * a.ndim)
    acc3 = lambda shape: pl.BlockSpec(shape, lambda i: (0, 0, 0))
    return pl.pallas_call(
        body, name=name, grid=(nb,),
        in_specs=[row(GROUP_WIDTH), row(w), row(GROUP_WIDTH), full(b), full(c), full(lam_conj), full(d)],
        out_specs=[row(GROUP_WIDTH), acc3(b.shape), acc3(c.shape), pl.BlockSpec((1, w), lambda i: (0, 0))],
        out_shape=[jax.ShapeDtypeStruct((l, GROUP_WIDTH), BF16), jax.ShapeDtypeStruct(b.shape, F32),
                   jax.ShapeDtypeStruct(c.shape, F32), jax.ShapeDtypeStruct((1, w), F32)],
        scratch_shapes=[pltpu.VMEM((tl, w), F32), pltpu.VMEM((tl, w), F32), pltpu.VMEM((1, w), F32)],
        compiler_params=_params(("arbitrary",)),
    )(dy, hs, proj, b, c, lam_conj, d)


def _s5_glu_fwd(y, glu_w, glu_b, name):
    l, d = y.shape
    tl = min(512, l)

    def body(y_ref, w_ref, b_ref, o_ref):
        yg = _gelu(y_ref[...])
        z = _mdot(yg, w_ref[...]) + b_ref[...]
        o_ref[...] = (yg * _sigmoid(z)).astype(o_ref.dtype)

    return pl.pallas_call(
        body, name=name, grid=(l // tl,),
        in_specs=[pl.BlockSpec((tl, d), lambda i: (i, 0)), pl.BlockSpec((d, d), lambda i: (0, 0)), pl.BlockSpec((1, d), lambda i: (0, 0))],
        out_specs=pl.BlockSpec((tl, d), lambda i: (i, 0)), out_shape=jax.ShapeDtypeStruct((l, d), BF16),
        compiler_params=_params(("parallel",)),
    )(y, glu_w, glu_b)


def _s5_glu_bwd(dmixed, y, proj, glu_w, glu_b, name):
    l, d = y.shape
    tl = min(512, l)

    def body(do_ref, y_ref, u_ref, w_ref, b_ref, dy_ref, dz_ref, yg_ref, db_ref, dd_ref):
        @pl.when(pl.program_id(0) == 0)
        def _():
            db_ref[...] = jnp.zeros_like(db_ref)
            dd_ref[...] = jnp.zeros_like(dd_ref)

        yv, do = y_ref[...], do_ref[...]
        yg = _gelu(yv)
        gate = _sigmoid(_mdot(yg, w_ref[...]) + b_ref[...])
        dz = do * yg * gate * (1.0 - gate)
        dyg = do * gate + _mdot(dz, w_ref[...], "nt")
        dy = dyg * _gelu_grad(yv)
        dy_ref[...] = dy
        dz_ref[...] = dz.astype(dz_ref.dtype)
        yg_ref[...] = yg.astype(yg_ref.dtype)
        db_ref[...] += jnp.sum(dz, axis=0, keepdims=True)
        dd_ref[...] += jnp.sum(dy * u_ref[...], axis=0, keepdims=True)

    row = pl.BlockSpec((tl, d), lambda i: (i, 0))
    vec = pl.BlockSpec((1, d), lambda i: (0, 0))
    return pl.pallas_call(
        body, name=name, grid=(l // tl,),
        in_specs=[row, row, row, pl.BlockSpec((d, d), lambda i: (0, 0)), vec],
        out_specs=[row, row, row, vec, vec],
        out_shape=[jax.ShapeDtypeStruct((l, d), F32), jax.ShapeDtypeStruct((l, d), BF16), jax.ShapeDtypeStruct((l, d), BF16),
                   jax.ShapeDtypeStruct((1, d), F32), jax.ShapeDtypeStruct((1, d), F32)],
        compiler_params=_params(("arbitrary",)),
    )(dmixed, y, proj, glu_w, glu_b)


def _sgu_pair(w_ref, x, j, dims):
    lo = lax.broadcasted_iota(jnp.int32, x.shape, 1) < (GROUP_WIDTH // SGU_HEADS)
    xb = _mx(x)
    r0 = _dot(w_ref[2 * j], xb, dims)
    r1 = _dot(w_ref[2 * j + 1], xb, dims)
    return jnp.where(lo, r0, r1)


def _sgu_norm(v, g, b):
    mu = jnp.mean(v, axis=-1, keepdims=True)
    c = v - mu
    r = lax.rsqrt(jnp.mean(c * c, axis=-1, keepdims=True) + LN_EPS)
    return c * r, r


def _sgu_fwd(proj, norm_g, norm_b, wm, bfull, name):
    l = proj.shape[0]
    tl = 256
    gw = GROUP_WIDTH

    def body(zu_ref, zv_ref, g_ref, b_ref, w_ref, bf_ref, o_ref):
        for c in range(tl // SGU_CHUNK):
            rows = slice(c * SGU_CHUNK, (c + 1) * SGU_CHUNK)
            u = _gelu(zu_ref[rows, :])
            vh, _ = _sgu_norm(_gelu(zv_ref[rows, :]), None, None)
            vn = vh * g_ref[...] + b_ref[...]
            for j in range(gw // 128):
                cols = slice(j * 128, (j + 1) * 128)
                mixed = _sgu_pair(w_ref, vn[:, cols], j, "nn") + bf_ref[:, cols]
                o_ref[rows, cols] = (u[:, cols] * mixed).astype(o_ref.dtype)

    vec = pl.BlockSpec((1, gw), lambda i: (0, 0))
    return pl.pallas_call(
        body, name=name, grid=(l // tl,),
        in_specs=[pl.BlockSpec((tl, gw), lambda i: (i, 1)), pl.BlockSpec((tl, gw), lambda i: (i, 2)), vec, vec,
                  pl.BlockSpec((SGU_HEADS, SGU_CHUNK, SGU_CHUNK), lambda i: (0, 0, 0)), pl.BlockSpec((SGU_CHUNK, gw), lambda i: (0, 0))],
        out_specs=pl.BlockSpec((tl, gw), lambda i: (i, 0)), out_shape=jax.ShapeDtypeStruct((l, gw), BF16),
        compiler_params=_params(("parallel",)),
    )(proj, proj, norm_g, norm_b, wm, bfull)


def _sgu_bwd(dmixed, proj, norm_g, norm_b, wm, bfull, name):
    l = proj.shape[0]
    tl = 256
    gw = GROUP_WIDTH
    hd = gw // SGU_HEADS

    def body(do_ref, zu_ref, zv_ref, g_ref, b_ref, w_ref, bf_ref, dzu_ref, dzv_ref, dw_ref, dbf_ref, dg_ref, dnb_ref):
        @pl.when(pl.program_id(0) == 0)
        def _():
            dw_ref[...] = jnp.zeros_like(dw_ref)
            dbf_ref[...] = jnp.zeros_like(dbf_ref)
            dg_ref[...] = jnp.zeros_like(dg_ref)
            dnb_ref[...] = jnp.zeros_like(dnb_ref)

        for c in range(tl // SGU_CHUNK):
            rows = slice(c * SGU_CHUNK, (c + 1) * SGU_CHUNK)
            zu, zv, do = zu_ref[rows, :], zv_ref[rows, :], do_ref[rows, :]
            u = _gelu(zu)
            vh, r = _sgu_norm(_gelu(zv), None, None)
            vn = vh * g_ref[...] + b_ref[...]
            dvn_parts, mixed_parts = [], []
            for j in range(gw // 128):
                cols = slice(j * 128, (j + 1) * 128)
                vb = vn[:, cols]
                mixed_parts.append(_sgu_pair(w_ref, vb, j, "nn") + bf_ref[:, cols])
                dm = do[:, cols] * u[:, cols]
                dvn_parts.append(_sgu_pair(w_ref, dm, j, "tn"))
                lo = lax.broadcasted_iota(jnp.int32, dm.shape, 1) < hd
                dw_ref[2 * j] += _mdot(jnp.where(lo, dm, 0.0), vb, "nt")
                dw_ref[2 * j + 1] += _mdot(jnp.where(lo, 0.0, dm), vb, "nt")
                dbf_ref[:, cols] += dm
            mixed = jnp.concatenate(mixed_parts, axis=1)
            dvn = jnp.concatenate(dvn_parts, axis=1)
            dzu_ref[rows, :] = (do * mixed * _gelu_grad(zu)).astype(dzu_ref.dtype)
            dg_ref[...] += jnp.sum(dvn * vh, axis=0, keepdims=True)
            dnb_ref[...] += jnp.sum(dvn, axis=0, keepdims=True)
            dvh = dvn * g_ref[...]
            m1 = jnp.mean(dvh, axis=-1, keepdims=True)
            m2 = jnp.mean(dvh * vh, axis=-1, keepdims=True)
            dv = r * (dvh - m1 - vh * m2)
            dzv_ref[rows, :] = (dv * _gelu_grad(zv)).astype(dzv_ref.dtype)

    vec = pl.BlockSpec((1, gw), lambda i: (0, 0))
    row = pl.BlockSpec((tl, gw), lambda i: (i, 0))
    wspec = pl.BlockSpec((SGU_HEADS, SGU_CHUNK, SGU_CHUNK), lambda i: (0, 0, 0))
    bspec = pl.BlockSpec((SGU_CHUNK, gw), lambda i: (0, 0))
    return pl.pallas_call(
        body, name=name, grid=(l // tl,),
        in_specs=[pl.BlockSpec((tl, gw), lambda i: (i, 1)), pl.BlockSpec((tl, gw), lambda i: (i, 1)), pl.BlockSpec((tl, gw), lambda i: (i, 2)),
                  vec, vec, wspec, bspec],
        out_specs=[row, row, wspec, bspec, vec, vec],
        out_shape=[jax.ShapeDtypeStruct((l, gw), BF16), jax.ShapeDtypeStruct((l, gw), BF16),
                   jax.ShapeDtypeStruct((SGU_HEADS, SGU_CHUNK, SGU_CHUNK), F32), jax.ShapeDtypeStruct((SGU_CHUNK, gw), F32),
                   jax.ShapeDtypeStruct((1, gw), F32), jax.ShapeDtypeStruct((1, gw), F32)],
        compiler_params=_params(("arbitrary",)),
    )(dmixed, proj, proj, norm_g, norm_b, wm, bfull)


HALO = 16


def _window_sums(ext, n_rows, forward):
    def sh(x, k):
        return pltpu.roll(x, (n_rows - k) if forward else k, axis=0)
    s2 = ext + sh(ext, 1)
    s4 = s2 + sh(s2, 2)
    s8 = s4 + sh(s4, 4)
    s16 = s8 + sh(s8, 8)
    return (s2, s4, s8, s16)


def _pool_fwd(proj, pool_w, scale, name):
    l = proj.shape[0]
    tl = 256
    gw = GROUP_WIDTH
    pg = gw // len(POOL_WINDOWS)

    def body(x_ref, halo_ref, w_ref, s_ref, o_ref, p_ref):
        i = pl.program_id(0)
        x = x_ref[...]
        halo = jnp.where(i > 0, halo_ref[...], 0.0)
        ext = jnp.concatenate([halo, x], axis=0)
        sums = _window_sums(ext, tl + HALO, False)
        t = i * tl + lax.broadcasted_iota(jnp.int32, (tl, pg), 0)
        for gi, win in enumerate(POOL_WINDOWS):
            cols = slice(gi * pg, (gi + 1) * pg)
            cnt = jnp.minimum(t + 1, win).astype(F32)
            pooled = sums[gi][HALO:, cols] / cnt - x[:, cols]
            p_ref[:, cols] = pooled
            o_ref[:, cols] = (_mdot(pooled, w_ref[gi]) * s_ref[:, cols]).astype(o_ref.dtype)

    row = pl.BlockSpec((tl, gw), lambda i: (i, 0))
    return pl.pallas_call(
        body, name=name, grid=(l // tl,),
        in_specs=[pl.BlockSpec((tl, gw), lambda i: (i, 3)),
                  pl.BlockSpec((HALO, gw), lambda i: (jnp.maximum(i * (tl // HALO) - 1, 0), 3)),
                  pl.BlockSpec((len(POOL_WINDOWS), pg, pg), lambda i: (0, 0, 0)), pl.BlockSpec((1, gw), lambda i: (0, 0))],
        out_specs=[row, row], out_shape=[jax.ShapeDtypeStruct((l, gw), BF16), jax.ShapeDtypeStruct((l, gw), F32)],
        compiler_params=_params(("parallel",)),
    )(proj, proj, pool_w, scale)


def _pool_bwd_map(dmixed, pooled, pool_w, scale, name):
    l, gw = pooled.shape
    tl = 256
    ng = len(POOL_WINDOWS)
    pg = gw // ng

    def body(do_ref, p_ref, w_ref, s_ref, dp_ref, dw_ref, ds_ref):
        @pl.when(pl.program_id(0) == 0)
        def _():
            dw_ref[...] = jnp.zeros_like(dw_ref)
            ds_ref[...] = jnp.zeros_like(ds_ref)

        for gi in range(ng):
            cols = slice(gi * pg, (gi + 1) * pg)
            do, pooled_g = do_ref[:, cols], p_ref[:, cols]
            mixed = _mdot(pooled_g, w_ref[gi])
            ds_ref[:, cols] += jnp.sum(do * mixed, axis=0, keepdims=True)
            dm = do * s_ref[:, cols]
            dw_ref[gi] += _mdot(pooled_g, dm, "tn")
            dp_ref[:, cols] = _mdot(dm, w_ref[gi], "nt")

    row = pl.BlockSpec((tl, gw), lambda i: (i, 0))
    wspec = pl.BlockSpec((ng, pg, pg), lambda i: (0, 0, 0))
    vec = pl.BlockSpec((1, gw), lambda i: (0, 0))
    return pl.pallas_call(
        body, name=name, grid=(l // tl,),
        in_specs=[pl.BlockSpec((tl, gw), lambda i: (i, 2)), row, wspec, vec], out_specs=[row, wspec, vec],
        out_shape=[jax.ShapeDtypeStruct((l, gw), F32), jax.ShapeDtypeStruct((ng, pg, pg), F32), jax.ShapeDtypeStruct((1, gw), F32)],
        compiler_params=_params(("arbitrary",)),
    )(dmixed, pooled, pool_w, scale)


def _pool_bwd_window(dpooled, name):
    l, gw = dpooled.shape
    tl = 256
    nb = l // tl
    pg = gw // len(POOL_WINDOWS)

    def body(d_ref, halo_ref, o_ref):
        i = pl.program_id(0)
        d = d_ref[...]
        halo = jnp.where(i < nb - 1, halo_ref[...], 0.0)
        ext = jnp.concatenate([d, halo], axis=0)
        t = i * tl + lax.broadcasted_iota(jnp.int32, (tl + HALO, pg), 0)
        for gi, win in enumerate(POOL_WINDOWS):
            cols = slice(gi * pg, (gi + 1) * pg)
            cnt = jnp.minimum(t + 1, win).astype(F32)
            sums = _window_sums(ext[:, cols] / cnt, tl + HALO, True)
            o_ref[:, cols] = (sums[gi][:tl, :] - d[:, cols]).astype(o_ref.dtype)

    row = pl.BlockSpec((tl, gw), lambda i: (i, 0))
    return pl.pallas_call(
        body, name=name, grid=(nb,),
        in_specs=[row, pl.BlockSpec((HALO, gw), lambda i: (jnp.minimum((i + 1) * (tl // HALO), l // HALO - 1), 0))],
        out_specs=row, out_shape=jax.ShapeDtypeStruct((l, gw), BF16), compiler_params=_params(("parallel",)),
    )(dpooled, dpooled)


CONV_HALO = 8
QKV_BLK = 4


def _head_sums(x):
    parts = []
    for hd in range(DN_HEADS):
        s = jnp.sum(x[:, hd * DN_HEAD_DIM:(hd + 1) * DN_HEAD_DIM], axis=-1, keepdims=True)
        parts.append(jnp.broadcast_to(s, (x.shape[0], DN_HEAD_DIM)))
    return jnp.concatenate(parts, axis=1)


def _gdn_pre_fwd(proj, proj_ab, conv_w, a_log, dt_bias, name):
    l = proj.shape[0]
    tl = 256
    gw = GROUP_WIDTH

    def body(xq, xk, xv, hq, hk, hv, w_ref, ab_ref, al_ref, dt_ref, qn_ref, kn_ref, v_ref, cq_ref, ck_ref, cv_ref, gb_ref):
        i = pl.program_id(0)
        for p, (x_ref, h_ref, c_ref) in enumerate(((xq, hq, cq_ref), (xk, hk, ck_ref), (xv, hv, cv_ref))):
            ext = jnp.concatenate([jnp.where(i > 0, h_ref[...], 0.0), x_ref[...]], axis=0)
            conv = jnp.zeros((tl, gw), F32)
            for j in range(DN_CONV):
                k = DN_CONV - 1 - j
                shifted = ext if k == 0 else pltpu.roll(ext, k, axis=0)
                conv = conv + shifted[CONV_HALO:, :] * w_ref[j:j + 1, p * gw:(p + 1) * gw]
            c_ref[...] = conv
            s = _silu(conv)
            if p == 2:
                v_ref[...] = s
            else:
                r = lax.rsqrt(_head_sums(s * s) + L2_EPS)
                (qn_ref if p == 0 else kn_ref)[...] = s * r * (DN_HEAD_DIM ** -0.5 if p == 0 else 1.0)
        ab = ab_ref[...]
        lane = lax.broadcasted_iota(jnp.int32, ab.shape, 1)
        g = -jnp.exp(al_ref[...]) * _softplus(ab + dt_ref[...])
        gb_ref[...] = jnp.where(lane < DN_HEADS, g, _sigmoid(ab))

    def xs(b):
        return pl.BlockSpec((tl, gw), lambda i: (i, b))

    def hs(b):
        return pl.BlockSpec((CONV_HALO, gw), lambda i: (jnp.maximum(i * (tl // CONV_HALO) - 1, 0), b))

    row = pl.BlockSpec((tl, gw), lambda i: (i, 0))
    abrow = pl.BlockSpec((tl, AB_PAD), lambda i: (i, 0))
    abvec = pl.BlockSpec((1, AB_PAD), lambda i: (0, 0))
    return pl.pallas_call(
        body, name=name, grid=(l // tl,),
        in_specs=[xs(QKV_BLK), xs(QKV_BLK + 1), xs(QKV_BLK + 2), hs(QKV_BLK), hs(QKV_BLK + 1), hs(QKV_BLK + 2),
                  pl.BlockSpec((DN_CONV, 3 * gw), lambda i: (0, 0)), abrow, abvec, abvec],
        out_specs=[row] * 6 + [abrow],
        out_shape=[jax.ShapeDtypeStruct((l, gw), F32)] * 6 + [jax.ShapeDtypeStruct((l, AB_PAD), F32)],
        compiler_params=_params(("parallel",)),
    )(proj, proj, proj, proj, proj, proj, conv_w, proj_ab, a_log, dt_bias)


def _gdn_pre_bwd(dq, dk, dv, cq, ck, cv, dgb, gb, proj_ab, a_log, dt_bias, name):
    l, gw = cq.shape
    tl = 256

    def body(dq_ref, dk_ref, dv_ref, cq_ref, ck_ref, cv_ref, dgb_ref, gb_ref, ab_ref, al_ref, dt_ref,
             dcq_ref, dck_ref, dcv_ref, dab_ref, dal_ref, ddt_ref):
        @pl.when(pl.program_id(0) == 0)
        def _():
            dal_ref[...] = jnp.zeros_like(dal_ref)
            ddt_ref[...] = jnp.zeros_like(ddt_ref)

        for p, (d_ref, c_ref, o_ref) in enumerate(((dq_ref, cq_ref, dcq_ref), (dk_ref, ck_ref, dck_ref), (dv_ref, cv_ref, dcv_ref))):
            c, d = c_ref[...], d_ref[...]
            if p == 2:
                ds = d
            else:
                s = _silu(c)
                r = lax.rsqrt(_head_sums(s * s) + L2_EPS)
                ds = (DN_HEAD_DIM ** -0.5 if p == 0 else 1.0) * r * (d - s * r * r * _head_sums(d * s))
            o_ref[...] = ds * _silu_grad(c)
        ab, dgb_v, gb_v = ab_ref[...], dgb_ref[...], gb_ref[...]
        lane = lax.broadcasted_iota(jnp.int32, ab.shape, 1)
        is_g = lane < DN_HEADS
        dpre = dgb_v * (-jnp.exp(al_ref[...])) * _sigmoid(ab + dt_ref[...])
        dab_ref[...] = jnp.where(is_g, dpre, dgb_v * gb_v * (1.0 - gb_v)).astype(dab_ref.dtype)
        dal_ref[...] += jnp.sum(jnp.where(is_g, dgb_v * gb_v, 0.0), axis=0, keepdims=True)
        ddt_ref[...] += jnp.sum(jnp.where(is_g, dpre, 0.0), axis=0, keepdims=True)

    row = pl.BlockSpec((tl, gw), lambda i: (i, 0))
    abrow = pl.BlockSpec((tl, AB_PAD), lambda i: (i, 0))
    abvec = pl.BlockSpec((1, AB_PAD), lambda i: (0, 0))
    return pl.pallas_call(
        body, name=name, grid=(l // tl,),
        in_specs=[row] * 6 + [abrow, abrow, abrow, abvec, abvec],
        out_specs=[row, row, row, abrow, abvec, abvec],
        out_shape=[jax.ShapeDtypeStruct((l, gw), F32)] * 3 + [jax.ShapeDtypeStruct((l, AB_PAD), BF16),
                   jax.ShapeDtypeStruct((1, AB_PAD), F32), jax.ShapeDtypeStruct((1, AB_PAD), F32)],
        compiler_params=_params(("arbitrary",)),
    )(dq, dk, dv, cq, ck, cv, dgb, gb, proj_ab, a_log, dt_bias)


def _conv_bwd(dc, proj, col_blk, w_part, name):
    l, gw = dc.shape
    tl = 256
    nb = l // tl

    def body(dc_ref, halo_ref, x_ref, w_ref, dx_ref, dw_ref):
        i = pl.program_id(0)

        @pl.when(i == 0)
        def _():
            dw_ref[...] = jnp.zeros_like(dw_ref)

        ext = jnp.concatenate([dc_ref[...], jnp.where(i < nb - 1, halo_ref[...], 0.0)], axis=0)
        x = x_ref[...]
        dx = jnp.zeros((tl, gw), F32)
        rid = lax.broadcasted_iota(jnp.int32, (8, gw), 0)
        dw = jnp.zeros((8, gw), F32)
        for j in range(DN_CONV):
            k = DN_CONV - 1 - j
            shifted = (ext if k == 0 else pltpu.roll(ext, tl + CONV_HALO - k, axis=0))[:tl, :]
            dx = dx + shifted * w_ref[j:j + 1, :]
            dw = dw + jnp.where(rid == j, jnp.sum(x * shifted, axis=0, keepdims=True), 0.0)
        dx_ref[...] = dx.astype(dx_ref.dtype)
        dw_ref[...] += dw

    row = pl.BlockSpec((tl, gw), lambda i: (i, 0))
    return pl.pallas_call(
        body, name=name, grid=(nb,),
        in_specs=[row, pl.BlockSpec((CONV_HALO, gw), lambda i: (jnp.minimum((i + 1) * (tl // CONV_HALO), l // CONV_HALO - 1), 0)),
                  pl.BlockSpec((tl, gw), lambda i: (i, col_blk)), pl.BlockSpec((DN_CONV, gw), lambda i: (0, 0))],
        out_specs=[row, pl.BlockSpec((8, gw), lambda i: (0, 0))],
        out_shape=[jax.ShapeDtypeStruct((l, gw), BF16), jax.ShapeDtypeStruct((8, gw), F32)],
        compiler_params=_params(("arbitrary",)),
    )(dc, dc, proj, w_part)


TERMS_CHUNKS = 4


def _bdot(a, b, dims="nn", precision=None):
    cd = {"nn": ((2,), (1,)), "nt": ((2,), (2,)), "tn": ((1,), (1,))}[dims]
    return lax.dot_general(a, b, (cd, ((0,), (0,))), preferred_element_type=F32, precision=precision)


def _bmdot(a, b, dims="nn"):
    return _bdot(_mx(a), _mx(b), dims)


def _wy_terms(q, k, v, gcol, beta, t=None):
    c = DN_CHUNK
    ii = lax.broadcasted_iota(jnp.int32, (1, c, c), 1)
    jj = lax.broadcasted_iota(jnp.int32, (1, c, c), 2)
    tril, strict = ii >= jj, ii > jj
    grow = jnp.sum(jnp.where(ii == jj, gcol, 0.0), axis=1, keepdims=True)
    gc_col = jnp.sum(jnp.where(tril, grow, 0.0), axis=2, keepdims=True)
    gc_row = jnp.sum(jnp.where(ii <= jj, gcol, 0.0), axis=1, keepdims=True)
    dec = jnp.exp(jnp.where(tril, gc_col - gc_row, -1e30))
    kb, vb = k * beta, v * beta
    kk = _bmdot(kb, k, "nt")
    if t is None:
        a = jnp.where(strict, kk * dec, 0.0)
        d = jnp.where((ii >> 3) == (jj >> 3), a, 0.0)
        t = jnp.where(ii == jj, 1.0, 0.0) - d
        p = _bdot(d, d, precision=HI)
        t = t + _bdot(t, p, precision=HI)
        t = t + _bdot(t, _bdot(p, p, precision=HI), precision=HI)
        for sh in (3, 4, 5):
            below = ((ii >> (sh + 1)) == (jj >> (sh + 1))) & ((ii >> sh) > (jj >> sh))
            t = t - _bdot(t, _bdot(jnp.where(below, a, 0.0), t, precision=HI), precision=HI)
    eg = jnp.exp(gc_col)
    gc_last = gc_col[:, c - 1:c, :]
    kbg = kb * eg
    qk0 = _bmdot(q, k, "nt")
    e2 = jnp.exp(gc_last - gc_col)
    return dict(ii=ii, jj=jj, tril=tril, strict=strict, dec=dec, kb=kb, vb=vb, kk=kk, t=t, eg=eg, kbg=kbg,
                qk0=qk0, qk=jnp.where(tril, qk0 * dec, 0.0), qg=q * eg, e2=e2, kt=k * e2, gl=jnp.exp(gc_last))


def _to_heads(x, g):
    return jnp.concatenate([x[:, h * DN_HEAD_DIM:(h + 1) * DN_HEAD_DIM].reshape(g, DN_CHUNK, DN_HEAD_DIM)
                            for h in range(DN_HEADS)], axis=0)


def _from_heads(t, ref, g):
    for h in range(DN_HEADS):
        ref[:, h * DN_HEAD_DIM:(h + 1) * DN_HEAD_DIM] = t[h * g:(h + 1) * g].reshape(g * DN_CHUNK, DN_HEAD_DIM).astype(ref.dtype)


def _head_columns(gbv, first_lane, g):
    lane = lax.broadcasted_iota(jnp.int32, gbv.shape, 1)
    return jnp.concatenate([jnp.sum(jnp.where(lane == first_lane + h, gbv, 0.0), axis=1, keepdims=True).reshape(g, DN_CHUNK, 1)
                            for h in range(DN_HEADS)], axis=0)


def _gdn_terms_fwd(qn, kn, v, gb, name):
    l = qn.shape[0]
    n_chunks = l // DN_CHUNK
    g = min(TERMS_CHUNKS, n_chunks)
    rows, c, nh = g * DN_CHUNK, DN_CHUNK, DN_HEADS

    def body(q_ref, k_ref, v_ref, gb_ref, u_ref, w_ref, qg_ref, kt_ref, qk_ref, t_ref, gl_ref):
        gbv = gb_ref[...]
        x = _wy_terms(_to_heads(q_ref[...], g), _to_heads(k_ref[...], g), _to_heads(v_ref[...], g),
                      _head_columns(gbv, 0, g), _head_columns(gbv, nh, g))
        _from_heads(_bmdot(x["t"], x["vb"]), u_ref, g)
        _from_heads(_bmdot(x["t"], x["kbg"]), w_ref, g)
        _from_heads(x["qg"], qg_ref, g)
        _from_heads(x["kt"], kt_ref, g)
        for h in range(nh):
            qk_ref[:, h] = x["qk"][h * g:(h + 1) * g]
            t_ref[:, h] = x["t"][h * g:(h + 1) * g]
            gl_ref[:, h] = jnp.broadcast_to(x["gl"][h * g:(h + 1) * g], (g, 1, 128))

    row = pl.BlockSpec((rows, GROUP_WIDTH), lambda i: (i, 0))
    sq = pl.BlockSpec((g, nh, c, c), lambda i: (i, 0, 0, 0))
    glb = pl.BlockSpec((g, nh, 1, 128), lambda i: (i, 0, 0, 0))
    return pl.pallas_call(
        body, name=name, grid=(n_chunks // g,), in_specs=[row, row, row, pl.BlockSpec((rows, AB_PAD), lambda i: (i, 0))],
        out_specs=[row] * 4 + [sq, sq, glb],
        out_shape=[jax.ShapeDtypeStruct((l, GROUP_WIDTH), F32)] * 4 + [jax.ShapeDtypeStruct((n_chunks, nh, c, c), F32)] * 2
        + [jax.ShapeDtypeStruct((n_chunks, nh, 1, 128), F32)],
        compiler_params=_params(("parallel",)),
    )(qn, kn, v, gb)


def _rec_specs(n_chunks, reverse):
    c, hd, nh = DN_CHUNK, DN_HEAD_DIM, DN_HEADS
    ch = (lambda n: n_chunks - 1 - n) if reverse else (lambda n: n)
    return (pl.BlockSpec((c, GROUP_WIDTH), lambda n: (ch(n), 0)), pl.BlockSpec((1, nh, c, c), lambda n: (ch(n), 0, 0, 0)),
            pl.BlockSpec((1, nh, 1, 128), lambda n: (ch(n), 0, 0, 0)), pl.BlockSpec((1, nh, hd, hd), lambda n: (ch(n), 0, 0, 0)))


def _gdn_rec_fwd(u, w, qg, kt, qk, gl, name):
    l = u.shape[0]
    n_chunks = l // DN_CHUNK
    hd, nh = DN_HEAD_DIM, DN_HEADS
    blk, sq, glb, st = _rec_specs(n_chunks, False)
    heads = range(nh)

    def body(u_ref, w_ref, qg_ref, kt_ref, qk_ref, gl_ref, o_ref, vn_ref, s_ref, state):
        @pl.when(pl.program_id(0) == 0)
        def _():
            state[...] = jnp.zeros_like(state)

        def cols(h):
            return slice(h * hd, (h + 1) * hd)
        s = [state[h] for h in heads]
        ws = [_mdot(w_ref[:, cols(h)], s[h]) for h in heads]
        vn = [u_ref[:, cols(h)] - ws[h] for h in heads]
        kv = [_mdot(kt_ref[:, cols(h)], vn[h], "tn") for h in heads]
        for h in heads:
            state[h] = s[h] * gl_ref[0, h] + kv[h]
        o1 = [_mdot(qg_ref[:, cols(h)], s[h]) for h in heads]
        o2 = [_mdot(qk_ref[0, h], vn[h]) for h in heads]
        for h in heads:
            s_ref[0, h] = s[h]
            o_ref[:, cols(h)] = o1[h] + o2[h]
            vn_ref[:, cols(h)] = vn[h]

    return pl.pallas_call(
        body, name=name, grid=(n_chunks,), in_specs=[blk, blk, blk, blk, sq, glb], out_specs=[blk, blk, st],
        out_shape=[jax.ShapeDtypeStruct((l, GROUP_WIDTH), F32)] * 2 + [jax.ShapeDtypeStruct((n_chunks, nh, hd, hd), F32)],
        scratch_shapes=[pltpu.VMEM((nh, hd, hd), F32)], compiler_params=_params(("arbitrary",)),
    )(u, w, qg, kt, qk, gl)


def _gdn_rec_bwd(do, w, qg, kt, vn, qk, gl, states, name):
    l = do.shape[0]
    n_chunks = l // DN_CHUNK
    hd, nh, c = DN_HEAD_DIM, DN_HEADS, DN_CHUNK
    blk, sq, glb, st = _rec_specs(n_chunks, True)
    heads = range(nh)

    def body(do_ref, w_ref, qg_ref, kt_ref, vn_ref, qk_ref, gl_ref, s_ref, dvn_ref, dw_ref, dkt_ref, dqg_ref, dqk_ref, dgl_ref, dstate):
        @pl.when(pl.program_id(0) == 0)
        def _():
            dstate[...] = jnp.zeros_like(dstate)

        def cols(h):
            return slice(h * hd, (h + 1) * hd)
        tril = lax.broadcasted_iota(jnp.int32, (c, c), 0) >= lax.broadcasted_iota(jnp.int32, (c, c), 1)
        ds = [dstate[h] for h in heads]
        dout = [do_ref[:, cols(h)] for h in heads]
        a1 = [_mdot(qk_ref[0, h], dout[h], "tn") for h in heads]
        a2 = [_mdot(kt_ref[:, cols(h)], ds[h]) for h in heads]
        dvn = [a1[h] + a2[h] for h in heads]
        b1 = [_mdot(qg_ref[:, cols(h)], dout[h], "tn") for h in heads]
        b2 = [_mdot(w_ref[:, cols(h)], dvn[h], "tn") for h in heads]
        for h in heads:
            dstate[h] = b1[h] + gl_ref[0, h] * ds[h] - b2[h]
        for h in heads:
            s, vnew = s_ref[0, h], vn_ref[:, cols(h)]
            dvn_ref[:, cols(h)] = dvn[h]
            dw_ref[:, cols(h)] = -_mdot(dvn[h], s, "nt")
            dkt_ref[:, cols(h)] = _mdot(vnew, ds[h], "nt")
            dqg_ref[:, cols(h)] = _mdot(dout[h], s, "nt")
            dqk_ref[0, h] = jnp.where(tril, _mdot(dout[h], vnew, "nt"), 0.0)
            dgl = jnp.sum(jnp.sum(ds[h] * s, axis=1, keepdims=True), axis=0, keepdims=True)
            dgl_ref[0, h] = jnp.broadcast_to(dgl, (1, 128))

    return pl.pallas_call(
        body, name=name, grid=(n_chunks,), in_specs=[blk] * 5 + [sq, glb, st], out_specs=[blk] * 4 + [sq, glb],
        out_shape=[jax.ShapeDtypeStruct((l, GROUP_WIDTH), F32)] * 4 + [jax.ShapeDtypeStruct((n_chunks, nh, c, c), F32),
                                                                       jax.ShapeDtypeStruct((n_chunks, nh, 1, 128), F32)],
        scratch_shapes=[pltpu.VMEM((nh, hd, hd), F32)], compiler_params=_params(("arbitrary",)),
    )(do, w, qg, kt, vn, qk, gl, states)


def _gdn_terms_bwd(qn, kn, v, gb, t_inv, dvn, dw, dkt, dqg, dqk, dgl, name):
    l = qn.shape[0]
    n_chunks = l // DN_CHUNK
    g = min(TERMS_CHUNKS, n_chunks)
    rows, c, nh = g * DN_CHUNK, DN_CHUNK, DN_HEADS

    def body(q_ref, k_ref, v_ref, gb_ref, t_ref, dvn_ref, dw_ref, dkt_ref, dqg_ref, dqk_ref, dgl_ref, dq_ref, dk_ref, dv_ref, dgb_ref):
        gbv = gb_ref[...]
        q, k, vv = _to_heads(q_ref[...], g), _to_heads(k_ref[...], g), _to_heads(v_ref[...], g)
        beta = _head_columns(gbv, nh, g)
        t = jnp.concatenate([t_ref[:, h] for h in range(nh)], axis=0)
        x = _wy_terms(q, k, vv, _head_columns(gbv, 0, g), beta, t=t)
        ii, jj, strict = x["ii"], x["jj"], x["strict"]
        du, dwv, dktv, dqgv = (_to_heads(r[...], g) for r in (dvn_ref, dw_ref, dkt_ref, dqg_ref))
        dqkv = jnp.concatenate([dqk_ref[:, h] for h in range(nh)], axis=0)
        dglv = jnp.concatenate([dgl_ref[:, h] for h in range(nh)], axis=0)[:, :, 0:1]
        dt = _bmdot(du, x["vb"], "nt") + _bmdot(dwv, x["kbg"], "nt")
        dvb = _bmdot(t, du, "tn")
        dkbg = _bmdot(t, dwv, "tn")
        tt_dt = _bdot(t, dt, "tn", precision=HI)
        da = jnp.where(strict, -_bdot(tt_dt, t, "nt", precision=HI), 0.0)
        dkk = da * x["dec"]
        dqk0 = dqkv * x["dec"]
        e = (da * x["kk"] + dqkv * x["qk0"]) * x["dec"]
        dkb = _bmdot(dkk, k) + dkbg * x["eg"]
        dk = _bmdot(dkk, x["kb"], "tn") + _bmdot(dqk0, q, "tn") + dktv * x["e2"] + dkb * beta
        dq = _bmdot(dqk0, k) + dqgv * x["eg"]
        s_kt = jnp.sum(dktv * x["kt"], axis=2, keepdims=True)
        dgc_c = (jnp.sum(e, axis=2, keepdims=True) + jnp.sum(dqgv * x["qg"], axis=2, keepdims=True) - s_kt
                 + jnp.sum(dkbg * x["kbg"], axis=2, keepdims=True))
        dgc_last = jnp.sum(s_kt, axis=1, keepdims=True) + dglv * x["gl"]
        rid = lax.broadcasted_iota(jnp.int32, (1, c, 1), 1)
        dgc_c = dgc_c + jnp.where(rid == c - 1, dgc_last, 0.0)
        dgc_r = jnp.sum(jnp.where(ii == jj, dgc_c, 0.0), axis=1, keepdims=True) - jnp.sum(e, axis=1, keepdims=True)
        dg = jnp.sum(jnp.where(jj >= ii, dgc_r, 0.0), axis=2, keepdims=True)
        dbeta = jnp.sum(dkb * k, axis=2, keepdims=True) + jnp.sum(dvb * vv, axis=2, keepdims=True)
        _from_heads(dq, dq_ref, g)
        _from_heads(dk, dk_ref, g)
        _from_heads(dvb * beta, dv_ref, g)
        lane = lax.broadcasted_iota(jnp.int32, gbv.shape, 1)
        dgb = jnp.zeros(gbv.shape, F32)
        for h in range(nh):
            dgb = dgb + jnp.where(lane == h, dg[h * g:(h + 1) * g].reshape(rows, 1), 0.0)
            dgb = dgb + jnp.where(lane == nh + h, dbeta[h * g:(h + 1) * g].reshape(rows, 1), 0.0)
        dgb_ref[...] = dgb

    row = pl.BlockSpec((rows, GROUP_WIDTH), lambda i: (i, 0))
    abrow = pl.BlockSpec((rows, AB_PAD), lambda i: (i, 0))
    sq = pl.BlockSpec((g, nh, c, c), lambda i: (i, 0, 0, 0))
    glb = pl.BlockSpec((g, nh, 1, 128), lambda i: (i, 0, 0, 0))
    return pl.pallas_call(
        body, name=name, grid=(n_chunks // g,), in_specs=[row, row, row, abrow, sq, row, row, row, row, sq, glb],
        out_specs=[row, row, row, abrow],
        out_shape=[jax.ShapeDtypeStruct((l, GROUP_WIDTH), F32)] * 3 + [jax.ShapeDtypeStruct((l, AB_PAD), F32)],
        compiler_params=_params(("parallel",)),
    )(qn, kn, v, gb, t_inv, dvn, dw, dkt, dqg, dqk, dgl)


def _gdn_post_fwd(o, proj, norm_g4, name):
    l, gw = o.shape
    tl = min(512, l)

    def body(o_ref, gate_ref, g_ref, out_ref):
        ov = o_ref[...]
        r = lax.rsqrt(_head_sums(ov * ov) * (1.0 / DN_HEAD_DIM) + RMS_EPS)
        out_ref[...] = (ov * r * g_ref[...] * _silu(gate_ref[...])).astype(out_ref.dtype)

    row = pl.BlockSpec((tl, gw), lambda i: (i, 0))
    return pl.pallas_call(
        body, name=name, grid=(l // tl,),
        in_specs=[row, pl.BlockSpec((tl, gw), lambda i: (i, 7)), pl.BlockSpec((1, gw), lambda i: (0, 0))],
        out_specs=row, out_shape=jax.ShapeDtypeStruct((l, gw), BF16), compiler_params=_params(("parallel",)),
    )(o, proj, norm_g4)


def _gdn_post_bwd(dmixed, o, proj, norm_g4, name):
    l, gw = o.shape
    tl = min(512, l)

    def body(d_ref, o_ref, gate_ref, g_ref, do_ref, dgate_ref, dng_ref):
        @pl.when(pl.program_id(0) == 0)
        def _():
            dng_ref[...] = jnp.zeros_like(dng_ref)

        ov, gate, d = o_ref[...], gate_ref[...], d_ref[...]
        r = lax.rsqrt(_head_sums(ov * ov) * (1.0 / DN_HEAD_DIM) + RMS_EPS)
        oh = ov * r
        sg = _silu(gate)
        dgate_ref[...] = (d * oh * g_ref[...] * _silu_grad(gate)).astype(dgate_ref.dtype)
        dng_ref[...] += jnp.sum(d * sg * oh, axis=0, keepdims=True)
        doh = d * g_ref[...] * sg
        do_ref[...] = r * (doh - oh * _head_sums(doh * oh) * (1.0 / DN_HEAD_DIM))

    row = pl.BlockSpec((tl, gw), lambda i: (i, 0))
    vec = pl.BlockSpec((1, gw), lambda i: (0, 0))
    return pl.pallas_call(
        body, name=name, grid=(l // tl,),
        in_specs=[pl.BlockSpec((tl, gw), lambda i: (i, 3)), row, pl.BlockSpec((tl, gw), lambda i: (i, 7)), vec],
        out_specs=[row, row, vec],
        out_shape=[jax.ShapeDtypeStruct((l, gw), F32), jax.ShapeDtypeStruct((l, gw), BF16), jax.ShapeDtypeStruct((1, gw), F32)],
        compiler_params=_params(("arbitrary",)),
    )(dmixed, o, proj, norm_g4)


def _run(hosts, name, fn):
    h = hosts.get(name)
    if h is None:
        return fn(None)
    res, outs = fn(h[0]())
    h[1](outs)
    return res


def _layer_fwd(x, xm, w, li, hosts):
    l = x.shape[0]
    nm = f"l{li}_"
    proj = _run(hosts, nm + "proj", lambda ops: _matmul(
        xm, w["w_main"], mode="nn", tm=1024, tn=1024, tk=2048,out_dtype=F32, name=nm + "proj", comm=ops))
    proj_ab = _matmul(xm, w["w_ab"], mode="nn", tm=1024, tn=AB_PAD, tk=2048, out_dtype=F32, name=nm + "proj_ab")
    hs, y = _s5_fwd(proj, w["s5_b"], w["s5_c"], w["s5_lam"], w["s5_d"], nm + "s5")
    m_s5 = _s5_glu_fwd(y, w["s5_glu_w"], w["s5_glu_b"], nm + "s5_glu")
    m_sgu = _sgu_fwd(proj, w["sgu_norm_g"], w["sgu_norm_b"], w["sgu_wm"], w["sgu_bfull"], nm + "sgu")
    m_pool, pooled = _pool_fwd(proj, w["pool_w"], w["pool_scale"], nm + "pool")
    qn, kn, v, cq, ck, cv, gb = _gdn_pre_fwd(proj, proj_ab, w["dn_conv_w"], w["dn_a_log"], w["dn_dt_bias"], nm + "gdn_pre")
    u, wy, qg, kt, qk, t_inv, gl = _gdn_terms_fwd(qn, kn, v, gb, nm + "gdn_terms")
    o, vn, states = _gdn_rec_fwd(u, wy, qg, kt, qk, gl, nm + "gdn_rec")
    m_dn = _gdn_post_fwd(o, proj, w["dn_norm_g4"], nm + "gdn_post")
    mixed = jnp.concatenate([m_s5, m_sgu, m_pool, m_dn], axis=1)
    y1 = _matmul(mixed, w["w_out"], mode="nn", tm=1024, tn=1024, tk=2048, out_dtype=F32, name=nm + "out_proj")
    h1, x1, x1m = _ln_fwd(x, y1, w["ln1_g"], w["ln1_b"], nm + "ln1")
    r = _run(hosts, nm + "up", lambda ops: _matmul(
        x1m, w["w_up"], mode="nn", tm=1024, tn=1024, tk=2048,out_dtype=BF16, name=nm + "up",
        epi=lambda acc: jnp.maximum(acc, 0.0), b_slab=w["w_up"].shape[2], comm=ops))
    y2 = _run(hosts, nm + "down", lambda ops: _matmul(
        r, w["w_down"], mode="nn", tm=1024, tn=1024, tk=2048,out_dtype=F32, name=nm + "down", a_fn=lambda a: a * a, comm=ops))
    h2, x2, x2m = _ln_fwd(x1, y2, w["ln2_g"], w["ln2_b"], nm + "ln2")
    saved = dict(xm=xm, proj=proj, proj_ab=proj_ab, hs=hs, y=y, pooled=pooled, qn=qn, kn=kn, v=v, cq=cq, ck=ck, cv=cv, gb=gb,
                 wy=wy, qg=qg, kt=kt, qk=qk, t_inv=t_inv, gl=gl, vn=vn, o=o, states=states, mixed=mixed, h1=h1, x1m=x1m,
                 r=r, h2=h2)
    return x2, x2m, saved


def _layer_bwd(dx2, s, w, small, li, hosts, g):
    nm = f"l{li}b_"
    l = dx2.shape[0]
    gw = GROUP_WIDTH
    wire = MXU_DTYPE
    dh2, dh2m, g["ln2_g"], g["ln2_b"] = _ln_bwd(dx2, s["h2"], w["ln2_g"], nm + "ln2")
    g["w_down"] = _run(hosts, nm + "dw_down", lambda ops: _matmul(
        s["r"], dh2m, mode="tn", tm=1024, tn=1024, tk=2048,out_dtype=wire, name=nm + "dw_down", a_fn=lambda a: a * a,
        comm=ops)).reshape(N_DEV, D_FF // N_DEV, D_MODEL)
    dpre = _run(hosts, nm + "dpre", lambda ops: _matmul(
        dh2m, w["w_down"], mode="nt", tm=1024, tn=1024, tk=2048,out_dtype=BF16, name=nm + "dpre",
        extras=[(s["r"], (None, None), lambda i, j: (i, j))], epi=lambda acc, r: acc * 2.0 * r.astype(F32), comm=ops))
    g["w_up"] = _matmul(s["x1m"], dpre, mode="tn", tm=1024, tn=1024, tk=2048,out_dtype=wire, name=nm + "dw_up",
                        out_slab=D_FF // N_DEV)
    dx1 = _run(hosts, nm + "dx1", lambda ops: _matmul(
        dpre, w["w_up"], mode="nt", tm=1024, tn=1024, tk=2048,out_dtype=F32, name=nm + "dx1",
        extras=[(dh2, (None, None), lambda i, j: (i, j))], epi=lambda acc, e: acc + ALPHA * e,
        b_slab=w["w_up"].shape[2], comm=ops))
    dh1, dh1m, g["ln1_g"], g["ln1_b"] = _ln_bwd(dx1, s["h1"], w["ln1_g"], nm + "ln1")
    g["w_out"] = _matmul(s["mixed"], dh1m, mode="tn", tm=1024, tn=1024, tk=2048,out_dtype=wire,
                         name=nm + "dw_out").reshape(N_DEV, D_MODEL // N_DEV, D_MODEL)
    dmixed = _run(hosts, nm + "dmixed", lambda ops: _matmul(
        dh1m, w["w_out"], mode="nt", tm=1024, tn=1024, tk=2048,out_dtype=F32, name=nm + "dmixed", comm=ops))
    proj, proj_ab = s["proj"], s["proj_ab"]
    dy, dz, yg, g["s5_glu_b"], g["s5_d"] = _s5_glu_bwd(dmixed, s["y"], proj, w["s5_glu_w"], w["s5_glu_b"], nm + "s5_glu")
    g["s5_glu_w"] = _matmul(yg, dz, mode="tn", tm=gw, tn=gw, tk=1024, out_dtype=wire,
                            name=nm + "dw_glu").reshape(N_DEV, gw // N_DEV, gw)
    du_s5, g["s5_b"], g["s5_c"], g["s5_lam"] = _s5_bwd(dy, s["hs"], proj, w["s5_b"], w["s5_c"], w["s5_lam_conj"], w["s5_d"], nm + "s5")
    dzu, dzv, g["sgu_w"], g["sgu_bfull"], g["sgu_norm_g"], g["sgu_norm_b"] = _sgu_bwd(
        dmixed, proj, w["sgu_norm_g"], w["sgu_norm_b"], w["sgu_wm"], w["sgu_bfull"], nm + "sgu")
    dpooled, g["pool_w"], g["pool_scale"] = _pool_bwd_map(dmixed, s["pooled"], w["pool_w"], w["pool_scale"], nm + "pool_map")
    dp = _pool_bwd_window(dpooled, nm + "pool_win")
    do, dgate, g["dn_norm_g4"] = _gdn_post_bwd(dmixed, s["o"], proj, w["dn_norm_g4"], nm + "gdn_post")
    dvn, dwy, dkt, dqg, dqk, dgl = _gdn_rec_bwd(do, s["wy"], s["qg"], s["kt"], s["vn"], s["qk"], s["gl"], s["states"], nm + "gdn_rec")
    dq, dk, dv, dgb = _gdn_terms_bwd(s["qn"], s["kn"], s["v"], s["gb"], s["t_inv"], dvn, dwy, dkt, dqg, dqk, dgl, nm + "gdn_terms")
    dcq, dck, dcv, dab, g["dn_a_log"], g["dn_dt_bias"] = _gdn_pre_bwd(
        dq, dk, dv, s["cq"], s["ck"], s["cv"], dgb, s["gb"], proj_ab, w["dn_a_log"], w["dn_dt_bias"], nm + "gdn_pre")
    dxs, dws = [], []
    for p, dc in enumerate((dcq, dck, dcv)):
        dxp, dwp = _conv_bwd(dc, proj, QKV_BLK + p, w["dn_conv_w"][:, p * gw:(p + 1) * gw], nm + f"conv{p}")
        dxs.append(dxp)
        dws.append(dwp)
    dconv = jnp.concatenate(dws, axis=1)
    g["dn_conv_w"] = jnp.transpose(dconv.reshape(dconv.shape[0], N_DEV, 3 * gw // N_DEV), (1, 0, 2))
    dproj = jnp.concatenate([du_s5, dzu, dzv, dp] + dxs + [dgate], axis=1)
    xm = s["xm"]
    g["small"] = _unprep_grads(g, small)
    dw_main = _run(hosts, nm + "dw_main", lambda ops: _matmul(
        xm, dproj, mode="tn", tm=1024, tn=1024, tk=2048,out_dtype=wire, name=nm + "dw_main", comm=ops))
    dw_ab = _matmul(xm, dab, mode="tn", tm=1024, tn=AB_PAD, tk=1024, out_dtype=wire, name=nm + "dw_ab")
    dw_in = jnp.concatenate([dw_main, dw_ab[:, :2 * DN_HEADS]], axis=1)
    g["w_in"] = jnp.transpose(dw_in.reshape(D_MODEL, N_DEV, dw_in.shape[1] // N_DEV), (1, 0, 2))
    dx_ab = _matmul(dab, w["w_ab"], mode="nt", tm=1024, tn=1024, tk=AB_PAD, out_dtype=F32, name=nm + "dx_ab",
                    extras=[(dh1, (None, None), lambda i, j: (i, j))], epi=lambda acc, e: acc + ALPHA * e)
    return _run(hosts, nm + "dx", lambda ops: _matmul(
        dproj, w["w_main"], mode="nt", tm=1024, tn=1024, tk=2048,out_dtype=F32, name=nm + "dx",
        extras=[(dx_ab, (None, None), lambda i, j: (i, j))], epi=lambda acc, e: acc + e, comm=ops))


SMALL = ("s5_lambda_re", "s5_lambda_im", "s5_log_step", "s5_b_re", "s5_b_im", "s5_c_re", "s5_c_im", "s5_d", "s5_glu_b",
         "sgu_norm_g", "sgu_norm_b", "sgu_w", "sgu_b", "pool_w", "pool_scale", "dn_a_log", "dn_dt_bias", "dn_norm_g",
         "ln1_g", "ln1_b", "ln2_g", "ln2_b")
SHARDED = ("w_in", "s5_glu_w", "dn_conv_w", "w_out", "w_up", "w_down")


def _pad_lanes(v, width=AB_PAD):
    return jnp.pad(v.reshape(1, -1), ((0, 0), (0, width - v.size)))


def _prep_small(p):
    mx = MXU_DTYPE
    lbr, lbi, bbr, bbi = _s5_discretize(p["s5_lambda_re"], p["s5_lambda_im"], p["s5_log_step"], p["s5_b_re"], p["s5_b_im"])
    b_compact, c_compact = _s5_compact(bbr, bbi, p["s5_c_re"], p["s5_c_im"])
    causal = jnp.tril(jnp.ones((SGU_CHUNK, SGU_CHUNK), F32))
    return dict(
        s5_b=b_compact.astype(mx), s5_c=c_compact.astype(mx),
        s5_lam=jnp.concatenate([lbr.reshape(1, -1), lbi.reshape(1, -1)], axis=1),
        s5_lam_conj=jnp.concatenate([lbr.reshape(1, -1), -lbi.reshape(1, -1)], axis=1),
        s5_d=p["s5_d"].reshape(1, -1), s5_glu_b=p["s5_glu_b"].reshape(1, -1),
        sgu_norm_g=p["sgu_norm_g"].reshape(1, -1), sgu_norm_b=p["sgu_norm_b"].reshape(1, -1),
        sgu_wm=(p["sgu_w"] * causal).astype(mx), sgu_bfull=jnp.repeat(p["sgu_b"].T, GROUP_WIDTH // SGU_HEADS, axis=1),
        pool_w=p["pool_w"].astype(mx), pool_scale=p["pool_scale"].reshape(1, -1),
        dn_a_log=_pad_lanes(p["dn_a_log"]), dn_dt_bias=_pad_lanes(p["dn_dt_bias"]),
        dn_norm_g4=jnp.tile(p["dn_norm_g"].reshape(1, -1), (1, DN_HEADS)),
        ln1_g=p["ln1_g"].reshape(1, -1), ln1_b=p["ln1_b"].reshape(1, -1),
        ln2_g=p["ln2_g"].reshape(1, -1), ln2_b=p["ln2_b"].reshape(1, -1),
    )


def _weight_views(name, t):
    if name == "w_in":
        w_in = jnp.transpose(t, (1, 0, 2)).reshape(t.shape[1], N_DEV * t.shape[2])
        pad = AB_PAD - (w_in.shape[1] - MAIN_COLS)
        return dict(w_main=w_in[:, :MAIN_COLS], w_ab=jnp.pad(w_in[:, MAIN_COLS:], ((0, 0), (0, pad))))
    if name == "dn_conv_w":
        return dict(dn_conv_w=jnp.transpose(t, (1, 0, 2)).reshape(t.shape[1], N_DEV * t.shape[2]))
    if name == "w_up":
        return dict(w_up=t)
    return {name: t.reshape(N_DEV * t.shape[1], t.shape[2])}


def _unprep_grads(g, p):
    causal = jnp.tril(jnp.ones((SGU_CHUNK, SGU_CHUNK), F32))
    dbbr, dbbi = _s5_uncompact_b(g["s5_b"])
    dc_re, dc_im = _s5_uncompact_c(g["s5_c"])
    dlbr, dlbi = g["s5_lam"][0, :S5_NS].reshape(S5_GROUPS, S5_STATE), g["s5_lam"][0, S5_NS:].reshape(S5_GROUPS, S5_STATE)
    _, vjp = jax.vjp(_s5_discretize, p["s5_lambda_re"], p["s5_lambda_im"], p["s5_log_step"], p["s5_b_re"], p["s5_b_im"])
    d_lre, d_lim, d_step, d_bre, d_bim = vjp((dlbr, dlbi, dbbr, dbbi))
    hd = GROUP_WIDTH // SGU_HEADS
    return dict(
        s5_lambda_re=d_lre, s5_lambda_im=d_lim, s5_log_step=d_step, s5_b_re=d_bre, s5_b_im=d_bim, s5_c_re=dc_re, s5_c_im=dc_im,
        s5_d=g["s5_d"].reshape(S5_GROUPS, S5_CH), s5_glu_b=g["s5_glu_b"].reshape(-1),
        sgu_norm_g=g["sgu_norm_g"].reshape(-1), sgu_norm_b=g["sgu_norm_b"].reshape(-1), sgu_w=g["sgu_w"] * causal,
        sgu_b=jnp.sum(g["sgu_bfull"].reshape(SGU_CHUNK, SGU_HEADS, hd), axis=2).T,
        pool_w=g["pool_w"], pool_scale=g["pool_scale"].reshape(-1),
        dn_a_log=g["dn_a_log"][0, :DN_HEADS], dn_dt_bias=g["dn_dt_bias"][0, :DN_HEADS],
        dn_norm_g=jnp.sum(g["dn_norm_g4"].reshape(DN_HEADS, DN_HEAD_DIM), axis=0),
        ln1_g=g["ln1_g"].reshape(-1), ln1_b=g["ln1_b"].reshape(-1), ln2_g=g["ln2_g"].reshape(-1), ln2_b=g["ln2_b"].reshape(-1),
    )


def _local_step(x, target, ops, small, fwd_hosts, bwd_hosts, grads):
    saved = []
    h, hm = x, x.astype(MXU_DTYPE)
    for i in range(DEPTH):
        h, hm, s = _layer_fwd(h, hm, ops[i], i, fwd_hosts)
        saved.append(s)
    loss, dh = _loss_head(h, target)
    for i in reversed(range(DEPTH)):
        dh = _layer_bwd(dh, saved[i], ops[i], small[i], i, bwd_hosts, grads[i])
    return loss, dh


def _adamw(w, gparts, m, v, name):
    rr, c = w.shape
    ng = len(gparts)
    r = rr // ng
    lanes = -(-c // 128) * 128
    tr = r
    while tr * lanes * 4 * N_DEV > (4 << 20) and tr % 16 == 0:
        tr //= 2
    nb = r // tr

    def body(w_ref, *rest):
        g_refs, (m_ref, v_ref, go_ref, d_ref, mo_ref, vo_ref) = rest[:ng], rest[ng:]
        layer = pl.program_id(0)
        g = jnp.zeros(m_ref.shape, F32)
        for li in range(ng):
            gl = g_refs[li][0].astype(F32)
            for s in range(1, N_DEV):
                gl = gl + g_refs[li][s].astype(F32)
            g = jnp.where(layer == li, gl, g)
        mn = ADAM_B1 * m_ref[...] + (1.0 - ADAM_B1) * g
        vn = ADAM_B2 * v_ref[...] + (1.0 - ADAM_B2) * g * g
        m_hat = mn / (1.0 - ADAM_B1 ** ADAM_STEP)
        v_hat = vn / (1.0 - ADAM_B2 ** ADAM_STEP)
        go_ref[...] = g
        d_ref[...] = -ADAM_LR * (m_hat / (jnp.sqrt(v_hat) + ADAM_EPS) + ADAM_WD * w_ref[...])
        mo_ref[...] = mn
        vo_ref[...] = vn

    row = pl.BlockSpec((tr, c), lambda li, i: (li * nb + i, 0))
    part_specs = [pl.BlockSpec((N_DEV, tr, c), functools.partial(lambda li, i, k: (0, jnp.where(li == k, i, 0), 0), k=k))
                  for k in range(ng)]
    return pl.pallas_call(
        body, name=name, grid=(ng, nb), in_specs=[row] + part_specs + [row, row],
        out_specs=[row] * 4, out_shape=[jax.ShapeDtypeStruct((rr, c), F32)] * 4, compiler_params=_params(("arbitrary", "arbitrary")),
    )(w, *gparts, m, v)


PACK_LANES = 128
PACK_ROWS = 4096


PACK_TILE = 8 * PACK_LANES


def _pack_rows(t):
    return -(-t.size // PACK_TILE) * 8


def _pack(vals):
    rows = []
    for t in vals:
        flat = t.reshape(-1)
        n_rows = _pack_rows(t)
        rows.append(jnp.pad(flat, (0, n_rows * PACK_LANES - flat.size)).reshape(n_rows, PACK_LANES))
    used = sum(r.shape[0] for r in rows)
    assert used <= PACK_ROWS, used
    return jnp.concatenate(rows + [jnp.zeros((PACK_ROWS - used, PACK_LANES), F32)], axis=0)


def _unpack(packed, like):
    out, off = [], 0
    for t in like:
        n_rows = _pack_rows(t)
        out.append(packed[off:off + n_rows].reshape(-1)[:t.size].reshape(t.shape))
        off += n_rows
    return out


def kernel(x, w_in, s5_lambda_re, s5_lambda_im, s5_log_step, s5_b_re, s5_b_im, s5_c_re, s5_c_im, s5_d, s5_glu_w, s5_glu_b, sgu_norm_g, sgu_norm_b, sgu_w, sgu_b, pool_w, pool_scale, dn_conv_w, dn_a_log, dn_dt_bias, dn_norm_g, w_out, ln1_g, ln1_b, w_up, w_down, ln2_g, ln2_b, loss_target, m_w_in, m_s5_lambda_re, m_s5_lambda_im, m_s5_log_step, m_s5_b_re, m_s5_b_im, m_s5_c_re, m_s5_c_im, m_s5_d, m_s5_glu_w, m_s5_glu_b, m_sgu_norm_g, m_sgu_norm_b, m_sgu_w, m_sgu_b, m_pool_w, m_pool_scale, m_dn_conv_w, m_dn_a_log, m_dn_dt_bias, m_dn_norm_g, m_w_out, m_ln1_g, m_ln1_b, m_w_up, m_w_down, m_ln2_g, m_ln2_b, v_w_in, v_s5_lambda_re, v_s5_lambda_im, v_s5_log_step, v_s5_b_re, v_s5_b_im, v_s5_c_re, v_s5_c_im, v_s5_d, v_s5_glu_w, v_s5_glu_b, v_sgu_norm_g, v_sgu_norm_b, v_sgu_w, v_sgu_b, v_pool_w, v_pool_scale, v_dn_conv_w, v_dn_a_log, v_dn_dt_bias, v_dn_norm_g, v_w_out, v_ln1_g, v_ln1_b, v_w_up, v_w_down, v_ln2_g, v_ln2_b):
    names = ("w_in", "s5_lambda_re", "s5_lambda_im", "s5_log_step", "s5_b_re", "s5_b_im", "s5_c_re", "s5_c_im", "s5_d", "s5_glu_w",
             "s5_glu_b", "sgu_norm_g", "sgu_norm_b", "sgu_w", "sgu_b", "pool_w", "pool_scale", "dn_conv_w", "dn_a_log", "dn_dt_bias",
             "dn_norm_g", "w_out", "ln1_g", "ln1_b", "w_up", "w_down", "ln2_g", "ln2_b")
    env = locals()
    w = {n: env[n] for n in names}
    m = {n: env["m_" + n] for n in names}
    v = {n: env["v_" + n] for n in names}

    wire = [{n: (w[n][i] if n == "dn_conv_w" else w[n][i].astype(MXU_DTYPE)) for n in SHARDED} for i in range(DEPTH)]
    small = [{n: w[n][i] for n in SMALL} for i in range(DEPTH)]
    ops = [_prep_small(small[i]) for i in range(DEPTH)]
    grads = [{} for _ in range(DEPTH)]
    recv = [{} for _ in range(DEPTH)]
    first = ("w_in", "s5_glu_w", "dn_conv_w", "w_out")

    def gather(layer, group):
        def take(outs):
            for n, t in zip(group, outs):
                ops[layer].update(_weight_views(n, t))
        return (lambda: [(wire[layer][n], False) for n in group]), take

    def scatter(layer, group, with_small=False):
        def make():
            sends = [(grads[layer][n], True) for n in group]
            if with_small:
                sends.append((_pack([grads[layer]["small"][n] for n in SMALL]), False))
            return sends
        def take(outs):
            recv[layer].update(dict(zip(group + (("small",) if with_small else ()), outs)))
        return make, take

    make, take = gather(0, first)
    take(_exchange(make(), "gather_first"))
    fwd_hosts = {"l0_proj": gather(0, ("w_up",)), "l0_up": gather(0, ("w_down",)), "l0_down": gather(1, first),
                 "l1_proj": gather(1, ("w_up",)), "l1_up": gather(1, ("w_down",))}
    late = ("w_in", "s5_glu_w", "dn_conv_w")
    bwd_hosts = {"l1b_dpre": scatter(1, ("w_down",)), "l1b_dx1": scatter(1, ("w_up",)), "l1b_dmixed": scatter(1, ("w_out",)),
                 "l0b_dw_down": scatter(1, late, with_small=True),
                 "l0b_dpre": scatter(0, ("w_down",)), "l0b_dx1": scatter(0, ("w_up",)), "l0b_dmixed": scatter(0, ("w_out",)),
                 "l0b_dw_main": scatter(0, ("s5_glu_w", "dn_conv_w"), with_small=True), "l0b_dx": scatter(0, ("w_in",))}
    loss, grad_x = _local_step(x[0], loss_target[0], ops, small, fwd_hosts, bwd_hosts, grads)

    g_out, d_out, m_out, v_out = {}, {}, {}, {}
    for n in SHARDED:
        shp = w[n].shape
        pad = (-shp[1]) % 8
        def rows(t):
            return jnp.pad(t, ((0, 0), (0, pad), (0, 0))).reshape(shp[0] * (shp[1] + pad), shp[2])
        res = _adamw(rows(w[n]), [recv[i][n] for i in range(DEPTH)], rows(m[n]), rows(v[n]), "adamw_" + n)
        g_out[n], d_out[n], m_out[n], v_out[n] = (t.reshape(shp[0], shp[1] + pad, shp[2])[:, :shp[1]] for t in res)
    def packed(src):
        return jnp.concatenate([_pack([src[n][i] for n in SMALL]) for i in range(DEPTH)], axis=0)
    res = _adamw(packed(w), [recv[i]["small"] for i in range(DEPTH)], packed(m), packed(v), "adamw_small")
    like = [w[n][0] for n in SMALL]
    for dst, pk in zip((g_out, d_out, m_out, v_out), res):
        per_layer = [_unpack(pk[i * PACK_ROWS:(i + 1) * PACK_ROWS], like) for i in range(DEPTH)]
        dst.update({n: jnp.stack([per_layer[i][k] for i in range(DEPTH)]) for k, n in enumerate(SMALL)})

    total = lax.psum(loss[0, 0], MESH_AXES)
    return (total, grad_x[None], *[g_out[n] for n in names], *[d_out[n] for n in names],
            *[m_out[n] for n in names], *[v_out[n] for n in names])
```

```python
import functools
import math

import jax
import jax.numpy as jnp
from jax import lax
from jax.experimental import pallas as pl
from jax.experimental.pallas import tpu as pltpu

F32 = jnp.float32
BF16 = jnp.bfloat16
MXU_DTYPE = jnp.bfloat16
HI = lax.Precision.HIGHEST

N_DEV = 8
D_MODEL = 2048
DEPTH = 2
GROUP_WIDTH = 512
S5_GROUPS, S5_CH, S5_STATE = 32, 16, 64
S5_NS = S5_GROUPS * S5_STATE
SGU_CHUNK, SGU_HEADS = 128, 8
POOL_WINDOWS = (2, 4, 8, 16)
DN_HEADS, DN_HEAD_DIM, DN_CONV, DN_CHUNK = 4, 128, 4, 64
D_FF = 4 * D_MODEL
LN_EPS, RMS_EPS, L2_EPS = 1e-5, 1e-6, 1e-6
ALPHA = (2 * DEPTH) ** 0.25
MAIN_COLS = 4096
AB_PAD = 128
ADAM_LR, ADAM_B1, ADAM_B2, ADAM_EPS, ADAM_WD, ADAM_STEP = 0.001, 0.9, 0.999, 1e-08, 0.01, 10
VMEM_LIMIT = 56 * 1024 * 1024
C_GELU = math.sqrt(2.0 / math.pi)


def _params(sem=None):
    return pltpu.CompilerParams(dimension_semantics=sem, vmem_limit_bytes=VMEM_LIMIT)


def _gelu(x):
    return 0.5 * x * (1.0 + jnp.tanh(C_GELU * (x + 0.044715 * x * x * x)))


def _gelu_grad(x):
    t = jnp.tanh(C_GELU * (x + 0.044715 * x * x * x))
    return 0.5 * (1.0 + t) + 0.5 * x * (1.0 - t * t) * C_GELU * (1.0 + 3.0 * 0.044715 * x * x)


def _sigmoid(x):
    return 1.0 / (1.0 + jnp.exp(-x))


def _silu(x):
    return x * _sigmoid(x)


def _silu_grad(x):
    s = _sigmoid(x)
    return s * (1.0 + x * (1.0 - s))


def _softplus(x):
    z = jnp.exp(-jnp.abs(x))
    small = z * (1.0 - z * (0.5 - z * (1.0 / 3.0)))
    return jnp.maximum(x, 0.0) + jnp.where(z < 1e-2, small, jnp.log(1.0 + z))


def _mx(x):
    return x.astype(MXU_DTYPE)


def _dot(a, b, dims="nn", precision=None):
    cd = {"nn": ((1,), (0,)), "nt": ((1,), (1,)), "tn": ((0,), (0,))}[dims]
    return lax.dot_general(a, b, (cd, ((), ())), preferred_element_type=F32, precision=precision)


def _mdot(a, b, dims="nn"):
    return _dot(_mx(a), _mx(b), dims)


MESH_AXES = ("x", "y", "c")
OFFSETS = [(dx, dy, dc) for dx in (0, 1) for dy in (0, 1) for dc in (0, 1)][1:]


def _me_and_peers():
    x, y, c = (lax.axis_index(a) for a in MESH_AXES)
    def flip(v, d):
        return 1 - v if d else v
    peers = [(flip(x, dx), flip(y, dy), flip(c, dc)) for dx, dy, dc in OFFSETS]
    def idx(p):
        return 4 * p[0] + 2 * p[1] + p[2]
    return idx((x, y, c)), peers, [idx(p) for p in peers]


SIBLING = OFFSETS.index((0, 0, 1))
SAME_CORE = [OFFSETS.index(f) for f in ((0, 1, 0), (1, 0, 0), (1, 1, 0))]


class _Comm:
    def __init__(self, ops):
        self.arrays = [a for a, _ in ops]
        self.scatter = [s for _, s in ops]
        self.n = n = len(ops)
        hbm = pl.BlockSpec(memory_space=pltpu.HBM)
        self.in_specs, self.out_specs = [hbm] * n, [hbm] * n
        self.out_shape = [jax.ShapeDtypeStruct(a.shape if s else (N_DEV,) + a.shape, a.dtype) for a, s in ops]
        npeer = len(OFFSETS)
        self.scratch = [pltpu.SemaphoreType.DMA((n, npeer)), pltpu.SemaphoreType.DMA((n, npeer)), pltpu.SemaphoreType.DMA((n,))]

    def _plan(self, ins, outs, sems, waiting):
        send_sems, recv_sems, local_sems = sems
        me, peers, peer_idx = _me_and_peers()

        def remote(k, d, src, dst, to):
            return pltpu.make_async_remote_copy(src_ref=src, dst_ref=dst, send_sem=send_sems.at[k, d], recv_sem=recv_sems.at[k, d],
                                                device_id=to, device_id_type=pl.DeviceIdType.MESH)
        plan = []
        for k in range(self.n):
            every = range(len(OFFSETS))
            if self.scatter[k]:
                local = pltpu.make_async_copy(ins[k].at[me], outs[k].at[me], local_sems.at[k])
                pushes = [remote(k, d, ins[k].at[peer_idx[d]], outs[k].at[me], peers[d]) for d in every]
                onward = []
            else:
                local = pltpu.make_async_copy(ins[k], outs[k].at[me], local_sems.at[k])
                pushes = [remote(k, d, ins[k], outs[k].at[me], peers[d]) for d in [SIBLING] + SAME_CORE]
                onward = SAME_CORE
            passed, arrivals = [], {}
            if waiting:
                passed = [(d, remote(k, d + 1, outs[k].at[peer_idx[d]], outs[k].at[peer_idx[d]], peers[SIBLING])) for d in onward]
                arrivals = {d: remote(k, d, outs[k].at[peer_idx[d]], outs[k].at[peer_idx[d]], peers[d]) for d in every}
            plan.append((local, pushes, passed, arrivals))
        return plan

    def start(self, ins, outs, sems):
        for local, pushes, _, _ in self._plan(ins, outs, sems, False):
            local.start()
            for cp in pushes:
                cp.start()

    def wait(self, ins, outs, sems):
        plan = self._plan(ins, outs, sems, True)
        for _, _, passed, arrivals in plan:
            for d, onward in passed:
                arrivals.pop(d).wait_recv()
                onward.start()
        for local, pushes, passed, arrivals in plan:
            for cp in arrivals.values():
                cp.wait_recv()
            for cp in pushes + [onward for _, onward in passed]:
                cp.wait_send()
            local.wait()


def _exchange(ops, name):
    cm = _Comm(ops)

    def body(*refs):
        ins, outs, sems = refs[:cm.n], refs[cm.n:2 * cm.n], refs[2 * cm.n:]
        cm.start(ins, outs, sems)
        cm.wait(ins, outs, sems)

    return pl.pallas_call(body, name=name, in_specs=cm.in_specs, out_specs=cm.out_specs, out_shape=cm.out_shape,
                          scratch_shapes=cm.scratch)(*cm.arrays)


def _matmul(a, b, *, mode, tm, tn, tk, out_dtype, name, a_fn=None, extras=(), epi=None, a_cols=None,
            b_slab=None, out_slab=None, comm=None):
    a_shape = a.shape if a_cols is None else (a.shape[0], a_cols)
    b_shape = b.shape if b_slab is None else (b.shape[1], N_DEV * b_slab)
    if mode == "nn":
        (m, k), n = a_shape, b_shape[1]
    elif mode == "nt":
        (m, k), n = a_shape, b_shape[0]
    else:
        (k, m), n = a_shape, b_shape[1]
    tm, tn, tk = min(tm, m), min(tn, n), min(tk, k)
    if b_slab is not None:
        tn, tk = (tn, min(tk, b_slab)) if mode == "nt" else (min(tn, b_slab), tk)
    assert m % tm == 0 and n % tn == 0 and k % tk == 0, (name, a.shape, b.shape, tm, tn, tk)
    gi, gj, nk = m // tm, n // tn, k // tk
    n_ex = len(extras)
    cm = _Comm(comm) if comm else None
    nc = cm.n if cm else 0

    def body(a_ref, b_ref, *rest):
        ex_refs, rest = rest[:n_ex], rest[n_ex:]
        c_ins, o_ref, c_outs, acc, sems = rest[:nc], rest[nc], rest[nc + 1:2 * nc + 1], rest[2 * nc + 1], rest[2 * nc + 2:]
        i, j, kk = pl.program_id(0), pl.program_id(1), pl.program_id(2)
        if cm:
            @pl.when((i == 0) & (j == 0) & (kk == 0))
            def _():
                cm.start(c_ins, c_outs, sems)

        av = a_ref[...]
        if a_fn is not None:
            av = a_fn(av)
        part = _dot(_mx(av), _mx(b_ref[...]), mode)

        def finish(r):
            if epi is not None:
                r = epi(r, *[e[...] for e in ex_refs])
            o_ref[...] = r.astype(out_dtype)

        if nk == 1:
            finish(part)
        else:
            @pl.when(kk == 0)
            def _():
                acc[...] = part

            @pl.when((kk > 0) & (kk < nk - 1))
            def _():
                acc[...] += part

            @pl.when(kk == nk - 1)
            def _():
                finish(acc[...] + part)

        if cm:
            @pl.when((i == gi - 1) & (j == gj - 1) & (kk == nk - 1))
            def _():
                cm.wait(c_ins, c_outs, sems)

    a_spec = pl.BlockSpec((tk, tm), lambda i, j, kk: (kk, i)) if mode == "tn" else pl.BlockSpec((tm, tk), lambda i, j, kk: (i, kk))
    if b_slab is None:
        b_spec = pl.BlockSpec((tn, tk), lambda i, j, kk: (j, kk)) if mode == "nt" else pl.BlockSpec((tk, tn), lambda i, j, kk: (kk, j))
    elif mode == "nt":
        assert b_slab % tk == 0
        b_spec = pl.BlockSpec((None, tn, tk), lambda i, j, kk: ((kk * tk) // b_slab, j, ((kk * tk) % b_slab) // tk))
    else:
        assert b_slab % tn == 0
        b_spec = pl.BlockSpec((None, tk, tn), lambda i, j, kk: ((j * tn) // b_slab, kk, ((j * tn) % b_slab) // tn))
    if out_slab is None:
        o_spec, o_shape = pl.BlockSpec((tm, tn), lambda i, j, kk: (i, j)), jax.ShapeDtypeStruct((m, n), out_dtype)
    else:
        assert out_slab % tn == 0 and n == N_DEV * out_slab
        o_spec = pl.BlockSpec((None, tm, tn), lambda i, j, kk: ((j * tn) // out_slab, i, ((j * tn) % out_slab) // tn))
        o_shape = jax.ShapeDtypeStruct((N_DEV, m, out_slab), out_dtype)
    ex_specs = [pl.BlockSpec(({None: tm, "tn": tn}.get(bs[0], bs[0]), tn if bs[1] is None else bs[1]),
                             functools.partial(lambda i, j, kk, f: f(i, j), f=im)) for (_, bs, im) in extras]
    res = pl.pallas_call(
        body,
        name=name,
        grid=(gi, gj, nk),
        in_specs=[a_spec, b_spec, *ex_specs] + (cm.in_specs if cm else []),
        out_specs=[o_spec] + (cm.out_specs if cm else []),
        out_shape=[o_shape] + (cm.out_shape if cm else []),
        scratch_shapes=[pltpu.VMEM((tm, tn) if nk > 1 else (8, 128), F32)] + (cm.scratch if cm else []),
        compiler_params=_params(("arbitrary",) * 3 if cm else ("parallel", "parallel", "arbitrary")),
    )(a, b, *[e[0] for e in extras], *(cm.arrays if cm else []))
    return (res[0], res[1:]) if cm else res[0]


def _ln_fwd(x, y, g, b, name):
    l, d = x.shape
    tl = 256

    def body(x_ref, y_ref, g_ref, b_ref, h_ref, o_ref, om_ref):
        h = ALPHA * x_ref[...] + y_ref[...]
        mu = jnp.mean(h, axis=-1, keepdims=True)
        c = h - mu
        var = jnp.mean(c * c, axis=-1, keepdims=True)
        h_ref[...] = h
        out = c * lax.rsqrt(var + LN_EPS) * g_ref[...] + b_ref[...]
        o_ref[...] = out
        om_ref[...] = out.astype(om_ref.dtype)

    row = pl.BlockSpec((tl, d), lambda i: (i, 0))
    vec = pl.BlockSpec((1, d), lambda i: (0, 0))
    return pl.pallas_call(
        body, name=name, grid=(l // tl,), in_specs=[row, row, vec, vec], out_specs=[row, row, row],
        out_shape=[jax.ShapeDtypeStruct((l, d), F32)] * 2 + [jax.ShapeDtypeStruct((l, d), MXU_DTYPE)],
        compiler_params=_params(("parallel",)),
    )(x, y, g, b)


def _ln_bwd(dout, h, g, name):
    l, d = h.shape
    tl = 256

    def body(do_ref, h_ref, g_ref, dh_ref, dhm_ref, dg_ref, db_ref):
        @pl.when(pl.program_id(0) == 0)
        def _():
            dg_ref[...] = jnp.zeros_like(dg_ref)
            db_ref[...] = jnp.zeros_like(db_ref)

        hv, do = h_ref[...], do_ref[...]
        mu = jnp.mean(hv, axis=-1, keepdims=True)
        c = hv - mu
        r = lax.rsqrt(jnp.mean(c * c, axis=-1, keepdims=True) + LN_EPS)
        xh = c * r
        dxh = do * g_ref[...]
        m1 = jnp.mean(dxh, axis=-1, keepdims=True)
        m2 = jnp.mean(dxh * xh, axis=-1, keepdims=True)
        dh = r * (dxh - m1 - xh * m2)
        dh_ref[...] = dh
        dhm_ref[...] = dh.astype(dhm_ref.dtype)
        dg_ref[...] += jnp.sum(do * xh, axis=0, keepdims=True)
        db_ref[...] += jnp.sum(do, axis=0, keepdims=True)

    row = pl.BlockSpec((tl, d), lambda i: (i, 0))
    vec = pl.BlockSpec((1, d), lambda i: (0, 0))
    return pl.pallas_call(
        body, name=name, grid=(l // tl,), in_specs=[row, row, vec], out_specs=[row, row, vec, vec],
        out_shape=[jax.ShapeDtypeStruct((l, d), F32), jax.ShapeDtypeStruct((l, d), MXU_DTYPE),
                   jax.ShapeDtypeStruct((1, d), F32), jax.ShapeDtypeStruct((1, d), F32)],
        compiler_params=_params(("arbitrary",)),
    )(dout, h, g)


def _loss_head(y, target):
    l, d = y.shape
    tl = 256

    def body(y_ref, t_ref, loss_ref, dy_ref):
        @pl.when(pl.program_id(0) == 0)
        def _():
            loss_ref[...] = jnp.zeros_like(loss_ref)

        e = y_ref[...] - t_ref[...]
        dy_ref[...] = e * (1.0 / d)
        s = jnp.sum(jnp.sum(e * e, axis=1, keepdims=True), axis=0, keepdims=True)
        loss_ref[...] += s * (0.5 / d)

    row = pl.BlockSpec((tl, d), lambda i: (i, 0))
    return pl.pallas_call(
        body, name="loss_head", grid=(l // tl,), in_specs=[row, row],
        out_specs=[pl.BlockSpec((1, 1), lambda i: (0, 0)), row],
        out_shape=[jax.ShapeDtypeStruct((1, 1), F32), jax.ShapeDtypeStruct((l, d), F32)],
        compiler_params=_params(("arbitrary",)),
    )(y, target)


def _s5_discretize(lam_re, lam_im, log_step, b_re, b_im):
    step = jnp.exp(log_step)[:, None]
    e = jnp.exp(lam_re * step)
    lbr, lbi = e * jnp.cos(lam_im * step), e * jnp.sin(lam_im * step)
    den = lam_re * lam_re + lam_im * lam_im
    qr = ((lbr - 1.0) * lam_re + lbi * lam_im) / den
    qi = (lbi * lam_re - (lbr - 1.0) * lam_im) / den
    bbr = qr[:, :, None] * b_re - qi[:, :, None] * b_im
    bbi = qr[:, :, None] * b_im + qi[:, :, None] * b_re
    return lbr, lbi, bbr, bbi


S5_TILES, S5_SLABS = 4, 8
S5_TILE_W, S5_SLAB_W = GROUP_WIDTH // S5_TILES, S5_NS // S5_TILES
S5_GPT = S5_GROUPS // S5_TILES


def _s5_compact(bbr, bbi, c_re, c_im):
    eye = jnp.eye(S5_GPT, dtype=F32)
    def bd(t):
        return jnp.einsum("tgph,gk->tghkp", t.reshape(S5_TILES, S5_GPT, S5_STATE, S5_CH), eye).reshape(S5_TILES, S5_TILE_W, S5_SLAB_W)
    def cd(t):
        return jnp.einsum("tghp,gk->tgpkh", t.reshape(S5_TILES, S5_GPT, S5_CH, S5_STATE), eye).reshape(S5_TILES, S5_SLAB_W, S5_TILE_W)
    return jnp.concatenate([bd(bbr), bd(bbi)], axis=0), jnp.concatenate([cd(c_re), -cd(c_im)], axis=0)


def _s5_uncompact_b(db):
    eye = jnp.eye(S5_GPT, dtype=F32)[None, :, None, :, None]
    def ex(t):
        d = jnp.sum(t.reshape(S5_TILES, S5_GPT, S5_CH, S5_GPT, S5_STATE) * eye, axis=3)
        return jnp.transpose(d, (0, 1, 3, 2)).reshape(S5_GROUPS, S5_STATE, S5_CH)
    return ex(db[:S5_TILES]), ex(db[S5_TILES:])


def _s5_uncompact_c(dc):
    eye = jnp.eye(S5_GPT, dtype=F32)[None, :, None, :, None]
    def ex(t):
        d = jnp.sum(t.reshape(S5_TILES, S5_GPT, S5_STATE, S5_GPT, S5_CH) * eye, axis=3)
        return jnp.transpose(d, (0, 1, 3, 2)).reshape(S5_GROUPS, S5_CH, S5_STATE)
    return ex(dc[:S5_TILES]), -ex(dc[S5_TILES:])


S5_ROWS = 256


def _s5_tile(j):
    t = j % S5_TILES
    return slice(t * S5_TILE_W, (t + 1) * S5_TILE_W)


def _s5_slab(j):
    return slice(j * S5_SLAB_W, (j + 1) * S5_SLAB_W)


def _s5_recur(src, lam_ref, carry, emit, n_rows, reverse, extra=()):
    ns = S5_NS
    lr, li = lam_ref[:, :ns], lam_ref[:, ns:]

    def step(t, c):
        row = (n_rows - 1 - t) if reverse else t
        cr, ci = c[0], c[1]
        nr = lr * cr - li * ci + src[pl.ds(row, 1), :ns]
        ni = lr * ci + li * cr + src[pl.ds(row, 1), ns:]
        return (nr, ni) + tuple(emit(row, nr, ni, cr, ci, c[2:]))

    fin = lax.fori_loop(0, n_rows, step, (carry[:, :ns], carry[:, ns:]) + tuple(extra))
    carry[:, :ns] = fin[0]
    carry[:, ns:] = fin[1]
    return fin[2:]


def _s5_fwd(proj, b, c, lam, d, name):
    l = proj.shape[0]
    tl = min(S5_ROWS, l)
    w = 2 * S5_NS

    def body(u_ref, b_ref, c_ref, lam_ref, d_ref, hs_ref, y_ref, bu, carry):
        @pl.when(pl.program_id(0) == 0)
        def _():
            carry[...] = jnp.zeros_like(carry)

        u = u_ref[...]
        um = _mx(u)
        for j in range(S5_SLABS):
            bu[:, _s5_slab(j)] = _dot(um[:, _s5_tile(j)], b_ref[j])

        def emit(row, nr, ni, cr, ci, extra):
            hs_ref[pl.ds(row, 1), :S5_NS] = nr
            hs_ref[pl.ds(row, 1), S5_NS:] = ni
            return extra

        _s5_recur(bu, lam_ref, carry, emit, tl, False)
        for t in range(S5_TILES):
            acc = _dot(_mx(hs_ref[:, _s5_slab(t)]), c_ref[t]) + _dot(_mx(hs_ref[:, _s5_slab(S5_TILES + t)]), c_ref[S5_TILES + t])
            y_ref[:, _s5_tile(t)] = acc + d_ref[:, _s5_tile(t)] * u[:, _s5_tile(t)]

    row = lambda width: pl.BlockSpec((tl, width), lambda i: (i, 0))
    full = lambda a: pl.BlockSpec(a.shape, lambda i: (0,) * a.ndim)
    return pl.pallas_call(
        body, name=name, grid=(l // tl,), in_specs=[row(GROUP_WIDTH), full(b), full(c), full(lam), full(d)],
        out_specs=[row(w), row(GROUP_WIDTH)],
        out_shape=[jax.ShapeDtypeStruct((l, w), F32), jax.ShapeDtypeStruct((l, GROUP_WIDTH), F32)],
        scratch_shapes=[pltpu.VMEM((tl, w), F32), pltpu.VMEM((1, w), F32)], compiler_params=_params(("arbitrary",)),
    )(proj, b, c, lam, d)


def _s5_bwd(dy, hs, proj, b, c, lam_conj, d, name):
    l = dy.shape[0]
    tl = min(S5_ROWS, l)
    nb = l // tl
    w = 2 * S5_NS

    def body(dy_ref, hs_ref, u_ref, b_ref, c_ref, lam_ref, d_ref, du_ref, db_ref, dc_ref, dl_ref, dh, adj, carry):
        @pl.when(pl.program_id(0) == 0)
        def _():
            carry[...] = jnp.zeros_like(carry)
            db_ref[...] = jnp.zeros_like(db_ref)
            dc_ref[...] = jnp.zeros_like(dc_ref)
            dl_ref[...] = jnp.zeros_like(dl_ref)

        dyv = dy_ref[...]
        dym, um = _mx(dyv), _mx(u_ref[...])
        for j in range(S5_SLABS):
            dh[:, _s5_slab(j)] = _dot(dym[:, _s5_tile(j)], c_ref[j], "nt")

        def emit(row, nr, ni, cr, ci, extra):
            adj[pl.ds(row, 1), :S5_NS] = nr
            adj[pl.ds(row, 1), S5_NS:] = ni
            hr, hi = hs_ref[pl.ds(row, 1), :S5_NS], hs_ref[pl.ds(row, 1), S5_NS:]
            return extra[0] + cr * hr + ci * hi, extra[1] + ci * hr - cr * hi

        dl = _s5_recur(dh, lam_ref, carry, emit, tl, True, extra=(dl_ref[:, :S5_NS], dl_ref[:, S5_NS:]))
        dl_ref[:, :S5_NS] = dl[0]
        dl_ref[:, S5_NS:] = dl[1]
        for t in range(S5_TILES):
            acc = (_dot(_mx(adj[:, _s5_slab(t)]), b_ref[t], "nt")
                   + _dot(_mx(adj[:, _s5_slab(S5_TILES + t)]), b_ref[S5_TILES + t], "nt"))
            du_ref[:, _s5_tile(t)] = (acc + d_ref[:, _s5_tile(t)] * dyv[:, _s5_tile(t)]).astype(du_ref.dtype)
        for j in range(S5_SLABS):
            dc_ref[j] += _dot(_mx(hs_ref[:, _s5_slab(j)]), dym[:, _s5_tile(j)], "tn")
            db_ref[j] += _dot(um[:, _s5_tile(j)], _mx(adj[:, _s5_slab(j)]), "tn")

    row = lambda width: pl.BlockSpec((tl, width), lambda i: (nb - 1 - i, 0))
    full = lambda a: pl.BlockSpec(a.shape, lambda i: (0,) * a.ndim)
    acc3 = lambda shape: pl.BlockSpec(shape, lambda i: (0, 0, 0))
    return pl.pallas_call(
        body, name=name, grid=(nb,),
        in_specs=[row(GROUP_WIDTH), row(w), row(GROUP_WIDTH), full(b), full(c), full(lam_conj), full(d)],
        out_specs=[row(GROUP_WIDTH), acc3(b.shape), acc3(c.shape), pl.BlockSpec((1, w), lambda i: (0, 0))],
        out_shape=[jax.ShapeDtypeStruct((l, GROUP_WIDTH), BF16), jax.ShapeDtypeStruct(b.shape, F32),
                   jax.ShapeDtypeStruct(c.shape, F32), jax.ShapeDtypeStruct((1, w), F32)],
        scratch_shapes=[pltpu.VMEM((tl, w), F32), pltpu.VMEM((tl, w), F32), pltpu.VMEM((1, w), F32)],
        compiler_params=_params(("arbitrary",)),
    )(dy, hs, proj, b, c, lam_conj, d)


def _s5_glu_fwd(y, glu_w, glu_b, name):
    l, d = y.shape
    tl = min(512, l)

    def body(y_ref, w_ref, b_ref, o_ref):
        yg = _gelu(y_ref[...])
        z = _mdot(yg, w_ref[...]) + b_ref[...]
        o_ref[...] = (yg * _sigmoid(z)).astype(o_ref.dtype)

    return pl.pallas_call(
        body, name=name, grid=(l // tl,),
        in_specs=[pl.BlockSpec((tl, d), lambda i: (i, 0)), pl.BlockSpec((d, d), lambda i: (0, 0)), pl.BlockSpec((1, d), lambda i: (0, 0))],
        out_specs=pl.BlockSpec((tl, d), lambda i: (i, 0)), out_shape=jax.ShapeDtypeStruct((l, d), BF16),
        compiler_params=_params(("parallel",)),
    )(y, glu_w, glu_b)


def _s5_glu_bwd(dmixed, y, proj, glu_w, glu_b, name):
    l, d = y.shape
    tl = min(512, l)

    def body(do_ref, y_ref, u_ref, w_ref, b_ref, dy_ref, dz_ref, yg_ref, db_ref, dd_ref):
        @pl.when(pl.program_id(0) == 0)
        def _():
            db_ref[...] = jnp.zeros_like(db_ref)
            dd_ref[...] = jnp.zeros_like(dd_ref)

        yv, do = y_ref[...], do_ref[...]
        yg = _gelu(yv)
        gate = _sigmoid(_mdot(yg, w_ref[...]) + b_ref[...])
        dz = do * yg * gate * (1.0 - gate)
        dyg = do * gate + _mdot(dz, w_ref[...], "nt")
        dy = dyg * _gelu_grad(yv)
        dy_ref[...] = dy
        dz_ref[...] = dz.astype(dz_ref.dtype)
        yg_ref[...] = yg.astype(yg_ref.dtype)
        db_ref[...] += jnp.sum(dz, axis=0, keepdims=True)
        dd_ref[...] += jnp.sum(dy * u_ref[...], axis=0, keepdims=True)

    row = pl.BlockSpec((tl, d), lambda i: (i, 0))
    vec = pl.BlockSpec((1, d), lambda i: (0, 0))
    return pl.pallas_call(
        body, name=name, grid=(l // tl,),
        in_specs=[row, row, row, pl.BlockSpec((d, d), lambda i: (0, 0)), vec],
        out_specs=[row, row, row, vec, vec],
        out_shape=[jax.ShapeDtypeStruct((l, d), F32), jax.ShapeDtypeStruct((l, d), BF16), jax.ShapeDtypeStruct((l, d), BF16),
                   jax.ShapeDtypeStruct((1, d), F32), jax.ShapeDtypeStruct((1, d), F32)],
        compiler_params=_params(("arbitrary",)),
    )(dmixed, y, proj, glu_w, glu_b)


def _sgu_pair(w_ref, x, j, dims):
    lo = lax.broadcasted_iota(jnp.int32, x.shape, 1) < (GROUP_WIDTH // SGU_HEADS)
    xb = _mx(x)
    r0 = _dot(w_ref[2 * j], xb, dims)
    r1 = _dot(w_ref[2 * j + 1], xb, dims)
    return jnp.where(lo, r0, r1)


def _sgu_norm(v, g, b):
    mu = jnp.mean(v, axis=-1, keepdims=True)
    c = v - mu
    r = lax.rsqrt(jnp.mean(c * c, axis=-1, keepdims=True) + LN_EPS)
    return c * r, r


def _sgu_fwd(proj, norm_g, norm_b, wm, bfull, name):
    l = proj.shape[0]
    tl = 256
    gw = GROUP_WIDTH

    def body(zu_ref, zv_ref, g_ref, b_ref, w_ref, bf_ref, o_ref):
        for c in range(tl // SGU_CHUNK):
            rows = slice(c * SGU_CHUNK, (c + 1) * SGU_CHUNK)
            u = _gelu(zu_ref[rows, :])
            vh, _ = _sgu_norm(_gelu(zv_ref[rows, :]), None, None)
            vn = vh * g_ref[...] + b_ref[...]
            for j in range(gw // 128):
                cols = slice(j * 128, (j + 1) * 128)
                mixed = _sgu_pair(w_ref, vn[:, cols], j, "nn") + bf_ref[:, cols]
                o_ref[rows, cols] = (u[:, cols] * mixed).astype(o_ref.dtype)

    vec = pl.BlockSpec((1, gw), lambda i: (0, 0))
    return pl.pallas_call(
        body, name=name, grid=(l // tl,),
        in_specs=[pl.BlockSpec((tl, gw), lambda i: (i, 1)), pl.BlockSpec((tl, gw), lambda i: (i, 2)), vec, vec,
                  pl.BlockSpec((SGU_HEADS, SGU_CHUNK, SGU_CHUNK), lambda i: (0, 0, 0)), pl.BlockSpec((SGU_CHUNK, gw), lambda i: (0, 0))],
        out_specs=pl.BlockSpec((tl, gw), lambda i: (i, 0)), out_shape=jax.ShapeDtypeStruct((l, gw), BF16),
        compiler_params=_params(("parallel",)),
    )(proj, proj, norm_g, norm_b, wm, bfull)


def _sgu_bwd(dmixed, proj, norm_g, norm_b, wm, bfull, name):
    l = proj.shape[0]
    tl = 256
    gw = GROUP_WIDTH
    hd = gw // SGU_HEADS

    def body(do_ref, zu_ref, zv_ref, g_ref, b_ref, w_ref, bf_ref, dzu_ref, dzv_ref, dw_ref, dbf_ref, dg_ref, dnb_ref):
        @pl.when(pl.program_id(0) == 0)
        def _():
            dw_ref[...] = jnp.zeros_like(dw_ref)
            dbf_ref[...] = jnp.zeros_like(dbf_ref)
            dg_ref[...] = jnp.zeros_like(dg_ref)
            dnb_ref[...] = jnp.zeros_like(dnb_ref)

        for c in range(tl // SGU_CHUNK):
            rows = slice(c * SGU_CHUNK, (c + 1) * SGU_CHUNK)
            zu, zv, do = zu_ref[rows, :], zv_ref[rows, :], do_ref[rows, :]
            u = _gelu(zu)
            vh, r = _sgu_norm(_gelu(zv), None, None)
            vn = vh * g_ref[...] + b_ref[...]
            dvn_parts, mixed_parts = [], []
            for j in range(gw // 128):
                cols = slice(j * 128, (j + 1) * 128)
                vb = vn[:, cols]
                mixed_parts.append(_sgu_pair(w_ref, vb, j, "nn") + bf_ref[:, cols])
                dm = do[:, cols] * u[:, cols]
                dvn_parts.append(_sgu_pair(w_ref, dm, j, "tn"))
                lo = lax.broadcasted_iota(jnp.int32, dm.shape, 1) < hd
                dw_ref[2 * j] += _mdot(jnp.where(lo, dm, 0.0), vb, "nt")
                dw_ref[2 * j + 1] += _mdot(jnp.where(lo, 0.0, dm), vb, "nt")
                dbf_ref[:, cols] += dm
            mixed = jnp.concatenate(mixed_parts, axis=1)
            dvn = jnp.concatenate(dvn_parts, axis=1)
            dzu_ref[rows, :] = (do * mixed * _gelu_grad(zu)).astype(dzu_ref.dtype)
            dg_ref[...] += jnp.sum(dvn * vh, axis=0, keepdims=True)
            dnb_ref[...] += jnp.sum(dvn, axis=0, keepdims=True)
            dvh = dvn * g_ref[...]
            m1 = jnp.mean(dvh, axis=-1, keepdims=True)
            m2 = jnp.mean(dvh * vh, axis=-1, keepdims=True)
            dv = r * (dvh - m1 - vh * m2)
            dzv_ref[rows, :] = (dv * _gelu_grad(zv)).astype(dzv_ref.dtype)

    vec = pl.BlockSpec((1, gw), lambda i: (0, 0))
    row = pl.BlockSpec((tl, gw), lambda i: (i, 0))
    wspec = pl.BlockSpec((SGU_HEADS, SGU_CHUNK, SGU_CHUNK), lambda i: (0, 0, 0))
    bspec = pl.BlockSpec((SGU_CHUNK, gw), lambda i: (0, 0))
    return pl.pallas_call(
        body, name=name, grid=(l // tl,),
        in_specs=[pl.BlockSpec((tl, gw), lambda i: (i, 1)), pl.BlockSpec((tl, gw), lambda i: (i, 1)), pl.BlockSpec((tl, gw), lambda i: (i, 2)),
                  vec, vec, wspec, bspec],
        out_specs=[row, row, wspec, bspec, vec, vec],
        out_shape=[jax.ShapeDtypeStruct((l, gw), BF16), jax.ShapeDtypeStruct((l, gw), BF16),
                   jax.ShapeDtypeStruct((SGU_HEADS, SGU_CHUNK, SGU_CHUNK), F32), jax.ShapeDtypeStruct((SGU_CHUNK, gw), F32),
                   jax.ShapeDtypeStruct((1, gw), F32), jax.ShapeDtypeStruct((1, gw), F32)],
        compiler_params=_params(("arbitrary",)),
    )(dmixed, proj, proj, norm_g, norm_b, wm, bfull)


HALO = 16


def _window_sums(ext, n_rows, forward):
    def sh(x, k):
        return pltpu.roll(x, (n_rows - k) if forward else k, axis=0)
    s2 = ext + sh(ext, 1)
    s4 = s2 + sh(s2, 2)
    s8 = s4 + sh(s4, 4)
    s16 = s8 + sh(s8, 8)
    return (s2, s4, s8, s16)


def _pool_fwd(proj, pool_w, scale, name):
    l = proj.shape[0]
    tl = 256
    gw = GROUP_WIDTH
    pg = gw // len(POOL_WINDOWS)

    def body(x_ref, halo_ref, w_ref, s_ref, o_ref, p_ref):
        i = pl.program_id(0)
        x = x_ref[...]
        halo = jnp.where(i > 0, halo_ref[...], 0.0)
        ext = jnp.concatenate([halo, x], axis=0)
        sums = _window_sums(ext, tl + HALO, False)
        t = i * tl + lax.broadcasted_iota(jnp.int32, (tl, pg), 0)
        for gi, win in enumerate(POOL_WINDOWS):
            cols = slice(gi * pg, (gi + 1) * pg)
            cnt = jnp.minimum(t + 1, win).astype(F32)
            pooled = sums[gi][HALO:, cols] / cnt - x[:, cols]
            p_ref[:, cols] = pooled
            o_ref[:, cols] = (_mdot(pooled, w_ref[gi]) * s_ref[:, cols]).astype(o_ref.dtype)

    row = pl.BlockSpec((tl, gw), lambda i: (i, 0))
    return pl.pallas_call(
        body, name=name, grid=(l // tl,),
        in_specs=[pl.BlockSpec((tl, gw), lambda i: (i, 3)),
                  pl.BlockSpec((HALO, gw), lambda i: (jnp.maximum(i * (tl // HALO) - 1, 0), 3)),
                  pl.BlockSpec((len(POOL_WINDOWS), pg, pg), lambda i: (0, 0, 0)), pl.BlockSpec((1, gw), lambda i: (0, 0))],
        out_specs=[row, row], out_shape=[jax.ShapeDtypeStruct((l, gw), BF16), jax.ShapeDtypeStruct((l, gw), F32)],
        compiler_params=_params(("parallel",)),
    )(proj, proj, pool_w, scale)


def _pool_bwd_map(dmixed, pooled, pool_w, scale, name):
    l, gw = pooled.shape
    tl = 256
    ng = len(POOL_WINDOWS)
    pg = gw // ng

    def body(do_ref, p_ref, w_ref, s_ref, dp_ref, dw_ref, ds_ref):
        @pl.when(pl.program_id(0) == 0)
        def _():
            dw_ref[...] = jnp.zeros_like(dw_ref)
            ds_ref[...] = jnp.zeros_like(ds_ref)

        for gi in range(ng):
            cols = slice(gi * pg, (gi + 1) * pg)
            do, pooled_g = do_ref[:, cols], p_ref[:, cols]
            mixed = _mdot(pooled_g, w_ref[gi])
            ds_ref[:, cols] += jnp.sum(do * mixed, axis=0, keepdims=True)
            dm = do * s_ref[:, cols]
            dw_ref[gi] += _mdot(pooled_g, dm, "tn")
            dp_ref[:, cols] = _mdot(dm, w_ref[gi], "nt")

    row = pl.BlockSpec((tl, gw), lambda i: (i, 0))
    wspec = pl.BlockSpec((ng, pg, pg), lambda i: (0, 0, 0))
    vec = pl.BlockSpec((1, gw), lambda i: (0, 0))
    return pl.pallas_call(
        body, name=name, grid=(l // tl,),
        in_specs=[pl.BlockSpec((tl, gw), lambda i: (i, 2)), row, wspec, vec], out_specs=[row, wspec, vec],
        out_shape=[jax.ShapeDtypeStruct((l, gw), F32), jax.ShapeDtypeStruct((ng, pg, pg), F32), jax.ShapeDtypeStruct((1, gw), F32)],
        compiler_params=_params(("arbitrary",)),
    )(dmixed, pooled, pool_w, scale)


def _pool_bwd_window(dpooled, name):
    l, gw = dpooled.shape
    tl = 256
    nb = l // tl
    pg = gw // len(POOL_WINDOWS)

    def body(d_ref, halo_ref, o_ref):
        i = pl.program_id(0)
        d = d_ref[...]
        halo = jnp.where(i < nb - 1, halo_ref[...], 0.0)
        ext = jnp.concatenate([d, halo], axis=0)
        t = i * tl + lax.broadcasted_iota(jnp.int32, (tl + HALO, pg), 0)
        for gi, win in enumerate(POOL_WINDOWS):
            cols = slice(gi * pg, (gi + 1) * pg)
            cnt = jnp.minimum(t + 1, win).astype(F32)
            sums = _window_sums(ext[:, cols] / cnt, tl + HALO, True)
            o_ref[:, cols] = (sums[gi][:tl, :] - d[:, cols]).astype(o_ref.dtype)

    row = pl.BlockSpec((tl, gw), lambda i: (i, 0))
    return pl.pallas_call(
        body, name=name, grid=(nb,),
        in_specs=[row, pl.BlockSpec((HALO, gw), lambda i: (jnp.minimum((i + 1) * (tl // HALO), l // HALO - 1), 0))],
        out_specs=row, out_shape=jax.ShapeDtypeStruct((l, gw), BF16), compiler_params=_params(("parallel",)),
    )(dpooled, dpooled)


CONV_HALO = 8
QKV_BLK = 4


def _head_sums(x):
    parts = []
    for hd in range(DN_HEADS):
        s = jnp.sum(x[:, hd * DN_HEAD_DIM:(hd + 1) * DN_HEAD_DIM], axis=-1, keepdims=True)
        parts.append(jnp.broadcast_to(s, (x.shape[0], DN_HEAD_DIM)))
    return jnp.concatenate(parts, axis=1)


def _gdn_pre_fwd(proj, proj_ab, conv_w, a_log, dt_bias, name):
    l = proj.shape[0]
    tl = 256
    gw = GROUP_WIDTH

    def body(xq, xk, xv, hq, hk, hv, w_ref, ab_ref, al_ref, dt_ref, qn_ref, kn_ref, v_ref, cq_ref, ck_ref, cv_ref, gb_ref):
        i = pl.program_id(0)
        for p, (x_ref, h_ref, c_ref) in enumerate(((xq, hq, cq_ref), (xk, hk, ck_ref), (xv, hv, cv_ref))):
            ext = jnp.concatenate([jnp.where(i > 0, h_ref[...], 0.0), x_ref[...]], axis=0)
            conv = jnp.zeros((tl, gw), F32)
            for j in range(DN_CONV):
                k = DN_CONV - 1 - j
                shifted = ext if k == 0 else pltpu.roll(ext, k, axis=0)
                conv = conv + shifted[CONV_HALO:, :] * w_ref[j:j + 1, p * gw:(p + 1) * gw]
            c_ref[...] = conv
            s = _silu(conv)
            if p == 2:
                v_ref[...] = s
            else:
                r = lax.rsqrt(_head_sums(s * s) + L2_EPS)
                (qn_ref if p == 0 else kn_ref)[...] = s * r * (DN_HEAD_DIM ** -0.5 if p == 0 else 1.0)
        ab = ab_ref[...]
        lane = lax.broadcasted_iota(jnp.int32, ab.shape, 1)
        g = -jnp.exp(al_ref[...]) * _softplus(ab + dt_ref[...])
        gb_ref[...] = jnp.where(lane < DN_HEADS, g, _sigmoid(ab))

    def xs(b):
        return pl.BlockSpec((tl, gw), lambda i: (i, b))

    def hs(b):
        return pl.BlockSpec((CONV_HALO, gw), lambda i: (jnp.maximum(i * (tl // CONV_HALO) - 1, 0), b))

    row = pl.BlockSpec((tl, gw), lambda i: (i, 0))
    abrow = pl.BlockSpec((tl, AB_PAD), lambda i: (i, 0))
    abvec = pl.BlockSpec((1, AB_PAD), lambda i: (0, 0))
    return pl.pallas_call(
        body, name=name, grid=(l // tl,),
        in_specs=[xs(QKV_BLK), xs(QKV_BLK + 1), xs(QKV_BLK + 2), hs(QKV_BLK), hs(QKV_BLK + 1), hs(QKV_BLK + 2),
                  pl.BlockSpec((DN_CONV, 3 * gw), lambda i: (0, 0)), abrow, abvec, abvec],
        out_specs=[row] * 6 + [abrow],
        out_shape=[jax.ShapeDtypeStruct((l, gw), F32)] * 6 + [jax.ShapeDtypeStruct((l, AB_PAD), F32)],
        compiler_params=_params(("parallel",)),
    )(proj, proj, proj, proj, proj, proj, conv_w, proj_ab, a_log, dt_bias)


def _gdn_pre_bwd(dq, dk, dv, cq, ck, cv, dgb, gb, proj_ab, a_log, dt_bias, name):
    l, gw = cq.shape
    tl = 256

    def body(dq_ref, dk_ref, dv_ref, cq_ref, ck_ref, cv_ref, dgb_ref, gb_ref, ab_ref, al_ref, dt_ref,
             dcq_ref, dck_ref, dcv_ref, dab_ref, dal_ref, ddt_ref):
        @pl.when(pl.program_id(0) == 0)
        def _():
            dal_ref[...] = jnp.zeros_like(dal_ref)
            ddt_ref[...] = jnp.zeros_like(ddt_ref)

        for p, (d_ref, c_ref, o_ref) in enumerate(((dq_ref, cq_ref, dcq_ref), (dk_ref, ck_ref, dck_ref), (dv_ref, cv_ref, dcv_ref))):
            c, d = c_ref[...], d_ref[...]
            if p == 2:
                ds = d
            else:
                s = _silu(c)
                r = lax.rsqrt(_head_sums(s * s) + L2_EPS)
                ds = (DN_HEAD_DIM ** -0.5 if p == 0 else 1.0) * r * (d - s * r * r * _head_sums(d * s))
            o_ref[...] = ds * _silu_grad(c)
        ab, dgb_v, gb_v = ab_ref[...], dgb_ref[...], gb_ref[...]
        lane = lax.broadcasted_iota(jnp.int32, ab.shape, 1)
        is_g = lane < DN_HEADS
        dpre = dgb_v * (-jnp.exp(al_ref[...])) * _sigmoid(ab + dt_ref[...])
        dab_ref[...] = jnp.where(is_g, dpre, dgb_v * gb_v * (1.0 - gb_v)).astype(dab_ref.dtype)
        dal_ref[...] += jnp.sum(jnp.where(is_g, dgb_v * gb_v, 0.0), axis=0, keepdims=True)
        ddt_ref[...] += jnp.sum(jnp.where(is_g, dpre, 0.0), axis=0, keepdims=True)

    row = pl.BlockSpec((tl, gw), lambda i: (i, 0))
    abrow = pl.BlockSpec((tl, AB_PAD), lambda i: (i, 0))
    abvec = pl.BlockSpec((1, AB_PAD), lambda i: (0, 0))
    return pl.pallas_call(
        body, name=name, grid=(l // tl,),
        in_specs=[row] * 6 + [abrow, abrow, abrow, abvec, abvec],
        out_specs=[row, row, row, abrow, abvec, abvec],
        out_shape=[jax.ShapeDtypeStruct((l, gw), F32)] * 3 + [jax.ShapeDtypeStruct((l, AB_PAD), BF16),
                   jax.ShapeDtypeStruct((1, AB_PAD), F32), jax.ShapeDtypeStruct((1, AB_PAD), F32)],
        compiler_params=_params(("arbitrary",)),
    )(dq, dk, dv, cq, ck, cv, dgb, gb, proj_ab, a_log, dt_bias)


def _conv_bwd(dc, proj, col_blk, w_part, name):
    l, gw = dc.shape
    tl = 256
    nb = l // tl

    def body(dc_ref, halo_ref, x_ref, w_ref, dx_ref, dw_ref):
        i = pl.program_id(0)

        @pl.when(i == 0)
        def _():
            dw_ref[...] = jnp.zeros_like(dw_ref)

        ext = jnp.concatenate([dc_ref[...], jnp.where(i < nb - 1, halo_ref[...], 0.0)], axis=0)
        x = x_ref[...]
        dx = jnp.zeros((tl, gw), F32)
        rid = lax.broadcasted_iota(jnp.int32, (8, gw), 0)
        dw = jnp.zeros((8, gw), F32)
        for j in range(DN_CONV):
            k = DN_CONV - 1 - j
            shifted = (ext if k == 0 else pltpu.roll(ext, tl + CONV_HALO - k, axis=0))[:tl, :]
            dx = dx + shifted * w_ref[j:j + 1, :]
            dw = dw + jnp.where(rid == j, jnp.sum(x * shifted, axis=0, keepdims=True), 0.0)
        dx_ref[...] = dx.astype(dx_ref.dtype)
        dw_ref[...] += dw

    row = pl.BlockSpec((tl, gw), lambda i: (i, 0))
    return pl.pallas_call(
        body, name=name, grid=(nb,),
        in_specs=[row, pl.BlockSpec((CONV_HALO, gw), lambda i: (jnp.minimum((i + 1) * (tl // CONV_HALO), l // CONV_HALO - 1), 0)),
                  pl.BlockSpec((tl, gw), lambda i: (i, col_blk)), pl.BlockSpec((DN_CONV, gw), lambda i: (0, 0))],
        out_specs=[row, pl.BlockSpec((8, gw), lambda i: (0, 0))],
        out_shape=[jax.ShapeDtypeStruct((l, gw), BF16), jax.ShapeDtypeStruct((8, gw), F32)],
        compiler_params=_params(("arbitrary",)),
    )(dc, dc, proj, w_part)


TERMS_CHUNKS = 4


def _bdot(a, b, dims="nn", precision=None):
    cd = {"nn": ((2,), (1,)), "nt": ((2,), (2,)), "tn": ((1,), (1,))}[dims]
    return lax.dot_general(a, b, (cd, ((0,), (0,))), preferred_element_type=F32, precision=precision)


def _bmdot(a, b, dims="nn"):
    return _bdot(_mx(a), _mx(b), dims)


def _wy_terms(q, k, v, gcol, beta, t=None):
    c = DN_CHUNK
    ii = lax.broadcasted_iota(jnp.int32, (1, c, c), 1)
    jj = lax.broadcasted_iota(jnp.int32, (1, c, c), 2)
    tril, strict = ii >= jj, ii > jj
    grow = jnp.sum(jnp.where(ii == jj, gcol, 0.0), axis=1, keepdims=True)
    gc_col = jnp.sum(jnp.where(tril, grow, 0.0), axis=2, keepdims=True)
    gc_row = jnp.sum(jnp.where(ii <= jj, gcol, 0.0), axis=1, keepdims=True)
    dec = jnp.exp(jnp.where(tril, gc_col - gc_row, -1e30))
    kb, vb = k * beta, v * beta
    kk = _bmdot(kb, k, "nt")
    if t is None:
        a = jnp.where(strict, kk * dec, 0.0)
        d = jnp.where((ii >> 3) == (jj >> 3), a, 0.0)
        t = jnp.where(ii == jj, 1.0, 0.0) - d
        p = _bdot(d, d, precision=HI)
        t = t + _bdot(t, p, precision=HI)
        t = t + _bdot(t, _bdot(p, p, precision=HI), precision=HI)
        for sh in (3, 4, 5):
            below = ((ii >> (sh + 1)) == (jj >> (sh + 1))) & ((ii >> sh) > (jj >> sh))
            t = t - _bdot(t, _bdot(jnp.where(below, a, 0.0), t, precision=HI), precision=HI)
    eg = jnp.exp(gc_col)
    gc_last = gc_col[:, c - 1:c, :]
    kbg = kb * eg
    qk0 = _bmdot(q, k, "nt")
    e2 = jnp.exp(gc_last - gc_col)
    return dict(ii=ii, jj=jj, tril=tril, strict=strict, dec=dec, kb=kb, vb=vb, kk=kk, t=t, eg=eg, kbg=kbg,
                qk0=qk0, qk=jnp.where(tril, qk0 * dec, 0.0), qg=q * eg, e2=e2, kt=k * e2, gl=jnp.exp(gc_last))


def _to_heads(x, g):
    return jnp.concatenate([x[:, h * DN_HEAD_DIM:(h + 1) * DN_HEAD_DIM].reshape(g, DN_CHUNK, DN_HEAD_DIM)
                            for h in range(DN_HEADS)], axis=0)


def _from_heads(t, ref, g):
    for h in range(DN_HEADS):
        ref[:, h * DN_HEAD_DIM:(h + 1) * DN_HEAD_DIM] = t[h * g:(h + 1) * g].reshape(g * DN_CHUNK, DN_HEAD_DIM).astype(ref.dtype)


def _head_columns(gbv, first_lane, g):
    lane = lax.broadcasted_iota(jnp.int32, gbv.shape, 1)
    return jnp.concatenate([jnp.sum(jnp.where(lane == first_lane + h, gbv, 0.0), axis=1, keepdims=True).reshape(g, DN_CHUNK, 1)
                            for h in range(DN_HEADS)], axis=0)


def _gdn_terms_fwd(qn, kn, v, gb, name):
    l = qn.shape[0]
    n_chunks = l // DN_CHUNK
    g = min(TERMS_CHUNKS, n_chunks)
    rows, c, nh = g * DN_CHUNK, DN_CHUNK, DN_HEADS

    def body(q_ref, k_ref, v_ref, gb_ref, u_ref, w_ref, qg_ref, kt_ref, qk_ref, t_ref, gl_ref):
        gbv = gb_ref[...]
        x = _wy_terms(_to_heads(q_ref[...], g), _to_heads(k_ref[...], g), _to_heads(v_ref[...], g),
                      _head_columns(gbv, 0, g), _head_columns(gbv, nh, g))
        _from_heads(_bmdot(x["t"], x["vb"]), u_ref, g)
        _from_heads(_bmdot(x["t"], x["kbg"]), w_ref, g)
        _from_heads(x["qg"], qg_ref, g)
        _from_heads(x["kt"], kt_ref, g)
        for h in range(nh):
            qk_ref[:, h] = x["qk"][h * g:(h + 1) * g]
            t_ref[:, h] = x["t"][h * g:(h + 1) * g]
            gl_ref[:, h] = jnp.broadcast_to(x["gl"][h * g:(h + 1) * g], (g, 1, 128))

    row = pl.BlockSpec((rows, GROUP_WIDTH), lambda i: (i, 0))
    sq = pl.BlockSpec((g, nh, c, c), lambda i: (i, 0, 0, 0))
    glb = pl.BlockSpec((g, nh, 1, 128), lambda i: (i, 0, 0, 0))
    return pl.pallas_call(
        body, name=name, grid=(n_chunks // g,), in_specs=[row, row, row, pl.BlockSpec((rows, AB_PAD), lambda i: (i, 0))],
        out_specs=[row] * 4 + [sq, sq, glb],
        out_shape=[jax.ShapeDtypeStruct((l, GROUP_WIDTH), F32)] * 4 + [jax.ShapeDtypeStruct((n_chunks, nh, c, c), F32)] * 2
        + [jax.ShapeDtypeStruct((n_chunks, nh, 1, 128), F32)],
        compiler_params=_params(("parallel",)),
    )(qn, kn, v, gb)


REC_CHUNKS = 4


def _rec_specs(n_chunks, reverse):
    c, hd, nh = DN_CHUNK, DN_HEAD_DIM, DN_HEADS
    g = min(REC_CHUNKS, n_chunks)
    nb = n_chunks // g
    blk_of = (lambda n: nb - 1 - n) if reverse else (lambda n: n)
    return g, nb, (pl.BlockSpec((g * c, GROUP_WIDTH), lambda n: (blk_of(n), 0)), pl.BlockSpec((g, nh, c, c), lambda n: (blk_of(n), 0, 0, 0)),
                   pl.BlockSpec((g, nh, 1, 128), lambda n: (blk_of(n), 0, 0, 0)), pl.BlockSpec((g, nh, hd, hd), lambda n: (blk_of(n), 0, 0, 0)))


def _gdn_rec_fwd(u, w, qg, kt, qk, gl, name):
    l = u.shape[0]
    n_chunks = l // DN_CHUNK
    hd, nh, c = DN_HEAD_DIM, DN_HEADS, DN_CHUNK
    g, nb, (blk, sq, glb, st) = _rec_specs(n_chunks, False)
    heads = range(nh)

    def body(u_ref, w_ref, qg_ref, kt_ref, qk_ref, gl_ref, o_ref, vn_ref, s_ref, state):
        @pl.when(pl.program_id(0) == 0)
        def _():
            state[...] = jnp.zeros_like(state)

        def cols(h):
            return slice(h * hd, (h + 1) * hd)
        for ci in range(g):
            rows = slice(ci * c, (ci + 1) * c)
            s = [state[h] for h in heads]
            ws = [_mdot(w_ref[rows, cols(h)], s[h]) for h in heads]
            vn = [u_ref[rows, cols(h)] - ws[h] for h in heads]
            kv = [_mdot(kt_ref[rows, cols(h)], vn[h], "tn") for h in heads]
            for h in heads:
                state[h] = s[h] * gl_ref[ci, h] + kv[h]
            o1 = [_mdot(qg_ref[rows, cols(h)], s[h]) for h in heads]
            o2 = [_mdot(qk_ref[ci, h], vn[h]) for h in heads]
            for h in heads:
                s_ref[ci, h] = s[h]
                o_ref[rows, cols(h)] = o1[h] + o2[h]
                vn_ref[rows, cols(h)] = vn[h]

    return pl.pallas_call(
        body, name=name, grid=(nb,), in_specs=[blk, blk, blk, blk, sq, glb], out_specs=[blk, blk, st],
        out_shape=[jax.ShapeDtypeStruct((l, GROUP_WIDTH), F32)] * 2 + [jax.ShapeDtypeStruct((n_chunks, nh, hd, hd), F32)],
        scratch_shapes=[pltpu.VMEM((nh, hd, hd), F32)], compiler_params=_params(("arbitrary",)),
    )(u, w, qg, kt, qk, gl)


def _gdn_rec_bwd(do, w, qg, kt, vn, qk, gl, states, name):
    l = do.shape[0]
    n_chunks = l // DN_CHUNK
    hd, nh, c = DN_HEAD_DIM, DN_HEADS, DN_CHUNK
    g, nb, (blk, sq, glb, st) = _rec_specs(n_chunks, True)
    heads = range(nh)

    def body(do_ref, w_ref, qg_ref, kt_ref, vn_ref, qk_ref, gl_ref, s_ref, dvn_ref, dw_ref, dkt_ref, dqg_ref, dqk_ref, dgl_ref, dstate):
        @pl.when(pl.program_id(0) == 0)
        def _():
            dstate[...] = jnp.zeros_like(dstate)

        def cols(h):
            return slice(h * hd, (h + 1) * hd)
        tril = lax.broadcasted_iota(jnp.int32, (c, c), 0) >= lax.broadcasted_iota(jnp.int32, (c, c), 1)
        for ci in reversed(range(g)):
            rows = slice(ci * c, (ci + 1) * c)
            ds = [dstate[h] for h in heads]
            dout = [do_ref[rows, cols(h)] for h in heads]
            a1 = [_mdot(qk_ref[ci, h], dout[h], "tn") for h in heads]
            a2 = [_mdot(kt_ref[rows, cols(h)], ds[h]) for h in heads]
            dvn = [a1[h] + a2[h] for h in heads]
            b1 = [_mdot(qg_ref[rows, cols(h)], dout[h], "tn") for h in heads]
            b2 = [_mdot(w_ref[rows, cols(h)], dvn[h], "tn") for h in heads]
            for h in heads:
                dstate[h] = b1[h] + gl_ref[ci, h] * ds[h] - b2[h]
            for h in heads:
                s, vnew = s_ref[ci, h], vn_ref[rows, cols(h)]
                dvn_ref[rows, cols(h)] = dvn[h]
                dw_ref[rows, cols(h)] = -_mdot(dvn[h], s, "nt")
                dkt_ref[rows, cols(h)] = _mdot(vnew, ds[h], "nt")
                dqg_ref[rows, cols(h)] = _mdot(dout[h], s, "nt")
                dqk_ref[ci, h] = jnp.where(tril, _mdot(dout[h], vnew, "nt"), 0.0)
                dgl = jnp.sum(jnp.sum(ds[h] * s, axis=1, keepdims=True), axis=0, keepdims=True)
                dgl_ref[ci, h] = jnp.broadcast_to(dgl, (1, 128))

    return pl.pallas_call(
        body, name=name, grid=(nb,), in_specs=[blk] * 5 + [sq, glb, st], out_specs=[blk] * 4 + [sq, glb],
        out_shape=[jax.ShapeDtypeStruct((l, GROUP_WIDTH), F32)] * 4 + [jax.ShapeDtypeStruct((n_chunks, nh, c, c), F32),
                                                                       jax.ShapeDtypeStruct((n_chunks, nh, 1, 128), F32)],
        scratch_shapes=[pltpu.VMEM((nh, hd, hd), F32)], compiler_params=_params(("arbitrary",)),
    )(do, w, qg, kt, vn, qk, gl, states)


def _gdn_terms_bwd(qn, kn, v, gb, t_inv, dvn, dw, dkt, dqg, dqk, dgl, name):
    l = qn.shape[0]
    n_chunks = l // DN_CHUNK
    g = min(TERMS_CHUNKS, n_chunks)
    rows, c, nh = g * DN_CHUNK, DN_CHUNK, DN_HEADS

    def body(q_ref, k_ref, v_ref, gb_ref, t_ref, dvn_ref, dw_ref, dkt_ref, dqg_ref, dqk_ref, dgl_ref, dq_ref, dk_ref, dv_ref, dgb_ref):
        gbv = gb_ref[...]
        q, k, vv = _to_heads(q_ref[...], g), _to_heads(k_ref[...], g), _to_heads(v_ref[...], g)
        beta = _head_columns(gbv, nh, g)
        t = jnp.concatenate([t_ref[:, h] for h in range(nh)], axis=0)
        x = _wy_terms(q, k, vv, _head_columns(gbv, 0, g), beta, t=t)
        ii, jj, strict = x["ii"], x["jj"], x["strict"]
        du, dwv, dktv, dqgv = (_to_heads(r[...], g) for r in (dvn_ref, dw_ref, dkt_ref, dqg_ref))
        dqkv = jnp.concatenate([dqk_ref[:, h] for h in range(nh)], axis=0)
        dglv = jnp.concatenate([dgl_ref[:, h] for h in range(nh)], axis=0)[:, :, 0:1]
        dt = _bmdot(du, x["vb"], "nt") + _bmdot(dwv, x["kbg"], "nt")
        dvb = _bmdot(t, du, "tn")
        dkbg = _bmdot(t, dwv, "tn")
        tt_dt = _bdot(t, dt, "tn", precision=HI)
        da = jnp.where(strict, -_bdot(tt_dt, t, "nt", precision=HI), 0.0)
        dkk = da * x["dec"]
        dqk0 = dqkv * x["dec"]
        e = (da * x["kk"] + dqkv * x["qk0"]) * x["dec"]
        dkb = _bmdot(dkk, k) + dkbg * x["eg"]
        dk = _bmdot(dkk, x["kb"], "tn") + _bmdot(dqk0, q, "tn") + dktv * x["e2"] + dkb * beta
        dq = _bmdot(dqk0, k) + dqgv * x["eg"]
        s_kt = jnp.sum(dktv * x["kt"], axis=2, keepdims=True)
        dgc_c = (jnp.sum(e, axis=2, keepdims=True) + jnp.sum(dqgv * x["qg"], axis=2, keepdims=True) - s_kt
                 + jnp.sum(dkbg * x["kbg"], axis=2, keepdims=True))
        dgc_last = jnp.sum(s_kt, axis=1, keepdims=True) + dglv * x["gl"]
        rid = lax.broadcasted_iota(jnp.int32, (1, c, 1), 1)
        dgc_c = dgc_c + jnp.where(rid == c - 1, dgc_last, 0.0)
        dgc_r = jnp.sum(jnp.where(ii == jj, dgc_c, 0.0), axis=1, keepdims=True) - jnp.sum(e, axis=1, keepdims=True)
        dg = jnp.sum(jnp.where(jj >= ii, dgc_r, 0.0), axis=2, keepdims=True)
        dbeta = jnp.sum(dkb * k, axis=2, keepdims=True) + jnp.sum(dvb * vv, axis=2, keepdims=True)
        _from_heads(dq, dq_ref, g)
        _from_heads(dk, dk_ref, g)
        _from_heads(dvb * beta, dv_ref, g)
        lane = lax.broadcasted_iota(jnp.int32, gbv.shape, 1)
        dgb = jnp.zeros(gbv.shape, F32)
        for h in range(nh):
            dgb = dgb + jnp.where(lane == h, dg[h * g:(h + 1) * g].reshape(rows, 1), 0.0)
            dgb = dgb + jnp.where(lane == nh + h, dbeta[h * g:(h + 1) * g].reshape(rows, 1), 0.0)
        dgb_ref[...] = dgb

    row = pl.BlockSpec((rows, GROUP_WIDTH), lambda i: (i, 0))
    abrow = pl.BlockSpec((rows, AB_PAD), lambda i: (i, 0))
    sq = pl.BlockSpec((g, nh, c, c), lambda i: (i, 0, 0, 0))
    glb = pl.BlockSpec((g, nh, 1, 128), lambda i: (i, 0, 0, 0))
    return pl.pallas_call(
        body, name=name, grid=(n_chunks // g,), in_specs=[row, row, row, abrow, sq, row, row, row, row, sq, glb],
        out_specs=[row, row, row, abrow],
        out_shape=[jax.ShapeDtypeStruct((l, GROUP_WIDTH), F32)] * 3 + [jax.ShapeDtypeStruct((l, AB_PAD), F32)],
        compiler_params=_params(("parallel",)),
    )(qn, kn, v, gb, t_inv, dvn, dw, dkt, dqg, dqk, dgl)


def _gdn_post_fwd(o, proj, norm_g4, name):
    l, gw = o.shape
    tl = min(512, l)

    def body(o_ref, gate_ref, g_ref, out_ref):
        ov = o_ref[...]
        r = lax.rsqrt(_head_sums(ov * ov) * (1.0 / DN_HEAD_DIM) + RMS_EPS)
        out_ref[...] = (ov * r * g_ref[...] * _silu(gate_ref[...])).astype(out_ref.dtype)

    row = pl.BlockSpec((tl, gw), lambda i: (i, 0))
    return pl.pallas_call(
        body, name=name, grid=(l // tl,),
        in_specs=[row, pl.BlockSpec((tl, gw), lambda i: (i, 7)), pl.BlockSpec((1, gw), lambda i: (0, 0))],
        out_specs=row, out_shape=jax.ShapeDtypeStruct((l, gw), BF16), compiler_params=_params(("parallel",)),
    )(o, proj, norm_g4)


def _gdn_post_bwd(dmixed, o, proj, norm_g4, name):
    l, gw = o.shape
    tl = min(512, l)

    def body(d_ref, o_ref, gate_ref, g_ref, do_ref, dgate_ref, dng_ref):
        @pl.when(pl.program_id(0) == 0)
        def _():
            dng_ref[...] = jnp.zeros_like(dng_ref)

        ov, gate, d = o_ref[...], gate_ref[...], d_ref[...]
        r = lax.rsqrt(_head_sums(ov * ov) * (1.0 / DN_HEAD_DIM) + RMS_EPS)
        oh = ov * r
        sg = _silu(gate)
        dgate_ref[...] = (d * oh * g_ref[...] * _silu_grad(gate)).astype(dgate_ref.dtype)
        dng_ref[...] += jnp.sum(d * sg * oh, axis=0, keepdims=True)
        doh = d * g_ref[...] * sg
        do_ref[...] = r * (doh - oh * _head_sums(doh * oh) * (1.0 / DN_HEAD_DIM))

    row = pl.BlockSpec((tl, gw), lambda i: (i, 0))
    vec = pl.BlockSpec((1, gw), lambda i: (0, 0))
    return pl.pallas_call(
        body, name=name, grid=(l // tl,),
        in_specs=[pl.BlockSpec((tl, gw), lambda i: (i, 3)), row, pl.BlockSpec((tl, gw), lambda i: (i, 7)), vec],
        out_specs=[row, row, vec],
        out_shape=[jax.ShapeDtypeStruct((l, gw), F32), jax.ShapeDtypeStruct((l, gw), BF16), jax.ShapeDtypeStruct((1, gw), F32)],
        compiler_params=_params(("arbitrary",)),
    )(dmixed, o, proj, norm_g4)


def _run(hosts, name, fn):
    h = hosts.get(name)
    if h is None:
        return fn(None)
    res, outs = fn(h[0]())
    h[1](outs)
    return res


def _layer_fwd(x, xm, w, li, hosts):
    l = x.shape[0]
    nm = f"l{li}_"
    proj = _run(hosts, nm + "proj", lambda ops: _matmul(
        xm, w["w_main"], mode="nn", tm=1024, tn=1024, tk=2048,out_dtype=F32, name=nm + "proj", comm=ops))
    proj_ab = _matmul(xm, w["w_ab"], mode="nn", tm=1024, tn=AB_PAD, tk=2048, out_dtype=F32, name=nm + "proj_ab")
    hs, y = _s5_fwd(proj, w["s5_b"], w["s5_c"], w["s5_lam"], w["s5_d"], nm + "s5")
    m_s5 = _s5_glu_fwd(y, w["s5_glu_w"], w["s5_glu_b"], nm + "s5_glu")
    m_sgu = _sgu_fwd(proj, w["sgu_norm_g"], w["sgu_norm_b"], w["sgu_wm"], w["sgu_bfull"], nm + "sgu")
    m_pool, pooled = _pool_fwd(proj, w["pool_w"], w["pool_scale"], nm + "pool")
    qn, kn, v, cq, ck, cv, gb = _gdn_pre_fwd(proj, proj_ab, w["dn_conv_w"], w["dn_a_log"], w["dn_dt_bias"], nm + "gdn_pre")
    u, wy, qg, kt, qk, t_inv, gl = _gdn_terms_fwd(qn, kn, v, gb, nm + "gdn_terms")
    o, vn, states = _gdn_rec_fwd(u, wy, qg, kt, qk, gl, nm + "gdn_rec")
    m_dn = _gdn_post_fwd(o, proj, w["dn_norm_g4"], nm + "gdn_post")
    mixed = jnp.concatenate([m_s5, m_sgu, m_pool, m_dn], axis=1)
    y1 = _matmul(mixed, w["w_out"], mode="nn", tm=1024, tn=1024, tk=2048, out_dtype=F32, name=nm + "out_proj")
    h1, x1, x1m = _ln_fwd(x, y1, w["ln1_g"], w["ln1_b"], nm + "ln1")
    r = _run(hosts, nm + "up", lambda ops: _matmul(
        x1m, w["w_up"], mode="nn", tm=1024, tn=1024, tk=2048,out_dtype=BF16, name=nm + "up",
        epi=lambda acc: jnp.maximum(acc, 0.0), b_slab=w["w_up"].shape[2], comm=ops))
    y2 = _run(hosts, nm + "down", lambda ops: _matmul(
        r, w["w_down"], mode="nn", tm=1024, tn=1024, tk=2048,out_dtype=F32, name=nm + "down", a_fn=lambda a: a * a, comm=ops))
    h2, x2, x2m = _ln_fwd(x1, y2, w["ln2_g"], w["ln2_b"], nm + "ln2")
    saved = dict(xm=xm, proj=proj, proj_ab=proj_ab, hs=hs, y=y, pooled=pooled, qn=qn, kn=kn, v=v, cq=cq, ck=ck, cv=cv, gb=gb,
                 wy=wy, qg=qg, kt=kt, qk=qk, t_inv=t_inv, gl=gl, vn=vn, o=o, states=states, mixed=mixed, h1=h1, x1m=x1m,
                 r=r, h2=h2)
    return x2, x2m, saved


def _layer_bwd(dx2, s, w, small, li, hosts, g):
    nm = f"l{li}b_"
    l = dx2.shape[0]
    gw = GROUP_WIDTH
    wire = MXU_DTYPE
    dh2, dh2m, g["ln2_g"], g["ln2_b"] = _ln_bwd(dx2, s["h2"], w["ln2_g"], nm + "ln2")
    g["w_down"] = _run(hosts, nm + "dw_down", lambda ops: _matmul(
        s["r"], dh2m, mode="tn", tm=1024, tn=1024, tk=2048,out_dtype=wire, name=nm + "dw_down", a_fn=lambda a: a * a,
        comm=ops)).reshape(N_DEV, D_FF // N_DEV, D_MODEL)
    dpre = _run(hosts, nm + "dpre", lambda ops: _matmul(
        dh2m, w["w_down"], mode="nt", tm=1024, tn=1024, tk=2048,out_dtype=BF16, name=nm + "dpre",
        extras=[(s["r"], (None, None), lambda i, j: (i, j))], epi=lambda acc, r: acc * 2.0 * r.astype(F32), comm=ops))
    g["w_up"] = _matmul(s["x1m"], dpre, mode="tn", tm=1024, tn=1024, tk=2048,out_dtype=wire, name=nm + "dw_up",
                        out_slab=D_FF // N_DEV)
    dx1 = _run(hosts, nm + "dx1", lambda ops: _matmul(
        dpre, w["w_up"], mode="nt", tm=1024, tn=1024, tk=2048,out_dtype=F32, name=nm + "dx1",
        extras=[(dh2, (None, None), lambda i, j: (i, j))], epi=lambda acc, e: acc + ALPHA * e,
        b_slab=w["w_up"].shape[2], comm=ops))
    dh1, dh1m, g["ln1_g"], g["ln1_b"] = _ln_bwd(dx1, s["h1"], w["ln1_g"], nm + "ln1")
    g["w_out"] = _matmul(s["mixed"], dh1m, mode="tn", tm=1024, tn=1024, tk=2048,out_dtype=wire,
                         name=nm + "dw_out").reshape(N_DEV, D_MODEL // N_DEV, D_MODEL)
    dmixed = _run(hosts, nm + "dmixed", lambda ops: _matmul(
        dh1m, w["w_out"], mode="nt", tm=1024, tn=1024, tk=2048,out_dtype=F32, name=nm + "dmixed", comm=ops))
    proj, proj_ab = s["proj"], s["proj_ab"]
    dy, dz, yg, g["s5_glu_b"], g["s5_d"] = _s5_glu_bwd(dmixed, s["y"], proj, w["s5_glu_w"], w["s5_glu_b"], nm + "s5_glu")
    g["s5_glu_w"] = _matmul(yg, dz, mode="tn", tm=gw, tn=gw, tk=1024, out_dtype=wire,
                            name=nm + "dw_glu").reshape(N_DEV, gw // N_DEV, gw)
    du_s5, g["s5_b"], g["s5_c"], g["s5_lam"] = _s5_bwd(dy, s["hs"], proj, w["s5_b"], w["s5_c"], w["s5_lam_conj"], w["s5_d"], nm + "s5")
    dzu, dzv, g["sgu_w"], g["sgu_bfull"], g["sgu_norm_g"], g["sgu_norm_b"] = _sgu_bwd(
        dmixed, proj, w["sgu_norm_g"], w["sgu_norm_b"], w["sgu_wm"], w["sgu_bfull"], nm + "sgu")
    dpooled, g["pool_w"], g["pool_scale"] = _pool_bwd_map(dmixed, s["pooled"], w["pool_w"], w["pool_scale"], nm + "pool_map")
    dp = _pool_bwd_window(dpooled, nm + "pool_win")
    do, dgate, g["dn_norm_g4"] = _gdn_post_bwd(dmixed, s["o"], proj, w["dn_norm_g4"], nm + "gdn_post")
    dvn, dwy, dkt, dqg, dqk, dgl = _gdn_rec_bwd(do, s["wy"], s["qg"], s["kt"], s["vn"], s["qk"], s["gl"], s["states"], nm + "gdn_rec")
    dq, dk, dv, dgb = _gdn_terms_bwd(s["qn"], s["kn"], s["v"], s["gb"], s["t_inv"], dvn, dwy, dkt, dqg, dqk, dgl, nm + "gdn_terms")
    dcq, dck, dcv, dab, g["dn_a_log"], g["dn_dt_bias"] = _gdn_pre_bwd(
        dq, dk, dv, s["cq"], s["ck"], s["cv"], dgb, s["gb"], proj_ab, w["dn_a_log"], w["dn_dt_bias"], nm + "gdn_pre")
    dxs, dws = [], []
    for p, dc in enumerate((dcq, dck, dcv)):
        dxp, dwp = _conv_bwd(dc, proj, QKV_BLK + p, w["dn_conv_w"][:, p * gw:(p + 1) * gw], nm + f"conv{p}")
        dxs.append(dxp)
        dws.append(dwp)
    dconv = jnp.concatenate(dws, axis=1)
    g["dn_conv_w"] = jnp.transpose(dconv.reshape(dconv.shape[0], N_DEV, 3 * gw // N_DEV), (1, 0, 2))
    dproj = jnp.concatenate([du_s5, dzu, dzv, dp] + dxs + [dgate], axis=1)
    xm = s["xm"]
    g["small"] = _unprep_grads(g, small)
    dw_main = _run(hosts, nm + "dw_main", lambda ops: _matmul(
        xm, dproj, mode="tn", tm=1024, tn=1024, tk=2048,out_dtype=wire, name=nm + "dw_main", comm=ops))
    dw_ab = _matmul(xm, dab, mode="tn", tm=1024, tn=AB_PAD, tk=1024, out_dtype=wire, name=nm + "dw_ab")
    dw_in = jnp.concatenate([dw_main, dw_ab[:, :2 * DN_HEADS]], axis=1)
    g["w_in"] = jnp.transpose(dw_in.reshape(D_MODEL, N_DEV, dw_in.shape[1] // N_DEV), (1, 0, 2))
    return _run(hosts, nm + "dx", lambda ops: _matmul(
        dproj, w["w_main"], mode="nt", tm=1024, tn=1024, tk=2048, out_dtype=F32, name=nm + "dx",
        extras=[(dh1, (None, None), lambda i, j: (i, j)), (dab, (None, AB_PAD), lambda i, j: (i, 0)),
                (w["w_ab"], ("tn", AB_PAD), lambda i, j: (j, 0))],
        epi=lambda acc, e, da, wab: acc + ALPHA * e + _dot(_mx(da), _mx(wab), "nt"), comm=ops))


SMALL = ("s5_lambda_re", "s5_lambda_im", "s5_log_step", "s5_b_re", "s5_b_im", "s5_c_re", "s5_c_im", "s5_d", "s5_glu_b",
         "sgu_norm_g", "sgu_norm_b", "sgu_w", "sgu_b", "pool_w", "pool_scale", "dn_a_log", "dn_dt_bias", "dn_norm_g",
         "ln1_g", "ln1_b", "ln2_g", "ln2_b")
SHARDED = ("w_in", "s5_glu_w", "dn_conv_w", "w_out", "w_up", "w_down")


def _pad_lanes(v, width=AB_PAD):
    return jnp.pad(v.reshape(1, -1), ((0, 0), (0, width - v.size)))


def _prep_small(p):
    mx = MXU_DTYPE
    lbr, lbi, bbr, bbi = _s5_discretize(p["s5_lambda_re"], p["s5_lambda_im"], p["s5_log_step"], p["s5_b_re"], p["s5_b_im"])
    b_compact, c_compact = _s5_compact(bbr, bbi, p["s5_c_re"], p["s5_c_im"])
    causal = jnp.tril(jnp.ones((SGU_CHUNK, SGU_CHUNK), F32))
    return dict(
        s5_b=b_compact.astype(mx), s5_c=c_compact.astype(mx),
        s5_lam=jnp.concatenate([lbr.reshape(1, -1), lbi.reshape(1, -1)], axis=1),
        s5_lam_conj=jnp.concatenate([lbr.reshape(1, -1), -lbi.reshape(1, -1)], axis=1),
        s5_d=p["s5_d"].reshape(1, -1), s5_glu_b=p["s5_glu_b"].reshape(1, -1),
        sgu_norm_g=p["sgu_norm_g"].reshape(1, -1), sgu_norm_b=p["sgu_norm_b"].reshape(1, -1),
        sgu_wm=(p["sgu_w"] * causal).astype(mx), sgu_bfull=jnp.repeat(p["sgu_b"].T, GROUP_WIDTH // SGU_HEADS, axis=1),
        pool_w=p["pool_w"].astype(mx), pool_scale=p["pool_scale"].reshape(1, -1),
        dn_a_log=_pad_lanes(p["dn_a_log"]), dn_dt_bias=_pad_lanes(p["dn_dt_bias"]),
        dn_norm_g4=jnp.tile(p["dn_norm_g"].reshape(1, -1), (1, DN_HEADS)),
        ln1_g=p["ln1_g"].reshape(1, -1), ln1_b=p["ln1_b"].reshape(1, -1),
        ln2_g=p["ln2_g"].reshape(1, -1), ln2_b=p["ln2_b"].reshape(1, -1),
    )


def _weight_views(name, t):
    if name == "w_in":
        w_in = jnp.transpose(t, (1, 0, 2)).reshape(t.shape[1], N_DEV * t.shape[2])
        pad = AB_PAD - (w_in.shape[1] - MAIN_COLS)
        return dict(w_main=w_in[:, :MAIN_COLS], w_ab=jnp.pad(w_in[:, MAIN_COLS:], ((0, 0), (0, pad))))
    if name == "dn_conv_w":
        return dict(dn_conv_w=jnp.transpose(t, (1, 0, 2)).reshape(t.shape[1], N_DEV * t.shape[2]))
    if name == "w_up":
        return dict(w_up=t)
    return {name: t.reshape(N_DEV * t.shape[1], t.shape[2])}


def _unprep_grads(g, p):
    causal = jnp.tril(jnp.ones((SGU_CHUNK, SGU_CHUNK), F32))
    dbbr, dbbi = _s5_uncompact_b(g["s5_b"])
    dc_re, dc_im = _s5_uncompact_c(g["s5_c"])
    dlbr, dlbi = g["s5_lam"][0, :S5_NS].reshape(S5_GROUPS, S5_STATE), g["s5_lam"][0, S5_NS:].reshape(S5_GROUPS, S5_STATE)
    _, vjp = jax.vjp(_s5_discretize, p["s5_lambda_re"], p["s5_lambda_im"], p["s5_log_step"], p["s5_b_re"], p["s5_b_im"])
    d_lre, d_lim, d_step, d_bre, d_bim = vjp((dlbr, dlbi, dbbr, dbbi))
    hd = GROUP_WIDTH // SGU_HEADS
    return dict(
        s5_lambda_re=d_lre, s5_lambda_im=d_lim, s5_log_step=d_step, s5_b_re=d_bre, s5_b_im=d_bim, s5_c_re=dc_re, s5_c_im=dc_im,
        s5_d=g["s5_d"].reshape(S5_GROUPS, S5_CH), s5_glu_b=g["s5_glu_b"].reshape(-1),
        sgu_norm_g=g["sgu_norm_g"].reshape(-1), sgu_norm_b=g["sgu_norm_b"].reshape(-1), sgu_w=g["sgu_w"] * causal,
        sgu_b=jnp.sum(g["sgu_bfull"].reshape(SGU_CHUNK, SGU_HEADS, hd), axis=2).T,
        pool_w=g["pool_w"], pool_scale=g["pool_scale"].reshape(-1),
        dn_a_log=g["dn_a_log"][0, :DN_HEADS], dn_dt_bias=g["dn_dt_bias"][0, :DN_HEADS],
        dn_norm_g=jnp.sum(g["dn_norm_g4"].reshape(DN_HEADS, DN_HEAD_DIM), axis=0),
        ln1_g=g["ln1_g"].reshape(-1), ln1_b=g["ln1_b"].reshape(-1), ln2_g=g["ln2_g"].reshape(-1), ln2_b=g["ln2_b"].reshape(-1),
    )


def _local_step(x, target, ops, small, fwd_hosts, bwd_hosts, grads):
    saved = []
    h, hm = x, x.astype(MXU_DTYPE)
    for i in range(DEPTH):
        h, hm, s = _layer_fwd(h, hm, ops[i], i, fwd_hosts)
        saved.append(s)
    loss, dh = _loss_head(h, target)
    for i in reversed(range(DEPTH)):
        dh = _layer_bwd(dh, saved[i], ops[i], small[i], i, bwd_hosts, grads[i])
    return loss, dh


def _adamw(w, gparts, m, v, name):
    rr, c = w.shape
    ng = len(gparts)
    r = rr // ng
    lanes = -(-c // 128) * 128
    tr = r
    while tr * lanes * 4 * N_DEV > (4 << 20) and tr % 16 == 0:
        tr //= 2
    nb = r // tr

    def body(w_ref, *rest):
        g_refs, (m_ref, v_ref, go_ref, d_ref, mo_ref, vo_ref) = rest[:ng], rest[ng:]
        layer = pl.program_id(0)
        g = jnp.zeros(m_ref.shape, F32)
        for li in range(ng):
            gl = g_refs[li][0].astype(F32)
            for s in range(1, N_DEV):
                gl = gl + g_refs[li][s].astype(F32)
            g = jnp.where(layer == li, gl, g)
        mn = ADAM_B1 * m_ref[...] + (1.0 - ADAM_B1) * g
        vn = ADAM_B2 * v_ref[...] + (1.0 - ADAM_B2) * g * g
        m_hat = mn / (1.0 - ADAM_B1 ** ADAM_STEP)
        v_hat = vn / (1.0 - ADAM_B2 ** ADAM_STEP)
        go_ref[...] = g
        d_ref[...] = -ADAM_LR * (m_hat / (jnp.sqrt(v_hat) + ADAM_EPS) + ADAM_WD * w_ref[...])
        mo_ref[...] = mn
        vo_ref[...] = vn

    row = pl.BlockSpec((tr, c), lambda li, i: (li * nb + i, 0))
    part_specs = [pl.BlockSpec((N_DEV, tr, c), functools.partial(lambda li, i, k: (0, jnp.where(li == k, i, 0), 0), k=k))
                  for k in range(ng)]
    return pl.pallas_call(
        body, name=name, grid=(ng, nb), in_specs=[row] + part_specs + [row, row],
        out_specs=[row] * 4, out_shape=[jax.ShapeDtypeStruct((rr, c), F32)] * 4, compiler_params=_params(("arbitrary", "arbitrary")),
    )(w, *gparts, m, v)


PACK_LANES = 128
PACK_ROWS = 4096


PACK_TILE = 8 * PACK_LANES


def _pack_rows(t):
    return -(-t.size // PACK_TILE) * 8


def _pack(vals):
    rows = []
    for t in vals:
        flat = t.reshape(-1)
        n_rows = _pack_rows(t)
        rows.append(jnp.pad(flat, (0, n_rows * PACK_LANES - flat.size)).reshape(n_rows, PACK_LANES))
    used = sum(r.shape[0] for r in rows)
    assert used <= PACK_ROWS, used
    return jnp.concatenate(rows + [jnp.zeros((PACK_ROWS - used, PACK_LANES), F32)], axis=0)


def _unpack(packed, like):
    out, off = [], 0
    for t in like:
        n_rows = _pack_rows(t)
        out.append(packed[off:off + n_rows].reshape(-1)[:t.size].reshape(t.shape))
        off += n_rows
    return out


def kernel(x, w_in, s5_lambda_re, s5_lambda_im, s5_log_step, s5_b_re, s5_b_im, s5_c_re, s5_c_im, s5_d, s5_glu_w, s5_glu_b, sgu_norm_g, sgu_norm_b, sgu_w, sgu_b, pool_w, pool_scale, dn_conv_w, dn_a_log, dn_dt_bias, dn_norm_g, w_out, ln1_g, ln1_b, w_up, w_down, ln2_g, ln2_b, loss_target, m_w_in, m_s5_lambda_re, m_s5_lambda_im, m_s5_log_step, m_s5_b_re, m_s5_b_im, m_s5_c_re, m_s5_c_im, m_s5_d, m_s5_glu_w, m_s5_glu_b, m_sgu_norm_g, m_sgu_norm_b, m_sgu_w, m_sgu_b, m_pool_w, m_pool_scale, m_dn_conv_w, m_dn_a_log, m_dn_dt_bias, m_dn_norm_g, m_w_out, m_ln1_g, m_ln1_b, m_w_up, m_w_down, m_ln2_g, m_ln2_b, v_w_in, v_s5_lambda_re, v_s5_lambda_im, v_s5_log_step, v_s5_b_re, v_s5_b_im, v_s5_c_re, v_s5_c_im, v_s5_d, v_s5_glu_w, v_s5_glu_b, v_sgu_norm_g, v_sgu_norm_b, v_sgu_w, v_sgu_b, v_pool_w, v_pool_scale, v_dn_conv_w, v_dn_a_log, v_dn_dt_bias, v_dn_norm_g, v_w_out, v_ln1_g, v_ln1_b, v_w_up, v_w_down, v_ln2_g, v_ln2_b):
    names = ("w_in", "s5_lambda_re", "s5_lambda_im", "s5_log_step", "s5_b_re", "s5_b_im", "s5_c_re", "s5_c_im", "s5_d", "s5_glu_w",
             "s5_glu_b", "sgu_norm_g", "sgu_norm_b", "sgu_w", "sgu_b", "pool_w", "pool_scale", "dn_conv_w", "dn_a_log", "dn_dt_bias",
             "dn_norm_g", "w_out", "ln1_g", "ln1_b", "w_up", "w_down", "ln2_g", "ln2_b")
    env = locals()
    w = {n: env[n] for n in names}
    m = {n: env["m_" + n] for n in names}
    v = {n: env["v_" + n] for n in names}

    wire = [{n: (w[n][i] if n == "dn_conv_w" else w[n][i].astype(MXU_DTYPE)) for n in SHARDED} for i in range(DEPTH)]
    small = [{n: w[n][i] for n in SMALL} for i in range(DEPTH)]
    ops = [_prep_small(small[i]) for i in range(DEPTH)]
    grads = [{} for _ in range(DEPTH)]
    recv = [{} for _ in range(DEPTH)]
    first = ("w_in", "s5_glu_w", "dn_conv_w", "w_out")

    def gather(layer, group):
        def take(outs):
            for n, t in zip(group, outs):
                ops[layer].update(_weight_views(n, t))
        return (lambda: [(wire[layer][n], False) for n in group]), take

    def scatter(layer, group, with_small=False):
        def make():
            sends = [(grads[layer][n], True) for n in group]
            if with_small:
                sends.append((_pack([grads[layer]["small"][n] for n in SMALL]), False))
            return sends
        def take(outs):
            recv[layer].update(dict(zip(group + (("small",) if with_small else ()), outs)))
        return make, take

    make, take = gather(0, first)
    take(_exchange(make(), "gather_first"))
    fwd_hosts = {"l0_proj": gather(0, ("w_up",)), "l0_up": gather(0, ("w_down",)), "l0_down": gather(1, first),
                 "l1_proj": gather(1, ("w_up",)), "l1_up": gather(1, ("w_down",))}
    late = ("w_in", "s5_glu_w", "dn_conv_w")
    bwd_hosts = {"l1b_dpre": scatter(1, ("w_down",)), "l1b_dx1": scatter(1, ("w_up",)), "l1b_dmixed": scatter(1, ("w_out",)),
                 "l0b_dw_down": scatter(1, late, with_small=True),
                 "l0b_dpre": scatter(0, ("w_down",)), "l0b_dx1": scatter(0, ("w_up",)), "l0b_dmixed": scatter(0, ("w_out",)),
                 "l0b_dw_main": scatter(0, ("s5_glu_w", "dn_conv_w"), with_small=True), "l0b_dx": scatter(0, ("w_in",))}
    loss, grad_x = _local_step(x[0], loss_target[0], ops, small, fwd_hosts, bwd_hosts, grads)

    g_out, d_out, m_out, v_out = {}, {}, {}, {}
    for n in SHARDED:
        shp = w[n].shape
        pad = (-shp[1]) % 8
        def rows(t):
            return jnp.pad(t, ((0, 0), (0, pad), (0, 0))).reshape(shp[0] * (shp[1] + pad), shp[2])
        res = _adamw(rows(w[n]), [recv[i][n] for i in range(DEPTH)], rows(m[n]), rows(v[n]), "adamw_" + n)
        g_out[n], d_out[n], m_out[n], v_out[n] = (t.reshape(shp[0], shp[1] + pad, shp[2])[:, :shp[1]] for t in res)
    def packed(src):
        return jnp.concatenate([_pack([src[n][i] for n in SMALL]) for i in range(DEPTH)], axis=0)
    res = _adamw(packed(w), [recv[i]["small"] for i in range(DEPTH)], packed(m), packed(v), "adamw_small")
    like = [w[n][0] for n in SMALL]
    for dst, pk in zip((g_out, d_out, m_out, v_out), res):
        per_layer = [_unpack(pk[i * PACK_ROWS:(i + 1) * PACK_ROWS], like) for i in range(DEPTH)]
        dst.update({n: jnp.stack([per_layer[i][k] for i in range(DEPTH)]) for k, n in enumerate(SMALL)})

    total = lax.psum(loss[0, 0], MESH_AXES)
    return (total, grad_x[None], *[g_out[n] for n in names], *[d_out[n] for n in names],
            *[m_out[n] for n in names], *[v_out[n] for n in names])
```

```python
import functools
import math

import jax
import jax.numpy as jnp
from jax import lax
from jax.experimental import pallas as pl
from jax.experimental.pallas import tpu as pltpu

F32 = jnp.float32
BF16 = jnp.bfloat16
MXU_DTYPE = jnp.bfloat16
HI = lax.Precision.HIGHEST

N_DEV = 8
D_MODEL = 2048
DEPTH = 2
GROUP_WIDTH = 512
S5_GROUPS, S5_CH, S5_STATE = 32, 16, 64
S5_NS = S5_GROUPS * S5_STATE
SGU_CHUNK, SGU_HEADS = 128, 8
POOL_WINDOWS = (2, 4, 8, 16)
DN_HEADS, DN_HEAD_DIM, DN_CONV, DN_CHUNK = 4, 128, 4, 64
D_FF = 4 * D_MODEL
LN_EPS, RMS_EPS, L2_EPS = 1e-5, 1e-6, 1e-6
ALPHA = (2 * DEPTH) ** 0.25
MAIN_COLS = 4096
AB_PAD = 128
ADAM_LR, ADAM_B1, ADAM_B2, ADAM_EPS, ADAM_WD, ADAM_STEP = 0.001, 0.9, 0.999, 1e-08, 0.01, 10
VMEM_LIMIT = 56 * 1024 * 1024
C_GELU = math.sqrt(2.0 / math.pi)


def _params(sem=None):
    return pltpu.CompilerParams(dimension_semantics=sem, vmem_limit_bytes=VMEM_LIMIT)


def _gelu(x):
    return 0.5 * x * (1.0 + jnp.tanh(C_GELU * (x + 0.044715 * x * x * x)))


def _gelu_grad(x):
    t = jnp.tanh(C_GELU * (x + 0.044715 * x * x * x))
    return 0.5 * (1.0 + t) + 0.5 * x * (1.0 - t * t) * C_GELU * (1.0 + 3.0 * 0.044715 * x * x)


def _sigmoid(x):
    return 1.0 / (1.0 + jnp.exp(-x))


def _silu(x):
    return x * _sigmoid(x)


def _silu_grad(x):
    s = _sigmoid(x)
    return s * (1.0 + x * (1.0 - s))


def _softplus(x):
    z = jnp.exp(-jnp.abs(x))
    small = z * (1.0 - z * (0.5 - z * (1.0 / 3.0)))
    return jnp.maximum(x, 0.0) + jnp.where(z < 1e-2, small, jnp.log(1.0 + z))


def _mx(x):
    return x.astype(MXU_DTYPE)


def _dot(a, b, dims="nn", precision=None):
    cd = {"nn": ((1,), (0,)), "nt": ((1,), (1,)), "tn": ((0,), (0,))}[dims]
    return lax.dot_general(a, b, (cd, ((), ())), preferred_element_type=F32, precision=precision)


def _mdot(a, b, dims="nn"):
    return _dot(_mx(a), _mx(b), dims)


MESH_AXES = ("x", "y", "c")
OFFSETS = [(dx, dy, dc) for dx in (0, 1) for dy in (0, 1) for dc in (0, 1)][1:]


def _me_and_peers():
    x, y, c = (lax.axis_index(a) for a in MESH_AXES)
    def flip(v, d):
        return 1 - v if d else v
    peers = [(flip(x, dx), flip(y, dy), flip(c, dc)) for dx, dy, dc in OFFSETS]
    def idx(p):
        return 4 * p[0] + 2 * p[1] + p[2]
    return idx((x, y, c)), peers, [idx(p) for p in peers]


SIBLING = OFFSETS.index((0, 0, 1))
SAME_CORE = [OFFSETS.index(f) for f in ((0, 1, 0), (1, 0, 0), (1, 1, 0))]


class _Comm:
    def __init__(self, ops):
        self.arrays = [a for a, _ in ops]
        self.scatter = [s for _, s in ops]
        self.n = n = len(ops)
        hbm = pl.BlockSpec(memory_space=pltpu.HBM)
        self.in_specs, self.out_specs = [hbm] * n, [hbm] * n
        self.out_shape = [jax.ShapeDtypeStruct(a.shape if s else (N_DEV,) + a.shape, a.dtype) for a, s in ops]
        npeer = len(OFFSETS)
        self.scratch = [pltpu.SemaphoreType.DMA((n, npeer)), pltpu.SemaphoreType.DMA((n, npeer)), pltpu.SemaphoreType.DMA((n,))]

    def _plan(self, ins, outs, sems, waiting):
        send_sems, recv_sems, local_sems = sems
        me, peers, peer_idx = _me_and_peers()

        def remote(k, d, src, dst, to):
            return pltpu.make_async_remote_copy(src_ref=src, dst_ref=dst, send_sem=send_sems.at[k, d], recv_sem=recv_sems.at[k, d],
                                                device_id=to, device_id_type=pl.DeviceIdType.MESH)
        plan = []
        for k in range(self.n):
            every = range(len(OFFSETS))
            if self.scatter[k]:
                local = pltpu.make_async_copy(ins[k].at[me], outs[k].at[me], local_sems.at[k])
                pushes = [remote(k, d, ins[k].at[peer_idx[d]], outs[k].at[me], peers[d]) for d in every]
                onward = []
            else:
                local = pltpu.make_async_copy(ins[k], outs[k].at[me], local_sems.at[k])
                pushes = [remote(k, d, ins[k], outs[k].at[me], peers[d]) for d in [SIBLING] + SAME_CORE]
                onward = SAME_CORE
            passed, arrivals = [], {}
            if waiting:
                passed = [(d, remote(k, d + 1, outs[k].at[peer_idx[d]], outs[k].at[peer_idx[d]], peers[SIBLING])) for d in onward]
                arrivals = {d: remote(k, d, outs[k].at[peer_idx[d]], outs[k].at[peer_idx[d]], peers[d]) for d in every}
            plan.append((local, pushes, passed, arrivals))
        return plan

    def start(self, ins, outs, sems):
        for local, pushes, _, _ in self._plan(ins, outs, sems, False):
            local.start()
            for cp in pushes:
                cp.start()

    def wait(self, ins, outs, sems):
        plan = self._plan(ins, outs, sems, True)
        for _, _, passed, arrivals in plan:
            for d, onward in passed:
                arrivals.pop(d).wait_recv()
                onward.start()
        for local, pushes, passed, arrivals in plan:
            for cp in arrivals.values():
                cp.wait_recv()
            for cp in pushes + [onward for _, onward in passed]:
                cp.wait_send()
            local.wait()


def _exchange(ops, name):
    cm = _Comm(ops)

    def body(*refs):
        ins, outs, sems = refs[:cm.n], refs[cm.n:2 * cm.n], refs[2 * cm.n:]
        cm.start(ins, outs, sems)
        cm.wait(ins, outs, sems)

    return pl.pallas_call(body, name=name, in_specs=cm.in_specs, out_specs=cm.out_specs, out_shape=cm.out_shape,
                          scratch_shapes=cm.scratch)(*cm.arrays)


def _matmul(a, b, *, mode, tm, tn, tk, out_dtype, name, a_fn=None, extras=(), epi=None, a_cols=None,
            b_slab=None, out_slab=None, comm=None):
    a_shape = a.shape if a_cols is None else (a.shape[0], a_cols)
    b_shape = b.shape if b_slab is None else (b.shape[1], N_DEV * b_slab)
    if mode == "nn":
        (m, k), n = a_shape, b_shape[1]
    elif mode == "nt":
        (m, k), n = a_shape, b_shape[0]
    else:
        (k, m), n = a_shape, b_shape[1]
    tm, tn, tk = min(tm, m), min(tn, n), min(tk, k)
    if b_slab is not None:
        tn, tk = (tn, min(tk, b_slab)) if mode == "nt" else (min(tn, b_slab), tk)
    assert m % tm == 0 and n % tn == 0 and k % tk == 0, (name, a.shape, b.shape, tm, tn, tk)
    gi, gj, nk = m // tm, n // tn, k // tk
    n_ex = len(extras)
    cm = _Comm(comm) if comm else None
    nc = cm.n if cm else 0

    def body(a_ref, b_ref, *rest):
        ex_refs, rest = rest[:n_ex], rest[n_ex:]
        c_ins, o_ref, c_outs, acc, sems = rest[:nc], rest[nc], rest[nc + 1:2 * nc + 1], rest[2 * nc + 1], rest[2 * nc + 2:]
        i, j, kk = pl.program_id(0), pl.program_id(1), pl.program_id(2)
        if cm:
            @pl.when((i == 0) & (j == 0) & (kk == 0))
            def _():
                cm.start(c_ins, c_outs, sems)

        av = a_ref[...]
        if a_fn is not None:
            av = a_fn(av)
        part = _dot(_mx(av), _mx(b_ref[...]), mode)

        def finish(r):
            if epi is not None:
                r = epi(r, *[e[...] for e in ex_refs])
            o_ref[...] = r.astype(out_dtype)

        if nk == 1:
            finish(part)
        else:
            @pl.when(kk == 0)
            def _():
                acc[...] = part

            @pl.when((kk > 0) & (kk < nk - 1))
            def _():
                acc[...] += part

            @pl.when(kk == nk - 1)
            def _():
                finish(acc[...] + part)

        if cm:
            @pl.when((i == gi - 1) & (j == gj - 1) & (kk == nk - 1))
            def _():
                cm.wait(c_ins, c_outs, sems)

    a_spec = pl.BlockSpec((tk, tm), lambda i, j, kk: (kk, i)) if mode == "tn" else pl.BlockSpec((tm, tk), lambda i, j, kk: (i, kk))
    if b_slab is None:
        b_spec = pl.BlockSpec((tn, tk), lambda i, j, kk: (j, kk)) if mode == "nt" else pl.BlockSpec((tk, tn), lambda i, j, kk: (kk, j))
    elif mode == "nt":
        assert b_slab % tk == 0
        b_spec = pl.BlockSpec((None, tn, tk), lambda i, j, kk: ((kk * tk) // b_slab, j, ((kk * tk) % b_slab) // tk))
    else:
        assert b_slab % tn == 0
        b_spec = pl.BlockSpec((None, tk, tn), lambda i, j, kk: ((j * tn) // b_slab, kk, ((j * tn) % b_slab) // tn))
    if out_slab is None:
        o_spec, o_shape = pl.BlockSpec((tm, tn), lambda i, j, kk: (i, j)), jax.ShapeDtypeStruct((m, n), out_dtype)
    else:
        assert out_slab % tn == 0 and n == N_DEV * out_slab
        o_spec = pl.BlockSpec((None, tm, tn), lambda i, j, kk: ((j * tn) // out_slab, i, ((j * tn) % out_slab) // tn))
        o_shape = jax.ShapeDtypeStruct((N_DEV, m, out_slab), out_dtype)
    ex_specs = [pl.BlockSpec(({None: tm, "tn": tn}.get(bs[0], bs[0]), tn if bs[1] is None else bs[1]),
                             functools.partial(lambda i, j, kk, f: f(i, j), f=im)) for (_, bs, im) in extras]
    res = pl.pallas_call(
        body,
        name=name,
        grid=(gi, gj, nk),
        in_specs=[a_spec, b_spec, *ex_specs] + (cm.in_specs if cm else []),
        out_specs=[o_spec] + (cm.out_specs if cm else []),
        out_shape=[o_shape] + (cm.out_shape if cm else []),
        scratch_shapes=[pltpu.VMEM((tm, tn) if nk > 1 else (8, 128), F32)] + (cm.scratch if cm else []),
        compiler_params=_params(("arbitrary",) * 3 if cm else ("parallel", "parallel", "arbitrary")),
    )(a, b, *[e[0] for e in extras], *(cm.arrays if cm else []))
    return (res[0], res[1:]) if cm else res[0]


def _ln_fwd(x, y, g, b, name):
    l, d = x.shape
    tl = 256

    def body(x_ref, y_ref, g_ref, b_ref, h_ref, o_ref, om_ref):
        h = ALPHA * x_ref[...] + y_ref[...]
        mu = jnp.mean(h, axis=-1, keepdims=True)
        c = h - mu
        var = jnp.mean(c * c, axis=-1, keepdims=True)
        h_ref[...] = h
        out = c * lax.rsqrt(var + LN_EPS) * g_ref[...] + b_ref[...]
        o_ref[...] = out
        om_ref[...] = out.astype(om_ref.dtype)

    row = pl.BlockSpec((tl, d), lambda i: (i, 0))
    vec = pl.BlockSpec((1, d), lambda i: (0, 0))
    return pl.pallas_call(
        body, name=name, grid=(l // tl,), in_specs=[row, row, vec, vec], out_specs=[row, row, row],
        out_shape=[jax.ShapeDtypeStruct((l, d), F32)] * 2 + [jax.ShapeDtypeStruct((l, d), MXU_DTYPE)],
        compiler_params=_params(("parallel",)),
    )(x, y, g, b)


def _ln_bwd(dout, h, g, name):
    l, d = h.shape
    tl = 256

    def body(do_ref, h_ref, g_ref, dh_ref, dhm_ref, dg_ref, db_ref):
        @pl.when(pl.program_id(0) == 0)
        def _():
            dg_ref[...] = jnp.zeros_like(dg_ref)
            db_ref[...] = jnp.zeros_like(db_ref)

        hv, do = h_ref[...], do_ref[...]
        mu = jnp.mean(hv, axis=-1, keepdims=True)
        c = hv - mu
        r = lax.rsqrt(jnp.mean(c * c, axis=-1, keepdims=True) + LN_EPS)
        xh = c * r
        dxh = do * g_ref[...]
        m1 = jnp.mean(dxh, axis=-1, keepdims=True)
        m2 = jnp.mean(dxh * xh, axis=-1, keepdims=True)
        dh = r * (dxh - m1 - xh * m2)
        dh_ref[...] = dh
        dhm_ref[...] = dh.astype(dhm_ref.dtype)
        dg_ref[...] += jnp.sum(do * xh, axis=0, keepdims=True)
        db_ref[...] += jnp.sum(do, axis=0, keepdims=True)

    row = pl.BlockSpec((tl, d), lambda i: (i, 0))
    vec = pl.BlockSpec((1, d), lambda i: (0, 0))
    return pl.pallas_call(
        body, name=name, grid=(l // tl,), in_specs=[row, row, vec], out_specs=[row, row, vec, vec],
        out_shape=[jax.ShapeDtypeStruct((l, d), F32), jax.ShapeDtypeStruct((l, d), MXU_DTYPE),
                   jax.ShapeDtypeStruct((1, d), F32), jax.ShapeDtypeStruct((1, d), F32)],
        compiler_params=_params(("arbitrary",)),
    )(dout, h, g)


def _loss_head(y, target):
    l, d = y.shape
    tl = 256

    def body(y_ref, t_ref, loss_ref, dy_ref):
        @pl.when(pl.program_id(0) == 0)
        def _():
            loss_ref[...] = jnp.zeros_like(loss_ref)

        e = y_ref[...] - t_ref[...]
        dy_ref[...] = e * (1.0 / d)
        s = jnp.sum(jnp.sum(e * e, axis=1, keepdims=True), axis=0, keepdims=True)
        loss_ref[...] += s * (0.5 / d)

    row = pl.BlockSpec((tl, d), lambda i: (i, 0))
    return pl.pallas_call(
        body, name="loss_head", grid=(l // tl,), in_specs=[row, row],
        out_specs=[pl.BlockSpec((1, 1), lambda i: (0, 0)), row],
        out_shape=[jax.ShapeDtypeStruct((1, 1), F32), jax.ShapeDtypeStruct((l, d), F32)],
        compiler_params=_params(("arbitrary",)),
    )(y, target)


def _s5_discretize(lam_re, lam_im, log_step, b_re, b_im):
    step = jnp.exp(log_step)[:, None]
    e = jnp.exp(lam_re * step)
    lbr, lbi = e * jnp.cos(lam_im * step), e * jnp.sin(lam_im * step)
    den = lam_re * lam_re + lam_im * lam_im
    qr = ((lbr - 1.0) * lam_re + lbi * lam_im) / den
    qi = (lbi * lam_re - (lbr - 1.0) * lam_im) / den
    bbr = qr[:, :, None] * b_re - qi[:, :, None] * b_im
    bbi = qr[:, :, None] * b_im + qi[:, :, None] * b_re
    return lbr, lbi, bbr, bbi


S5_TILES, S5_SLABS = 4, 8
S5_TILE_W, S5_SLAB_W = GROUP_WIDTH // S5_TILES, S5_NS // S5_TILES
S5_GPT = S5_GROUPS // S5_TILES


def _s5_compact(bbr, bbi, c_re, c_im):
    eye = jnp.eye(S5_GPT, dtype=F32)
    def bd(t):
        return jnp.einsum("tgph,gk->tghkp", t.reshape(S5_TILES, S5_GPT, S5_STATE, S5_CH), eye).reshape(S5_TILES, S5_TILE_W, S5_SLAB_W)
    def cd(t):
        return jnp.einsum("tghp,gk->tgpkh", t.reshape(S5_TILES, S5_GPT, S5_CH, S5_STATE), eye).reshape(S5_TILES, S5_SLAB_W, S5_TILE_W)
    return jnp.concatenate([bd(bbr), bd(bbi)], axis=0), jnp.concatenate([cd(c_re), -cd(c_im)], axis=0)


def _s5_uncompact_b(db):
    eye = jnp.eye(S5_GPT, dtype=F32)[None, :, None, :, None]
    def ex(t):
        d = jnp.sum(t.reshape(S5_TILES, S5_GPT, S5_CH, S5_GPT, S5_STATE) * eye, axis=3)
        return jnp.transpose(d, (0, 1, 3, 2)).reshape(S5_GROUPS, S5_STATE, S5_CH)
    return ex(db[:S5_TILES]), ex(db[S5_TILES:])


def _s5_uncompact_c(dc):
    eye = jnp.eye(S5_GPT, dtype=F32)[None, :, None, :, None]
    def ex(t):
        d = jnp.sum(t.reshape(S5_TILES, S5_GPT, S5_STATE, S5_GPT, S5_CH) * eye, axis=3)
        return jnp.transpose(d, (0, 1, 3, 2)).reshape(S5_GROUPS, S5_CH, S5_STATE)
    return ex(dc[:S5_TILES]), -ex(dc[S5_TILES:])


S5_ROWS = 256


def _s5_tile(j):
    t = j % S5_TILES
    return slice(t * S5_TILE_W, (t + 1) * S5_TILE_W)


def _s5_slab(j):
    return slice(j * S5_SLAB_W, (j + 1) * S5_SLAB_W)


def _s5_recur(src, lam_ref, carry, emit, n_rows, reverse, extra=()):
    ns = S5_NS
    lr, li = lam_ref[:, :ns], lam_ref[:, ns:]

    def step(t, c):
        row = (n_rows - 1 - t) if reverse else t
        cr, ci = c[0], c[1]
        nr = lr * cr - li * ci + src[pl.ds(row, 1), :ns]
        ni = lr * ci + li * cr + src[pl.ds(row, 1), ns:]
        return (nr, ni) + tuple(emit(row, nr, ni, cr, ci, c[2:]))

    fin = lax.fori_loop(0, n_rows, step, (carry[:, :ns], carry[:, ns:]) + tuple(extra))
    carry[:, :ns] = fin[0]
    carry[:, ns:] = fin[1]
    return fin[2:]


def _s5_fwd(proj, b, c, lam, d, name):
    l = proj.shape[0]
    tl = min(S5_ROWS, l)
    w = 2 * S5_NS

    def body(u_ref, b_ref, c_ref, lam_ref, d_ref, hs_ref, y_ref, bu, carry):
        @pl.when(pl.program_id(0) == 0)
        def _():
            carry[...] = jnp.zeros_like(carry)

        u = u_ref[...]
        um = _mx(u)
        for j in range(S5_SLABS):
            bu[:, _s5_slab(j)] = _dot(um[:, _s5_tile(j)], b_ref[j])

        def emit(row, nr, ni, cr, ci, extra):
            hs_ref[pl.ds(row, 1), :S5_NS] = nr
            hs_ref[pl.ds(row, 1), S5_NS:] = ni
            return extra

        _s5_recur(bu, lam_ref, carry, emit, tl, False)
        for t in range(S5_TILES):
            acc = _dot(_mx(hs_ref[:, _s5_slab(t)]), c_ref[t]) + _dot(_mx(hs_ref[:, _s5_slab(S5_TILES + t)]), c_ref[S5_TILES + t])
            y_ref[:, _s5_tile(t)] = acc + d_ref[:, _s5_tile(t)] * u[:, _s5_tile(t)]

    row = lambda width: pl.BlockSpec((tl, width), lambda i: (i, 0))
    full = lambda a: pl.BlockSpec(a.shape, lambda i: (0,) * a.ndim)
    return pl.pallas_call(
        body, name=name, grid=(l // tl,), in_specs=[row(GROUP_WIDTH), full(b), full(c), full(lam), full(d)],
        out_specs=[row(w), row(GROUP_WIDTH)],
        out_shape=[jax.ShapeDtypeStruct((l, w), F32), jax.ShapeDtypeStruct((l, GROUP_WIDTH), F32)],
        scratch_shapes=[pltpu.VMEM((tl, w), F32), pltpu.VMEM((1, w), F32)], compiler_params=_params(("arbitrary",)),
    )(proj, b, c, lam, d)


def _s5_bwd(dy, hs, proj, b, c, lam_conj, d, name):
    l = dy.shape[0]
    tl = min(S5_ROWS, l)
    nb = l // tl
    w = 2 * S5_NS

    def body(dy_ref, hs_ref, u_ref, b_ref, c_ref, lam_ref, d_ref, du_ref, db_ref, dc_ref, dl_ref, dh, adj, carry):
        @pl.when(pl.program_id(0) == 0)
        def _():
            carry[...] = jnp.zeros_like(carry)
            db_ref[...] = jnp.zeros_like(db_ref)
            dc_ref[...] = jnp.zeros_like(dc_ref)
            dl_ref[...] = jnp.zeros_like(dl_ref)

        dyv = dy_ref[...]
        dym, um = _mx(dyv), _mx(u_ref[...])
        for j in range(S5_SLABS):
            dh[:, _s5_slab(j)] = _dot(dym[:, _s5_tile(j)], c_ref[j], "nt")

        def emit(row, nr, ni, cr, ci, extra):
            adj[pl.ds(row, 1), :S5_NS] = nr
            adj[pl.ds(row, 1), S5_NS:] = ni
            hr, hi = hs_ref[pl.ds(row, 1), :S5_NS], hs_ref[pl.ds(row, 1), S5_NS:]
            return extra[0] + cr * hr + ci * hi, extra[1] + ci * hr - cr * hi

        dl = _s5_recur(dh, lam_ref, carry, emit, tl, True, extra=(dl_ref[:, :S5_NS], dl_ref[:, S5_NS:]))
        dl_ref[:, :S5_NS] = dl[0]
        dl_ref[:, S5_NS:] = dl[1]
        for t in range(S5_TILES):
            acc = (_dot(_mx(adj[:, _s5_slab(t)]), b_ref[t], "nt")
                   + _dot(_mx(adj[:, _s5_slab(S5_TILES + t)]), b_ref[S5_TILES + t], "nt"))
            du_ref[:, _s5_tile(t)] = (acc + d_ref[:, _s5_tile(t)] * dyv[:, _s5_tile(t)]).astype(du_ref.dtype)
        for j in range(S5_SLABS):
            dc_ref[j] += _dot(_mx(hs_ref[:, _s5_slab(j)]), dym[:, _s5_tile(j)], "tn")
            db_ref[j] += _dot(um[:, _s5_tile(j)], _mx(adj[:, _s5_slab(j)]), "tn")

    row = lambda width: pl.BlockSpec((tl, width), lambda i: (nb - 1 - i, 0))
    full = lambda a: pl.BlockSpec(a.shape, lambda i: (0,) * a.ndim)
    acc3 = lambda shape: pl.BlockSpec(shape, lambda i: (0, 0, 0))
    return pl.pallas_call(
        body, name=name, grid=(nb,),
        in_specs=[row(GROUP_WIDTH), row(w), row(GROUP_WIDTH), full(b), full(c), full(lam_conj), full(d)],
        out_specs=[row(GROUP_WIDTH), acc3(b.shape), acc3(c.shape), pl.BlockSpec((1, w), lambda i: (0, 0))],
        out_shape=[jax.ShapeDtypeStruct((l, GROUP_WIDTH), BF16), jax.ShapeDtypeStruct(b.shape, F32),
                   jax.ShapeDtypeStruct(c.shape, F32), jax.ShapeDtypeStruct((1, w), F32)],
        scratch_shapes=[pltpu.VMEM((tl, w), F32), pltpu.VMEM((tl, w), F32), pltpu.VMEM((1, w), F32)],
        compiler_params=_params(("arbitrary",)),
    )(dy, hs, proj, b, c, lam_conj, d)


def _s5_glu_fwd(y, glu_w, glu_b, name):
    l, d = y.shape
    tl = min(512, l)

    def body(y_ref, w_ref, b_ref, o_ref):
        yg = _gelu(y_ref[...])
        z = _mdot(yg, w_ref[...]) + b_ref[...]
        o_ref[...] = (yg * _sigmoid(z)).astype(o_ref.dtype)

    return pl.pallas_call(
        body, name=name, grid=(l // tl,),
        in_specs=[pl.BlockSpec((tl, d), lambda i: (i, 0)), pl.BlockSpec((d, d), lambda i: (0, 0)), pl.BlockSpec((1, d), lambda i: (0, 0))],
        out_specs=pl.BlockSpec((tl, d), lambda i: (i, 0)), out_shape=jax.ShapeDtypeStruct((l, d), BF16),
        compiler_params=_params(("parallel",)),
    )(y, glu_w, glu_b)


def _s5_glu_bwd(dmixed, y, proj, glu_w, glu_b, name):
    l, d = y.shape
    tl = min(512, l)

    def body(do_ref, y_ref, u_ref, w_ref, b_ref, dy_ref, dz_ref, yg_ref, db_ref, dd_ref):
        @pl.when(pl.program_id(0) == 0)
        def _():
            db_ref[...] = jnp.zeros_like(db_ref)
            dd_ref[...] = jnp.zeros_like(dd_ref)

        yv, do = y_ref[...], do_ref[...]
        yg = _gelu(yv)
        gate = _sigmoid(_mdot(yg, w_ref[...]) + b_ref[...])
        dz = do * yg * gate * (1.0 - gate)
        dyg = do * gate + _mdot(dz, w_ref[...], "nt")
        dy = dyg * _gelu_grad(yv)
        dy_ref[...] = dy
        dz_ref[...] = dz.astype(dz_ref.dtype)
        yg_ref[...] = yg.astype(yg_ref.dtype)
        db_ref[...] += jnp.sum(dz, axis=0, keepdims=True)
        dd_ref[...] += jnp.sum(dy * u_ref[...], axis=0, keepdims=True)

    row = pl.BlockSpec((tl, d), lambda i: (i, 0))
    vec = pl.BlockSpec((1, d), lambda i: (0, 0))
    return pl.pallas_call(
        body, name=name, grid=(l // tl,),
        in_specs=[row, row, row, pl.BlockSpec((d, d), lambda i: (0, 0)), vec],
        out_specs=[row, row, row, vec, vec],
        out_shape=[jax.ShapeDtypeStruct((l, d), F32), jax.ShapeDtypeStruct((l, d), BF16), jax.ShapeDtypeStruct((l, d), BF16),
                   jax.ShapeDtypeStruct((1, d), F32), jax.ShapeDtypeStruct((1, d), F32)],
        compiler_params=_params(("arbitrary",)),
    )(dmixed, y, proj, glu_w, glu_b)


def _sgu_pair(w_ref, x, j, dims):
    lo = lax.broadcasted_iota(jnp.int32, x.shape, 1) < (GROUP_WIDTH // SGU_HEADS)
    xb = _mx(x)
    r0 = _dot(w_ref[2 * j], xb, dims)
    r1 = _dot(w_ref[2 * j + 1], xb, dims)
    return jnp.where(lo, r0, r1)


def _sgu_norm(v, g, b):
    mu = jnp.mean(v, axis=-1, keepdims=True)
    c = v - mu
    r = lax.rsqrt(jnp.mean(c * c, axis=-1, keepdims=True) + LN_EPS)
    return c * r, r


def _sgu_fwd(proj, norm_g, norm_b, wm, bfull, name):
    l = proj.shape[0]
    tl = 256
    gw = GROUP_WIDTH

    def body(zu_ref, zv_ref, g_ref, b_ref, w_ref, bf_ref, o_ref):
        for c in range(tl // SGU_CHUNK):
            rows = slice(c * SGU_CHUNK, (c + 1) * SGU_CHUNK)
            u = _gelu(zu_ref[rows, :])
            vh, _ = _sgu_norm(_gelu(zv_ref[rows, :]), None, None)
            vn = vh * g_ref[...] + b_ref[...]
            for j in range(gw // 128):
                cols = slice(j * 128, (j + 1) * 128)
                mixed = _sgu_pair(w_ref, vn[:, cols], j, "nn") + bf_ref[:, cols]
                o_ref[rows, cols] = (u[:, cols] * mixed).astype(o_ref.dtype)

    vec = pl.BlockSpec((1, gw), lambda i: (0, 0))
    return pl.pallas_call(
        body, name=name, grid=(l // tl,),
        in_specs=[pl.BlockSpec((tl, gw), lambda i: (i, 1)), pl.BlockSpec((tl, gw), lambda i: (i, 2)), vec, vec,
                  pl.BlockSpec((SGU_HEADS, SGU_CHUNK, SGU_CHUNK), lambda i: (0, 0, 0)), pl.BlockSpec((SGU_CHUNK, gw), lambda i: (0, 0))],
        out_specs=pl.BlockSpec((tl, gw), lambda i: (i, 0)), out_shape=jax.ShapeDtypeStruct((l, gw), BF16),
        compiler_params=_params(("parallel",)),
    )(proj, proj, norm_g, norm_b, wm, bfull)


def _sgu_bwd(dmixed, proj, norm_g, norm_b, wm, bfull, name):
    l = proj.shape[0]
    tl = 256
    gw = GROUP_WIDTH
    hd = gw // SGU_HEADS

    def body(do_ref, zu_ref, zv_ref, g_ref, b_ref, w_ref, bf_ref, dzu_ref, dzv_ref, dw_ref, dbf_ref, dg_ref, dnb_ref):
        @pl.when(pl.program_id(0) == 0)
        def _():
            dw_ref[...] = jnp.zeros_like(dw_ref)
            dbf_ref[...] = jnp.zeros_like(dbf_ref)
            dg_ref[...] = jnp.zeros_like(dg_ref)
            dnb_ref[...] = jnp.zeros_like(dnb_ref)

        for c in range(tl // SGU_CHUNK):
            rows = slice(c * SGU_CHUNK, (c + 1) * SGU_CHUNK)
            zu, zv, do = zu_ref[rows, :], zv_ref[rows, :], do_ref[rows, :]
            u = _gelu(zu)
            vh, r = _sgu_norm(_gelu(zv), None, None)
            vn = vh * g_ref[...] + b_ref[...]
            dvn_parts, mixed_parts = [], []
            for j in range(gw // 128):
                cols = slice(j * 128, (j + 1) * 128)
                vb = vn[:, cols]
                mixed_parts.append(_sgu_pair(w_ref, vb, j, "nn") + bf_ref[:, cols])
                dm = do[:, cols] * u[:, cols]
                dvn_parts.append(_sgu_pair(w_ref, dm, j, "tn"))
                lo = lax.broadcasted_iota(jnp.int32, dm.shape, 1) < hd
                dw_ref[2 * j] += _mdot(jnp.where(lo, dm, 0.0), vb, "nt")
                dw_ref[2 * j + 1] += _mdot(jnp.where(lo, 0.0, dm), vb, "nt")
                dbf_ref[:, cols] += dm
            mixed = jnp.concatenate(mixed_parts, axis=1)
            dvn = jnp.concatenate(dvn_parts, axis=1)
            dzu_ref[rows, :] = (do * mixed * _gelu_grad(zu)).astype(dzu_ref.dtype)
            dg_ref[...] += jnp.sum(dvn * vh, axis=0, keepdims=True)
            dnb_ref[...] += jnp.sum(dvn, axis=0, keepdims=True)
            dvh = dvn * g_ref[...]
            m1 = jnp.mean(dvh, axis=-1, keepdims=True)
            m2 = jnp.mean(dvh * vh, axis=-1, keepdims=True)
            dv = r * (dvh - m1 - vh * m2)
            dzv_ref[rows, :] = (dv * _gelu_grad(zv)).astype(dzv_ref.dtype)

    vec = pl.BlockSpec((1, gw), lambda i: (0, 0))
    row = pl.BlockSpec((tl, gw), lambda i: (i, 0))
    wspec = pl.BlockSpec((SGU_HEADS, SGU_CHUNK, SGU_CHUNK), lambda i: (0, 0, 0))
    bspec = pl.BlockSpec((SGU_CHUNK, gw), lambda i: (0, 0))
    return pl.pallas_call(
        body, name=name, grid=(l // tl,),
        in_specs=[pl.BlockSpec((tl, gw), lambda i: (i, 1)), pl.BlockSpec((tl, gw), lambda i: (i, 1)), pl.BlockSpec((tl, gw), lambda i: (i, 2)),
                  vec, vec, wspec, bspec],
        out_specs=[row, row, wspec, bspec, vec, vec],
        out_shape=[jax.ShapeDtypeStruct((l, gw), BF16), jax.ShapeDtypeStruct((l, gw), BF16),
                   jax.ShapeDtypeStruct((SGU_HEADS, SGU_CHUNK, SGU_CHUNK), F32), jax.ShapeDtypeStruct((SGU_CHUNK, gw), F32),
                   jax.ShapeDtypeStruct((1, gw), F32), jax.ShapeDtypeStruct((1, gw), F32)],
        compiler_params=_params(("arbitrary",)),
    )(dmixed, proj, proj, norm_g, norm_b, wm, bfull)


HALO = 16


def _window_sums(ext, n_rows, forward):
    def sh(x, k):
        return pltpu.roll(x, (n_rows - k) if forward else k, axis=0)
    s2 = ext + sh(ext, 1)
    s4 = s2 + sh(s2, 2)
    s8 = s4 + sh(s4, 4)
    s16 = s8 + sh(s8, 8)
    return (s2, s4, s8, s16)


def _pool_fwd(proj, pool_w, scale, name):
    l = proj.shape[0]
    tl = 256
    gw = GROUP_WIDTH
    pg = gw // len(POOL_WINDOWS)

    def body(x_ref, halo_ref, w_ref, s_ref, o_ref, p_ref):
        i = pl.program_id(0)
        x = x_ref[...]
        halo = jnp.where(i > 0, halo_ref[...], 0.0)
        ext = jnp.concatenate([halo, x], axis=0)
        sums = _window_sums(ext, tl + HALO, False)
        t = i * tl + lax.broadcasted_iota(jnp.int32, (tl, pg), 0)
        for gi, win in enumerate(POOL_WINDOWS):
            cols = slice(gi * pg, (gi + 1) * pg)
            cnt = jnp.minimum(t + 1, win).astype(F32)
            pooled = sums[gi][HALO:, cols] / cnt - x[:, cols]
            p_ref[:, cols] = pooled
            o_ref[:, cols] = (_mdot(pooled, w_ref[gi]) * s_ref[:, cols]).astype(o_ref.dtype)

    row = pl.BlockSpec((tl, gw), lambda i: (i, 0))
    return pl.pallas_call(
        body, name=name, grid=(l // tl,),
        in_specs=[pl.BlockSpec((tl, gw), lambda i: (i, 3)),
                  pl.BlockSpec((HALO, gw), lambda i: (jnp.maximum(i * (tl // HALO) - 1, 0), 3)),
                  pl.BlockSpec((len(POOL_WINDOWS), pg, pg), lambda i: (0, 0, 0)), pl.BlockSpec((1, gw), lambda i: (0, 0))],
        out_specs=[row, row], out_shape=[jax.ShapeDtypeStruct((l, gw), BF16), jax.ShapeDtypeStruct((l, gw), F32)],
        compiler_params=_params(("parallel",)),
    )(proj, proj, pool_w, scale)


def _pool_bwd_map(dmixed, pooled, pool_w, scale, name):
    l, gw = pooled.shape
    tl = 256
    ng = len(POOL_WINDOWS)
    pg = gw // ng

    def body(do_ref, p_ref, w_ref, s_ref, dp_ref, dw_ref, ds_ref):
        @pl.when(pl.program_id(0) == 0)
        def _():
            dw_ref[...] = jnp.zeros_like(dw_ref)
            ds_ref[...] = jnp.zeros_like(ds_ref)

        for gi in range(ng):
            cols = slice(gi * pg, (gi + 1) * pg)
            do, pooled_g = do_ref[:, cols], p_ref[:, cols]
            mixed = _mdot(pooled_g, w_ref[gi])
            ds_ref[:, cols] += jnp.sum(do * mixed, axis=0, keepdims=True)
            dm = do * s_ref[:, cols]
            dw_ref[gi] += _mdot(pooled_g, dm, "tn")
            dp_ref[:, cols] = _mdot(dm, w_ref[gi], "nt")

    row = pl.BlockSpec((tl, gw), lambda i: (i, 0))
    wspec = pl.BlockSpec((ng, pg, pg), lambda i: (0, 0, 0))
    vec = pl.BlockSpec((1, gw), lambda i: (0, 0))
    return pl.pallas_call(
        body, name=name, grid=(l // tl,),
        in_specs=[pl.BlockSpec((tl, gw), lambda i: (i, 2)), row, wspec, vec], out_specs=[row, wspec, vec],
        out_shape=[jax.ShapeDtypeStruct((l, gw), F32), jax.ShapeDtypeStruct((ng, pg, pg), F32), jax.ShapeDtypeStruct((1, gw), F32)],
        compiler_params=_params(("arbitrary",)),
    )(dmixed, pooled, pool_w, scale)


def _pool_bwd_window(dpooled, name):
    l, gw = dpooled.shape
    tl = 256
    nb = l // tl
    pg = gw // len(POOL_WINDOWS)

    def body(d_ref, halo_ref, o_ref):
        i = pl.program_id(0)
        d = d_ref[...]
        halo = jnp.where(i < nb - 1, halo_ref[...], 0.0)
        ext = jnp.concatenate([d, halo], axis=0)
        t = i * tl + lax.broadcasted_iota(jnp.int32, (tl + HALO, pg), 0)
        for gi, win in enumerate(POOL_WINDOWS):
            cols = slice(gi * pg, (gi + 1) * pg)
            cnt = jnp.minimum(t + 1, win).astype(F32)
            sums = _window_sums(ext[:, cols] / cnt, tl + HALO, True)
            o_ref[:, cols] = (sums[gi][:tl, :] - d[:, cols]).astype(o_ref.dtype)

    row = pl.BlockSpec((tl, gw), lambda i: (i, 0))
    return pl.pallas_call(
        body, name=name, grid=(nb,),
        in_specs=[row, pl.BlockSpec((HALO, gw), lambda i: (jnp.minimum((i + 1) * (tl // HALO), l // HALO - 1), 0))],
        out_specs=row, out_shape=jax.ShapeDtypeStruct((l, gw), BF16), compiler_params=_params(("parallel",)),
    )(dpooled, dpooled)


CONV_HALO = 8
QKV_BLK = 4


def _head_sums(x):
    parts = []
    for hd in range(DN_HEADS):
        s = jnp.sum(x[:, hd * DN_HEAD_DIM:(hd + 1) * DN_HEAD_DIM], axis=-1, keepdims=True)
        parts.append(jnp.broadcast_to(s, (x.shape[0], DN_HEAD_DIM)))
    return jnp.concatenate(parts, axis=1)


def _gdn_pre_fwd(proj, proj_ab, conv_w, a_log, dt_bias, name):
    l = proj.shape[0]
    tl = 256
    gw = GROUP_WIDTH

    def body(xq, xk, xv, hq, hk, hv, w_ref, ab_ref, al_ref, dt_ref, qn_ref, kn_ref, v_ref, cq_ref, ck_ref, cv_ref, gb_ref):
        i = pl.program_id(0)
        for p, (x_ref, h_ref, c_ref) in enumerate(((xq, hq, cq_ref), (xk, hk, ck_ref), (xv, hv, cv_ref))):
            ext = jnp.concatenate([jnp.where(i > 0, h_ref[...], 0.0), x_ref[...]], axis=0)
            conv = jnp.zeros((tl, gw), F32)
            for j in range(DN_CONV):
                k = DN_CONV - 1 - j
                shifted = ext if k == 0 else pltpu.roll(ext, k, axis=0)
                conv = conv + shifted[CONV_HALO:, :] * w_ref[j:j + 1, p * gw:(p + 1) * gw]
            c_ref[...] = conv
            s = _silu(conv)
            if p == 2:
                v_ref[...] = s
            else:
                r = lax.rsqrt(_head_sums(s * s) + L2_EPS)
                (qn_ref if p == 0 else kn_ref)[...] = s * r * (DN_HEAD_DIM ** -0.5 if p == 0 else 1.0)
        ab = ab_ref[...]
        lane = lax.broadcasted_iota(jnp.int32, ab.shape, 1)
        g = -jnp.exp(al_ref[...]) * _softplus(ab + dt_ref[...])
        gb_ref[...] = jnp.where(lane < DN_HEADS, g, _sigmoid(ab))

    def xs(b):
        return pl.BlockSpec((tl, gw), lambda i: (i, b))

    def hs(b):
        return pl.BlockSpec((CONV_HALO, gw), lambda i: (jnp.maximum(i * (tl // CONV_HALO) - 1, 0), b))

    row = pl.BlockSpec((tl, gw), lambda i: (i, 0))
    abrow = pl.BlockSpec((tl, AB_PAD), lambda i: (i, 0))
    abvec = pl.BlockSpec((1, AB_PAD), lambda i: (0, 0))
    return pl.pallas_call(
        body, name=name, grid=(l // tl,),
        in_specs=[xs(QKV_BLK), xs(QKV_BLK + 1), xs(QKV_BLK + 2), hs(QKV_BLK), hs(QKV_BLK + 1), hs(QKV_BLK + 2),
                  pl.BlockSpec((DN_CONV, 3 * gw), lambda i: (0, 0)), abrow, abvec, abvec],
        out_specs=[row] * 6 + [abrow],
        out_shape=[jax.ShapeDtypeStruct((l, gw), F32)] * 6 + [jax.ShapeDtypeStruct((l, AB_PAD), F32)],
        compiler_params=_params(("parallel",)),
    )(proj, proj, proj, proj, proj, proj, conv_w, proj_ab, a_log, dt_bias)


def _gdn_pre_bwd(dq, dk, dv, cq, ck, cv, dgb, gb, proj_ab, a_log, dt_bias, name):
    l, gw = cq.shape
    tl = 256

    def body(dq_ref, dk_ref, dv_ref, cq_ref, ck_ref, cv_ref, dgb_ref, gb_ref, ab_ref, al_ref, dt_ref,
             dcq_ref, dck_ref, dcv_ref, dab_ref, dal_ref, ddt_ref):
        @pl.when(pl.program_id(0) == 0)
        def _():
            dal_ref[...] = jnp.zeros_like(dal_ref)
            ddt_ref[...] = jnp.zeros_like(ddt_ref)

        for p, (d_ref, c_ref, o_ref) in enumerate(((dq_ref, cq_ref, dcq_ref), (dk_ref, ck_ref, dck_ref), (dv_ref, cv_ref, dcv_ref))):
            c, d = c_ref[...], d_ref[...]
            if p == 2:
                ds = d
            else:
                s = _silu(c)
                r = lax.rsqrt(_head_sums(s * s) + L2_EPS)
                ds = (DN_HEAD_DIM ** -0.5 if p == 0 else 1.0) * r * (d - s * r * r * _head_sums(d * s))
            o_ref[...] = ds * _silu_grad(c)
        ab, dgb_v, gb_v = ab_ref[...], dgb_ref[...], gb_ref[...]
        lane = lax.broadcasted_iota(jnp.int32, ab.shape, 1)
        is_g = lane < DN_HEADS
        dpre = dgb_v * (-jnp.exp(al_ref[...])) * _sigmoid(ab + dt_ref[...])
        dab_ref[...] = jnp.where(is_g, dpre, dgb_v * gb_v * (1.0 - gb_v)).astype(dab_ref.dtype)
        dal_ref[...] += jnp.sum(jnp.where(is_g, dgb_v * gb_v, 0.0), axis=0, keepdims=True)
        ddt_ref[...] += jnp.sum(jnp.where(is_g, dpre, 0.0), axis=0, keepdims=True)

    row = pl.BlockSpec((tl, gw), lambda i: (i, 0))
    abrow = pl.BlockSpec((tl, AB_PAD), lambda i: (i, 0))
    abvec = pl.BlockSpec((1, AB_PAD), lambda i: (0, 0))
    return pl.pallas_call(
        body, name=name, grid=(l // tl,),
        in_specs=[row] * 6 + [abrow, abrow, abrow, abvec, abvec],
        out_specs=[row, row, row, abrow, abvec, abvec],
        out_shape=[jax.ShapeDtypeStruct((l, gw), F32)] * 3 + [jax.ShapeDtypeStruct((l, AB_PAD), BF16),
                   jax.ShapeDtypeStruct((1, AB_PAD), F32), jax.ShapeDtypeStruct((1, AB_PAD), F32)],
        compiler_params=_params(("arbitrary",)),
    )(dq, dk, dv, cq, ck, cv, dgb, gb, proj_ab, a_log, dt_bias)


def _conv_bwd(dc, proj, col_blk, w_part, name):
    l, gw = dc.shape
    tl = 256
    nb = l // tl

    def body(dc_ref, halo_ref, x_ref, w_ref, dx_ref, dw_ref):
        i = pl.program_id(0)

        @pl.when(i == 0)
        def _():
            dw_ref[...] = jnp.zeros_like(dw_ref)

        ext = jnp.concatenate([dc_ref[...], jnp.where(i < nb - 1, halo_ref[...], 0.0)], axis=0)
        x = x_ref[...]
        dx = jnp.zeros((tl, gw), F32)
        rid = lax.broadcasted_iota(jnp.int32, (8, gw), 0)
        dw = jnp.zeros((8, gw), F32)
        for j in range(DN_CONV):
            k = DN_CONV - 1 - j
            shifted = (ext if k == 0 else pltpu.roll(ext, tl + CONV_HALO - k, axis=0))[:tl, :]
            dx = dx + shifted * w_ref[j:j + 1, :]
            dw = dw + jnp.where(rid == j, jnp.sum(x * shifted, axis=0, keepdims=True), 0.0)
        dx_ref[...] = dx.astype(dx_ref.dtype)
        dw_ref[...] += dw

    row = pl.BlockSpec((tl, gw), lambda i: (i, 0))
    return pl.pallas_call(
        body, name=name, grid=(nb,),
        in_specs=[row, pl.BlockSpec((CONV_HALO, gw), lambda i: (jnp.minimum((i + 1) * (tl // CONV_HALO), l // CONV_HALO - 1), 0)),
                  pl.BlockSpec((tl, gw), lambda i: (i, col_blk)), pl.BlockSpec((DN_CONV, gw), lambda i: (0, 0))],
        out_specs=[row, pl.BlockSpec((8, gw), lambda i: (0, 0))],
        out_shape=[jax.ShapeDtypeStruct((l, gw), BF16), jax.ShapeDtypeStruct((8, gw), F32)],
        compiler_params=_params(("arbitrary",)),
    )(dc, dc, proj, w_part)


TERMS_CHUNKS = 4


def _bdot(a, b, dims="nn", precision=None):
    cd = {"nn": ((2,), (1,)), "nt": ((2,), (2,)), "tn": ((1,), (1,))}[dims]
    return lax.dot_general(a, b, (cd, ((0,), (0,))), preferred_element_type=F32, precision=precision)


def _bmdot(a, b, dims="nn"):
    return _bdot(_mx(a), _mx(b), dims)


def _wy_terms(q, k, v, gcol, beta, t=None):
    c = DN_CHUNK
    ii = lax.broadcasted_iota(jnp.int32, (1, c, c), 1)
    jj = lax.broadcasted_iota(jnp.int32, (1, c, c), 2)
    tril, strict = ii >= jj, ii > jj
    grow = jnp.sum(jnp.where(ii == jj, gcol, 0.0), axis=1, keepdims=True)
    gc_col = jnp.sum(jnp.where(tril, grow, 0.0), axis=2, keepdims=True)
    gc_row = jnp.sum(jnp.where(ii <= jj, gcol, 0.0), axis=1, keepdims=True)
    dec = jnp.exp(jnp.where(tril, gc_col - gc_row, -1e30))
    kb, vb = k * beta, v * beta
    kk = _bmdot(kb, k, "nt")
    if t is None:
        a = jnp.where(strict, kk * dec, 0.0)
        d = jnp.where((ii >> 3) == (jj >> 3), a, 0.0)
        t = jnp.where(ii == jj, 1.0, 0.0) - d
        p = _bdot(d, d, precision=HI)
        t = t + _bdot(t, p, precision=HI)
        t = t + _bdot(t, _bdot(p, p, precision=HI), precision=HI)
        for sh in (3, 4, 5):
            below = ((ii >> (sh + 1)) == (jj >> (sh + 1))) & ((ii >> sh) > (jj >> sh))
            t = t - _bdot(t, _bdot(jnp.where(below, a, 0.0), t, precision=HI), precision=HI)
    eg = jnp.exp(gc_col)
    gc_last = gc_col[:, c - 1:c, :]
    kbg = kb * eg
    qk0 = _bmdot(q, k, "nt")
    e2 = jnp.exp(gc_last - gc_col)
    return dict(ii=ii, jj=jj, tril=tril, strict=strict, dec=dec, kb=kb, vb=vb, kk=kk, t=t, eg=eg, kbg=kbg,
                qk0=qk0, qk=jnp.where(tril, qk0 * dec, 0.0), qg=q * eg, e2=e2, kt=k * e2, gl=jnp.exp(gc_last))


def _to_heads(x, g):
    return jnp.concatenate([x[:, h * DN_HEAD_DIM:(h + 1) * DN_HEAD_DIM].reshape(g, DN_CHUNK, DN_HEAD_DIM)
                            for h in range(DN_HEADS)], axis=0)


def _from_heads(t, ref, g):
    for h in range(DN_HEADS):
        ref[:, h * DN_HEAD_DIM:(h + 1) * DN_HEAD_DIM] = t[h * g:(h + 1) * g].reshape(g * DN_CHUNK, DN_HEAD_DIM).astype(ref.dtype)


def _head_columns(gbv, first_lane, g):
    lane = lax.broadcasted_iota(jnp.int32, gbv.shape, 1)
    return jnp.concatenate([jnp.sum(jnp.where(lane == first_lane + h, gbv, 0.0), axis=1, keepdims=True).reshape(g, DN_CHUNK, 1)
                            for h in range(DN_HEADS)], axis=0)


def _gdn_terms_fwd(qn, kn, v, gb, name):
    l = qn.shape[0]
    n_chunks = l // DN_CHUNK
    g = min(TERMS_CHUNKS, n_chunks)
    rows, c, nh = g * DN_CHUNK, DN_CHUNK, DN_HEADS

    def body(q_ref, k_ref, v_ref, gb_ref, u_ref, w_ref, qg_ref, kt_ref, qk_ref, t_ref, gl_ref):
        gbv = gb_ref[...]
        x = _wy_terms(_to_heads(q_ref[...], g), _to_heads(k_ref[...], g), _to_heads(v_ref[...], g),
                      _head_columns(gbv, 0, g), _head_columns(gbv, nh, g))
        _from_heads(_bmdot(x["t"], x["vb"]), u_ref, g)
        _from_heads(_bmdot(x["t"], x["kbg"]), w_ref, g)
        _from_heads(x["qg"], qg_ref, g)
        _from_heads(x["kt"], kt_ref, g)
        for h in range(nh):
            qk_ref[:, h] = x["qk"][h * g:(h + 1) * g]
            t_ref[:, h] = x["t"][h * g:(h + 1) * g]
            gl_ref[:, h] = jnp.broadcast_to(x["gl"][h * g:(h + 1) * g], (g, 1, 128))

    row = pl.BlockSpec((rows, GROUP_WIDTH), lambda i: (i, 0))
    sq = pl.BlockSpec((g, nh, c, c), lambda i: (i, 0, 0, 0))
    glb = pl.BlockSpec((g, nh, 1, 128), lambda i: (i, 0, 0, 0))
    return pl.pallas_call(
        body, name=name, grid=(n_chunks // g,), in_specs=[row, row, row, pl.BlockSpec((rows, AB_PAD), lambda i: (i, 0))],
        out_specs=[row] * 4 + [sq, sq, glb],
        out_shape=[jax.ShapeDtypeStruct((l, GROUP_WIDTH), F32)] * 4 + [jax.ShapeDtypeStruct((n_chunks, nh, c, c), F32)] * 2
        + [jax.ShapeDtypeStruct((n_chunks, nh, 1, 128), F32)],
        compiler_params=_params(("parallel",)),
    )(qn, kn, v, gb)


REC_CHUNKS = 4


def _rec_specs(n_chunks, reverse):
    c, hd, nh = DN_CHUNK, DN_HEAD_DIM, DN_HEADS
    g = min(REC_CHUNKS, n_chunks)
    nb = n_chunks // g
    blk_of = (lambda n: nb - 1 - n) if reverse else (lambda n: n)
    return g, nb, (pl.BlockSpec((g * c, GROUP_WIDTH), lambda n: (blk_of(n), 0)), pl.BlockSpec((g, nh, c, c), lambda n: (blk_of(n), 0, 0, 0)),
                   pl.BlockSpec((g, nh, 1, 128), lambda n: (blk_of(n), 0, 0, 0)), pl.BlockSpec((g, nh, hd, hd), lambda n: (blk_of(n), 0, 0, 0)))


def _gdn_rec_fwd(u, w, qg, kt, qk, gl, name):
    l = u.shape[0]
    n_chunks = l // DN_CHUNK
    hd, nh, c = DN_HEAD_DIM, DN_HEADS, DN_CHUNK
    g, nb, (blk, sq, glb, st) = _rec_specs(n_chunks, False)
    heads = range(nh)

    def body(u_ref, w_ref, qg_ref, kt_ref, qk_ref, gl_ref, o_ref, vn_ref, s_ref, state):
        @pl.when(pl.program_id(0) == 0)
        def _():
            state[...] = jnp.zeros_like(state)

        def cols(h):
            return slice(h * hd, (h + 1) * hd)
        for ci in range(g):
            rows = slice(ci * c, (ci + 1) * c)
            s = [state[h] for h in heads]
            ws = [_mdot(w_ref[rows, cols(h)], s[h]) for h in heads]
            vn = [u_ref[rows, cols(h)] - ws[h] for h in heads]
            kv = [_mdot(kt_ref[rows, cols(h)], vn[h], "tn") for h in heads]
            for h in heads:
                state[h] = s[h] * gl_ref[ci, h] + kv[h]
            o1 = [_mdot(qg_ref[rows, cols(h)], s[h]) for h in heads]
            o2 = [_mdot(qk_ref[ci, h], vn[h]) for h in heads]
            for h in heads:
                s_ref[ci, h] = s[h]
                o_ref[rows, cols(h)] = o1[h] + o2[h]
                vn_ref[rows, cols(h)] = vn[h]

    return pl.pallas_call(
        body, name=name, grid=(nb,), in_specs=[blk, blk, blk, blk, sq, glb], out_specs=[blk, blk, st],
        out_shape=[jax.ShapeDtypeStruct((l, GROUP_WIDTH), F32)] * 2 + [jax.ShapeDtypeStruct((n_chunks, nh, hd, hd), F32)],
        scratch_shapes=[pltpu.VMEM((nh, hd, hd), F32)], compiler_params=_params(("arbitrary",)),
    )(u, w, qg, kt, qk, gl)


def _gdn_rec_bwd(do, w, qg, kt, vn, qk, gl, states, name):
    l = do.shape[0]
    n_chunks = l // DN_CHUNK
    hd, nh, c = DN_HEAD_DIM, DN_HEADS, DN_CHUNK
    g, nb, (blk, sq, glb, st) = _rec_specs(n_chunks, True)
    heads = range(nh)

    def body(do_ref, w_ref, qg_ref, kt_ref, vn_ref, qk_ref, gl_ref, s_ref, dvn_ref, dw_ref, dkt_ref, dqg_ref, dqk_ref, dgl_ref, dstate):
        @pl.when(pl.program_id(0) == 0)
        def _():
            dstate[...] = jnp.zeros_like(dstate)

        def cols(h):
            return slice(h * hd, (h + 1) * hd)
        tril = lax.broadcasted_iota(jnp.int32, (c, c), 0) >= lax.broadcasted_iota(jnp.int32, (c, c), 1)
        for ci in reversed(range(g)):
            rows = slice(ci * c, (ci + 1) * c)
            ds = [dstate[h] for h in heads]
            dout = [do_ref[rows, cols(h)] for h in heads]
            a1 = [_mdot(qk_ref[ci, h], dout[h], "tn") for h in heads]
            a2 = [_mdot(kt_ref[rows, cols(h)], ds[h]) for h in heads]
            dvn = [a1[h] + a2[h] for h in heads]
            b1 = [_mdot(qg_ref[rows, cols(h)], dout[h], "tn") for h in heads]
            b2 = [_mdot(w_ref[rows, cols(h)], dvn[h], "tn") for h in heads]
            for h in heads:
                dstate[h] = b1[h] + gl_ref[ci, h] * ds[h] - b2[h]
            for h in heads:
                s, vnew = s_ref[ci, h], vn_ref[rows, cols(h)]
                dvn_ref[rows, cols(h)] = dvn[h]
                dw_ref[rows, cols(h)] = -_mdot(dvn[h], s, "nt")
                dkt_ref[rows, cols(h)] = _mdot(vnew, ds[h], "nt")
                dqg_ref[rows, cols(h)] = _mdot(dout[h], s, "nt")
                dqk_ref[ci, h] = jnp.where(tril, _mdot(dout[h], vnew, "nt"), 0.0)
                dgl = jnp.sum(jnp.sum(ds[h] * s, axis=1, keepdims=True), axis=0, keepdims=True)
                dgl_ref[ci, h] = jnp.broadcast_to(dgl, (1, 128))

    return pl.pallas_call(
        body, name=name, grid=(nb,), in_specs=[blk] * 5 + [sq, glb, st], out_specs=[blk] * 4 + [sq, glb],
        out_shape=[jax.ShapeDtypeStruct((l, GROUP_WIDTH), F32)] * 4 + [jax.ShapeDtypeStruct((n_chunks, nh, c, c), F32),
                                                                       jax.ShapeDtypeStruct((n_chunks, nh, 1, 128), F32)],
        scratch_shapes=[pltpu.VMEM((nh, hd, hd), F32)], compiler_params=_params(("arbitrary",)),
    )(do, w, qg, kt, vn, qk, gl, states)


def _gdn_terms_bwd(qn, kn, v, gb, t_inv, dvn, dw, dkt, dqg, dqk, dgl, name):
    l = qn.shape[0]
    n_chunks = l // DN_CHUNK
    g = min(TERMS_CHUNKS, n_chunks)
    rows, c, nh = g * DN_CHUNK, DN_CHUNK, DN_HEADS

    def body(q_ref, k_ref, v_ref, gb_ref, t_ref, dvn_ref, dw_ref, dkt_ref, dqg_ref, dqk_ref, dgl_ref, dq_ref, dk_ref, dv_ref, dgb_ref):
        gbv = gb_ref[...]
        q, k, vv = _to_heads(q_ref[...], g), _to_heads(k_ref[...], g), _to_heads(v_ref[...], g)
        beta = _head_columns(gbv, nh, g)
        t = jnp.concatenate([t_ref[:, h] for h in range(nh)], axis=0)
        x = _wy_terms(q, k, vv, _head_columns(gbv, 0, g), beta, t=t)
        ii, jj, strict = x["ii"], x["jj"], x["strict"]
        du, dwv, dktv, dqgv = (_to_heads(r[...], g) for r in (dvn_ref, dw_ref, dkt_ref, dqg_ref))
        dqkv = jnp.concatenate([dqk_ref[:, h] for h in range(nh)], axis=0)
        dglv = jnp.concatenate([dgl_ref[:, h] for h in range(nh)], axis=0)[:, :, 0:1]
        dt = _bmdot(du, x["vb"], "nt") + _bmdot(dwv, x["kbg"], "nt")
        dvb = _bmdot(t, du, "tn")
        dkbg = _bmdot(t, dwv, "tn")
        tt_dt = _bdot(t, dt, "tn", precision=HI)
        da = jnp.where(strict, -_bdot(tt_dt, t, "nt", precision=HI), 0.0)
        dkk = da * x["dec"]
        dqk0 = dqkv * x["dec"]
        e = (da * x["kk"] + dqkv * x["qk0"]) * x["dec"]
        dkb = _bmdot(dkk, k) + dkbg * x["eg"]
        dk = _bmdot(dkk, x["kb"], "tn") + _bmdot(dqk0, q, "tn") + dktv * x["e2"] + dkb * beta
        dq = _bmdot(dqk0, k) + dqgv * x["eg"]
        s_kt = jnp.sum(dktv * x["kt"], axis=2, keepdims=True)
        dgc_c = (jnp.sum(e, axis=2, keepdims=True) + jnp.sum(dqgv * x["qg"], axis=2, keepdims=True) - s_kt
                 + jnp.sum(dkbg * x["kbg"], axis=2, keepdims=True))
        dgc_last = jnp.sum(s_kt, axis=1, keepdims=True) + dglv * x["gl"]
        rid = lax.broadcasted_iota(jnp.int32, (1, c, 1), 1)
        dgc_c = dgc_c + jnp.where(rid == c - 1, dgc_last, 0.0)
        dgc_r = jnp.sum(jnp.where(ii == jj, dgc_c, 0.0), axis=1, keepdims=True) - jnp.sum(e, axis=1, keepdims=True)
        dg = jnp.sum(jnp.where(jj >= ii, dgc_r, 0.0), axis=2, keepdims=True)
        dbeta = jnp.sum(dkb * k, axis=2, keepdims=True) + jnp.sum(dvb * vv, axis=2, keepdims=True)
        _from_heads(dq, dq_ref, g)
        _from_heads(dk, dk_ref, g)
        _from_heads(dvb * beta, dv_ref, g)
        lane = lax.broadcasted_iota(jnp.int32, gbv.shape, 1)
        dgb = jnp.zeros(gbv.shape, F32)
        for h in range(nh):
            dgb = dgb + jnp.where(lane == h, dg[h * g:(h + 1) * g].reshape(rows, 1), 0.0)
            dgb = dgb + jnp.where(lane == nh + h, dbeta[h * g:(h + 1) * g].reshape(rows, 1), 0.0)
        dgb_ref[...] = dgb

    row = pl.BlockSpec((rows, GROUP_WIDTH), lambda i: (i, 0))
    abrow = pl.BlockSpec((rows, AB_PAD), lambda i: (i, 0))
    sq = pl.BlockSpec((g, nh, c, c), lambda i: (i, 0, 0, 0))
    glb = pl.BlockSpec((g, nh, 1, 128), lambda i: (i, 0, 0, 0))
    return pl.pallas_call(
        body, name=name, grid=(n_chunks // g,), in_specs=[row, row, row, abrow, sq, row, row, row, row, sq, glb],
        out_specs=[row, row, row, abrow],
        out_shape=[jax.ShapeDtypeStruct((l, GROUP_WIDTH), F32)] * 3 + [jax.ShapeDtypeStruct((l, AB_PAD), F32)],
        compiler_params=_params(("parallel",)),
    )(qn, kn, v, gb, t_inv, dvn, dw, dkt, dqg, dqk, dgl)


def _gdn_post_fwd(o, proj, norm_g4, name):
    l, gw = o.shape
    tl = min(512, l)

    def body(o_ref, gate_ref, g_ref, out_ref):
        ov = o_ref[...]
        r = lax.rsqrt(_head_sums(ov * ov) * (1.0 / DN_HEAD_DIM) + RMS_EPS)
        out_ref[...] = (ov * r * g_ref[...] * _silu(gate_ref[...])).astype(out_ref.dtype)

    row = pl.BlockSpec((tl, gw), lambda i: (i, 0))
    return pl.pallas_call(
        body, name=name, grid=(l // tl,),
        in_specs=[row, pl.BlockSpec((tl, gw), lambda i: (i, 7)), pl.BlockSpec((1, gw), lambda i: (0, 0))],
        out_specs=row, out_shape=jax.ShapeDtypeStruct((l, gw), BF16), compiler_params=_params(("parallel",)),
    )(o, proj, norm_g4)


def _gdn_post_bwd(dmixed, o, proj, norm_g4, name):
    l, gw = o.shape
    tl = min(512, l)

    def body(d_ref, o_ref, gate_ref, g_ref, do_ref, dgate_ref, dng_ref):
        @pl.when(pl.program_id(0) == 0)
        def _():
            dng_ref[...] = jnp.zeros_like(dng_ref)

        ov, gate, d = o_ref[...], gate_ref[...], d_ref[...]
        r = lax.rsqrt(_head_sums(ov * ov) * (1.0 / DN_HEAD_DIM) + RMS_EPS)
        oh = ov * r
        sg = _silu(gate)
        dgate_ref[...] = (d * oh * g_ref[...] * _silu_grad(gate)).astype(dgate_ref.dtype)
        dng_ref[...] += jnp.sum(d * sg * oh, axis=0, keepdims=True)
        doh = d * g_ref[...] * sg
        do_ref[...] = r * (doh - oh * _head_sums(doh * oh) * (1.0 / DN_HEAD_DIM))

    row = pl.BlockSpec((tl, gw), lambda i: (i, 0))
    vec = pl.BlockSpec((1, gw), lambda i: (0, 0))
    return pl.pallas_call(
        body, name=name, grid=(l // tl,),
        in_specs=[pl.BlockSpec((tl, gw), lambda i: (i, 3)), row, pl.BlockSpec((tl, gw), lambda i: (i, 7)), vec],
        out_specs=[row, row, vec],
        out_shape=[jax.ShapeDtypeStruct((l, gw), F32), jax.ShapeDtypeStruct((l, gw), BF16), jax.ShapeDtypeStruct((1, gw), F32)],
        compiler_params=_params(("arbitrary",)),
    )(dmixed, o, proj, norm_g4)


def _run(hosts, name, fn):
    h = hosts.get(name)
    if h is None:
        return fn(None)
    res, outs = fn(h[0]())
    h[1](outs)
    return res


def _layer_fwd(x, xm, w, li, hosts):
    l = x.shape[0]
    nm = f"l{li}_"
    proj = _run(hosts, nm + "proj", lambda ops: _matmul(
        xm, w["w_main"], mode="nn", tm=1024, tn=1024, tk=2048,out_dtype=F32, name=nm + "proj", comm=ops))
    proj_ab = _matmul(xm, w["w_ab"], mode="nn", tm=1024, tn=AB_PAD, tk=2048, out_dtype=F32, name=nm + "proj_ab")
    hs, y = _s5_fwd(proj, w["s5_b"], w["s5_c"], w["s5_lam"], w["s5_d"], nm + "s5")
    m_s5 = _s5_glu_fwd(y, w["s5_glu_w"], w["s5_glu_b"], nm + "s5_glu")
    m_sgu = _sgu_fwd(proj, w["sgu_norm_g"], w["sgu_norm_b"], w["sgu_wm"], w["sgu_bfull"], nm + "sgu")
    m_pool, pooled = _pool_fwd(proj, w["pool_w"], w["pool_scale"], nm + "pool")
    qn, kn, v, cq, ck, cv, gb = _gdn_pre_fwd(proj, proj_ab, w["dn_conv_w"], w["dn_a_log"], w["dn_dt_bias"], nm + "gdn_pre")
    u, wy, qg, kt, qk, t_inv, gl = _gdn_terms_fwd(qn, kn, v, gb, nm + "gdn_terms")
    o, vn, states = _gdn_rec_fwd(u, wy, qg, kt, qk, gl, nm + "gdn_rec")
    m_dn = _gdn_post_fwd(o, proj, w["dn_norm_g4"], nm + "gdn_post")
    mixed = jnp.concatenate([m_s5, m_sgu, m_pool, m_dn], axis=1)
    y1 = _matmul(mixed, w["w_out"], mode="nn", tm=1024, tn=1024, tk=2048, out_dtype=F32, name=nm + "out_proj")
    h1, x1, x1m = _ln_fwd(x, y1, w["ln1_g"], w["ln1_b"], nm + "ln1")
    r = _run(hosts, nm + "up", lambda ops: _matmul(
        x1m, w["w_up"], mode="nn", tm=1024, tn=1024, tk=2048,out_dtype=BF16, name=nm + "up",
        epi=lambda acc: jnp.maximum(acc, 0.0), b_slab=w["w_up"].shape[2], comm=ops))
    y2 = _run(hosts, nm + "down", lambda ops: _matmul(
        r, w["w_down"], mode="nn", tm=1024, tn=1024, tk=2048,out_dtype=F32, name=nm + "down", a_fn=lambda a: a * a, comm=ops))
    h2, x2, x2m = _ln_fwd(x1, y2, w["ln2_g"], w["ln2_b"], nm + "ln2")
    saved = dict(xm=xm, proj=proj, proj_ab=proj_ab, hs=hs, y=y, pooled=pooled, qn=qn, kn=kn, v=v, cq=cq, ck=ck, cv=cv, gb=gb,
                 wy=wy, qg=qg, kt=kt, qk=qk, t_inv=t_inv, gl=gl, vn=vn, o=o, states=states, mixed=mixed, h1=h1, x1m=x1m,
                 r=r, h2=h2)
    return x2, x2m, saved


def _layer_bwd(dx2, s, w, small, li, hosts, g):
    nm = f"l{li}b_"
    l = dx2.shape[0]
    gw = GROUP_WIDTH
    wire = MXU_DTYPE
    dh2, dh2m, g["ln2_g"], g["ln2_b"] = _ln_bwd(dx2, s["h2"], w["ln2_g"], nm + "ln2")
    g["w_down"] = _run(hosts, nm + "dw_down", lambda ops: _matmul(
        s["r"], dh2m, mode="tn", tm=1024, tn=1024, tk=2048,out_dtype=wire, name=nm + "dw_down", a_fn=lambda a: a * a,
        comm=ops)).reshape(N_DEV, D_FF // N_DEV, D_MODEL)
    dpre = _run(hosts, nm + "dpre", lambda ops: _matmul(
        dh2m, w["w_down"], mode="nt", tm=1024, tn=1024, tk=2048,out_dtype=BF16, name=nm + "dpre",
        extras=[(s["r"], (None, None), lambda i, j: (i, j))], epi=lambda acc, r: acc * 2.0 * r.astype(F32), comm=ops))
    g["w_up"] = _matmul(s["x1m"], dpre, mode="tn", tm=1024, tn=1024, tk=2048,out_dtype=wire, name=nm + "dw_up",
                        out_slab=D_FF // N_DEV)
    dx1 = _run(hosts, nm + "dx1", lambda ops: _matmul(
        dpre, w["w_up"], mode="nt", tm=1024, tn=1024, tk=2048,out_dtype=F32, name=nm + "dx1",
        extras=[(dh2, (None, None), lambda i, j: (i, j))], epi=lambda acc, e: acc + ALPHA * e,
        b_slab=w["w_up"].shape[2], comm=ops))
    dh1, dh1m, g["ln1_g"], g["ln1_b"] = _ln_bwd(dx1, s["h1"], w["ln1_g"], nm + "ln1")
    g["w_out"] = _matmul(s["mixed"], dh1m, mode="tn", tm=1024, tn=1024, tk=2048,out_dtype=wire,
                         name=nm + "dw_out").reshape(N_DEV, D_MODEL // N_DEV, D_MODEL)
    dmixed = _run(hosts, nm + "dmixed", lambda ops: _matmul(
        dh1m, w["w_out"], mode="nt", tm=1024, tn=1024, tk=2048,out_dtype=F32, name=nm + "dmixed", comm=ops))
    proj, proj_ab = s["proj"], s["proj_ab"]
    dy, dz, yg, g["s5_glu_b"], g["s5_d"] = _s5_glu_bwd(dmixed, s["y"], proj, w["s5_glu_w"], w["s5_glu_b"], nm + "s5_glu")
    g["s5_glu_w"] = _matmul(yg, dz, mode="tn", tm=gw, tn=gw, tk=1024, out_dtype=wire,
                            name=nm + "dw_glu").reshape(N_DEV, gw // N_DEV, gw)
    du_s5, g["s5_b"], g["s5_c"], g["s5_lam"] = _s5_bwd(dy, s["hs"], proj, w["s5_b"], w["s5_c"], w["s5_lam_conj"], w["s5_d"], nm + "s5")
    dzu, dzv, g["sgu_w"], g["sgu_bfull"], g["sgu_norm_g"], g["sgu_norm_b"] = _sgu_bwd(
        dmixed, proj, w["sgu_norm_g"], w["sgu_norm_b"], w["sgu_wm"], w["sgu_bfull"], nm + "sgu")
    dpooled, g["pool_w"], g["pool_scale"] = _pool_bwd_map(dmixed, s["pooled"], w["pool_w"], w["pool_scale"], nm + "pool_map")
    dp = _pool_bwd_window(dpooled, nm + "pool_win")
    do, dgate, g["dn_norm_g4"] = _gdn_post_bwd(dmixed, s["o"], proj, w["dn_norm_g4"], nm + "gdn_post")
    dvn, dwy, dkt, dqg, dqk, dgl = _gdn_rec_bwd(do, s["wy"], s["qg"], s["kt"], s["vn"], s["qk"], s["gl"], s["states"], nm + "gdn_rec")
    dq, dk, dv, dgb = _gdn_terms_bwd(s["qn"], s["kn"], s["v"], s["gb"], s["t_inv"], dvn, dwy, dkt, dqg, dqk, dgl, nm + "gdn_terms")
    dcq, dck, dcv, dab, g["dn_a_log"], g["dn_dt_bias"] = _gdn_pre_bwd(
        dq, dk, dv, s["cq"], s["ck"], s["cv"], dgb, s["gb"], proj_ab, w["dn_a_log"], w["dn_dt_bias"], nm + "gdn_pre")
    dxs, dws = [], []
    for p, dc in enumerate((dcq, dck, dcv)):
        dxp, dwp = _conv_bwd(dc, proj, QKV_BLK + p, w["dn_conv_w"][:, p * gw:(p + 1) * gw], nm + f"conv{p}")
        dxs.append(dxp)
        dws.append(dwp)
    dconv = jnp.concatenate(dws, axis=1)
    g["dn_conv_w"] = jnp.transpose(dconv.reshape(dconv.shape[0], N_DEV, 3 * gw // N_DEV), (1, 0, 2))
    dproj = jnp.concatenate([du_s5, dzu, dzv, dp] + dxs + [dgate], axis=1)
    xm = s["xm"]
    g["small"] = _unprep_grads(g, small)
    dw_main = _run(hosts, nm + "dw_main", lambda ops: _matmul(
        xm, dproj, mode="tn", tm=1024, tn=1024, tk=2048,out_dtype=wire, name=nm + "dw_main", comm=ops))
    dw_ab = _matmul(xm, dab, mode="tn", tm=1024, tn=AB_PAD, tk=1024, out_dtype=wire, name=nm + "dw_ab")
    dw_in = jnp.concatenate([dw_main, dw_ab[:, :2 * DN_HEADS]], axis=1)
    g["w_in"] = jnp.transpose(dw_in.reshape(D_MODEL, N_DEV, dw_in.shape[1] // N_DEV), (1, 0, 2))
    return _run(hosts, nm + "dx", lambda ops: _matmul(
        dproj, w["w_main"], mode="nt", tm=1024, tn=1024, tk=2048, out_dtype=F32, name=nm + "dx",
        extras=[(dh1, (None, None), lambda i, j: (i, j)), (dab, (None, AB_PAD), lambda i, j: (i, 0)),
                (w["w_ab"], ("tn", AB_PAD), lambda i, j: (j, 0))],
        epi=lambda acc, e, da, wab: acc + ALPHA * e + _dot(_mx(da), _mx(wab), "nt"), comm=ops))


SMALL = ("s5_lambda_re", "s5_lambda_im", "s5_log_step", "s5_b_re", "s5_b_im", "s5_c_re", "s5_c_im", "s5_d", "s5_glu_b",
         "sgu_norm_g", "sgu_norm_b", "sgu_w", "sgu_b", "pool_w", "pool_scale", "dn_a_log", "dn_dt_bias", "dn_norm_g",
         "ln1_g", "ln1_b", "ln2_g", "ln2_b")
SHARDED = ("w_in", "s5_glu_w", "dn_conv_w", "w_out", "w_up", "w_down")


def _pad_lanes(v, width=AB_PAD):
    return jnp.pad(v.reshape(1, -1), ((0, 0), (0, width - v.size)))


def _prep_small(p):
    mx = MXU_DTYPE
    lbr, lbi, bbr, bbi = _s5_discretize(p["s5_lambda_re"], p["s5_lambda_im"], p["s5_log_step"], p["s5_b_re"], p["s5_b_im"])
    b_compact, c_compact = _s5_compact(bbr, bbi, p["s5_c_re"], p["s5_c_im"])
    causal = jnp.tril(jnp.ones((SGU_CHUNK, SGU_CHUNK), F32))
    return dict(
        s5_b=b_compact.astype(mx), s5_c=c_compact.astype(mx),
        s5_lam=jnp.concatenate([lbr.reshape(1, -1), lbi.reshape(1, -1)], axis=1),
        s5_lam_conj=jnp.concatenate([lbr.reshape(1, -1), -lbi.reshape(1, -1)], axis=1),
        s5_d=p["s5_d"].reshape(1, -1), s5_glu_b=p["s5_glu_b"].reshape(1, -1),
        sgu_norm_g=p["sgu_norm_g"].reshape(1, -1), sgu_norm_b=p["sgu_norm_b"].reshape(1, -1),
        sgu_wm=(p["sgu_w"] * causal).astype(mx), sgu_bfull=jnp.repeat(p["sgu_b"].T, GROUP_WIDTH // SGU_HEADS, axis=1),
        pool_w=p["pool_w"].astype(mx), pool_scale=p["pool_scale"].reshape(1, -1),
        dn_a_log=_pad_lanes(p["dn_a_log"]), dn_dt_bias=_pad_lanes(p["dn_dt_bias"]),
        dn_norm_g4=jnp.tile(p["dn_norm_g"].reshape(1, -1), (1, DN_HEADS)),
        ln1_g=p["ln1_g"].reshape(1, -1), ln1_b=p["ln1_b"].reshape(1, -1),
        ln2_g=p["ln2_g"].reshape(1, -1), ln2_b=p["ln2_b"].reshape(1, -1),
    )


def _weight_views(name, t):
    if name == "w_in":
        w_in = jnp.transpose(t, (1, 0, 2)).reshape(t.shape[1], N_DEV * t.shape[2])
        pad = AB_PAD - (w_in.shape[1] - MAIN_COLS)
        return dict(w_main=w_in[:, :MAIN_COLS], w_ab=jnp.pad(w_in[:, MAIN_COLS:], ((0, 0), (0, pad))))
    if name == "dn_conv_w":
        return dict(dn_conv_w=jnp.transpose(t, (1, 0, 2)).reshape(t.shape[1], N_DEV * t.shape[2]))
    if name == "w_up":
        return dict(w_up=t)
    return {name: t.reshape(N_DEV * t.shape[1], t.shape[2])}


def _unprep_grads(g, p):
    causal = jnp.tril(jnp.ones((SGU_CHUNK, SGU_CHUNK), F32))
    dbbr, dbbi = _s5_uncompact_b(g["s5_b"])
    dc_re, dc_im = _s5_uncompact_c(g["s5_c"])
    dlbr, dlbi = g["s5_lam"][0, :S5_NS].reshape(S5_GROUPS, S5_STATE), g["s5_lam"][0, S5_NS:].reshape(S5_GROUPS, S5_STATE)
    _, vjp = jax.vjp(_s5_discretize, p["s5_lambda_re"], p["s5_lambda_im"], p["s5_log_step"], p["s5_b_re"], p["s5_b_im"])
    d_lre, d_lim, d_step, d_bre, d_bim = vjp((dlbr, dlbi, dbbr, dbbi))
    hd = GROUP_WIDTH // SGU_HEADS
    return dict(
        s5_lambda_re=d_lre, s5_lambda_im=d_lim, s5_log_step=d_step, s5_b_re=d_bre, s5_b_im=d_bim, s5_c_re=dc_re, s5_c_im=dc_im,
        s5_d=g["s5_d"].reshape(S5_GROUPS, S5_CH), s5_glu_b=g["s5_glu_b"].reshape(-1),
        sgu_norm_g=g["sgu_norm_g"].reshape(-1), sgu_norm_b=g["sgu_norm_b"].reshape(-1), sgu_w=g["sgu_w"] * causal,
        sgu_b=jnp.sum(g["sgu_bfull"].reshape(SGU_CHUNK, SGU_HEADS, hd), axis=2).T,
        pool_w=g["pool_w"], pool_scale=g["pool_scale"].reshape(-1),
        dn_a_log=g["dn_a_log"][0, :DN_HEADS], dn_dt_bias=g["dn_dt_bias"][0, :DN_HEADS],
        dn_norm_g=jnp.sum(g["dn_norm_g4"].reshape(DN_HEADS, DN_HEAD_DIM), axis=0),
        ln1_g=g["ln1_g"].reshape(-1), ln1_b=g["ln1_b"].reshape(-1), ln2_g=g["ln2_g"].reshape(-1), ln2_b=g["ln2_b"].reshape(-1),
    )


def _local_step(x, target, ops, small, fwd_hosts, bwd_hosts, grads):
    saved = []
    h, hm = x, x.astype(MXU_DTYPE)
    for i in range(DEPTH):
        h, hm, s = _layer_fwd(h, hm, ops[i], i, fwd_hosts)
        saved.append(s)
    loss, dh = _loss_head(h, target)
    for i in reversed(range(DEPTH)):
        dh = _layer_bwd(dh, saved[i], ops[i], small[i], i, bwd_hosts, grads[i])
    return loss, dh


def _adamw(w, gparts, m, v, name):
    rr, c = w.shape
    ng = len(gparts)
    r = rr // ng
    lanes = -(-c // 128) * 128
    tr = r
    while tr * lanes * 4 * N_DEV > (4 << 20) and tr % 16 == 0:
        tr //= 2
    nb = r // tr

    def body(w_ref, *rest):
        g_refs, (m_ref, v_ref, go_ref, d_ref, mo_ref, vo_ref) = rest[:ng], rest[ng:]
        layer = pl.program_id(0)
        g = jnp.zeros(m_ref.shape, F32)
        for li in range(ng):
            gl = g_refs[li][0].astype(F32)
            for s in range(1, N_DEV):
                gl = gl + g_refs[li][s].astype(F32)
            g = jnp.where(layer == li, gl, g)
        mn = ADAM_B1 * m_ref[...] + (1.0 - ADAM_B1) * g
        vn = ADAM_B2 * v_ref[...] + (1.0 - ADAM_B2) * g * g
        m_hat = mn / (1.0 - ADAM_B1 ** ADAM_STEP)
        v_hat = vn / (1.0 - ADAM_B2 ** ADAM_STEP)
        go_ref[...] = g
        d_ref[...] = -ADAM_LR * (m_hat / (jnp.sqrt(v_hat) + ADAM_EPS) + ADAM_WD * w_ref[...])
        mo_ref[...] = mn
        vo_ref[...] = vn

    row = pl.BlockSpec((tr, c), lambda li, i: (li * nb + i, 0))
    part_specs = [pl.BlockSpec((N_DEV, tr, c), functools.partial(lambda li, i, k: (0, jnp.where(li == k, i, 0), 0), k=k))
                  for k in range(ng)]
    return pl.pallas_call(
        body, name=name, grid=(ng, nb), in_specs=[row] + part_specs + [row, row],
        out_specs=[row] * 4, out_shape=[jax.ShapeDtypeStruct((rr, c), F32)] * 4, compiler_params=_params(("arbitrary", "arbitrary")),
    )(w, *gparts, m, v)


PACK_LANES = 128
PACK_ROWS = 8192


PACK_TILE = 8 * PACK_LANES


def _pack_rows(t):
    return -(-t.size // PACK_TILE) * 8


def _pack(vals):
    rows = []
    for t in vals:
        flat = t.reshape(-1)
        n_rows = _pack_rows(t)
        rows.append(jnp.pad(flat, (0, n_rows * PACK_LANES - flat.size)).reshape(n_rows, PACK_LANES))
    used = sum(r.shape[0] for r in rows)
    assert used <= PACK_ROWS, used
    return jnp.concatenate(rows + [jnp.zeros((PACK_ROWS - used, PACK_LANES), F32)], axis=0)


def _unpack(packed, like):
    out, off = [], 0
    for t in like:
        n_rows = _pack_rows(t)
        out.append(packed[off:off + n_rows].reshape(-1)[:t.size].reshape(t.shape))
        off += n_rows
    return out


def kernel(x, w_in, s5_lambda_re, s5_lambda_im, s5_log_step, s5_b_re, s5_b_im, s5_c_re, s5_c_im, s5_d, s5_glu_w, s5_glu_b, sgu_norm_g, sgu_norm_b, sgu_w, sgu_b, pool_w, pool_scale, dn_conv_w, dn_a_log, dn_dt_bias, dn_norm_g, w_out, ln1_g, ln1_b, w_up, w_down, ln2_g, ln2_b, loss_target, m_w_in, m_s5_lambda_re, m_s5_lambda_im, m_s5_log_step, m_s5_b_re, m_s5_b_im, m_s5_c_re, m_s5_c_im, m_s5_d, m_s5_glu_w, m_s5_glu_b, m_sgu_norm_g, m_sgu_norm_b, m_sgu_w, m_sgu_b, m_pool_w, m_pool_scale, m_dn_conv_w, m_dn_a_log, m_dn_dt_bias, m_dn_norm_g, m_w_out, m_ln1_g, m_ln1_b, m_w_up, m_w_down, m_ln2_g, m_ln2_b, v_w_in, v_s5_lambda_re, v_s5_lambda_im, v_s5_log_step, v_s5_b_re, v_s5_b_im, v_s5_c_re, v_s5_c_im, v_s5_d, v_s5_glu_w, v_s5_glu_b, v_sgu_norm_g, v_sgu_norm_b, v_sgu_w, v_sgu_b, v_pool_w, v_pool_scale, v_dn_conv_w, v_dn_a_log, v_dn_dt_bias, v_dn_norm_g, v_w_out, v_ln1_g, v_ln1_b, v_w_up, v_w_down, v_ln2_g, v_ln2_b):
    names = ("w_in", "s5_lambda_re", "s5_lambda_im", "s5_log_step", "s5_b_re", "s5_b_im", "s5_c_re", "s5_c_im", "s5_d", "s5_glu_w",
             "s5_glu_b", "sgu_norm_g", "sgu_norm_b", "sgu_w", "sgu_b", "pool_w", "pool_scale", "dn_conv_w", "dn_a_log", "dn_dt_bias",
             "dn_norm_g", "w_out", "ln1_g", "ln1_b", "w_up", "w_down", "ln2_g", "ln2_b")
    env = locals()
    w = {n: env[n] for n in names}
    m = {n: env["m_" + n] for n in names}
    v = {n: env["v_" + n] for n in names}

    wire = [{n: (w[n][i] if n == "dn_conv_w" else w[n][i].astype(MXU_DTYPE)) for n in SHARDED} for i in range(DEPTH)]
    small = [{n: w[n][i] for n in SMALL} for i in range(DEPTH)]
    ops = [_prep_small(small[i]) for i in range(DEPTH)]
    grads = [{} for _ in range(DEPTH)]
    recv = [{} for _ in range(DEPTH)]
    first = ("w_in", "s5_glu_w", "dn_conv_w", "w_out")

    def gather(layer, group):
        def take(outs):
            for n, t in zip(group, outs):
                ops[layer].update(_weight_views(n, t))
        return (lambda: [(wire[layer][n], False) for n in group]), take

    def scatter(layer, group, with_small=False):
        def make():
            sends = [(grads[layer][n], True) for n in group]
            if with_small:
                sends.append((_pack([jnp.stack([grads[i]["small"][n] for i in range(DEPTH)]) for n in SMALL]), False))
            return sends
        def take(outs):
            recv[layer].update(dict(zip(group + (("small",) if with_small else ()), outs)))
        return make, take

    make, take = gather(0, first)
    take(_exchange(make(), "gather_first"))
    fwd_hosts = {"l0_proj": gather(0, ("w_up",)), "l0_up": gather(0, ("w_down",)), "l0_down": gather(1, first),
                 "l1_proj": gather(1, ("w_up",)), "l1_up": gather(1, ("w_down",))}
    late = ("w_in", "s5_glu_w", "dn_conv_w")
    bwd_hosts = {"l1b_dpre": scatter(1, ("w_down",)), "l1b_dx1": scatter(1, ("w_up",)), "l1b_dmixed": scatter(1, ("w_out",)),
                 "l0b_dw_down": scatter(1, late),
                 "l0b_dpre": scatter(0, ("w_down",)), "l0b_dx1": scatter(0, ("w_up",)), "l0b_dmixed": scatter(0, ("w_out",)),
                 "l0b_dw_main": scatter(0, ("s5_glu_w", "dn_conv_w"), with_small=True), "l0b_dx": scatter(0, ("w_in",))}
    loss, grad_x = _local_step(x[0], loss_target[0], ops, small, fwd_hosts, bwd_hosts, grads)

    g_out, d_out, m_out, v_out = {}, {}, {}, {}
    for n in SHARDED:
        shp = w[n].shape
        pad = (-shp[1]) % 8
        def rows(t):
            return jnp.pad(t, ((0, 0), (0, pad), (0, 0))).reshape(shp[0] * (shp[1] + pad), shp[2])
        res = _adamw(rows(w[n]), [recv[i][n] for i in range(DEPTH)], rows(m[n]), rows(v[n]), "adamw_" + n)
        g_out[n], d_out[n], m_out[n], v_out[n] = (t.reshape(shp[0], shp[1] + pad, shp[2])[:, :shp[1]] for t in res)
    like = [w[n] for n in SMALL]
    res = _adamw(_pack(like), [recv[0]["small"]], _pack([m[n] for n in SMALL]), _pack([v[n] for n in SMALL]), "adamw_small")
    for dst, pk in zip((g_out, d_out, m_out, v_out), res):
        dst.update(dict(zip(SMALL, _unpack(pk, like))))

    total = lax.psum(loss[0, 0], MESH_AXES)
    return (total, grad_x[None], *[g_out[n] for n in names], *[d_out[n] for n in names],
            *[m_out[n] for n in names], *[v_out[n] for n in names])
```

```python
import functools
import math

import jax
import jax.numpy as jnp
from jax import lax
from jax.experimental import pallas as pl
from jax.experimental.pallas import tpu as pltpu

F32 = jnp.float32
BF16 = jnp.bfloat16
MXU_DTYPE = jnp.bfloat16
HI = lax.Precision.HIGHEST

N_DEV = 8
D_MODEL = 2048
DEPTH = 2
GROUP_WIDTH = 512
S5_GROUPS, S5_CH, S5_STATE = 32, 16, 64
S5_NS = S5_GROUPS * S5_STATE
SGU_CHUNK, SGU_HEADS = 128, 8
POOL_WINDOWS = (2, 4, 8, 16)
DN_HEADS, DN_HEAD_DIM, DN_CONV, DN_CHUNK = 4, 128, 4, 64
D_FF = 4 * D_MODEL
LN_EPS, RMS_EPS, L2_EPS = 1e-5, 1e-6, 1e-6
ALPHA = (2 * DEPTH) ** 0.25
MAIN_COLS = 4096
AB_PAD = 128
ADAM_LR, ADAM_B1, ADAM_B2, ADAM_EPS, ADAM_WD, ADAM_STEP = 0.001, 0.9, 0.999, 1e-08, 0.01, 10
VMEM_LIMIT = 56 * 1024 * 1024
MIX_ROWS = 512
WIDE_ROWS = 256
C_GELU = math.sqrt(2.0 / math.pi)


def _params(sem=None):
    return pltpu.CompilerParams(dimension_semantics=sem, vmem_limit_bytes=VMEM_LIMIT)


def _gelu(x):
    return 0.5 * x * (1.0 + jnp.tanh(C_GELU * (x + 0.044715 * x * x * x)))


def _gelu_grad(x):
    t = jnp.tanh(C_GELU * (x + 0.044715 * x * x * x))
    return 0.5 * (1.0 + t) + 0.5 * x * (1.0 - t * t) * C_GELU * (1.0 + 3.0 * 0.044715 * x * x)


def _sigmoid(x):
    return 1.0 / (1.0 + jnp.exp(-x))


def _silu(x):
    return x * _sigmoid(x)


def _silu_grad(x):
    s = _sigmoid(x)
    return s * (1.0 + x * (1.0 - s))


def _softplus(x):
    z = jnp.exp(-jnp.abs(x))
    small = z * (1.0 - z * (0.5 - z * (1.0 / 3.0)))
    return jnp.maximum(x, 0.0) + jnp.where(z < 1e-2, small, jnp.log(1.0 + z))


def _mx(x):
    return x.astype(MXU_DTYPE)


def _dot(a, b, dims="nn", precision=None):
    cd = {"nn": ((1,), (0,)), "nt": ((1,), (1,)), "tn": ((0,), (0,))}[dims]
    return lax.dot_general(a, b, (cd, ((), ())), preferred_element_type=F32, precision=precision)


def _mdot(a, b, dims="nn"):
    return _dot(_mx(a), _mx(b), dims)


MESH_AXES = ("x", "y", "c")
OFFSETS = [(dx, dy, dc) for dx in (0, 1) for dy in (0, 1) for dc in (0, 1)][1:]


def _me_and_peers():
    x, y, c = (lax.axis_index(a) for a in MESH_AXES)
    def flip(v, d):
        return 1 - v if d else v
    peers = [(flip(x, dx), flip(y, dy), flip(c, dc)) for dx, dy, dc in OFFSETS]
    def idx(p):
        return 4 * p[0] + 2 * p[1] + p[2]
    return idx((x, y, c)), peers, [idx(p) for p in peers]


SIBLING = OFFSETS.index((0, 0, 1))
SAME_CORE = [OFFSETS.index(f) for f in ((0, 1, 0), (1, 0, 0), (1, 1, 0))]


class _Comm:
    def __init__(self, ops):
        self.arrays = [a for a, _ in ops]
        self.scatter = [s for _, s in ops]
        self.n = n = len(ops)
        hbm = pl.BlockSpec(memory_space=pltpu.HBM)
        self.in_specs, self.out_specs = [hbm] * n, [hbm] * n
        self.out_shape = [jax.ShapeDtypeStruct(a.shape if s else (N_DEV,) + a.shape, a.dtype) for a, s in ops]
        npeer = len(OFFSETS)
        self.scratch = [pltpu.SemaphoreType.DMA((n, npeer)), pltpu.SemaphoreType.DMA((n, npeer)), pltpu.SemaphoreType.DMA((n,))]

    def _plan(self, ins, outs, sems, waiting):
        send_sems, recv_sems, local_sems = sems
        me, peers, peer_idx = _me_and_peers()

        def remote(k, d, src, dst, to):
            return pltpu.make_async_remote_copy(src_ref=src, dst_ref=dst, send_sem=send_sems.at[k, d], recv_sem=recv_sems.at[k, d],
                                                device_id=to, device_id_type=pl.DeviceIdType.MESH)
        plan = []
        for k in range(self.n):
            every = range(len(OFFSETS))
            if self.scatter[k]:
                local = pltpu.make_async_copy(ins[k].at[me], outs[k].at[me], local_sems.at[k])
                pushes = [remote(k, d, ins[k].at[peer_idx[d]], outs[k].at[me], peers[d]) for d in every]
                onward = []
            else:
                local = pltpu.make_async_copy(ins[k], outs[k].at[me], local_sems.at[k])
                pushes = [remote(k, d, ins[k], outs[k].at[me], peers[d]) for d in [SIBLING] + SAME_CORE]
                onward = SAME_CORE
            passed, arrivals = [], {}
            if waiting:
                passed = [(d, remote(k, d + 1, outs[k].at[peer_idx[d]], outs[k].at[peer_idx[d]], peers[SIBLING])) for d in onward]
                arrivals = {d: remote(k, d, outs[k].at[peer_idx[d]], outs[k].at[peer_idx[d]], peers[d]) for d in every}
            plan.append((local, pushes, passed, arrivals))
        return plan

    def start(self, ins, outs, sems):
        for local, pushes, _, _ in self._plan(ins, outs, sems, False):
            local.start()
            for cp in pushes:
                cp.start()

    def wait(self, ins, outs, sems):
        plan = self._plan(ins, outs, sems, True)
        for _, _, passed, arrivals in plan:
            for d, onward in passed:
                arrivals.pop(d).wait_recv()
                onward.start()
        for local, pushes, passed, arrivals in plan:
            for cp in arrivals.values():
                cp.wait_recv()
            for cp in pushes + [onward for _, onward in passed]:
                cp.wait_send()
            local.wait()


def _exchange(ops, name):
    cm = _Comm(ops)

    def body(*refs):
        ins, outs, sems = refs[:cm.n], refs[cm.n:2 * cm.n], refs[2 * cm.n:]
        cm.start(ins, outs, sems)
        cm.wait(ins, outs, sems)

    return pl.pallas_call(body, name=name, in_specs=cm.in_specs, out_specs=cm.out_specs, out_shape=cm.out_shape,
                          scratch_shapes=cm.scratch)(*cm.arrays)


def _matmul(a, b, *, mode, tm, tn, tk, out_dtype, name, a_fn=None, extras=(), epi=None, a_cols=None,
            b_slab=None, out_slab=None, comm=None):
    a_shape = a.shape if a_cols is None else (a.shape[0], a_cols)
    b_shape = b.shape if b_slab is None else (b.shape[1], N_DEV * b_slab)
    if mode == "nn":
        (m, k), n = a_shape, b_shape[1]
    elif mode == "nt":
        (m, k), n = a_shape, b_shape[0]
    else:
        (k, m), n = a_shape, b_shape[1]
    tm, tn, tk = min(tm, m), min(tn, n), min(tk, k)
    if b_slab is not None:
        tn, tk = (tn, min(tk, b_slab)) if mode == "nt" else (min(tn, b_slab), tk)
    assert m % tm == 0 and n % tn == 0 and k % tk == 0, (name, a.shape, b.shape, tm, tn, tk)
    gi, gj, nk = m // tm, n // tn, k // tk
    n_ex = len(extras)
    cm = _Comm(comm) if comm else None
    nc = cm.n if cm else 0

    def body(a_ref, b_ref, *rest):
        ex_refs, rest = rest[:n_ex], rest[n_ex:]
        c_ins, o_ref, c_outs, acc, sems = rest[:nc], rest[nc], rest[nc + 1:2 * nc + 1], rest[2 * nc + 1], rest[2 * nc + 2:]
        i, j, kk = pl.program_id(0), pl.program_id(1), pl.program_id(2)
        if cm:
            @pl.when((i == 0) & (j == 0) & (kk == 0))
            def _():
                cm.start(c_ins, c_outs, sems)

        av = a_ref[...]
        if a_fn is not None:
            av = a_fn(av)
        part = _dot(_mx(av), _mx(b_ref[...]), mode)

        def finish(r):
            if epi is not None:
                r = epi(r, *[e[...] for e in ex_refs])
            o_ref[...] = r.astype(out_dtype)

        if nk == 1:
            finish(part)
        else:
            @pl.when(kk == 0)
            def _():
                acc[...] = part

            @pl.when((kk > 0) & (kk < nk - 1))
            def _():
                acc[...] += part

            @pl.when(kk == nk - 1)
            def _():
                finish(acc[...] + part)

        if cm:
            @pl.when((i == gi - 1) & (j == gj - 1) & (kk == nk - 1))
            def _():
                cm.wait(c_ins, c_outs, sems)

    a_spec = pl.BlockSpec((tk, tm), lambda i, j, kk: (kk, i)) if mode == "tn" else pl.BlockSpec((tm, tk), lambda i, j, kk: (i, kk))
    if b_slab is None:
        b_spec = pl.BlockSpec((tn, tk), lambda i, j, kk: (j, kk)) if mode == "nt" else pl.BlockSpec((tk, tn), lambda i, j, kk: (kk, j))
    elif mode == "nt":
        assert b_slab % tk == 0
        b_spec = pl.BlockSpec((None, tn, tk), lambda i, j, kk: ((kk * tk) // b_slab, j, ((kk * tk) % b_slab) // tk))
    else:
        assert b_slab % tn == 0
        b_spec = pl.BlockSpec((None, tk, tn), lambda i, j, kk: ((j * tn) // b_slab, kk, ((j * tn) % b_slab) // tn))
    if out_slab is None:
        o_spec, o_shape = pl.BlockSpec((tm, tn), lambda i, j, kk: (i, j)), jax.ShapeDtypeStruct((m, n), out_dtype)
    else:
        assert out_slab % tn == 0 and n == N_DEV * out_slab
        o_spec = pl.BlockSpec((None, tm, tn), lambda i, j, kk: ((j * tn) // out_slab, i, ((j * tn) % out_slab) // tn))
        o_shape = jax.ShapeDtypeStruct((N_DEV, m, out_slab), out_dtype)
    ex_specs = [pl.BlockSpec(({None: tm, "tn": tn}.get(bs[0], bs[0]), tn if bs[1] is None else bs[1]),
                             functools.partial(lambda i, j, kk, f: f(i, j), f=im)) for (_, bs, im) in extras]
    res = pl.pallas_call(
        body,
        name=name,
        grid=(gi, gj, nk),
        in_specs=[a_spec, b_spec, *ex_specs] + (cm.in_specs if cm else []),
        out_specs=[o_spec] + (cm.out_specs if cm else []),
        out_shape=[o_shape] + (cm.out_shape if cm else []),
        scratch_shapes=[pltpu.VMEM((tm, tn) if nk > 1 else (8, 128), F32)] + (cm.scratch if cm else []),
        compiler_params=_params(("arbitrary",) * 3 if cm else ("parallel", "parallel", "arbitrary")),
    )(a, b, *[e[0] for e in extras], *(cm.arrays if cm else []))
    return (res[0], res[1:]) if cm else res[0]


def _ln_fwd(x, y, g, b, name):
    l, d = x.shape
    tl = min(WIDE_ROWS, l)

    def body(x_ref, y_ref, g_ref, b_ref, h_ref, o_ref, om_ref):
        h = ALPHA * x_ref[...] + y_ref[...]
        mu = jnp.mean(h, axis=-1, keepdims=True)
        c = h - mu
        var = jnp.mean(c * c, axis=-1, keepdims=True)
        h_ref[...] = h
        out = c * lax.rsqrt(var + LN_EPS) * g_ref[...] + b_ref[...]
        o_ref[...] = out
        om_ref[...] = out.astype(om_ref.dtype)

    row = pl.BlockSpec((tl, d), lambda i: (i, 0))
    vec = pl.BlockSpec((1, d), lambda i: (0, 0))
    return pl.pallas_call(
        body, name=name, grid=(l // tl,), in_specs=[row, row, vec, vec], out_specs=[row, row, row],
        out_shape=[jax.ShapeDtypeStruct((l, d), F32)] * 2 + [jax.ShapeDtypeStruct((l, d), MXU_DTYPE)],
        compiler_params=_params(("parallel",)),
    )(x, y, g, b)


def _ln_bwd(dout, h, g, name):
    l, d = h.shape
    tl = min(WIDE_ROWS, l)

    def body(do_ref, h_ref, g_ref, dh_ref, dhm_ref, dg_ref, db_ref):
        @pl.when(pl.program_id(0) == 0)
        def _():
            dg_ref[...] = jnp.zeros_like(dg_ref)
            db_ref[...] = jnp.zeros_like(db_ref)

        hv, do = h_ref[...], do_ref[...]
        mu = jnp.mean(hv, axis=-1, keepdims=True)
        c = hv - mu
        r = lax.rsqrt(jnp.mean(c * c, axis=-1, keepdims=True) + LN_EPS)
        xh = c * r
        dxh = do * g_ref[...]
        m1 = jnp.mean(dxh, axis=-1, keepdims=True)
        m2 = jnp.mean(dxh * xh, axis=-1, keepdims=True)
        dh = r * (dxh - m1 - xh * m2)
        dh_ref[...] = dh
        dhm_ref[...] = dh.astype(dhm_ref.dtype)
        dg_ref[...] += jnp.sum(do * xh, axis=0, keepdims=True)
        db_ref[...] += jnp.sum(do, axis=0, keepdims=True)

    row = pl.BlockSpec((tl, d), lambda i: (i, 0))
    vec = pl.BlockSpec((1, d), lambda i: (0, 0))
    return pl.pallas_call(
        body, name=name, grid=(l // tl,), in_specs=[row, row, vec], out_specs=[row, row, vec, vec],
        out_shape=[jax.ShapeDtypeStruct((l, d), F32), jax.ShapeDtypeStruct((l, d), MXU_DTYPE),
                   jax.ShapeDtypeStruct((1, d), F32), jax.ShapeDtypeStruct((1, d), F32)],
        compiler_params=_params(("arbitrary",)),
    )(dout, h, g)


def _loss_head(y, target):
    l, d = y.shape
    tl = min(WIDE_ROWS, l)

    def body(y_ref, t_ref, loss_ref, dy_ref):
        @pl.when(pl.program_id(0) == 0)
        def _():
            loss_ref[...] = jnp.zeros_like(loss_ref)

        e = y_ref[...] - t_ref[...]
        dy_ref[...] = e * (1.0 / d)
        s = jnp.sum(jnp.sum(e * e, axis=1, keepdims=True), axis=0, keepdims=True)
        loss_ref[...] += s * (0.5 / d)

    row = pl.BlockSpec((tl, d), lambda i: (i, 0))
    return pl.pallas_call(
        body, name="loss_head", grid=(l // tl,), in_specs=[row, row],
        out_specs=[pl.BlockSpec((1, 1), lambda i: (0, 0)), row],
        out_shape=[jax.ShapeDtypeStruct((1, 1), F32), jax.ShapeDtypeStruct((l, d), F32)],
        compiler_params=_params(("arbitrary",)),
    )(y, target)


def _s5_discretize(lam_re, lam_im, log_step, b_re, b_im):
    step = jnp.exp(log_step)[:, None]
    e = jnp.exp(lam_re * step)
    lbr, lbi = e * jnp.cos(lam_im * step), e * jnp.sin(lam_im * step)
    den = lam_re * lam_re + lam_im * lam_im
    qr = ((lbr - 1.0) * lam_re + lbi * lam_im) / den
    qi = (lbi * lam_re - (lbr - 1.0) * lam_im) / den
    bbr = qr[:, :, None] * b_re - qi[:, :, None] * b_im
    bbi = qr[:, :, None] * b_im + qi[:, :, None] * b_re
    return lbr, lbi, bbr, bbi


S5_TILES, S5_SLABS = 4, 8
S5_TILE_W, S5_SLAB_W = GROUP_WIDTH // S5_TILES, S5_NS // S5_TILES
S5_GPT = S5_GROUPS // S5_TILES


def _s5_compact(bbr, bbi, c_re, c_im):
    eye = jnp.eye(S5_GPT, dtype=F32)
    def bd(t):
        return jnp.einsum("tgph,gk->tghkp", t.reshape(S5_TILES, S5_GPT, S5_STATE, S5_CH), eye).reshape(S5_TILES, S5_TILE_W, S5_SLAB_W)
    def cd(t):
        return jnp.einsum("tghp,gk->tgpkh", t.reshape(S5_TILES, S5_GPT, S5_CH, S5_STATE), eye).reshape(S5_TILES, S5_SLAB_W, S5_TILE_W)
    return jnp.concatenate([bd(bbr), bd(bbi)], axis=0), jnp.concatenate([cd(c_re), -cd(c_im)], axis=0)


def _s5_uncompact_b(db):
    eye = jnp.eye(S5_GPT, dtype=F32)[None, :, None, :, None]
    def ex(t):
        d = jnp.sum(t.reshape(S5_TILES, S5_GPT, S5_CH, S5_GPT, S5_STATE) * eye, axis=3)
        return jnp.transpose(d, (0, 1, 3, 2)).reshape(S5_GROUPS, S5_STATE, S5_CH)
    return ex(db[:S5_TILES]), ex(db[S5_TILES:])


def _s5_uncompact_c(dc):
    eye = jnp.eye(S5_GPT, dtype=F32)[None, :, None, :, None]
    def ex(t):
        d = jnp.sum(t.reshape(S5_TILES, S5_GPT, S5_STATE, S5_GPT, S5_CH) * eye, axis=3)
        return jnp.transpose(d, (0, 1, 3, 2)).reshape(S5_GROUPS, S5_CH, S5_STATE)
    return ex(dc[:S5_TILES]), -ex(dc[S5_TILES:])


S5_ROWS = 256


def _s5_tile(j):
    t = j % S5_TILES
    return slice(t * S5_TILE_W, (t + 1) * S5_TILE_W)


def _s5_slab(j):
    return slice(j * S5_SLAB_W, (j + 1) * S5_SLAB_W)


def _s5_recur(src, lam_ref, carry, emit, n_rows, reverse, extra=()):
    ns = S5_NS
    lr, li = lam_ref[:, :ns], lam_ref[:, ns:]

    def step(t, c):
        row = (n_rows - 1 - t) if reverse else t
        cr, ci = c[0], c[1]
        nr = lr * cr - li * ci + src[pl.ds(row, 1), :ns]
        ni = lr * ci + li * cr + src[pl.ds(row, 1), ns:]
        return (nr, ni) + tuple(emit(row, nr, ni, cr, ci, c[2:]))

    fin = lax.fori_loop(0, n_rows, step, (carry[:, :ns], carry[:, ns:]) + tuple(extra))
    carry[:, :ns] = fin[0]
    carry[:, ns:] = fin[1]
    return fin[2:]


def _s5_fwd(proj, b, c, lam, d, name):
    l = proj.shape[0]
    tl = min(S5_ROWS, l)
    w = 2 * S5_NS

    def body(u_ref, b_ref, c_ref, lam_ref, d_ref, hs_ref, y_ref, bu, carry):
        @pl.when(pl.program_id(0) == 0)
        def _():
            carry[...] = jnp.zeros_like(carry)

        u = u_ref[...]
        um = _mx(u)
        for j in range(S5_SLABS):
            bu[:, _s5_slab(j)] = _dot(um[:, _s5_tile(j)], b_ref[j])

        def emit(row, nr, ni, cr, ci, extra):
            hs_ref[pl.ds(row, 1), :S5_NS] = nr
            hs_ref[pl.ds(row, 1), S5_NS:] = ni
            return extra

        _s5_recur(bu, lam_ref, carry, emit, tl, False)
        for t in range(S5_TILES):
            acc = _dot(_mx(hs_ref[:, _s5_slab(t)]), c_ref[t]) + _dot(_mx(hs_ref[:, _s5_slab(S5_TILES + t)]), c_ref[S5_TILES + t])
            y_ref[:, _s5_tile(t)] = acc + d_ref[:, _s5_tile(t)] * u[:, _s5_tile(t)]

    row = lambda width: pl.BlockSpec((tl, width), lambda i: (i, 0))
    full = lambda a: pl.BlockSpec(a.shape, lambda i: (0,) * a.ndim)
    return pl.pallas_call(
        body, name=name, grid=(l // tl,), in_specs=[row(GROUP_WIDTH), full(b), full(c), full(lam), full(d)],
        out_specs=[row(w), row(GROUP_WIDTH)],
        out_shape=[jax.ShapeDtypeStruct((l, w), F32), jax.ShapeDtypeStruct((l, GROUP_WIDTH), F32)],
        scratch_shapes=[pltpu.VMEM((tl, w), F32), pltpu.VMEM((1, w), F32)], compiler_params=_params(("arbitrary",)),
    )(proj, b, c, lam, d)


def _s5_bwd(dy, hs, proj, b, c, lam_conj, d, name):
    l = dy.shape[0]
    tl = min(S5_ROWS, l)
    nb = l // tl
    w = 2 * S5_NS

    def body(dy_ref, hs_ref, u_ref, b_ref, c_ref, lam_ref, d_ref, du_ref, db_ref, dc_ref, dl_ref, dh, adj, carry):
        @pl.when(pl.program_id(0) == 0)
        def _():
            carry[...] = jnp.zeros_like(carry)
            db_ref[...] = jnp.zeros_like(db_ref)
            dc_ref[...] = jnp.zeros_like(dc_ref)
            dl_ref[...] = jnp.zeros_like(dl_ref)

        dyv = dy_ref[...]
        dym, um = _mx(dyv), _mx(u_ref[...])
        for j in range(S5_SLABS):
            dh[:, _s5_slab(j)] = _dot(dym[:, _s5_tile(j)], c_ref[j], "nt")

        def emit(row, nr, ni, cr, ci, extra):
            adj[pl.ds(row, 1), :S5_NS] = nr
            adj[pl.ds(row, 1), S5_NS:] = ni
            hr, hi = hs_ref[pl.ds(row, 1), :S5_NS], hs_ref[pl.ds(row, 1), S5_NS:]
            return extra[0] + cr * hr + ci * hi, extra[1] + ci * hr - cr * hi

        dl = _s5_recur(dh, lam_ref, carry, emit, tl, True, extra=(dl_ref[:, :S5_NS], dl_ref[:, S5_NS:]))
        dl_ref[:, :S5_NS] = dl[0]
        dl_ref[:, S5_NS:] = dl[1]
        for t in range(S5_TILES):
            acc = (_dot(_mx(adj[:, _s5_slab(t)]), b_ref[t], "nt")
                   + _dot(_mx(adj[:, _s5_slab(S5_TILES + t)]), b_ref[S5_TILES + t], "nt"))
            du_ref[:, _s5_tile(t)] = (acc + d_ref[:, _s5_tile(t)] * dyv[:, _s5_tile(t)]).astype(du_ref.dtype)
        for j in range(S5_SLABS):
            dc_ref[j] += _dot(_mx(hs_ref[:, _s5_slab(j)]), dym[:, _s5_tile(j)], "tn")
            db_ref[j] += _dot(um[:, _s5_tile(j)], _mx(adj[:, _s5_slab(j)]), "tn")

    row = lambda width: pl.BlockSpec((tl, width), lambda i: (nb - 1 - i, 0))
    full = lambda a: pl.BlockSpec(a.shape, lambda i: (0,) * a.ndim)
    acc3 = lambda shape: pl.BlockSpec(shape, lambda i: (0, 0, 0))
    return pl.pallas_call(
        body, name=name, grid=(nb,),
        in_specs=[row(GROUP_WIDTH), row(w), row(GROUP_WIDTH), full(b), full(c), full(lam_conj), full(d)],
        out_specs=[row(GROUP_WIDTH), acc3(b.shape), acc3(c.shape), pl.BlockSpec((1, w), lambda i: (0, 0))],
        out_shape=[jax.ShapeDtypeStruct((l, GROUP_WIDTH), BF16), jax.ShapeDtypeStruct(b.shape, F32),
                   jax.ShapeDtypeStruct(c.shape, F32), jax.ShapeDtypeStruct((1, w), F32)],
        scratch_shapes=[pltpu.VMEM((tl, w), F32), pltpu.VMEM((tl, w), F32), pltpu.VMEM((1, w), F32)],
        compiler_params=_params(("arbitrary",)),
    )(dy, hs, proj, b, c, lam_conj, d)


def _s5_glu_fwd(y, glu_w, glu_b, name):
    l, d = y.shape
    tl = min(2 * MIX_ROWS, l)

    def body(y_ref, w_ref, b_ref, o_ref):
        yg = _gelu(y_ref[...])
        z = _mdot(yg, w_ref[...]) + b_ref[...]
        o_ref[...] = (yg * _sigmoid(z)).astype(o_ref.dtype)

    return pl.pallas_call(
        body, name=name, grid=(l // tl,),
        in_specs=[pl.BlockSpec((tl, d), lambda i: (i, 0)), pl.BlockSpec((d, d), lambda i: (0, 0)), pl.BlockSpec((1, d), lambda i: (0, 0))],
        out_specs=pl.BlockSpec((tl, d), lambda i: (i, 0)), out_shape=jax.ShapeDtypeStruct((l, d), BF16),
        compiler_params=_params(("parallel",)),
    )(y, glu_w, glu_b)


def _s5_glu_bwd(dmixed, y, proj, glu_w, glu_b, name):
    l, d = y.shape
    tl = min(2 * MIX_ROWS, l)

    def body(do_ref, y_ref, u_ref, w_ref, b_ref, dy_ref, dz_ref, yg_ref, db_ref, dd_ref):
        @pl.when(pl.program_id(0) == 0)
        def _():
            db_ref[...] = jnp.zeros_like(db_ref)
            dd_ref[...] = jnp.zeros_like(dd_ref)

        yv, do = y_ref[...], do_ref[...]
        yg = _gelu(yv)
        gate = _sigmoid(_mdot(yg, w_ref[...]) + b_ref[...])
        dz = do * yg * gate * (1.0 - gate)
        dyg = do * gate + _mdot(dz, w_ref[...], "nt")
        dy = dyg * _gelu_grad(yv)
        dy_ref[...] = dy
        dz_ref[...] = dz.astype(dz_ref.dtype)
        yg_ref[...] = yg.astype(yg_ref.dtype)
        db_ref[...] += jnp.sum(dz, axis=0, keepdims=True)
        dd_ref[...] += jnp.sum(dy * u_ref[...], axis=0, keepdims=True)

    row = pl.BlockSpec((tl, d), lambda i: (i, 0))
    vec = pl.BlockSpec((1, d), lambda i: (0, 0))
    return pl.pallas_call(
        body, name=name, grid=(l // tl,),
        in_specs=[row, row, row, pl.BlockSpec((d, d), lambda i: (0, 0)), vec],
        out_specs=[row, row, row, vec, vec],
        out_shape=[jax.ShapeDtypeStruct((l, d), F32), jax.ShapeDtypeStruct((l, d), BF16), jax.ShapeDtypeStruct((l, d), BF16),
                   jax.ShapeDtypeStruct((1, d), F32), jax.ShapeDtypeStruct((1, d), F32)],
        compiler_params=_params(("arbitrary",)),
    )(dmixed, y, proj, glu_w, glu_b)


def _sgu_pair(w_ref, x, j, dims):
    lo = lax.broadcasted_iota(jnp.int32, x.shape, 1) < (GROUP_WIDTH // SGU_HEADS)
    xb = _mx(x)
    r0 = _dot(w_ref[2 * j], xb, dims)
    r1 = _dot(w_ref[2 * j + 1], xb, dims)
    return jnp.where(lo, r0, r1)


def _sgu_norm(v, g, b):
    mu = jnp.mean(v, axis=-1, keepdims=True)
    c = v - mu
    r = lax.rsqrt(jnp.mean(c * c, axis=-1, keepdims=True) + LN_EPS)
    return c * r, r


def _sgu_fwd(proj, norm_g, norm_b, wm, bfull, name):
    l = proj.shape[0]
    tl = min(MIX_ROWS, l)
    gw = GROUP_WIDTH

    def body(zu_ref, zv_ref, g_ref, b_ref, w_ref, bf_ref, o_ref):
        for c in range(tl // SGU_CHUNK):
            rows = slice(c * SGU_CHUNK, (c + 1) * SGU_CHUNK)
            u = _gelu(zu_ref[rows, :])
            vh, _ = _sgu_norm(_gelu(zv_ref[rows, :]), None, None)
            vn = vh * g_ref[...] + b_ref[...]
            for j in range(gw // 128):
                cols = slice(j * 128, (j + 1) * 128)
                mixed = _sgu_pair(w_ref, vn[:, cols], j, "nn") + bf_ref[:, cols]
                o_ref[rows, cols] = (u[:, cols] * mixed).astype(o_ref.dtype)

    vec = pl.BlockSpec((1, gw), lambda i: (0, 0))
    return pl.pallas_call(
        body, name=name, grid=(l // tl,),
        in_specs=[pl.BlockSpec((tl, gw), lambda i: (i, 1)), pl.BlockSpec((tl, gw), lambda i: (i, 2)), vec, vec,
                  pl.BlockSpec((SGU_HEADS, SGU_CHUNK, SGU_CHUNK), lambda i: (0, 0, 0)), pl.BlockSpec((SGU_CHUNK, gw), lambda i: (0, 0))],
        out_specs=pl.BlockSpec((tl, gw), lambda i: (i, 0)), out_shape=jax.ShapeDtypeStruct((l, gw), BF16),
        compiler_params=_params(("parallel",)),
    )(proj, proj, norm_g, norm_b, wm, bfull)


def _sgu_bwd(dmixed, proj, norm_g, norm_b, wm, bfull, name):
    l = proj.shape[0]
    tl = min(MIX_ROWS, l)
    gw = GROUP_WIDTH
    hd = gw // SGU_HEADS

    def body(do_ref, zu_ref, zv_ref, g_ref, b_ref, w_ref, bf_ref, dzu_ref, dzv_ref, dw_ref, dbf_ref, dg_ref, dnb_ref):
        @pl.when(pl.program_id(0) == 0)
        def _():
            dw_ref[...] = jnp.zeros_like(dw_ref)
            dbf_ref[...] = jnp.zeros_like(dbf_ref)
            dg_ref[...] = jnp.zeros_like(dg_ref)
            dnb_ref[...] = jnp.zeros_like(dnb_ref)

        for c in range(tl // SGU_CHUNK):
            rows = slice(c * SGU_CHUNK, (c + 1) * SGU_CHUNK)
            zu, zv, do = zu_ref[rows, :], zv_ref[rows, :], do_ref[rows, :]
            u = _gelu(zu)
            vh, r = _sgu_norm(_gelu(zv), None, None)
            vn = vh * g_ref[...] + b_ref[...]
            dvn_parts, mixed_parts = [], []
            for j in range(gw // 128):
                cols = slice(j * 128, (j + 1) * 128)
                vb = vn[:, cols]
                mixed_parts.append(_sgu_pair(w_ref, vb, j, "nn") + bf_ref[:, cols])
                dm = do[:, cols] * u[:, cols]
                dvn_parts.append(_sgu_pair(w_ref, dm, j, "tn"))
                lo = lax.broadcasted_iota(jnp.int32, dm.shape, 1) < hd
                dw_ref[2 * j] += _mdot(jnp.where(lo, dm, 0.0), vb, "nt")
                dw_ref[2 * j + 1] += _mdot(jnp.where(lo, 0.0, dm), vb, "nt")
                dbf_ref[:, cols] += dm
            mixed = jnp.concatenate(mixed_parts, axis=1)
            dvn = jnp.concatenate(dvn_parts, axis=1)
            dzu_ref[rows, :] = (do * mixed * _gelu_grad(zu)).astype(dzu_ref.dtype)
            dg_ref[...] += jnp.sum(dvn * vh, axis=0, keepdims=True)
            dnb_ref[...] += jnp.sum(dvn, axis=0, keepdims=True)
            dvh = dvn * g_ref[...]
            m1 = jnp.mean(dvh, axis=-1, keepdims=True)
            m2 = jnp.mean(dvh * vh, axis=-1, keepdims=True)
            dv = r * (dvh - m1 - vh * m2)
            dzv_ref[rows, :] = (dv * _gelu_grad(zv)).astype(dzv_ref.dtype)

    vec = pl.BlockSpec((1, gw), lambda i: (0, 0))
    row = pl.BlockSpec((tl, gw), lambda i: (i, 0))
    wspec = pl.BlockSpec((SGU_HEADS, SGU_CHUNK, SGU_CHUNK), lambda i: (0, 0, 0))
    bspec = pl.BlockSpec((SGU_CHUNK, gw), lambda i: (0, 0))
    return pl.pallas_call(
        body, name=name, grid=(l // tl,),
        in_specs=[pl.BlockSpec((tl, gw), lambda i: (i, 1)), pl.BlockSpec((tl, gw), lambda i: (i, 1)), pl.BlockSpec((tl, gw), lambda i: (i, 2)),
                  vec, vec, wspec, bspec],
        out_specs=[row, row, wspec, bspec, vec, vec],
        out_shape=[jax.ShapeDtypeStruct((l, gw), BF16), jax.ShapeDtypeStruct((l, gw), BF16),
                   jax.ShapeDtypeStruct((SGU_HEADS, SGU_CHUNK, SGU_CHUNK), F32), jax.ShapeDtypeStruct((SGU_CHUNK, gw), F32),
                   jax.ShapeDtypeStruct((1, gw), F32), jax.ShapeDtypeStruct((1, gw), F32)],
        compiler_params=_params(("arbitrary",)),
    )(dmixed, proj, proj, norm_g, norm_b, wm, bfull)


HALO = 16


def _window_sums(ext, n_rows, forward):
    def sh(x, k):
        return pltpu.roll(x, (n_rows - k) if forward else k, axis=0)
    s2 = ext + sh(ext, 1)
    s4 = s2 + sh(s2, 2)
    s8 = s4 + sh(s4, 4)
    s16 = s8 + sh(s8, 8)
    return (s2, s4, s8, s16)


def _pool_fwd(proj, pool_w, scale, name):
    l = proj.shape[0]
    tl = min(MIX_ROWS, l)
    gw = GROUP_WIDTH
    pg = gw // len(POOL_WINDOWS)

    def body(x_ref, halo_ref, w_ref, s_ref, o_ref, p_ref):
        i = pl.program_id(0)
        x = x_ref[...]
        halo = jnp.where(i > 0, halo_ref[...], 0.0)
        ext = jnp.concatenate([halo, x], axis=0)
        sums = _window_sums(ext, tl + HALO, False)
        t = i * tl + lax.broadcasted_iota(jnp.int32, (tl, pg), 0)
        for gi, win in enumerate(POOL_WINDOWS):
            cols = slice(gi * pg, (gi + 1) * pg)
            cnt = jnp.minimum(t + 1, win).astype(F32)
            pooled = sums[gi][HALO:, cols] / cnt - x[:, cols]
            p_ref[:, cols] = pooled
            o_ref[:, cols] = (_mdot(pooled, w_ref[gi]) * s_ref[:, cols]).astype(o_ref.dtype)

    row = pl.BlockSpec((tl, gw), lambda i: (i, 0))
    return pl.pallas_call(
        body, name=name, grid=(l // tl,),
        in_specs=[pl.BlockSpec((tl, gw), lambda i: (i, 3)),
                  pl.BlockSpec((HALO, gw), lambda i: (jnp.maximum(i * (tl // HALO) - 1, 0), 3)),
                  pl.BlockSpec((len(POOL_WINDOWS), pg, pg), lambda i: (0, 0, 0)), pl.BlockSpec((1, gw), lambda i: (0, 0))],
        out_specs=[row, row], out_shape=[jax.ShapeDtypeStruct((l, gw), BF16), jax.ShapeDtypeStruct((l, gw), F32)],
        compiler_params=_params(("parallel",)),
    )(proj, proj, pool_w, scale)


def _pool_bwd_map(dmixed, pooled, pool_w, scale, name):
    l, gw = pooled.shape
    tl = min(MIX_ROWS, l)
    ng = len(POOL_WINDOWS)
    pg = gw // ng

    def body(do_ref, p_ref, w_ref, s_ref, dp_ref, dw_ref, ds_ref):
        @pl.when(pl.program_id(0) == 0)
        def _():
            dw_ref[...] = jnp.zeros_like(dw_ref)
            ds_ref[...] = jnp.zeros_like(ds_ref)

        for gi in range(ng):
            cols = slice(gi * pg, (gi + 1) * pg)
            do, pooled_g = do_ref[:, cols], p_ref[:, cols]
            mixed = _mdot(pooled_g, w_ref[gi])
            ds_ref[:, cols] += jnp.sum(do * mixed, axis=0, keepdims=True)
            dm = do * s_ref[:, cols]
            dw_ref[gi] += _mdot(pooled_g, dm, "tn")
            dp_ref[:, cols] = _mdot(dm, w_ref[gi], "nt")

    row = pl.BlockSpec((tl, gw), lambda i: (i, 0))
    wspec = pl.BlockSpec((ng, pg, pg), lambda i: (0, 0, 0))
    vec = pl.BlockSpec((1, gw), lambda i: (0, 0))
    return pl.pallas_call(
        body, name=name, grid=(l // tl,),
        in_specs=[pl.BlockSpec((tl, gw), lambda i: (i, 2)), row, wspec, vec], out_specs=[row, wspec, vec],
        out_shape=[jax.ShapeDtypeStruct((l, gw), F32), jax.ShapeDtypeStruct((ng, pg, pg), F32), jax.ShapeDtypeStruct((1, gw), F32)],
        compiler_params=_params(("arbitrary",)),
    )(dmixed, pooled, pool_w, scale)


def _pool_bwd_window(dpooled, name):
    l, gw = dpooled.shape
    tl = min(MIX_ROWS, l)
    nb = l // tl
    pg = gw // len(POOL_WINDOWS)

    def body(d_ref, halo_ref, o_ref):
        i = pl.program_id(0)
        d = d_ref[...]
        halo = jnp.where(i < nb - 1, halo_ref[...], 0.0)
        ext = jnp.concatenate([d, halo], axis=0)
        t = i * tl + lax.broadcasted_iota(jnp.int32, (tl + HALO, pg), 0)
        for gi, win in enumerate(POOL_WINDOWS):
            cols = slice(gi * pg, (gi + 1) * pg)
            cnt = jnp.minimum(t + 1, win).astype(F32)
            sums = _window_sums(ext[:, cols] / cnt, tl + HALO, True)
            o_ref[:, cols] = (sums[gi][:tl, :] - d[:, cols]).astype(o_ref.dtype)

    row = pl.BlockSpec((tl, gw), lambda i: (i, 0))
    return pl.pallas_call(
        body, name=name, grid=(nb,),
        in_specs=[row, pl.BlockSpec((HALO, gw), lambda i: (jnp.minimum((i + 1) * (tl // HALO), l // HALO - 1), 0))],
        out_specs=row, out_shape=jax.ShapeDtypeStruct((l, gw), BF16), compiler_params=_params(("parallel",)),
    )(dpooled, dpooled)


CONV_HALO = 8
QKV_BLK = 4


def _head_sums(x):
    parts = []
    for hd in range(DN_HEADS):
        s = jnp.sum(x[:, hd * DN_HEAD_DIM:(hd + 1) * DN_HEAD_DIM], axis=-1, keepdims=True)
        parts.append(jnp.broadcast_to(s, (x.shape[0], DN_HEAD_DIM)))
    return jnp.concatenate(parts, axis=1)


def _gdn_pre_fwd(proj, proj_ab, conv_w, a_log, dt_bias, name):
    l = proj.shape[0]
    tl = min(MIX_ROWS, l)
    gw = GROUP_WIDTH

    def body(xq, xk, xv, hq, hk, hv, w_ref, ab_ref, al_ref, dt_ref, qn_ref, kn_ref, v_ref, cq_ref, ck_ref, cv_ref, gb_ref):
        i = pl.program_id(0)
        for p, (x_ref, h_ref, c_ref) in enumerate(((xq, hq, cq_ref), (xk, hk, ck_ref), (xv, hv, cv_ref))):
            ext = jnp.concatenate([jnp.where(i > 0, h_ref[...], 0.0), x_ref[...]], axis=0)
            conv = jnp.zeros((tl, gw), F32)
            for j in range(DN_CONV):
                k = DN_CONV - 1 - j
                shifted = ext if k == 0 else pltpu.roll(ext, k, axis=0)
                conv = conv + shifted[CONV_HALO:, :] * w_ref[j:j + 1, p * gw:(p + 1) * gw]
            c_ref[...] = conv
            s = _silu(conv)
            if p == 2:
                v_ref[...] = s
            else:
                r = lax.rsqrt(_head_sums(s * s) + L2_EPS)
                (qn_ref if p == 0 else kn_ref)[...] = s * r * (DN_HEAD_DIM ** -0.5 if p == 0 else 1.0)
        ab = ab_ref[...]
        lane = lax.broadcasted_iota(jnp.int32, ab.shape, 1)
        g = -jnp.exp(al_ref[...]) * _softplus(ab + dt_ref[...])
        gb_ref[...] = jnp.where(lane < DN_HEADS, g, _sigmoid(ab))

    def xs(b):
        return pl.BlockSpec((tl, gw), lambda i: (i, b))

    def hs(b):
        return pl.BlockSpec((CONV_HALO, gw), lambda i: (jnp.maximum(i * (tl // CONV_HALO) - 1, 0), b))

    row = pl.BlockSpec((tl, gw), lambda i: (i, 0))
    abrow = pl.BlockSpec((tl, AB_PAD), lambda i: (i, 0))
    abvec = pl.BlockSpec((1, AB_PAD), lambda i: (0, 0))
    return pl.pallas_call(
        body, name=name, grid=(l // tl,),
        in_specs=[xs(QKV_BLK), xs(QKV_BLK + 1), xs(QKV_BLK + 2), hs(QKV_BLK), hs(QKV_BLK + 1), hs(QKV_BLK + 2),
                  pl.BlockSpec((DN_CONV, 3 * gw), lambda i: (0, 0)), abrow, abvec, abvec],
        out_specs=[row] * 6 + [abrow],
        out_shape=[jax.ShapeDtypeStruct((l, gw), F32)] * 6 + [jax.ShapeDtypeStruct((l, AB_PAD), F32)],
        compiler_params=_params(("parallel",)),
    )(proj, proj, proj, proj, proj, proj, conv_w, proj_ab, a_log, dt_bias)


def _gdn_pre_bwd(dq, dk, dv, cq, ck, cv, dgb, gb, proj_ab, a_log, dt_bias, name):
    l, gw = cq.shape
    tl = min(MIX_ROWS, l)

    def body(dq_ref, dk_ref, dv_ref, cq_ref, ck_ref, cv_ref, dgb_ref, gb_ref, ab_ref, al_ref, dt_ref,
             dcq_ref, dck_ref, dcv_ref, dab_ref, dal_ref, ddt_ref):
        @pl.when(pl.program_id(0) == 0)
        def _():
            dal_ref[...] = jnp.zeros_like(dal_ref)
            ddt_ref[...] = jnp.zeros_like(ddt_ref)

        for p, (d_ref, c_ref, o_ref) in enumerate(((dq_ref, cq_ref, dcq_ref), (dk_ref, ck_ref, dck_ref), (dv_ref, cv_ref, dcv_ref))):
            c, d = c_ref[...], d_ref[...]
            if p == 2:
                ds = d
            else:
                s = _silu(c)
                r = lax.rsqrt(_head_sums(s * s) + L2_EPS)
                ds = (DN_HEAD_DIM ** -0.5 if p == 0 else 1.0) * r * (d - s * r * r * _head_sums(d * s))
            o_ref[...] = ds * _silu_grad(c)
        ab, dgb_v, gb_v = ab_ref[...], dgb_ref[...], gb_ref[...]
        lane = lax.broadcasted_iota(jnp.int32, ab.shape, 1)
        is_g = lane < DN_HEADS
        dpre = dgb_v * (-jnp.exp(al_ref[...])) * _sigmoid(ab + dt_ref[...])
        dab_ref[...] = jnp.where(is_g, dpre, dgb_v * gb_v * (1.0 - gb_v)).astype(dab_ref.dtype)
        dal_ref[...] += jnp.sum(jnp.where(is_g, dgb_v * gb_v, 0.0), axis=0, keepdims=True)
        ddt_ref[...] += jnp.sum(jnp.where(is_g, dpre, 0.0), axis=0, keepdims=True)

    row = pl.BlockSpec((tl, gw), lambda i: (i, 0))
    abrow = pl.BlockSpec((tl, AB_PAD), lambda i: (i, 0))
    abvec = pl.BlockSpec((1, AB_PAD), lambda i: (0, 0))
    return pl.pallas_call(
        body, name=name, grid=(l // tl,),
        in_specs=[row] * 6 + [abrow, abrow, abrow, abvec, abvec],
        out_specs=[row, row, row, abrow, abvec, abvec],
        out_shape=[jax.ShapeDtypeStruct((l, gw), F32)] * 3 + [jax.ShapeDtypeStruct((l, AB_PAD), BF16),
                   jax.ShapeDtypeStruct((1, AB_PAD), F32), jax.ShapeDtypeStruct((1, AB_PAD), F32)],
        compiler_params=_params(("arbitrary",)),
    )(dq, dk, dv, cq, ck, cv, dgb, gb, proj_ab, a_log, dt_bias)


def _conv_bwd(dc, proj, col_blk, w_part, name):
    l, gw = dc.shape
    tl = min(MIX_ROWS, l)
    nb = l // tl

    def body(dc_ref, halo_ref, x_ref, w_ref, dx_ref, dw_ref):
        i = pl.program_id(0)

        @pl.when(i == 0)
        def _():
            dw_ref[...] = jnp.zeros_like(dw_ref)

        ext = jnp.concatenate([dc_ref[...], jnp.where(i < nb - 1, halo_ref[...], 0.0)], axis=0)
        x = x_ref[...]
        dx = jnp.zeros((tl, gw), F32)
        rid = lax.broadcasted_iota(jnp.int32, (8, gw), 0)
        dw = jnp.zeros((8, gw), F32)
        for j in range(DN_CONV):
            k = DN_CONV - 1 - j
            shifted = (ext if k == 0 else pltpu.roll(ext, tl + CONV_HALO - k, axis=0))[:tl, :]
            dx = dx + shifted * w_ref[j:j + 1, :]
            dw = dw + jnp.where(rid == j, jnp.sum(x * shifted, axis=0, keepdims=True), 0.0)
        dx_ref[...] = dx.astype(dx_ref.dtype)
        dw_ref[...] += dw

    row = pl.BlockSpec((tl, gw), lambda i: (i, 0))
    return pl.pallas_call(
        body, name=name, grid=(nb,),
        in_specs=[row, pl.BlockSpec((CONV_HALO, gw), lambda i: (jnp.minimum((i + 1) * (tl // CONV_HALO), l // CONV_HALO - 1), 0)),
                  pl.BlockSpec((tl, gw), lambda i: (i, col_blk)), pl.BlockSpec((DN_CONV, gw), lambda i: (0, 0))],
        out_specs=[row, pl.BlockSpec((8, gw), lambda i: (0, 0))],
        out_shape=[jax.ShapeDtypeStruct((l, gw), BF16), jax.ShapeDtypeStruct((8, gw), F32)],
        compiler_params=_params(("arbitrary",)),
    )(dc, dc, proj, w_part)


TERMS_CHUNKS = 4


def _bdot(a, b, dims="nn", precision=None):
    cd = {"nn": ((2,), (1,)), "nt": ((2,), (2,)), "tn": ((1,), (1,))}[dims]
    return lax.dot_general(a, b, (cd, ((0,), (0,))), preferred_element_type=F32, precision=precision)


def _bmdot(a, b, dims="nn"):
    return _bdot(_mx(a), _mx(b), dims)


def _wy_terms(q, k, v, gcol, beta, t=None):
    c = DN_CHUNK
    ii = lax.broadcasted_iota(jnp.int32, (1, c, c), 1)
    jj = lax.broadcasted_iota(jnp.int32, (1, c, c), 2)
    tril, strict = ii >= jj, ii > jj
    grow = jnp.sum(jnp.where(ii == jj, gcol, 0.0), axis=1, keepdims=True)
    gc_col = jnp.sum(jnp.where(tril, grow, 0.0), axis=2, keepdims=True)
    gc_row = jnp.sum(jnp.where(ii <= jj, gcol, 0.0), axis=1, keepdims=True)
    dec = jnp.exp(jnp.where(tril, gc_col - gc_row, -1e30))
    kb, vb = k * beta, v * beta
    kk = _bmdot(kb, k, "nt")
    if t is None:
        a = jnp.where(strict, kk * dec, 0.0)
        d = jnp.where((ii >> 3) == (jj >> 3), a, 0.0)
        t = jnp.where(ii == jj, 1.0, 0.0) - d
        p = _bdot(d, d, precision=HI)
        t = t + _bdot(t, p, precision=HI)
        t = t + _bdot(t, _bdot(p, p, precision=HI), precision=HI)
        for sh in (3, 4, 5):
            below = ((ii >> (sh + 1)) == (jj >> (sh + 1))) & ((ii >> sh) > (jj >> sh))
            t = t - _bdot(t, _bdot(jnp.where(below, a, 0.0), t, precision=HI), precision=HI)
    eg = jnp.exp(gc_col)
    gc_last = gc_col[:, c - 1:c, :]
    kbg = kb * eg
    qk0 = _bmdot(q, k, "nt")
    e2 = jnp.exp(gc_last - gc_col)
    return dict(ii=ii, jj=jj, tril=tril, strict=strict, dec=dec, kb=kb, vb=vb, kk=kk, t=t, eg=eg, kbg=kbg,
                qk0=qk0, qk=jnp.where(tril, qk0 * dec, 0.0), qg=q * eg, e2=e2, kt=k * e2, gl=jnp.exp(gc_last))


def _to_heads(x, g):
    return jnp.concatenate([x[:, h * DN_HEAD_DIM:(h + 1) * DN_HEAD_DIM].reshape(g, DN_CHUNK, DN_HEAD_DIM)
                            for h in range(DN_HEADS)], axis=0)


def _from_heads(t, ref, g):
    for h in range(DN_HEADS):
        ref[:, h * DN_HEAD_DIM:(h + 1) * DN_HEAD_DIM] = t[h * g:(h + 1) * g].reshape(g * DN_CHUNK, DN_HEAD_DIM).astype(ref.dtype)


def _head_columns(gbv, first_lane, g):
    lane = lax.broadcasted_iota(jnp.int32, gbv.shape, 1)
    return jnp.concatenate([jnp.sum(jnp.where(lane == first_lane + h, gbv, 0.0), axis=1, keepdims=True).reshape(g, DN_CHUNK, 1)
                            for h in range(DN_HEADS)], axis=0)


def _gdn_terms_fwd(qn, kn, v, gb, name):
    l = qn.shape[0]
    n_chunks = l // DN_CHUNK
    g = min(TERMS_CHUNKS, n_chunks)
    rows, c, nh = g * DN_CHUNK, DN_CHUNK, DN_HEADS

    def body(q_ref, k_ref, v_ref, gb_ref, u_ref, w_ref, qg_ref, kt_ref, qk_ref, t_ref, gl_ref):
        gbv = gb_ref[...]
        x = _wy_terms(_to_heads(q_ref[...], g), _to_heads(k_ref[...], g), _to_heads(v_ref[...], g),
                      _head_columns(gbv, 0, g), _head_columns(gbv, nh, g))
        _from_heads(_bmdot(x["t"], x["vb"]), u_ref, g)
        _from_heads(_bmdot(x["t"], x["kbg"]), w_ref, g)
        _from_heads(x["qg"], qg_ref, g)
        _from_heads(x["kt"], kt_ref, g)
        for h in range(nh):
            qk_ref[:, h] = x["qk"][h * g:(h + 1) * g]
            t_ref[:, h] = x["t"][h * g:(h + 1) * g]
            gl_ref[:, h] = jnp.broadcast_to(x["gl"][h * g:(h + 1) * g], (g, 1, 128))

    row = pl.BlockSpec((rows, GROUP_WIDTH), lambda i: (i, 0))
    sq = pl.BlockSpec((g, nh, c, c), lambda i: (i, 0, 0, 0))
    glb = pl.BlockSpec((g, nh, 1, 128), lambda i: (i, 0, 0, 0))
    return pl.pallas_call(
        body, name=name, grid=(n_chunks // g,), in_specs=[row, row, row, pl.BlockSpec((rows, AB_PAD), lambda i: (i, 0))],
        out_specs=[row] * 4 + [sq, sq, glb],
        out_shape=[jax.ShapeDtypeStruct((l, GROUP_WIDTH), F32)] * 4 + [jax.ShapeDtypeStruct((n_chunks, nh, c, c), F32)] * 2
        + [jax.ShapeDtypeStruct((n_chunks, nh, 1, 128), F32)],
        compiler_params=_params(("parallel",)),
    )(qn, kn, v, gb)


REC_CHUNKS = 4


def _rec_specs(n_chunks, reverse):
    c, hd, nh = DN_CHUNK, DN_HEAD_DIM, DN_HEADS
    g = min(REC_CHUNKS, n_chunks)
    nb = n_chunks // g
    blk_of = (lambda n: nb - 1 - n) if reverse else (lambda n: n)
    return g, nb, (pl.BlockSpec((g * c, GROUP_WIDTH), lambda n: (blk_of(n), 0)), pl.BlockSpec((g, nh, c, c), lambda n: (blk_of(n), 0, 0, 0)),
                   pl.BlockSpec((g, nh, 1, 128), lambda n: (blk_of(n), 0, 0, 0)), pl.BlockSpec((g, nh, hd, hd), lambda n: (blk_of(n), 0, 0, 0)))


def _gdn_rec_fwd(u, w, qg, kt, qk, gl, name):
    l = u.shape[0]
    n_chunks = l // DN_CHUNK
    hd, nh, c = DN_HEAD_DIM, DN_HEADS, DN_CHUNK
    g, nb, (blk, sq, glb, st) = _rec_specs(n_chunks, False)
    heads = range(nh)

    def body(u_ref, w_ref, qg_ref, kt_ref, qk_ref, gl_ref, o_ref, vn_ref, s_ref, state):
        @pl.when(pl.program_id(0) == 0)
        def _():
            state[...] = jnp.zeros_like(state)

        def cols(h):
            return slice(h * hd, (h + 1) * hd)
        for ci in range(g):
            rows = slice(ci * c, (ci + 1) * c)
            s = [state[h] for h in heads]
            ws = [_mdot(w_ref[rows, cols(h)], s[h]) for h in heads]
            vn = [u_ref[rows, cols(h)] - ws[h] for h in heads]
            kv = [_mdot(kt_ref[rows, cols(h)], vn[h], "tn") for h in heads]
            for h in heads:
                state[h] = s[h] * gl_ref[ci, h] + kv[h]
            o1 = [_mdot(qg_ref[rows, cols(h)], s[h]) for h in heads]
            o2 = [_mdot(qk_ref[ci, h], vn[h]) for h in heads]
            for h in heads:
                s_ref[ci, h] = s[h]
                o_ref[rows, cols(h)] = o1[h] + o2[h]
                vn_ref[rows, cols(h)] = vn[h]

    return pl.pallas_call(
        body, name=name, grid=(nb,), in_specs=[blk, blk, blk, blk, sq, glb], out_specs=[blk, blk, st],
        out_shape=[jax.ShapeDtypeStruct((l, GROUP_WIDTH), F32)] * 2 + [jax.ShapeDtypeStruct((n_chunks, nh, hd, hd), F32)],
        scratch_shapes=[pltpu.VMEM((nh, hd, hd), F32)], compiler_params=_params(("arbitrary",)),
    )(u, w, qg, kt, qk, gl)


def _gdn_rec_bwd(do, w, qg, kt, vn, qk, gl, states, name):
    l = do.shape[0]
    n_chunks = l // DN_CHUNK
    hd, nh, c = DN_HEAD_DIM, DN_HEADS, DN_CHUNK
    g, nb, (blk, sq, glb, st) = _rec_specs(n_chunks, True)
    heads = range(nh)

    def body(do_ref, w_ref, qg_ref, kt_ref, vn_ref, qk_ref, gl_ref, s_ref, dvn_ref, dw_ref, dkt_ref, dqg_ref, dqk_ref, dgl_ref, dstate):
        @pl.when(pl.program_id(0) == 0)
        def _():
            dstate[...] = jnp.zeros_like(dstate)

        def cols(h):
            return slice(h * hd, (h + 1) * hd)
        tril = lax.broadcasted_iota(jnp.int32, (c, c), 0) >= lax.broadcasted_iota(jnp.int32, (c, c), 1)
        for ci in reversed(range(g)):
            rows = slice(ci * c, (ci + 1) * c)
            ds = [dstate[h] for h in heads]
            dout = [do_ref[rows, cols(h)] for h in heads]
            a1 = [_mdot(qk_ref[ci, h], dout[h], "tn") for h in heads]
            a2 = [_mdot(kt_ref[rows, cols(h)], ds[h]) for h in heads]
            dvn = [a1[h] + a2[h] for h in heads]
            b1 = [_mdot(qg_ref[rows, cols(h)], dout[h], "tn") for h in heads]
            b2 = [_mdot(w_ref[rows, cols(h)], dvn[h], "tn") for h in heads]
            for h in heads:
                dstate[h] = b1[h] + gl_ref[ci, h] * ds[h] - b2[h]
            for h in heads:
                s, vnew = s_ref[ci, h], vn_ref[rows, cols(h)]
                dvn_ref[rows, cols(h)] = dvn[h]
                dw_ref[rows, cols(h)] = -_mdot(dvn[h], s, "nt")
                dkt_ref[rows, cols(h)] = _mdot(vnew, ds[h], "nt")
                dqg_ref[rows, cols(h)] = _mdot(dout[h], s, "nt")
                dqk_ref[ci, h] = jnp.where(tril, _mdot(dout[h], vnew, "nt"), 0.0)
                dgl = jnp.sum(jnp.sum(ds[h] * s, axis=1, keepdims=True), axis=0, keepdims=True)
                dgl_ref[ci, h] = jnp.broadcast_to(dgl, (1, 128))

    return pl.pallas_call(
        body, name=name, grid=(nb,), in_specs=[blk] * 5 + [sq, glb, st], out_specs=[blk] * 4 + [sq, glb],
        out_shape=[jax.ShapeDtypeStruct((l, GROUP_WIDTH), F32)] * 4 + [jax.ShapeDtypeStruct((n_chunks, nh, c, c), F32),
                                                                       jax.ShapeDtypeStruct((n_chunks, nh, 1, 128), F32)],
        scratch_shapes=[pltpu.VMEM((nh, hd, hd), F32)], compiler_params=_params(("arbitrary",)),
    )(do, w, qg, kt, vn, qk, gl, states)


def _gdn_terms_bwd(qn, kn, v, gb, t_inv, dvn, dw, dkt, dqg, dqk, dgl, name):
    l = qn.shape[0]
    n_chunks = l // DN_CHUNK
    g = min(TERMS_CHUNKS, n_chunks)
    rows, c, nh = g * DN_CHUNK, DN_CHUNK, DN_HEADS

    def body(q_ref, k_ref, v_ref, gb_ref, t_ref, dvn_ref, dw_ref, dkt_ref, dqg_ref, dqk_ref, dgl_ref, dq_ref, dk_ref, dv_ref, dgb_ref):
        gbv = gb_ref[...]
        q, k, vv = _to_heads(q_ref[...], g), _to_heads(k_ref[...], g), _to_heads(v_ref[...], g)
        beta = _head_columns(gbv, nh, g)
        t = jnp.concatenate([t_ref[:, h] for h in range(nh)], axis=0)
        x = _wy_terms(q, k, vv, _head_columns(gbv, 0, g), beta, t=t)
        ii, jj, strict = x["ii"], x["jj"], x["strict"]
        du, dwv, dktv, dqgv = (_to_heads(r[...], g) for r in (dvn_ref, dw_ref, dkt_ref, dqg_ref))
        dqkv = jnp.concatenate([dqk_ref[:, h] for h in range(nh)], axis=0)
        dglv = jnp.concatenate([dgl_ref[:, h] for h in range(nh)], axis=0)[:, :, 0:1]
        dt = _bmdot(du, x["vb"], "nt") + _bmdot(dwv, x["kbg"], "nt")
        dvb = _bmdot(t, du, "tn")
        dkbg = _bmdot(t, dwv, "tn")
        tt_dt = _bdot(t, dt, "tn", precision=HI)
        da = jnp.where(strict, -_bdot(tt_dt, t, "nt", precision=HI), 0.0)
        dkk = da * x["dec"]
        dqk0 = dqkv * x["dec"]
        e = (da * x["kk"] + dqkv * x["qk0"]) * x["dec"]
        dkb = _bmdot(dkk, k) + dkbg * x["eg"]
        dk = _bmdot(dkk, x["kb"], "tn") + _bmdot(dqk0, q, "tn") + dktv * x["e2"] + dkb * beta
        dq = _bmdot(dqk0, k) + dqgv * x["eg"]
        s_kt = jnp.sum(dktv * x["kt"], axis=2, keepdims=True)
        dgc_c = (jnp.sum(e, axis=2, keepdims=True) + jnp.sum(dqgv * x["qg"], axis=2, keepdims=True) - s_kt
                 + jnp.sum(dkbg * x["kbg"], axis=2, keepdims=True))
        dgc_last = jnp.sum(s_kt, axis=1, keepdims=True) + dglv * x["gl"]
        rid = lax.broadcasted_iota(jnp.int32, (1, c, 1), 1)
        dgc_c = dgc_c + jnp.where(rid == c - 1, dgc_last, 0.0)
        dgc_r = jnp.sum(jnp.where(ii == jj, dgc_c, 0.0), axis=1, keepdims=True) - jnp.sum(e, axis=1, keepdims=True)
        dg = jnp.sum(jnp.where(jj >= ii, dgc_r, 0.0), axis=2, keepdims=True)
        dbeta = jnp.sum(dkb * k, axis=2, keepdims=True) + jnp.sum(dvb * vv, axis=2, keepdims=True)
        _from_heads(dq, dq_ref, g)
        _from_heads(dk, dk_ref, g)
        _from_heads(dvb * beta, dv_ref, g)
        lane = lax.broadcasted_iota(jnp.int32, gbv.shape, 1)
        dgb = jnp.zeros(gbv.shape, F32)
        for h in range(nh):
            dgb = dgb + jnp.where(lane == h, dg[h * g:(h + 1) * g].reshape(rows, 1), 0.0)
            dgb = dgb + jnp.where(lane == nh + h, dbeta[h * g:(h + 1) * g].reshape(rows, 1), 0.0)
        dgb_ref[...] = dgb

    row = pl.BlockSpec((rows, GROUP_WIDTH), lambda i: (i, 0))
    abrow = pl.BlockSpec((rows, AB_PAD), lambda i: (i, 0))
    sq = pl.BlockSpec((g, nh, c, c), lambda i: (i, 0, 0, 0))
    glb = pl.BlockSpec((g, nh, 1, 128), lambda i: (i, 0, 0, 0))
    return pl.pallas_call(
        body, name=name, grid=(n_chunks // g,), in_specs=[row, row, row, abrow, sq, row, row, row, row, sq, glb],
        out_specs=[row, row, row, abrow],
        out_shape=[jax.ShapeDtypeStruct((l, GROUP_WIDTH), F32)] * 3 + [jax.ShapeDtypeStruct((l, AB_PAD), F32)],
        compiler_params=_params(("parallel",)),
    )(qn, kn, v, gb, t_inv, dvn, dw, dkt, dqg, dqk, dgl)


def _gdn_post_fwd(o, proj, norm_g4, name):
    l, gw = o.shape
    tl = min(2 * MIX_ROWS, l)

    def body(o_ref, gate_ref, g_ref, out_ref):
        ov = o_ref[...]
        r = lax.rsqrt(_head_sums(ov * ov) * (1.0 / DN_HEAD_DIM) + RMS_EPS)
        out_ref[...] = (ov * r * g_ref[...] * _silu(gate_ref[...])).astype(out_ref.dtype)

    row = pl.BlockSpec((tl, gw), lambda i: (i, 0))
    return pl.pallas_call(
        body, name=name, grid=(l // tl,),
        in_specs=[row, pl.BlockSpec((tl, gw), lambda i: (i, 7)), pl.BlockSpec((1, gw), lambda i: (0, 0))],
        out_specs=row, out_shape=jax.ShapeDtypeStruct((l, gw), BF16), compiler_params=_params(("parallel",)),
    )(o, proj, norm_g4)


def _gdn_post_bwd(dmixed, o, proj, norm_g4, name):
    l, gw = o.shape
    tl = min(2 * MIX_ROWS, l)

    def body(d_ref, o_ref, gate_ref, g_ref, do_ref, dgate_ref, dng_ref):
        @pl.when(pl.program_id(0) == 0)
        def _():
            dng_ref[...] = jnp.zeros_like(dng_ref)

        ov, gate, d = o_ref[...], gate_ref[...], d_ref[...]
        r = lax.rsqrt(_head_sums(ov * ov) * (1.0 / DN_HEAD_DIM) + RMS_EPS)
        oh = ov * r
        sg = _silu(gate)
        dgate_ref[...] = (d * oh * g_ref[...] * _silu_grad(gate)).astype(dgate_ref.dtype)
        dng_ref[...] += jnp.sum(d * sg * oh, axis=0, keepdims=True)
        doh = d * g_ref[...] * sg
        do_ref[...] = r * (doh - oh * _head_sums(doh * oh) * (1.0 / DN_HEAD_DIM))

    row = pl.BlockSpec((tl, gw), lambda i: (i, 0))
    vec = pl.BlockSpec((1, gw), lambda i: (0, 0))
    return pl.pallas_call(
        body, name=name, grid=(l // tl,),
        in_specs=[pl.BlockSpec((tl, gw), lambda i: (i, 3)), row, pl.BlockSpec((tl, gw), lambda i: (i, 7)), vec],
        out_specs=[row, row, vec],
        out_shape=[jax.ShapeDtypeStruct((l, gw), F32), jax.ShapeDtypeStruct((l, gw), BF16), jax.ShapeDtypeStruct((1, gw), F32)],
        compiler_params=_params(("arbitrary",)),
    )(dmixed, o, proj, norm_g4)


def _run(hosts, name, fn):
    h = hosts.get(name)
    if h is None:
        return fn(None)
    res, outs = fn(h[0]())
    h[1](outs)
    return res


def _layer_fwd(x, xm, w, li, hosts):
    l = x.shape[0]
    nm = f"l{li}_"
    proj = _run(hosts, nm + "proj", lambda ops: _matmul(
        xm, w["w_main"], mode="nn", tm=1024, tn=1024, tk=2048,out_dtype=F32, name=nm + "proj", comm=ops))
    proj_ab = _matmul(xm, w["w_ab"], mode="nn", tm=1024, tn=AB_PAD, tk=2048, out_dtype=F32, name=nm + "proj_ab")
    hs, y = _s5_fwd(proj, w["s5_b"], w["s5_c"], w["s5_lam"], w["s5_d"], nm + "s5")
    m_s5 = _s5_glu_fwd(y, w["s5_glu_w"], w["s5_glu_b"], nm + "s5_glu")
    m_sgu = _sgu_fwd(proj, w["sgu_norm_g"], w["sgu_norm_b"], w["sgu_wm"], w["sgu_bfull"], nm + "sgu")
    m_pool, pooled = _pool_fwd(proj, w["pool_w"], w["pool_scale"], nm + "pool")
    qn, kn, v, cq, ck, cv, gb = _gdn_pre_fwd(proj, proj_ab, w["dn_conv_w"], w["dn_a_log"], w["dn_dt_bias"], nm + "gdn_pre")
    u, wy, qg, kt, qk, t_inv, gl = _gdn_terms_fwd(qn, kn, v, gb, nm + "gdn_terms")
    o, vn, states = _gdn_rec_fwd(u, wy, qg, kt, qk, gl, nm + "gdn_rec")
    m_dn = _gdn_post_fwd(o, proj, w["dn_norm_g4"], nm + "gdn_post")
    mixed = jnp.concatenate([m_s5, m_sgu, m_pool, m_dn], axis=1)
    y1 = _matmul(mixed, w["w_out"], mode="nn", tm=1024, tn=1024, tk=2048, out_dtype=F32, name=nm + "out_proj")
    h1, x1, x1m = _ln_fwd(x, y1, w["ln1_g"], w["ln1_b"], nm + "ln1")
    r = _run(hosts, nm + "up", lambda ops: _matmul(
        x1m, w["w_up"], mode="nn", tm=1024, tn=1024, tk=2048,out_dtype=BF16, name=nm + "up",
        epi=lambda acc: jnp.maximum(acc, 0.0), b_slab=w["w_up"].shape[2], comm=ops))
    y2 = _run(hosts, nm + "down", lambda ops: _matmul(
        r, w["w_down"], mode="nn", tm=1024, tn=1024, tk=2048,out_dtype=F32, name=nm + "down", a_fn=lambda a: a * a, comm=ops))
    h2, x2, x2m = _ln_fwd(x1, y2, w["ln2_g"], w["ln2_b"], nm + "ln2")
    saved = dict(xm=xm, proj=proj, proj_ab=proj_ab, hs=hs, y=y, pooled=pooled, qn=qn, kn=kn, v=v, cq=cq, ck=ck, cv=cv, gb=gb,
                 wy=wy, qg=qg, kt=kt, qk=qk, t_inv=t_inv, gl=gl, vn=vn, o=o, states=states, mixed=mixed, h1=h1, x1m=x1m,
                 r=r, h2=h2)
    return x2, x2m, saved


def _layer_bwd(dx2, s, w, small, li, hosts, g):
    nm = f"l{li}b_"
    l = dx2.shape[0]
    gw = GROUP_WIDTH
    wire = MXU_DTYPE
    dh2, dh2m, g["ln2_g"], g["ln2_b"] = _ln_bwd(dx2, s["h2"], w["ln2_g"], nm + "ln2")
    g["w_down"] = _run(hosts, nm + "dw_down", lambda ops: _matmul(
        s["r"], dh2m, mode="tn", tm=1024, tn=1024, tk=2048,out_dtype=wire, name=nm + "dw_down", a_fn=lambda a: a * a,
        comm=ops)).reshape(N_DEV, D_FF // N_DEV, D_MODEL)
    dpre = _run(hosts, nm + "dpre", lambda ops: _matmul(
        dh2m, w["w_down"], mode="nt", tm=1024, tn=1024, tk=2048,out_dtype=BF16, name=nm + "dpre",
        extras=[(s["r"], (None, None), lambda i, j: (i, j))], epi=lambda acc, r: acc * 2.0 * r.astype(F32), comm=ops))
    g["w_up"] = _matmul(s["x1m"], dpre, mode="tn", tm=1024, tn=1024, tk=2048,out_dtype=wire, name=nm + "dw_up",
                        out_slab=D_FF // N_DEV)
    dx1 = _run(hosts, nm + "dx1", lambda ops: _matmul(
        dpre, w["w_up"], mode="nt", tm=1024, tn=1024, tk=2048,out_dtype=F32, name=nm + "dx1",
        extras=[(dh2, (None, None), lambda i, j: (i, j))], epi=lambda acc, e: acc + ALPHA * e,
        b_slab=w["w_up"].shape[2], comm=ops))
    dh1, dh1m, g["ln1_g"], g["ln1_b"] = _ln_bwd(dx1, s["h1"], w["ln1_g"], nm + "ln1")
    g["w_out"] = _matmul(s["mixed"], dh1m, mode="tn", tm=1024, tn=1024, tk=2048,out_dtype=wire,
                         name=nm + "dw_out").reshape(N_DEV, D_MODEL // N_DEV, D_MODEL)
    dmixed = _run(hosts, nm + "dmixed", lambda ops: _matmul(
        dh1m, w["w_out"], mode="nt", tm=1024, tn=1024, tk=2048,out_dtype=F32, name=nm + "dmixed", comm=ops))
    proj, proj_ab = s["proj"], s["proj_ab"]
    dy, dz, yg, g["s5_glu_b"], g["s5_d"] = _s5_glu_bwd(dmixed, s["y"], proj, w["s5_glu_w"], w["s5_glu_b"], nm + "s5_glu")
    g["s5_glu_w"] = _matmul(yg, dz, mode="tn", tm=gw, tn=gw, tk=1024, out_dtype=wire,
                            name=nm + "dw_glu").reshape(N_DEV, gw // N_DEV, gw)
    du_s5, g["s5_b"], g["s5_c"], g["s5_lam"] = _s5_bwd(dy, s["hs"], proj, w["s5_b"], w["s5_c"], w["s5_lam_conj"], w["s5_d"], nm + "s5")
    dzu, dzv, g["sgu_w"], g["sgu_bfull"], g["sgu_norm_g"], g["sgu_norm_b"] = _sgu_bwd(
        dmixed, proj, w["sgu_norm_g"], w["sgu_norm_b"], w["sgu_wm"], w["sgu_bfull"], nm + "sgu")
    dpooled, g["pool_w"], g["pool_scale"] = _pool_bwd_map(dmixed, s["pooled"], w["pool_w"], w["pool_scale"], nm + "pool_map")
    dp = _pool_bwd_window(dpooled, nm + "pool_win")
    do, dgate, g["dn_norm_g4"] = _gdn_post_bwd(dmixed, s["o"], proj, w["dn_norm_g4"], nm + "gdn_post")
    dvn, dwy, dkt, dqg, dqk, dgl = _gdn_rec_bwd(do, s["wy"], s["qg"], s["kt"], s["vn"], s["qk"], s["gl"], s["states"], nm + "gdn_rec")
    dq, dk, dv, dgb = _gdn_terms_bwd(s["qn"], s["kn"], s["v"], s["gb"], s["t_inv"], dvn, dwy, dkt, dqg, dqk, dgl, nm + "gdn_terms")
    dcq, dck, dcv, dab, g["dn_a_log"], g["dn_dt_bias"] = _gdn_pre_bwd(
        dq, dk, dv, s["cq"], s["ck"], s["cv"], dgb, s["gb"], proj_ab, w["dn_a_log"], w["dn_dt_bias"], nm + "gdn_pre")
    dxs, dws = [], []
    for p, dc in enumerate((dcq, dck, dcv)):
        dxp, dwp = _conv_bwd(dc, proj, QKV_BLK + p, w["dn_conv_w"][:, p * gw:(p + 1) * gw], nm + f"conv{p}")
        dxs.append(dxp)
        dws.append(dwp)
    dconv = jnp.concatenate(dws, axis=1)
    g["dn_conv_w"] = jnp.transpose(dconv.reshape(dconv.shape[0], N_DEV, 3 * gw // N_DEV), (1, 0, 2))
    dproj = jnp.concatenate([du_s5, dzu, dzv, dp] + dxs + [dgate], axis=1)
    xm = s["xm"]
    g["small"] = _unprep_grads(g, small)
    dw_main = _run(hosts, nm + "dw_main", lambda ops: _matmul(
        xm, dproj, mode="tn", tm=1024, tn=1024, tk=2048,out_dtype=wire, name=nm + "dw_main", comm=ops))
    dw_ab = _matmul(xm, dab, mode="tn", tm=1024, tn=AB_PAD, tk=1024, out_dtype=wire, name=nm + "dw_ab")
    dw_in = jnp.concatenate([dw_main, dw_ab[:, :2 * DN_HEADS]], axis=1)
    g["w_in"] = jnp.transpose(dw_in.reshape(D_MODEL, N_DEV, dw_in.shape[1] // N_DEV), (1, 0, 2))
    return _run(hosts, nm + "dx", lambda ops: _matmul(
        dproj, w["w_main"], mode="nt", tm=1024, tn=1024, tk=2048, out_dtype=F32, name=nm + "dx",
        extras=[(dh1, (None, None), lambda i, j: (i, j)), (dab, (None, AB_PAD), lambda i, j: (i, 0)),
                (w["w_ab"], ("tn", AB_PAD), lambda i, j: (j, 0))],
        epi=lambda acc, e, da, wab: acc + ALPHA * e + _dot(_mx(da), _mx(wab), "nt"), comm=ops))


SMALL = ("s5_lambda_re", "s5_lambda_im", "s5_log_step", "s5_b_re", "s5_b_im", "s5_c_re", "s5_c_im", "s5_d", "s5_glu_b",
         "sgu_norm_g", "sgu_norm_b", "sgu_w", "sgu_b", "pool_w", "pool_scale", "dn_a_log", "dn_dt_bias", "dn_norm_g",
         "ln1_g", "ln1_b", "ln2_g", "ln2_b")
SHARDED = ("w_in", "s5_glu_w", "dn_conv_w", "w_out", "w_up", "w_down")


def _pad_lanes(v, width=AB_PAD):
    return jnp.pad(v.reshape(1, -1), ((0, 0), (0, width - v.size)))


def _prep_small(p):
    mx = MXU_DTYPE
    lbr, lbi, bbr, bbi = _s5_discretize(p["s5_lambda_re"], p["s5_lambda_im"], p["s5_log_step"], p["s5_b_re"], p["s5_b_im"])
    b_compact, c_compact = _s5_compact(bbr, bbi, p["s5_c_re"], p["s5_c_im"])
    causal = jnp.tril(jnp.ones((SGU_CHUNK, SGU_CHUNK), F32))
    return dict(
        s5_b=b_compact.astype(mx), s5_c=c_compact.astype(mx),
        s5_lam=jnp.concatenate([lbr.reshape(1, -1), lbi.reshape(1, -1)], axis=1),
        s5_lam_conj=jnp.concatenate([lbr.reshape(1, -1), -lbi.reshape(1, -1)], axis=1),
        s5_d=p["s5_d"].reshape(1, -1), s5_glu_b=p["s5_glu_b"].reshape(1, -1),
        sgu_norm_g=p["sgu_norm_g"].reshape(1, -1), sgu_norm_b=p["sgu_norm_b"].reshape(1, -1),
        sgu_wm=(p["sgu_w"] * causal).astype(mx), sgu_bfull=jnp.repeat(p["sgu_b"].T, GROUP_WIDTH // SGU_HEADS, axis=1),
        pool_w=p["pool_w"].astype(mx), pool_scale=p["pool_scale"].reshape(1, -1),
        dn_a_log=_pad_lanes(p["dn_a_log"]), dn_dt_bias=_pad_lanes(p["dn_dt_bias"]),
        dn_norm_g4=jnp.tile(p["dn_norm_g"].reshape(1, -1), (1, DN_HEADS)),
        ln1_g=p["ln1_g"].reshape(1, -1), ln1_b=p["ln1_b"].reshape(1, -1),
        ln2_g=p["ln2_g"].reshape(1, -1), ln2_b=p["ln2_b"].reshape(1, -1),
    )


def _weight_views(name, t):
    if name == "w_in":
        w_in = jnp.transpose(t, (1, 0, 2)).reshape(t.shape[1], N_DEV * t.shape[2])
        pad = AB_PAD - (w_in.shape[1] - MAIN_COLS)
        return dict(w_main=w_in[:, :MAIN_COLS], w_ab=jnp.pad(w_in[:, MAIN_COLS:], ((0, 0), (0, pad))))
    if name == "dn_conv_w":
        return dict(dn_conv_w=jnp.transpose(t, (1, 0, 2)).reshape(t.shape[1], N_DEV * t.shape[2]))
    if name == "w_up":
        return dict(w_up=t)
    return {name: t.reshape(N_DEV * t.shape[1], t.shape[2])}


def _unprep_grads(g, p):
    causal = jnp.tril(jnp.ones((SGU_CHUNK, SGU_CHUNK), F32))
    dbbr, dbbi = _s5_uncompact_b(g["s5_b"])
    dc_re, dc_im = _s5_uncompact_c(g["s5_c"])
    dlbr, dlbi = g["s5_lam"][0, :S5_NS].reshape(S5_GROUPS, S5_STATE), g["s5_lam"][0, S5_NS:].reshape(S5_GROUPS, S5_STATE)
    _, vjp = jax.vjp(_s5_discretize, p["s5_lambda_re"], p["s5_lambda_im"], p["s5_log_step"], p["s5_b_re"], p["s5_b_im"])
    d_lre, d_lim, d_step, d_bre, d_bim = vjp((dlbr, dlbi, dbbr, dbbi))
    hd = GROUP_WIDTH // SGU_HEADS
    return dict(
        s5_lambda_re=d_lre, s5_lambda_im=d_lim, s5_log_step=d_step, s5_b_re=d_bre, s5_b_im=d_bim, s5_c_re=dc_re, s5_c_im=dc_im,
        s5_d=g["s5_d"].reshape(S5_GROUPS, S5_CH), s5_glu_b=g["s5_glu_b"].reshape(-1),
        sgu_norm_g=g["sgu_norm_g"].reshape(-1), sgu_norm_b=g["sgu_norm_b"].reshape(-1), sgu_w=g["sgu_w"] * causal,
        sgu_b=jnp.sum(g["sgu_bfull"].reshape(SGU_CHUNK, SGU_HEADS, hd), axis=2).T,
        pool_w=g["pool_w"], pool_scale=g["pool_scale"].reshape(-1),
        dn_a_log=g["dn_a_log"][0, :DN_HEADS], dn_dt_bias=g["dn_dt_bias"][0, :DN_HEADS],
        dn_norm_g=jnp.sum(g["dn_norm_g4"].reshape(DN_HEADS, DN_HEAD_DIM), axis=0),
        ln1_g=g["ln1_g"].reshape(-1), ln1_b=g["ln1_b"].reshape(-1), ln2_g=g["ln2_g"].reshape(-1), ln2_b=g["ln2_b"].reshape(-1),
    )


def _local_step(x, target, ops, small, fwd_hosts, bwd_hosts, grads):
    saved = []
    h, hm = x, x.astype(MXU_DTYPE)
    for i in range(DEPTH):
        h, hm, s = _layer_fwd(h, hm, ops[i], i, fwd_hosts)
        saved.append(s)
    loss, dh = _loss_head(h, target)
    for i in reversed(range(DEPTH)):
        dh = _layer_bwd(dh, saved[i], ops[i], small[i], i, bwd_hosts, grads[i])
    return loss, dh


def _adamw(w, gparts, m, v, name):
    rr, c = w.shape
    ng = len(gparts)
    r = rr // ng
    lanes = -(-c // 128) * 128
    tr = r
    while tr * lanes * 4 * N_DEV > (4 << 20) and tr % 16 == 0:
        tr //= 2
    nb = r // tr

    def body(w_ref, *rest):
        g_refs, (m_ref, v_ref, go_ref, d_ref, mo_ref, vo_ref) = rest[:ng], rest[ng:]
        layer = pl.program_id(0)
        g = jnp.zeros(m_ref.shape, F32)
        for li in range(ng):
            gl = g_refs[li][0].astype(F32)
            for s in range(1, N_DEV):
                gl = gl + g_refs[li][s].astype(F32)
            g = jnp.where(layer == li, gl, g)
        mn = ADAM_B1 * m_ref[...] + (1.0 - ADAM_B1) * g
        vn = ADAM_B2 * v_ref[...] + (1.0 - ADAM_B2) * g * g
        m_hat = mn / (1.0 - ADAM_B1 ** ADAM_STEP)
        v_hat = vn / (1.0 - ADAM_B2 ** ADAM_STEP)
        go_ref[...] = g
        d_ref[...] = -ADAM_LR * (m_hat / (jnp.sqrt(v_hat) + ADAM_EPS) + ADAM_WD * w_ref[...])
        mo_ref[...] = mn
        vo_ref[...] = vn

    row = pl.BlockSpec((tr, c), lambda li, i: (li * nb + i, 0))
    part_specs = [pl.BlockSpec((N_DEV, tr, c), functools.partial(lambda li, i, k: (0, jnp.where(li == k, i, 0), 0), k=k))
                  for k in range(ng)]
    return pl.pallas_call(
        body, name=name, grid=(ng, nb), in_specs=[row] + part_specs + [row, row],
        out_specs=[row] * 4, out_shape=[jax.ShapeDtypeStruct((rr, c), F32)] * 4, compiler_params=_params(("arbitrary", "arbitrary")),
    )(w, *gparts, m, v)


PACK_LANES = 128
PACK_ROWS = 8192


PACK_TILE = 8 * PACK_LANES


def _pack_rows(t):
    return -(-t.size // PACK_TILE) * 8


def _pack(vals):
    rows = []
    for t in vals:
        flat = t.reshape(-1)
        n_rows = _pack_rows(t)
        rows.append(jnp.pad(flat, (0, n_rows * PACK_LANES - flat.size)).reshape(n_rows, PACK_LANES))
    used = sum(r.shape[0] for r in rows)
    assert used <= PACK_ROWS, used
    return jnp.concatenate(rows + [jnp.zeros((PACK_ROWS - used, PACK_LANES), F32)], axis=0)


def _unpack(packed, like):
    out, off = [], 0
    for t in like:
        n_rows = _pack_rows(t)
        out.append(packed[off:off + n_rows].reshape(-1)[:t.size].reshape(t.shape))
        off += n_rows
    return out


def kernel(x, w_in, s5_lambda_re, s5_lambda_im, s5_log_step, s5_b_re, s5_b_im, s5_c_re, s5_c_im, s5_d, s5_glu_w, s5_glu_b, sgu_norm_g, sgu_norm_b, sgu_w, sgu_b, pool_w, pool_scale, dn_conv_w, dn_a_log, dn_dt_bias, dn_norm_g, w_out, ln1_g, ln1_b, w_up, w_down, ln2_g, ln2_b, loss_target, m_w_in, m_s5_lambda_re, m_s5_lambda_im, m_s5_log_step, m_s5_b_re, m_s5_b_im, m_s5_c_re, m_s5_c_im, m_s5_d, m_s5_glu_w, m_s5_glu_b, m_sgu_norm_g, m_sgu_norm_b, m_sgu_w, m_sgu_b, m_pool_w, m_pool_scale, m_dn_conv_w, m_dn_a_log, m_dn_dt_bias, m_dn_norm_g, m_w_out, m_ln1_g, m_ln1_b, m_w_up, m_w_down, m_ln2_g, m_ln2_b, v_w_in, v_s5_lambda_re, v_s5_lambda_im, v_s5_log_step, v_s5_b_re, v_s5_b_im, v_s5_c_re, v_s5_c_im, v_s5_d, v_s5_glu_w, v_s5_glu_b, v_sgu_norm_g, v_sgu_norm_b, v_sgu_w, v_sgu_b, v_pool_w, v_pool_scale, v_dn_conv_w, v_dn_a_log, v_dn_dt_bias, v_dn_norm_g, v_w_out, v_ln1_g, v_ln1_b, v_w_up, v_w_down, v_ln2_g, v_ln2_b):
    names = ("w_in", "s5_lambda_re", "s5_lambda_im", "s5_log_step", "s5_b_re", "s5_b_im", "s5_c_re", "s5_c_im", "s5_d", "s5_glu_w",
             "s5_glu_b", "sgu_norm_g", "sgu_norm_b", "sgu_w", "sgu_b", "pool_w", "pool_scale", "dn_conv_w", "dn_a_log", "dn_dt_bias",
             "dn_norm_g", "w_out", "ln1_g", "ln1_b", "w_up", "w_down", "ln2_g", "ln2_b")
    env = locals()
    w = {n: env[n] for n in names}
    m = {n: env["m_" + n] for n in names}
    v = {n: env["v_" + n] for n in names}

    wire = [{n: (w[n][i] if n == "dn_conv_w" else w[n][i].astype(MXU_DTYPE)) for n in SHARDED} for i in range(DEPTH)]
    small = [{n: w[n][i] for n in SMALL} for i in range(DEPTH)]
    ops = [_prep_small(small[i]) for i in range(DEPTH)]
    grads = [{} for _ in range(DEPTH)]
    recv = [{} for _ in range(DEPTH)]
    first = ("w_in", "s5_glu_w", "dn_conv_w", "w_out")

    def gather(layer, group):
        def take(outs):
            for n, t in zip(group, outs):
                ops[layer].update(_weight_views(n, t))
        return (lambda: [(wire[layer][n], False) for n in group]), take

    def scatter(layer, group, with_small=False):
        def make():
            sends = [(grads[layer][n], True) for n in group]
            if with_small:
                sends.append((_pack([jnp.stack([grads[i]["small"][n] for i in range(DEPTH)]) for n in SMALL]), False))
            return sends
        def take(outs):
            recv[layer].update(dict(zip(group + (("small",) if with_small else ()), outs)))
        return make, take

    make, take = gather(0, first)
    take(_exchange(make(), "gather_first"))
    fwd_hosts = {"l0_proj": gather(0, ("w_up",)), "l0_up": gather(0, ("w_down",)), "l0_down": gather(1, first),
                 "l1_proj": gather(1, ("w_up",)), "l1_up": gather(1, ("w_down",))}
    late = ("w_in", "s5_glu_w", "dn_conv_w")
    bwd_hosts = {"l1b_dpre": scatter(1, ("w_down",)), "l1b_dx1": scatter(1, ("w_up",)), "l1b_dmixed": scatter(1, ("w_out",)),
                 "l0b_dw_down": scatter(1, late),
                 "l0b_dpre": scatter(0, ("w_down",)), "l0b_dx1": scatter(0, ("w_up",)), "l0b_dmixed": scatter(0, ("w_out",)),
                 "l0b_dw_main": scatter(0, ("s5_glu_w", "dn_conv_w"), with_small=True), "l0b_dx": scatter(0, ("w_in",))}
    loss, grad_x = _local_step(x[0], loss_target[0], ops, small, fwd_hosts, bwd_hosts, grads)

    g_out, d_out, m_out, v_out = {}, {}, {}, {}
    for n in SHARDED:
        shp = w[n].shape
        pad = (-shp[1]) % 8
        def rows(t):
            return jnp.pad(t, ((0, 0), (0, pad), (0, 0))).reshape(shp[0] * (shp[1] + pad), shp[2])
        res = _adamw(rows(w[n]), [recv[i][n] for i in range(DEPTH)], rows(m[n]), rows(v[n]), "adamw_" + n)
        g_out[n], d_out[n], m_out[n], v_out[n] = (t.reshape(shp[0], shp[1] + pad, shp[2])[:, :shp[1]] for t in res)
    like = [w[n] for n in SMALL]
    res = _adamw(_pack(like), [recv[0]["small"]], _pack([m[n] for n in SMALL]), _pack([v[n] for n in SMALL]), "adamw_small")
    for dst, pk in zip((g_out, d_out, m_out, v_out), res):
        dst.update(dict(zip(SMALL, _unpack(pk, like))))

    total = lax.psum(loss[0, 0], MESH_AXES)
    return (total, grad_x[None], *[g_out[n] for n in names], *[d_out[n] for n in names],
            *[m_out[n] for n in names], *[v_out[n] for n in names])
```

```python
import functools
import math

import jax
import jax.numpy as jnp
from jax import lax
from jax.experimental import pallas as pl
from jax.experimental.pallas import tpu as pltpu

F32 = jnp.float32
BF16 = jnp.bfloat16
MXU_DTYPE = jnp.bfloat16
HI = lax.Precision.HIGHEST

N_DEV = 8
D_MODEL = 2048
DEPTH = 2
GROUP_WIDTH = 512
S5_GROUPS, S5_CH, S5_STATE = 32, 16, 64
S5_NS = S5_GROUPS * S5_STATE
SGU_CHUNK, SGU_HEADS = 128, 8
POOL_WINDOWS = (2, 4, 8, 16)
DN_HEADS, DN_HEAD_DIM, DN_CONV, DN_CHUNK = 4, 128, 4, 64
D_FF = 4 * D_MODEL
LN_EPS, RMS_EPS, L2_EPS = 1e-5, 1e-6, 1e-6
ALPHA = (2 * DEPTH) ** 0.25
MAIN_COLS = 4096
AB_PAD = 128
ADAM_LR, ADAM_B1, ADAM_B2, ADAM_EPS, ADAM_WD, ADAM_STEP = 0.001, 0.9, 0.999, 1e-08, 0.01, 10
VMEM_LIMIT = 56 * 1024 * 1024
MIX_ROWS = 512
WIDE_ROWS = 256
C_GELU = math.sqrt(2.0 / math.pi)


def _params(sem=None):
    return pltpu.CompilerParams(dimension_semantics=sem, vmem_limit_bytes=VMEM_LIMIT)


def _gelu(x):
    return 0.5 * x * (1.0 + jnp.tanh(C_GELU * (x + 0.044715 * x * x * x)))


def _gelu_grad(x):
    t = jnp.tanh(C_GELU * (x + 0.044715 * x * x * x))
    return 0.5 * (1.0 + t) + 0.5 * x * (1.0 - t * t) * C_GELU * (1.0 + 3.0 * 0.044715 * x * x)


def _sigmoid(x):
    return 1.0 / (1.0 + jnp.exp(-x))


def _silu(x):
    return x * _sigmoid(x)


def _silu_grad(x):
    s = _sigmoid(x)
    return s * (1.0 + x * (1.0 - s))


def _softplus(x):
    z = jnp.exp(-jnp.abs(x))
    small = z * (1.0 - z * (0.5 - z * (1.0 / 3.0)))
    return jnp.maximum(x, 0.0) + jnp.where(z < 1e-2, small, jnp.log(1.0 + z))


def _mx(x):
    return x.astype(MXU_DTYPE)


def _dot(a, b, dims="nn", precision=None):
    cd = {"nn": ((1,), (0,)), "nt": ((1,), (1,)), "tn": ((0,), (0,))}[dims]
    return lax.dot_general(a, b, (cd, ((), ())), preferred_element_type=F32, precision=precision)


def _mdot(a, b, dims="nn"):
    return _dot(_mx(a), _mx(b), dims)


MESH_AXES = ("x", "y", "c")
OFFSETS = [(dx, dy, dc) for dx in (0, 1) for dy in (0, 1) for dc in (0, 1)][1:]


def _me_and_peers():
    x, y, c = (lax.axis_index(a) for a in MESH_AXES)
    def flip(v, d):
        return 1 - v if d else v
    peers = [(flip(x, dx), flip(y, dy), flip(c, dc)) for dx, dy, dc in OFFSETS]
    def idx(p):
        return 4 * p[0] + 2 * p[1] + p[2]
    return idx((x, y, c)), peers, [idx(p) for p in peers]


SIBLING = OFFSETS.index((0, 0, 1))
SAME_CORE = [OFFSETS.index(f) for f in ((0, 1, 0), (1, 0, 0), (1, 1, 0))]


class _Comm:
    def __init__(self, ops):
        self.arrays = [a for a, _ in ops]
        self.scatter = [s for _, s in ops]
        self.n = n = len(ops)
        hbm = pl.BlockSpec(memory_space=pltpu.HBM)
        self.in_specs, self.out_specs = [hbm] * n, [hbm] * n
        self.out_shape = [jax.ShapeDtypeStruct(a.shape if s else (N_DEV,) + a.shape, a.dtype) for a, s in ops]
        npeer = len(OFFSETS)
        self.scratch = [pltpu.SemaphoreType.DMA((n, npeer)), pltpu.SemaphoreType.DMA((n, npeer)), pltpu.SemaphoreType.DMA((n,))]

    def _plan(self, ins, outs, sems, waiting):
        send_sems, recv_sems, local_sems = sems
        me, peers, peer_idx = _me_and_peers()

        def remote(k, d, src, dst, to):
            return pltpu.make_async_remote_copy(src_ref=src, dst_ref=dst, send_sem=send_sems.at[k, d], recv_sem=recv_sems.at[k, d],
                                                device_id=to, device_id_type=pl.DeviceIdType.MESH)
        plan = []
        for k in range(self.n):
            every = range(len(OFFSETS))
            if self.scatter[k]:
                local = pltpu.make_async_copy(ins[k].at[me], outs[k].at[me], local_sems.at[k])
                pushes = [remote(k, d, ins[k].at[peer_idx[d]], outs[k].at[me], peers[d]) for d in every]
                onward = []
            else:
                local = pltpu.make_async_copy(ins[k], outs[k].at[me], local_sems.at[k])
                pushes = [remote(k, d, ins[k], outs[k].at[me], peers[d]) for d in [SIBLING] + SAME_CORE]
                onward = SAME_CORE
            passed, arrivals = [], {}
            if waiting:
                passed = [(d, remote(k, d + 1, outs[k].at[peer_idx[d]], outs[k].at[peer_idx[d]], peers[SIBLING])) for d in onward]
                arrivals = {d: remote(k, d, outs[k].at[peer_idx[d]], outs[k].at[peer_idx[d]], peers[d]) for d in every}
            plan.append((local, pushes, passed, arrivals))
        return plan

    def start(self, ins, outs, sems):
        for local, pushes, _, _ in self._plan(ins, outs, sems, False):
            local.start()
            for cp in pushes:
                cp.start()

    def wait(self, ins, outs, sems):
        plan = self._plan(ins, outs, sems, True)
        for _, _, passed, arrivals in plan:
            for d, onward in passed:
                arrivals.pop(d).wait_recv()
                onward.start()
        for local, pushes, passed, arrivals in plan:
            for cp in arrivals.values():
                cp.wait_recv()
            for cp in pushes + [onward for _, onward in passed]:
                cp.wait_send()
            local.wait()


def _exchange(ops, name):
    cm = _Comm(ops)

    def body(*refs):
        ins, outs, sems = refs[:cm.n], refs[cm.n:2 * cm.n], refs[2 * cm.n:]
        cm.start(ins, outs, sems)
        cm.wait(ins, outs, sems)

    return pl.pallas_call(body, name=name, in_specs=cm.in_specs, out_specs=cm.out_specs, out_shape=cm.out_shape,
                          scratch_shapes=cm.scratch)(*cm.arrays)


def _matmul(a, b, *, mode, tm, tn, tk, out_dtype, name, a_fn=None, extras=(), epi=None, a_cols=None,
            b_slab=None, out_slab=None, comm=None):
    a_shape = a.shape if a_cols is None else (a.shape[0], a_cols)
    b_shape = b.shape if b_slab is None else (b.shape[1], N_DEV * b_slab)
    if mode == "nn":
        (m, k), n = a_shape, b_shape[1]
    elif mode == "nt":
        (m, k), n = a_shape, b_shape[0]
    else:
        (k, m), n = a_shape, b_shape[1]
    tm, tn, tk = min(tm, m), min(tn, n), min(tk, k)
    if b_slab is not None:
        tn, tk = (tn, min(tk, b_slab)) if mode == "nt" else (min(tn, b_slab), tk)
    assert m % tm == 0 and n % tn == 0 and k % tk == 0, (name, a.shape, b.shape, tm, tn, tk)
    gi, gj, nk = m // tm, n // tn, k // tk
    n_ex = len(extras)
    cm = _Comm(comm) if comm else None
    nc = cm.n if cm else 0

    def body(a_ref, b_ref, *rest):
        ex_refs, rest = rest[:n_ex], rest[n_ex:]
        c_ins, o_ref, c_outs, acc, sems = rest[:nc], rest[nc], rest[nc + 1:2 * nc + 1], rest[2 * nc + 1], rest[2 * nc + 2:]
        i, j, kk = pl.program_id(0), pl.program_id(1), pl.program_id(2)
        if cm:
            @pl.when((i == 0) & (j == 0) & (kk == 0))
            def _():
                cm.start(c_ins, c_outs, sems)

        av = a_ref[...]
        if a_fn is not None:
            av = a_fn(av)
        part = _dot(_mx(av), _mx(b_ref[...]), mode)

        def finish(r):
            if epi is not None:
                r = epi(r, *[e[...] for e in ex_refs])
            o_ref[...] = r.astype(out_dtype)

        if nk == 1:
            finish(part)
        else:
            @pl.when(kk == 0)
            def _():
                acc[...] = part

            @pl.when((kk > 0) & (kk < nk - 1))
            def _():
                acc[...] += part

            @pl.when(kk == nk - 1)
            def _():
                finish(acc[...] + part)

        if cm:
            @pl.when((i == gi - 1) & (j == gj - 1) & (kk == nk - 1))
            def _():
                cm.wait(c_ins, c_outs, sems)

    a_spec = pl.BlockSpec((tk, tm), lambda i, j, kk: (kk, i)) if mode == "tn" else pl.BlockSpec((tm, tk), lambda i, j, kk: (i, kk))
    if b_slab is None:
        b_spec = pl.BlockSpec((tn, tk), lambda i, j, kk: (j, kk)) if mode == "nt" else pl.BlockSpec((tk, tn), lambda i, j, kk: (kk, j))
    elif mode == "nt":
        assert b_slab % tk == 0
        b_spec = pl.BlockSpec((None, tn, tk), lambda i, j, kk: ((kk * tk) // b_slab, j, ((kk * tk) % b_slab) // tk))
    else:
        assert b_slab % tn == 0
        b_spec = pl.BlockSpec((None, tk, tn), lambda i, j, kk: ((j * tn) // b_slab, kk, ((j * tn) % b_slab) // tn))
    if out_slab is None:
        o_spec, o_shape = pl.BlockSpec((tm, tn), lambda i, j, kk: (i, j)), jax.ShapeDtypeStruct((m, n), out_dtype)
    else:
        assert out_slab % tn == 0 and n == N_DEV * out_slab
        o_spec = pl.BlockSpec((None, tm, tn), lambda i, j, kk: ((j * tn) // out_slab, i, ((j * tn) % out_slab) // tn))
        o_shape = jax.ShapeDtypeStruct((N_DEV, m, out_slab), out_dtype)
    ex_specs = [pl.BlockSpec(({None: tm, "tn": tn}.get(bs[0], bs[0]), tn if bs[1] is None else bs[1]),
                             functools.partial(lambda i, j, kk, f: f(i, j), f=im)) for (_, bs, im) in extras]
    res = pl.pallas_call(
        body,
        name=name,
        grid=(gi, gj, nk),
        in_specs=[a_spec, b_spec, *ex_specs] + (cm.in_specs if cm else []),
        out_specs=[o_spec] + (cm.out_specs if cm else []),
        out_shape=[o_shape] + (cm.out_shape if cm else []),
        scratch_shapes=[pltpu.VMEM((tm, tn) if nk > 1 else (8, 128), F32)] + (cm.scratch if cm else []),
        compiler_params=_params(("arbitrary",) * 3 if cm else ("parallel", "parallel", "arbitrary")),
    )(a, b, *[e[0] for e in extras], *(cm.arrays if cm else []))
    return (res[0], res[1:]) if cm else res[0]


def _ln_fwd(x, y, g, b, name):
    l, d = x.shape
    tl = min(WIDE_ROWS, l)

    def body(x_ref, y_ref, g_ref, b_ref, h_ref, o_ref, om_ref):
        h = ALPHA * x_ref[...] + y_ref[...]
        mu = jnp.mean(h, axis=-1, keepdims=True)
        c = h - mu
        var = jnp.mean(c * c, axis=-1, keepdims=True)
        h_ref[...] = h
        out = c * lax.rsqrt(var + LN_EPS) * g_ref[...] + b_ref[...]
        o_ref[...] = out
        om_ref[...] = out.astype(om_ref.dtype)

    row = pl.BlockSpec((tl, d), lambda i: (i, 0))
    vec = pl.BlockSpec((1, d), lambda i: (0, 0))
    return pl.pallas_call(
        body, name=name, grid=(l // tl,), in_specs=[row, row, vec, vec], out_specs=[row, row, row],
        out_shape=[jax.ShapeDtypeStruct((l, d), F32)] * 2 + [jax.ShapeDtypeStruct((l, d), MXU_DTYPE)],
        compiler_params=_params(("parallel",)),
    )(x, y, g, b)


def _ln_bwd(dout, h, g, name):
    l, d = h.shape
    tl = min(WIDE_ROWS, l)

    def body(do_ref, h_ref, g_ref, dh_ref, dhm_ref, dg_ref, db_ref):
        @pl.when(pl.program_id(0) == 0)
        def _():
            dg_ref[...] = jnp.zeros_like(dg_ref)
            db_ref[...] = jnp.zeros_like(db_ref)

        hv, do = h_ref[...], do_ref[...]
        mu = jnp.mean(hv, axis=-1, keepdims=True)
        c = hv - mu
        r = lax.rsqrt(jnp.mean(c * c, axis=-1, keepdims=True) + LN_EPS)
        xh = c * r
        dxh = do * g_ref[...]
        m1 = jnp.mean(dxh, axis=-1, keepdims=True)
        m2 = jnp.mean(dxh * xh, axis=-1, keepdims=True)
        dh = r * (dxh - m1 - xh * m2)
        dh_ref[...] = dh
        dhm_ref[...] = dh.astype(dhm_ref.dtype)
        dg_ref[...] += jnp.sum(do * xh, axis=0, keepdims=True)
        db_ref[...] += jnp.sum(do, axis=0, keepdims=True)

    row = pl.BlockSpec((tl, d), lambda i: (i, 0))
    vec = pl.BlockSpec((1, d), lambda i: (0, 0))
    return pl.pallas_call(
        body, name=name, grid=(l // tl,), in_specs=[row, row, vec], out_specs=[row, row, vec, vec],
        out_shape=[jax.ShapeDtypeStruct((l, d), F32), jax.ShapeDtypeStruct((l, d), MXU_DTYPE),
                   jax.ShapeDtypeStruct((1, d), F32), jax.ShapeDtypeStruct((1, d), F32)],
        compiler_params=_params(("arbitrary",)),
    )(dout, h, g)


def _loss_head(y, target):
    l, d = y.shape
    tl = min(WIDE_ROWS, l)

    def body(y_ref, t_ref, loss_ref, dy_ref):
        @pl.when(pl.program_id(0) == 0)
        def _():
            loss_ref[...] = jnp.zeros_like(loss_ref)

        e = y_ref[...] - t_ref[...]
        dy_ref[...] = e * (1.0 / d)
        s = jnp.sum(jnp.sum(e * e, axis=1, keepdims=True), axis=0, keepdims=True)
        loss_ref[...] += s * (0.5 / d)

    row = pl.BlockSpec((tl, d), lambda i: (i, 0))
    return pl.pallas_call(
        body, name="loss_head", grid=(l // tl,), in_specs=[row, row],
        out_specs=[pl.BlockSpec((1, 1), lambda i: (0, 0)), row],
        out_shape=[jax.ShapeDtypeStruct((1, 1), F32), jax.ShapeDtypeStruct((l, d), F32)],
        compiler_params=_params(("arbitrary",)),
    )(y, target)


def _s5_discretize(lam_re, lam_im, log_step, b_re, b_im):
    step = jnp.exp(log_step)[:, None]
    e = jnp.exp(lam_re * step)
    lbr, lbi = e * jnp.cos(lam_im * step), e * jnp.sin(lam_im * step)
    den = lam_re * lam_re + lam_im * lam_im
    qr = ((lbr - 1.0) * lam_re + lbi * lam_im) / den
    qi = (lbi * lam_re - (lbr - 1.0) * lam_im) / den
    bbr = qr[:, :, None] * b_re - qi[:, :, None] * b_im
    bbi = qr[:, :, None] * b_im + qi[:, :, None] * b_re
    return lbr, lbi, bbr, bbi


S5_TILES, S5_SLABS = 4, 8
S5_TILE_W, S5_SLAB_W = GROUP_WIDTH // S5_TILES, S5_NS // S5_TILES
S5_GPT = S5_GROUPS // S5_TILES


def _s5_compact(bbr, bbi, c_re, c_im):
    eye = jnp.eye(S5_GPT, dtype=F32)
    def bd(t):
        return jnp.einsum("tgph,gk->tghkp", t.reshape(S5_TILES, S5_GPT, S5_STATE, S5_CH), eye).reshape(S5_TILES, S5_TILE_W, S5_SLAB_W)
    def cd(t):
        return jnp.einsum("tghp,gk->tgpkh", t.reshape(S5_TILES, S5_GPT, S5_CH, S5_STATE), eye).reshape(S5_TILES, S5_SLAB_W, S5_TILE_W)
    return jnp.concatenate([bd(bbr), bd(bbi)], axis=0), jnp.concatenate([cd(c_re), -cd(c_im)], axis=0)


def _s5_uncompact_b(db):
    eye = jnp.eye(S5_GPT, dtype=F32)[None, :, None, :, None]
    def ex(t):
        d = jnp.sum(t.reshape(S5_TILES, S5_GPT, S5_CH, S5_GPT, S5_STATE) * eye, axis=3)
        return jnp.transpose(d, (0, 1, 3, 2)).reshape(S5_GROUPS, S5_STATE, S5_CH)
    return ex(db[:S5_TILES]), ex(db[S5_TILES:])


def _s5_uncompact_c(dc):
    eye = jnp.eye(S5_GPT, dtype=F32)[None, :, None, :, None]
    def ex(t):
        d = jnp.sum(t.reshape(S5_TILES, S5_GPT, S5_STATE, S5_GPT, S5_CH) * eye, axis=3)
        return jnp.transpose(d, (0, 1, 3, 2)).reshape(S5_GROUPS, S5_CH, S5_STATE)
    return ex(dc[:S5_TILES]), -ex(dc[S5_TILES:])


S5_ROWS = 512


def _s5_tile(j):
    t = j % S5_TILES
    return slice(t * S5_TILE_W, (t + 1) * S5_TILE_W)


def _s5_slab(j):
    return slice(j * S5_SLAB_W, (j + 1) * S5_SLAB_W)


def _s5_recur(src, lam_ref, carry, emit, n_rows, reverse, extra=()):
    ns = S5_NS
    lr, li = lam_ref[:, :ns], lam_ref[:, ns:]

    def step(t, c):
        row = (n_rows - 1 - t) if reverse else t
        cr, ci = c[0], c[1]
        nr = lr * cr - li * ci + src[pl.ds(row, 1), :ns]
        ni = lr * ci + li * cr + src[pl.ds(row, 1), ns:]
        return (nr, ni) + tuple(emit(row, nr, ni, cr, ci, c[2:]))

    fin = lax.fori_loop(0, n_rows, step, (carry[:, :ns], carry[:, ns:]) + tuple(extra))
    carry[:, :ns] = fin[0]
    carry[:, ns:] = fin[1]
    return fin[2:]


def _s5_fwd(proj, b, c, lam, d, name):
    l = proj.shape[0]
    tl = min(S5_ROWS, l)
    w = 2 * S5_NS

    def body(u_ref, b_ref, c_ref, lam_ref, d_ref, hs_ref, y_ref, bu, carry):
        @pl.when(pl.program_id(0) == 0)
        def _():
            carry[...] = jnp.zeros_like(carry)

        u = u_ref[...]
        um = _mx(u)
        for j in range(S5_SLABS):
            bu[:, _s5_slab(j)] = _dot(um[:, _s5_tile(j)], b_ref[j])

        def emit(row, nr, ni, cr, ci, extra):
            hs_ref[pl.ds(row, 1), :S5_NS] = nr
            hs_ref[pl.ds(row, 1), S5_NS:] = ni
            return extra

        _s5_recur(bu, lam_ref, carry, emit, tl, False)
        for t in range(S5_TILES):
            acc = _dot(_mx(hs_ref[:, _s5_slab(t)]), c_ref[t]) + _dot(_mx(hs_ref[:, _s5_slab(S5_TILES + t)]), c_ref[S5_TILES + t])
            y_ref[:, _s5_tile(t)] = acc + d_ref[:, _s5_tile(t)] * u[:, _s5_tile(t)]

    row = lambda width: pl.BlockSpec((tl, width), lambda i: (i, 0))
    full = lambda a: pl.BlockSpec(a.shape, lambda i: (0,) * a.ndim)
    return pl.pallas_call(
        body, name=name, grid=(l // tl,), in_specs=[row(GROUP_WIDTH), full(b), full(c), full(lam), full(d)],
        out_specs=[row(w), row(GROUP_WIDTH)],
        out_shape=[jax.ShapeDtypeStruct((l, w), F32), jax.ShapeDtypeStruct((l, GROUP_WIDTH), F32)],
        scratch_shapes=[pltpu.VMEM((tl, w), F32), pltpu.VMEM((1, w), F32)], compiler_params=_params(("arbitrary",)),
    )(proj, b, c, lam, d)


def _s5_bwd(dy, hs, proj, b, c, lam_conj, d, name):
    l = dy.shape[0]
    tl = min(S5_ROWS, l)
    nb = l // tl
    w = 2 * S5_NS

    def body(dy_ref, hs_ref, u_ref, b_ref, c_ref, lam_ref, d_ref, du_ref, db_ref, dc_ref, dl_ref, dh, adj, carry):
        @pl.when(pl.program_id(0) == 0)
        def _():
            carry[...] = jnp.zeros_like(carry)
            db_ref[...] = jnp.zeros_like(db_ref)
            dc_ref[...] = jnp.zeros_like(dc_ref)
            dl_ref[...] = jnp.zeros_like(dl_ref)

        dyv = dy_ref[...]
        dym, um = _mx(dyv), _mx(u_ref[...])
        for j in range(S5_SLABS):
            dh[:, _s5_slab(j)] = _dot(dym[:, _s5_tile(j)], c_ref[j], "nt")

        def emit(row, nr, ni, cr, ci, extra):
            adj[pl.ds(row, 1), :S5_NS] = nr
            adj[pl.ds(row, 1), S5_NS:] = ni
            hr, hi = hs_ref[pl.ds(row, 1), :S5_NS], hs_ref[pl.ds(row, 1), S5_NS:]
            return extra[0] + cr * hr + ci * hi, extra[1] + ci * hr - cr * hi

        dl = _s5_recur(dh, lam_ref, carry, emit, tl, True, extra=(dl_ref[:, :S5_NS], dl_ref[:, S5_NS:]))
        dl_ref[:, :S5_NS] = dl[0]
        dl_ref[:, S5_NS:] = dl[1]
        for t in range(S5_TILES):
            acc = (_dot(_mx(adj[:, _s5_slab(t)]), b_ref[t], "nt")
                   + _dot(_mx(adj[:, _s5_slab(S5_TILES + t)]), b_ref[S5_TILES + t], "nt"))
            du_ref[:, _s5_tile(t)] = (acc + d_ref[:, _s5_tile(t)] * dyv[:, _s5_tile(t)]).astype(du_ref.dtype)
        for j in range(S5_SLABS):
            dc_ref[j] += _dot(_mx(hs_ref[:, _s5_slab(j)]), dym[:, _s5_tile(j)], "tn")
            db_ref[j] += _dot(um[:, _s5_tile(j)], _mx(adj[:, _s5_slab(j)]), "tn")

    row = lambda width: pl.BlockSpec((tl, width), lambda i: (nb - 1 - i, 0))
    full = lambda a: pl.BlockSpec(a.shape, lambda i: (0,) * a.ndim)
    acc3 = lambda shape: pl.BlockSpec(shape, lambda i: (0, 0, 0))
    return pl.pallas_call(
        body, name=name, grid=(nb,),
        in_specs=[row(GROUP_WIDTH), row(w), row(GROUP_WIDTH), full(b), full(c), full(lam_conj), full(d)],
        out_specs=[row(GROUP_WIDTH), acc3(b.shape), acc3(c.shape), pl.BlockSpec((1, w), lambda i: (0, 0))],
        out_shape=[jax.ShapeDtypeStruct((l, GROUP_WIDTH), BF16), jax.ShapeDtypeStruct(b.shape, F32),
                   jax.ShapeDtypeStruct(c.shape, F32), jax.ShapeDtypeStruct((1, w), F32)],
        scratch_shapes=[pltpu.VMEM((tl, w), F32), pltpu.VMEM((tl, w), F32), pltpu.VMEM((1, w), F32)],
        compiler_params=_params(("arbitrary",)),
    )(dy, hs, proj, b, c, lam_conj, d)


def _s5_glu_fwd(y, glu_w, glu_b, name):
    l, d = y.shape
    tl = min(2 * MIX_ROWS, l)

    def body(y_ref, w_ref, b_ref, o_ref):
        yg = _gelu(y_ref[...])
        z = _mdot(yg, w_ref[...]) + b_ref[...]
        o_ref[...] = (yg * _sigmoid(z)).astype(o_ref.dtype)

    return pl.pallas_call(
        body, name=name, grid=(l // tl,),
        in_specs=[pl.BlockSpec((tl, d), lambda i: (i, 0)), pl.BlockSpec((d, d), lambda i: (0, 0)), pl.BlockSpec((1, d), lambda i: (0, 0))],
        out_specs=pl.BlockSpec((tl, d), lambda i: (i, 0)), out_shape=jax.ShapeDtypeStruct((l, d), BF16),
        compiler_params=_params(("parallel",)),
    )(y, glu_w, glu_b)


def _s5_glu_bwd(dmixed, y, proj, glu_w, glu_b, name):
    l, d = y.shape
    tl = min(2 * MIX_ROWS, l)

    def body(do_ref, y_ref, u_ref, w_ref, b_ref, dy_ref, dz_ref, yg_ref, db_ref, dd_ref):
        @pl.when(pl.program_id(0) == 0)
        def _():
            db_ref[...] = jnp.zeros_like(db_ref)
            dd_ref[...] = jnp.zeros_like(dd_ref)

        yv, do = y_ref[...], do_ref[...]
        yg = _gelu(yv)
        gate = _sigmoid(_mdot(yg, w_ref[...]) + b_ref[...])
        dz = do * yg * gate * (1.0 - gate)
        dyg = do * gate + _mdot(dz, w_ref[...], "nt")
        dy = dyg * _gelu_grad(yv)
        dy_ref[...] = dy
        dz_ref[...] = dz.astype(dz_ref.dtype)
        yg_ref[...] = yg.astype(yg_ref.dtype)
        db_ref[...] += jnp.sum(dz, axis=0, keepdims=True)
        dd_ref[...] += jnp.sum(dy * u_ref[...], axis=0, keepdims=True)

    row = pl.BlockSpec((tl, d), lambda i: (i, 0))
    vec = pl.BlockSpec((1, d), lambda i: (0, 0))
    return pl.pallas_call(
        body, name=name, grid=(l // tl,),
        in_specs=[row, row, row, pl.BlockSpec((d, d), lambda i: (0, 0)), vec],
        out_specs=[row, row, row, vec, vec],
        out_shape=[jax.ShapeDtypeStruct((l, d), F32), jax.ShapeDtypeStruct((l, d), BF16), jax.ShapeDtypeStruct((l, d), BF16),
                   jax.ShapeDtypeStruct((1, d), F32), jax.ShapeDtypeStruct((1, d), F32)],
        compiler_params=_params(("arbitrary",)),
    )(dmixed, y, proj, glu_w, glu_b)


def _sgu_pair(w_ref, x, j, dims):
    lo = lax.broadcasted_iota(jnp.int32, x.shape, 1) < (GROUP_WIDTH // SGU_HEADS)
    xb = _mx(x)
    r0 = _dot(w_ref[2 * j], xb, dims)
    r1 = _dot(w_ref[2 * j + 1], xb, dims)
    return jnp.where(lo, r0, r1)


def _sgu_norm(v, g, b):
    mu = jnp.mean(v, axis=-1, keepdims=True)
    c = v - mu
    r = lax.rsqrt(jnp.mean(c * c, axis=-1, keepdims=True) + LN_EPS)
    return c * r, r


def _sgu_fwd(proj, norm_g, norm_b, wm, bfull, name):
    l = proj.shape[0]
    tl = min(MIX_ROWS, l)
    gw = GROUP_WIDTH

    def body(zu_ref, zv_ref, g_ref, b_ref, w_ref, bf_ref, o_ref):
        for c in range(tl // SGU_CHUNK):
            rows = slice(c * SGU_CHUNK, (c + 1) * SGU_CHUNK)
            u = _gelu(zu_ref[rows, :])
            vh, _ = _sgu_norm(_gelu(zv_ref[rows, :]), None, None)
            vn = vh * g_ref[...] + b_ref[...]
            for j in range(gw // 128):
                cols = slice(j * 128, (j + 1) * 128)
                mixed = _sgu_pair(w_ref, vn[:, cols], j, "nn") + bf_ref[:, cols]
                o_ref[rows, cols] = (u[:, cols] * mixed).astype(o_ref.dtype)

    vec = pl.BlockSpec((1, gw), lambda i: (0, 0))
    return pl.pallas_call(
        body, name=name, grid=(l // tl,),
        in_specs=[pl.BlockSpec((tl, gw), lambda i: (i, 1)), pl.BlockSpec((tl, gw), lambda i: (i, 2)), vec, vec,
                  pl.BlockSpec((SGU_HEADS, SGU_CHUNK, SGU_CHUNK), lambda i: (0, 0, 0)), pl.BlockSpec((SGU_CHUNK, gw), lambda i: (0, 0))],
        out_specs=pl.BlockSpec((tl, gw), lambda i: (i, 0)), out_shape=jax.ShapeDtypeStruct((l, gw), BF16),
        compiler_params=_params(("parallel",)),
    )(proj, proj, norm_g, norm_b, wm, bfull)


def _sgu_bwd(dmixed, proj, norm_g, norm_b, wm, bfull, name):
    l = proj.shape[0]
    tl = min(MIX_ROWS, l)
    gw = GROUP_WIDTH
    hd = gw // SGU_HEADS

    def body(do_ref, zu_ref, zv_ref, g_ref, b_ref, w_ref, bf_ref, dzu_ref, dzv_ref, dw_ref, dbf_ref, dg_ref, dnb_ref):
        @pl.when(pl.program_id(0) == 0)
        def _():
            dw_ref[...] = jnp.zeros_like(dw_ref)
            dbf_ref[...] = jnp.zeros_like(dbf_ref)
            dg_ref[...] = jnp.zeros_like(dg_ref)
            dnb_ref[...] = jnp.zeros_like(dnb_ref)

        for c in range(tl // SGU_CHUNK):
            rows = slice(c * SGU_CHUNK, (c + 1) * SGU_CHUNK)
            zu, zv, do = zu_ref[rows, :], zv_ref[rows, :], do_ref[rows, :]
            u = _gelu(zu)
            vh, r = _sgu_norm(_gelu(zv), None, None)
            vn = vh * g_ref[...] + b_ref[...]
            dvn_parts, mixed_parts = [], []
            for j in range(gw // 128):
                cols = slice(j * 128, (j + 1) * 128)
                vb = vn[:, cols]
                mixed_parts.append(_sgu_pair(w_ref, vb, j, "nn") + bf_ref[:, cols])
                dm = do[:, cols] * u[:, cols]
                dvn_parts.append(_sgu_pair(w_ref, dm, j, "tn"))
                lo = lax.broadcasted_iota(jnp.int32, dm.shape, 1) < hd
                dw_ref[2 * j] += _mdot(jnp.where(lo, dm, 0.0), vb, "nt")
                dw_ref[2 * j + 1] += _mdot(jnp.where(lo, 0.0, dm), vb, "nt")
                dbf_ref[:, cols] += dm
            mixed = jnp.concatenate(mixed_parts, axis=1)
            dvn = jnp.concatenate(dvn_parts, axis=1)
            dzu_ref[rows, :] = (do * mixed * _gelu_grad(zu)).astype(dzu_ref.dtype)
            dg_ref[...] += jnp.sum(dvn * vh, axis=0, keepdims=True)
            dnb_ref[...] += jnp.sum(dvn, axis=0, keepdims=True)
            dvh = dvn * g_ref[...]
            m1 = jnp.mean(dvh, axis=-1, keepdims=True)
            m2 = jnp.mean(dvh * vh, axis=-1, keepdims=True)
            dv = r * (dvh - m1 - vh * m2)
            dzv_ref[rows, :] = (dv * _gelu_grad(zv)).astype(dzv_ref.dtype)

    vec = pl.BlockSpec((1, gw), lambda i: (0, 0))
    row = pl.BlockSpec((tl, gw), lambda i: (i, 0))
    wspec = pl.BlockSpec((SGU_HEADS, SGU_CHUNK, SGU_CHUNK), lambda i: (0, 0, 0))
    bspec = pl.BlockSpec((SGU_CHUNK, gw), lambda i: (0, 0))
    return pl.pallas_call(
        body, name=name, grid=(l // tl,),
        in_specs=[pl.BlockSpec((tl, gw), lambda i: (i, 1)), pl.BlockSpec((tl, gw), lambda i: (i, 1)), pl.BlockSpec((tl, gw), lambda i: (i, 2)),
                  vec, vec, wspec, bspec],
        out_specs=[row, row, wspec, bspec, vec, vec],
        out_shape=[jax.ShapeDtypeStruct((l, gw), BF16), jax.ShapeDtypeStruct((l, gw), BF16),
                   jax.ShapeDtypeStruct((SGU_HEADS, SGU_CHUNK, SGU_CHUNK), F32), jax.ShapeDtypeStruct((SGU_CHUNK, gw), F32),
                   jax.ShapeDtypeStruct((1, gw), F32), jax.ShapeDtypeStruct((1, gw), F32)],
        compiler_params=_params(("arbitrary",)),
    )(dmixed, proj, proj, norm_g, norm_b, wm, bfull)


HALO = 16


def _window_sums(ext, n_rows, forward):
    def sh(x, k):
        return pltpu.roll(x, (n_rows - k) if forward else k, axis=0)
    s2 = ext + sh(ext, 1)
    s4 = s2 + sh(s2, 2)
    s8 = s4 + sh(s4, 4)
    s16 = s8 + sh(s8, 8)
    return (s2, s4, s8, s16)


def _pool_fwd(proj, pool_w, scale, name):
    l = proj.shape[0]
    tl = min(MIX_ROWS, l)
    gw = GROUP_WIDTH
    pg = gw // len(POOL_WINDOWS)

    def body(x_ref, halo_ref, w_ref, s_ref, o_ref, p_ref):
        i = pl.program_id(0)
        x = x_ref[...]
        halo = jnp.where(i > 0, halo_ref[...], 0.0)
        ext = jnp.concatenate([halo, x], axis=0)
        sums = _window_sums(ext, tl + HALO, False)
        t = i * tl + lax.broadcasted_iota(jnp.int32, (tl, pg), 0)
        for gi, win in enumerate(POOL_WINDOWS):
            cols = slice(gi * pg, (gi + 1) * pg)
            cnt = jnp.minimum(t + 1, win).astype(F32)
            pooled = sums[gi][HALO:, cols] / cnt - x[:, cols]
            p_ref[:, cols] = pooled
            o_ref[:, cols] = (_mdot(pooled, w_ref[gi]) * s_ref[:, cols]).astype(o_ref.dtype)

    row = pl.BlockSpec((tl, gw), lambda i: (i, 0))
    return pl.pallas_call(
        body, name=name, grid=(l // tl,),
        in_specs=[pl.BlockSpec((tl, gw), lambda i: (i, 3)),
                  pl.BlockSpec((HALO, gw), lambda i: (jnp.maximum(i * (tl // HALO) - 1, 0), 3)),
                  pl.BlockSpec((len(POOL_WINDOWS), pg, pg), lambda i: (0, 0, 0)), pl.BlockSpec((1, gw), lambda i: (0, 0))],
        out_specs=[row, row], out_shape=[jax.ShapeDtypeStruct((l, gw), BF16), jax.ShapeDtypeStruct((l, gw), F32)],
        compiler_params=_params(("parallel",)),
    )(proj, proj, pool_w, scale)


def _pool_bwd_map(dmixed, pooled, pool_w, scale, name):
    l, gw = pooled.shape
    tl = min(MIX_ROWS, l)
    ng = len(POOL_WINDOWS)
    pg = gw // ng

    def body(do_ref, p_ref, w_ref, s_ref, dp_ref, dw_ref, ds_ref):
        @pl.when(pl.program_id(0) == 0)
        def _():
            dw_ref[...] = jnp.zeros_like(dw_ref)
            ds_ref[...] = jnp.zeros_like(ds_ref)

        for gi in range(ng):
            cols = slice(gi * pg, (gi + 1) * pg)
            do, pooled_g = do_ref[:, cols], p_ref[:, cols]
            mixed = _mdot(pooled_g, w_ref[gi])
            ds_ref[:, cols] += jnp.sum(do * mixed, axis=0, keepdims=True)
            dm = do * s_ref[:, cols]
            dw_ref[gi] += _mdot(pooled_g, dm, "tn")
            dp_ref[:, cols] = _mdot(dm, w_ref[gi], "nt")

    row = pl.BlockSpec((tl, gw), lambda i: (i, 0))
    wspec = pl.BlockSpec((ng, pg, pg), lambda i: (0, 0, 0))
    vec = pl.BlockSpec((1, gw), lambda i: (0, 0))
    return pl.pallas_call(
        body, name=name, grid=(l // tl,),
        in_specs=[pl.BlockSpec((tl, gw), lambda i: (i, 2)), row, wspec, vec], out_specs=[row, wspec, vec],
        out_shape=[jax.ShapeDtypeStruct((l, gw), F32), jax.ShapeDtypeStruct((ng, pg, pg), F32), jax.ShapeDtypeStruct((1, gw), F32)],
        compiler_params=_params(("arbitrary",)),
    )(dmixed, pooled, pool_w, scale)


def _pool_bwd_window(dpooled, name):
    l, gw = dpooled.shape
    tl = min(MIX_ROWS, l)
    nb = l // tl
    pg = gw // len(POOL_WINDOWS)

    def body(d_ref, halo_ref, o_ref):
        i = pl.program_id(0)
        d = d_ref[...]
        halo = jnp.where(i < nb - 1, halo_ref[...], 0.0)
        ext = jnp.concatenate([d, halo], axis=0)
        t = i * tl + lax.broadcasted_iota(jnp.int32, (tl + HALO, pg), 0)
        for gi, win in enumerate(POOL_WINDOWS):
            cols = slice(gi * pg, (gi + 1) * pg)
            cnt = jnp.minimum(t + 1, win).astype(F32)
            sums = _window_sums(ext[:, cols] / cnt, tl + HALO, True)
            o_ref[:, cols] = (sums[gi][:tl, :] - d[:, cols]).astype(o_ref.dtype)

    row = pl.BlockSpec((tl, gw), lambda i: (i, 0))
    return pl.pallas_call(
        body, name=name, grid=(nb,),
        in_specs=[row, pl.BlockSpec((HALO, gw), lambda i: (jnp.minimum((i + 1) * (tl // HALO), l // HALO - 1), 0))],
        out_specs=row, out_shape=jax.ShapeDtypeStruct((l, gw), BF16), compiler_params=_params(("parallel",)),
    )(dpooled, dpooled)


CONV_HALO = 8
QKV_BLK = 4


def _head_sums(x):
    parts = []
    for hd in range(DN_HEADS):
        s = jnp.sum(x[:, hd * DN_HEAD_DIM:(hd + 1) * DN_HEAD_DIM], axis=-1, keepdims=True)
        parts.append(jnp.broadcast_to(s, (x.shape[0], DN_HEAD_DIM)))
    return jnp.concatenate(parts, axis=1)


def _gdn_pre_fwd(proj, proj_ab, conv_w, a_log, dt_bias, name):
    l = proj.shape[0]
    tl = min(MIX_ROWS, l)
    gw = GROUP_WIDTH

    def body(xq, xk, xv, hq, hk, hv, w_ref, ab_ref, al_ref, dt_ref, qn_ref, kn_ref, v_ref, cq_ref, ck_ref, cv_ref, gb_ref):
        i = pl.program_id(0)
        for p, (x_ref, h_ref, c_ref) in enumerate(((xq, hq, cq_ref), (xk, hk, ck_ref), (xv, hv, cv_ref))):
            ext = jnp.concatenate([jnp.where(i > 0, h_ref[...], 0.0), x_ref[...]], axis=0)
            conv = jnp.zeros((tl, gw), F32)
            for j in range(DN_CONV):
                k = DN_CONV - 1 - j
                shifted = ext if k == 0 else pltpu.roll(ext, k, axis=0)
                conv = conv + shifted[CONV_HALO:, :] * w_ref[j:j + 1, p * gw:(p + 1) * gw]
            c_ref[...] = conv
            s = _silu(conv)
            if p == 2:
                v_ref[...] = s
            else:
                r = lax.rsqrt(_head_sums(s * s) + L2_EPS)
                (qn_ref if p == 0 else kn_ref)[...] = s * r * (DN_HEAD_DIM ** -0.5 if p == 0 else 1.0)
        ab = ab_ref[...]
        lane = lax.broadcasted_iota(jnp.int32, ab.shape, 1)
        g = -jnp.exp(al_ref[...]) * _softplus(ab + dt_ref[...])
        gb_ref[...] = jnp.where(lane < DN_HEADS, g, _sigmoid(ab))

    def xs(b):
        return pl.BlockSpec((tl, gw), lambda i: (i, b))

    def hs(b):
        return pl.BlockSpec((CONV_HALO, gw), lambda i: (jnp.maximum(i * (tl // CONV_HALO) - 1, 0), b))

    row = pl.BlockSpec((tl, gw), lambda i: (i, 0))
    abrow = pl.BlockSpec((tl, AB_PAD), lambda i: (i, 0))
    abvec = pl.BlockSpec((1, AB_PAD), lambda i: (0, 0))
    return pl.pallas_call(
        body, name=name, grid=(l // tl,),
        in_specs=[xs(QKV_BLK), xs(QKV_BLK + 1), xs(QKV_BLK + 2), hs(QKV_BLK), hs(QKV_BLK + 1), hs(QKV_BLK + 2),
                  pl.BlockSpec((DN_CONV, 3 * gw), lambda i: (0, 0)), abrow, abvec, abvec],
        out_specs=[row] * 6 + [abrow],
        out_shape=[jax.ShapeDtypeStruct((l, gw), F32)] * 6 + [jax.ShapeDtypeStruct((l, AB_PAD), F32)],
        compiler_params=_params(("parallel",)),
    )(proj, proj, proj, proj, proj, proj, conv_w, proj_ab, a_log, dt_bias)


def _gdn_pre_bwd(dq, dk, dv, cq, ck, cv, dgb, gb, proj_ab, a_log, dt_bias, name):
    l, gw = cq.shape
    tl = min(MIX_ROWS, l)

    def body(dq_ref, dk_ref, dv_ref, cq_ref, ck_ref, cv_ref, dgb_ref, gb_ref, ab_ref, al_ref, dt_ref,
             dcq_ref, dck_ref, dcv_ref, dab_ref, dal_ref, ddt_ref):
        @pl.when(pl.program_id(0) == 0)
        def _():
            dal_ref[...] = jnp.zeros_like(dal_ref)
            ddt_ref[...] = jnp.zeros_like(ddt_ref)

        for p, (d_ref, c_ref, o_ref) in enumerate(((dq_ref, cq_ref, dcq_ref), (dk_ref, ck_ref, dck_ref), (dv_ref, cv_ref, dcv_ref))):
            c, d = c_ref[...], d_ref[...]
            if p == 2:
                ds = d
            else:
                s = _silu(c)
                r = lax.rsqrt(_head_sums(s * s) + L2_EPS)
                ds = (DN_HEAD_DIM ** -0.5 if p == 0 else 1.0) * r * (d - s * r * r * _head_sums(d * s))
            o_ref[...] = ds * _silu_grad(c)
        ab, dgb_v, gb_v = ab_ref[...], dgb_ref[...], gb_ref[...]
        lane = lax.broadcasted_iota(jnp.int32, ab.shape, 1)
        is_g = lane < DN_HEADS
        dpre = dgb_v * (-jnp.exp(al_ref[...])) * _sigmoid(ab + dt_ref[...])
        dab_ref[...] = jnp.where(is_g, dpre, dgb_v * gb_v * (1.0 - gb_v)).astype(dab_ref.dtype)
        dal_ref[...] += jnp.sum(jnp.where(is_g, dgb_v * gb_v, 0.0), axis=0, keepdims=True)
        ddt_ref[...] += jnp.sum(jnp.where(is_g, dpre, 0.0), axis=0, keepdims=True)

    row = pl.BlockSpec((tl, gw), lambda i: (i, 0))
    abrow = pl.BlockSpec((tl, AB_PAD), lambda i: (i, 0))
    abvec = pl.BlockSpec((1, AB_PAD), lambda i: (0, 0))
    return pl.pallas_call(
        body, name=name, grid=(l // tl,),
        in_specs=[row] * 6 + [abrow, abrow, abrow, abvec, abvec],
        out_specs=[row, row, row, abrow, abvec, abvec],
        out_shape=[jax.ShapeDtypeStruct((l, gw), F32)] * 3 + [jax.ShapeDtypeStruct((l, AB_PAD), BF16),
                   jax.ShapeDtypeStruct((1, AB_PAD), F32), jax.ShapeDtypeStruct((1, AB_PAD), F32)],
        compiler_params=_params(("arbitrary",)),
    )(dq, dk, dv, cq, ck, cv, dgb, gb, proj_ab, a_log, dt_bias)


def _conv_bwd(dc, proj, col_blk, w_part, name):
    l, gw = dc.shape
    tl = min(MIX_ROWS, l)
    nb = l // tl

    def body(dc_ref, halo_ref, x_ref, w_ref, dx_ref, dw_ref):
        i = pl.program_id(0)

        @pl.when(i == 0)
        def _():
            dw_ref[...] = jnp.zeros_like(dw_ref)

        ext = jnp.concatenate([dc_ref[...], jnp.where(i < nb - 1, halo_ref[...], 0.0)], axis=0)
        x = x_ref[...]
        dx = jnp.zeros((tl, gw), F32)
        rid = lax.broadcasted_iota(jnp.int32, (8, gw), 0)
        dw = jnp.zeros((8, gw), F32)
        for j in range(DN_CONV):
            k = DN_CONV - 1 - j
            shifted = (ext if k == 0 else pltpu.roll(ext, tl + CONV_HALO - k, axis=0))[:tl, :]
            dx = dx + shifted * w_ref[j:j + 1, :]
            dw = dw + jnp.where(rid == j, jnp.sum(x * shifted, axis=0, keepdims=True), 0.0)
        dx_ref[...] = dx.astype(dx_ref.dtype)
        dw_ref[...] += dw

    row = pl.BlockSpec((tl, gw), lambda i: (i, 0))
    return pl.pallas_call(
        body, name=name, grid=(nb,),
        in_specs=[row, pl.BlockSpec((CONV_HALO, gw), lambda i: (jnp.minimum((i + 1) * (tl // CONV_HALO), l // CONV_HALO - 1), 0)),
                  pl.BlockSpec((tl, gw), lambda i: (i, col_blk)), pl.BlockSpec((DN_CONV, gw), lambda i: (0, 0))],
        out_specs=[row, pl.BlockSpec((8, gw), lambda i: (0, 0))],
        out_shape=[jax.ShapeDtypeStruct((l, gw), BF16), jax.ShapeDtypeStruct((8, gw), F32)],
        compiler_params=_params(("arbitrary",)),
    )(dc, dc, proj, w_part)


TERMS_CHUNKS = 8


def _bdot(a, b, dims="nn", precision=None):
    cd = {"nn": ((2,), (1,)), "nt": ((2,), (2,)), "tn": ((1,), (1,))}[dims]
    return lax.dot_general(a, b, (cd, ((0,), (0,))), preferred_element_type=F32, precision=precision)


def _bmdot(a, b, dims="nn"):
    return _bdot(_mx(a), _mx(b), dims)


def _wy_terms(q, k, v, gcol, beta, t=None):
    c = DN_CHUNK
    ii = lax.broadcasted_iota(jnp.int32, (1, c, c), 1)
    jj = lax.broadcasted_iota(jnp.int32, (1, c, c), 2)
    tril, strict = ii >= jj, ii > jj
    grow = jnp.sum(jnp.where(ii == jj, gcol, 0.0), axis=1, keepdims=True)
    gc_col = jnp.sum(jnp.where(tril, grow, 0.0), axis=2, keepdims=True)
    gc_row = jnp.sum(jnp.where(ii <= jj, gcol, 0.0), axis=1, keepdims=True)
    dec = jnp.exp(jnp.where(tril, gc_col - gc_row, -1e30))
    kb, vb = k * beta, v * beta
    kk = _bmdot(kb, k, "nt")
    if t is None:
        a = jnp.where(strict, kk * dec, 0.0)
        d = jnp.where((ii >> 3) == (jj >> 3), a, 0.0)
        t = jnp.where(ii == jj, 1.0, 0.0) - d
        p = _bdot(d, d, precision=HI)
        t = t + _bdot(t, p, precision=HI)
        t = t + _bdot(t, _bdot(p, p, precision=HI), precision=HI)
        for sh in (3, 4, 5):
            below = ((ii >> (sh + 1)) == (jj >> (sh + 1))) & ((ii >> sh) > (jj >> sh))
            t = t - _bdot(t, _bdot(jnp.where(below, a, 0.0), t, precision=HI), precision=HI)
    eg = jnp.exp(gc_col)
    gc_last = gc_col[:, c - 1:c, :]
    kbg = kb * eg
    qk0 = _bmdot(q, k, "nt")
    e2 = jnp.exp(gc_last - gc_col)
    return dict(ii=ii, jj=jj, tril=tril, strict=strict, dec=dec, kb=kb, vb=vb, kk=kk, t=t, eg=eg, kbg=kbg,
                qk0=qk0, qk=jnp.where(tril, qk0 * dec, 0.0), qg=q * eg, e2=e2, kt=k * e2, gl=jnp.exp(gc_last))


def _to_heads(x, g):
    return jnp.concatenate([x[:, h * DN_HEAD_DIM:(h + 1) * DN_HEAD_DIM].reshape(g, DN_CHUNK, DN_HEAD_DIM)
                            for h in range(DN_HEADS)], axis=0)


def _from_heads(t, ref, g):
    for h in range(DN_HEADS):
        ref[:, h * DN_HEAD_DIM:(h + 1) * DN_HEAD_DIM] = t[h * g:(h + 1) * g].reshape(g * DN_CHUNK, DN_HEAD_DIM).astype(ref.dtype)


def _head_columns(gbv, first_lane, g):
    lane = lax.broadcasted_iota(jnp.int32, gbv.shape, 1)
    return jnp.concatenate([jnp.sum(jnp.where(lane == first_lane + h, gbv, 0.0), axis=1, keepdims=True).reshape(g, DN_CHUNK, 1)
                            for h in range(DN_HEADS)], axis=0)


def _gdn_terms_fwd(qn, kn, v, gb, name):
    l = qn.shape[0]
    n_chunks = l // DN_CHUNK
    g = min(TERMS_CHUNKS, n_chunks)
    rows, c, nh = g * DN_CHUNK, DN_CHUNK, DN_HEADS

    def body(q_ref, k_ref, v_ref, gb_ref, u_ref, w_ref, qg_ref, kt_ref, qk_ref, t_ref, gl_ref):
        gbv = gb_ref[...]
        x = _wy_terms(_to_heads(q_ref[...], g), _to_heads(k_ref[...], g), _to_heads(v_ref[...], g),
                      _head_columns(gbv, 0, g), _head_columns(gbv, nh, g))
        _from_heads(_bmdot(x["t"], x["vb"]), u_ref, g)
        _from_heads(_bmdot(x["t"], x["kbg"]), w_ref, g)
        _from_heads(x["qg"], qg_ref, g)
        _from_heads(x["kt"], kt_ref, g)
        for h in range(nh):
            qk_ref[:, h] = x["qk"][h * g:(h + 1) * g]
            t_ref[:, h] = x["t"][h * g:(h + 1) * g]
            gl_ref[:, h] = jnp.broadcast_to(x["gl"][h * g:(h + 1) * g], (g, 1, 128))

    row = pl.BlockSpec((rows, GROUP_WIDTH), lambda i: (i, 0))
    sq = pl.BlockSpec((g, nh, c, c), lambda i: (i, 0, 0, 0))
    glb = pl.BlockSpec((g, nh, 1, 128), lambda i: (i, 0, 0, 0))
    return pl.pallas_call(
        body, name=name, grid=(n_chunks // g,), in_specs=[row, row, row, pl.BlockSpec((rows, AB_PAD), lambda i: (i, 0))],
        out_specs=[row] * 4 + [sq, sq, glb],
        out_shape=[jax.ShapeDtypeStruct((l, GROUP_WIDTH), F32)] * 4 + [jax.ShapeDtypeStruct((n_chunks, nh, c, c), F32)] * 2
        + [jax.ShapeDtypeStruct((n_chunks, nh, 1, 128), F32)],
        compiler_params=_params(("parallel",)),
    )(qn, kn, v, gb)


REC_CHUNKS = 8


def _rec_specs(n_chunks, reverse):
    c, hd, nh = DN_CHUNK, DN_HEAD_DIM, DN_HEADS
    g = min(REC_CHUNKS, n_chunks)
    nb = n_chunks // g
    blk_of = (lambda n: nb - 1 - n) if reverse else (lambda n: n)
    return g, nb, (pl.BlockSpec((g * c, GROUP_WIDTH), lambda n: (blk_of(n), 0)), pl.BlockSpec((g, nh, c, c), lambda n: (blk_of(n), 0, 0, 0)),
                   pl.BlockSpec((g, nh, 1, 128), lambda n: (blk_of(n), 0, 0, 0)), pl.BlockSpec((g, nh, hd, hd), lambda n: (blk_of(n), 0, 0, 0)))


def _gdn_rec_fwd(u, w, qg, kt, qk, gl, name):
    l = u.shape[0]
    n_chunks = l // DN_CHUNK
    hd, nh, c = DN_HEAD_DIM, DN_HEADS, DN_CHUNK
    g, nb, (blk, sq, glb, st) = _rec_specs(n_chunks, False)
    heads = range(nh)

    def body(u_ref, w_ref, qg_ref, kt_ref, qk_ref, gl_ref, o_ref, vn_ref, s_ref, state):
        @pl.when(pl.program_id(0) == 0)
        def _():
            state[...] = jnp.zeros_like(state)

        def cols(h):
            return slice(h * hd, (h + 1) * hd)
        for ci in range(g):
            rows = slice(ci * c, (ci + 1) * c)
            s = [state[h] for h in heads]
            ws = [_mdot(w_ref[rows, cols(h)], s[h]) for h in heads]
            vn = [u_ref[rows, cols(h)] - ws[h] for h in heads]
            kv = [_mdot(kt_ref[rows, cols(h)], vn[h], "tn") for h in heads]
            for h in heads:
                state[h] = s[h] * gl_ref[ci, h] + kv[h]
            o1 = [_mdot(qg_ref[rows, cols(h)], s[h]) for h in heads]
            o2 = [_mdot(qk_ref[ci, h], vn[h]) for h in heads]
            for h in heads:
                s_ref[ci, h] = s[h]
                o_ref[rows, cols(h)] = o1[h] + o2[h]
                vn_ref[rows, cols(h)] = vn[h]

    return pl.pallas_call(
        body, name=name, grid=(nb,), in_specs=[blk, blk, blk, blk, sq, glb], out_specs=[blk, blk, st],
        out_shape=[jax.ShapeDtypeStruct((l, GROUP_WIDTH), F32)] * 2 + [jax.ShapeDtypeStruct((n_chunks, nh, hd, hd), F32)],
        scratch_shapes=[pltpu.VMEM((nh, hd, hd), F32)], compiler_params=_params(("arbitrary",)),
    )(u, w, qg, kt, qk, gl)


def _gdn_rec_bwd(do, w, qg, kt, vn, qk, gl, states, name):
    l = do.shape[0]
    n_chunks = l // DN_CHUNK
    hd, nh, c = DN_HEAD_DIM, DN_HEADS, DN_CHUNK
    g, nb, (blk, sq, glb, st) = _rec_specs(n_chunks, True)
    heads = range(nh)

    def body(do_ref, w_ref, qg_ref, kt_ref, vn_ref, qk_ref, gl_ref, s_ref, dvn_ref, dw_ref, dkt_ref, dqg_ref, dqk_ref, dgl_ref, dstate):
        @pl.when(pl.program_id(0) == 0)
        def _():
            dstate[...] = jnp.zeros_like(dstate)

        def cols(h):
            return slice(h * hd, (h + 1) * hd)
        tril = lax.broadcasted_iota(jnp.int32, (c, c), 0) >= lax.broadcasted_iota(jnp.int32, (c, c), 1)
        for ci in reversed(range(g)):
            rows = slice(ci * c, (ci + 1) * c)
            ds = [dstate[h] for h in heads]
            dout = [do_ref[rows, cols(h)] for h in heads]
            a1 = [_mdot(qk_ref[ci, h], dout[h], "tn") for h in heads]
            a2 = [_mdot(kt_ref[rows, cols(h)], ds[h]) for h in heads]
            dvn = [a1[h] + a2[h] for h in heads]
            b1 = [_mdot(qg_ref[rows, cols(h)], dout[h], "tn") for h in heads]
            b2 = [_mdot(w_ref[rows, cols(h)], dvn[h], "tn") for h in heads]
            for h in heads:
                dstate[h] = b1[h] + gl_ref[ci, h] * ds[h] - b2[h]
            for h in heads:
                s, vnew = s_ref[ci, h], vn_ref[rows, cols(h)]
                dvn_ref[rows, cols(h)] = dvn[h]
                dw_ref[rows, cols(h)] = -_mdot(dvn[h], s, "nt")
                dkt_ref[rows, cols(h)] = _mdot(vnew, ds[h], "nt")
                dqg_ref[rows, cols(h)] = _mdot(dout[h], s, "nt")
                dqk_ref[ci, h] = jnp.where(tril, _mdot(dout[h], vnew, "nt"), 0.0)
                dgl = jnp.sum(jnp.sum(ds[h] * s, axis=1, keepdims=True), axis=0, keepdims=True)
                dgl_ref[ci, h] = jnp.broadcast_to(dgl, (1, 128))

    return pl.pallas_call(
        body, name=name, grid=(nb,), in_specs=[blk] * 5 + [sq, glb, st], out_specs=[blk] * 4 + [sq, glb],
        out_shape=[jax.ShapeDtypeStruct((l, GROUP_WIDTH), F32)] * 4 + [jax.ShapeDtypeStruct((n_chunks, nh, c, c), F32),
                                                                       jax.ShapeDtypeStruct((n_chunks, nh, 1, 128), F32)],
        scratch_shapes=[pltpu.VMEM((nh, hd, hd), F32)], compiler_params=_params(("arbitrary",)),
    )(do, w, qg, kt, vn, qk, gl, states)


def _gdn_terms_bwd(qn, kn, v, gb, t_inv, dvn, dw, dkt, dqg, dqk, dgl, name):
    l = qn.shape[0]
    n_chunks = l // DN_CHUNK
    g = min(TERMS_CHUNKS, n_chunks)
    rows, c, nh = g * DN_CHUNK, DN_CHUNK, DN_HEADS

    def body(q_ref, k_ref, v_ref, gb_ref, t_ref, dvn_ref, dw_ref, dkt_ref, dqg_ref, dqk_ref, dgl_ref, dq_ref, dk_ref, dv_ref, dgb_ref):
        gbv = gb_ref[...]
        q, k, vv = _to_heads(q_ref[...], g), _to_heads(k_ref[...], g), _to_heads(v_ref[...], g)
        beta = _head_columns(gbv, nh, g)
        t = jnp.concatenate([t_ref[:, h] for h in range(nh)], axis=0)
        x = _wy_terms(q, k, vv, _head_columns(gbv, 0, g), beta, t=t)
        ii, jj, strict = x["ii"], x["jj"], x["strict"]
        du, dwv, dktv, dqgv = (_to_heads(r[...], g) for r in (dvn_ref, dw_ref, dkt_ref, dqg_ref))
        dqkv = jnp.concatenate([dqk_ref[:, h] for h in range(nh)], axis=0)
        dglv = jnp.concatenate([dgl_ref[:, h] for h in range(nh)], axis=0)[:, :, 0:1]
        dt = _bmdot(du, x["vb"], "nt") + _bmdot(dwv, x["kbg"], "nt")
        dvb = _bmdot(t, du, "tn")
        dkbg = _bmdot(t, dwv, "tn")
        tt_dt = _bdot(t, dt, "tn", precision=HI)
        da = jnp.where(strict, -_bdot(tt_dt, t, "nt", precision=HI), 0.0)
        dkk = da * x["dec"]
        dqk0 = dqkv * x["dec"]
        e = (da * x["kk"] + dqkv * x["qk0"]) * x["dec"]
        dkb = _bmdot(dkk, k) + dkbg * x["eg"]
        dk = _bmdot(dkk, x["kb"], "tn") + _bmdot(dqk0, q, "tn") + dktv * x["e2"] + dkb * beta
        dq = _bmdot(dqk0, k) + dqgv * x["eg"]
        s_kt = jnp.sum(dktv * x["kt"], axis=2, keepdims=True)
        dgc_c = (jnp.sum(e, axis=2, keepdims=True) + jnp.sum(dqgv * x["qg"], axis=2, keepdims=True) - s_kt
                 + jnp.sum(dkbg * x["kbg"], axis=2, keepdims=True))
        dgc_last = jnp.sum(s_kt, axis=1, keepdims=True) + dglv * x["gl"]
        rid = lax.broadcasted_iota(jnp.int32, (1, c, 1), 1)
        dgc_c = dgc_c + jnp.where(rid == c - 1, dgc_last, 0.0)
        dgc_r = jnp.sum(jnp.where(ii == jj, dgc_c, 0.0), axis=1, keepdims=True) - jnp.sum(e, axis=1, keepdims=True)
        dg = jnp.sum(jnp.where(jj >= ii, dgc_r, 0.0), axis=2, keepdims=True)
        dbeta = jnp.sum(dkb * k, axis=2, keepdims=True) + jnp.sum(dvb * vv, axis=2, keepdims=True)
        _from_heads(dq, dq_ref, g)
        _from_heads(dk, dk_ref, g)
        _from_heads(dvb * beta, dv_ref, g)
        lane = lax.broadcasted_iota(jnp.int32, gbv.shape, 1)
        dgb = jnp.zeros(gbv.shape, F32)
        for h in range(nh):
            dgb = dgb + jnp.where(lane == h, dg[h * g:(h + 1) * g].reshape(rows, 1), 0.0)
            dgb = dgb + jnp.where(lane == nh + h, dbeta[h * g:(h + 1) * g].reshape(rows, 1), 0.0)
        dgb_ref[...] = dgb

    row = pl.BlockSpec((rows, GROUP_WIDTH), lambda i: (i, 0))
    abrow = pl.BlockSpec((rows, AB_PAD), lambda i: (i, 0))
    sq = pl.BlockSpec((g, nh, c, c), lambda i: (i, 0, 0, 0))
    glb = pl.BlockSpec((g, nh, 1, 128), lambda i: (i, 0, 0, 0))
    return pl.pallas_call(
        body, name=name, grid=(n_chunks // g,), in_specs=[row, row, row, abrow, sq, row, row, row, row, sq, glb],
        out_specs=[row, row, row, abrow],
        out_shape=[jax.ShapeDtypeStruct((l, GROUP_WIDTH), F32)] * 3 + [jax.ShapeDtypeStruct((l, AB_PAD), F32)],
        compiler_params=_params(("parallel",)),
    )(qn, kn, v, gb, t_inv, dvn, dw, dkt, dqg, dqk, dgl)


def _gdn_post_fwd(o, proj, norm_g4, name):
    l, gw = o.shape
    tl = min(2 * MIX_ROWS, l)

    def body(o_ref, gate_ref, g_ref, out_ref):
        ov = o_ref[...]
        r = lax.rsqrt(_head_sums(ov * ov) * (1.0 / DN_HEAD_DIM) + RMS_EPS)
        out_ref[...] = (ov * r * g_ref[...] * _silu(gate_ref[...])).astype(out_ref.dtype)

    row = pl.BlockSpec((tl, gw), lambda i: (i, 0))
    return pl.pallas_call(
        body, name=name, grid=(l // tl,),
        in_specs=[row, pl.BlockSpec((tl, gw), lambda i: (i, 7)), pl.BlockSpec((1, gw), lambda i: (0, 0))],
        out_specs=row, out_shape=jax.ShapeDtypeStruct((l, gw), BF16), compiler_params=_params(("parallel",)),
    )(o, proj, norm_g4)


def _gdn_post_bwd(dmixed, o, proj, norm_g4, name):
    l, gw = o.shape
    tl = min(2 * MIX_ROWS, l)

    def body(d_ref, o_ref, gate_ref, g_ref, do_ref, dgate_ref, dng_ref):
        @pl.when(pl.program_id(0) == 0)
        def _():
            dng_ref[...] = jnp.zeros_like(dng_ref)

        ov, gate, d = o_ref[...], gate_ref[...], d_ref[...]
        r = lax.rsqrt(_head_sums(ov * ov) * (1.0 / DN_HEAD_DIM) + RMS_EPS)
        oh = ov * r
        sg = _silu(gate)
        dgate_ref[...] = (d * oh * g_ref[...] * _silu_grad(gate)).astype(dgate_ref.dtype)
        dng_ref[...] += jnp.sum(d * sg * oh, axis=0, keepdims=True)
        doh = d * g_ref[...] * sg
        do_ref[...] = r * (doh - oh * _head_sums(doh * oh) * (1.0 / DN_HEAD_DIM))

    row = pl.BlockSpec((tl, gw), lambda i: (i, 0))
    vec = pl.BlockSpec((1, gw), lambda i: (0, 0))
    return pl.pallas_call(
        body, name=name, grid=(l // tl,),
        in_specs=[pl.BlockSpec((tl, gw), lambda i: (i, 3)), row, pl.BlockSpec((tl, gw), lambda i: (i, 7)), vec],
        out_specs=[row, row, vec],
        out_shape=[jax.ShapeDtypeStruct((l, gw), F32), jax.ShapeDtypeStruct((l, gw), BF16), jax.ShapeDtypeStruct((1, gw), F32)],
        compiler_params=_params(("arbitrary",)),
    )(dmixed, o, proj, norm_g4)


def _run(hosts, name, fn):
    h = hosts.get(name)
    if h is None:
        return fn(None)
    res, outs = fn(h[0]())
    h[1](outs)
    return res


def _layer_fwd(x, xm, w, li, hosts):
    l = x.shape[0]
    nm = f"l{li}_"
    proj = _run(hosts, nm + "proj", lambda ops: _matmul(
        xm, w["w_main"], mode="nn", tm=1024, tn=1024, tk=2048,out_dtype=F32, name=nm + "proj", comm=ops))
    proj_ab = _matmul(xm, w["w_ab"], mode="nn", tm=1024, tn=AB_PAD, tk=2048, out_dtype=F32, name=nm + "proj_ab")
    hs, y = _s5_fwd(proj, w["s5_b"], w["s5_c"], w["s5_lam"], w["s5_d"], nm + "s5")
    m_s5 = _s5_glu_fwd(y, w["s5_glu_w"], w["s5_glu_b"], nm + "s5_glu")
    m_sgu = _sgu_fwd(proj, w["sgu_norm_g"], w["sgu_norm_b"], w["sgu_wm"], w["sgu_bfull"], nm + "sgu")
    m_pool, pooled = _pool_fwd(proj, w["pool_w"], w["pool_scale"], nm + "pool")
    qn, kn, v, cq, ck, cv, gb = _gdn_pre_fwd(proj, proj_ab, w["dn_conv_w"], w["dn_a_log"], w["dn_dt_bias"], nm + "gdn_pre")
    u, wy, qg, kt, qk, t_inv, gl = _gdn_terms_fwd(qn, kn, v, gb, nm + "gdn_terms")
    o, vn, states = _gdn_rec_fwd(u, wy, qg, kt, qk, gl, nm + "gdn_rec")
    m_dn = _gdn_post_fwd(o, proj, w["dn_norm_g4"], nm + "gdn_post")
    mixed = jnp.concatenate([m_s5, m_sgu, m_pool, m_dn], axis=1)
    y1 = _matmul(mixed, w["w_out"], mode="nn", tm=1024, tn=1024, tk=2048, out_dtype=F32, name=nm + "out_proj")
    h1, x1, x1m = _ln_fwd(x, y1, w["ln1_g"], w["ln1_b"], nm + "ln1")
    r = _run(hosts, nm + "up", lambda ops: _matmul(
        x1m, w["w_up"], mode="nn", tm=1024, tn=1024, tk=2048,out_dtype=BF16, name=nm + "up",
        epi=lambda acc: jnp.maximum(acc, 0.0), b_slab=w["w_up"].shape[2], comm=ops))
    y2 = _run(hosts, nm + "down", lambda ops: _matmul(
        r, w["w_down"], mode="nn", tm=1024, tn=1024, tk=2048,out_dtype=F32, name=nm + "down", a_fn=lambda a: a * a, comm=ops))
    h2, x2, x2m = _ln_fwd(x1, y2, w["ln2_g"], w["ln2_b"], nm + "ln2")
    saved = dict(xm=xm, proj=proj, proj_ab=proj_ab, hs=hs, y=y, pooled=pooled, qn=qn, kn=kn, v=v, cq=cq, ck=ck, cv=cv, gb=gb,
                 wy=wy, qg=qg, kt=kt, qk=qk, t_inv=t_inv, gl=gl, vn=vn, o=o, states=states, mixed=mixed, h1=h1, x1m=x1m,
                 r=r, h2=h2)
    return x2, x2m, saved


def _layer_bwd(dx2, s, w, small, li, hosts, g):
    nm = f"l{li}b_"
    l = dx2.shape[0]
    gw = GROUP_WIDTH
    wire = MXU_DTYPE
    dh2, dh2m, g["ln2_g"], g["ln2_b"] = _ln_bwd(dx2, s["h2"], w["ln2_g"], nm + "ln2")
    g["w_down"] = _run(hosts, nm + "dw_down", lambda ops: _matmul(
        s["r"], dh2m, mode="tn", tm=1024, tn=1024, tk=2048,out_dtype=wire, name=nm + "dw_down", a_fn=lambda a: a * a,
        comm=ops)).reshape(N_DEV, D_FF // N_DEV, D_MODEL)
    dpre = _run(hosts, nm + "dpre", lambda ops: _matmul(
        dh2m, w["w_down"], mode="nt", tm=1024, tn=1024, tk=2048,out_dtype=BF16, name=nm + "dpre",
        extras=[(s["r"], (None, None), lambda i, j: (i, j))], epi=lambda acc, r: acc * 2.0 * r.astype(F32), comm=ops))
    g["w_up"] = _matmul(s["x1m"], dpre, mode="tn", tm=1024, tn=1024, tk=2048,out_dtype=wire, name=nm + "dw_up",
                        out_slab=D_FF // N_DEV)
    dx1 = _run(hosts, nm + "dx1", lambda ops: _matmul(
        dpre, w["w_up"], mode="nt", tm=1024, tn=1024, tk=2048,out_dtype=F32, name=nm + "dx1",
        extras=[(dh2, (None, None), lambda i, j: (i, j))], epi=lambda acc, e: acc + ALPHA * e,
        b_slab=w["w_up"].shape[2], comm=ops))
    dh1, dh1m, g["ln1_g"], g["ln1_b"] = _ln_bwd(dx1, s["h1"], w["ln1_g"], nm + "ln1")
    g["w_out"] = _matmul(s["mixed"], dh1m, mode="tn", tm=1024, tn=1024, tk=2048,out_dtype=wire,
                         name=nm + "dw_out").reshape(N_DEV, D_MODEL // N_DEV, D_MODEL)
    dmixed = _run(hosts, nm + "dmixed", lambda ops: _matmul(
        dh1m, w["w_out"], mode="nt", tm=1024, tn=1024, tk=2048,out_dtype=F32, name=nm + "dmixed", comm=ops))
    proj, proj_ab = s["proj"], s["proj_ab"]
    dy, dz, yg, g["s5_glu_b"], g["s5_d"] = _s5_glu_bwd(dmixed, s["y"], proj, w["s5_glu_w"], w["s5_glu_b"], nm + "s5_glu")
    g["s5_glu_w"] = _matmul(yg, dz, mode="tn", tm=gw, tn=gw, tk=1024, out_dtype=wire,
                            name=nm + "dw_glu").reshape(N_DEV, gw // N_DEV, gw)
    du_s5, g["s5_b"], g["s5_c"], g["s5_lam"] = _s5_bwd(dy, s["hs"], proj, w["s5_b"], w["s5_c"], w["s5_lam_conj"], w["s5_d"], nm + "s5")
    dzu, dzv, g["sgu_w"], g["sgu_bfull"], g["sgu_norm_g"], g["sgu_norm_b"] = _sgu_bwd(
        dmixed, proj, w["sgu_norm_g"], w["sgu_norm_b"], w["sgu_wm"], w["sgu_bfull"], nm + "sgu")
    dpooled, g["pool_w"], g["pool_scale"] = _pool_bwd_map(dmixed, s["pooled"], w["pool_w"], w["pool_scale"], nm + "pool_map")
    dp = _pool_bwd_window(dpooled, nm + "pool_win")
    do, dgate, g["dn_norm_g4"] = _gdn_post_bwd(dmixed, s["o"], proj, w["dn_norm_g4"], nm + "gdn_post")
    dvn, dwy, dkt, dqg, dqk, dgl = _gdn_rec_bwd(do, s["wy"], s["qg"], s["kt"], s["vn"], s["qk"], s["gl"], s["states"], nm + "gdn_rec")
    dq, dk, dv, dgb = _gdn_terms_bwd(s["qn"], s["kn"], s["v"], s["gb"], s["t_inv"], dvn, dwy, dkt, dqg, dqk, dgl, nm + "gdn_terms")
    dcq, dck, dcv, dab, g["dn_a_log"], g["dn_dt_bias"] = _gdn_pre_bwd(
        dq, dk, dv, s["cq"], s["ck"], s["cv"], dgb, s["gb"], proj_ab, w["dn_a_log"], w["dn_dt_bias"], nm + "gdn_pre")
    dxs, dws = [], []
    for p, dc in enumerate((dcq, dck, dcv)):
        dxp, dwp = _conv_bwd(dc, proj, QKV_BLK + p, w["dn_conv_w"][:, p * gw:(p + 1) * gw], nm + f"conv{p}")
        dxs.append(dxp)
        dws.append(dwp)
    dconv = jnp.concatenate(dws, axis=1)
    g["dn_conv_w"] = jnp.transpose(dconv.reshape(dconv.shape[0], N_DEV, 3 * gw // N_DEV), (1, 0, 2))
    dproj = jnp.concatenate([du_s5, dzu, dzv, dp] + dxs + [dgate], axis=1)
    xm = s["xm"]
    g["small"] = _unprep_grads(g, small)
    dw_main = _run(hosts, nm + "dw_main", lambda ops: _matmul(
        xm, dproj, mode="tn", tm=1024, tn=1024, tk=2048,out_dtype=wire, name=nm + "dw_main", comm=ops))
    dw_ab = _matmul(xm, dab, mode="tn", tm=1024, tn=AB_PAD, tk=1024, out_dtype=wire, name=nm + "dw_ab")
    dw_in = jnp.concatenate([dw_main, dw_ab[:, :2 * DN_HEADS]], axis=1)
    g["w_in"] = jnp.transpose(dw_in.reshape(D_MODEL, N_DEV, dw_in.shape[1] // N_DEV), (1, 0, 2))
    return _run(hosts, nm + "dx", lambda ops: _matmul(
        dproj, w["w_main"], mode="nt", tm=1024, tn=1024, tk=2048, out_dtype=F32, name=nm + "dx",
        extras=[(dh1, (None, None), lambda i, j: (i, j)), (dab, (None, AB_PAD), lambda i, j: (i, 0)),
                (w["w_ab"], ("tn", AB_PAD), lambda i, j: (j, 0))],
        epi=lambda acc, e, da, wab: acc + ALPHA * e + _dot(_mx(da), _mx(wab), "nt"), comm=ops))


SMALL = ("s5_lambda_re", "s5_lambda_im", "s5_log_step", "s5_b_re", "s5_b_im", "s5_c_re", "s5_c_im", "s5_d", "s5_glu_b",
         "sgu_norm_g", "sgu_norm_b", "sgu_w", "sgu_b", "pool_w", "pool_scale", "dn_a_log", "dn_dt_bias", "dn_norm_g",
         "ln1_g", "ln1_b", "ln2_g", "ln2_b")
SHARDED = ("w_in", "s5_glu_w", "dn_conv_w", "w_out", "w_up", "w_down")


def _pad_lanes(v, width=AB_PAD):
    return jnp.pad(v.reshape(1, -1), ((0, 0), (0, width - v.size)))


def _prep_small(p):
    mx = MXU_DTYPE
    lbr, lbi, bbr, bbi = _s5_discretize(p["s5_lambda_re"], p["s5_lambda_im"], p["s5_log_step"], p["s5_b_re"], p["s5_b_im"])
    b_compact, c_compact = _s5_compact(bbr, bbi, p["s5_c_re"], p["s5_c_im"])
    causal = jnp.tril(jnp.ones((SGU_CHUNK, SGU_CHUNK), F32))
    return dict(
        s5_b=b_compact.astype(mx), s5_c=c_compact.astype(mx),
        s5_lam=jnp.concatenate([lbr.reshape(1, -1), lbi.reshape(1, -1)], axis=1),
        s5_lam_conj=jnp.concatenate([lbr.reshape(1, -1), -lbi.reshape(1, -1)], axis=1),
        s5_d=p["s5_d"].reshape(1, -1), s5_glu_b=p["s5_glu_b"].reshape(1, -1),
        sgu_norm_g=p["sgu_norm_g"].reshape(1, -1), sgu_norm_b=p["sgu_norm_b"].reshape(1, -1),
        sgu_wm=(p["sgu_w"] * causal).astype(mx), sgu_bfull=jnp.repeat(p["sgu_b"].T, GROUP_WIDTH // SGU_HEADS, axis=1),
        pool_w=p["pool_w"].astype(mx), pool_scale=p["pool_scale"].reshape(1, -1),
        dn_a_log=_pad_lanes(p["dn_a_log"]), dn_dt_bias=_pad_lanes(p["dn_dt_bias"]),
        dn_norm_g4=jnp.tile(p["dn_norm_g"].reshape(1, -1), (1, DN_HEADS)),
        ln1_g=p["ln1_g"].reshape(1, -1), ln1_b=p["ln1_b"].reshape(1, -1),
        ln2_g=p["ln2_g"].reshape(1, -1), ln2_b=p["ln2_b"].reshape(1, -1),
    )


def _weight_views(name, t):
    if name == "w_in":
        w_in = jnp.transpose(t, (1, 0, 2)).reshape(t.shape[1], N_DEV * t.shape[2])
        pad = AB_PAD - (w_in.shape[1] - MAIN_COLS)
        return dict(w_main=w_in[:, :MAIN_COLS], w_ab=jnp.pad(w_in[:, MAIN_COLS:], ((0, 0), (0, pad))))
    if name == "dn_conv_w":
        return dict(dn_conv_w=jnp.transpose(t, (1, 0, 2)).reshape(t.shape[1], N_DEV * t.shape[2]))
    if name == "w_up":
        return dict(w_up=t)
    return {name: t.reshape(N_DEV * t.shape[1], t.shape[2])}


def _unprep_grads(g, p):
    causal = jnp.tril(jnp.ones((SGU_CHUNK, SGU_CHUNK), F32))
    dbbr, dbbi = _s5_uncompact_b(g["s5_b"])
    dc_re, dc_im = _s5_uncompact_c(g["s5_c"])
    dlbr, dlbi = g["s5_lam"][0, :S5_NS].reshape(S5_GROUPS, S5_STATE), g["s5_lam"][0, S5_NS:].reshape(S5_GROUPS, S5_STATE)
    _, vjp = jax.vjp(_s5_discretize, p["s5_lambda_re"], p["s5_lambda_im"], p["s5_log_step"], p["s5_b_re"], p["s5_b_im"])
    d_lre, d_lim, d_step, d_bre, d_bim = vjp((dlbr, dlbi, dbbr, dbbi))
    hd = GROUP_WIDTH // SGU_HEADS
    return dict(
        s5_lambda_re=d_lre, s5_lambda_im=d_lim, s5_log_step=d_step, s5_b_re=d_bre, s5_b_im=d_bim, s5_c_re=dc_re, s5_c_im=dc_im,
        s5_d=g["s5_d"].reshape(S5_GROUPS, S5_CH), s5_glu_b=g["s5_glu_b"].reshape(-1),
        sgu_norm_g=g["sgu_norm_g"].reshape(-1), sgu_norm_b=g["sgu_norm_b"].reshape(-1), sgu_w=g["sgu_w"] * causal,
        sgu_b=jnp.sum(g["sgu_bfull"].reshape(SGU_CHUNK, SGU_HEADS, hd), axis=2).T,
        pool_w=g["pool_w"], pool_scale=g["pool_scale"].reshape(-1),
        dn_a_log=g["dn_a_log"][0, :DN_HEADS], dn_dt_bias=g["dn_dt_bias"][0, :DN_HEADS],
        dn_norm_g=jnp.sum(g["dn_norm_g4"].reshape(DN_HEADS, DN_HEAD_DIM), axis=0),
        ln1_g=g["ln1_g"].reshape(-1), ln1_b=g["ln1_b"].reshape(-1), ln2_g=g["ln2_g"].reshape(-1), ln2_b=g["ln2_b"].reshape(-1),
    )


def _local_step(x, target, ops, small, fwd_hosts, bwd_hosts, grads):
    saved = []
    h, hm = x, x.astype(MXU_DTYPE)
    for i in range(DEPTH):
        h, hm, s = _layer_fwd(h, hm, ops[i], i, fwd_hosts)
        saved.append(s)
    loss, dh = _loss_head(h, target)
    for i in reversed(range(DEPTH)):
        dh = _layer_bwd(dh, saved[i], ops[i], small[i], i, bwd_hosts, grads[i])
    return loss, dh


def _adamw(w, gparts, m, v, name):
    rr, c = w.shape
    ng = len(gparts)
    r = rr // ng
    lanes = -(-c // 128) * 128
    tr = r
    while tr * lanes * 4 * N_DEV > (4 << 20) and tr % 16 == 0:
        tr //= 2
    nb = r // tr

    def body(w_ref, *rest):
        g_refs, (m_ref, v_ref, go_ref, d_ref, mo_ref, vo_ref) = rest[:ng], rest[ng:]
        layer = pl.program_id(0)
        g = jnp.zeros(m_ref.shape, F32)
        for li in range(ng):
            gl = g_refs[li][0].astype(F32)
            for s in range(1, N_DEV):
                gl = gl + g_refs[li][s].astype(F32)
            g = jnp.where(layer == li, gl, g)
        mn = ADAM_B1 * m_ref[...] + (1.0 - ADAM_B1) * g
        vn = ADAM_B2 * v_ref[...] + (1.0 - ADAM_B2) * g * g
        m_hat = mn / (1.0 - ADAM_B1 ** ADAM_STEP)
        v_hat = vn / (1.0 - ADAM_B2 ** ADAM_STEP)
        go_ref[...] = g
        d_ref[...] = -ADAM_LR * (m_hat / (jnp.sqrt(v_hat) + ADAM_EPS) + ADAM_WD * w_ref[...])
        mo_ref[...] = mn
        vo_ref[...] = vn

    row = pl.BlockSpec((tr, c), lambda li, i: (li * nb + i, 0))
    part_specs = [pl.BlockSpec((N_DEV, tr, c), functools.partial(lambda li, i, k: (0, jnp.where(li == k, i, 0), 0), k=k))
                  for k in range(ng)]
    return pl.pallas_call(
        body, name=name, grid=(ng, nb), in_specs=[row] + part_specs + [row, row],
        out_specs=[row] * 4, out_shape=[jax.ShapeDtypeStruct((rr, c), F32)] * 4, compiler_params=_params(("arbitrary", "arbitrary")),
    )(w, *gparts, m, v)


PACK_LANES = 128
PACK_ROWS = 8192


PACK_TILE = 8 * PACK_LANES


def _pack_rows(t):
    return -(-t.size // PACK_TILE) * 8


def _pack(vals):
    rows = []
    for t in vals:
        flat = t.reshape(-1)
        n_rows = _pack_rows(t)
        rows.append(jnp.pad(flat, (0, n_rows * PACK_LANES - flat.size)).reshape(n_rows, PACK_LANES))
    used = sum(r.shape[0] for r in rows)
    assert used <= PACK_ROWS, used
    return jnp.concatenate(rows + [jnp.zeros((PACK_ROWS - used, PACK_LANES), F32)], axis=0)


def _unpack(packed, like):
    out, off = [], 0
    for t in like:
        n_rows = _pack_rows(t)
        out.append(packed[off:off + n_rows].reshape(-1)[:t.size].reshape(t.shape))
        off += n_rows
    return out


def kernel(x, w_in, s5_lambda_re, s5_lambda_im, s5_log_step, s5_b_re, s5_b_im, s5_c_re, s5_c_im, s5_d, s5_glu_w, s5_glu_b, sgu_norm_g, sgu_norm_b, sgu_w, sgu_b, pool_w, pool_scale, dn_conv_w, dn_a_log, dn_dt_bias, dn_norm_g, w_out, ln1_g, ln1_b, w_up, w_down, ln2_g, ln2_b, loss_target, m_w_in, m_s5_lambda_re, m_s5_lambda_im, m_s5_log_step, m_s5_b_re, m_s5_b_im, m_s5_c_re, m_s5_c_im, m_s5_d, m_s5_glu_w, m_s5_glu_b, m_sgu_norm_g, m_sgu_norm_b, m_sgu_w, m_sgu_b, m_pool_w, m_pool_scale, m_dn_conv_w, m_dn_a_log, m_dn_dt_bias, m_dn_norm_g, m_w_out, m_ln1_g, m_ln1_b, m_w_up, m_w_down, m_ln2_g, m_ln2_b, v_w_in, v_s5_lambda_re, v_s5_lambda_im, v_s5_log_step, v_s5_b_re, v_s5_b_im, v_s5_c_re, v_s5_c_im, v_s5_d, v_s5_glu_w, v_s5_glu_b, v_sgu_norm_g, v_sgu_norm_b, v_sgu_w, v_sgu_b, v_pool_w, v_pool_scale, v_dn_conv_w, v_dn_a_log, v_dn_dt_bias, v_dn_norm_g, v_w_out, v_ln1_g, v_ln1_b, v_w_up, v_w_down, v_ln2_g, v_ln2_b):
    names = ("w_in", "s5_lambda_re", "s5_lambda_im", "s5_log_step", "s5_b_re", "s5_b_im", "s5_c_re", "s5_c_im", "s5_d", "s5_glu_w",
             "s5_glu_b", "sgu_norm_g", "sgu_norm_b", "sgu_w", "sgu_b", "pool_w", "pool_scale", "dn_conv_w", "dn_a_log", "dn_dt_bias",
             "dn_norm_g", "w_out", "ln1_g", "ln1_b", "w_up", "w_down", "ln2_g", "ln2_b")
    env = locals()
    w = {n: env[n] for n in names}
    m = {n: env["m_" + n] for n in names}
    v = {n: env["v_" + n] for n in names}

    wire = [{n: (w[n][i] if n == "dn_conv_w" else w[n][i].astype(MXU_DTYPE)) for n in SHARDED} for i in range(DEPTH)]
    small = [{n: w[n][i] for n in SMALL} for i in range(DEPTH)]
    ops = [_prep_small(small[i]) for i in range(DEPTH)]
    grads = [{} for _ in range(DEPTH)]
    recv = [{} for _ in range(DEPTH)]
    first = ("w_in", "s5_glu_w", "dn_conv_w", "w_out")

    def gather(layer, group):
        def take(outs):
            for n, t in zip(group, outs):
                ops[layer].update(_weight_views(n, t))
        return (lambda: [(wire[layer][n], False) for n in group]), take

    def scatter(layer, group, with_small=False):
        def make():
            sends = [(grads[layer][n], True) for n in group]
            if with_small:
                sends.append((_pack([jnp.stack([grads[i]["small"][n] for i in range(DEPTH)]) for n in SMALL]), False))
            return sends
        def take(outs):
            recv[layer].update(dict(zip(group + (("small",) if with_small else ()), outs)))
        return make, take

    make, take = gather(0, first)
    take(_exchange(make(), "gather_first"))
    fwd_hosts = {"l0_proj": gather(0, ("w_up",)), "l0_up": gather(0, ("w_down",)), "l0_down": gather(1, first),
                 "l1_proj": gather(1, ("w_up",)), "l1_up": gather(1, ("w_down",))}
    late = ("w_in", "s5_glu_w", "dn_conv_w")
    bwd_hosts = {"l1b_dpre": scatter(1, ("w_down",)), "l1b_dx1": scatter(1, ("w_up",)), "l1b_dmixed": scatter(1, ("w_out",)),
                 "l0b_dw_down": scatter(1, late),
                 "l0b_dpre": scatter(0, ("w_down",)), "l0b_dx1": scatter(0, ("w_up",)), "l0b_dmixed": scatter(0, ("w_out",)),
                 "l0b_dw_main": scatter(0, ("s5_glu_w", "dn_conv_w"), with_small=True), "l0b_dx": scatter(0, ("w_in",))}
    loss, grad_x = _local_step(x[0], loss_target[0], ops, small, fwd_hosts, bwd_hosts, grads)

    g_out, d_out, m_out, v_out = {}, {}, {}, {}
    for n in SHARDED:
        shp = w[n].shape
        pad = (-shp[1]) % 8
        def rows(t):
            return jnp.pad(t, ((0, 0), (0, pad), (0, 0))).reshape(shp[0] * (shp[1] + pad), shp[2])
        res = _adamw(rows(w[n]), [recv[i][n] for i in range(DEPTH)], rows(m[n]), rows(v[n]), "adamw_" + n)
        g_out[n], d_out[n], m_out[n], v_out[n] = (t.reshape(shp[0], shp[1] + pad, shp[2])[:, :shp[1]] for t in res)
    like = [w[n] for n in SMALL]
    res = _adamw(_pack(like), [recv[0]["small"]], _pack([m[n] for n in SMALL]), _pack([v[n] for n in SMALL]), "adamw_small")
    for dst, pk in zip((g_out, d_out, m_out, v_out), res):
        dst.update(dict(zip(SMALL, _unpack(pk, like))))

    total = lax.psum(loss[0, 0], MESH_AXES)
    return (total, grad_x[None], *[g_out[n] for n in names], *[d_out[n] for n in names],
            *[m_out[n] for n in names], *[v_out[n] for n in names])
```

```python
import functools
import math

import jax
import jax.numpy as jnp
from jax import lax
from jax.experimental import pallas as pl
from jax.experimental.pallas import tpu as pltpu

F32 = jnp.float32
BF16 = jnp.bfloat16
MXU_DTYPE = jnp.bfloat16
HI = lax.Precision.HIGHEST

N_DEV = 8
D_MODEL = 2048
DEPTH = 2
GROUP_WIDTH = 512
S5_GROUPS, S5_CH, S5_STATE = 32, 16, 64
S5_NS = S5_GROUPS * S5_STATE
SGU_CHUNK, SGU_HEADS = 128, 8
POOL_WINDOWS = (2, 4, 8, 16)
DN_HEADS, DN_HEAD_DIM, DN_CONV, DN_CHUNK = 4, 128, 4, 64
D_FF = 4 * D_MODEL
LN_EPS, RMS_EPS, L2_EPS = 1e-5, 1e-6, 1e-6
ALPHA = (2 * DEPTH) ** 0.25
MAIN_COLS = 4096
AB_PAD = 128
ADAM_LR, ADAM_B1, ADAM_B2, ADAM_EPS, ADAM_WD, ADAM_STEP = 0.001, 0.9, 0.999, 1e-08, 0.01, 10
VMEM_LIMIT = 56 * 1024 * 1024
MIX_ROWS = 512
WIDE_ROWS = 512
C_GELU = math.sqrt(2.0 / math.pi)


def _params(sem=None):
    return pltpu.CompilerParams(dimension_semantics=sem, vmem_limit_bytes=VMEM_LIMIT)


def _gelu(x):
    return 0.5 * x * (1.0 + jnp.tanh(C_GELU * (x + 0.044715 * x * x * x)))


def _gelu_grad(x):
    t = jnp.tanh(C_GELU * (x + 0.044715 * x * x * x))
    return 0.5 * (1.0 + t) + 0.5 * x * (1.0 - t * t) * C_GELU * (1.0 + 3.0 * 0.044715 * x * x)


def _sigmoid(x):
    return 1.0 / (1.0 + jnp.exp(-x))


def _silu(x):
    return x * _sigmoid(x)


def _silu_grad(x):
    s = _sigmoid(x)
    return s * (1.0 + x * (1.0 - s))


def _softplus(x):
    z = jnp.exp(-jnp.abs(x))
    small = z * (1.0 - z * (0.5 - z * (1.0 / 3.0)))
    return jnp.maximum(x, 0.0) + jnp.where(z < 1e-2, small, jnp.log(1.0 + z))


def _mx(x):
    return x.astype(MXU_DTYPE)


def _dot(a, b, dims="nn", precision=None):
    cd = {"nn": ((1,), (0,)), "nt": ((1,), (1,)), "tn": ((0,), (0,))}[dims]
    return lax.dot_general(a, b, (cd, ((), ())), preferred_element_type=F32, precision=precision)


def _mdot(a, b, dims="nn"):
    return _dot(_mx(a), _mx(b), dims)


MESH_AXES = ("x", "y", "c")
OFFSETS = [(dx, dy, dc) for dx in (0, 1) for dy in (0, 1) for dc in (0, 1)][1:]


def _me_and_peers():
    x, y, c = (lax.axis_index(a) for a in MESH_AXES)
    def flip(v, d):
        return 1 - v if d else v
    peers = [(flip(x, dx), flip(y, dy), flip(c, dc)) for dx, dy, dc in OFFSETS]
    def idx(p):
        return 4 * p[0] + 2 * p[1] + p[2]
    return idx((x, y, c)), peers, [idx(p) for p in peers]


SIBLING = OFFSETS.index((0, 0, 1))
SAME_CORE = [OFFSETS.index(f) for f in ((0, 1, 0), (1, 0, 0), (1, 1, 0))]


class _Comm:
    def __init__(self, ops):
        self.arrays = [a for a, _ in ops]
        self.scatter = [s for _, s in ops]
        self.n = n = len(ops)
        hbm = pl.BlockSpec(memory_space=pltpu.HBM)
        self.in_specs, self.out_specs = [hbm] * n, [hbm] * n
        self.out_shape = [jax.ShapeDtypeStruct(a.shape if s else (N_DEV,) + a.shape, a.dtype) for a, s in ops]
        npeer = len(OFFSETS)
        self.scratch = [pltpu.SemaphoreType.DMA((n, npeer)), pltpu.SemaphoreType.DMA((n, npeer)), pltpu.SemaphoreType.DMA((n,))]

    def _plan(self, ins, outs, sems, waiting):
        send_sems, recv_sems, local_sems = sems
        me, peers, peer_idx = _me_and_peers()

        def remote(k, d, src, dst, to):
            return pltpu.make_async_remote_copy(src_ref=src, dst_ref=dst, send_sem=send_sems.at[k, d], recv_sem=recv_sems.at[k, d],
                                                device_id=to, device_id_type=pl.DeviceIdType.MESH)
        plan = []
        for k in range(self.n):
            every = range(len(OFFSETS))
            if self.scatter[k]:
                local = pltpu.make_async_copy(ins[k].at[me], outs[k].at[me], local_sems.at[k])
                pushes = [remote(k, d, ins[k].at[peer_idx[d]], outs[k].at[me], peers[d]) for d in every]
                onward = []
            else:
                local = pltpu.make_async_copy(ins[k], outs[k].at[me], local_sems.at[k])
                pushes = [remote(k, d, ins[k], outs[k].at[me], peers[d]) for d in [SIBLING] + SAME_CORE]
                onward = SAME_CORE
            passed, arrivals = [], {}
            if waiting:
                passed = [(d, remote(k, d + 1, outs[k].at[peer_idx[d]], outs[k].at[peer_idx[d]], peers[SIBLING])) for d in onward]
                arrivals = {d: remote(k, d, outs[k].at[peer_idx[d]], outs[k].at[peer_idx[d]], peers[d]) for d in every}
            plan.append((local, pushes, passed, arrivals))
        return plan

    def start(self, ins, outs, sems):
        for local, pushes, _, _ in self._plan(ins, outs, sems, False):
            local.start()
            for cp in pushes:
                cp.start()

    def wait(self, ins, outs, sems):
        plan = self._plan(ins, outs, sems, True)
        for _, _, passed, arrivals in plan:
            for d, onward in passed:
                arrivals.pop(d).wait_recv()
                onward.start()
        for local, pushes, passed, arrivals in plan:
            for cp in arrivals.values():
                cp.wait_recv()
            for cp in pushes + [onward for _, onward in passed]:
                cp.wait_send()
            local.wait()


def _exchange(ops, name):
    cm = _Comm(ops)

    def body(*refs):
        ins, outs, sems = refs[:cm.n], refs[cm.n:2 * cm.n], refs[2 * cm.n:]
        cm.start(ins, outs, sems)
        cm.wait(ins, outs, sems)

    return pl.pallas_call(body, name=name, in_specs=cm.in_specs, out_specs=cm.out_specs, out_shape=cm.out_shape,
                          scratch_shapes=cm.scratch)(*cm.arrays)


def _matmul(a, b, *, mode, tm, tn, tk, out_dtype, name, a_fn=None, extras=(), epi=None, a_cols=None,
            b_slab=None, out_slab=None, comm=None):
    a_shape = a.shape if a_cols is None else (a.shape[0], a_cols)
    b_shape = b.shape if b_slab is None else (b.shape[1], N_DEV * b_slab)
    if mode == "nn":
        (m, k), n = a_shape, b_shape[1]
    elif mode == "nt":
        (m, k), n = a_shape, b_shape[0]
    else:
        (k, m), n = a_shape, b_shape[1]
    tm, tn, tk = min(tm, m), min(tn, n), min(tk, k)
    if b_slab is not None:
        tn, tk = (tn, min(tk, b_slab)) if mode == "nt" else (min(tn, b_slab), tk)
    assert m % tm == 0 and n % tn == 0 and k % tk == 0, (name, a.shape, b.shape, tm, tn, tk)
    gi, gj, nk = m // tm, n // tn, k // tk
    n_ex = len(extras)
    cm = _Comm(comm) if comm else None
    nc = cm.n if cm else 0

    def body(a_ref, b_ref, *rest):
        ex_refs, rest = rest[:n_ex], rest[n_ex:]
        c_ins, o_ref, c_outs, acc, sems = rest[:nc], rest[nc], rest[nc + 1:2 * nc + 1], rest[2 * nc + 1], rest[2 * nc + 2:]
        i, j, kk = pl.program_id(0), pl.program_id(1), pl.program_id(2)
        if cm:
            @pl.when((i == 0) & (j == 0) & (kk == 0))
            def _():
                cm.start(c_ins, c_outs, sems)

        av = a_ref[...]
        if a_fn is not None:
            av = a_fn(av)
        part = _dot(_mx(av), _mx(b_ref[...]), mode)

        def finish(r):
            if epi is not None:
                r = epi(r, *[e[...] for e in ex_refs])
            o_ref[...] = r.astype(out_dtype)

        if nk == 1:
            finish(part)
        else:
            @pl.when(kk == 0)
            def _():
                acc[...] = part

            @pl.when((kk > 0) & (kk < nk - 1))
            def _():
                acc[...] += part

            @pl.when(kk == nk - 1)
            def _():
                finish(acc[...] + part)

        if cm:
            @pl.when((i == gi - 1) & (j == gj - 1) & (kk == nk - 1))
            def _():
                cm.wait(c_ins, c_outs, sems)

    a_spec = pl.BlockSpec((tk, tm), lambda i, j, kk: (kk, i)) if mode == "tn" else pl.BlockSpec((tm, tk), lambda i, j, kk: (i, kk))
    if b_slab is None:
        b_spec = pl.BlockSpec((tn, tk), lambda i, j, kk: (j, kk)) if mode == "nt" else pl.BlockSpec((tk, tn), lambda i, j, kk: (kk, j))
    elif mode == "nt":
        assert b_slab % tk == 0
        b_spec = pl.BlockSpec((None, tn, tk), lambda i, j, kk: ((kk * tk) // b_slab, j, ((kk * tk) % b_slab) // tk))
    else:
        assert b_slab % tn == 0
        b_spec = pl.BlockSpec((None, tk, tn), lambda i, j, kk: ((j * tn) // b_slab, kk, ((j * tn) % b_slab) // tn))
    if out_slab is None:
        o_spec, o_shape = pl.BlockSpec((tm, tn), lambda i, j, kk: (i, j)), jax.ShapeDtypeStruct((m, n), out_dtype)
    else:
        assert out_slab % tn == 0 and n == N_DEV * out_slab
        o_spec = pl.BlockSpec((None, tm, tn), lambda i, j, kk: ((j * tn) // out_slab, i, ((j * tn) % out_slab) // tn))
        o_shape = jax.ShapeDtypeStruct((N_DEV, m, out_slab), out_dtype)
    ex_specs = [pl.BlockSpec(({None: tm, "tn": tn}.get(bs[0], bs[0]), tn if bs[1] is None else bs[1]),
                             functools.partial(lambda i, j, kk, f: f(i, j), f=im)) for (_, bs, im) in extras]
    res = pl.pallas_call(
        body,
        name=name,
        grid=(gi, gj, nk),
        in_specs=[a_spec, b_spec, *ex_specs] + (cm.in_specs if cm else []),
        out_specs=[o_spec] + (cm.out_specs if cm else []),
        out_shape=[o_shape] + (cm.out_shape if cm else []),
        scratch_shapes=[pltpu.VMEM((tm, tn) if nk > 1 else (8, 128), F32)] + (cm.scratch if cm else []),
        compiler_params=_params(("arbitrary",) * 3 if cm else ("parallel", "parallel", "arbitrary")),
    )(a, b, *[e[0] for e in extras], *(cm.arrays if cm else []))
    return (res[0], res[1:]) if cm else res[0]


def _ln_fwd(x, y, g, b, name):
    l, d = x.shape
    tl = min(WIDE_ROWS, l)

    def body(x_ref, y_ref, g_ref, b_ref, h_ref, o_ref, om_ref):
        h = ALPHA * x_ref[...] + y_ref[...]
        mu = jnp.mean(h, axis=-1, keepdims=True)
        c = h - mu
        var = jnp.mean(c * c, axis=-1, keepdims=True)
        h_ref[...] = h
        out = c * lax.rsqrt(var + LN_EPS) * g_ref[...] + b_ref[...]
        o_ref[...] = out
        om_ref[...] = out.astype(om_ref.dtype)

    row = pl.BlockSpec((tl, d), lambda i: (i, 0))
    vec = pl.BlockSpec((1, d), lambda i: (0, 0))
    return pl.pallas_call(
        body, name=name, grid=(l // tl,), in_specs=[row, row, vec, vec], out_specs=[row, row, row],
        out_shape=[jax.ShapeDtypeStruct((l, d), F32)] * 2 + [jax.ShapeDtypeStruct((l, d), MXU_DTYPE)],
        compiler_params=_params(("parallel",)),
    )(x, y, g, b)


def _ln_bwd(dout, h, g, name):
    l, d = h.shape
    tl = min(WIDE_ROWS, l)

    def body(do_ref, h_ref, g_ref, dh_ref, dhm_ref, dg_ref, db_ref):
        @pl.when(pl.program_id(0) == 0)
        def _():
            dg_ref[...] = jnp.zeros_like(dg_ref)
            db_ref[...] = jnp.zeros_like(db_ref)

        hv, do = h_ref[...], do_ref[...]
        mu = jnp.mean(hv, axis=-1, keepdims=True)
        c = hv - mu
        r = lax.rsqrt(jnp.mean(c * c, axis=-1, keepdims=True) + LN_EPS)
        xh = c * r
        dxh = do * g_ref[...]
        m1 = jnp.mean(dxh, axis=-1, keepdims=True)
        m2 = jnp.mean(dxh * xh, axis=-1, keepdims=True)
        dh = r * (dxh - m1 - xh * m2)
        dh_ref[...] = dh
        dhm_ref[...] = dh.astype(dhm_ref.dtype)
        dg_ref[...] += jnp.sum(do * xh, axis=0, keepdims=True)
        db_ref[...] += jnp.sum(do, axis=0, keepdims=True)

    row = pl.BlockSpec((tl, d), lambda i: (i, 0))
    vec = pl.BlockSpec((1, d), lambda i: (0, 0))
    return pl.pallas_call(
        body, name=name, grid=(l // tl,), in_specs=[row, row, vec], out_specs=[row, row, vec, vec],
        out_shape=[jax.ShapeDtypeStruct((l, d), F32), jax.ShapeDtypeStruct((l, d), MXU_DTYPE),
                   jax.ShapeDtypeStruct((1, d), F32), jax.ShapeDtypeStruct((1, d), F32)],
        compiler_params=_params(("arbitrary",)),
    )(dout, h, g)


def _loss_head(y, target):
    l, d = y.shape
    tl = min(WIDE_ROWS, l)

    def body(y_ref, t_ref, loss_ref, dy_ref):
        @pl.when(pl.program_id(0) == 0)
        def _():
            loss_ref[...] = jnp.zeros_like(loss_ref)

        e = y_ref[...] - t_ref[...]
        dy_ref[...] = e * (1.0 / d)
        s = jnp.sum(jnp.sum(e * e, axis=1, keepdims=True), axis=0, keepdims=True)
        loss_ref[...] += s * (0.5 / d)

    row = pl.BlockSpec((tl, d), lambda i: (i, 0))
    return pl.pallas_call(
        body, name="loss_head", grid=(l // tl,), in_specs=[row, row],
        out_specs=[pl.BlockSpec((1, 1), lambda i: (0, 0)), row],
        out_shape=[jax.ShapeDtypeStruct((1, 1), F32), jax.ShapeDtypeStruct((l, d), F32)],
        compiler_params=_params(("arbitrary",)),
    )(y, target)


def _s5_discretize(lam_re, lam_im, log_step, b_re, b_im):
    step = jnp.exp(log_step)[:, None]
    e = jnp.exp(lam_re * step)
    lbr, lbi = e * jnp.cos(lam_im * step), e * jnp.sin(lam_im * step)
    den = lam_re * lam_re + lam_im * lam_im
    qr = ((lbr - 1.0) * lam_re + lbi * lam_im) / den
    qi = (lbi * lam_re - (lbr - 1.0) * lam_im) / den
    bbr = qr[:, :, None] * b_re - qi[:, :, None] * b_im
    bbi = qr[:, :, None] * b_im + qi[:, :, None] * b_re
    return lbr, lbi, bbr, bbi


S5_TILES, S5_SLABS = 4, 8
S5_TILE_W, S5_SLAB_W = GROUP_WIDTH // S5_TILES, S5_NS // S5_TILES
S5_GPT = S5_GROUPS // S5_TILES


def _s5_compact(bbr, bbi, c_re, c_im):
    eye = jnp.eye(S5_GPT, dtype=F32)
    def bd(t):
        return jnp.einsum("tgph,gk->tghkp", t.reshape(S5_TILES, S5_GPT, S5_STATE, S5_CH), eye).reshape(S5_TILES, S5_TILE_W, S5_SLAB_W)
    def cd(t):
        return jnp.einsum("tghp,gk->tgpkh", t.reshape(S5_TILES, S5_GPT, S5_CH, S5_STATE), eye).reshape(S5_TILES, S5_SLAB_W, S5_TILE_W)
    return jnp.concatenate([bd(bbr), bd(bbi)], axis=0), jnp.concatenate([cd(c_re), -cd(c_im)], axis=0)


def _s5_uncompact_b(db):
    eye = jnp.eye(S5_GPT, dtype=F32)[None, :, None, :, None]
    def ex(t):
        d = jnp.sum(t.reshape(S5_TILES, S5_GPT, S5_CH, S5_GPT, S5_STATE) * eye, axis=3)
        return jnp.transpose(d, (0, 1, 3, 2)).reshape(S5_GROUPS, S5_STATE, S5_CH)
    return ex(db[:S5_TILES]), ex(db[S5_TILES:])


def _s5_uncompact_c(dc):
    eye = jnp.eye(S5_GPT, dtype=F32)[None, :, None, :, None]
    def ex(t):
        d = jnp.sum(t.reshape(S5_TILES, S5_GPT, S5_STATE, S5_GPT, S5_CH) * eye, axis=3)
        return jnp.transpose(d, (0, 1, 3, 2)).reshape(S5_GROUPS, S5_CH, S5_STATE)
    return ex(dc[:S5_TILES]), -ex(dc[S5_TILES:])


S5_ROWS = 512


def _s5_tile(j):
    t = j % S5_TILES
    return slice(t * S5_TILE_W, (t + 1) * S5_TILE_W)


def _s5_slab(j):
    return slice(j * S5_SLAB_W, (j + 1) * S5_SLAB_W)


def _s5_recur(src, lam_ref, carry, emit, n_rows, reverse, extra=()):
    ns = S5_NS
    lr, li = lam_ref[:, :ns], lam_ref[:, ns:]

    def step(t, c):
        row = (n_rows - 1 - t) if reverse else t
        cr, ci = c[0], c[1]
        nr = lr * cr - li * ci + src[pl.ds(row, 1), :ns]
        ni = lr * ci + li * cr + src[pl.ds(row, 1), ns:]
        return (nr, ni) + tuple(emit(row, nr, ni, cr, ci, c[2:]))

    fin = lax.fori_loop(0, n_rows, step, (carry[:, :ns], carry[:, ns:]) + tuple(extra))
    carry[:, :ns] = fin[0]
    carry[:, ns:] = fin[1]
    return fin[2:]


def _s5_fwd(proj, b, c, lam, d, name):
    l = proj.shape[0]
    tl = min(S5_ROWS, l)
    w = 2 * S5_NS

    def body(u_ref, b_ref, c_ref, lam_ref, d_ref, hs_ref, y_ref, bu, carry):
        @pl.when(pl.program_id(0) == 0)
        def _():
            carry[...] = jnp.zeros_like(carry)

        u = u_ref[...]
        um = _mx(u)
        for j in range(S5_SLABS):
            bu[:, _s5_slab(j)] = _dot(um[:, _s5_tile(j)], b_ref[j])

        def emit(row, nr, ni, cr, ci, extra):
            hs_ref[pl.ds(row, 1), :S5_NS] = nr
            hs_ref[pl.ds(row, 1), S5_NS:] = ni
            return extra

        _s5_recur(bu, lam_ref, carry, emit, tl, False)
        for t in range(S5_TILES):
            acc = _dot(_mx(hs_ref[:, _s5_slab(t)]), c_ref[t]) + _dot(_mx(hs_ref[:, _s5_slab(S5_TILES + t)]), c_ref[S5_TILES + t])
            y_ref[:, _s5_tile(t)] = acc + d_ref[:, _s5_tile(t)] * u[:, _s5_tile(t)]

    row = lambda width: pl.BlockSpec((tl, width), lambda i: (i, 0))
    full = lambda a: pl.BlockSpec(a.shape, lambda i: (0,) * a.ndim)
    return pl.pallas_call(
        body, name=name, grid=(l // tl,), in_specs=[row(GROUP_WIDTH), full(b), full(c), full(lam), full(d)],
        out_specs=[row(w), row(GROUP_WIDTH)],
        out_shape=[jax.ShapeDtypeStruct((l, w), F32), jax.ShapeDtypeStruct((l, GROUP_WIDTH), F32)],
        scratch_shapes=[pltpu.VMEM((tl, w), F32), pltpu.VMEM((1, w), F32)], compiler_params=_params(("arbitrary",)),
    )(proj, b, c, lam, d)


def _s5_bwd(dy, hs, proj, b, c, lam_conj, d, name):
    l = dy.shape[0]
    tl = min(S5_ROWS, l)
    nb = l // tl
    w = 2 * S5_NS

    def body(dy_ref, hs_ref, u_ref, b_ref, c_ref, lam_ref, d_ref, du_ref, db_ref, dc_ref, dl_ref, dh, adj, carry):
        @pl.when(pl.program_id(0) == 0)
        def _():
            carry[...] = jnp.zeros_like(carry)
            db_ref[...] = jnp.zeros_like(db_ref)
            dc_ref[...] = jnp.zeros_like(dc_ref)
            dl_ref[...] = jnp.zeros_like(dl_ref)

        dyv = dy_ref[...]
        dym, um = _mx(dyv), _mx(u_ref[...])
        for j in range(S5_SLABS):
            dh[:, _s5_slab(j)] = _dot(dym[:, _s5_tile(j)], c_ref[j], "nt")

        def emit(row, nr, ni, cr, ci, extra):
            adj[pl.ds(row, 1), :S5_NS] = nr
            adj[pl.ds(row, 1), S5_NS:] = ni
            hr, hi = hs_ref[pl.ds(row, 1), :S5_NS], hs_ref[pl.ds(row, 1), S5_NS:]
            return extra[0] + cr * hr + ci * hi, extra[1] + ci * hr - cr * hi

        dl = _s5_recur(dh, lam_ref, carry, emit, tl, True, extra=(dl_ref[:, :S5_NS], dl_ref[:, S5_NS:]))
        dl_ref[:, :S5_NS] = dl[0]
        dl_ref[:, S5_NS:] = dl[1]
        for t in range(S5_TILES):
            acc = (_dot(_mx(adj[:, _s5_slab(t)]), b_ref[t], "nt")
                   + _dot(_mx(adj[:, _s5_slab(S5_TILES + t)]), b_ref[S5_TILES + t], "nt"))
            du_ref[:, _s5_tile(t)] = (acc + d_ref[:, _s5_tile(t)] * dyv[:, _s5_tile(t)]).astype(du_ref.dtype)
        for j in range(S5_SLABS):
            dc_ref[j] += _dot(_mx(hs_ref[:, _s5_slab(j)]), dym[:, _s5_tile(j)], "tn")
            db_ref[j] += _dot(um[:, _s5_tile(j)], _mx(adj[:, _s5_slab(j)]), "tn")

    row = lambda width: pl.BlockSpec((tl, width), lambda i: (nb - 1 - i, 0))
    full = lambda a: pl.BlockSpec(a.shape, lambda i: (0,) * a.ndim)
    acc3 = lambda shape: pl.BlockSpec(shape, lambda i: (0, 0, 0))
    return pl.pallas_call(
        body, name=name, grid=(nb,),
        in_specs=[row(GROUP_WIDTH), row(w), row(GROUP_WIDTH), full(b), full(c), full(lam_conj), full(d)],
        out_specs=[row(GROUP_WIDTH), acc3(b.shape), acc3(c.shape), pl.BlockSpec((1, w), lambda i: (0, 0))],
        out_shape=[jax.ShapeDtypeStruct((l, GROUP_WIDTH), BF16), jax.ShapeDtypeStruct(b.shape, F32),
                   jax.ShapeDtypeStruct(c.shape, F32), jax.ShapeDtypeStruct((1, w), F32)],
        scratch_shapes=[pltpu.VMEM((tl, w), F32), pltpu.VMEM((tl, w), F32), pltpu.VMEM((1, w), F32)],
        compiler_params=_params(("arbitrary",)),
    )(dy, hs, proj, b, c, lam_conj, d)


def _s5_glu_fwd(y, glu_w, glu_b, name):
    l, d = y.shape
    tl = min(2 * MIX_ROWS, l)

    def body(y_ref, w_ref, b_ref, o_ref):
        yg = _gelu(y_ref[...])
        z = _mdot(yg, w_ref[...]) + b_ref[...]
        o_ref[...] = (yg * _sigmoid(z)).astype(o_ref.dtype)

    return pl.pallas_call(
        body, name=name, grid=(l // tl,),
        in_specs=[pl.BlockSpec((tl, d), lambda i: (i, 0)), pl.BlockSpec((d, d), lambda i: (0, 0)), pl.BlockSpec((1, d), lambda i: (0, 0))],
        out_specs=pl.BlockSpec((tl, d), lambda i: (i, 0)), out_shape=jax.ShapeDtypeStruct((l, d), BF16),
        compiler_params=_params(("parallel",)),
    )(y, glu_w, glu_b)


def _s5_glu_bwd(dmixed, y, proj, glu_w, glu_b, name):
    l, d = y.shape
    tl = min(2 * MIX_ROWS, l)

    def body(do_ref, y_ref, u_ref, w_ref, b_ref, dy_ref, dz_ref, yg_ref, db_ref, dd_ref):
        @pl.when(pl.program_id(0) == 0)
        def _():
            db_ref[...] = jnp.zeros_like(db_ref)
            dd_ref[...] = jnp.zeros_like(dd_ref)

        yv, do = y_ref[...], do_ref[...]
        yg = _gelu(yv)
        gate = _sigmoid(_mdot(yg, w_ref[...]) + b_ref[...])
        dz = do * yg * gate * (1.0 - gate)
        dyg = do * gate + _mdot(dz, w_ref[...], "nt")
        dy = dyg * _gelu_grad(yv)
        dy_ref[...] = dy
        dz_ref[...] = dz.astype(dz_ref.dtype)
        yg_ref[...] = yg.astype(yg_ref.dtype)
        db_ref[...] += jnp.sum(dz, axis=0, keepdims=True)
        dd_ref[...] += jnp.sum(dy * u_ref[...], axis=0, keepdims=True)

    row = pl.BlockSpec((tl, d), lambda i: (i, 0))
    vec = pl.BlockSpec((1, d), lambda i: (0, 0))
    return pl.pallas_call(
        body, name=name, grid=(l // tl,),
        in_specs=[row, row, row, pl.BlockSpec((d, d), lambda i: (0, 0)), vec],
        out_specs=[row, row, row, vec, vec],
        out_shape=[jax.ShapeDtypeStruct((l, d), F32), jax.ShapeDtypeStruct((l, d), BF16), jax.ShapeDtypeStruct((l, d), BF16),
                   jax.ShapeDtypeStruct((1, d), F32), jax.ShapeDtypeStruct((1, d), F32)],
        compiler_params=_params(("arbitrary",)),
    )(dmixed, y, proj, glu_w, glu_b)


def _sgu_pair(w_ref, x, j, dims):
    lo = lax.broadcasted_iota(jnp.int32, x.shape, 1) < (GROUP_WIDTH // SGU_HEADS)
    xb = _mx(x)
    r0 = _dot(w_ref[2 * j], xb, dims)
    r1 = _dot(w_ref[2 * j + 1], xb, dims)
    return jnp.where(lo, r0, r1)


def _sgu_norm(v, g, b):
    mu = jnp.mean(v, axis=-1, keepdims=True)
    c = v - mu
    r = lax.rsqrt(jnp.mean(c * c, axis=-1, keepdims=True) + LN_EPS)
    return c * r, r


def _sgu_fwd(proj, norm_g, norm_b, wm, bfull, name):
    l = proj.shape[0]
    tl = min(MIX_ROWS, l)
    gw = GROUP_WIDTH

    def body(zu_ref, zv_ref, g_ref, b_ref, w_ref, bf_ref, o_ref):
        for c in range(tl // SGU_CHUNK):
            rows = slice(c * SGU_CHUNK, (c + 1) * SGU_CHUNK)
            u = _gelu(zu_ref[rows, :])
            vh, _ = _sgu_norm(_gelu(zv_ref[rows, :]), None, None)
            vn = vh * g_ref[...] + b_ref[...]
            for j in range(gw // 128):
                cols = slice(j * 128, (j + 1) * 128)
                mixed = _sgu_pair(w_ref, vn[:, cols], j, "nn") + bf_ref[:, cols]
                o_ref[rows, cols] = (u[:, cols] * mixed).astype(o_ref.dtype)

    vec = pl.BlockSpec((1, gw), lambda i: (0, 0))
    return pl.pallas_call(
        body, name=name, grid=(l // tl,),
        in_specs=[pl.BlockSpec((tl, gw), lambda i: (i, 1)), pl.BlockSpec((tl, gw), lambda i: (i, 2)), vec, vec,
                  pl.BlockSpec((SGU_HEADS, SGU_CHUNK, SGU_CHUNK), lambda i: (0, 0, 0)), pl.BlockSpec((SGU_CHUNK, gw), lambda i: (0, 0))],
        out_specs=pl.BlockSpec((tl, gw), lambda i: (i, 0)), out_shape=jax.ShapeDtypeStruct((l, gw), BF16),
        compiler_params=_params(("parallel",)),
    )(proj, proj, norm_g, norm_b, wm, bfull)


def _sgu_bwd(dmixed, proj, norm_g, norm_b, wm, bfull, name):
    l = proj.shape[0]
    tl = min(MIX_ROWS, l)
    gw = GROUP_WIDTH
    hd = gw // SGU_HEADS

    def body(do_ref, zu_ref, zv_ref, g_ref, b_ref, w_ref, bf_ref, dzu_ref, dzv_ref, dw_ref, dbf_ref, dg_ref, dnb_ref):
        @pl.when(pl.program_id(0) == 0)
        def _():
            dw_ref[...] = jnp.zeros_like(dw_ref)
            dbf_ref[...] = jnp.zeros_like(dbf_ref)
            dg_ref[...] = jnp.zeros_like(dg_ref)
            dnb_ref[...] = jnp.zeros_like(dnb_ref)

        for c in range(tl // SGU_CHUNK):
            rows = slice(c * SGU_CHUNK, (c + 1) * SGU_CHUNK)
            zu, zv, do = zu_ref[rows, :], zv_ref[rows, :], do_ref[rows, :]
            u = _gelu(zu)
            vh, r = _sgu_norm(_gelu(zv), None, None)
            vn = vh * g_ref[...] + b_ref[...]
            dvn_parts, mixed_parts = [], []
            for j in range(gw // 128):
                cols = slice(j * 128, (j + 1) * 128)
                vb = vn[:, cols]
                mixed_parts.append(_sgu_pair(w_ref, vb, j, "nn") + bf_ref[:, cols])
                dm = do[:, cols] * u[:, cols]
                dvn_parts.append(_sgu_pair(w_ref, dm, j, "tn"))
                lo = lax.broadcasted_iota(jnp.int32, dm.shape, 1) < hd
                dw_ref[2 * j] += _mdot(jnp.where(lo, dm, 0.0), vb, "nt")
                dw_ref[2 * j + 1] += _mdot(jnp.where(lo, 0.0, dm), vb, "nt")
                dbf_ref[:, cols] += dm
            mixed = jnp.concatenate(mixed_parts, axis=1)
            dvn = jnp.concatenate(dvn_parts, axis=1)
            dzu_ref[rows, :] = (do * mixed * _gelu_grad(zu)).astype(dzu_ref.dtype)
            dg_ref[...] += jnp.sum(dvn * vh, axis=0, keepdims=True)
            dnb_ref[...] += jnp.sum(dvn, axis=0, keepdims=True)
            dvh = dvn * g_ref[...]
            m1 = jnp.mean(dvh, axis=-1, keepdims=True)
            m2 = jnp.mean(dvh * vh, axis=-1, keepdims=True)
            dv = r * (dvh - m1 - vh * m2)
            dzv_ref[rows, :] = (dv * _gelu_grad(zv)).astype(dzv_ref.dtype)

    vec = pl.BlockSpec((1, gw), lambda i: (0, 0))
    row = pl.BlockSpec((tl, gw), lambda i: (i, 0))
    wspec = pl.BlockSpec((SGU_HEADS, SGU_CHUNK, SGU_CHUNK), lambda i: (0, 0, 0))
    bspec = pl.BlockSpec((SGU_CHUNK, gw), lambda i: (0, 0))
    return pl.pallas_call(
        body, name=name, grid=(l // tl,),
        in_specs=[pl.BlockSpec((tl, gw), lambda i: (i, 1)), pl.BlockSpec((tl, gw), lambda i: (i, 1)), pl.BlockSpec((tl, gw), lambda i: (i, 2)),
                  vec, vec, wspec, bspec],
        out_specs=[row, row, wspec, bspec, vec, vec],
        out_shape=[jax.ShapeDtypeStruct((l, gw), BF16), jax.ShapeDtypeStruct((l, gw), BF16),
                   jax.ShapeDtypeStruct((SGU_HEADS, SGU_CHUNK, SGU_CHUNK), F32), jax.ShapeDtypeStruct((SGU_CHUNK, gw), F32),
                   jax.ShapeDtypeStruct((1, gw), F32), jax.ShapeDtypeStruct((1, gw), F32)],
        compiler_params=_params(("arbitrary",)),
    )(dmixed, proj, proj, norm_g, norm_b, wm, bfull)


HALO = 16


def _window_sums(ext, n_rows, forward):
    def sh(x, k):
        return pltpu.roll(x, (n_rows - k) if forward else k, axis=0)
    s2 = ext + sh(ext, 1)
    s4 = s2 + sh(s2, 2)
    s8 = s4 + sh(s4, 4)
    s16 = s8 + sh(s8, 8)
    return (s2, s4, s8, s16)


def _pool_fwd(proj, pool_w, scale, name):
    l = proj.shape[0]
    tl = min(MIX_ROWS, l)
    gw = GROUP_WIDTH
    pg = gw // len(POOL_WINDOWS)

    def body(x_ref, halo_ref, w_ref, s_ref, o_ref, p_ref):
        i = pl.program_id(0)
        x = x_ref[...]
        halo = jnp.where(i > 0, halo_ref[...], 0.0)
        ext = jnp.concatenate([halo, x], axis=0)
        sums = _window_sums(ext, tl + HALO, False)
        t = i * tl + lax.broadcasted_iota(jnp.int32, (tl, pg), 0)
        for gi, win in enumerate(POOL_WINDOWS):
            cols = slice(gi * pg, (gi + 1) * pg)
            cnt = jnp.minimum(t + 1, win).astype(F32)
            pooled = sums[gi][HALO:, cols] / cnt - x[:, cols]
            p_ref[:, cols] = pooled
            o_ref[:, cols] = (_mdot(pooled, w_ref[gi]) * s_ref[:, cols]).astype(o_ref.dtype)

    row = pl.BlockSpec((tl, gw), lambda i: (i, 0))
    return pl.pallas_call(
        body, name=name, grid=(l // tl,),
        in_specs=[pl.BlockSpec((tl, gw), lambda i: (i, 3)),
                  pl.BlockSpec((HALO, gw), lambda i: (jnp.maximum(i * (tl // HALO) - 1, 0), 3)),
                  pl.BlockSpec((len(POOL_WINDOWS), pg, pg), lambda i: (0, 0, 0)), pl.BlockSpec((1, gw), lambda i: (0, 0))],
        out_specs=[row, row], out_shape=[jax.ShapeDtypeStruct((l, gw), BF16), jax.ShapeDtypeStruct((l, gw), F32)],
        compiler_params=_params(("parallel",)),
    )(proj, proj, pool_w, scale)


def _pool_bwd_map(dmixed, pooled, pool_w, scale, name):
    l, gw = pooled.shape
    tl = min(MIX_ROWS, l)
    ng = len(POOL_WINDOWS)
    pg = gw // ng

    def body(do_ref, p_ref, w_ref, s_ref, dp_ref, dw_ref, ds_ref):
        @pl.when(pl.program_id(0) == 0)
        def _():
            dw_ref[...] = jnp.zeros_like(dw_ref)
            ds_ref[...] = jnp.zeros_like(ds_ref)

        for gi in range(ng):
            cols = slice(gi * pg, (gi + 1) * pg)
            do, pooled_g = do_ref[:, cols], p_ref[:, cols]
            mixed = _mdot(pooled_g, w_ref[gi])
            ds_ref[:, cols] += jnp.sum(do * mixed, axis=0, keepdims=True)
            dm = do * s_ref[:, cols]
            dw_ref[gi] += _mdot(pooled_g, dm, "tn")
            dp_ref[:, cols] = _mdot(dm, w_ref[gi], "nt")

    row = pl.BlockSpec((tl, gw), lambda i: (i, 0))
    wspec = pl.BlockSpec((ng, pg, pg), lambda i: (0, 0, 0))
    vec = pl.BlockSpec((1, gw), lambda i: (0, 0))
    return pl.pallas_call(
        body, name=name, grid=(l // tl,),
        in_specs=[pl.BlockSpec((tl, gw), lambda i: (i, 2)), row, wspec, vec], out_specs=[row, wspec, vec],
        out_shape=[jax.ShapeDtypeStruct((l, gw), F32), jax.ShapeDtypeStruct((ng, pg, pg), F32), jax.ShapeDtypeStruct((1, gw), F32)],
        compiler_params=_params(("arbitrary",)),
    )(dmixed, pooled, pool_w, scale)


def _pool_bwd_window(dpooled, name):
    l, gw = dpooled.shape
    tl = min(MIX_ROWS, l)
    nb = l // tl
    pg = gw // len(POOL_WINDOWS)

    def body(d_ref, halo_ref, o_ref):
        i = pl.program_id(0)
        d = d_ref[...]
        halo = jnp.where(i < nb - 1, halo_ref[...], 0.0)
        ext = jnp.concatenate([d, halo], axis=0)
        t = i * tl + lax.broadcasted_iota(jnp.int32, (tl + HALO, pg), 0)
        for gi, win in enumerate(POOL_WINDOWS):
            cols = slice(gi * pg, (gi + 1) * pg)
            cnt = jnp.minimum(t + 1, win).astype(F32)
            sums = _window_sums(ext[:, cols] / cnt, tl + HALO, True)
            o_ref[:, cols] = (sums[gi][:tl, :] - d[:, cols]).astype(o_ref.dtype)

    row = pl.BlockSpec((tl, gw), lambda i: (i, 0))
    return pl.pallas_call(
        body, name=name, grid=(nb,),
        in_specs=[row, pl.BlockSpec((HALO, gw), lambda i: (jnp.minimum((i + 1) * (tl // HALO), l // HALO - 1), 0))],
        out_specs=row, out_shape=jax.ShapeDtypeStruct((l, gw), BF16), compiler_params=_params(("parallel",)),
    )(dpooled, dpooled)


CONV_HALO = 8
QKV_BLK = 4


def _head_sums(x):
    parts = []
    for hd in range(DN_HEADS):
        s = jnp.sum(x[:, hd * DN_HEAD_DIM:(hd + 1) * DN_HEAD_DIM], axis=-1, keepdims=True)
        parts.append(jnp.broadcast_to(s, (x.shape[0], DN_HEAD_DIM)))
    return jnp.concatenate(parts, axis=1)


def _gdn_pre_fwd(proj, proj_ab, conv_w, a_log, dt_bias, name):
    l = proj.shape[0]
    tl = min(MIX_ROWS, l)
    gw = GROUP_WIDTH

    def body(xq, xk, xv, hq, hk, hv, w_ref, ab_ref, al_ref, dt_ref, qn_ref, kn_ref, v_ref, cq_ref, ck_ref, cv_ref, gb_ref):
        i = pl.program_id(0)
        for p, (x_ref, h_ref, c_ref) in enumerate(((xq, hq, cq_ref), (xk, hk, ck_ref), (xv, hv, cv_ref))):
            ext = jnp.concatenate([jnp.where(i > 0, h_ref[...], 0.0), x_ref[...]], axis=0)
            conv = jnp.zeros((tl, gw), F32)
            for j in range(DN_CONV):
                k = DN_CONV - 1 - j
                shifted = ext if k == 0 else pltpu.roll(ext, k, axis=0)
                conv = conv + shifted[CONV_HALO:, :] * w_ref[j:j + 1, p * gw:(p + 1) * gw]
            c_ref[...] = conv
            s = _silu(conv)
            if p == 2:
                v_ref[...] = s
            else:
                r = lax.rsqrt(_head_sums(s * s) + L2_EPS)
                (qn_ref if p == 0 else kn_ref)[...] = s * r * (DN_HEAD_DIM ** -0.5 if p == 0 else 1.0)
        ab = ab_ref[...]
        lane = lax.broadcasted_iota(jnp.int32, ab.shape, 1)
        g = -jnp.exp(al_ref[...]) * _softplus(ab + dt_ref[...])
        gb_ref[...] = jnp.where(lane < DN_HEADS, g, _sigmoid(ab))

    def xs(b):
        return pl.BlockSpec((tl, gw), lambda i: (i, b))

    def hs(b):
        return pl.BlockSpec((CONV_HALO, gw), lambda i: (jnp.maximum(i * (tl // CONV_HALO) - 1, 0), b))

    row = pl.BlockSpec((tl, gw), lambda i: (i, 0))
    abrow = pl.BlockSpec((tl, AB_PAD), lambda i: (i, 0))
    abvec = pl.BlockSpec((1, AB_PAD), lambda i: (0, 0))
    return pl.pallas_call(
        body, name=name, grid=(l // tl,),
        in_specs=[xs(QKV_BLK), xs(QKV_BLK + 1), xs(QKV_BLK + 2), hs(QKV_BLK), hs(QKV_BLK + 1), hs(QKV_BLK + 2),
                  pl.BlockSpec((DN_CONV, 3 * gw), lambda i: (0, 0)), abrow, abvec, abvec],
        out_specs=[row] * 6 + [abrow],
        out_shape=[jax.ShapeDtypeStruct((l, gw), F32)] * 6 + [jax.ShapeDtypeStruct((l, AB_PAD), F32)],
        compiler_params=_params(("parallel",)),
    )(proj, proj, proj, proj, proj, proj, conv_w, proj_ab, a_log, dt_bias)


def _gdn_pre_bwd(dq, dk, dv, cq, ck, cv, dgb, gb, proj_ab, a_log, dt_bias, name):
    l, gw = cq.shape
    tl = min(MIX_ROWS, l)

    def body(dq_ref, dk_ref, dv_ref, cq_ref, ck_ref, cv_ref, dgb_ref, gb_ref, ab_ref, al_ref, dt_ref,
             dcq_ref, dck_ref, dcv_ref, dab_ref, dal_ref, ddt_ref):
        @pl.when(pl.program_id(0) == 0)
        def _():
            dal_ref[...] = jnp.zeros_like(dal_ref)
            ddt_ref[...] = jnp.zeros_like(ddt_ref)

        for p, (d_ref, c_ref, o_ref) in enumerate(((dq_ref, cq_ref, dcq_ref), (dk_ref, ck_ref, dck_ref), (dv_ref, cv_ref, dcv_ref))):
            c, d = c_ref[...], d_ref[...]
            if p == 2:
                ds = d
            else:
                s = _silu(c)
                r = lax.rsqrt(_head_sums(s * s) + L2_EPS)
                ds = (DN_HEAD_DIM ** -0.5 if p == 0 else 1.0) * r * (d - s * r * r * _head_sums(d * s))
            o_ref[...] = ds * _silu_grad(c)
        ab, dgb_v, gb_v = ab_ref[...], dgb_ref[...], gb_ref[...]
        lane = lax.broadcasted_iota(jnp.int32, ab.shape, 1)
        is_g = lane < DN_HEADS
        dpre = dgb_v * (-jnp.exp(al_ref[...])) * _sigmoid(ab + dt_ref[...])
        dab_ref[...] = jnp.where(is_g, dpre, dgb_v * gb_v * (1.0 - gb_v)).astype(dab_ref.dtype)
        dal_ref[...] += jnp.sum(jnp.where(is_g, dgb_v * gb_v, 0.0), axis=0, keepdims=True)
        ddt_ref[...] += jnp.sum(jnp.where(is_g, dpre, 0.0), axis=0, keepdims=True)

    row = pl.BlockSpec((tl, gw), lambda i: (i, 0))
    abrow = pl.BlockSpec((tl, AB_PAD), lambda i: (i, 0))
    abvec = pl.BlockSpec((1, AB_PAD), lambda i: (0, 0))
    return pl.pallas_call(
        body, name=name, grid=(l // tl,),
        in_specs=[row] * 6 + [abrow, abrow, abrow, abvec, abvec],
        out_specs=[row, row, row, abrow, abvec, abvec],
        out_shape=[jax.ShapeDtypeStruct((l, gw), F32)] * 3 + [jax.ShapeDtypeStruct((l, AB_PAD), BF16),
                   jax.ShapeDtypeStruct((1, AB_PAD), F32), jax.ShapeDtypeStruct((1, AB_PAD), F32)],
        compiler_params=_params(("arbitrary",)),
    )(dq, dk, dv, cq, ck, cv, dgb, gb, proj_ab, a_log, dt_bias)


def _conv_bwd(dc, proj, col_blk, w_part, name):
    l, gw = dc.shape
    tl = min(MIX_ROWS, l)
    nb = l // tl

    def body(dc_ref, halo_ref, x_ref, w_ref, dx_ref, dw_ref):
        i = pl.program_id(0)

        @pl.when(i == 0)
        def _():
            dw_ref[...] = jnp.zeros_like(dw_ref)

        ext = jnp.concatenate([dc_ref[...], jnp.where(i < nb - 1, halo_ref[...], 0.0)], axis=0)
        x = x_ref[...]
        dx = jnp.zeros((tl, gw), F32)
        rid = lax.broadcasted_iota(jnp.int32, (8, gw), 0)
        dw = jnp.zeros((8, gw), F32)
        for j in range(DN_CONV):
            k = DN_CONV - 1 - j
            shifted = (ext if k == 0 else pltpu.roll(ext, tl + CONV_HALO - k, axis=0))[:tl, :]
            dx = dx + shifted * w_ref[j:j + 1, :]
            dw = dw + jnp.where(rid == j, jnp.sum(x * shifted, axis=0, keepdims=True), 0.0)
        dx_ref[...] = dx.astype(dx_ref.dtype)
        dw_ref[...] += dw

    row = pl.BlockSpec((tl, gw), lambda i: (i, 0))
    return pl.pallas_call(
        body, name=name, grid=(nb,),
        in_specs=[row, pl.BlockSpec((CONV_HALO, gw), lambda i: (jnp.minimum((i + 1) * (tl // CONV_HALO), l // CONV_HALO - 1), 0)),
                  pl.BlockSpec((tl, gw), lambda i: (i, col_blk)), pl.BlockSpec((DN_CONV, gw), lambda i: (0, 0))],
        out_specs=[row, pl.BlockSpec((8, gw), lambda i: (0, 0))],
        out_shape=[jax.ShapeDtypeStruct((l, gw), BF16), jax.ShapeDtypeStruct((8, gw), F32)],
        compiler_params=_params(("arbitrary",)),
    )(dc, dc, proj, w_part)


TERMS_CHUNKS = 8


def _bdot(a, b, dims="nn", precision=None):
    cd = {"nn": ((2,), (1,)), "nt": ((2,), (2,)), "tn": ((1,), (1,))}[dims]
    return lax.dot_general(a, b, (cd, ((0,), (0,))), preferred_element_type=F32, precision=precision)


def _bmdot(a, b, dims="nn"):
    return _bdot(_mx(a), _mx(b), dims)


def _bdot3(a, b, dims="nn"):
    ah, bh = a.astype(BF16), b.astype(BF16)
    al, bl = (a - ah.astype(F32)).astype(BF16), (b - bh.astype(F32)).astype(BF16)
    return _bdot(ah, bh, dims) + (_bdot(ah, bl, dims) + _bdot(al, bh, dims))


def _wy_terms(q, k, v, gcol, beta, t=None):
    c = DN_CHUNK
    ii = lax.broadcasted_iota(jnp.int32, (1, c, c), 1)
    jj = lax.broadcasted_iota(jnp.int32, (1, c, c), 2)
    tril, strict = ii >= jj, ii > jj
    grow = jnp.sum(jnp.where(ii == jj, gcol, 0.0), axis=1, keepdims=True)
    gc_col = jnp.sum(jnp.where(tril, grow, 0.0), axis=2, keepdims=True)
    gc_row = jnp.sum(jnp.where(ii <= jj, gcol, 0.0), axis=1, keepdims=True)
    dec = jnp.exp(jnp.where(tril, gc_col - gc_row, -1e30))
    kb, vb = k * beta, v * beta
    kk = _bmdot(kb, k, "nt")
    if t is None:
        a = jnp.where(strict, kk * dec, 0.0)
        d = jnp.where((ii >> 3) == (jj >> 3), a, 0.0)
        t = jnp.where(ii == jj, 1.0, 0.0) - d
        p = _bdot(d, d, precision=HI)
        t = t + _bdot(t, p, precision=HI)
        t = t + _bdot(t, _bdot(p, p, precision=HI), precision=HI)
        for sh in (3, 4, 5):
            below = ((ii >> (sh + 1)) == (jj >> (sh + 1))) & ((ii >> sh) > (jj >> sh))
            t = t - _bdot3(t, _bdot3(jnp.where(below, a, 0.0), t))
    eg = jnp.exp(gc_col)
    gc_last = gc_col[:, c - 1:c, :]
    kbg = kb * eg
    qk0 = _bmdot(q, k, "nt")
    e2 = jnp.exp(gc_last - gc_col)
    return dict(ii=ii, jj=jj, tril=tril, strict=strict, dec=dec, kb=kb, vb=vb, kk=kk, t=t, eg=eg, kbg=kbg,
                qk0=qk0, qk=jnp.where(tril, qk0 * dec, 0.0), qg=q * eg, e2=e2, kt=k * e2, gl=jnp.exp(gc_last))


def _to_heads(x, g):
    return jnp.concatenate([x[:, h * DN_HEAD_DIM:(h + 1) * DN_HEAD_DIM].reshape(g, DN_CHUNK, DN_HEAD_DIM)
                            for h in range(DN_HEADS)], axis=0)


def _from_heads(t, ref, g):
    for h in range(DN_HEADS):
        ref[:, h * DN_HEAD_DIM:(h + 1) * DN_HEAD_DIM] = t[h * g:(h + 1) * g].reshape(g * DN_CHUNK, DN_HEAD_DIM).astype(ref.dtype)


def _head_columns(gbv, first_lane, g):
    lane = lax.broadcasted_iota(jnp.int32, gbv.shape, 1)
    return jnp.concatenate([jnp.sum(jnp.where(lane == first_lane + h, gbv, 0.0), axis=1, keepdims=True).reshape(g, DN_CHUNK, 1)
                            for h in range(DN_HEADS)], axis=0)


def _gdn_terms_fwd(qn, kn, v, gb, name):
    l = qn.shape[0]
    n_chunks = l // DN_CHUNK
    g = min(TERMS_CHUNKS, n_chunks)
    rows, c, nh = g * DN_CHUNK, DN_CHUNK, DN_HEADS

    def body(q_ref, k_ref, v_ref, gb_ref, u_ref, w_ref, qg_ref, kt_ref, qk_ref, t_ref, gl_ref):
        gbv = gb_ref[...]
        x = _wy_terms(_to_heads(q_ref[...], g), _to_heads(k_ref[...], g), _to_heads(v_ref[...], g),
                      _head_columns(gbv, 0, g), _head_columns(gbv, nh, g))
        _from_heads(_bmdot(x["t"], x["vb"]), u_ref, g)
        _from_heads(_bmdot(x["t"], x["kbg"]), w_ref, g)
        _from_heads(x["qg"], qg_ref, g)
        _from_heads(x["kt"], kt_ref, g)
        for h in range(nh):
            qk_ref[:, h] = x["qk"][h * g:(h + 1) * g]
            t_ref[:, h] = x["t"][h * g:(h + 1) * g]
            gl_ref[:, h] = jnp.broadcast_to(x["gl"][h * g:(h + 1) * g], (g, 1, 128))

    row = pl.BlockSpec((rows, GROUP_WIDTH), lambda i: (i, 0))
    sq = pl.BlockSpec((g, nh, c, c), lambda i: (i, 0, 0, 0))
    glb = pl.BlockSpec((g, nh, 1, 128), lambda i: (i, 0, 0, 0))
    return pl.pallas_call(
        body, name=name, grid=(n_chunks // g,), in_specs=[row, row, row, pl.BlockSpec((rows, AB_PAD), lambda i: (i, 0))],
        out_specs=[row] * 4 + [sq, sq, glb],
        out_shape=[jax.ShapeDtypeStruct((l, GROUP_WIDTH), F32)] * 4 + [jax.ShapeDtypeStruct((n_chunks, nh, c, c), F32)] * 2
        + [jax.ShapeDtypeStruct((n_chunks, nh, 1, 128), F32)],
        compiler_params=_params(("parallel",)),
    )(qn, kn, v, gb)


REC_CHUNKS = 8


def _rec_specs(n_chunks, reverse):
    c, hd, nh = DN_CHUNK, DN_HEAD_DIM, DN_HEADS
    g = min(REC_CHUNKS, n_chunks)
    nb = n_chunks // g
    blk_of = (lambda n: nb - 1 - n) if reverse else (lambda n: n)
    return g, nb, (pl.BlockSpec((g * c, GROUP_WIDTH), lambda n: (blk_of(n), 0)), pl.BlockSpec((g, nh, c, c), lambda n: (blk_of(n), 0, 0, 0)),
                   pl.BlockSpec((g, nh, 1, 128), lambda n: (blk_of(n), 0, 0, 0)), pl.BlockSpec((g, nh, hd, hd), lambda n: (blk_of(n), 0, 0, 0)))


def _gdn_rec_fwd(u, w, qg, kt, qk, gl, name):
    l = u.shape[0]
    n_chunks = l // DN_CHUNK
    hd, nh, c = DN_HEAD_DIM, DN_HEADS, DN_CHUNK
    g, nb, (blk, sq, glb, st) = _rec_specs(n_chunks, False)
    heads = range(nh)

    def body(u_ref, w_ref, qg_ref, kt_ref, qk_ref, gl_ref, o_ref, vn_ref, s_ref, state):
        @pl.when(pl.program_id(0) == 0)
        def _():
            state[...] = jnp.zeros_like(state)

        def cols(h):
            return slice(h * hd, (h + 1) * hd)
        for ci in range(g):
            rows = slice(ci * c, (ci + 1) * c)
            s = [state[h] for h in heads]
            ws = [_mdot(w_ref[rows, cols(h)], s[h]) for h in heads]
            vn = [u_ref[rows, cols(h)] - ws[h] for h in heads]
            kv = [_mdot(kt_ref[rows, cols(h)], vn[h], "tn") for h in heads]
            for h in heads:
                state[h] = s[h] * gl_ref[ci, h] + kv[h]
            o1 = [_mdot(qg_ref[rows, cols(h)], s[h]) for h in heads]
            o2 = [_mdot(qk_ref[ci, h], vn[h]) for h in heads]
            for h in heads:
                s_ref[ci, h] = s[h]
                o_ref[rows, cols(h)] = o1[h] + o2[h]
                vn_ref[rows, cols(h)] = vn[h]

    return pl.pallas_call(
        body, name=name, grid=(nb,), in_specs=[blk, blk, blk, blk, sq, glb], out_specs=[blk, blk, st],
        out_shape=[jax.ShapeDtypeStruct((l, GROUP_WIDTH), F32)] * 2 + [jax.ShapeDtypeStruct((n_chunks, nh, hd, hd), F32)],
        scratch_shapes=[pltpu.VMEM((nh, hd, hd), F32)], compiler_params=_params(("arbitrary",)),
    )(u, w, qg, kt, qk, gl)


def _gdn_rec_bwd(do, w, qg, kt, vn, qk, gl, states, name):
    l = do.shape[0]
    n_chunks = l // DN_CHUNK
    hd, nh, c = DN_HEAD_DIM, DN_HEADS, DN_CHUNK
    g, nb, (blk, sq, glb, st) = _rec_specs(n_chunks, True)
    heads = range(nh)

    def body(do_ref, w_ref, qg_ref, kt_ref, vn_ref, qk_ref, gl_ref, s_ref, dvn_ref, dw_ref, dkt_ref, dqg_ref, dqk_ref, dgl_ref, dstate):
        @pl.when(pl.program_id(0) == 0)
        def _():
            dstate[...] = jnp.zeros_like(dstate)

        def cols(h):
            return slice(h * hd, (h + 1) * hd)
        tril = lax.broadcasted_iota(jnp.int32, (c, c), 0) >= lax.broadcasted_iota(jnp.int32, (c, c), 1)
        for ci in reversed(range(g)):
            rows = slice(ci * c, (ci + 1) * c)
            ds = [dstate[h] for h in heads]
            dout = [do_ref[rows, cols(h)] for h in heads]
            a1 = [_mdot(qk_ref[ci, h], dout[h], "tn") for h in heads]
            a2 = [_mdot(kt_ref[rows, cols(h)], ds[h]) for h in heads]
            dvn = [a1[h] + a2[h] for h in heads]
            b1 = [_mdot(qg_ref[rows, cols(h)], dout[h], "tn") for h in heads]
            b2 = [_mdot(w_ref[rows, cols(h)], dvn[h], "tn") for h in heads]
            for h in heads:
                dstate[h] = b1[h] + gl_ref[ci, h] * ds[h] - b2[h]
            for h in heads:
                s, vnew = s_ref[ci, h], vn_ref[rows, cols(h)]
                dvn_ref[rows, cols(h)] = dvn[h]
                dw_ref[rows, cols(h)] = -_mdot(dvn[h], s, "nt")
                dkt_ref[rows, cols(h)] = _mdot(vnew, ds[h], "nt")
                dqg_ref[rows, cols(h)] = _mdot(dout[h], s, "nt")
                dqk_ref[ci, h] = jnp.where(tril, _mdot(dout[h], vnew, "nt"), 0.0)
                dgl = jnp.sum(jnp.sum(ds[h] * s, axis=1, keepdims=True), axis=0, keepdims=True)
                dgl_ref[ci, h] = jnp.broadcast_to(dgl, (1, 128))

    return pl.pallas_call(
        body, name=name, grid=(nb,), in_specs=[blk] * 5 + [sq, glb, st], out_specs=[blk] * 4 + [sq, glb],
        out_shape=[jax.ShapeDtypeStruct((l, GROUP_WIDTH), F32)] * 4 + [jax.ShapeDtypeStruct((n_chunks, nh, c, c), F32),
                                                                       jax.ShapeDtypeStruct((n_chunks, nh, 1, 128), F32)],
        scratch_shapes=[pltpu.VMEM((nh, hd, hd), F32)], compiler_params=_params(("arbitrary",)),
    )(do, w, qg, kt, vn, qk, gl, states)


def _gdn_terms_bwd(qn, kn, v, gb, t_inv, dvn, dw, dkt, dqg, dqk, dgl, name):
    l = qn.shape[0]
    n_chunks = l // DN_CHUNK
    g = min(TERMS_CHUNKS, n_chunks)
    rows, c, nh = g * DN_CHUNK, DN_CHUNK, DN_HEADS

    def body(q_ref, k_ref, v_ref, gb_ref, t_ref, dvn_ref, dw_ref, dkt_ref, dqg_ref, dqk_ref, dgl_ref, dq_ref, dk_ref, dv_ref, dgb_ref):
        gbv = gb_ref[...]
        q, k, vv = _to_heads(q_ref[...], g), _to_heads(k_ref[...], g), _to_heads(v_ref[...], g)
        beta = _head_columns(gbv, nh, g)
        t = jnp.concatenate([t_ref[:, h] for h in range(nh)], axis=0)
        x = _wy_terms(q, k, vv, _head_columns(gbv, 0, g), beta, t=t)
        ii, jj, strict = x["ii"], x["jj"], x["strict"]
        du, dwv, dktv, dqgv = (_to_heads(r[...], g) for r in (dvn_ref, dw_ref, dkt_ref, dqg_ref))
        dqkv = jnp.concatenate([dqk_ref[:, h] for h in range(nh)], axis=0)
        dglv = jnp.concatenate([dgl_ref[:, h] for h in range(nh)], axis=0)[:, :, 0:1]
        dt = _bmdot(du, x["vb"], "nt") + _bmdot(dwv, x["kbg"], "nt")
        dvb = _bmdot(t, du, "tn")
        dkbg = _bmdot(t, dwv, "tn")
        da = jnp.where(strict, -_bdot3(_bdot3(t, dt, "tn"), t, "nt"), 0.0)
        dkk = da * x["dec"]
        dqk0 = dqkv * x["dec"]
        e = (da * x["kk"] + dqkv * x["qk0"]) * x["dec"]
        dkb = _bmdot(dkk, k) + dkbg * x["eg"]
        dk = _bmdot(dkk, x["kb"], "tn") + _bmdot(dqk0, q, "tn") + dktv * x["e2"] + dkb * beta
        dq = _bmdot(dqk0, k) + dqgv * x["eg"]
        s_kt = jnp.sum(dktv * x["kt"], axis=2, keepdims=True)
        dgc_c = (jnp.sum(e, axis=2, keepdims=True) + jnp.sum(dqgv * x["qg"], axis=2, keepdims=True) - s_kt
                 + jnp.sum(dkbg * x["kbg"], axis=2, keepdims=True))
        dgc_last = jnp.sum(s_kt, axis=1, keepdims=True) + dglv * x["gl"]
        rid = lax.broadcasted_iota(jnp.int32, (1, c, 1), 1)
        dgc_c = dgc_c + jnp.where(rid == c - 1, dgc_last, 0.0)
        dgc_r = jnp.sum(jnp.where(ii == jj, dgc_c, 0.0), axis=1, keepdims=True) - jnp.sum(e, axis=1, keepdims=True)
        dg = jnp.sum(jnp.where(jj >= ii, dgc_r, 0.0), axis=2, keepdims=True)
        dbeta = jnp.sum(dkb * k, axis=2, keepdims=True) + jnp.sum(dvb * vv, axis=2, keepdims=True)
        _from_heads(dq, dq_ref, g)
        _from_heads(dk, dk_ref, g)
        _from_heads(dvb * beta, dv_ref, g)
        lane = lax.broadcasted_iota(jnp.int32, gbv.shape, 1)
        dgb = jnp.zeros(gbv.shape, F32)
        for h in range(nh):
            dgb = dgb + jnp.where(lane == h, dg[h * g:(h + 1) * g].reshape(rows, 1), 0.0)
            dgb = dgb + jnp.where(lane == nh + h, dbeta[h * g:(h + 1) * g].reshape(rows, 1), 0.0)
        dgb_ref[...] = dgb

    row = pl.BlockSpec((rows, GROUP_WIDTH), lambda i: (i, 0))
    abrow = pl.BlockSpec((rows, AB_PAD), lambda i: (i, 0))
    sq = pl.BlockSpec((g, nh, c, c), lambda i: (i, 0, 0, 0))
    glb = pl.BlockSpec((g, nh, 1, 128), lambda i: (i, 0, 0, 0))
    return pl.pallas_call(
        body, name=name, grid=(n_chunks // g,), in_specs=[row, row, row, abrow, sq, row, row, row, row, sq, glb],
        out_specs=[row, row, row, abrow],
        out_shape=[jax.ShapeDtypeStruct((l, GROUP_WIDTH), F32)] * 3 + [jax.ShapeDtypeStruct((l, AB_PAD), F32)],
        compiler_params=_params(("parallel",)),
    )(qn, kn, v, gb, t_inv, dvn, dw, dkt, dqg, dqk, dgl)


def _gdn_post_fwd(o, proj, norm_g4, name):
    l, gw = o.shape
    tl = min(2 * MIX_ROWS, l)

    def body(o_ref, gate_ref, g_ref, out_ref):
        ov = o_ref[...]
        r = lax.rsqrt(_head_sums(ov * ov) * (1.0 / DN_HEAD_DIM) + RMS_EPS)
        out_ref[...] = (ov * r * g_ref[...] * _silu(gate_ref[...])).astype(out_ref.dtype)

    row = pl.BlockSpec((tl, gw), lambda i: (i, 0))
    return pl.pallas_call(
        body, name=name, grid=(l // tl,),
        in_specs=[row, pl.BlockSpec((tl, gw), lambda i: (i, 7)), pl.BlockSpec((1, gw), lambda i: (0, 0))],
        out_specs=row, out_shape=jax.ShapeDtypeStruct((l, gw), BF16), compiler_params=_params(("parallel",)),
    )(o, proj, norm_g4)


def _gdn_post_bwd(dmixed, o, proj, norm_g4, name):
    l, gw = o.shape
    tl = min(2 * MIX_ROWS, l)

    def body(d_ref, o_ref, gate_ref, g_ref, do_ref, dgate_ref, dng_ref):
        @pl.when(pl.program_id(0) == 0)
        def _():
            dng_ref[...] = jnp.zeros_like(dng_ref)

        ov, gate, d = o_ref[...], gate_ref[...], d_ref[...]
        r = lax.rsqrt(_head_sums(ov * ov) * (1.0 / DN_HEAD_DIM) + RMS_EPS)
        oh = ov * r
        sg = _silu(gate)
        dgate_ref[...] = (d * oh * g_ref[...] * _silu_grad(gate)).astype(dgate_ref.dtype)
        dng_ref[...] += jnp.sum(d * sg * oh, axis=0, keepdims=True)
        doh = d * g_ref[...] * sg
        do_ref[...] = r * (doh - oh * _head_sums(doh * oh) * (1.0 / DN_HEAD_DIM))

    row = pl.BlockSpec((tl, gw), lambda i: (i, 0))
    vec = pl.BlockSpec((1, gw), lambda i: (0, 0))
    return pl.pallas_call(
        body, name=name, grid=(l // tl,),
        in_specs=[pl.BlockSpec((tl, gw), lambda i: (i, 3)), row, pl.BlockSpec((tl, gw), lambda i: (i, 7)), vec],
        out_specs=[row, row, vec],
        out_shape=[jax.ShapeDtypeStruct((l, gw), F32), jax.ShapeDtypeStruct((l, gw), BF16), jax.ShapeDtypeStruct((1, gw), F32)],
        compiler_params=_params(("arbitrary",)),
    )(dmixed, o, proj, norm_g4)


def _run(hosts, name, fn):
    h = hosts.get(name)
    if h is None:
        return fn(None)
    res, outs = fn(h[0]())
    h[1](outs)
    return res


def _layer_fwd(x, xm, w, li, hosts):
    l = x.shape[0]
    nm = f"l{li}_"
    proj = _run(hosts, nm + "proj", lambda ops: _matmul(
        xm, w["w_main"], mode="nn", tm=1024, tn=1024, tk=2048,out_dtype=F32, name=nm + "proj", comm=ops))
    proj_ab = _matmul(xm, w["w_ab"], mode="nn", tm=1024, tn=AB_PAD, tk=2048, out_dtype=F32, name=nm + "proj_ab")
    hs, y = _s5_fwd(proj, w["s5_b"], w["s5_c"], w["s5_lam"], w["s5_d"], nm + "s5")
    m_s5 = _s5_glu_fwd(y, w["s5_glu_w"], w["s5_glu_b"], nm + "s5_glu")
    m_sgu = _sgu_fwd(proj, w["sgu_norm_g"], w["sgu_norm_b"], w["sgu_wm"], w["sgu_bfull"], nm + "sgu")
    m_pool, pooled = _pool_fwd(proj, w["pool_w"], w["pool_scale"], nm + "pool")
    qn, kn, v, cq, ck, cv, gb = _gdn_pre_fwd(proj, proj_ab, w["dn_conv_w"], w["dn_a_log"], w["dn_dt_bias"], nm + "gdn_pre")
    u, wy, qg, kt, qk, t_inv, gl = _gdn_terms_fwd(qn, kn, v, gb, nm + "gdn_terms")
    o, vn, states = _gdn_rec_fwd(u, wy, qg, kt, qk, gl, nm + "gdn_rec")
    m_dn = _gdn_post_fwd(o, proj, w["dn_norm_g4"], nm + "gdn_post")
    mixed = jnp.concatenate([m_s5, m_sgu, m_pool, m_dn], axis=1)
    y1 = _matmul(mixed, w["w_out"], mode="nn", tm=1024, tn=1024, tk=2048, out_dtype=F32, name=nm + "out_proj")
    h1, x1, x1m = _ln_fwd(x, y1, w["ln1_g"], w["ln1_b"], nm + "ln1")
    r = _run(hosts, nm + "up", lambda ops: _matmul(
        x1m, w["w_up"], mode="nn", tm=1024, tn=1024, tk=2048,out_dtype=BF16, name=nm + "up",
        epi=lambda acc: jnp.maximum(acc, 0.0), b_slab=w["w_up"].shape[2], comm=ops))
    y2 = _run(hosts, nm + "down", lambda ops: _matmul(
        r, w["w_down"], mode="nn", tm=1024, tn=1024, tk=2048,out_dtype=F32, name=nm + "down", a_fn=lambda a: a * a, comm=ops))
    h2, x2, x2m = _ln_fwd(x1, y2, w["ln2_g"], w["ln2_b"], nm + "ln2")
    saved = dict(xm=xm, proj=proj, proj_ab=proj_ab, hs=hs, y=y, pooled=pooled, qn=qn, kn=kn, v=v, cq=cq, ck=ck, cv=cv, gb=gb,
                 wy=wy, qg=qg, kt=kt, qk=qk, t_inv=t_inv, gl=gl, vn=vn, o=o, states=states, mixed=mixed, h1=h1, x1m=x1m,
                 r=r, h2=h2)
    return x2, x2m, saved


def _layer_bwd(dx2, s, w, small, li, hosts, g):
    nm = f"l{li}b_"
    l = dx2.shape[0]
    gw = GROUP_WIDTH
    wire = MXU_DTYPE
    dh2, dh2m, g["ln2_g"], g["ln2_b"] = _ln_bwd(dx2, s["h2"], w["ln2_g"], nm + "ln2")
    g["w_down"] = _run(hosts, nm + "dw_down", lambda ops: _matmul(
        s["r"], dh2m, mode="tn", tm=1024, tn=1024, tk=2048,out_dtype=wire, name=nm + "dw_down", a_fn=lambda a: a * a,
        comm=ops)).reshape(N_DEV, D_FF // N_DEV, D_MODEL)
    dpre = _run(hosts, nm + "dpre", lambda ops: _matmul(
        dh2m, w["w_down"], mode="nt", tm=1024, tn=1024, tk=2048,out_dtype=BF16, name=nm + "dpre",
        extras=[(s["r"], (None, None), lambda i, j: (i, j))], epi=lambda acc, r: acc * 2.0 * r.astype(F32), comm=ops))
    g["w_up"] = _matmul(s["x1m"], dpre, mode="tn", tm=1024, tn=1024, tk=2048,out_dtype=wire, name=nm + "dw_up",
                        out_slab=D_FF // N_DEV)
    dx1 = _run(hosts, nm + "dx1", lambda ops: _matmul(
        dpre, w["w_up"], mode="nt", tm=1024, tn=1024, tk=2048,out_dtype=F32, name=nm + "dx1",
        extras=[(dh2, (None, None), lambda i, j: (i, j))], epi=lambda acc, e: acc + ALPHA * e,
        b_slab=w["w_up"].shape[2], comm=ops))
    dh1, dh1m, g["ln1_g"], g["ln1_b"] = _ln_bwd(dx1, s["h1"], w["ln1_g"], nm + "ln1")
    g["w_out"] = _matmul(s["mixed"], dh1m, mode="tn", tm=1024, tn=1024, tk=2048,out_dtype=wire,
                         name=nm + "dw_out").reshape(N_DEV, D_MODEL // N_DEV, D_MODEL)
    dmixed = _run(hosts, nm + "dmixed", lambda ops: _matmul(
        dh1m, w["w_out"], mode="nt", tm=1024, tn=1024, tk=2048,out_dtype=F32, name=nm + "dmixed", comm=ops))
    proj, proj_ab = s["proj"], s["proj_ab"]
    dy, dz, yg, g["s5_glu_b"], g["s5_d"] = _s5_glu_bwd(dmixed, s["y"], proj, w["s5_glu_w"], w["s5_glu_b"], nm + "s5_glu")
    g["s5_glu_w"] = _matmul(yg, dz, mode="tn", tm=gw, tn=gw, tk=1024, out_dtype=wire,
                            name=nm + "dw_glu").reshape(N_DEV, gw // N_DEV, gw)
    du_s5, g["s5_b"], g["s5_c"], g["s5_lam"] = _s5_bwd(dy, s["hs"], proj, w["s5_b"], w["s5_c"], w["s5_lam_conj"], w["s5_d"], nm + "s5")
    dzu, dzv, g["sgu_w"], g["sgu_bfull"], g["sgu_norm_g"], g["sgu_norm_b"] = _sgu_bwd(
        dmixed, proj, w["sgu_norm_g"], w["sgu_norm_b"], w["sgu_wm"], w["sgu_bfull"], nm + "sgu")
    dpooled, g["pool_w"], g["pool_scale"] = _pool_bwd_map(dmixed, s["pooled"], w["pool_w"], w["pool_scale"], nm + "pool_map")
    dp = _pool_bwd_window(dpooled, nm + "pool_win")
    do, dgate, g["dn_norm_g4"] = _gdn_post_bwd(dmixed, s["o"], proj, w["dn_norm_g4"], nm + "gdn_post")
    dvn, dwy, dkt, dqg, dqk, dgl = _gdn_rec_bwd(do, s["wy"], s["qg"], s["kt"], s["vn"], s["qk"], s["gl"], s["states"], nm + "gdn_rec")
    dq, dk, dv, dgb = _gdn_terms_bwd(s["qn"], s["kn"], s["v"], s["gb"], s["t_inv"], dvn, dwy, dkt, dqg, dqk, dgl, nm + "gdn_terms")
    dcq, dck, dcv, dab, g["dn_a_log"], g["dn_dt_bias"] = _gdn_pre_bwd(
        dq, dk, dv, s["cq"], s["ck"], s["cv"], dgb, s["gb"], proj_ab, w["dn_a_log"], w["dn_dt_bias"], nm + "gdn_pre")
    dxs, dws = [], []
    for p, dc in enumerate((dcq, dck, dcv)):
        dxp, dwp = _conv_bwd(dc, proj, QKV_BLK + p, w["dn_conv_w"][:, p * gw:(p + 1) * gw], nm + f"conv{p}")
        dxs.append(dxp)
        dws.append(dwp)
    dconv = jnp.concatenate(dws, axis=1)
    g["dn_conv_w"] = jnp.transpose(dconv.reshape(dconv.shape[0], N_DEV, 3 * gw // N_DEV), (1, 0, 2))
    dproj = jnp.concatenate([du_s5, dzu, dzv, dp] + dxs + [dgate], axis=1)
    xm = s["xm"]
    g["small"] = _unprep_grads(g, small)
    dw_main = _run(hosts, nm + "dw_main", lambda ops: _matmul(
        xm, dproj, mode="tn", tm=1024, tn=1024, tk=2048,out_dtype=wire, name=nm + "dw_main", comm=ops))
    dw_ab = _matmul(xm, dab, mode="tn", tm=1024, tn=AB_PAD, tk=1024, out_dtype=wire, name=nm + "dw_ab")
    dw_in = jnp.concatenate([dw_main, dw_ab[:, :2 * DN_HEADS]], axis=1)
    g["w_in"] = jnp.transpose(dw_in.reshape(D_MODEL, N_DEV, dw_in.shape[1] // N_DEV), (1, 0, 2))
    return _run(hosts, nm + "dx", lambda ops: _matmul(
        dproj, w["w_main"], mode="nt", tm=1024, tn=1024, tk=2048, out_dtype=F32, name=nm + "dx",
        extras=[(dh1, (None, None), lambda i, j: (i, j)), (dab, (None, AB_PAD), lambda i, j: (i, 0)),
                (w["w_ab"], ("tn", AB_PAD), lambda i, j: (j, 0))],
        epi=lambda acc, e, da, wab: acc + ALPHA * e + _dot(_mx(da), _mx(wab), "nt"), comm=ops))


SMALL = ("s5_lambda_re", "s5_lambda_im", "s5_log_step", "s5_b_re", "s5_b_im", "s5_c_re", "s5_c_im", "s5_d", "s5_glu_b",
         "sgu_norm_g", "sgu_norm_b", "sgu_w", "sgu_b", "pool_w", "pool_scale", "dn_a_log", "dn_dt_bias", "dn_norm_g",
         "ln1_g", "ln1_b", "ln2_g", "ln2_b")
SHARDED = ("w_in", "s5_glu_w", "dn_conv_w", "w_out", "w_up", "w_down")


def _pad_lanes(v, width=AB_PAD):
    return jnp.pad(v.reshape(1, -1), ((0, 0), (0, width - v.size)))


def _prep_small(p):
    mx = MXU_DTYPE
    lbr, lbi, bbr, bbi = _s5_discretize(p["s5_lambda_re"], p["s5_lambda_im"], p["s5_log_step"], p["s5_b_re"], p["s5_b_im"])
    b_compact, c_compact = _s5_compact(bbr, bbi, p["s5_c_re"], p["s5_c_im"])
    causal = jnp.tril(jnp.ones((SGU_CHUNK, SGU_CHUNK), F32))
    return dict(
        s5_b=b_compact.astype(mx), s5_c=c_compact.astype(mx),
        s5_lam=jnp.concatenate([lbr.reshape(1, -1), lbi.reshape(1, -1)], axis=1),
        s5_lam_conj=jnp.concatenate([lbr.reshape(1, -1), -lbi.reshape(1, -1)], axis=1),
        s5_d=p["s5_d"].reshape(1, -1), s5_glu_b=p["s5_glu_b"].reshape(1, -1),
        sgu_norm_g=p["sgu_norm_g"].reshape(1, -1), sgu_norm_b=p["sgu_norm_b"].reshape(1, -1),
        sgu_wm=(p["sgu_w"] * causal).astype(mx), sgu_bfull=jnp.repeat(p["sgu_b"].T, GROUP_WIDTH // SGU_HEADS, axis=1),
        pool_w=p["pool_w"].astype(mx), pool_scale=p["pool_scale"].reshape(1, -1),
        dn_a_log=_pad_lanes(p["dn_a_log"]), dn_dt_bias=_pad_lanes(p["dn_dt_bias"]),
        dn_norm_g4=jnp.tile(p["dn_norm_g"].reshape(1, -1), (1, DN_HEADS)),
        ln1_g=p["ln1_g"].reshape(1, -1), ln1_b=p["ln1_b"].reshape(1, -1),
        ln2_g=p["ln2_g"].reshape(1, -1), ln2_b=p["ln2_b"].reshape(1, -1),
    )


def _weight_views(name, t):
    if name == "w_in":
        w_in = jnp.transpose(t, (1, 0, 2)).reshape(t.shape[1], N_DEV * t.shape[2])
        pad = AB_PAD - (w_in.shape[1] - MAIN_COLS)
        return dict(w_main=w_in[:, :MAIN_COLS], w_ab=jnp.pad(w_in[:, MAIN_COLS:], ((0, 0), (0, pad))))
    if name == "dn_conv_w":
        return dict(dn_conv_w=jnp.transpose(t, (1, 0, 2)).reshape(t.shape[1], N_DEV * t.shape[2]))
    if name == "w_up":
        return dict(w_up=t)
    return {name: t.reshape(N_DEV * t.shape[1], t.shape[2])}


def _unprep_grads(g, p):
    causal = jnp.tril(jnp.ones((SGU_CHUNK, SGU_CHUNK), F32))
    dbbr, dbbi = _s5_uncompact_b(g["s5_b"])
    dc_re, dc_im = _s5_uncompact_c(g["s5_c"])
    dlbr, dlbi = g["s5_lam"][0, :S5_NS].reshape(S5_GROUPS, S5_STATE), g["s5_lam"][0, S5_NS:].reshape(S5_GROUPS, S5_STATE)
    _, vjp = jax.vjp(_s5_discretize, p["s5_lambda_re"], p["s5_lambda_im"], p["s5_log_step"], p["s5_b_re"], p["s5_b_im"])
    d_lre, d_lim, d_step, d_bre, d_bim = vjp((dlbr, dlbi, dbbr, dbbi))
    hd = GROUP_WIDTH // SGU_HEADS
    return dict(
        s5_lambda_re=d_lre, s5_lambda_im=d_lim, s5_log_step=d_step, s5_b_re=d_bre, s5_b_im=d_bim, s5_c_re=dc_re, s5_c_im=dc_im,
        s5_d=g["s5_d"].reshape(S5_GROUPS, S5_CH), s5_glu_b=g["s5_glu_b"].reshape(-1),
        sgu_norm_g=g["sgu_norm_g"].reshape(-1), sgu_norm_b=g["sgu_norm_b"].reshape(-1), sgu_w=g["sgu_w"] * causal,
        sgu_b=jnp.sum(g["sgu_bfull"].reshape(SGU_CHUNK, SGU_HEADS, hd), axis=2).T,
        pool_w=g["pool_w"], pool_scale=g["pool_scale"].reshape(-1),
        dn_a_log=g["dn_a_log"][0, :DN_HEADS], dn_dt_bias=g["dn_dt_bias"][0, :DN_HEADS],
        dn_norm_g=jnp.sum(g["dn_norm_g4"].reshape(DN_HEADS, DN_HEAD_DIM), axis=0),
        ln1_g=g["ln1_g"].reshape(-1), ln1_b=g["ln1_b"].reshape(-1), ln2_g=g["ln2_g"].reshape(-1), ln2_b=g["ln2_b"].reshape(-1),
    )


def _local_step(x, target, ops, small, fwd_hosts, bwd_hosts, grads):
    saved = []
    h, hm = x, x.astype(MXU_DTYPE)
    for i in range(DEPTH):
        h, hm, s = _layer_fwd(h, hm, ops[i], i, fwd_hosts)
        saved.append(s)
    loss, dh = _loss_head(h, target)
    for i in reversed(range(DEPTH)):
        dh = _layer_bwd(dh, saved[i], ops[i], small[i], i, bwd_hosts, grads[i])
    return loss, dh


def _adamw(w, gparts, m, v, name):
    rr, c = w.shape
    ng = len(gparts)
    r = rr // ng
    lanes = -(-c // 128) * 128
    tr = r
    while tr * lanes * 4 * N_DEV > (4 << 20) and tr % 16 == 0:
        tr //= 2
    nb = r // tr

    def body(w_ref, *rest):
        g_refs, (m_ref, v_ref, go_ref, d_ref, mo_ref, vo_ref) = rest[:ng], rest[ng:]
        layer = pl.program_id(0)
        g = jnp.zeros(m_ref.shape, F32)
        for li in range(ng):
            gl = g_refs[li][0].astype(F32)
            for s in range(1, N_DEV):
                gl = gl + g_refs[li][s].astype(F32)
            g = jnp.where(layer == li, gl, g)
        mn = ADAM_B1 * m_ref[...] + (1.0 - ADAM_B1) * g
        vn = ADAM_B2 * v_ref[...] + (1.0 - ADAM_B2) * g * g
        m_hat = mn / (1.0 - ADAM_B1 ** ADAM_STEP)
        v_hat = vn / (1.0 - ADAM_B2 ** ADAM_STEP)
        go_ref[...] = g
        d_ref[...] = -ADAM_LR * (m_hat / (jnp.sqrt(v_hat) + ADAM_EPS) + ADAM_WD * w_ref[...])
        mo_ref[...] = mn
        vo_ref[...] = vn

    row = pl.BlockSpec((tr, c), lambda li, i: (li * nb + i, 0))
    part_specs = [pl.BlockSpec((N_DEV, tr, c), functools.partial(lambda li, i, k: (0, jnp.where(li == k, i, 0), 0), k=k))
                  for k in range(ng)]
    return pl.pallas_call(
        body, name=name, grid=(ng, nb), in_specs=[row] + part_specs + [row, row],
        out_specs=[row] * 4, out_shape=[jax.ShapeDtypeStruct((rr, c), F32)] * 4, compiler_params=_params(("arbitrary", "arbitrary")),
    )(w, *gparts, m, v)


PACK_LANES = 128
PACK_ROWS = 8192


PACK_TILE = 8 * PACK_LANES


def _pack_rows(t):
    return -(-t.size // PACK_TILE) * 8


def _pack(vals):
    rows = []
    for t in vals:
        flat = t.reshape(-1)
        n_rows = _pack_rows(t)
        rows.append(jnp.pad(flat, (0, n_rows * PACK_LANES - flat.size)).reshape(n_rows, PACK_LANES))
    used = sum(r.shape[0] for r in rows)
    assert used <= PACK_ROWS, used
    return jnp.concatenate(rows + [jnp.zeros((PACK_ROWS - used, PACK_LANES), F32)], axis=0)


def _unpack(packed, like):
    out, off = [], 0
    for t in like:
        n_rows = _pack_rows(t)
        out.append(packed[off:off + n_rows].reshape(-1)[:t.size].reshape(t.shape))
        off += n_rows
    return out


def kernel(x, w_in, s5_lambda_re, s5_lambda_im, s5_log_step, s5_b_re, s5_b_im, s5_c_re, s5_c_im, s5_d, s5_glu_w, s5_glu_b, sgu_norm_g, sgu_norm_b, sgu_w, sgu_b, pool_w, pool_scale, dn_conv_w, dn_a_log, dn_dt_bias, dn_norm_g, w_out, ln1_g, ln1_b, w_up, w_down, ln2_g, ln2_b, loss_target, m_w_in, m_s5_lambda_re, m_s5_lambda_im, m_s5_log_step, m_s5_b_re, m_s5_b_im, m_s5_c_re, m_s5_c_im, m_s5_d, m_s5_glu_w, m_s5_glu_b, m_sgu_norm_g, m_sgu_norm_b, m_sgu_w, m_sgu_b, m_pool_w, m_pool_scale, m_dn_conv_w, m_dn_a_log, m_dn_dt_bias, m_dn_norm_g, m_w_out, m_ln1_g, m_ln1_b, m_w_up, m_w_down, m_ln2_g, m_ln2_b, v_w_in, v_s5_lambda_re, v_s5_lambda_im, v_s5_log_step, v_s5_b_re, v_s5_b_im, v_s5_c_re, v_s5_c_im, v_s5_d, v_s5_glu_w, v_s5_glu_b, v_sgu_norm_g, v_sgu_norm_b, v_sgu_w, v_sgu_b, v_pool_w, v_pool_scale, v_dn_conv_w, v_dn_a_log, v_dn_dt_bias, v_dn_norm_g, v_w_out, v_ln1_g, v_ln1_b, v_w_up, v_w_down, v_ln2_g, v_ln2_b):
    names = ("w_in", "s5_lambda_re", "s5_lambda_im", "s5_log_step", "s5_b_re", "s5_b_im", "s5_c_re", "s5_c_im", "s5_d", "s5_glu_w",
             "s5_glu_b", "sgu_norm_g", "sgu_norm_b", "sgu_w", "sgu_b", "pool_w", "pool_scale", "dn_conv_w", "dn_a_log", "dn_dt_bias",
             "dn_norm_g", "w_out", "ln1_g", "ln1_b", "w_up", "w_down", "ln2_g", "ln2_b")
    env = locals()
    w = {n: env[n] for n in names}
    m = {n: env["m_" + n] for n in names}
    v = {n: env["v_" + n] for n in names}

    wire = [{n: (w[n][i] if n == "dn_conv_w" else w[n][i].astype(MXU_DTYPE)) for n in SHARDED} for i in range(DEPTH)]
    small = [{n: w[n][i] for n in SMALL} for i in range(DEPTH)]
    ops = [_prep_small(small[i]) for i in range(DEPTH)]
    grads = [{} for _ in range(DEPTH)]
    recv = [{} for _ in range(DEPTH)]
    first = ("w_in", "s5_glu_w", "dn_conv_w", "w_out")

    def gather(layer, group):
        def take(outs):
            for n, t in zip(group, outs):
                ops[layer].update(_weight_views(n, t))
        return (lambda: [(wire[layer][n], False) for n in group]), take

    def scatter(layer, group, with_small=False):
        def make():
            sends = [(grads[layer][n], True) for n in group]
            if with_small:
                sends.append((_pack([jnp.stack([grads[i]["small"][n] for i in range(DEPTH)]) for n in SMALL]), False))
            return sends
        def take(outs):
            recv[layer].update(dict(zip(group + (("small",) if with_small else ()), outs)))
        return make, take

    make, take = gather(0, first)
    take(_exchange(make(), "gather_first"))
    fwd_hosts = {"l0_proj": gather(0, ("w_up",)), "l0_up": gather(0, ("w_down",)), "l0_down": gather(1, first),
                 "l1_proj": gather(1, ("w_up",)), "l1_up": gather(1, ("w_down",))}
    late = ("w_in", "s5_glu_w", "dn_conv_w")
    bwd_hosts = {"l1b_dpre": scatter(1, ("w_down",)), "l1b_dx1": scatter(1, ("w_up",)), "l1b_dmixed": scatter(1, ("w_out",)),
                 "l0b_dw_down": scatter(1, late),
                 "l0b_dpre": scatter(0, ("w_down",)), "l0b_dx1": scatter(0, ("w_up",)), "l0b_dmixed": scatter(0, ("w_out",)),
                 "l0b_dw_main": scatter(0, ("s5_glu_w", "dn_conv_w"), with_small=True), "l0b_dx": scatter(0, ("w_in",))}
    loss, grad_x = _local_step(x[0], loss_target[0], ops, small, fwd_hosts, bwd_hosts, grads)

    g_out, d_out, m_out, v_out = {}, {}, {}, {}
    for n in SHARDED:
        shp = w[n].shape
        pad = (-shp[1]) % 8
        def rows(t):
            return jnp.pad(t, ((0, 0), (0, pad), (0, 0))).reshape(shp[0] * (shp[1] + pad), shp[2])
        res = _adamw(rows(w[n]), [recv[i][n] for i in range(DEPTH)], rows(m[n]), rows(v[n]), "adamw_" + n)
        g_out[n], d_out[n], m_out[n], v_out[n] = (t.reshape(shp[0], shp[1] + pad, shp[2])[:, :shp[1]] for t in res)
    like = [w[n] for n in SMALL]
    res = _adamw(_pack(like), [recv[0]["small"]], _pack([m[n] for n in SMALL]), _pack([v[n] for n in SMALL]), "adamw_small")
    for dst, pk in zip((g_out, d_out, m_out, v_out), res):
        dst.update(dict(zip(SMALL, _unpack(pk, like))))

    total = lax.psum(loss[0, 0], MESH_AXES)
    return (total, grad_x[None], *[g_out[n] for n in names], *[d_out[n] for n in names],
            *[m_out[n] for n in names], *[v_out[n] for n in names])
```

```python
import functools
import math

import jax
import jax.numpy as jnp
from jax import lax
from jax.experimental import pallas as pl
from jax.experimental.pallas import tpu as pltpu

F32 = jnp.float32
BF16 = jnp.bfloat16
MXU_DTYPE = jnp.bfloat16
HI = lax.Precision.HIGHEST

N_DEV = 8
D_MODEL = 2048
DEPTH = 2
GROUP_WIDTH = 512
S5_GROUPS, S5_CH, S5_STATE = 32, 16, 64
S5_NS = S5_GROUPS * S5_STATE
SGU_CHUNK, SGU_HEADS = 128, 8
POOL_WINDOWS = (2, 4, 8, 16)
DN_HEADS, DN_HEAD_DIM, DN_CONV, DN_CHUNK = 4, 128, 4, 64
D_FF = 4 * D_MODEL
LN_EPS, RMS_EPS, L2_EPS = 1e-5, 1e-6, 1e-6
ALPHA = (2 * DEPTH) ** 0.25
MAIN_COLS = 4096
AB_PAD = 128
ADAM_LR, ADAM_B1, ADAM_B2, ADAM_EPS, ADAM_WD, ADAM_STEP = 0.001, 0.9, 0.999, 1e-08, 0.01, 10
VMEM_LIMIT = 56 * 1024 * 1024
MIX_ROWS = 1024
SGU_ROWS = 512
WIDE_ROWS = 512
C_GELU = math.sqrt(2.0 / math.pi)


def _params(sem=None):
    return pltpu.CompilerParams(dimension_semantics=sem, vmem_limit_bytes=VMEM_LIMIT)


def _gelu(x):
    return 0.5 * x * (1.0 + jnp.tanh(C_GELU * (x + 0.044715 * x * x * x)))


def _gelu_grad(x):
    t = jnp.tanh(C_GELU * (x + 0.044715 * x * x * x))
    return 0.5 * (1.0 + t) + 0.5 * x * (1.0 - t * t) * C_GELU * (1.0 + 3.0 * 0.044715 * x * x)


def _sigmoid(x):
    return 1.0 / (1.0 + jnp.exp(-x))


def _silu(x):
    return x * _sigmoid(x)


def _silu_grad(x):
    s = _sigmoid(x)
    return s * (1.0 + x * (1.0 - s))


def _softplus(x):
    z = jnp.exp(-jnp.abs(x))
    small = z * (1.0 - z * (0.5 - z * (1.0 / 3.0)))
    return jnp.maximum(x, 0.0) + jnp.where(z < 1e-2, small, jnp.log(1.0 + z))


def _mx(x):
    return x.astype(MXU_DTYPE)


def _dot(a, b, dims="nn", precision=None):
    cd = {"nn": ((1,), (0,)), "nt": ((1,), (1,)), "tn": ((0,), (0,))}[dims]
    return lax.dot_general(a, b, (cd, ((), ())), preferred_element_type=F32, precision=precision)


def _mdot(a, b, dims="nn"):
    return _dot(_mx(a), _mx(b), dims)


MESH_AXES = ("x", "y", "c")
OFFSETS = [(dx, dy, dc) for dx in (0, 1) for dy in (0, 1) for dc in (0, 1)][1:]


def _me_and_peers():
    x, y, c = (lax.axis_index(a) for a in MESH_AXES)
    def flip(v, d):
        return 1 - v if d else v
    peers = [(flip(x, dx), flip(y, dy), flip(c, dc)) for dx, dy, dc in OFFSETS]
    def idx(p):
        return 4 * p[0] + 2 * p[1] + p[2]
    return idx((x, y, c)), peers, [idx(p) for p in peers]


SIBLING = OFFSETS.index((0, 0, 1))
SAME_CORE = [OFFSETS.index(f) for f in ((0, 1, 0), (1, 0, 0), (1, 1, 0))]


class _Comm:
    def __init__(self, ops):
        self.arrays = [a for a, _ in ops]
        self.scatter = [s for _, s in ops]
        self.n = n = len(ops)
        hbm = pl.BlockSpec(memory_space=pltpu.HBM)
        self.in_specs, self.out_specs = [hbm] * n, [hbm] * n
        self.out_shape = [jax.ShapeDtypeStruct(a.shape if s else (N_DEV,) + a.shape, a.dtype) for a, s in ops]
        npeer = len(OFFSETS)
        self.scratch = [pltpu.SemaphoreType.DMA((n, npeer)), pltpu.SemaphoreType.DMA((n, npeer)), pltpu.SemaphoreType.DMA((n,))]

    def _plan(self, ins, outs, sems, waiting):
        send_sems, recv_sems, local_sems = sems
        me, peers, peer_idx = _me_and_peers()

        def remote(k, d, src, dst, to):
            return pltpu.make_async_remote_copy(src_ref=src, dst_ref=dst, send_sem=send_sems.at[k, d], recv_sem=recv_sems.at[k, d],
                                                device_id=to, device_id_type=pl.DeviceIdType.MESH)
        plan = []
        for k in range(self.n):
            every = range(len(OFFSETS))
            if self.scatter[k]:
                local = pltpu.make_async_copy(ins[k].at[me], outs[k].at[me], local_sems.at[k])
                pushes = [remote(k, d, ins[k].at[peer_idx[d]], outs[k].at[me], peers[d]) for d in every]
                onward = []
            else:
                local = pltpu.make_async_copy(ins[k], outs[k].at[me], local_sems.at[k])
                pushes = [remote(k, d, ins[k], outs[k].at[me], peers[d]) for d in [SIBLING] + SAME_CORE]
                onward = SAME_CORE
            passed, arrivals = [], {}
            if waiting:
                passed = [(d, remote(k, d + 1, outs[k].at[peer_idx[d]], outs[k].at[peer_idx[d]], peers[SIBLING])) for d in onward]
                arrivals = {d: remote(k, d, outs[k].at[peer_idx[d]], outs[k].at[peer_idx[d]], peers[d]) for d in every}
            plan.append((local, pushes, passed, arrivals))
        return plan

    def start(self, ins, outs, sems):
        for local, pushes, _, _ in self._plan(ins, outs, sems, False):
            local.start()
            for cp in pushes:
                cp.start()

    def wait(self, ins, outs, sems):
        plan = self._plan(ins, outs, sems, True)
        for _, _, passed, arrivals in plan:
            for d, onward in passed:
                arrivals.pop(d).wait_recv()
                onward.start()
        for local, pushes, passed, arrivals in plan:
            for cp in arrivals.values():
                cp.wait_recv()
            for cp in pushes + [onward for _, onward in passed]:
                cp.wait_send()
            local.wait()


def _exchange(ops, name):
    cm = _Comm(ops)

    def body(*refs):
        ins, outs, sems = refs[:cm.n], refs[cm.n:2 * cm.n], refs[2 * cm.n:]
        cm.start(ins, outs, sems)
        cm.wait(ins, outs, sems)

    return pl.pallas_call(body, name=name, in_specs=cm.in_specs, out_specs=cm.out_specs, out_shape=cm.out_shape,
                          scratch_shapes=cm.scratch)(*cm.arrays)


def _matmul(a, b, *, mode, tm, tn, tk, out_dtype, name, a_fn=None, extras=(), epi=None, a_cols=None,
            b_slab=None, out_slab=None, comm=None):
    a_shape = a.shape if a_cols is None else (a.shape[0], a_cols)
    b_shape = b.shape if b_slab is None else (b.shape[1], N_DEV * b_slab)
    if mode == "nn":
        (m, k), n = a_shape, b_shape[1]
    elif mode == "nt":
        (m, k), n = a_shape, b_shape[0]
    else:
        (k, m), n = a_shape, b_shape[1]
    tm, tn, tk = min(tm, m), min(tn, n), min(tk, k)
    if b_slab is not None:
        tn, tk = (tn, min(tk, b_slab)) if mode == "nt" else (min(tn, b_slab), tk)
    assert m % tm == 0 and n % tn == 0 and k % tk == 0, (name, a.shape, b.shape, tm, tn, tk)
    gi, gj, nk = m // tm, n // tn, k // tk
    n_ex = len(extras)
    cm = _Comm(comm) if comm else None
    nc = cm.n if cm else 0

    def body(a_ref, b_ref, *rest):
        ex_refs, rest = rest[:n_ex], rest[n_ex:]
        c_ins, o_ref, c_outs, acc, sems = rest[:nc], rest[nc], rest[nc + 1:2 * nc + 1], rest[2 * nc + 1], rest[2 * nc + 2:]
        i, j, kk = pl.program_id(0), pl.program_id(1), pl.program_id(2)
        if cm:
            @pl.when((i == 0) & (j == 0) & (kk == 0))
            def _():
                cm.start(c_ins, c_outs, sems)

        av = a_ref[...]
        if a_fn is not None:
            av = a_fn(av)
        part = _dot(_mx(av), _mx(b_ref[...]), mode)

        def finish(r):
            if epi is not None:
                r = epi(r, *[e[...] for e in ex_refs])
            o_ref[...] = r.astype(out_dtype)

        if nk == 1:
            finish(part)
        else:
            @pl.when(kk == 0)
            def _():
                acc[...] = part

            @pl.when((kk > 0) & (kk < nk - 1))
            def _():
                acc[...] += part

            @pl.when(kk == nk - 1)
            def _():
                finish(acc[...] + part)

        if cm:
            @pl.when((i == gi - 1) & (j == gj - 1) & (kk == nk - 1))
            def _():
                cm.wait(c_ins, c_outs, sems)

    a_spec = pl.BlockSpec((tk, tm), lambda i, j, kk: (kk, i)) if mode == "tn" else pl.BlockSpec((tm, tk), lambda i, j, kk: (i, kk))
    if b_slab is None:
        b_spec = pl.BlockSpec((tn, tk), lambda i, j, kk: (j, kk)) if mode == "nt" else pl.BlockSpec((tk, tn), lambda i, j, kk: (kk, j))
    elif mode == "nt":
        assert b_slab % tk == 0
        b_spec = pl.BlockSpec((None, tn, tk), lambda i, j, kk: ((kk * tk) // b_slab, j, ((kk * tk) % b_slab) // tk))
    else:
        assert b_slab % tn == 0
        b_spec = pl.BlockSpec((None, tk, tn), lambda i, j, kk: ((j * tn) // b_slab, kk, ((j * tn) % b_slab) // tn))
    if out_slab is None:
        o_spec, o_shape = pl.BlockSpec((tm, tn), lambda i, j, kk: (i, j)), jax.ShapeDtypeStruct((m, n), out_dtype)
    else:
        assert out_slab % tn == 0 and n == N_DEV * out_slab
        o_spec = pl.BlockSpec((None, tm, tn), lambda i, j, kk: ((j * tn) // out_slab, i, ((j * tn) % out_slab) // tn))
        o_shape = jax.ShapeDtypeStruct((N_DEV, m, out_slab), out_dtype)
    ex_specs = [pl.BlockSpec(({None: tm, "tn": tn}.get(bs[0], bs[0]), tn if bs[1] is None else bs[1]),
                             functools.partial(lambda i, j, kk, f: f(i, j), f=im)) for (_, bs, im) in extras]
    res = pl.pallas_call(
        body,
        name=name,
        grid=(gi, gj, nk),
        in_specs=[a_spec, b_spec, *ex_specs] + (cm.in_specs if cm else []),
        out_specs=[o_spec] + (cm.out_specs if cm else []),
        out_shape=[o_shape] + (cm.out_shape if cm else []),
        scratch_shapes=[pltpu.VMEM((tm, tn) if nk > 1 else (8, 128), F32)] + (cm.scratch if cm else []),
        compiler_params=_params(("arbitrary",) * 3 if cm else ("parallel", "parallel", "arbitrary")),
    )(a, b, *[e[0] for e in extras], *(cm.arrays if cm else []))
    return (res[0], res[1:]) if cm else res[0]


def _ln_fwd(x, y, g, b, name):
    l, d = x.shape
    tl = min(WIDE_ROWS, l)

    def body(x_ref, y_ref, g_ref, b_ref, h_ref, o_ref, om_ref):
        h = ALPHA * x_ref[...] + y_ref[...]
        mu = jnp.mean(h, axis=-1, keepdims=True)
        c = h - mu
        var = jnp.mean(c * c, axis=-1, keepdims=True)
        h_ref[...] = h
        out = c * lax.rsqrt(var + LN_EPS) * g_ref[...] + b_ref[...]
        o_ref[...] = out
        om_ref[...] = out.astype(om_ref.dtype)

    row = pl.BlockSpec((tl, d), lambda i: (i, 0))
    vec = pl.BlockSpec((1, d), lambda i: (0, 0))
    return pl.pallas_call(
        body, name=name, grid=(l // tl,), in_specs=[row, row, vec, vec], out_specs=[row, row, row],
        out_shape=[jax.ShapeDtypeStruct((l, d), F32)] * 2 + [jax.ShapeDtypeStruct((l, d), MXU_DTYPE)],
        compiler_params=_params(("parallel",)),
    )(x, y, g, b)


def _ln_bwd(dout, h, g, name):
    l, d = h.shape
    tl = min(WIDE_ROWS, l)

    def body(do_ref, h_ref, g_ref, dh_ref, dhm_ref, dg_ref, db_ref):
        @pl.when(pl.program_id(0) == 0)
        def _():
            dg_ref[...] = jnp.zeros_like(dg_ref)
            db_ref[...] = jnp.zeros_like(db_ref)

        hv, do = h_ref[...], do_ref[...]
        mu = jnp.mean(hv, axis=-1, keepdims=True)
        c = hv - mu
        r = lax.rsqrt(jnp.mean(c * c, axis=-1, keepdims=True) + LN_EPS)
        xh = c * r
        dxh = do * g_ref[...]
        m1 = jnp.mean(dxh, axis=-1, keepdims=True)
        m2 = jnp.mean(dxh * xh, axis=-1, keepdims=True)
        dh = r * (dxh - m1 - xh * m2)
        dh_ref[...] = dh
        dhm_ref[...] = dh.astype(dhm_ref.dtype)
        dg_ref[...] += jnp.sum(do * xh, axis=0, keepdims=True)
        db_ref[...] += jnp.sum(do, axis=0, keepdims=True)

    row = pl.BlockSpec((tl, d), lambda i: (i, 0))
    vec = pl.BlockSpec((1, d), lambda i: (0, 0))
    return pl.pallas_call(
        body, name=name, grid=(l // tl,), in_specs=[row, row, vec], out_specs=[row, row, vec, vec],
        out_shape=[jax.ShapeDtypeStruct((l, d), F32), jax.ShapeDtypeStruct((l, d), MXU_DTYPE),
                   jax.ShapeDtypeStruct((1, d), F32), jax.ShapeDtypeStruct((1, d), F32)],
        compiler_params=_params(("arbitrary",)),
    )(dout, h, g)


def _loss_head(y, target):
    l, d = y.shape
    tl = min(WIDE_ROWS, l)

    def body(y_ref, t_ref, loss_ref, dy_ref):
        @pl.when(pl.program_id(0) == 0)
        def _():
            loss_ref[...] = jnp.zeros_like(loss_ref)

        e = y_ref[...] - t_ref[...]
        dy_ref[...] = e * (1.0 / d)
        s = jnp.sum(jnp.sum(e * e, axis=1, keepdims=True), axis=0, keepdims=True)
        loss_ref[...] += s * (0.5 / d)

    row = pl.BlockSpec((tl, d), lambda i: (i, 0))
    return pl.pallas_call(
        body, name="loss_head", grid=(l // tl,), in_specs=[row, row],
        out_specs=[pl.BlockSpec((1, 1), lambda i: (0, 0)), row],
        out_shape=[jax.ShapeDtypeStruct((1, 1), F32), jax.ShapeDtypeStruct((l, d), F32)],
        compiler_params=_params(("arbitrary",)),
    )(y, target)


def _s5_discretize(lam_re, lam_im, log_step, b_re, b_im):
    step = jnp.exp(log_step)[:, None]
    e = jnp.exp(lam_re * step)
    lbr, lbi = e * jnp.cos(lam_im * step), e * jnp.sin(lam_im * step)
    den = lam_re * lam_re + lam_im * lam_im
    qr = ((lbr - 1.0) * lam_re + lbi * lam_im) / den
    qi = (lbi * lam_re - (lbr - 1.0) * lam_im) / den
    bbr = qr[:, :, None] * b_re - qi[:, :, None] * b_im
    bbi = qr[:, :, None] * b_im + qi[:, :, None] * b_re
    return lbr, lbi, bbr, bbi


S5_TILES, S5_SLABS = 4, 8
S5_TILE_W, S5_SLAB_W = GROUP_WIDTH // S5_TILES, S5_NS // S5_TILES
S5_GPT = S5_GROUPS // S5_TILES


def _s5_compact(bbr, bbi, c_re, c_im):
    eye = jnp.eye(S5_GPT, dtype=F32)
    def bd(t):
        return jnp.einsum("tgph,gk->tghkp", t.reshape(S5_TILES, S5_GPT, S5_STATE, S5_CH), eye).reshape(S5_TILES, S5_TILE_W, S5_SLAB_W)
    def cd(t):
        return jnp.einsum("tghp,gk->tgpkh", t.reshape(S5_TILES, S5_GPT, S5_CH, S5_STATE), eye).reshape(S5_TILES, S5_SLAB_W, S5_TILE_W)
    return jnp.concatenate([bd(bbr), bd(bbi)], axis=0), jnp.concatenate([cd(c_re), -cd(c_im)], axis=0)


def _s5_uncompact_b(db):
    eye = jnp.eye(S5_GPT, dtype=F32)[None, :, None, :, None]
    def ex(t):
        d = jnp.sum(t.reshape(S5_TILES, S5_GPT, S5_CH, S5_GPT, S5_STATE) * eye, axis=3)
        return jnp.transpose(d, (0, 1, 3, 2)).reshape(S5_GROUPS, S5_STATE, S5_CH)
    return ex(db[:S5_TILES]), ex(db[S5_TILES:])


def _s5_uncompact_c(dc):
    eye = jnp.eye(S5_GPT, dtype=F32)[None, :, None, :, None]
    def ex(t):
        d = jnp.sum(t.reshape(S5_TILES, S5_GPT, S5_STATE, S5_GPT, S5_CH) * eye, axis=3)
        return jnp.transpose(d, (0, 1, 3, 2)).reshape(S5_GROUPS, S5_CH, S5_STATE)
    return ex(dc[:S5_TILES]), -ex(dc[S5_TILES:])


S5_ROWS = 512


def _s5_tile(j):
    t = j % S5_TILES
    return slice(t * S5_TILE_W, (t + 1) * S5_TILE_W)


def _s5_slab(j):
    return slice(j * S5_SLAB_W, (j + 1) * S5_SLAB_W)


def _s5_recur(src, lam_ref, carry, emit, n_rows, reverse, extra=()):
    ns = S5_NS
    lr, li = lam_ref[:, :ns], lam_ref[:, ns:]

    def step(t, c):
        row = (n_rows - 1 - t) if reverse else t
        cr, ci = c[0], c[1]
        nr = lr * cr - li * ci + src[pl.ds(row, 1), :ns]
        ni = lr * ci + li * cr + src[pl.ds(row, 1), ns:]
        return (nr, ni) + tuple(emit(row, nr, ni, cr, ci, c[2:]))

    fin = lax.fori_loop(0, n_rows, step, (carry[:, :ns], carry[:, ns:]) + tuple(extra))
    carry[:, :ns] = fin[0]
    carry[:, ns:] = fin[1]
    return fin[2:]


def _s5_fwd(proj, b, c, lam, d, name):
    l = proj.shape[0]
    tl = min(S5_ROWS, l)
    w = 2 * S5_NS

    def body(u_ref, b_ref, c_ref, lam_ref, d_ref, hs_ref, y_ref, bu, carry):
        @pl.when(pl.program_id(0) == 0)
        def _():
            carry[...] = jnp.zeros_like(carry)

        u = u_ref[...]
        um = _mx(u)
        for j in range(S5_SLABS):
            bu[:, _s5_slab(j)] = _dot(um[:, _s5_tile(j)], b_ref[j])

        def emit(row, nr, ni, cr, ci, extra):
            hs_ref[pl.ds(row, 1), :S5_NS] = nr
            hs_ref[pl.ds(row, 1), S5_NS:] = ni
            return extra

        _s5_recur(bu, lam_ref, carry, emit, tl, False)
        for t in range(S5_TILES):
            acc = _dot(_mx(hs_ref[:, _s5_slab(t)]), c_ref[t]) + _dot(_mx(hs_ref[:, _s5_slab(S5_TILES + t)]), c_ref[S5_TILES + t])
            y_ref[:, _s5_tile(t)] = acc + d_ref[:, _s5_tile(t)] * u[:, _s5_tile(t)]

    row = lambda width: pl.BlockSpec((tl, width), lambda i: (i, 0))
    full = lambda a: pl.BlockSpec(a.shape, lambda i: (0,) * a.ndim)
    return pl.pallas_call(
        body, name=name, grid=(l // tl,), in_specs=[row(GROUP_WIDTH), full(b), full(c), full(lam), full(d)],
        out_specs=[row(w), row(GROUP_WIDTH)],
        out_shape=[jax.ShapeDtypeStruct((l, w), F32), jax.ShapeDtypeStruct((l, GROUP_WIDTH), F32)],
        scratch_shapes=[pltpu.VMEM((tl, w), F32), pltpu.VMEM((1, w), F32)], compiler_params=_params(("arbitrary",)),
    )(proj, b, c, lam, d)


def _s5_bwd(dy, hs, proj, b, c, lam_conj, d, name):
    l = dy.shape[0]
    tl = min(S5_ROWS, l)
    nb = l // tl
    w = 2 * S5_NS

    def body(dy_ref, hs_ref, u_ref, b_ref, c_ref, lam_ref, d_ref, du_ref, db_ref, dc_ref, dl_ref, dh, adj, carry):
        @pl.when(pl.program_id(0) == 0)
        def _():
            carry[...] = jnp.zeros_like(carry)
            db_ref[...] = jnp.zeros_like(db_ref)
            dc_ref[...] = jnp.zeros_like(dc_ref)
            dl_ref[...] = jnp.zeros_like(dl_ref)

        dyv = dy_ref[...]
        dym, um = _mx(dyv), _mx(u_ref[...])
        for j in range(S5_SLABS):
            dh[:, _s5_slab(j)] = _dot(dym[:, _s5_tile(j)], c_ref[j], "nt")

        def emit(row, nr, ni, cr, ci, extra):
            adj[pl.ds(row, 1), :S5_NS] = nr
            adj[pl.ds(row, 1), S5_NS:] = ni
            hr, hi = hs_ref[pl.ds(row, 1), :S5_NS], hs_ref[pl.ds(row, 1), S5_NS:]
            return extra[0] + cr * hr + ci * hi, extra[1] + ci * hr - cr * hi

        dl = _s5_recur(dh, lam_ref, carry, emit, tl, True, extra=(dl_ref[:, :S5_NS], dl_ref[:, S5_NS:]))
        dl_ref[:, :S5_NS] = dl[0]
        dl_ref[:, S5_NS:] = dl[1]
        for t in range(S5_TILES):
            acc = (_dot(_mx(adj[:, _s5_slab(t)]), b_ref[t], "nt")
                   + _dot(_mx(adj[:, _s5_slab(S5_TILES + t)]), b_ref[S5_TILES + t], "nt"))
            du_ref[:, _s5_tile(t)] = (acc + d_ref[:, _s5_tile(t)] * dyv[:, _s5_tile(t)]).astype(du_ref.dtype)
        for j in range(S5_SLABS):
            dc_ref[j] += _dot(_mx(hs_ref[:, _s5_slab(j)]), dym[:, _s5_tile(j)], "tn")
            db_ref[j] += _dot(um[:, _s5_tile(j)], _mx(adj[:, _s5_slab(j)]), "tn")

    row = lambda width: pl.BlockSpec((tl, width), lambda i: (nb - 1 - i, 0))
    full = lambda a: pl.BlockSpec(a.shape, lambda i: (0,) * a.ndim)
    acc3 = lambda shape: pl.BlockSpec(shape, lambda i: (0, 0, 0))
    return pl.pallas_call(
        body, name=name, grid=(nb,),
        in_specs=[row(GROUP_WIDTH), row(w), row(GROUP_WIDTH), full(b), full(c), full(lam_conj), full(d)],
        out_specs=[row(GROUP_WIDTH), acc3(b.shape), acc3(c.shape), pl.BlockSpec((1, w), lambda i: (0, 0))],
        out_shape=[jax.ShapeDtypeStruct((l, GROUP_WIDTH), BF16), jax.ShapeDtypeStruct(b.shape, F32),
                   jax.ShapeDtypeStruct(c.shape, F32), jax.ShapeDtypeStruct((1, w), F32)],
        scratch_shapes=[pltpu.VMEM((tl, w), F32), pltpu.VMEM((tl, w), F32), pltpu.VMEM((1, w), F32)],
        compiler_params=_params(("arbitrary",)),
    )(dy, hs, proj, b, c, lam_conj, d)


def _s5_glu_fwd(y, glu_w, glu_b, name):
    l, d = y.shape
    tl = min(MIX_ROWS, l)

    def body(y_ref, w_ref, b_ref, o_ref):
        yg = _gelu(y_ref[...])
        z = _mdot(yg, w_ref[...]) + b_ref[...]
        o_ref[...] = (yg * _sigmoid(z)).astype(o_ref.dtype)

    return pl.pallas_call(
        body, name=name, grid=(l // tl,),
        in_specs=[pl.BlockSpec((tl, d), lambda i: (i, 0)), pl.BlockSpec((d, d), lambda i: (0, 0)), pl.BlockSpec((1, d), lambda i: (0, 0))],
        out_specs=pl.BlockSpec((tl, d), lambda i: (i, 0)), out_shape=jax.ShapeDtypeStruct((l, d), BF16),
        compiler_params=_params(("parallel",)),
    )(y, glu_w, glu_b)


def _s5_glu_bwd(dmixed, y, proj, glu_w, glu_b, name):
    l, d = y.shape
    tl = min(MIX_ROWS, l)

    def body(do_ref, y_ref, u_ref, w_ref, b_ref, dy_ref, dz_ref, yg_ref, db_ref, dd_ref):
        @pl.when(pl.program_id(0) == 0)
        def _():
            db_ref[...] = jnp.zeros_like(db_ref)
            dd_ref[...] = jnp.zeros_like(dd_ref)

        yv, do = y_ref[...], do_ref[...]
        yg = _gelu(yv)
        gate = _sigmoid(_mdot(yg, w_ref[...]) + b_ref[...])
        dz = do * yg * gate * (1.0 - gate)
        dyg = do * gate + _mdot(dz, w_ref[...], "nt")
        dy = dyg * _gelu_grad(yv)
        dy_ref[...] = dy
        dz_ref[...] = dz.astype(dz_ref.dtype)
        yg_ref[...] = yg.astype(yg_ref.dtype)
        db_ref[...] += jnp.sum(dz, axis=0, keepdims=True)
        dd_ref[...] += jnp.sum(dy * u_ref[...], axis=0, keepdims=True)

    row = pl.BlockSpec((tl, d), lambda i: (i, 0))
    vec = pl.BlockSpec((1, d), lambda i: (0, 0))
    return pl.pallas_call(
        body, name=name, grid=(l // tl,),
        in_specs=[row, row, row, pl.BlockSpec((d, d), lambda i: (0, 0)), vec],
        out_specs=[row, row, row, vec, vec],
        out_shape=[jax.ShapeDtypeStruct((l, d), F32), jax.ShapeDtypeStruct((l, d), BF16), jax.ShapeDtypeStruct((l, d), BF16),
                   jax.ShapeDtypeStruct((1, d), F32), jax.ShapeDtypeStruct((1, d), F32)],
        compiler_params=_params(("arbitrary",)),
    )(dmixed, y, proj, glu_w, glu_b)


def _sgu_pair(w_ref, x, j, dims):
    lo = lax.broadcasted_iota(jnp.int32, x.shape, 1) < (GROUP_WIDTH // SGU_HEADS)
    xb = _mx(x)
    r0 = _dot(w_ref[2 * j], xb, dims)
    r1 = _dot(w_ref[2 * j + 1], xb, dims)
    return jnp.where(lo, r0, r1)


def _sgu_norm(v, g, b):
    mu = jnp.mean(v, axis=-1, keepdims=True)
    c = v - mu
    r = lax.rsqrt(jnp.mean(c * c, axis=-1, keepdims=True) + LN_EPS)
    return c * r, r


def _sgu_fwd(proj, norm_g, norm_b, wm, bfull, name):
    l = proj.shape[0]
    tl = min(SGU_ROWS, l)
    gw = GROUP_WIDTH

    def body(zu_ref, zv_ref, g_ref, b_ref, w_ref, bf_ref, o_ref):
        for c in range(tl // SGU_CHUNK):
            rows = slice(c * SGU_CHUNK, (c + 1) * SGU_CHUNK)
            u = _gelu(zu_ref[rows, :])
            vh, _ = _sgu_norm(_gelu(zv_ref[rows, :]), None, None)
            vn = vh * g_ref[...] + b_ref[...]
            for j in range(gw // 128):
                cols = slice(j * 128, (j + 1) * 128)
                mixed = _sgu_pair(w_ref, vn[:, cols], j, "nn") + bf_ref[:, cols]
                o_ref[rows, cols] = (u[:, cols] * mixed).astype(o_ref.dtype)

    vec = pl.BlockSpec((1, gw), lambda i: (0, 0))
    return pl.pallas_call(
        body, name=name, grid=(l // tl,),
        in_specs=[pl.BlockSpec((tl, gw), lambda i: (i, 1)), pl.BlockSpec((tl, gw), lambda i: (i, 2)), vec, vec,
                  pl.BlockSpec((SGU_HEADS, SGU_CHUNK, SGU_CHUNK), lambda i: (0, 0, 0)), pl.BlockSpec((SGU_CHUNK, gw), lambda i: (0, 0))],
        out_specs=pl.BlockSpec((tl, gw), lambda i: (i, 0)), out_shape=jax.ShapeDtypeStruct((l, gw), BF16),
        compiler_params=_params(("parallel",)),
    )(proj, proj, norm_g, norm_b, wm, bfull)


def _sgu_bwd(dmixed, proj, norm_g, norm_b, wm, bfull, name):
    l = proj.shape[0]
    tl = min(SGU_ROWS, l)
    gw = GROUP_WIDTH
    hd = gw // SGU_HEADS

    def body(do_ref, zu_ref, zv_ref, g_ref, b_ref, w_ref, bf_ref, dzu_ref, dzv_ref, dw_ref, dbf_ref, dg_ref, dnb_ref):
        @pl.when(pl.program_id(0) == 0)
        def _():
            dw_ref[...] = jnp.zeros_like(dw_ref)
            dbf_ref[...] = jnp.zeros_like(dbf_ref)
            dg_ref[...] = jnp.zeros_like(dg_ref)
            dnb_ref[...] = jnp.zeros_like(dnb_ref)

        for c in range(tl // SGU_CHUNK):
            rows = slice(c * SGU_CHUNK, (c + 1) * SGU_CHUNK)
            zu, zv, do = zu_ref[rows, :], zv_ref[rows, :], do_ref[rows, :]
            u = _gelu(zu)
            vh, r = _sgu_norm(_gelu(zv), None, None)
            vn = vh * g_ref[...] + b_ref[...]
            dvn_parts, mixed_parts = [], []
            for j in range(gw // 128):
                cols = slice(j * 128, (j + 1) * 128)
                vb = vn[:, cols]
                mixed_parts.append(_sgu_pair(w_ref, vb, j, "nn") + bf_ref[:, cols])
                dm = do[:, cols] * u[:, cols]
                dvn_parts.append(_sgu_pair(w_ref, dm, j, "tn"))
                lo = lax.broadcasted_iota(jnp.int32, dm.shape, 1) < hd
                dw_ref[2 * j] += _mdot(jnp.where(lo, dm, 0.0), vb, "nt")
                dw_ref[2 * j + 1] += _mdot(jnp.where(lo, 0.0, dm), vb, "nt")
                dbf_ref[:, cols] += dm
            mixed = jnp.concatenate(mixed_parts, axis=1)
            dvn = jnp.concatenate(dvn_parts, axis=1)
            dzu_ref[rows, :] = (do * mixed * _gelu_grad(zu)).astype(dzu_ref.dtype)
            dg_ref[...] += jnp.sum(dvn * vh, axis=0, keepdims=True)
            dnb_ref[...] += jnp.sum(dvn, axis=0, keepdims=True)
            dvh = dvn * g_ref[...]
            m1 = jnp.mean(dvh, axis=-1, keepdims=True)
            m2 = jnp.mean(dvh * vh, axis=-1, keepdims=True)
            dv = r * (dvh - m1 - vh * m2)
            dzv_ref[rows, :] = (dv * _gelu_grad(zv)).astype(dzv_ref.dtype)

    vec = pl.BlockSpec((1, gw), lambda i: (0, 0))
    row = pl.BlockSpec((tl, gw), lambda i: (i, 0))
    wspec = pl.BlockSpec((SGU_HEADS, SGU_CHUNK, SGU_CHUNK), lambda i: (0, 0, 0))
    bspec = pl.BlockSpec((SGU_CHUNK, gw), lambda i: (0, 0))
    return pl.pallas_call(
        body, name=name, grid=(l // tl,),
        in_specs=[pl.BlockSpec((tl, gw), lambda i: (i, 1)), pl.BlockSpec((tl, gw), lambda i: (i, 1)), pl.BlockSpec((tl, gw), lambda i: (i, 2)),
                  vec, vec, wspec, bspec],
        out_specs=[row, row, wspec, bspec, vec, vec],
        out_shape=[jax.ShapeDtypeStruct((l, gw), BF16), jax.ShapeDtypeStruct((l, gw), BF16),
                   jax.ShapeDtypeStruct((SGU_HEADS, SGU_CHUNK, SGU_CHUNK), F32), jax.ShapeDtypeStruct((SGU_CHUNK, gw), F32),
                   jax.ShapeDtypeStruct((1, gw), F32), jax.ShapeDtypeStruct((1, gw), F32)],
        compiler_params=_params(("arbitrary",)),
    )(dmixed, proj, proj, norm_g, norm_b, wm, bfull)


HALO = 16


def _window_sums(ext, n_rows, forward):
    def sh(x, k):
        return pltpu.roll(x, (n_rows - k) if forward else k, axis=0)
    s2 = ext + sh(ext, 1)
    s4 = s2 + sh(s2, 2)
    s8 = s4 + sh(s4, 4)
    s16 = s8 + sh(s8, 8)
    return (s2, s4, s8, s16)


def _pool_fwd(proj, pool_w, scale, name):
    l = proj.shape[0]
    tl = min(MIX_ROWS, l)
    gw = GROUP_WIDTH
    pg = gw // len(POOL_WINDOWS)

    def body(x_ref, halo_ref, w_ref, s_ref, o_ref, p_ref):
        i = pl.program_id(0)
        x = x_ref[...]
        halo = jnp.where(i > 0, halo_ref[...], 0.0)
        ext = jnp.concatenate([halo, x], axis=0)
        sums = _window_sums(ext, tl + HALO, False)
        t = i * tl + lax.broadcasted_iota(jnp.int32, (tl, pg), 0)
        for gi, win in enumerate(POOL_WINDOWS):
            cols = slice(gi * pg, (gi + 1) * pg)
            cnt = jnp.minimum(t + 1, win).astype(F32)
            pooled = sums[gi][HALO:, cols] / cnt - x[:, cols]
            p_ref[:, cols] = pooled
            o_ref[:, cols] = (_mdot(pooled, w_ref[gi]) * s_ref[:, cols]).astype(o_ref.dtype)

    row = pl.BlockSpec((tl, gw), lambda i: (i, 0))
    return pl.pallas_call(
        body, name=name, grid=(l // tl,),
        in_specs=[pl.BlockSpec((tl, gw), lambda i: (i, 3)),
                  pl.BlockSpec((HALO, gw), lambda i: (jnp.maximum(i * (tl // HALO) - 1, 0), 3)),
                  pl.BlockSpec((len(POOL_WINDOWS), pg, pg), lambda i: (0, 0, 0)), pl.BlockSpec((1, gw), lambda i: (0, 0))],
        out_specs=[row, row], out_shape=[jax.ShapeDtypeStruct((l, gw), BF16), jax.ShapeDtypeStruct((l, gw), F32)],
        compiler_params=_params(("parallel",)),
    )(proj, proj, pool_w, scale)


def _pool_bwd_map(dmixed, pooled, pool_w, scale, name):
    l, gw = pooled.shape
    tl = min(MIX_ROWS, l)
    ng = len(POOL_WINDOWS)
    pg = gw // ng

    def body(do_ref, p_ref, w_ref, s_ref, dp_ref, dw_ref, ds_ref):
        @pl.when(pl.program_id(0) == 0)
        def _():
            dw_ref[...] = jnp.zeros_like(dw_ref)
            ds_ref[...] = jnp.zeros_like(ds_ref)

        for gi in range(ng):
            cols = slice(gi * pg, (gi + 1) * pg)
            do, pooled_g = do_ref[:, cols], p_ref[:, cols]
            mixed = _mdot(pooled_g, w_ref[gi])
            ds_ref[:, cols] += jnp.sum(do * mixed, axis=0, keepdims=True)
            dm = do * s_ref[:, cols]
            dw_ref[gi] += _mdot(pooled_g, dm, "tn")
            dp_ref[:, cols] = _mdot(dm, w_ref[gi], "nt")

    row = pl.BlockSpec((tl, gw), lambda i: (i, 0))
    wspec = pl.BlockSpec((ng, pg, pg), lambda i: (0, 0, 0))
    vec = pl.BlockSpec((1, gw), lambda i: (0, 0))
    return pl.pallas_call(
        body, name=name, grid=(l // tl,),
        in_specs=[pl.BlockSpec((tl, gw), lambda i: (i, 2)), row, wspec, vec], out_specs=[row, wspec, vec],
        out_shape=[jax.ShapeDtypeStruct((l, gw), F32), jax.ShapeDtypeStruct((ng, pg, pg), F32), jax.ShapeDtypeStruct((1, gw), F32)],
        compiler_params=_params(("arbitrary",)),
    )(dmixed, pooled, pool_w, scale)


def _pool_bwd_window(dpooled, name):
    l, gw = dpooled.shape
    tl = min(MIX_ROWS, l)
    nb = l // tl
    pg = gw // len(POOL_WINDOWS)

    def body(d_ref, halo_ref, o_ref):
        i = pl.program_id(0)
        d = d_ref[...]
        halo = jnp.where(i < nb - 1, halo_ref[...], 0.0)
        ext = jnp.concatenate([d, halo], axis=0)
        t = i * tl + lax.broadcasted_iota(jnp.int32, (tl + HALO, pg), 0)
        for gi, win in enumerate(POOL_WINDOWS):
            cols = slice(gi * pg, (gi + 1) * pg)
            cnt = jnp.minimum(t + 1, win).astype(F32)
            sums = _window_sums(ext[:, cols] / cnt, tl + HALO, True)
            o_ref[:, cols] = (sums[gi][:tl, :] - d[:, cols]).astype(o_ref.dtype)

    row = pl.BlockSpec((tl, gw), lambda i: (i, 0))
    return pl.pallas_call(
        body, name=name, grid=(nb,),
        in_specs=[row, pl.BlockSpec((HALO, gw), lambda i: (jnp.minimum((i + 1) * (tl // HALO), l // HALO - 1), 0))],
        out_specs=row, out_shape=jax.ShapeDtypeStruct((l, gw), BF16), compiler_params=_params(("parallel",)),
    )(dpooled, dpooled)


CONV_HALO = 8
QKV_BLK = 4


def _head_sums(x):
    parts = []
    for hd in range(DN_HEADS):
        s = jnp.sum(x[:, hd * DN_HEAD_DIM:(hd + 1) * DN_HEAD_DIM], axis=-1, keepdims=True)
        parts.append(jnp.broadcast_to(s, (x.shape[0], DN_HEAD_DIM)))
    return jnp.concatenate(parts, axis=1)


def _gdn_pre_fwd(proj, proj_ab, conv_w, a_log, dt_bias, name):
    l = proj.shape[0]
    tl = min(MIX_ROWS, l)
    gw = GROUP_WIDTH

    def body(xq, xk, xv, hq, hk, hv, w_ref, ab_ref, al_ref, dt_ref, qn_ref, kn_ref, v_ref, cq_ref, ck_ref, cv_ref, gb_ref):
        i = pl.program_id(0)
        for p, (x_ref, h_ref, c_ref) in enumerate(((xq, hq, cq_ref), (xk, hk, ck_ref), (xv, hv, cv_ref))):
            ext = jnp.concatenate([jnp.where(i > 0, h_ref[...], 0.0), x_ref[...]], axis=0)
            conv = jnp.zeros((tl, gw), F32)
            for j in range(DN_CONV):
                k = DN_CONV - 1 - j
                shifted = ext if k == 0 else pltpu.roll(ext, k, axis=0)
                conv = conv + shifted[CONV_HALO:, :] * w_ref[j:j + 1, p * gw:(p + 1) * gw]
            c_ref[...] = conv
            s = _silu(conv)
            if p == 2:
                v_ref[...] = s
            else:
                r = lax.rsqrt(_head_sums(s * s) + L2_EPS)
                (qn_ref if p == 0 else kn_ref)[...] = s * r * (DN_HEAD_DIM ** -0.5 if p == 0 else 1.0)
        ab = ab_ref[...]
        lane = lax.broadcasted_iota(jnp.int32, ab.shape, 1)
        g = -jnp.exp(al_ref[...]) * _softplus(ab + dt_ref[...])
        gb_ref[...] = jnp.where(lane < DN_HEADS, g, _sigmoid(ab))

    def xs(b):
        return pl.BlockSpec((tl, gw), lambda i: (i, b))

    def hs(b):
        return pl.BlockSpec((CONV_HALO, gw), lambda i: (jnp.maximum(i * (tl // CONV_HALO) - 1, 0), b))

    row = pl.BlockSpec((tl, gw), lambda i: (i, 0))
    abrow = pl.BlockSpec((tl, AB_PAD), lambda i: (i, 0))
    abvec = pl.BlockSpec((1, AB_PAD), lambda i: (0, 0))
    return pl.pallas_call(
        body, name=name, grid=(l // tl,),
        in_specs=[xs(QKV_BLK), xs(QKV_BLK + 1), xs(QKV_BLK + 2), hs(QKV_BLK), hs(QKV_BLK + 1), hs(QKV_BLK + 2),
                  pl.BlockSpec((DN_CONV, 3 * gw), lambda i: (0, 0)), abrow, abvec, abvec],
        out_specs=[row] * 6 + [abrow],
        out_shape=[jax.ShapeDtypeStruct((l, gw), F32)] * 6 + [jax.ShapeDtypeStruct((l, AB_PAD), F32)],
        compiler_params=_params(("parallel",)),
    )(proj, proj, proj, proj, proj, proj, conv_w, proj_ab, a_log, dt_bias)


def _gdn_pre_bwd(dq, dk, dv, cq, ck, cv, dgb, gb, proj_ab, a_log, dt_bias, name):
    l, gw = cq.shape
    tl = min(MIX_ROWS, l)

    def body(dq_ref, dk_ref, dv_ref, cq_ref, ck_ref, cv_ref, dgb_ref, gb_ref, ab_ref, al_ref, dt_ref,
             dcq_ref, dck_ref, dcv_ref, dab_ref, dal_ref, ddt_ref):
        @pl.when(pl.program_id(0) == 0)
        def _():
            dal_ref[...] = jnp.zeros_like(dal_ref)
            ddt_ref[...] = jnp.zeros_like(ddt_ref)

        for p, (d_ref, c_ref, o_ref) in enumerate(((dq_ref, cq_ref, dcq_ref), (dk_ref, ck_ref, dck_ref), (dv_ref, cv_ref, dcv_ref))):
            c, d = c_ref[...], d_ref[...]
            if p == 2:
                ds = d
            else:
                s = _silu(c)
                r = lax.rsqrt(_head_sums(s * s) + L2_EPS)
                ds = (DN_HEAD_DIM ** -0.5 if p == 0 else 1.0) * r * (d - s * r * r * _head_sums(d * s))
            o_ref[...] = ds * _silu_grad(c)
        ab, dgb_v, gb_v = ab_ref[...], dgb_ref[...], gb_ref[...]
        lane = lax.broadcasted_iota(jnp.int32, ab.shape, 1)
        is_g = lane < DN_HEADS
        dpre = dgb_v * (-jnp.exp(al_ref[...])) * _sigmoid(ab + dt_ref[...])
        dab_ref[...] = jnp.where(is_g, dpre, dgb_v * gb_v * (1.0 - gb_v)).astype(dab_ref.dtype)
        dal_ref[...] += jnp.sum(jnp.where(is_g, dgb_v * gb_v, 0.0), axis=0, keepdims=True)
        ddt_ref[...] += jnp.sum(jnp.where(is_g, dpre, 0.0), axis=0, keepdims=True)

    row = pl.BlockSpec((tl, gw), lambda i: (i, 0))
    abrow = pl.BlockSpec((tl, AB_PAD), lambda i: (i, 0))
    abvec = pl.BlockSpec((1, AB_PAD), lambda i: (0, 0))
    return pl.pallas_call(
        body, name=name, grid=(l // tl,),
        in_specs=[row] * 6 + [abrow, abrow, abrow, abvec, abvec],
        out_specs=[row, row, row, abrow, abvec, abvec],
        out_shape=[jax.ShapeDtypeStruct((l, gw), F32)] * 3 + [jax.ShapeDtypeStruct((l, AB_PAD), BF16),
                   jax.ShapeDtypeStruct((1, AB_PAD), F32), jax.ShapeDtypeStruct((1, AB_PAD), F32)],
        compiler_params=_params(("arbitrary",)),
    )(dq, dk, dv, cq, ck, cv, dgb, gb, proj_ab, a_log, dt_bias)


def _conv_bwd(dc, proj, col_blk, w_part, name):
    l, gw = dc.shape
    tl = min(MIX_ROWS, l)
    nb = l // tl

    def body(dc_ref, halo_ref, x_ref, w_ref, dx_ref, dw_ref):
        i = pl.program_id(0)

        @pl.when(i == 0)
        def _():
            dw_ref[...] = jnp.zeros_like(dw_ref)

        ext = jnp.concatenate([dc_ref[...], jnp.where(i < nb - 1, halo_ref[...], 0.0)], axis=0)
        x = x_ref[...]
        dx = jnp.zeros((tl, gw), F32)
        rid = lax.broadcasted_iota(jnp.int32, (8, gw), 0)
        dw = jnp.zeros((8, gw), F32)
        for j in range(DN_CONV):
            k = DN_CONV - 1 - j
            shifted = (ext if k == 0 else pltpu.roll(ext, tl + CONV_HALO - k, axis=0))[:tl, :]
            dx = dx + shifted * w_ref[j:j + 1, :]
            dw = dw + jnp.where(rid == j, jnp.sum(x * shifted, axis=0, keepdims=True), 0.0)
        dx_ref[...] = dx.astype(dx_ref.dtype)
        dw_ref[...] += dw

    row = pl.BlockSpec((tl, gw), lambda i: (i, 0))
    return pl.pallas_call(
        body, name=name, grid=(nb,),
        in_specs=[row, pl.BlockSpec((CONV_HALO, gw), lambda i: (jnp.minimum((i + 1) * (tl // CONV_HALO), l // CONV_HALO - 1), 0)),
                  pl.BlockSpec((tl, gw), lambda i: (i, col_blk)), pl.BlockSpec((DN_CONV, gw), lambda i: (0, 0))],
        out_specs=[row, pl.BlockSpec((8, gw), lambda i: (0, 0))],
        out_shape=[jax.ShapeDtypeStruct((l, gw), BF16), jax.ShapeDtypeStruct((8, gw), F32)],
        compiler_params=_params(("arbitrary",)),
    )(dc, dc, proj, w_part)


TERMS_CHUNKS = 8


def _bdot(a, b, dims="nn", precision=None):
    cd = {"nn": ((2,), (1,)), "nt": ((2,), (2,)), "tn": ((1,), (1,))}[dims]
    return lax.dot_general(a, b, (cd, ((0,), (0,))), preferred_element_type=F32, precision=precision)


def _bmdot(a, b, dims="nn"):
    return _bdot(_mx(a), _mx(b), dims)


def _bdot3(a, b, dims="nn"):
    ah, bh = a.astype(BF16), b.astype(BF16)
    al, bl = (a - ah.astype(F32)).astype(BF16), (b - bh.astype(F32)).astype(BF16)
    return _bdot(ah, bh, dims) + (_bdot(ah, bl, dims) + _bdot(al, bh, dims))


def _wy_terms(q, k, v, gcol, beta, t=None):
    c = DN_CHUNK
    ii = lax.broadcasted_iota(jnp.int32, (1, c, c), 1)
    jj = lax.broadcasted_iota(jnp.int32, (1, c, c), 2)
    tril, strict = ii >= jj, ii > jj
    grow = jnp.sum(jnp.where(ii == jj, gcol, 0.0), axis=1, keepdims=True)
    gc_col = jnp.sum(jnp.where(tril, grow, 0.0), axis=2, keepdims=True)
    gc_row = jnp.sum(jnp.where(ii <= jj, gcol, 0.0), axis=1, keepdims=True)
    dec = jnp.exp(jnp.where(tril, gc_col - gc_row, -1e30))
    kb, vb = k * beta, v * beta
    kk = _bmdot(kb, k, "nt")
    if t is None:
        a = jnp.where(strict, kk * dec, 0.0)
        d = jnp.where((ii >> 3) == (jj >> 3), a, 0.0)
        t = jnp.where(ii == jj, 1.0, 0.0) - d
        p = _bdot(d, d, precision=HI)
        t = t + _bdot(t, p, precision=HI)
        t = t + _bdot(t, _bdot(p, p, precision=HI), precision=HI)
        for sh in (3, 4, 5):
            below = ((ii >> (sh + 1)) == (jj >> (sh + 1))) & ((ii >> sh) > (jj >> sh))
            t = t - _bdot3(t, _bdot3(jnp.where(below, a, 0.0), t))
    eg = jnp.exp(gc_col)
    gc_last = gc_col[:, c - 1:c, :]
    kbg = kb * eg
    qk0 = _bmdot(q, k, "nt")
    e2 = jnp.exp(gc_last - gc_col)
    return dict(ii=ii, jj=jj, tril=tril, strict=strict, dec=dec, kb=kb, vb=vb, kk=kk, t=t, eg=eg, kbg=kbg,
                qk0=qk0, qk=jnp.where(tril, qk0 * dec, 0.0), qg=q * eg, e2=e2, kt=k * e2, gl=jnp.exp(gc_last))


def _to_heads(x, g):
    return jnp.concatenate([x[:, h * DN_HEAD_DIM:(h + 1) * DN_HEAD_DIM].reshape(g, DN_CHUNK, DN_HEAD_DIM)
                            for h in range(DN_HEADS)], axis=0)


def _from_heads(t, ref, g):
    for h in range(DN_HEADS):
        ref[:, h * DN_HEAD_DIM:(h + 1) * DN_HEAD_DIM] = t[h * g:(h + 1) * g].reshape(g * DN_CHUNK, DN_HEAD_DIM).astype(ref.dtype)


def _head_columns(gbv, first_lane, g):
    lane = lax.broadcasted_iota(jnp.int32, gbv.shape, 1)
    return jnp.concatenate([jnp.sum(jnp.where(lane == first_lane + h, gbv, 0.0), axis=1, keepdims=True).reshape(g, DN_CHUNK, 1)
                            for h in range(DN_HEADS)], axis=0)


def _gdn_terms_fwd(qn, kn, v, gb, name):
    l = qn.shape[0]
    n_chunks = l // DN_CHUNK
    g = min(TERMS_CHUNKS, n_chunks)
    rows, c, nh = g * DN_CHUNK, DN_CHUNK, DN_HEADS

    def body(q_ref, k_ref, v_ref, gb_ref, u_ref, w_ref, qg_ref, kt_ref, qk_ref, t_ref, gl_ref):
        gbv = gb_ref[...]
        x = _wy_terms(_to_heads(q_ref[...], g), _to_heads(k_ref[...], g), _to_heads(v_ref[...], g),
                      _head_columns(gbv, 0, g), _head_columns(gbv, nh, g))
        _from_heads(_bmdot(x["t"], x["vb"]), u_ref, g)
        _from_heads(_bmdot(x["t"], x["kbg"]), w_ref, g)
        _from_heads(x["qg"], qg_ref, g)
        _from_heads(x["kt"], kt_ref, g)
        for h in range(nh):
            qk_ref[:, h] = x["qk"][h * g:(h + 1) * g]
            t_ref[:, h] = x["t"][h * g:(h + 1) * g]
            gl_ref[:, h] = jnp.broadcast_to(x["gl"][h * g:(h + 1) * g], (g, 1, 128))

    row = pl.BlockSpec((rows, GROUP_WIDTH), lambda i: (i, 0))
    sq = pl.BlockSpec((g, nh, c, c), lambda i: (i, 0, 0, 0))
    glb = pl.BlockSpec((g, nh, 1, 128), lambda i: (i, 0, 0, 0))
    return pl.pallas_call(
        body, name=name, grid=(n_chunks // g,), in_specs=[row, row, row, pl.BlockSpec((rows, AB_PAD), lambda i: (i, 0))],
        out_specs=[row] * 4 + [sq, sq, glb],
        out_shape=[jax.ShapeDtypeStruct((l, GROUP_WIDTH), F32)] * 4 + [jax.ShapeDtypeStruct((n_chunks, nh, c, c), F32)] * 2
        + [jax.ShapeDtypeStruct((n_chunks, nh, 1, 128), F32)],
        compiler_params=_params(("parallel",)),
    )(qn, kn, v, gb)


REC_CHUNKS = 8


def _rec_specs(n_chunks, reverse):
    c, hd, nh = DN_CHUNK, DN_HEAD_DIM, DN_HEADS
    g = min(REC_CHUNKS, n_chunks)
    nb = n_chunks // g
    blk_of = (lambda n: nb - 1 - n) if reverse else (lambda n: n)
    return g, nb, (pl.BlockSpec((g * c, GROUP_WIDTH), lambda n: (blk_of(n), 0)), pl.BlockSpec((g, nh, c, c), lambda n: (blk_of(n), 0, 0, 0)),
                   pl.BlockSpec((g, nh, 1, 128), lambda n: (blk_of(n), 0, 0, 0)), pl.BlockSpec((g, nh, hd, hd), lambda n: (blk_of(n), 0, 0, 0)))


def _gdn_rec_fwd(u, w, qg, kt, qk, gl, name):
    l = u.shape[0]
    n_chunks = l // DN_CHUNK
    hd, nh, c = DN_HEAD_DIM, DN_HEADS, DN_CHUNK
    g, nb, (blk, sq, glb, st) = _rec_specs(n_chunks, False)
    heads = range(nh)

    def body(u_ref, w_ref, qg_ref, kt_ref, qk_ref, gl_ref, o_ref, vn_ref, s_ref, state):
        @pl.when(pl.program_id(0) == 0)
        def _():
            state[...] = jnp.zeros_like(state)

        def cols(h):
            return slice(h * hd, (h + 1) * hd)
        for ci in range(g):
            rows = slice(ci * c, (ci + 1) * c)
            s = [state[h] for h in heads]
            ws = [_mdot(w_ref[rows, cols(h)], s[h]) for h in heads]
            vn = [u_ref[rows, cols(h)] - ws[h] for h in heads]
            kv = [_mdot(kt_ref[rows, cols(h)], vn[h], "tn") for h in heads]
            for h in heads:
                state[h] = s[h] * gl_ref[ci, h] + kv[h]
            o1 = [_mdot(qg_ref[rows, cols(h)], s[h]) for h in heads]
            o2 = [_mdot(qk_ref[ci, h], vn[h]) for h in heads]
            for h in heads:
                s_ref[ci, h] = s[h]
                o_ref[rows, cols(h)] = o1[h] + o2[h]
                vn_ref[rows, cols(h)] = vn[h]

    return pl.pallas_call(
        body, name=name, grid=(nb,), in_specs=[blk, blk, blk, blk, sq, glb], out_specs=[blk, blk, st],
        out_shape=[jax.ShapeDtypeStruct((l, GROUP_WIDTH), F32)] * 2 + [jax.ShapeDtypeStruct((n_chunks, nh, hd, hd), F32)],
        scratch_shapes=[pltpu.VMEM((nh, hd, hd), F32)], compiler_params=_params(("arbitrary",)),
    )(u, w, qg, kt, qk, gl)


def _gdn_rec_bwd(do, w, qg, kt, vn, qk, gl, states, name):
    l = do.shape[0]
    n_chunks = l // DN_CHUNK
    hd, nh, c = DN_HEAD_DIM, DN_HEADS, DN_CHUNK
    g, nb, (blk, sq, glb, st) = _rec_specs(n_chunks, True)
    heads = range(nh)

    def body(do_ref, w_ref, qg_ref, kt_ref, vn_ref, qk_ref, gl_ref, s_ref, dvn_ref, dw_ref, dkt_ref, dqg_ref, dqk_ref, dgl_ref, dstate):
        @pl.when(pl.program_id(0) == 0)
        def _():
            dstate[...] = jnp.zeros_like(dstate)

        def cols(h):
            return slice(h * hd, (h + 1) * hd)
        tril = lax.broadcasted_iota(jnp.int32, (c, c), 0) >= lax.broadcasted_iota(jnp.int32, (c, c), 1)
        for ci in reversed(range(g)):
            rows = slice(ci * c, (ci + 1) * c)
            ds = [dstate[h] for h in heads]
            dout = [do_ref[rows, cols(h)] for h in heads]
            a1 = [_mdot(qk_ref[ci, h], dout[h], "tn") for h in heads]
            a2 = [_mdot(kt_ref[rows, cols(h)], ds[h]) for h in heads]
            dvn = [a1[h] + a2[h] for h in heads]
            b1 = [_mdot(qg_ref[rows, cols(h)], dout[h], "tn") for h in heads]
            b2 = [_mdot(w_ref[rows, cols(h)], dvn[h], "tn") for h in heads]
            for h in heads:
                dstate[h] = b1[h] + gl_ref[ci, h] * ds[h] - b2[h]
            for h in heads:
                s, vnew = s_ref[ci, h], vn_ref[rows, cols(h)]
                dvn_ref[rows, cols(h)] = dvn[h]
                dw_ref[rows, cols(h)] = -_mdot(dvn[h], s, "nt")
                dkt_ref[rows, cols(h)] = _mdot(vnew, ds[h], "nt")
                dqg_ref[rows, cols(h)] = _mdot(dout[h], s, "nt")
                dqk_ref[ci, h] = jnp.where(tril, _mdot(dout[h], vnew, "nt"), 0.0)
                dgl = jnp.sum(jnp.sum(ds[h] * s, axis=1, keepdims=True), axis=0, keepdims=True)
                dgl_ref[ci, h] = jnp.broadcast_to(dgl, (1, 128))

    return pl.pallas_call(
        body, name=name, grid=(nb,), in_specs=[blk] * 5 + [sq, glb, st], out_specs=[blk] * 4 + [sq, glb],
        out_shape=[jax.ShapeDtypeStruct((l, GROUP_WIDTH), F32)] * 4 + [jax.ShapeDtypeStruct((n_chunks, nh, c, c), F32),
                                                                       jax.ShapeDtypeStruct((n_chunks, nh, 1, 128), F32)],
        scratch_shapes=[pltpu.VMEM((nh, hd, hd), F32)], compiler_params=_params(("arbitrary",)),
    )(do, w, qg, kt, vn, qk, gl, states)


def _gdn_terms_bwd(qn, kn, v, gb, t_inv, dvn, dw, dkt, dqg, dqk, dgl, name):
    l = qn.shape[0]
    n_chunks = l // DN_CHUNK
    g = min(TERMS_CHUNKS, n_chunks)
    rows, c, nh = g * DN_CHUNK, DN_CHUNK, DN_HEADS

    def body(q_ref, k_ref, v_ref, gb_ref, t_ref, dvn_ref, dw_ref, dkt_ref, dqg_ref, dqk_ref, dgl_ref, dq_ref, dk_ref, dv_ref, dgb_ref):
        gbv = gb_ref[...]
        q, k, vv = _to_heads(q_ref[...], g), _to_heads(k_ref[...], g), _to_heads(v_ref[...], g)
        beta = _head_columns(gbv, nh, g)
        t = jnp.concatenate([t_ref[:, h] for h in range(nh)], axis=0)
        x = _wy_terms(q, k, vv, _head_columns(gbv, 0, g), beta, t=t)
        ii, jj, strict = x["ii"], x["jj"], x["strict"]
        du, dwv, dktv, dqgv = (_to_heads(r[...], g) for r in (dvn_ref, dw_ref, dkt_ref, dqg_ref))
        dqkv = jnp.concatenate([dqk_ref[:, h] for h in range(nh)], axis=0)
        dglv = jnp.concatenate([dgl_ref[:, h] for h in range(nh)], axis=0)[:, :, 0:1]
        dt = _bmdot(du, x["vb"], "nt") + _bmdot(dwv, x["kbg"], "nt")
        dvb = _bmdot(t, du, "tn")
        dkbg = _bmdot(t, dwv, "tn")
        da = jnp.where(strict, -_bdot3(_bdot3(t, dt, "tn"), t, "nt"), 0.0)
        dkk = da * x["dec"]
        dqk0 = dqkv * x["dec"]
        e = (da * x["kk"] + dqkv * x["qk0"]) * x["dec"]
        dkb = _bmdot(dkk, k) + dkbg * x["eg"]
        dk = _bmdot(dkk, x["kb"], "tn") + _bmdot(dqk0, q, "tn") + dktv * x["e2"] + dkb * beta
        dq = _bmdot(dqk0, k) + dqgv * x["eg"]
        s_kt = jnp.sum(dktv * x["kt"], axis=2, keepdims=True)
        dgc_c = (jnp.sum(e, axis=2, keepdims=True) + jnp.sum(dqgv * x["qg"], axis=2, keepdims=True) - s_kt
                 + jnp.sum(dkbg * x["kbg"], axis=2, keepdims=True))
        dgc_last = jnp.sum(s_kt, axis=1, keepdims=True) + dglv * x["gl"]
        rid = lax.broadcasted_iota(jnp.int32, (1, c, 1), 1)
        dgc_c = dgc_c + jnp.where(rid == c - 1, dgc_last, 0.0)
        dgc_r = jnp.sum(jnp.where(ii == jj, dgc_c, 0.0), axis=1, keepdims=True) - jnp.sum(e, axis=1, keepdims=True)
        dg = jnp.sum(jnp.where(jj >= ii, dgc_r, 0.0), axis=2, keepdims=True)
        dbeta = jnp.sum(dkb * k, axis=2, keepdims=True) + jnp.sum(dvb * vv, axis=2, keepdims=True)
        _from_heads(dq, dq_ref, g)
        _from_heads(dk, dk_ref, g)
        _from_heads(dvb * beta, dv_ref, g)
        lane = lax.broadcasted_iota(jnp.int32, gbv.shape, 1)
        dgb = jnp.zeros(gbv.shape, F32)
        for h in range(nh):
            dgb = dgb + jnp.where(lane == h, dg[h * g:(h + 1) * g].reshape(rows, 1), 0.0)
            dgb = dgb + jnp.where(lane == nh + h, dbeta[h * g:(h + 1) * g].reshape(rows, 1), 0.0)
        dgb_ref[...] = dgb

    row = pl.BlockSpec((rows, GROUP_WIDTH), lambda i: (i, 0))
    abrow = pl.BlockSpec((rows, AB_PAD), lambda i: (i, 0))
    sq = pl.BlockSpec((g, nh, c, c), lambda i: (i, 0, 0, 0))
    glb = pl.BlockSpec((g, nh, 1, 128), lambda i: (i, 0, 0, 0))
    return pl.pallas_call(
        body, name=name, grid=(n_chunks // g,), in_specs=[row, row, row, abrow, sq, row, row, row, row, sq, glb],
        out_specs=[row, row, row, abrow],
        out_shape=[jax.ShapeDtypeStruct((l, GROUP_WIDTH), F32)] * 3 + [jax.ShapeDtypeStruct((l, AB_PAD), F32)],
        compiler_params=_params(("parallel",)),
    )(qn, kn, v, gb, t_inv, dvn, dw, dkt, dqg, dqk, dgl)


def _gdn_post_fwd(o, proj, norm_g4, name):
    l, gw = o.shape
    tl = min(MIX_ROWS, l)

    def body(o_ref, gate_ref, g_ref, out_ref):
        ov = o_ref[...]
        r = lax.rsqrt(_head_sums(ov * ov) * (1.0 / DN_HEAD_DIM) + RMS_EPS)
        out_ref[...] = (ov * r * g_ref[...] * _silu(gate_ref[...])).astype(out_ref.dtype)

    row = pl.BlockSpec((tl, gw), lambda i: (i, 0))
    return pl.pallas_call(
        body, name=name, grid=(l // tl,),
        in_specs=[row, pl.BlockSpec((tl, gw), lambda i: (i, 7)), pl.BlockSpec((1, gw), lambda i: (0, 0))],
        out_specs=row, out_shape=jax.ShapeDtypeStruct((l, gw), BF16), compiler_params=_params(("parallel",)),
    )(o, proj, norm_g4)


def _gdn_post_bwd(dmixed, o, proj, norm_g4, name):
    l, gw = o.shape
    tl = min(MIX_ROWS, l)

    def body(d_ref, o_ref, gate_ref, g_ref, do_ref, dgate_ref, dng_ref):
        @pl.when(pl.program_id(0) == 0)
        def _():
            dng_ref[...] = jnp.zeros_like(dng_ref)

        ov, gate, d = o_ref[...], gate_ref[...], d_ref[...]
        r = lax.rsqrt(_head_sums(ov * ov) * (1.0 / DN_HEAD_DIM) + RMS_EPS)
        oh = ov * r
        sg = _silu(gate)
        dgate_ref[...] = (d * oh * g_ref[...] * _silu_grad(gate)).astype(dgate_ref.dtype)
        dng_ref[...] += jnp.sum(d * sg * oh, axis=0, keepdims=True)
        doh = d * g_ref[...] * sg
        do_ref[...] = r * (doh - oh * _head_sums(doh * oh) * (1.0 / DN_HEAD_DIM))

    row = pl.BlockSpec((tl, gw), lambda i: (i, 0))
    vec = pl.BlockSpec((1, gw), lambda i: (0, 0))
    return pl.pallas_call(
        body, name=name, grid=(l // tl,),
        in_specs=[pl.BlockSpec((tl, gw), lambda i: (i, 3)), row, pl.BlockSpec((tl, gw), lambda i: (i, 7)), vec],
        out_specs=[row, row, vec],
        out_shape=[jax.ShapeDtypeStruct((l, gw), F32), jax.ShapeDtypeStruct((l, gw), BF16), jax.ShapeDtypeStruct((1, gw), F32)],
        compiler_params=_params(("arbitrary",)),
    )(dmixed, o, proj, norm_g4)


def _run(hosts, name, fn):
    h = hosts.get(name)
    if h is None:
        return fn(None)
    res, outs = fn(h[0]())
    h[1](outs)
    return res


def _layer_fwd(x, xm, w, li, hosts):
    l = x.shape[0]
    nm = f"l{li}_"
    proj = _run(hosts, nm + "proj", lambda ops: _matmul(
        xm, w["w_main"], mode="nn", tm=1024, tn=1024, tk=2048,out_dtype=F32, name=nm + "proj", comm=ops))
    proj_ab = _matmul(xm, w["w_ab"], mode="nn", tm=1024, tn=AB_PAD, tk=2048, out_dtype=F32, name=nm + "proj_ab")
    hs, y = _s5_fwd(proj, w["s5_b"], w["s5_c"], w["s5_lam"], w["s5_d"], nm + "s5")
    m_s5 = _s5_glu_fwd(y, w["s5_glu_w"], w["s5_glu_b"], nm + "s5_glu")
    m_sgu = _sgu_fwd(proj, w["sgu_norm_g"], w["sgu_norm_b"], w["sgu_wm"], w["sgu_bfull"], nm + "sgu")
    m_pool, pooled = _pool_fwd(proj, w["pool_w"], w["pool_scale"], nm + "pool")
    qn, kn, v, cq, ck, cv, gb = _gdn_pre_fwd(proj, proj_ab, w["dn_conv_w"], w["dn_a_log"], w["dn_dt_bias"], nm + "gdn_pre")
    u, wy, qg, kt, qk, t_inv, gl = _gdn_terms_fwd(qn, kn, v, gb, nm + "gdn_terms")
    o, vn, states = _gdn_rec_fwd(u, wy, qg, kt, qk, gl, nm + "gdn_rec")
    m_dn = _gdn_post_fwd(o, proj, w["dn_norm_g4"], nm + "gdn_post")
    mixed = jnp.concatenate([m_s5, m_sgu, m_pool, m_dn], axis=1)
    y1 = _matmul(mixed, w["w_out"], mode="nn", tm=1024, tn=1024, tk=2048, out_dtype=F32, name=nm + "out_proj")
    h1, x1, x1m = _ln_fwd(x, y1, w["ln1_g"], w["ln1_b"], nm + "ln1")
    r = _run(hosts, nm + "up", lambda ops: _matmul(
        x1m, w["w_up"], mode="nn", tm=1024, tn=1024, tk=2048,out_dtype=BF16, name=nm + "up",
        epi=lambda acc: jnp.maximum(acc, 0.0), b_slab=w["w_up"].shape[2], comm=ops))
    y2 = _run(hosts, nm + "down", lambda ops: _matmul(
        r, w["w_down"], mode="nn", tm=1024, tn=1024, tk=2048,out_dtype=F32, name=nm + "down", a_fn=lambda a: a * a, comm=ops))
    h2, x2, x2m = _ln_fwd(x1, y2, w["ln2_g"], w["ln2_b"], nm + "ln2")
    saved = dict(xm=xm, proj=proj, proj_ab=proj_ab, hs=hs, y=y, pooled=pooled, qn=qn, kn=kn, v=v, cq=cq, ck=ck, cv=cv, gb=gb,
                 wy=wy, qg=qg, kt=kt, qk=qk, t_inv=t_inv, gl=gl, vn=vn, o=o, states=states, mixed=mixed, h1=h1, x1m=x1m,
                 r=r, h2=h2)
    return x2, x2m, saved


def _layer_bwd(dx2, s, w, small, li, hosts, g):
    nm = f"l{li}b_"
    l = dx2.shape[0]
    gw = GROUP_WIDTH
    wire = MXU_DTYPE
    dh2, dh2m, g["ln2_g"], g["ln2_b"] = _ln_bwd(dx2, s["h2"], w["ln2_g"], nm + "ln2")
    g["w_down"] = _run(hosts, nm + "dw_down", lambda ops: _matmul(
        s["r"], dh2m, mode="tn", tm=1024, tn=1024, tk=2048,out_dtype=wire, name=nm + "dw_down", a_fn=lambda a: a * a,
        comm=ops)).reshape(N_DEV, D_FF // N_DEV, D_MODEL)
    dpre = _run(hosts, nm + "dpre", lambda ops: _matmul(
        dh2m, w["w_down"], mode="nt", tm=1024, tn=1024, tk=2048,out_dtype=BF16, name=nm + "dpre",
        extras=[(s["r"], (None, None), lambda i, j: (i, j))], epi=lambda acc, r: acc * 2.0 * r.astype(F32), comm=ops))
    g["w_up"] = _matmul(s["x1m"], dpre, mode="tn", tm=1024, tn=1024, tk=2048,out_dtype=wire, name=nm + "dw_up",
                        out_slab=D_FF // N_DEV)
    dx1 = _run(hosts, nm + "dx1", lambda ops: _matmul(
        dpre, w["w_up"], mode="nt", tm=1024, tn=1024, tk=2048,out_dtype=F32, name=nm + "dx1",
        extras=[(dh2, (None, None), lambda i, j: (i, j))], epi=lambda acc, e: acc + ALPHA * e,
        b_slab=w["w_up"].shape[2], comm=ops))
    dh1, dh1m, g["ln1_g"], g["ln1_b"] = _ln_bwd(dx1, s["h1"], w["ln1_g"], nm + "ln1")
    g["w_out"] = _matmul(s["mixed"], dh1m, mode="tn", tm=1024, tn=1024, tk=2048,out_dtype=wire,
                         name=nm + "dw_out").reshape(N_DEV, D_MODEL // N_DEV, D_MODEL)
    dmixed = _run(hosts, nm + "dmixed", lambda ops: _matmul(
        dh1m, w["w_out"], mode="nt", tm=1024, tn=1024, tk=2048,out_dtype=F32, name=nm + "dmixed", comm=ops))
    proj, proj_ab = s["proj"], s["proj_ab"]
    dy, dz, yg, g["s5_glu_b"], g["s5_d"] = _s5_glu_bwd(dmixed, s["y"], proj, w["s5_glu_w"], w["s5_glu_b"], nm + "s5_glu")
    g["s5_glu_w"] = _matmul(yg, dz, mode="tn", tm=gw, tn=gw, tk=1024, out_dtype=wire,
                            name=nm + "dw_glu").reshape(N_DEV, gw // N_DEV, gw)
    du_s5, g["s5_b"], g["s5_c"], g["s5_lam"] = _s5_bwd(dy, s["hs"], proj, w["s5_b"], w["s5_c"], w["s5_lam_conj"], w["s5_d"], nm + "s5")
    dzu, dzv, g["sgu_w"], g["sgu_bfull"], g["sgu_norm_g"], g["sgu_norm_b"] = _sgu_bwd(
        dmixed, proj, w["sgu_norm_g"], w["sgu_norm_b"], w["sgu_wm"], w["sgu_bfull"], nm + "sgu")
    dpooled, g["pool_w"], g["pool_scale"] = _pool_bwd_map(dmixed, s["pooled"], w["pool_w"], w["pool_scale"], nm + "pool_map")
    dp = _pool_bwd_window(dpooled, nm + "pool_win")
    do, dgate, g["dn_norm_g4"] = _gdn_post_bwd(dmixed, s["o"], proj, w["dn_norm_g4"], nm + "gdn_post")
    dvn, dwy, dkt, dqg, dqk, dgl = _gdn_rec_bwd(do, s["wy"], s["qg"], s["kt"], s["vn"], s["qk"], s["gl"], s["states"], nm + "gdn_rec")
    dq, dk, dv, dgb = _gdn_terms_bwd(s["qn"], s["kn"], s["v"], s["gb"], s["t_inv"], dvn, dwy, dkt, dqg, dqk, dgl, nm + "gdn_terms")
    dcq, dck, dcv, dab, g["dn_a_log"], g["dn_dt_bias"] = _gdn_pre_bwd(
        dq, dk, dv, s["cq"], s["ck"], s["cv"], dgb, s["gb"], proj_ab, w["dn_a_log"], w["dn_dt_bias"], nm + "gdn_pre")
    dxs, dws = [], []
    for p, dc in enumerate((dcq, dck, dcv)):
        dxp, dwp = _conv_bwd(dc, proj, QKV_BLK + p, w["dn_conv_w"][:, p * gw:(p + 1) * gw], nm + f"conv{p}")
        dxs.append(dxp)
        dws.append(dwp)
    dconv = jnp.concatenate(dws, axis=1)
    g["dn_conv_w"] = jnp.transpose(dconv.reshape(dconv.shape[0], N_DEV, 3 * gw // N_DEV), (1, 0, 2))
    dproj = jnp.concatenate([du_s5, dzu, dzv, dp] + dxs + [dgate], axis=1)
    xm = s["xm"]
    g["small"] = _unprep_grads(g, small)
    dw_main = _run(hosts, nm + "dw_main", lambda ops: _matmul(
        xm, dproj, mode="tn", tm=1024, tn=1024, tk=2048,out_dtype=wire, name=nm + "dw_main", comm=ops))
    dw_ab = _matmul(xm, dab, mode="tn", tm=1024, tn=AB_PAD, tk=1024, out_dtype=wire, name=nm + "dw_ab")
    dw_in = jnp.concatenate([dw_main, dw_ab[:, :2 * DN_HEADS]], axis=1)
    g["w_in"] = jnp.transpose(dw_in.reshape(D_MODEL, N_DEV, dw_in.shape[1] // N_DEV), (1, 0, 2))
    return _run(hosts, nm + "dx", lambda ops: _matmul(
        dproj, w["w_main"], mode="nt", tm=1024, tn=1024, tk=2048, out_dtype=F32, name=nm + "dx",
        extras=[(dh1, (None, None), lambda i, j: (i, j)), (dab, (None, AB_PAD), lambda i, j: (i, 0)),
                (w["w_ab"], ("tn", AB_PAD), lambda i, j: (j, 0))],
        epi=lambda acc, e, da, wab: acc + ALPHA * e + _dot(_mx(da), _mx(wab), "nt"), comm=ops))


SMALL = ("s5_lambda_re", "s5_lambda_im", "s5_log_step", "s5_b_re", "s5_b_im", "s5_c_re", "s5_c_im", "s5_d", "s5_glu_b",
         "sgu_norm_g", "sgu_norm_b", "sgu_w", "sgu_b", "pool_w", "pool_scale", "dn_a_log", "dn_dt_bias", "dn_norm_g",
         "ln1_g", "ln1_b", "ln2_g", "ln2_b")
SHARDED = ("w_in", "s5_glu_w", "dn_conv_w", "w_out", "w_up", "w_down")


def _pad_lanes(v, width=AB_PAD):
    return jnp.pad(v.reshape(1, -1), ((0, 0), (0, width - v.size)))


def _prep_small(p):
    mx = MXU_DTYPE
    lbr, lbi, bbr, bbi = _s5_discretize(p["s5_lambda_re"], p["s5_lambda_im"], p["s5_log_step"], p["s5_b_re"], p["s5_b_im"])
    b_compact, c_compact = _s5_compact(bbr, bbi, p["s5_c_re"], p["s5_c_im"])
    causal = jnp.tril(jnp.ones((SGU_CHUNK, SGU_CHUNK), F32))
    return dict(
        s5_b=b_compact.astype(mx), s5_c=c_compact.astype(mx),
        s5_lam=jnp.concatenate([lbr.reshape(1, -1), lbi.reshape(1, -1)], axis=1),
        s5_lam_conj=jnp.concatenate([lbr.reshape(1, -1), -lbi.reshape(1, -1)], axis=1),
        s5_d=p["s5_d"].reshape(1, -1), s5_glu_b=p["s5_glu_b"].reshape(1, -1),
        sgu_norm_g=p["sgu_norm_g"].reshape(1, -1), sgu_norm_b=p["sgu_norm_b"].reshape(1, -1),
        sgu_wm=(p["sgu_w"] * causal).astype(mx), sgu_bfull=jnp.repeat(p["sgu_b"].T, GROUP_WIDTH // SGU_HEADS, axis=1),
        pool_w=p["pool_w"].astype(mx), pool_scale=p["pool_scale"].reshape(1, -1),
        dn_a_log=_pad_lanes(p["dn_a_log"]), dn_dt_bias=_pad_lanes(p["dn_dt_bias"]),
        dn_norm_g4=jnp.tile(p["dn_norm_g"].reshape(1, -1), (1, DN_HEADS)),
        ln1_g=p["ln1_g"].reshape(1, -1), ln1_b=p["ln1_b"].reshape(1, -1),
        ln2_g=p["ln2_g"].reshape(1, -1), ln2_b=p["ln2_b"].reshape(1, -1),
    )


def _weight_views(name, t):
    if name == "w_in":
        w_in = jnp.transpose(t, (1, 0, 2)).reshape(t.shape[1], N_DEV * t.shape[2])
        pad = AB_PAD - (w_in.shape[1] - MAIN_COLS)
        return dict(w_main=w_in[:, :MAIN_COLS], w_ab=jnp.pad(w_in[:, MAIN_COLS:], ((0, 0), (0, pad))))
    if name == "dn_conv_w":
        return dict(dn_conv_w=jnp.transpose(t, (1, 0, 2)).reshape(t.shape[1], N_DEV * t.shape[2]))
    if name == "w_up":
        return dict(w_up=t)
    return {name: t.reshape(N_DEV * t.shape[1], t.shape[2])}


def _unprep_grads(g, p):
    causal = jnp.tril(jnp.ones((SGU_CHUNK, SGU_CHUNK), F32))
    dbbr, dbbi = _s5_uncompact_b(g["s5_b"])
    dc_re, dc_im = _s5_uncompact_c(g["s5_c"])
    dlbr, dlbi = g["s5_lam"][0, :S5_NS].reshape(S5_GROUPS, S5_STATE), g["s5_lam"][0, S5_NS:].reshape(S5_GROUPS, S5_STATE)
    _, vjp = jax.vjp(_s5_discretize, p["s5_lambda_re"], p["s5_lambda_im"], p["s5_log_step"], p["s5_b_re"], p["s5_b_im"])
    d_lre, d_lim, d_step, d_bre, d_bim = vjp((dlbr, dlbi, dbbr, dbbi))
    hd = GROUP_WIDTH // SGU_HEADS
    return dict(
        s5_lambda_re=d_lre, s5_lambda_im=d_lim, s5_log_step=d_step, s5_b_re=d_bre, s5_b_im=d_bim, s5_c_re=dc_re, s5_c_im=dc_im,
        s5_d=g["s5_d"].reshape(S5_GROUPS, S5_CH), s5_glu_b=g["s5_glu_b"].reshape(-1),
        sgu_norm_g=g["sgu_norm_g"].reshape(-1), sgu_norm_b=g["sgu_norm_b"].reshape(-1), sgu_w=g["sgu_w"] * causal,
        sgu_b=jnp.sum(g["sgu_bfull"].reshape(SGU_CHUNK, SGU_HEADS, hd), axis=2).T,
        pool_w=g["pool_w"], pool_scale=g["pool_scale"].reshape(-1),
        dn_a_log=g["dn_a_log"][0, :DN_HEADS], dn_dt_bias=g["dn_dt_bias"][0, :DN_HEADS],
        dn_norm_g=jnp.sum(g["dn_norm_g4"].reshape(DN_HEADS, DN_HEAD_DIM), axis=0),
        ln1_g=g["ln1_g"].reshape(-1), ln1_b=g["ln1_b"].reshape(-1), ln2_g=g["ln2_g"].reshape(-1), ln2_b=g["ln2_b"].reshape(-1),
    )


def _local_step(x, target, ops, small, fwd_hosts, bwd_hosts, grads):
    saved = []
    h, hm = x, x.astype(MXU_DTYPE)
    for i in range(DEPTH):
        h, hm, s = _layer_fwd(h, hm, ops[i], i, fwd_hosts)
        saved.append(s)
    loss, dh = _loss_head(h, target)
    for i in reversed(range(DEPTH)):
        dh = _layer_bwd(dh, saved[i], ops[i], small[i], i, bwd_hosts, grads[i])
    return loss, dh


def _adamw(w, gparts, m, v, name):
    rr, c = w.shape
    ng = len(gparts)
    r = rr // ng
    lanes = -(-c // 128) * 128
    tr = r
    while tr * lanes * 4 * N_DEV > (4 << 20) and tr % 16 == 0:
        tr //= 2
    nb = r // tr

    def body(w_ref, *rest):
        g_refs, (m_ref, v_ref, go_ref, d_ref, mo_ref, vo_ref) = rest[:ng], rest[ng:]
        layer = pl.program_id(0)
        g = jnp.zeros(m_ref.shape, F32)
        for li in range(ng):
            gl = g_refs[li][0].astype(F32)
            for s in range(1, N_DEV):
                gl = gl + g_refs[li][s].astype(F32)
            g = jnp.where(layer == li, gl, g)
        mn = ADAM_B1 * m_ref[...] + (1.0 - ADAM_B1) * g
        vn = ADAM_B2 * v_ref[...] + (1.0 - ADAM_B2) * g * g
        m_hat = mn / (1.0 - ADAM_B1 ** ADAM_STEP)
        v_hat = vn / (1.0 - ADAM_B2 ** ADAM_STEP)
        go_ref[...] = g
        d_ref[...] = -ADAM_LR * (m_hat / (jnp.sqrt(v_hat) + ADAM_EPS) + ADAM_WD * w_ref[...])
        mo_ref[...] = mn
        vo_ref[...] = vn

    row = pl.BlockSpec((tr, c), lambda li, i: (li * nb + i, 0))
    part_specs = [pl.BlockSpec((N_DEV, tr, c), functools.partial(lambda li, i, k: (0, jnp.where(li == k, i, 0), 0), k=k))
                  for k in range(ng)]
    return pl.pallas_call(
        body, name=name, grid=(ng, nb), in_specs=[row] + part_specs + [row, row],
        out_specs=[row] * 4, out_shape=[jax.ShapeDtypeStruct((rr, c), F32)] * 4, compiler_params=_params(("arbitrary", "arbitrary")),
    )(w, *gparts, m, v)


PACK_LANES = 128
PACK_ROWS = 8192


PACK_TILE = 8 * PACK_LANES


def _pack_rows(t):
    return -(-t.size // PACK_TILE) * 8


def _pack(vals):
    rows = []
    for t in vals:
        flat = t.reshape(-1)
        n_rows = _pack_rows(t)
        rows.append(jnp.pad(flat, (0, n_rows * PACK_LANES - flat.size)).reshape(n_rows, PACK_LANES))
    used = sum(r.shape[0] for r in rows)
    assert used <= PACK_ROWS, used
    return jnp.concatenate(rows + [jnp.zeros((PACK_ROWS - used, PACK_LANES), F32)], axis=0)


def _unpack(packed, like):
    out, off = [], 0
    for t in like:
        n_rows = _pack_rows(t)
        out.append(packed[off:off + n_rows].reshape(-1)[:t.size].reshape(t.shape))
        off += n_rows
    return out


def kernel(x, w_in, s5_lambda_re, s5_lambda_im, s5_log_step, s5_b_re, s5_b_im, s5_c_re, s5_c_im, s5_d, s5_glu_w, s5_glu_b, sgu_norm_g, sgu_norm_b, sgu_w, sgu_b, pool_w, pool_scale, dn_conv_w, dn_a_log, dn_dt_bias, dn_norm_g, w_out, ln1_g, ln1_b, w_up, w_down, ln2_g, ln2_b, loss_target, m_w_in, m_s5_lambda_re, m_s5_lambda_im, m_s5_log_step, m_s5_b_re, m_s5_b_im, m_s5_c_re, m_s5_c_im, m_s5_d, m_s5_glu_w, m_s5_glu_b, m_sgu_norm_g, m_sgu_norm_b, m_sgu_w, m_sgu_b, m_pool_w, m_pool_scale, m_dn_conv_w, m_dn_a_log, m_dn_dt_bias, m_dn_norm_g, m_w_out, m_ln1_g, m_ln1_b, m_w_up, m_w_down, m_ln2_g, m_ln2_b, v_w_in, v_s5_lambda_re, v_s5_lambda_im, v_s5_log_step, v_s5_b_re, v_s5_b_im, v_s5_c_re, v_s5_c_im, v_s5_d, v_s5_glu_w, v_s5_glu_b, v_sgu_norm_g, v_sgu_norm_b, v_sgu_w, v_sgu_b, v_pool_w, v_pool_scale, v_dn_conv_w, v_dn_a_log, v_dn_dt_bias, v_dn_norm_g, v_w_out, v_ln1_g, v_ln1_b, v_w_up, v_w_down, v_ln2_g, v_ln2_b):
    names = ("w_in", "s5_lambda_re", "s5_lambda_im", "s5_log_step", "s5_b_re", "s5_b_im", "s5_c_re", "s5_c_im", "s5_d", "s5_glu_w",
             "s5_glu_b", "sgu_norm_g", "sgu_norm_b", "sgu_w", "sgu_b", "pool_w", "pool_scale", "dn_conv_w", "dn_a_log", "dn_dt_bias",
             "dn_norm_g", "w_out", "ln1_g", "ln1_b", "w_up", "w_down", "ln2_g", "ln2_b")
    env = locals()
    w = {n: env[n] for n in names}
    m = {n: env["m_" + n] for n in names}
    v = {n: env["v_" + n] for n in names}

    wire = [{n: (w[n][i] if n == "dn_conv_w" else w[n][i].astype(MXU_DTYPE)) for n in SHARDED} for i in range(DEPTH)]
    small = [{n: w[n][i] for n in SMALL} for i in range(DEPTH)]
    ops = [_prep_small(small[i]) for i in range(DEPTH)]
    grads = [{} for _ in range(DEPTH)]
    recv = [{} for _ in range(DEPTH)]
    first = ("w_in", "s5_glu_w", "dn_conv_w", "w_out")

    def gather(layer, group):
        def take(outs):
            for n, t in zip(group, outs):
                ops[layer].update(_weight_views(n, t))
        return (lambda: [(wire[layer][n], False) for n in group]), take

    def scatter(layer, group, with_small=False):
        def make():
            sends = [(grads[layer][n], True) for n in group]
            if with_small:
                sends.append((_pack([jnp.stack([grads[i]["small"][n] for i in range(DEPTH)]) for n in SMALL]), False))
            return sends
        def take(outs):
            recv[layer].update(dict(zip(group + (("small",) if with_small else ()), outs)))
        return make, take

    make, take = gather(0, first)
    take(_exchange(make(), "gather_first"))
    fwd_hosts = {"l0_proj": gather(0, ("w_up",)), "l0_up": gather(0, ("w_down",)), "l0_down": gather(1, first),
                 "l1_proj": gather(1, ("w_up",)), "l1_up": gather(1, ("w_down",))}
    late = ("w_in", "s5_glu_w", "dn_conv_w")
    bwd_hosts = {"l1b_dpre": scatter(1, ("w_down",)), "l1b_dx1": scatter(1, ("w_up",)), "l1b_dmixed": scatter(1, ("w_out",)),
                 "l0b_dw_down": scatter(1, late),
                 "l0b_dpre": scatter(0, ("w_down",)), "l0b_dx1": scatter(0, ("w_up",)), "l0b_dmixed": scatter(0, ("w_out",)),
                 "l0b_dw_main": scatter(0, ("s5_glu_w", "dn_conv_w"), with_small=True), "l0b_dx": scatter(0, ("w_in",))}
    loss, grad_x = _local_step(x[0], loss_target[0], ops, small, fwd_hosts, bwd_hosts, grads)

    g_out, d_out, m_out, v_out = {}, {}, {}, {}
    for n in SHARDED:
        shp = w[n].shape
        pad = (-shp[1]) % 8
        def rows(t):
            return jnp.pad(t, ((0, 0), (0, pad), (0, 0))).reshape(shp[0] * (shp[1] + pad), shp[2])
        res = _adamw(rows(w[n]), [recv[i][n] for i in range(DEPTH)], rows(m[n]), rows(v[n]), "adamw_" + n)
        g_out[n], d_out[n], m_out[n], v_out[n] = (t.reshape(shp[0], shp[1] + pad, shp[2])[:, :shp[1]] for t in res)
    like = [w[n] for n in SMALL]
    res = _adamw(_pack(like), [recv[0]["small"]], _pack([m[n] for n in SMALL]), _pack([v[n] for n in SMALL]), "adamw_small")
    for dst, pk in zip((g_out, d_out, m_out, v_out), res):
        dst.update(dict(zip(SMALL, _unpack(pk, like))))

    total = lax.psum(loss[0, 0], MESH_AXES)
    return (total, grad_x[None], *[g_out[n] for n in names], *[d_out[n] for n in names],
            *[m_out[n] for n in names], *[v_out[n] for n in names])
```

```python
import functools
import math

import jax
import jax.numpy as jnp
from jax import lax
from jax.experimental import pallas as pl
from jax.experimental.pallas import tpu as pltpu

F32 = jnp.float32
BF16 = jnp.bfloat16
MXU_DTYPE = jnp.bfloat16
HI = lax.Precision.HIGHEST

N_DEV = 8
D_MODEL = 2048
DEPTH = 2
GROUP_WIDTH = 512
S5_GROUPS, S5_CH, S5_STATE = 32, 16, 64
S5_NS = S5_GROUPS * S5_STATE
SGU_CHUNK, SGU_HEADS = 128, 8
POOL_WINDOWS = (2, 4, 8, 16)
DN_HEADS, DN_HEAD_DIM, DN_CONV, DN_CHUNK = 4, 128, 4, 64
D_FF = 4 * D_MODEL
LN_EPS, RMS_EPS, L2_EPS = 1e-5, 1e-6, 1e-6
ALPHA = (2 * DEPTH) ** 0.25
MAIN_COLS = 4096
AB_PAD = 128
ADAM_LR, ADAM_B1, ADAM_B2, ADAM_EPS, ADAM_WD, ADAM_STEP = 0.001, 0.9, 0.999, 1e-08, 0.01, 10
VMEM_LIMIT = 56 * 1024 * 1024
MIX_ROWS = 1024
SGU_ROWS = 512
WIDE_ROWS = 512
C_GELU = math.sqrt(2.0 / math.pi)


def _params(sem=None):
    return pltpu.CompilerParams(dimension_semantics=sem, vmem_limit_bytes=VMEM_LIMIT)


def _gelu(x):
    return 0.5 * x * (1.0 + jnp.tanh(C_GELU * (x + 0.044715 * x * x * x)))


def _gelu_grad(x):
    t = jnp.tanh(C_GELU * (x + 0.044715 * x * x * x))
    return 0.5 * (1.0 + t) + 0.5 * x * (1.0 - t * t) * C_GELU * (1.0 + 3.0 * 0.044715 * x * x)


def _sigmoid(x):
    return 1.0 / (1.0 + jnp.exp(-x))


def _silu(x):
    return x * _sigmoid(x)


def _silu_grad(x):
    s = _sigmoid(x)
    return s * (1.0 + x * (1.0 - s))


def _softplus(x):
    z = jnp.exp(-jnp.abs(x))
    small = z * (1.0 - z * (0.5 - z * (1.0 / 3.0)))
    return jnp.maximum(x, 0.0) + jnp.where(z < 1e-2, small, jnp.log(1.0 + z))


def _mx(x):
    return x.astype(MXU_DTYPE)


def _dot(a, b, dims="nn", precision=None):
    cd = {"nn": ((1,), (0,)), "nt": ((1,), (1,)), "tn": ((0,), (0,))}[dims]
    return lax.dot_general(a, b, (cd, ((), ())), preferred_element_type=F32, precision=precision)


def _mdot(a, b, dims="nn"):
    return _dot(_mx(a), _mx(b), dims)


MESH_AXES = ("x", "y", "c")
OFFSETS = [(dx, dy, dc) for dx in (0, 1) for dy in (0, 1) for dc in (0, 1)][1:]


def _me_and_peers():
    x, y, c = (lax.axis_index(a) for a in MESH_AXES)
    def flip(v, d):
        return 1 - v if d else v
    peers = [(flip(x, dx), flip(y, dy), flip(c, dc)) for dx, dy, dc in OFFSETS]
    def idx(p):
        return 4 * p[0] + 2 * p[1] + p[2]
    return idx((x, y, c)), peers, [idx(p) for p in peers]


SIBLING = OFFSETS.index((0, 0, 1))
SAME_CORE = [OFFSETS.index(f) for f in ((0, 1, 0), (1, 0, 0), (1, 1, 0))]


class _Comm:
    def __init__(self, ops):
        self.arrays = [a for a, _ in ops]
        self.scatter = [s for _, s in ops]
        self.n = n = len(ops)
        hbm = pl.BlockSpec(memory_space=pltpu.HBM)
        self.in_specs, self.out_specs = [hbm] * n, [hbm] * n
        self.out_shape = [jax.ShapeDtypeStruct(a.shape if s else (N_DEV,) + a.shape, a.dtype) for a, s in ops]
        npeer = len(OFFSETS)
        self.scratch = [pltpu.SemaphoreType.DMA((n, npeer)), pltpu.SemaphoreType.DMA((n, npeer)), pltpu.SemaphoreType.DMA((n,))]

    def _plan(self, ins, outs, sems, waiting):
        send_sems, recv_sems, local_sems = sems
        me, peers, peer_idx = _me_and_peers()

        def remote(k, d, src, dst, to):
            return pltpu.make_async_remote_copy(src_ref=src, dst_ref=dst, send_sem=send_sems.at[k, d], recv_sem=recv_sems.at[k, d],
                                                device_id=to, device_id_type=pl.DeviceIdType.MESH)
        plan = []
        for k in range(self.n):
            every = range(len(OFFSETS))
            if self.scatter[k]:
                local = pltpu.make_async_copy(ins[k].at[me], outs[k].at[me], local_sems.at[k])
                pushes = [remote(k, d, ins[k].at[peer_idx[d]], outs[k].at[me], peers[d]) for d in every]
                onward = []
            else:
                local = pltpu.make_async_copy(ins[k], outs[k].at[me], local_sems.at[k])
                pushes = [remote(k, d, ins[k], outs[k].at[me], peers[d]) for d in [SIBLING] + SAME_CORE]
                onward = SAME_CORE
            passed, arrivals = [], {}
            if waiting:
                passed = [(d, remote(k, d + 1, outs[k].at[peer_idx[d]], outs[k].at[peer_idx[d]], peers[SIBLING])) for d in onward]
                arrivals = {d: remote(k, d, outs[k].at[peer_idx[d]], outs[k].at[peer_idx[d]], peers[d]) for d in every}
            plan.append((local, pushes, passed, arrivals))
        return plan

    def start(self, ins, outs, sems):
        for local, pushes, _, _ in self._plan(ins, outs, sems, False):
            local.start()
            for cp in pushes:
                cp.start()

    def wait(self, ins, outs, sems):
        plan = self._plan(ins, outs, sems, True)
        for _, _, passed, arrivals in plan:
            for d, onward in passed:
                arrivals.pop(d).wait_recv()
                onward.start()
        for local, pushes, passed, arrivals in plan:
            for cp in arrivals.values():
                cp.wait_recv()
            for cp in pushes + [onward for _, onward in passed]:
                cp.wait_send()
            local.wait()


def _exchange(ops, name):
    cm = _Comm(ops)

    def body(*refs):
        ins, outs, sems = refs[:cm.n], refs[cm.n:2 * cm.n], refs[2 * cm.n:]
        cm.start(ins, outs, sems)
        cm.wait(ins, outs, sems)

    return pl.pallas_call(body, name=name, in_specs=cm.in_specs, out_specs=cm.out_specs, out_shape=cm.out_shape,
                          scratch_shapes=cm.scratch)(*cm.arrays)


def _matmul(a, b, *, mode, tm, tn, tk, out_dtype, name, a_fn=None, extras=(), epi=None, a_cols=None,
            b_slab=None, out_slab=None, comm=None):
    a_shape = a.shape if a_cols is None else (a.shape[0], a_cols)
    b_shape = b.shape if b_slab is None else (b.shape[1], N_DEV * b_slab)
    if mode == "nn":
        (m, k), n = a_shape, b_shape[1]
    elif mode == "nt":
        (m, k), n = a_shape, b_shape[0]
    else:
        (k, m), n = a_shape, b_shape[1]
    tm, tn, tk = min(tm, m), min(tn, n), min(tk, k)
    if b_slab is not None:
        tn, tk = (tn, min(tk, b_slab)) if mode == "nt" else (min(tn, b_slab), tk)
    assert m % tm == 0 and n % tn == 0 and k % tk == 0, (name, a.shape, b.shape, tm, tn, tk)
    gi, gj, nk = m // tm, n // tn, k // tk
    n_ex = len(extras)
    cm = _Comm(comm) if comm else None
    nc = cm.n if cm else 0

    def body(a_ref, b_ref, *rest):
        ex_refs, rest = rest[:n_ex], rest[n_ex:]
        c_ins, o_ref, c_outs, acc, sems = rest[:nc], rest[nc], rest[nc + 1:2 * nc + 1], rest[2 * nc + 1], rest[2 * nc + 2:]
        i, j, kk = pl.program_id(0), pl.program_id(1), pl.program_id(2)
        if cm:
            @pl.when((i == 0) & (j == 0) & (kk == 0))
            def _():
                cm.start(c_ins, c_outs, sems)

        av = a_ref[...]
        if a_fn is not None:
            av = a_fn(av)
        part = _dot(_mx(av), _mx(b_ref[...]), mode)

        def finish(r):
            if epi is not None:
                r = epi(r, *[e[...] for e in ex_refs])
            o_ref[...] = r.astype(out_dtype)

        if nk == 1:
            finish(part)
        else:
            @pl.when(kk == 0)
            def _():
                acc[...] = part

            @pl.when((kk > 0) & (kk < nk - 1))
            def _():
                acc[...] += part

            @pl.when(kk == nk - 1)
            def _():
                finish(acc[...] + part)

        if cm:
            @pl.when((i == gi - 1) & (j == gj - 1) & (kk == nk - 1))
            def _():
                cm.wait(c_ins, c_outs, sems)

    a_spec = pl.BlockSpec((tk, tm), lambda i, j, kk: (kk, i)) if mode == "tn" else pl.BlockSpec((tm, tk), lambda i, j, kk: (i, kk))
    if b_slab is None:
        b_spec = pl.BlockSpec((tn, tk), lambda i, j, kk: (j, kk)) if mode == "nt" else pl.BlockSpec((tk, tn), lambda i, j, kk: (kk, j))
    elif mode == "nt":
        assert b_slab % tk == 0
        b_spec = pl.BlockSpec((None, tn, tk), lambda i, j, kk: ((kk * tk) // b_slab, j, ((kk * tk) % b_slab) // tk))
    else:
        assert b_slab % tn == 0
        b_spec = pl.BlockSpec((None, tk, tn), lambda i, j, kk: ((j * tn) // b_slab, kk, ((j * tn) % b_slab) // tn))
    if out_slab is None:
        o_spec, o_shape = pl.BlockSpec((tm, tn), lambda i, j, kk: (i, j)), jax.ShapeDtypeStruct((m, n), out_dtype)
    else:
        assert out_slab % tn == 0 and n == N_DEV * out_slab
        o_spec = pl.BlockSpec((None, tm, tn), lambda i, j, kk: ((j * tn) // out_slab, i, ((j * tn) % out_slab) // tn))
        o_shape = jax.ShapeDtypeStruct((N_DEV, m, out_slab), out_dtype)
    ex_specs = [pl.BlockSpec(({None: tm, "tn": tn}.get(bs[0], bs[0]), tn if bs[1] is None else bs[1]),
                             functools.partial(lambda i, j, kk, f: f(i, j), f=im)) for (_, bs, im) in extras]
    res = pl.pallas_call(
        body,
        name=name,
        grid=(gi, gj, nk),
        in_specs=[a_spec, b_spec, *ex_specs] + (cm.in_specs if cm else []),
        out_specs=[o_spec] + (cm.out_specs if cm else []),
        out_shape=[o_shape] + (cm.out_shape if cm else []),
        scratch_shapes=[pltpu.VMEM((tm, tn) if nk > 1 else (8, 128), F32)] + (cm.scratch if cm else []),
        compiler_params=_params(("arbitrary",) * 3 if cm else ("parallel", "parallel", "arbitrary")),
    )(a, b, *[e[0] for e in extras], *(cm.arrays if cm else []))
    return (res[0], res[1:]) if cm else res[0]


def _ln_fwd(x, y, g, b, name):
    l, d = x.shape
    tl = min(WIDE_ROWS, l)

    def body(x_ref, y_ref, g_ref, b_ref, h_ref, o_ref, om_ref):
        h = ALPHA * x_ref[...] + y_ref[...]
        mu = jnp.mean(h, axis=-1, keepdims=True)
        c = h - mu
        var = jnp.mean(c * c, axis=-1, keepdims=True)
        h_ref[...] = h
        out = c * lax.rsqrt(var + LN_EPS) * g_ref[...] + b_ref[...]
        o_ref[...] = out
        om_ref[...] = out.astype(om_ref.dtype)

    row = pl.BlockSpec((tl, d), lambda i: (i, 0))
    vec = pl.BlockSpec((1, d), lambda i: (0, 0))
    return pl.pallas_call(
        body, name=name, grid=(l // tl,), in_specs=[row, row, vec, vec], out_specs=[row, row, row],
        out_shape=[jax.ShapeDtypeStruct((l, d), F32)] * 2 + [jax.ShapeDtypeStruct((l, d), MXU_DTYPE)],
        compiler_params=_params(("parallel",)),
    )(x, y, g, b)


def _ln_bwd(dout, h, g, name):
    l, d = h.shape
    tl = min(WIDE_ROWS, l)

    def body(do_ref, h_ref, g_ref, dh_ref, dhm_ref, dg_ref, db_ref):
        @pl.when(pl.program_id(0) == 0)
        def _():
            dg_ref[...] = jnp.zeros_like(dg_ref)
            db_ref[...] = jnp.zeros_like(db_ref)

        hv, do = h_ref[...], do_ref[...]
        mu = jnp.mean(hv, axis=-1, keepdims=True)
        c = hv - mu
        r = lax.rsqrt(jnp.mean(c * c, axis=-1, keepdims=True) + LN_EPS)
        xh = c * r
        dxh = do * g_ref[...]
        m1 = jnp.mean(dxh, axis=-1, keepdims=True)
        m2 = jnp.mean(dxh * xh, axis=-1, keepdims=True)
        dh = r * (dxh - m1 - xh * m2)
        dh_ref[...] = dh
        dhm_ref[...] = dh.astype(dhm_ref.dtype)
        dg_ref[...] += jnp.sum(do * xh, axis=0, keepdims=True)
        db_ref[...] += jnp.sum(do, axis=0, keepdims=True)

    row = pl.BlockSpec((tl, d), lambda i: (i, 0))
    vec = pl.BlockSpec((1, d), lambda i: (0, 0))
    return pl.pallas_call(
        body, name=name, grid=(l // tl,), in_specs=[row, row, vec], out_specs=[row, row, vec, vec],
        out_shape=[jax.ShapeDtypeStruct((l, d), F32), jax.ShapeDtypeStruct((l, d), MXU_DTYPE),
                   jax.ShapeDtypeStruct((1, d), F32), jax.ShapeDtypeStruct((1, d), F32)],
        compiler_params=_params(("arbitrary",)),
    )(dout, h, g)


def _loss_head(y, target):
    l, d = y.shape
    tl = min(WIDE_ROWS, l)

    def body(y_ref, t_ref, loss_ref, dy_ref):
        @pl.when(pl.program_id(0) == 0)
        def _():
            loss_ref[...] = jnp.zeros_like(loss_ref)

        e = y_ref[...] - t_ref[...]
        dy_ref[...] = e * (1.0 / d)
        s = jnp.sum(jnp.sum(e * e, axis=1, keepdims=True), axis=0, keepdims=True)
        loss_ref[...] += s * (0.5 / d)

    row = pl.BlockSpec((tl, d), lambda i: (i, 0))
    return pl.pallas_call(
        body, name="loss_head", grid=(l // tl,), in_specs=[row, row],
        out_specs=[pl.BlockSpec((1, 1), lambda i: (0, 0)), row],
        out_shape=[jax.ShapeDtypeStruct((1, 1), F32), jax.ShapeDtypeStruct((l, d), F32)],
        compiler_params=_params(("arbitrary",)),
    )(y, target)


def _s5_discretize(lam_re, lam_im, log_step, b_re, b_im):
    step = jnp.exp(log_step)[:, None]
    e = jnp.exp(lam_re * step)
    lbr, lbi = e * jnp.cos(lam_im * step), e * jnp.sin(lam_im * step)
    den = lam_re * lam_re + lam_im * lam_im
    qr = ((lbr - 1.0) * lam_re + lbi * lam_im) / den
    qi = (lbi * lam_re - (lbr - 1.0) * lam_im) / den
    bbr = qr[:, :, None] * b_re - qi[:, :, None] * b_im
    bbi = qr[:, :, None] * b_im + qi[:, :, None] * b_re
    return lbr, lbi, bbr, bbi


S5_TILES, S5_SLABS = 4, 8
S5_TILE_W, S5_SLAB_W = GROUP_WIDTH // S5_TILES, S5_NS // S5_TILES
S5_GPT = S5_GROUPS // S5_TILES


def _s5_compact(bbr, bbi, c_re, c_im):
    eye = jnp.eye(S5_GPT, dtype=F32)
    def bd(t):
        return jnp.einsum("tgph,gk->tghkp", t.reshape(S5_TILES, S5_GPT, S5_STATE, S5_CH), eye).reshape(S5_TILES, S5_TILE_W, S5_SLAB_W)
    def cd(t):
        return jnp.einsum("tghp,gk->tgpkh", t.reshape(S5_TILES, S5_GPT, S5_CH, S5_STATE), eye).reshape(S5_TILES, S5_SLAB_W, S5_TILE_W)
    return jnp.concatenate([bd(bbr), bd(bbi)], axis=0), jnp.concatenate([cd(c_re), -cd(c_im)], axis=0)


def _s5_uncompact_b(db):
    eye = jnp.eye(S5_GPT, dtype=F32)[None, :, None, :, None]
    def ex(t):
        d = jnp.sum(t.reshape(S5_TILES, S5_GPT, S5_CH, S5_GPT, S5_STATE) * eye, axis=3)
        return jnp.transpose(d, (0, 1, 3, 2)).reshape(S5_GROUPS, S5_STATE, S5_CH)
    return ex(db[:S5_TILES]), ex(db[S5_TILES:])


def _s5_uncompact_c(dc):
    eye = jnp.eye(S5_GPT, dtype=F32)[None, :, None, :, None]
    def ex(t):
        d = jnp.sum(t.reshape(S5_TILES, S5_GPT, S5_STATE, S5_GPT, S5_CH) * eye, axis=3)
        return jnp.transpose(d, (0, 1, 3, 2)).reshape(S5_GROUPS, S5_CH, S5_STATE)
    return ex(dc[:S5_TILES]), -ex(dc[S5_TILES:])


S5_ROWS = 512


def _s5_tile(j):
    t = j % S5_TILES
    return slice(t * S5_TILE_W, (t + 1) * S5_TILE_W)


def _s5_slab(j):
    return slice(j * S5_SLAB_W, (j + 1) * S5_SLAB_W)


def _s5_recur(src, lam_ref, carry, emit, n_rows, reverse, extra=()):
    ns = S5_NS
    lr, li = lam_ref[:, :ns], lam_ref[:, ns:]

    def step(t, c):
        row = (n_rows - 1 - t) if reverse else t
        cr, ci = c[0], c[1]
        nr = lr * cr - li * ci + src[pl.ds(row, 1), :ns]
        ni = lr * ci + li * cr + src[pl.ds(row, 1), ns:]
        return (nr, ni) + tuple(emit(row, nr, ni, cr, ci, c[2:]))

    fin = lax.fori_loop(0, n_rows, step, (carry[:, :ns], carry[:, ns:]) + tuple(extra))
    carry[:, :ns] = fin[0]
    carry[:, ns:] = fin[1]
    return fin[2:]


def _s5_fwd(proj, b, c, lam, d, name):
    l = proj.shape[0]
    tl = min(S5_ROWS, l)
    w = 2 * S5_NS

    def body(u_ref, b_ref, c_ref, lam_ref, d_ref, hs_ref, y_ref, bu, carry):
        @pl.when(pl.program_id(0) == 0)
        def _():
            carry[...] = jnp.zeros_like(carry)

        u = u_ref[...]
        um = _mx(u)
        for j in range(S5_SLABS):
            bu[:, _s5_slab(j)] = _dot(um[:, _s5_tile(j)], b_ref[j])

        def emit(row, nr, ni, cr, ci, extra):
            hs_ref[pl.ds(row, 1), :S5_NS] = nr
            hs_ref[pl.ds(row, 1), S5_NS:] = ni
            return extra

        _s5_recur(bu, lam_ref, carry, emit, tl, False)
        for t in range(S5_TILES):
            acc = _dot(_mx(hs_ref[:, _s5_slab(t)]), c_ref[t]) + _dot(_mx(hs_ref[:, _s5_slab(S5_TILES + t)]), c_ref[S5_TILES + t])
            y_ref[:, _s5_tile(t)] = acc + d_ref[:, _s5_tile(t)] * u[:, _s5_tile(t)]

    row = lambda width: pl.BlockSpec((tl, width), lambda i: (i, 0))
    full = lambda a: pl.BlockSpec(a.shape, lambda i: (0,) * a.ndim)
    return pl.pallas_call(
        body, name=name, grid=(l // tl,), in_specs=[row(GROUP_WIDTH), full(b), full(c), full(lam), full(d)],
        out_specs=[row(w), row(GROUP_WIDTH)],
        out_shape=[jax.ShapeDtypeStruct((l, w), F32), jax.ShapeDtypeStruct((l, GROUP_WIDTH), F32)],
        scratch_shapes=[pltpu.VMEM((tl, w), F32), pltpu.VMEM((1, w), F32)], compiler_params=_params(("arbitrary",)),
    )(proj, b, c, lam, d)


def _s5_bwd(dy, hs, proj, b, c, lam_conj, d, name):
    l = dy.shape[0]
    tl = min(S5_ROWS, l)
    nb = l // tl
    w = 2 * S5_NS

    def body(dy_ref, hs_ref, u_ref, b_ref, c_ref, lam_ref, d_ref, du_ref, db_ref, dc_ref, dl_ref, dh, adj, carry):
        @pl.when(pl.program_id(0) == 0)
        def _():
            carry[...] = jnp.zeros_like(carry)
            db_ref[...] = jnp.zeros_like(db_ref)
            dc_ref[...] = jnp.zeros_like(dc_ref)
            dl_ref[...] = jnp.zeros_like(dl_ref)

        dyv = dy_ref[...]
        dym, um = _mx(dyv), _mx(u_ref[...])
        for j in range(S5_SLABS):
            dh[:, _s5_slab(j)] = _dot(dym[:, _s5_tile(j)], c_ref[j], "nt")

        def emit(row, nr, ni, cr, ci, extra):
            adj[pl.ds(row, 1), :S5_NS] = nr
            adj[pl.ds(row, 1), S5_NS:] = ni
            hr, hi = hs_ref[pl.ds(row, 1), :S5_NS], hs_ref[pl.ds(row, 1), S5_NS:]
            return extra[0] + cr * hr + ci * hi, extra[1] + ci * hr - cr * hi

        dl = _s5_recur(dh, lam_ref, carry, emit, tl, True, extra=(dl_ref[:, :S5_NS], dl_ref[:, S5_NS:]))
        dl_ref[:, :S5_NS] = dl[0]
        dl_ref[:, S5_NS:] = dl[1]
        for t in range(S5_TILES):
            acc = (_dot(_mx(adj[:, _s5_slab(t)]), b_ref[t], "nt")
                   + _dot(_mx(adj[:, _s5_slab(S5_TILES + t)]), b_ref[S5_TILES + t], "nt"))
            du_ref[:, _s5_tile(t)] = (acc + d_ref[:, _s5_tile(t)] * dyv[:, _s5_tile(t)]).astype(du_ref.dtype)
        for j in range(S5_SLABS):
            dc_ref[j] += _dot(_mx(hs_ref[:, _s5_slab(j)]), dym[:, _s5_tile(j)], "tn")
            db_ref[j] += _dot(um[:, _s5_tile(j)], _mx(adj[:, _s5_slab(j)]), "tn")

    row = lambda width: pl.BlockSpec((tl, width), lambda i: (nb - 1 - i, 0))
    full = lambda a: pl.BlockSpec(a.shape, lambda i: (0,) * a.ndim)
    acc3 = lambda shape: pl.BlockSpec(shape, lambda i: (0, 0, 0))
    return pl.pallas_call(
        body, name=name, grid=(nb,),
        in_specs=[row(GROUP_WIDTH), row(w), row(GROUP_WIDTH), full(b), full(c), full(lam_conj), full(d)],
        out_specs=[row(GROUP_WIDTH), acc3(b.shape), acc3(c.shape), pl.BlockSpec((1, w), lambda i: (0, 0))],
        out_shape=[jax.ShapeDtypeStruct((l, GROUP_WIDTH), BF16), jax.ShapeDtypeStruct(b.shape, F32),
                   jax.ShapeDtypeStruct(c.shape, F32), jax.ShapeDtypeStruct((1, w), F32)],
        scratch_shapes=[pltpu.VMEM((tl, w), F32), pltpu.VMEM((tl, w), F32), pltpu.VMEM((1, w), F32)],
        compiler_params=_params(("arbitrary",)),
    )(dy, hs, proj, b, c, lam_conj, d)


def _s5_glu_fwd(y, glu_w, glu_b, name):
    l, d = y.shape
    tl = min(MIX_ROWS, l)

    def body(y_ref, w_ref, b_ref, o_ref):
        yg = _gelu(y_ref[...])
        z = _mdot(yg, w_ref[...]) + b_ref[...]
        o_ref[...] = (yg * _sigmoid(z)).astype(o_ref.dtype)

    return pl.pallas_call(
        body, name=name, grid=(l // tl,),
        in_specs=[pl.BlockSpec((tl, d), lambda i: (i, 0)), pl.BlockSpec((d, d), lambda i: (0, 0)), pl.BlockSpec((1, d), lambda i: (0, 0))],
        out_specs=pl.BlockSpec((tl, d), lambda i: (i, 0)), out_shape=jax.ShapeDtypeStruct((l, d), BF16),
        compiler_params=_params(("parallel",)),
    )(y, glu_w, glu_b)


def _s5_glu_bwd(dmixed, y, proj, glu_w, glu_b, name):
    l, d = y.shape
    tl = min(MIX_ROWS, l)

    def body(do_ref, y_ref, u_ref, w_ref, b_ref, dy_ref, dz_ref, yg_ref, db_ref, dd_ref):
        @pl.when(pl.program_id(0) == 0)
        def _():
            db_ref[...] = jnp.zeros_like(db_ref)
            dd_ref[...] = jnp.zeros_like(dd_ref)

        yv, do = y_ref[...], do_ref[...]
        yg = _gelu(yv)
        gate = _sigmoid(_mdot(yg, w_ref[...]) + b_ref[...])
        dz = do * yg * gate * (1.0 - gate)
        dyg = do * gate + _mdot(dz, w_ref[...], "nt")
        dy = dyg * _gelu_grad(yv)
        dy_ref[...] = dy
        dz_ref[...] = dz.astype(dz_ref.dtype)
        yg_ref[...] = yg.astype(yg_ref.dtype)
        db_ref[...] += jnp.sum(dz, axis=0, keepdims=True)
        dd_ref[...] += jnp.sum(dy * u_ref[...], axis=0, keepdims=True)

    row = pl.BlockSpec((tl, d), lambda i: (i, 0))
    vec = pl.BlockSpec((1, d), lambda i: (0, 0))
    return pl.pallas_call(
        body, name=name, grid=(l // tl,),
        in_specs=[row, row, row, pl.BlockSpec((d, d), lambda i: (0, 0)), vec],
        out_specs=[row, row, row, vec, vec],
        out_shape=[jax.ShapeDtypeStruct((l, d), F32), jax.ShapeDtypeStruct((l, d), BF16), jax.ShapeDtypeStruct((l, d), BF16),
                   jax.ShapeDtypeStruct((1, d), F32), jax.ShapeDtypeStruct((1, d), F32)],
        compiler_params=_params(("arbitrary",)),
    )(dmixed, y, proj, glu_w, glu_b)


def _sgu_pair(w_ref, x, j, dims):
    lo = lax.broadcasted_iota(jnp.int32, x.shape, 1) < (GROUP_WIDTH // SGU_HEADS)
    xb = _mx(x)
    r0 = _dot(w_ref[2 * j], xb, dims)
    r1 = _dot(w_ref[2 * j + 1], xb, dims)
    return jnp.where(lo, r0, r1)


def _sgu_norm(v, g, b):
    mu = jnp.mean(v, axis=-1, keepdims=True)
    c = v - mu
    r = lax.rsqrt(jnp.mean(c * c, axis=-1, keepdims=True) + LN_EPS)
    return c * r, r


def _sgu_fwd(proj, norm_g, norm_b, wm, bfull, name):
    l = proj.shape[0]
    tl = min(SGU_ROWS, l)
    gw = GROUP_WIDTH

    def body(zu_ref, zv_ref, g_ref, b_ref, w_ref, bf_ref, o_ref):
        for c in range(tl // SGU_CHUNK):
            rows = slice(c * SGU_CHUNK, (c + 1) * SGU_CHUNK)
            u = _gelu(zu_ref[rows, :])
            vh, _ = _sgu_norm(_gelu(zv_ref[rows, :]), None, None)
            vn = vh * g_ref[...] + b_ref[...]
            for j in range(gw // 128):
                cols = slice(j * 128, (j + 1) * 128)
                mixed = _sgu_pair(w_ref, vn[:, cols], j, "nn") + bf_ref[:, cols]
                o_ref[rows, cols] = (u[:, cols] * mixed).astype(o_ref.dtype)

    vec = pl.BlockSpec((1, gw), lambda i: (0, 0))
    return pl.pallas_call(
        body, name=name, grid=(l // tl,),
        in_specs=[pl.BlockSpec((tl, gw), lambda i: (i, 1)), pl.BlockSpec((tl, gw), lambda i: (i, 2)), vec, vec,
                  pl.BlockSpec((SGU_HEADS, SGU_CHUNK, SGU_CHUNK), lambda i: (0, 0, 0)), pl.BlockSpec((SGU_CHUNK, gw), lambda i: (0, 0))],
        out_specs=pl.BlockSpec((tl, gw), lambda i: (i, 0)), out_shape=jax.ShapeDtypeStruct((l, gw), BF16),
        compiler_params=_params(("parallel",)),
    )(proj, proj, norm_g, norm_b, wm, bfull)


def _sgu_bwd(dmixed, proj, norm_g, norm_b, wm, bfull, name):
    l = proj.shape[0]
    tl = min(SGU_ROWS, l)
    gw = GROUP_WIDTH
    hd = gw // SGU_HEADS

    def body(do_ref, zu_ref, zv_ref, g_ref, b_ref, w_ref, bf_ref, dzu_ref, dzv_ref, dw_ref, dbf_ref, dg_ref, dnb_ref):
        @pl.when(pl.program_id(0) == 0)
        def _():
            dw_ref[...] = jnp.zeros_like(dw_ref)
            dbf_ref[...] = jnp.zeros_like(dbf_ref)
            dg_ref[...] = jnp.zeros_like(dg_ref)
            dnb_ref[...] = jnp.zeros_like(dnb_ref)

        for c in range(tl // SGU_CHUNK):
            rows = slice(c * SGU_CHUNK, (c + 1) * SGU_CHUNK)
            zu, zv, do = zu_ref[rows, :], zv_ref[rows, :], do_ref[rows, :]
            u = _gelu(zu)
            vh, r = _sgu_norm(_gelu(zv), None, None)
            vn = vh * g_ref[...] + b_ref[...]
            dvn_parts, mixed_parts = [], []
            for j in range(gw // 128):
                cols = slice(j * 128, (j + 1) * 128)
                vb = vn[:, cols]
                mixed_parts.append(_sgu_pair(w_ref, vb, j, "nn") + bf_ref[:, cols])
                dm = do[:, cols] * u[:, cols]
                dvn_parts.append(_sgu_pair(w_ref, dm, j, "tn"))
                lo = lax.broadcasted_iota(jnp.int32, dm.shape, 1) < hd
                dw_ref[2 * j] += _mdot(jnp.where(lo, dm, 0.0), vb, "nt")
                dw_ref[2 * j + 1] += _mdot(jnp.where(lo, 0.0, dm), vb, "nt")
                dbf_ref[:, cols] += dm
            mixed = jnp.concatenate(mixed_parts, axis=1)
            dvn = jnp.concatenate(dvn_parts, axis=1)
            dzu_ref[rows, :] = (do * mixed * _gelu_grad(zu)).astype(dzu_ref.dtype)
            dg_ref[...] += jnp.sum(dvn * vh, axis=0, keepdims=True)
            dnb_ref[...] += jnp.sum(dvn, axis=0, keepdims=True)
            dvh = dvn * g_ref[...]
            m1 = jnp.mean(dvh, axis=-1, keepdims=True)
            m2 = jnp.mean(dvh * vh, axis=-1, keepdims=True)
            dv = r * (dvh - m1 - vh * m2)
            dzv_ref[rows, :] = (dv * _gelu_grad(zv)).astype(dzv_ref.dtype)

    vec = pl.BlockSpec((1, gw), lambda i: (0, 0))
    row = pl.BlockSpec((tl, gw), lambda i: (i, 0))
    wspec = pl.BlockSpec((SGU_HEADS, SGU_CHUNK, SGU_CHUNK), lambda i: (0, 0, 0))
    bspec = pl.BlockSpec((SGU_CHUNK, gw), lambda i: (0, 0))
    return pl.pallas_call(
        body, name=name, grid=(l // tl,),
        in_specs=[pl.BlockSpec((tl, gw), lambda i: (i, 1)), pl.BlockSpec((tl, gw), lambda i: (i, 1)), pl.BlockSpec((tl, gw), lambda i: (i, 2)),
                  vec, vec, wspec, bspec],
        out_specs=[row, row, wspec, bspec, vec, vec],
        out_shape=[jax.ShapeDtypeStruct((l, gw), BF16), jax.ShapeDtypeStruct((l, gw), BF16),
                   jax.ShapeDtypeStruct((SGU_HEADS, SGU_CHUNK, SGU_CHUNK), F32), jax.ShapeDtypeStruct((SGU_CHUNK, gw), F32),
                   jax.ShapeDtypeStruct((1, gw), F32), jax.ShapeDtypeStruct((1, gw), F32)],
        compiler_params=_params(("arbitrary",)),
    )(dmixed, proj, proj, norm_g, norm_b, wm, bfull)


HALO = 16


def _window_sums(ext, n_rows, forward):
    def sh(x, k):
        return pltpu.roll(x, (n_rows - k) if forward else k, axis=0)
    s2 = ext + sh(ext, 1)
    s4 = s2 + sh(s2, 2)
    s8 = s4 + sh(s4, 4)
    s16 = s8 + sh(s8, 8)
    return (s2, s4, s8, s16)


def _pool_fwd(proj, pool_w, scale, name):
    l = proj.shape[0]
    tl = min(MIX_ROWS, l)
    gw = GROUP_WIDTH
    pg = gw // len(POOL_WINDOWS)

    def body(x_ref, halo_ref, w_ref, s_ref, o_ref, p_ref):
        i = pl.program_id(0)
        x = x_ref[...]
        halo = jnp.where(i > 0, halo_ref[...], 0.0)
        ext = jnp.concatenate([halo, x], axis=0)
        sums = _window_sums(ext, tl + HALO, False)
        t = i * tl + lax.broadcasted_iota(jnp.int32, (tl, pg), 0)
        for gi, win in enumerate(POOL_WINDOWS):
            cols = slice(gi * pg, (gi + 1) * pg)
            cnt = jnp.minimum(t + 1, win).astype(F32)
            pooled = sums[gi][HALO:, cols] / cnt - x[:, cols]
            p_ref[:, cols] = pooled
            o_ref[:, cols] = (_mdot(pooled, w_ref[gi]) * s_ref[:, cols]).astype(o_ref.dtype)

    row = pl.BlockSpec((tl, gw), lambda i: (i, 0))
    return pl.pallas_call(
        body, name=name, grid=(l // tl,),
        in_specs=[pl.BlockSpec((tl, gw), lambda i: (i, 3)),
                  pl.BlockSpec((HALO, gw), lambda i: (jnp.maximum(i * (tl // HALO) - 1, 0), 3)),
                  pl.BlockSpec((len(POOL_WINDOWS), pg, pg), lambda i: (0, 0, 0)), pl.BlockSpec((1, gw), lambda i: (0, 0))],
        out_specs=[row, row], out_shape=[jax.ShapeDtypeStruct((l, gw), BF16), jax.ShapeDtypeStruct((l, gw), F32)],
        compiler_params=_params(("parallel",)),
    )(proj, proj, pool_w, scale)


def _pool_bwd_map(dmixed, pooled, pool_w, scale, name):
    l, gw = pooled.shape
    tl = min(MIX_ROWS, l)
    ng = len(POOL_WINDOWS)
    pg = gw // ng

    def body(do_ref, p_ref, w_ref, s_ref, dp_ref, dw_ref, ds_ref):
        @pl.when(pl.program_id(0) == 0)
        def _():
            dw_ref[...] = jnp.zeros_like(dw_ref)
            ds_ref[...] = jnp.zeros_like(ds_ref)

        for gi in range(ng):
            cols = slice(gi * pg, (gi + 1) * pg)
            do, pooled_g = do_ref[:, cols], p_ref[:, cols]
            mixed = _mdot(pooled_g, w_ref[gi])
            ds_ref[:, cols] += jnp.sum(do * mixed, axis=0, keepdims=True)
            dm = do * s_ref[:, cols]
            dw_ref[gi] += _mdot(pooled_g, dm, "tn")
            dp_ref[:, cols] = _mdot(dm, w_ref[gi], "nt")

    row = pl.BlockSpec((tl, gw), lambda i: (i, 0))
    wspec = pl.BlockSpec((ng, pg, pg), lambda i: (0, 0, 0))
    vec = pl.BlockSpec((1, gw), lambda i: (0, 0))
    return pl.pallas_call(
        body, name=name, grid=(l // tl,),
        in_specs=[pl.BlockSpec((tl, gw), lambda i: (i, 2)), row, wspec, vec], out_specs=[row, wspec, vec],
        out_shape=[jax.ShapeDtypeStruct((l, gw), F32), jax.ShapeDtypeStruct((ng, pg, pg), F32), jax.ShapeDtypeStruct((1, gw), F32)],
        compiler_params=_params(("arbitrary",)),
    )(dmixed, pooled, pool_w, scale)


def _pool_bwd_window(dpooled, name):
    l, gw = dpooled.shape
    tl = min(MIX_ROWS, l)
    nb = l // tl
    pg = gw // len(POOL_WINDOWS)

    def body(d_ref, halo_ref, o_ref):
        i = pl.program_id(0)
        d = d_ref[...]
        halo = jnp.where(i < nb - 1, halo_ref[...], 0.0)
        ext = jnp.concatenate([d, halo], axis=0)
        t = i * tl + lax.broadcasted_iota(jnp.int32, (tl + HALO, pg), 0)
        for gi, win in enumerate(POOL_WINDOWS):
            cols = slice(gi * pg, (gi + 1) * pg)
            cnt = jnp.minimum(t + 1, win).astype(F32)
            sums = _window_sums(ext[:, cols] / cnt, tl + HALO, True)
            o_ref[:, cols] = (sums[gi][:tl, :] - d[:, cols]).astype(o_ref.dtype)

    row = pl.BlockSpec((tl, gw), lambda i: (i, 0))
    return pl.pallas_call(
        body, name=name, grid=(nb,),
        in_specs=[row, pl.BlockSpec((HALO, gw), lambda i: (jnp.minimum((i + 1) * (tl // HALO), l // HALO - 1), 0))],
        out_specs=row, out_shape=jax.ShapeDtypeStruct((l, gw), BF16), compiler_params=_params(("parallel",)),
    )(dpooled, dpooled)


CONV_HALO = 8
QKV_BLK = 4


def _head_sums(x):
    parts = []
    for hd in range(DN_HEADS):
        s = jnp.sum(x[:, hd * DN_HEAD_DIM:(hd + 1) * DN_HEAD_DIM], axis=-1, keepdims=True)
        parts.append(jnp.broadcast_to(s, (x.shape[0], DN_HEAD_DIM)))
    return jnp.concatenate(parts, axis=1)


def _gdn_pre_fwd(proj, proj_ab, conv_w, a_log, dt_bias, name):
    l = proj.shape[0]
    tl = min(MIX_ROWS, l)
    gw = GROUP_WIDTH

    def body(xq, xk, xv, hq, hk, hv, w_ref, ab_ref, al_ref, dt_ref, qn_ref, kn_ref, v_ref, cq_ref, ck_ref, cv_ref, gb_ref):
        i = pl.program_id(0)
        for p, (x_ref, h_ref, c_ref) in enumerate(((xq, hq, cq_ref), (xk, hk, ck_ref), (xv, hv, cv_ref))):
            ext = jnp.concatenate([jnp.where(i > 0, h_ref[...], 0.0), x_ref[...]], axis=0)
            conv = jnp.zeros((tl, gw), F32)
            for j in range(DN_CONV):
                k = DN_CONV - 1 - j
                shifted = ext if k == 0 else pltpu.roll(ext, k, axis=0)
                conv = conv + shifted[CONV_HALO:, :] * w_ref[j:j + 1, p * gw:(p + 1) * gw]
            c_ref[...] = conv
            s = _silu(conv)
            if p == 2:
                v_ref[...] = s
            else:
                r = lax.rsqrt(_head_sums(s * s) + L2_EPS)
                (qn_ref if p == 0 else kn_ref)[...] = s * r * (DN_HEAD_DIM ** -0.5 if p == 0 else 1.0)
        ab = ab_ref[...]
        lane = lax.broadcasted_iota(jnp.int32, ab.shape, 1)
        g = -jnp.exp(al_ref[...]) * _softplus(ab + dt_ref[...])
        gb_ref[...] = jnp.where(lane < DN_HEADS, g, _sigmoid(ab))

    def xs(b):
        return pl.BlockSpec((tl, gw), lambda i: (i, b))

    def hs(b):
        return pl.BlockSpec((CONV_HALO, gw), lambda i: (jnp.maximum(i * (tl // CONV_HALO) - 1, 0), b))

    row = pl.BlockSpec((tl, gw), lambda i: (i, 0))
    abrow = pl.BlockSpec((tl, AB_PAD), lambda i: (i, 0))
    abvec = pl.BlockSpec((1, AB_PAD), lambda i: (0, 0))
    return pl.pallas_call(
        body, name=name, grid=(l // tl,),
        in_specs=[xs(QKV_BLK), xs(QKV_BLK + 1), xs(QKV_BLK + 2), hs(QKV_BLK), hs(QKV_BLK + 1), hs(QKV_BLK + 2),
                  pl.BlockSpec((DN_CONV, 3 * gw), lambda i: (0, 0)), abrow, abvec, abvec],
        out_specs=[row] * 6 + [abrow],
        out_shape=[jax.ShapeDtypeStruct((l, gw), F32)] * 6 + [jax.ShapeDtypeStruct((l, AB_PAD), F32)],
        compiler_params=_params(("parallel",)),
    )(proj, proj, proj, proj, proj, proj, conv_w, proj_ab, a_log, dt_bias)


def _gdn_pre_bwd(dq, dk, dv, cq, ck, cv, dgb, gb, proj_ab, a_log, dt_bias, name):
    l, gw = cq.shape
    tl = min(MIX_ROWS, l)

    def body(dq_ref, dk_ref, dv_ref, cq_ref, ck_ref, cv_ref, dgb_ref, gb_ref, ab_ref, al_ref, dt_ref,
             dcq_ref, dck_ref, dcv_ref, dab_ref, dal_ref, ddt_ref):
        @pl.when(pl.program_id(0) == 0)
        def _():
            dal_ref[...] = jnp.zeros_like(dal_ref)
            ddt_ref[...] = jnp.zeros_like(ddt_ref)

        for p, (d_ref, c_ref, o_ref) in enumerate(((dq_ref, cq_ref, dcq_ref), (dk_ref, ck_ref, dck_ref), (dv_ref, cv_ref, dcv_ref))):
            c, d = c_ref[...], d_ref[...]
            if p == 2:
                ds = d
            else:
                s = _silu(c)
                r = lax.rsqrt(_head_sums(s * s) + L2_EPS)
                ds = (DN_HEAD_DIM ** -0.5 if p == 0 else 1.0) * r * (d - s * r * r * _head_sums(d * s))
            o_ref[...] = ds * _silu_grad(c)
        ab, dgb_v, gb_v = ab_ref[...], dgb_ref[...], gb_ref[...]
        lane = lax.broadcasted_iota(jnp.int32, ab.shape, 1)
        is_g = lane < DN_HEADS
        dpre = dgb_v * (-jnp.exp(al_ref[...])) * _sigmoid(ab + dt_ref[...])
        dab_ref[...] = jnp.where(is_g, dpre, dgb_v * gb_v * (1.0 - gb_v)).astype(dab_ref.dtype)
        dal_ref[...] += jnp.sum(jnp.where(is_g, dgb_v * gb_v, 0.0), axis=0, keepdims=True)
        ddt_ref[...] += jnp.sum(jnp.where(is_g, dpre, 0.0), axis=0, keepdims=True)

    row = pl.BlockSpec((tl, gw), lambda i: (i, 0))
    abrow = pl.BlockSpec((tl, AB_PAD), lambda i: (i, 0))
    abvec = pl.BlockSpec((1, AB_PAD), lambda i: (0, 0))
    return pl.pallas_call(
        body, name=name, grid=(l // tl,),
        in_specs=[row] * 6 + [abrow, abrow, abrow, abvec, abvec],
        out_specs=[row, row, row, abrow, abvec, abvec],
        out_shape=[jax.ShapeDtypeStruct((l, gw), F32)] * 3 + [jax.ShapeDtypeStruct((l, AB_PAD), BF16),
                   jax.ShapeDtypeStruct((1, AB_PAD), F32), jax.ShapeDtypeStruct((1, AB_PAD), F32)],
        compiler_params=_params(("arbitrary",)),
    )(dq, dk, dv, cq, ck, cv, dgb, gb, proj_ab, a_log, dt_bias)


def _conv_bwd(dc, proj, col_blk, w_part, name):
    l, gw = dc.shape
    tl = min(MIX_ROWS, l)
    nb = l // tl

    def body(dc_ref, halo_ref, x_ref, w_ref, dx_ref, dw_ref):
        i = pl.program_id(0)

        @pl.when(i == 0)
        def _():
            dw_ref[...] = jnp.zeros_like(dw_ref)

        ext = jnp.concatenate([dc_ref[...], jnp.where(i < nb - 1, halo_ref[...], 0.0)], axis=0)
        x = x_ref[...]
        dx = jnp.zeros((tl, gw), F32)
        rid = lax.broadcasted_iota(jnp.int32, (8, gw), 0)
        dw = jnp.zeros((8, gw), F32)
        for j in range(DN_CONV):
            k = DN_CONV - 1 - j
            shifted = (ext if k == 0 else pltpu.roll(ext, tl + CONV_HALO - k, axis=0))[:tl, :]
            dx = dx + shifted * w_ref[j:j + 1, :]
            dw = dw + jnp.where(rid == j, jnp.sum(x * shifted, axis=0, keepdims=True), 0.0)
        dx_ref[...] = dx.astype(dx_ref.dtype)
        dw_ref[...] += dw

    row = pl.BlockSpec((tl, gw), lambda i: (i, 0))
    return pl.pallas_call(
        body, name=name, grid=(nb,),
        in_specs=[row, pl.BlockSpec((CONV_HALO, gw), lambda i: (jnp.minimum((i + 1) * (tl // CONV_HALO), l // CONV_HALO - 1), 0)),
                  pl.BlockSpec((tl, gw), lambda i: (i, col_blk)), pl.BlockSpec((DN_CONV, gw), lambda i: (0, 0))],
        out_specs=[row, pl.BlockSpec((8, gw), lambda i: (0, 0))],
        out_shape=[jax.ShapeDtypeStruct((l, gw), BF16), jax.ShapeDtypeStruct((8, gw), F32)],
        compiler_params=_params(("arbitrary",)),
    )(dc, dc, proj, w_part)


TERMS_CHUNKS = 8


def _bdot(a, b, dims="nn", precision=None):
    cd = {"nn": ((2,), (1,)), "nt": ((2,), (2,)), "tn": ((1,), (1,))}[dims]
    return lax.dot_general(a, b, (cd, ((0,), (0,))), preferred_element_type=F32, precision=precision)


def _bmdot(a, b, dims="nn"):
    return _bdot(_mx(a), _mx(b), dims)


def _bdot3(a, b, dims="nn"):
    ah, bh = a.astype(BF16), b.astype(BF16)
    al, bl = (a - ah.astype(F32)).astype(BF16), (b - bh.astype(F32)).astype(BF16)
    return _bdot(ah, bh, dims) + (_bdot(ah, bl, dims) + _bdot(al, bh, dims))


def _wy_terms(q, k, v, gcol, beta, t=None):
    c = DN_CHUNK
    ii = lax.broadcasted_iota(jnp.int32, (1, c, c), 1)
    jj = lax.broadcasted_iota(jnp.int32, (1, c, c), 2)
    tril, strict = ii >= jj, ii > jj
    grow = jnp.sum(jnp.where(ii == jj, gcol, 0.0), axis=1, keepdims=True)
    gc_col = jnp.sum(jnp.where(tril, grow, 0.0), axis=2, keepdims=True)
    gc_row = jnp.sum(jnp.where(ii <= jj, gcol, 0.0), axis=1, keepdims=True)
    dec = jnp.exp(jnp.where(tril, gc_col - gc_row, -1e30))
    kb, vb = k * beta, v * beta
    kk = _bmdot(kb, k, "nt")
    if t is None:
        a = jnp.where(strict, kk * dec, 0.0)
        d = jnp.where((ii >> 3) == (jj >> 3), a, 0.0)
        t = jnp.where(ii == jj, 1.0, 0.0) - d
        p = _bdot(d, d, precision=HI)
        t = t + _bdot(t, p, precision=HI)
        t = t + _bdot(t, _bdot(p, p, precision=HI), precision=HI)
        for sh in (3, 4, 5):
            below = ((ii >> (sh + 1)) == (jj >> (sh + 1))) & ((ii >> sh) > (jj >> sh))
            t = t - _bdot3(t, _bdot3(jnp.where(below, a, 0.0), t))
    eg = jnp.exp(gc_col)
    gc_last = gc_col[:, c - 1:c, :]
    kbg = kb * eg
    qk0 = _bmdot(q, k, "nt")
    e2 = jnp.exp(gc_last - gc_col)
    return dict(ii=ii, jj=jj, tril=tril, strict=strict, dec=dec, kb=kb, vb=vb, kk=kk, t=t, eg=eg, kbg=kbg,
                qk0=qk0, qk=jnp.where(tril, qk0 * dec, 0.0), qg=q * eg, e2=e2, kt=k * e2, gl=jnp.exp(gc_last))


def _to_heads(x, g):
    return jnp.concatenate([x[:, h * DN_HEAD_DIM:(h + 1) * DN_HEAD_DIM].reshape(g, DN_CHUNK, DN_HEAD_DIM)
                            for h in range(DN_HEADS)], axis=0)


def _from_heads(t, ref, g):
    for h in range(DN_HEADS):
        ref[:, h * DN_HEAD_DIM:(h + 1) * DN_HEAD_DIM] = t[h * g:(h + 1) * g].reshape(g * DN_CHUNK, DN_HEAD_DIM).astype(ref.dtype)


def _head_columns(gbv, first_lane, g):
    lane = lax.broadcasted_iota(jnp.int32, gbv.shape, 1)
    return jnp.concatenate([jnp.sum(jnp.where(lane == first_lane + h, gbv, 0.0), axis=1, keepdims=True).reshape(g, DN_CHUNK, 1)
                            for h in range(DN_HEADS)], axis=0)


def _gdn_terms_fwd(qn, kn, v, gb, name):
    l = qn.shape[0]
    n_chunks = l // DN_CHUNK
    g = min(TERMS_CHUNKS, n_chunks)
    rows, c, nh = g * DN_CHUNK, DN_CHUNK, DN_HEADS

    def body(q_ref, k_ref, v_ref, gb_ref, u_ref, w_ref, qg_ref, kt_ref, qk_ref, t_ref, gl_ref):
        gbv = gb_ref[...]
        x = _wy_terms(_to_heads(q_ref[...], g), _to_heads(k_ref[...], g), _to_heads(v_ref[...], g),
                      _head_columns(gbv, 0, g), _head_columns(gbv, nh, g))
        _from_heads(_bmdot(x["t"], x["vb"]), u_ref, g)
        _from_heads(_bmdot(x["t"], x["kbg"]), w_ref, g)
        _from_heads(x["qg"], qg_ref, g)
        _from_heads(x["kt"], kt_ref, g)
        for h in range(nh):
            qk_ref[:, h] = x["qk"][h * g:(h + 1) * g]
            t_ref[:, h] = x["t"][h * g:(h + 1) * g]
            gl_ref[:, h] = jnp.broadcast_to(x["gl"][h * g:(h + 1) * g], (g, 1, 128))

    row = pl.BlockSpec((rows, GROUP_WIDTH), lambda i: (i, 0))
    sq = pl.BlockSpec((g, nh, c, c), lambda i: (i, 0, 0, 0))
    glb = pl.BlockSpec((g, nh, 1, 128), lambda i: (i, 0, 0, 0))
    return pl.pallas_call(
        body, name=name, grid=(n_chunks // g,), in_specs=[row, row, row, pl.BlockSpec((rows, AB_PAD), lambda i: (i, 0))],
        out_specs=[row] * 4 + [sq, sq, glb],
        out_shape=[jax.ShapeDtypeStruct((l, GROUP_WIDTH), F32)] * 4 + [jax.ShapeDtypeStruct((n_chunks, nh, c, c), F32)] * 2
        + [jax.ShapeDtypeStruct((n_chunks, nh, 1, 128), F32)],
        compiler_params=_params(("parallel",)),
    )(qn, kn, v, gb)


REC_CHUNKS = 8


def _rec_specs(n_chunks, reverse):
    c, hd, nh = DN_CHUNK, DN_HEAD_DIM, DN_HEADS
    g = min(REC_CHUNKS, n_chunks)
    nb = n_chunks // g
    blk_of = (lambda n: nb - 1 - n) if reverse else (lambda n: n)
    return g, nb, (pl.BlockSpec((g * c, GROUP_WIDTH), lambda n: (blk_of(n), 0)), pl.BlockSpec((g, nh, c, c), lambda n: (blk_of(n), 0, 0, 0)),
                   pl.BlockSpec((g, nh, 1, 128), lambda n: (blk_of(n), 0, 0, 0)), pl.BlockSpec((g, nh, hd, hd), lambda n: (blk_of(n), 0, 0, 0)))


def _gdn_rec_fwd(u, w, qg, kt, qk, gl, name):
    l = u.shape[0]
    n_chunks = l // DN_CHUNK
    hd, nh, c = DN_HEAD_DIM, DN_HEADS, DN_CHUNK
    g, nb, (blk, sq, glb, st) = _rec_specs(n_chunks, False)
    heads = range(nh)

    def body(u_ref, w_ref, qg_ref, kt_ref, qk_ref, gl_ref, o_ref, vn_ref, s_ref, state):
        @pl.when(pl.program_id(0) == 0)
        def _():
            state[...] = jnp.zeros_like(state)

        def cols(h):
            return slice(h * hd, (h + 1) * hd)
        for ci in range(g):
            rows = slice(ci * c, (ci + 1) * c)
            s = [state[h] for h in heads]
            ws = [_mdot(w_ref[rows, cols(h)], s[h]) for h in heads]
            vn = [u_ref[rows, cols(h)] - ws[h] for h in heads]
            kv = [_mdot(kt_ref[rows, cols(h)], vn[h], "tn") for h in heads]
            for h in heads:
                state[h] = s[h] * gl_ref[ci, h] + kv[h]
            o1 = [_mdot(qg_ref[rows, cols(h)], s[h]) for h in heads]
            o2 = [_mdot(qk_ref[ci, h], vn[h]) for h in heads]
            for h in heads:
                s_ref[ci, h] = s[h]
                o_ref[rows, cols(h)] = o1[h] + o2[h]
                vn_ref[rows, cols(h)] = vn[h]

    return pl.pallas_call(
        body, name=name, grid=(nb,), in_specs=[blk, blk, blk, blk, sq, glb], out_specs=[blk, blk, st],
        out_shape=[jax.ShapeDtypeStruct((l, GROUP_WIDTH), F32)] * 2 + [jax.ShapeDtypeStruct((n_chunks, nh, hd, hd), F32)],
        scratch_shapes=[pltpu.VMEM((nh, hd, hd), F32)], compiler_params=_params(("arbitrary",)),
    )(u, w, qg, kt, qk, gl)


def _gdn_rec_bwd(do, w, qg, kt, vn, qk, gl, states, name):
    l = do.shape[0]
    n_chunks = l // DN_CHUNK
    hd, nh, c = DN_HEAD_DIM, DN_HEADS, DN_CHUNK
    g, nb, (blk, sq, glb, st) = _rec_specs(n_chunks, True)
    heads = range(nh)

    def body(do_ref, w_ref, qg_ref, kt_ref, vn_ref, qk_ref, gl_ref, s_ref, dvn_ref, dw_ref, dkt_ref, dqg_ref, dqk_ref, dgl_ref, dstate):
        @pl.when(pl.program_id(0) == 0)
        def _():
            dstate[...] = jnp.zeros_like(dstate)

        def cols(h):
            return slice(h * hd, (h + 1) * hd)
        tril = lax.broadcasted_iota(jnp.int32, (c, c), 0) >= lax.broadcasted_iota(jnp.int32, (c, c), 1)
        for ci in reversed(range(g)):
            rows = slice(ci * c, (ci + 1) * c)
            ds = [dstate[h] for h in heads]
            dout = [do_ref[rows, cols(h)] for h in heads]
            a1 = [_mdot(qk_ref[ci, h], dout[h], "tn") for h in heads]
            a2 = [_mdot(kt_ref[rows, cols(h)], ds[h]) for h in heads]
            dvn = [a1[h] + a2[h] for h in heads]
            b1 = [_mdot(qg_ref[rows, cols(h)], dout[h], "tn") for h in heads]
            b2 = [_mdot(w_ref[rows, cols(h)], dvn[h], "tn") for h in heads]
            for h in heads:
                dstate[h] = b1[h] + gl_ref[ci, h] * ds[h] - b2[h]
            for h in heads:
                s, vnew = s_ref[ci, h], vn_ref[rows, cols(h)]
                dvn_ref[rows, cols(h)] = dvn[h]
                dw_ref[rows, cols(h)] = -_mdot(dvn[h], s, "nt")
                dkt_ref[rows, cols(h)] = _mdot(vnew, ds[h], "nt")
                dqg_ref[rows, cols(h)] = _mdot(dout[h], s, "nt")
                dqk_ref[ci, h] = jnp.where(tril, _mdot(dout[h], vnew, "nt"), 0.0)
                dgl = jnp.sum(jnp.sum(ds[h] * s, axis=1, keepdims=True), axis=0, keepdims=True)
                dgl_ref[ci, h] = jnp.broadcast_to(dgl, (1, 128))

    return pl.pallas_call(
        body, name=name, grid=(nb,), in_specs=[blk] * 5 + [sq, glb, st], out_specs=[blk] * 4 + [sq, glb],
        out_shape=[jax.ShapeDtypeStruct((l, GROUP_WIDTH), F32)] * 4 + [jax.ShapeDtypeStruct((n_chunks, nh, c, c), F32),
                                                                       jax.ShapeDtypeStruct((n_chunks, nh, 1, 128), F32)],
        scratch_shapes=[pltpu.VMEM((nh, hd, hd), F32)], compiler_params=_params(("arbitrary",)),
    )(do, w, qg, kt, vn, qk, gl, states)


def _gdn_terms_bwd(qn, kn, v, gb, t_inv, dvn, dw, dkt, dqg, dqk, dgl, name):
    l = qn.shape[0]
    n_chunks = l // DN_CHUNK
    g = min(TERMS_CHUNKS, n_chunks)
    rows, c, nh = g * DN_CHUNK, DN_CHUNK, DN_HEADS

    def body(q_ref, k_ref, v_ref, gb_ref, t_ref, dvn_ref, dw_ref, dkt_ref, dqg_ref, dqk_ref, dgl_ref, dq_ref, dk_ref, dv_ref, dgb_ref):
        gbv = gb_ref[...]
        q, k, vv = _to_heads(q_ref[...], g), _to_heads(k_ref[...], g), _to_heads(v_ref[...], g)
        beta = _head_columns(gbv, nh, g)
        t = jnp.concatenate([t_ref[:, h] for h in range(nh)], axis=0)
        x = _wy_terms(q, k, vv, _head_columns(gbv, 0, g), beta, t=t)
        ii, jj, strict = x["ii"], x["jj"], x["strict"]
        du, dwv, dktv, dqgv = (_to_heads(r[...], g) for r in (dvn_ref, dw_ref, dkt_ref, dqg_ref))
        dqkv = jnp.concatenate([dqk_ref[:, h] for h in range(nh)], axis=0)
        dglv = jnp.concatenate([dgl_ref[:, h] for h in range(nh)], axis=0)[:, :, 0:1]
        dt = _bmdot(du, x["vb"], "nt") + _bmdot(dwv, x["kbg"], "nt")
        dvb = _bmdot(t, du, "tn")
        dkbg = _bmdot(t, dwv, "tn")
        da = jnp.where(strict, -_bdot3(_bdot3(t, dt, "tn"), t, "nt"), 0.0)
        dkk = da * x["dec"]
        dqk0 = dqkv * x["dec"]
        e = (da * x["kk"] + dqkv * x["qk0"]) * x["dec"]
        dkb = _bmdot(dkk, k) + dkbg * x["eg"]
        dk = _bmdot(dkk, x["kb"], "tn") + _bmdot(dqk0, q, "tn") + dktv * x["e2"] + dkb * beta
        dq = _bmdot(dqk0, k) + dqgv * x["eg"]
        s_kt = jnp.sum(dktv * x["kt"], axis=2, keepdims=True)
        dgc_c = (jnp.sum(e, axis=2, keepdims=True) + jnp.sum(dqgv * x["qg"], axis=2, keepdims=True) - s_kt
                 + jnp.sum(dkbg * x["kbg"], axis=2, keepdims=True))
        dgc_last = jnp.sum(s_kt, axis=1, keepdims=True) + dglv * x["gl"]
        rid = lax.broadcasted_iota(jnp.int32, (1, c, 1), 1)
        dgc_c = dgc_c + jnp.where(rid == c - 1, dgc_last, 0.0)
        dgc_r = jnp.sum(jnp.where(ii == jj, dgc_c, 0.0), axis=1, keepdims=True) - jnp.sum(e, axis=1, keepdims=True)
        dg = jnp.sum(jnp.where(jj >= ii, dgc_r, 0.0), axis=2, keepdims=True)
        dbeta = jnp.sum(dkb * k, axis=2, keepdims=True) + jnp.sum(dvb * vv, axis=2, keepdims=True)
        _from_heads(dq, dq_ref, g)
        _from_heads(dk, dk_ref, g)
        _from_heads(dvb * beta, dv_ref, g)
        lane = lax.broadcasted_iota(jnp.int32, gbv.shape, 1)
        dgb = jnp.zeros(gbv.shape, F32)
        for h in range(nh):
            dgb = dgb + jnp.where(lane == h, dg[h * g:(h + 1) * g].reshape(rows, 1), 0.0)
            dgb = dgb + jnp.where(lane == nh + h, dbeta[h * g:(h + 1) * g].reshape(rows, 1), 0.0)
        dgb_ref[...] = dgb

    row = pl.BlockSpec((rows, GROUP_WIDTH), lambda i: (i, 0))
    abrow = pl.BlockSpec((rows, AB_PAD), lambda i: (i, 0))
    sq = pl.BlockSpec((g, nh, c, c), lambda i: (i, 0, 0, 0))
    glb = pl.BlockSpec((g, nh, 1, 128), lambda i: (i, 0, 0, 0))
    return pl.pallas_call(
        body, name=name, grid=(n_chunks // g,), in_specs=[row, row, row, abrow, sq, row, row, row, row, sq, glb],
        out_specs=[row, row, row, abrow],
        out_shape=[jax.ShapeDtypeStruct((l, GROUP_WIDTH), F32)] * 3 + [jax.ShapeDtypeStruct((l, AB_PAD), F32)],
        compiler_params=_params(("parallel",)),
    )(qn, kn, v, gb, t_inv, dvn, dw, dkt, dqg, dqk, dgl)


def _gdn_post_fwd(o, proj, norm_g4, name):
    l, gw = o.shape
    tl = min(MIX_ROWS, l)

    def body(o_ref, gate_ref, g_ref, out_ref):
        ov = o_ref[...]
        r = lax.rsqrt(_head_sums(ov * ov) * (1.0 / DN_HEAD_DIM) + RMS_EPS)
        out_ref[...] = (ov * r * g_ref[...] * _silu(gate_ref[...])).astype(out_ref.dtype)

    row = pl.BlockSpec((tl, gw), lambda i: (i, 0))
    return pl.pallas_call(
        body, name=name, grid=(l // tl,),
        in_specs=[row, pl.BlockSpec((tl, gw), lambda i: (i, 7)), pl.BlockSpec((1, gw), lambda i: (0, 0))],
        out_specs=row, out_shape=jax.ShapeDtypeStruct((l, gw), BF16), compiler_params=_params(("parallel",)),
    )(o, proj, norm_g4)


def _gdn_post_bwd(dmixed, o, proj, norm_g4, name):
    l, gw = o.shape
    tl = min(MIX_ROWS, l)

    def body(d_ref, o_ref, gate_ref, g_ref, do_ref, dgate_ref, dng_ref):
        @pl.when(pl.program_id(0) == 0)
        def _():
            dng_ref[...] = jnp.zeros_like(dng_ref)

        ov, gate, d = o_ref[...], gate_ref[...], d_ref[...]
        r = lax.rsqrt(_head_sums(ov * ov) * (1.0 / DN_HEAD_DIM) + RMS_EPS)
        oh = ov * r
        sg = _silu(gate)
        dgate_ref[...] = (d * oh * g_ref[...] * _silu_grad(gate)).astype(dgate_ref.dtype)
        dng_ref[...] += jnp.sum(d * sg * oh, axis=0, keepdims=True)
        doh = d * g_ref[...] * sg
        do_ref[...] = r * (doh - oh * _head_sums(doh * oh) * (1.0 / DN_HEAD_DIM))

    row = pl.BlockSpec((tl, gw), lambda i: (i, 0))
    vec = pl.BlockSpec((1, gw), lambda i: (0, 0))
    return pl.pallas_call(
        body, name=name, grid=(l // tl,),
        in_specs=[pl.BlockSpec((tl, gw), lambda i: (i, 3)), row, pl.BlockSpec((tl, gw), lambda i: (i, 7)), vec],
        out_specs=[row, row, vec],
        out_shape=[jax.ShapeDtypeStruct((l, gw), F32), jax.ShapeDtypeStruct((l, gw), BF16), jax.ShapeDtypeStruct((1, gw), F32)],
        compiler_params=_params(("arbitrary",)),
    )(dmixed, o, proj, norm_g4)


def _run(hosts, name, fn):
    h = hosts.get(name)
    if h is None:
        return fn(None)
    res, outs = fn(h[0]())
    h[1](outs)
    return res


def _layer_fwd(x, xm, w, li, hosts):
    l = x.shape[0]
    nm = f"l{li}_"
    proj = _run(hosts, nm + "proj", lambda ops: _matmul(
        xm, w["w_main"], mode="nn", tm=1024, tn=1024, tk=2048,out_dtype=F32, name=nm + "proj", comm=ops))
    proj_ab = _matmul(xm, w["w_ab"], mode="nn", tm=1024, tn=AB_PAD, tk=2048, out_dtype=F32, name=nm + "proj_ab")
    hs, y = _s5_fwd(proj, w["s5_b"], w["s5_c"], w["s5_lam"], w["s5_d"], nm + "s5")
    m_s5 = _s5_glu_fwd(y, w["s5_glu_w"], w["s5_glu_b"], nm + "s5_glu")
    m_sgu = _sgu_fwd(proj, w["sgu_norm_g"], w["sgu_norm_b"], w["sgu_wm"], w["sgu_bfull"], nm + "sgu")
    m_pool, pooled = _pool_fwd(proj, w["pool_w"], w["pool_scale"], nm + "pool")
    qn, kn, v, cq, ck, cv, gb = _gdn_pre_fwd(proj, proj_ab, w["dn_conv_w"], w["dn_a_log"], w["dn_dt_bias"], nm + "gdn_pre")
    u, wy, qg, kt, qk, t_inv, gl = _gdn_terms_fwd(qn, kn, v, gb, nm + "gdn_terms")
    o, vn, states = _gdn_rec_fwd(u, wy, qg, kt, qk, gl, nm + "gdn_rec")
    m_dn = _gdn_post_fwd(o, proj, w["dn_norm_g4"], nm + "gdn_post")
    mixed = jnp.concatenate([m_s5, m_sgu, m_pool, m_dn], axis=1)
    y1 = _matmul(mixed, w["w_out"], mode="nn", tm=1024, tn=1024, tk=2048, out_dtype=F32, name=nm + "out_proj")
    h1, x1, x1m = _ln_fwd(x, y1, w["ln1_g"], w["ln1_b"], nm + "ln1")
    r = _run(hosts, nm + "up", lambda ops: _matmul(
        x1m, w["w_up"], mode="nn", tm=1024, tn=1024, tk=2048,out_dtype=BF16, name=nm + "up",
        epi=lambda acc: jnp.maximum(acc, 0.0), b_slab=w["w_up"].shape[2], comm=ops))
    y2 = _run(hosts, nm + "down", lambda ops: _matmul(
        r, w["w_down"], mode="nn", tm=1024, tn=1024, tk=2048,out_dtype=F32, name=nm + "down", a_fn=lambda a: a * a, comm=ops))
    h2, x2, x2m = _ln_fwd(x1, y2, w["ln2_g"], w["ln2_b"], nm + "ln2")
    saved = dict(xm=xm, proj=proj, proj_ab=proj_ab, hs=hs, y=y, pooled=pooled, qn=qn, kn=kn, v=v, cq=cq, ck=ck, cv=cv, gb=gb,
                 wy=wy, qg=qg, kt=kt, qk=qk, t_inv=t_inv, gl=gl, vn=vn, o=o, states=states, mixed=mixed, h1=h1, x1m=x1m,
                 r=r, h2=h2)
    return x2, x2m, saved


def _layer_bwd(dx2, s, w, small, li, hosts, g):
    nm = f"l{li}b_"
    l = dx2.shape[0]
    gw = GROUP_WIDTH
    wire = MXU_DTYPE
    dh2, dh2m, g["ln2_g"], g["ln2_b"] = _ln_bwd(dx2, s["h2"], w["ln2_g"], nm + "ln2")
    g["w_down"] = _run(hosts, nm + "dw_down", lambda ops: _matmul(
        s["r"], dh2m, mode="tn", tm=1024, tn=1024, tk=2048,out_dtype=wire, name=nm + "dw_down", a_fn=lambda a: a * a,
        comm=ops)).reshape(N_DEV, D_FF // N_DEV, D_MODEL)
    dpre = _run(hosts, nm + "dpre", lambda ops: _matmul(
        dh2m, w["w_down"], mode="nt", tm=1024, tn=1024, tk=2048,out_dtype=BF16, name=nm + "dpre",
        extras=[(s["r"], (None, None), lambda i, j: (i, j))], epi=lambda acc, r: acc * 2.0 * r.astype(F32), comm=ops))
    g["w_up"] = _matmul(s["x1m"], dpre, mode="tn", tm=1024, tn=1024, tk=2048,out_dtype=wire, name=nm + "dw_up",
                        out_slab=D_FF // N_DEV)
    dx1 = _run(hosts, nm + "dx1", lambda ops: _matmul(
        dpre, w["w_up"], mode="nt", tm=1024, tn=1024, tk=2048,out_dtype=F32, name=nm + "dx1",
        extras=[(dh2, (None, None), lambda i, j: (i, j))], epi=lambda acc, e: acc + ALPHA * e,
        b_slab=w["w_up"].shape[2], comm=ops))
    dh1, dh1m, g["ln1_g"], g["ln1_b"] = _ln_bwd(dx1, s["h1"], w["ln1_g"], nm + "ln1")
    g["w_out"] = _matmul(s["mixed"], dh1m, mode="tn", tm=1024, tn=1024, tk=2048,out_dtype=wire,
                         name=nm + "dw_out").reshape(N_DEV, D_MODEL // N_DEV, D_MODEL)
    dmixed = _run(hosts, nm + "dmixed", lambda ops: _matmul(
        dh1m, w["w_out"], mode="nt", tm=1024, tn=1024, tk=2048,out_dtype=F32, name=nm + "dmixed", comm=ops))
    proj, proj_ab = s["proj"], s["proj_ab"]
    dy, dz, yg, g["s5_glu_b"], g["s5_d"] = _s5_glu_bwd(dmixed, s["y"], proj, w["s5_glu_w"], w["s5_glu_b"], nm + "s5_glu")
    g["s5_glu_w"] = _matmul(yg, dz, mode="tn", tm=gw, tn=gw, tk=1024, out_dtype=wire,
                            name=nm + "dw_glu").reshape(N_DEV, gw // N_DEV, gw)
    du_s5, g["s5_b"], g["s5_c"], g["s5_lam"] = _s5_bwd(dy, s["hs"], proj, w["s5_b"], w["s5_c"], w["s5_lam_conj"], w["s5_d"], nm + "s5")
    dzu, dzv, g["sgu_w"], g["sgu_bfull"], g["sgu_norm_g"], g["sgu_norm_b"] = _sgu_bwd(
        dmixed, proj, w["sgu_norm_g"], w["sgu_norm_b"], w["sgu_wm"], w["sgu_bfull"], nm + "sgu")
    dpooled, g["pool_w"], g["pool_scale"] = _pool_bwd_map(dmixed, s["pooled"], w["pool_w"], w["pool_scale"], nm + "pool_map")
    dp = _pool_bwd_window(dpooled, nm + "pool_win")
    do, dgate, g["dn_norm_g4"] = _gdn_post_bwd(dmixed, s["o"], proj, w["dn_norm_g4"], nm + "gdn_post")
    dvn, dwy, dkt, dqg, dqk, dgl = _gdn_rec_bwd(do, s["wy"], s["qg"], s["kt"], s["vn"], s["qk"], s["gl"], s["states"], nm + "gdn_rec")
    dq, dk, dv, dgb = _gdn_terms_bwd(s["qn"], s["kn"], s["v"], s["gb"], s["t_inv"], dvn, dwy, dkt, dqg, dqk, dgl, nm + "gdn_terms")
    dcq, dck, dcv, dab, g["dn_a_log"], g["dn_dt_bias"] = _gdn_pre_bwd(
        dq, dk, dv, s["cq"], s["ck"], s["cv"], dgb, s["gb"], proj_ab, w["dn_a_log"], w["dn_dt_bias"], nm + "gdn_pre")
    dxs, dws = [], []
    for p, dc in enumerate((dcq, dck, dcv)):
        dxp, dwp = _conv_bwd(dc, proj, QKV_BLK + p, w["dn_conv_w"][:, p * gw:(p + 1) * gw], nm + f"conv{p}")
        dxs.append(dxp)
        dws.append(dwp)
    dconv = jnp.concatenate(dws, axis=1)
    g["dn_conv_w"] = jnp.transpose(dconv.reshape(dconv.shape[0], N_DEV, 3 * gw // N_DEV), (1, 0, 2))
    dproj = jnp.concatenate([du_s5, dzu, dzv, dp] + dxs + [dgate], axis=1)
    xm = s["xm"]
    g["small"] = _unprep_grads(g, small)
    dw_main = _run(hosts, nm + "dw_main", lambda ops: _matmul(
        xm, dproj, mode="tn", tm=1024, tn=1024, tk=2048,out_dtype=wire, name=nm + "dw_main", comm=ops))
    dw_ab = _matmul(xm, dab, mode="tn", tm=1024, tn=AB_PAD, tk=1024, out_dtype=wire, name=nm + "dw_ab")
    dw_in = jnp.concatenate([dw_main, dw_ab[:, :2 * DN_HEADS]], axis=1)
    g["w_in"] = jnp.transpose(dw_in.reshape(D_MODEL, N_DEV, dw_in.shape[1] // N_DEV), (1, 0, 2))
    return _run(hosts, nm + "dx", lambda ops: _matmul(
        dproj, w["w_main"], mode="nt", tm=1024, tn=1024, tk=2048, out_dtype=F32, name=nm + "dx",
        extras=[(dh1, (None, None), lambda i, j: (i, j)), (dab, (None, AB_PAD), lambda i, j: (i, 0)),
                (w["w_ab"], ("tn", AB_PAD), lambda i, j: (j, 0))],
        epi=lambda acc, e, da, wab: acc + ALPHA * e + _dot(_mx(da), _mx(wab), "nt"), comm=ops))


SMALL = ("s5_lambda_re", "s5_lambda_im", "s5_log_step", "s5_b_re", "s5_b_im", "s5_c_re", "s5_c_im", "s5_d", "s5_glu_b",
         "sgu_norm_g", "sgu_norm_b", "sgu_w", "sgu_b", "pool_w", "pool_scale", "dn_a_log", "dn_dt_bias", "dn_norm_g",
         "ln1_g", "ln1_b", "ln2_g", "ln2_b")
SHARDED = ("w_in", "s5_glu_w", "dn_conv_w", "w_out", "w_up", "w_down")


def _pad_lanes(v, width=AB_PAD):
    return jnp.pad(v.reshape(1, -1), ((0, 0), (0, width - v.size)))


def _prep_small(p):
    mx = MXU_DTYPE
    lbr, lbi, bbr, bbi = _s5_discretize(p["s5_lambda_re"], p["s5_lambda_im"], p["s5_log_step"], p["s5_b_re"], p["s5_b_im"])
    b_compact, c_compact = _s5_compact(bbr, bbi, p["s5_c_re"], p["s5_c_im"])
    causal = jnp.tril(jnp.ones((SGU_CHUNK, SGU_CHUNK), F32))
    return dict(
        s5_b=b_compact.astype(mx), s5_c=c_compact.astype(mx),
        s5_lam=jnp.concatenate([lbr.reshape(1, -1), lbi.reshape(1, -1)], axis=1),
        s5_lam_conj=jnp.concatenate([lbr.reshape(1, -1), -lbi.reshape(1, -1)], axis=1),
        s5_d=p["s5_d"].reshape(1, -1), s5_glu_b=p["s5_glu_b"].reshape(1, -1),
        sgu_norm_g=p["sgu_norm_g"].reshape(1, -1), sgu_norm_b=p["sgu_norm_b"].reshape(1, -1),
        sgu_wm=(p["sgu_w"] * causal).astype(mx), sgu_bfull=jnp.repeat(p["sgu_b"].T, GROUP_WIDTH // SGU_HEADS, axis=1),
        pool_w=p["pool_w"].astype(mx), pool_scale=p["pool_scale"].reshape(1, -1),
        dn_a_log=_pad_lanes(p["dn_a_log"]), dn_dt_bias=_pad_lanes(p["dn_dt_bias"]),
        dn_norm_g4=jnp.tile(p["dn_norm_g"].reshape(1, -1), (1, DN_HEADS)),
        ln1_g=p["ln1_g"].reshape(1, -1), ln1_b=p["ln1_b"].reshape(1, -1),
        ln2_g=p["ln2_g"].reshape(1, -1), ln2_b=p["ln2_b"].reshape(1, -1),
    )


def _weight_views(name, t):
    if name == "w_in":
        w_in = jnp.transpose(t, (1, 0, 2)).reshape(t.shape[1], N_DEV * t.shape[2])
        pad = AB_PAD - (w_in.shape[1] - MAIN_COLS)
        return dict(w_main=w_in[:, :MAIN_COLS], w_ab=jnp.pad(w_in[:, MAIN_COLS:], ((0, 0), (0, pad))))
    if name == "dn_conv_w":
        return dict(dn_conv_w=jnp.transpose(t, (1, 0, 2)).reshape(t.shape[1], N_DEV * t.shape[2]))
    if name == "w_up":
        return dict(w_up=t)
    return {name: t.reshape(N_DEV * t.shape[1], t.shape[2])}


def _unprep_grads(g, p):
    causal = jnp.tril(jnp.ones((SGU_CHUNK, SGU_CHUNK), F32))
    dbbr, dbbi = _s5_uncompact_b(g["s5_b"])
    dc_re, dc_im = _s5_uncompact_c(g["s5_c"])
    dlbr, dlbi = g["s5_lam"][0, :S5_NS].reshape(S5_GROUPS, S5_STATE), g["s5_lam"][0, S5_NS:].reshape(S5_GROUPS, S5_STATE)
    _, vjp = jax.vjp(_s5_discretize, p["s5_lambda_re"], p["s5_lambda_im"], p["s5_log_step"], p["s5_b_re"], p["s5_b_im"])
    d_lre, d_lim, d_step, d_bre, d_bim = vjp((dlbr, dlbi, dbbr, dbbi))
    hd = GROUP_WIDTH // SGU_HEADS
    return dict(
        s5_lambda_re=d_lre, s5_lambda_im=d_lim, s5_log_step=d_step, s5_b_re=d_bre, s5_b_im=d_bim, s5_c_re=dc_re, s5_c_im=dc_im,
        s5_d=g["s5_d"].reshape(S5_GROUPS, S5_CH), s5_glu_b=g["s5_glu_b"].reshape(-1),
        sgu_norm_g=g["sgu_norm_g"].reshape(-1), sgu_norm_b=g["sgu_norm_b"].reshape(-1), sgu_w=g["sgu_w"] * causal,
        sgu_b=jnp.sum(g["sgu_bfull"].reshape(SGU_CHUNK, SGU_HEADS, hd), axis=2).T,
        pool_w=g["pool_w"], pool_scale=g["pool_scale"].reshape(-1),
        dn_a_log=g["dn_a_log"][0, :DN_HEADS], dn_dt_bias=g["dn_dt_bias"][0, :DN_HEADS],
        dn_norm_g=jnp.sum(g["dn_norm_g4"].reshape(DN_HEADS, DN_HEAD_DIM), axis=0),
        ln1_g=g["ln1_g"].reshape(-1), ln1_b=g["ln1_b"].reshape(-1), ln2_g=g["ln2_g"].reshape(-1), ln2_b=g["ln2_b"].reshape(-1),
    )


def _local_step(x, target, ops, small, fwd_hosts, bwd_hosts, grads):
    saved = []
    h, hm = x, x.astype(MXU_DTYPE)
    for i in range(DEPTH):
        h, hm, s = _layer_fwd(h, hm, ops[i], i, fwd_hosts)
        saved.append(s)
    loss, dh = _loss_head(h, target)
    for i in reversed(range(DEPTH)):
        dh = _layer_bwd(dh, saved[i], ops[i], small[i], i, bwd_hosts, grads[i])
    return loss, dh


def _adamw(w, gparts, m, v, name):
    rr, c = w.shape
    ng = len(gparts)
    r = rr // ng
    lanes = -(-c // 128) * 128
    tr = r
    while tr * lanes * 4 * N_DEV > (4 << 20) and tr % 16 == 0:
        tr //= 2
    nb = r // tr

    def body(w_ref, *rest):
        g_refs, (m_ref, v_ref, go_ref, d_ref, mo_ref, vo_ref) = rest[:ng], rest[ng:]
        layer = pl.program_id(0)
        g = jnp.zeros(m_ref.shape, F32)
        for li in range(ng):
            gl = g_refs[li][0].astype(F32)
            for s in range(1, N_DEV):
                gl = gl + g_refs[li][s].astype(F32)
            g = jnp.where(layer == li, gl, g)
        mn = ADAM_B1 * m_ref[...] + (1.0 - ADAM_B1) * g
        vn = ADAM_B2 * v_ref[...] + (1.0 - ADAM_B2) * g * g
        m_hat = mn / (1.0 - ADAM_B1 ** ADAM_STEP)
        v_hat = vn / (1.0 - ADAM_B2 ** ADAM_STEP)
        go_ref[...] = g
        d_ref[...] = -ADAM_LR * (m_hat / (jnp.sqrt(v_hat) + ADAM_EPS) + ADAM_WD * w_ref[...])
        mo_ref[...] = mn
        vo_ref[...] = vn

    row = pl.BlockSpec((tr, c), lambda li, i: (li * nb + i, 0))
    part_specs = [pl.BlockSpec((N_DEV, tr, c), functools.partial(lambda li, i, k: (0, jnp.where(li == k, i, 0), 0), k=k))
                  for k in range(ng)]
    return pl.pallas_call(
        body, name=name, grid=(ng, nb), in_specs=[row] + part_specs + [row, row],
        out_specs=[row] * 4, out_shape=[jax.ShapeDtypeStruct((rr, c), F32)] * 4, compiler_params=_params(("arbitrary", "arbitrary")),
    )(w, *gparts, m, v)


PACK_LANES = 128
PACK_ROWS = 8192


PACK_TILE = 8 * PACK_LANES


def _pack_rows(t):
    return -(-t.size // PACK_TILE) * 8


def _pack(vals):
    rows = []
    for t in vals:
        flat = t.reshape(-1)
        n_rows = _pack_rows(t)
        rows.append(jnp.pad(flat, (0, n_rows * PACK_LANES - flat.size)).reshape(n_rows, PACK_LANES))
    used = sum(r.shape[0] for r in rows)
    assert used <= PACK_ROWS, used
    return jnp.concatenate(rows + [jnp.zeros((PACK_ROWS - used, PACK_LANES), F32)], axis=0)


def _unpack(packed, like):
    out, off = [], 0
    for t in like:
        n_rows = _pack_rows(t)
        out.append(packed[off:off + n_rows].reshape(-1)[:t.size].reshape(t.shape))
        off += n_rows
    return out


def kernel(x, w_in, s5_lambda_re, s5_lambda_im, s5_log_step, s5_b_re, s5_b_im, s5_c_re, s5_c_im, s5_d, s5_glu_w, s5_glu_b, sgu_norm_g, sgu_norm_b, sgu_w, sgu_b, pool_w, pool_scale, dn_conv_w, dn_a_log, dn_dt_bias, dn_norm_g, w_out, ln1_g, ln1_b, w_up, w_down, ln2_g, ln2_b, loss_target, m_w_in, m_s5_lambda_re, m_s5_lambda_im, m_s5_log_step, m_s5_b_re, m_s5_b_im, m_s5_c_re, m_s5_c_im, m_s5_d, m_s5_glu_w, m_s5_glu_b, m_sgu_norm_g, m_sgu_norm_b, m_sgu_w, m_sgu_b, m_pool_w, m_pool_scale, m_dn_conv_w, m_dn_a_log, m_dn_dt_bias, m_dn_norm_g, m_w_out, m_ln1_g, m_ln1_b, m_w_up, m_w_down, m_ln2_g, m_ln2_b, v_w_in, v_s5_lambda_re, v_s5_lambda_im, v_s5_log_step, v_s5_b_re, v_s5_b_im, v_s5_c_re, v_s5_c_im, v_s5_d, v_s5_glu_w, v_s5_glu_b, v_sgu_norm_g, v_sgu_norm_b, v_sgu_w, v_sgu_b, v_pool_w, v_pool_scale, v_dn_conv_w, v_dn_a_log, v_dn_dt_bias, v_dn_norm_g, v_w_out, v_ln1_g, v_ln1_b, v_w_up, v_w_down, v_ln2_g, v_ln2_b):
    names = ("w_in", "s5_lambda_re", "s5_lambda_im", "s5_log_step", "s5_b_re", "s5_b_im", "s5_c_re", "s5_c_im", "s5_d", "s5_glu_w",
             "s5_glu_b", "sgu_norm_g", "sgu_norm_b", "sgu_w", "sgu_b", "pool_w", "pool_scale", "dn_conv_w", "dn_a_log", "dn_dt_bias",
             "dn_norm_g", "w_out", "ln1_g", "ln1_b", "w_up", "w_down", "ln2_g", "ln2_b")
    env = locals()
    w = {n: env[n] for n in names}
    m = {n: env["m_" + n] for n in names}
    v = {n: env["v_" + n] for n in names}

    wire = [{n: (w[n][i] if n == "dn_conv_w" else w[n][i].astype(MXU_DTYPE)) for n in SHARDED} for i in range(DEPTH)]
    small = [{n: w[n][i] for n in SMALL} for i in range(DEPTH)]
    ops = [_prep_small(small[i]) for i in range(DEPTH)]
    grads = [{} for _ in range(DEPTH)]
    recv = [{} for _ in range(DEPTH)]
    first = ("w_in", "s5_glu_w", "dn_conv_w", "w_out")

    def gather(layer, group):
        def take(outs):
            for n, t in zip(group, outs):
                ops[layer].update(_weight_views(n, t))
        return (lambda: [(wire[layer][n], False) for n in group]), take

    def scatter(layer, group, with_small=False):
        def make():
            sends = [(grads[layer][n], True) for n in group]
            if with_small:
                sends.append((_pack([jnp.stack([grads[i]["small"][n] for i in range(DEPTH)]) for n in SMALL]), False))
            return sends
        def take(outs):
            recv[layer].update(dict(zip(group + (("small",) if with_small else ()), outs)))
        return make, take

    make, take = gather(0, first[:1])
    take(_exchange(make(), "gather_first"))
    fwd_hosts = {"l0_proj": gather(0, first[1:] + ("w_up",)), "l0_up": gather(0, ("w_down",)), "l0_down": gather(1, first),
                 "l1_proj": gather(1, ("w_up",)), "l1_up": gather(1, ("w_down",))}
    late = ("w_in", "s5_glu_w", "dn_conv_w")
    bwd_hosts = {"l1b_dpre": scatter(1, ("w_down",)), "l1b_dx1": scatter(1, ("w_up",)), "l1b_dmixed": scatter(1, ("w_out",)),
                 "l0b_dw_down": scatter(1, late),
                 "l0b_dpre": scatter(0, ("w_down",)), "l0b_dx1": scatter(0, ("w_up",)), "l0b_dmixed": scatter(0, ("w_out",)),
                 "l0b_dw_main": scatter(0, ("s5_glu_w", "dn_conv_w"), with_small=True), "l0b_dx": scatter(0, ("w_in",))}
    loss, grad_x = _local_step(x[0], loss_target[0], ops, small, fwd_hosts, bwd_hosts, grads)

    g_out, d_out, m_out, v_out = {}, {}, {}, {}
    for n in SHARDED:
        shp = w[n].shape
        pad = (-shp[1]) % 8
        def rows(t):
            return jnp.pad(t, ((0, 0), (0, pad), (0, 0))).reshape(shp[0] * (shp[1] + pad), shp[2])
        res = _adamw(rows(w[n]), [recv[i][n] for i in range(DEPTH)], rows(m[n]), rows(v[n]), "adamw_" + n)
        g_out[n], d_out[n], m_out[n], v_out[n] = (t.reshape(shp[0], shp[1] + pad, shp[2])[:, :shp[1]] for t in res)
    like = [w[n] for n in SMALL]
    res = _adamw(_pack(like), [recv[0]["small"]], _pack([m[n] for n in SMALL]), _pack([v[n] for n in SMALL]), "adamw_small")
    for dst, pk in zip((g_out, d_out, m_out, v_out), res):
        dst.update(dict(zip(SMALL, _unpack(pk, like))))

    total = lax.psum(loss[0, 0], MESH_AXES)
    return (total, grad_x[None], *[g_out[n] for n in names], *[d_out[n] for n in names],
            *[m_out[n] for n in names], *[v_out[n] for n in names])
```

```python
import functools
import math

import jax
import jax.numpy as jnp
from jax import lax
from jax.experimental import pallas as pl
from jax.experimental.pallas import tpu as pltpu

F32 = jnp.float32
BF16 = jnp.bfloat16
MXU_DTYPE = jnp.bfloat16
HI = lax.Precision.HIGHEST

N_DEV = 8
D_MODEL = 2048
DEPTH = 2
GROUP_WIDTH = 512
S5_GROUPS, S5_CH, S5_STATE = 32, 16, 64
S5_NS = S5_GROUPS * S5_STATE
SGU_CHUNK, SGU_HEADS = 128, 8
POOL_WINDOWS = (2, 4, 8, 16)
DN_HEADS, DN_HEAD_DIM, DN_CONV, DN_CHUNK = 4, 128, 4, 64
D_FF = 4 * D_MODEL
LN_EPS, RMS_EPS, L2_EPS = 1e-5, 1e-6, 1e-6
ALPHA = (2 * DEPTH) ** 0.25
MAIN_COLS = 4096
AB_PAD = 128
ADAM_LR, ADAM_B1, ADAM_B2, ADAM_EPS, ADAM_WD, ADAM_STEP = 0.001, 0.9, 0.999, 1e-08, 0.01, 10
VMEM_LIMIT = 56 * 1024 * 1024
MIX_ROWS = 1024
SGU_ROWS = 512
WIDE_ROWS = 512
C_GELU = math.sqrt(2.0 / math.pi)


def _params(sem=None):
    return pltpu.CompilerParams(dimension_semantics=sem, vmem_limit_bytes=VMEM_LIMIT)


def _gelu(x):
    return 0.5 * x * (1.0 + jnp.tanh(C_GELU * (x + 0.044715 * x * x * x)))


def _gelu_grad(x):
    t = jnp.tanh(C_GELU * (x + 0.044715 * x * x * x))
    return 0.5 * (1.0 + t) + 0.5 * x * (1.0 - t * t) * C_GELU * (1.0 + 3.0 * 0.044715 * x * x)


def _sigmoid(x):
    return 1.0 / (1.0 + jnp.exp(-x))


def _silu(x):
    return x * _sigmoid(x)


def _silu_grad(x):
    s = _sigmoid(x)
    return s * (1.0 + x * (1.0 - s))


def _softplus(x):
    z = jnp.exp(-jnp.abs(x))
    small = z * (1.0 - z * (0.5 - z * (1.0 / 3.0)))
    return jnp.maximum(x, 0.0) + jnp.where(z < 1e-2, small, jnp.log(1.0 + z))


def _mx(x):
    return x.astype(MXU_DTYPE)


def _dot(a, b, dims="nn", precision=None):
    cd = {"nn": ((1,), (0,)), "nt": ((1,), (1,)), "tn": ((0,), (0,))}[dims]
    return lax.dot_general(a, b, (cd, ((), ())), preferred_element_type=F32, precision=precision)


def _mdot(a, b, dims="nn"):
    return _dot(_mx(a), _mx(b), dims)


MESH_AXES = ("x", "y", "c")
OFFSETS = [(dx, dy, dc) for dx in (0, 1) for dy in (0, 1) for dc in (0, 1)][1:]


def _me_and_peers():
    x, y, c = (lax.axis_index(a) for a in MESH_AXES)
    def flip(v, d):
        return 1 - v if d else v
    peers = [(flip(x, dx), flip(y, dy), flip(c, dc)) for dx, dy, dc in OFFSETS]
    def idx(p):
        return 4 * p[0] + 2 * p[1] + p[2]
    return idx((x, y, c)), peers, [idx(p) for p in peers]


SIBLING = OFFSETS.index((0, 0, 1))
SAME_CORE = [OFFSETS.index(f) for f in ((0, 1, 0), (1, 0, 0), (1, 1, 0))]


class _Comm:
    def __init__(self, ops):
        self.arrays = [a for a, _ in ops]
        self.scatter = [s for _, s in ops]
        self.n = n = len(ops)
        hbm = pl.BlockSpec(memory_space=pltpu.HBM)
        self.in_specs, self.out_specs = [hbm] * n, [hbm] * n
        self.out_shape = [jax.ShapeDtypeStruct(a.shape if s else (N_DEV,) + a.shape, a.dtype) for a, s in ops]
        npeer = len(OFFSETS)
        self.scratch = [pltpu.SemaphoreType.DMA((n, npeer)), pltpu.SemaphoreType.DMA((n, npeer)), pltpu.SemaphoreType.DMA((n,))]

    def _plan(self, ins, outs, sems, waiting):
        send_sems, recv_sems, local_sems = sems
        me, peers, peer_idx = _me_and_peers()

        def remote(k, d, src, dst, to):
            return pltpu.make_async_remote_copy(src_ref=src, dst_ref=dst, send_sem=send_sems.at[k, d], recv_sem=recv_sems.at[k, d],
                                                device_id=to, device_id_type=pl.DeviceIdType.MESH)
        plan = []
        for k in range(self.n):
            every = range(len(OFFSETS))
            if self.scatter[k]:
                local = pltpu.make_async_copy(ins[k].at[me], outs[k].at[me], local_sems.at[k])
                pushes = [remote(k, d, ins[k].at[peer_idx[d]], outs[k].at[me], peers[d]) for d in every]
                onward = []
            else:
                local = pltpu.make_async_copy(ins[k], outs[k].at[me], local_sems.at[k])
                pushes = [remote(k, d, ins[k], outs[k].at[me], peers[d]) for d in [SIBLING] + SAME_CORE]
                onward = SAME_CORE
            passed, arrivals = [], {}
            if waiting:
                passed = [(d, remote(k, d + 1, outs[k].at[peer_idx[d]], outs[k].at[peer_idx[d]], peers[SIBLING])) for d in onward]
                arrivals = {d: remote(k, d, outs[k].at[peer_idx[d]], outs[k].at[peer_idx[d]], peers[d]) for d in every}
            plan.append((local, pushes, passed, arrivals))
        return plan

    def start(self, ins, outs, sems):
        for local, pushes, _, _ in self._plan(ins, outs, sems, False):
            local.start()
            for cp in pushes:
                cp.start()

    def wait(self, ins, outs, sems):
        plan = self._plan(ins, outs, sems, True)
        for _, _, passed, arrivals in plan:
            for d, onward in passed:
                arrivals.pop(d).wait_recv()
                onward.start()
        for local, pushes, passed, arrivals in plan:
            for cp in arrivals.values():
                cp.wait_recv()
            for cp in pushes + [onward for _, onward in passed]:
                cp.wait_send()
            local.wait()


def _exchange(ops, name):
    cm = _Comm(ops)

    def body(*refs):
        ins, outs, sems = refs[:cm.n], refs[cm.n:2 * cm.n], refs[2 * cm.n:]
        cm.start(ins, outs, sems)
        cm.wait(ins, outs, sems)

    return pl.pallas_call(body, name=name, in_specs=cm.in_specs, out_specs=cm.out_specs, out_shape=cm.out_shape,
                          scratch_shapes=cm.scratch)(*cm.arrays)


def _matmul(a, b, *, mode, tm, tn, tk, out_dtype, name, a_fn=None, extras=(), epi=None, a_cols=None,
            b_slab=None, out_slab=None, comm=None):
    a_shape = a.shape if a_cols is None else (a.shape[0], a_cols)
    b_shape = b.shape if b_slab is None else (b.shape[1], N_DEV * b_slab)
    if mode == "nn":
        (m, k), n = a_shape, b_shape[1]
    elif mode == "nt":
        (m, k), n = a_shape, b_shape[0]
    else:
        (k, m), n = a_shape, b_shape[1]
    tm, tn, tk = min(tm, m), min(tn, n), min(tk, k)
    if b_slab is not None:
        tn, tk = (tn, min(tk, b_slab)) if mode == "nt" else (min(tn, b_slab), tk)
    assert m % tm == 0 and n % tn == 0 and k % tk == 0, (name, a.shape, b.shape, tm, tn, tk)
    gi, gj, nk = m // tm, n // tn, k // tk
    n_ex = len(extras)
    cm = _Comm(comm) if comm else None
    nc = cm.n if cm else 0

    def body(a_ref, b_ref, *rest):
        ex_refs, rest = rest[:n_ex], rest[n_ex:]
        c_ins, o_ref, c_outs, acc, sems = rest[:nc], rest[nc], rest[nc + 1:2 * nc + 1], rest[2 * nc + 1], rest[2 * nc + 2:]
        i, j, kk = pl.program_id(0), pl.program_id(1), pl.program_id(2)
        if cm:
            @pl.when((i == 0) & (j == 0) & (kk == 0))
            def _():
                cm.start(c_ins, c_outs, sems)

        av = a_ref[...]
        if a_fn is not None:
            av = a_fn(av)
        part = _dot(_mx(av), _mx(b_ref[...]), mode)

        def finish(r):
            if epi is not None:
                r = epi(r, *[e[...] for e in ex_refs])
            o_ref[...] = r.astype(out_dtype)

        if nk == 1:
            finish(part)
        else:
            @pl.when(kk == 0)
            def _():
                acc[...] = part

            @pl.when((kk > 0) & (kk < nk - 1))
            def _():
                acc[...] += part

            @pl.when(kk == nk - 1)
            def _():
                finish(acc[...] + part)

        if cm:
            @pl.when((i == gi - 1) & (j == gj - 1) & (kk == nk - 1))
            def _():
                cm.wait(c_ins, c_outs, sems)

    a_spec = pl.BlockSpec((tk, tm), lambda i, j, kk: (kk, i)) if mode == "tn" else pl.BlockSpec((tm, tk), lambda i, j, kk: (i, kk))
    if b_slab is None:
        b_spec = pl.BlockSpec((tn, tk), lambda i, j, kk: (j, kk)) if mode == "nt" else pl.BlockSpec((tk, tn), lambda i, j, kk: (kk, j))
    elif mode == "nt":
        assert b_slab % tk == 0
        b_spec = pl.BlockSpec((None, tn, tk), lambda i, j, kk: ((kk * tk) // b_slab, j, ((kk * tk) % b_slab) // tk))
    else:
        assert b_slab % tn == 0
        b_spec = pl.BlockSpec((None, tk, tn), lambda i, j, kk: ((j * tn) // b_slab, kk, ((j * tn) % b_slab) // tn))
    if out_slab is None:
        o_spec, o_shape = pl.BlockSpec((tm, tn), lambda i, j, kk: (i, j)), jax.ShapeDtypeStruct((m, n), out_dtype)
    else:
        assert out_slab % tn == 0 and n == N_DEV * out_slab
        o_spec = pl.BlockSpec((None, tm, tn), lambda i, j, kk: ((j * tn) // out_slab, i, ((j * tn) % out_slab) // tn))
        o_shape = jax.ShapeDtypeStruct((N_DEV, m, out_slab), out_dtype)
    ex_specs = [pl.BlockSpec(({None: tm, "tn": tn}.get(bs[0], bs[0]), tn if bs[1] is None else bs[1]),
                             functools.partial(lambda i, j, kk, f: f(i, j), f=im)) for (_, bs, im) in extras]
    res = pl.pallas_call(
        body,
        name=name,
        grid=(gi, gj, nk),
        in_specs=[a_spec, b_spec, *ex_specs] + (cm.in_specs if cm else []),
        out_specs=[o_spec] + (cm.out_specs if cm else []),
        out_shape=[o_shape] + (cm.out_shape if cm else []),
        scratch_shapes=[pltpu.VMEM((tm, tn) if nk > 1 else (8, 128), F32)] + (cm.scratch if cm else []),
        compiler_params=_params(("arbitrary",) * 3 if cm else ("parallel", "parallel", "arbitrary")),
    )(a, b, *[e[0] for e in extras], *(cm.arrays if cm else []))
    return (res[0], res[1:]) if cm else res[0]


def _ln_fwd(x, y, g, b, name):
    l, d = x.shape
    tl = min(WIDE_ROWS, l)

    def body(x_ref, y_ref, g_ref, b_ref, h_ref, o_ref, om_ref):
        h = ALPHA * x_ref[...] + y_ref[...]
        mu = jnp.mean(h, axis=-1, keepdims=True)
        c = h - mu
        var = jnp.mean(c * c, axis=-1, keepdims=True)
        h_ref[...] = h
        out = c * lax.rsqrt(var + LN_EPS) * g_ref[...] + b_ref[...]
        o_ref[...] = out
        om_ref[...] = out.astype(om_ref.dtype)

    row = pl.BlockSpec((tl, d), lambda i: (i, 0))
    vec = pl.BlockSpec((1, d), lambda i: (0, 0))
    return pl.pallas_call(
        body, name=name, grid=(l // tl,), in_specs=[row, row, vec, vec], out_specs=[row, row, row],
        out_shape=[jax.ShapeDtypeStruct((l, d), F32)] * 2 + [jax.ShapeDtypeStruct((l, d), MXU_DTYPE)],
        compiler_params=_params(("parallel",)),
    )(x, y, g, b)


def _ln_loss(x, y, g, b, target, name):
    l, d = x.shape
    tl = min(WIDE_ROWS, l)

    def body(x_ref, y_ref, g_ref, b_ref, t_ref, h_ref, loss_ref, dy_ref):
        @pl.when(pl.program_id(0) == 0)
        def _():
            loss_ref[...] = jnp.zeros_like(loss_ref)

        h = ALPHA * x_ref[...] + y_ref[...]
        mu = jnp.mean(h, axis=-1, keepdims=True)
        c = h - mu
        var = jnp.mean(c * c, axis=-1, keepdims=True)
        h_ref[...] = h
        e = c * lax.rsqrt(var + LN_EPS) * g_ref[...] + b_ref[...] - t_ref[...]
        dy_ref[...] = e * (1.0 / d)
        loss_ref[...] += jnp.sum(jnp.sum(e * e, axis=1, keepdims=True), axis=0, keepdims=True) * (0.5 / d)

    row = pl.BlockSpec((tl, d), lambda i: (i, 0))
    vec = pl.BlockSpec((1, d), lambda i: (0, 0))
    return pl.pallas_call(
        body, name=name, grid=(l // tl,), in_specs=[row, row, vec, vec, row],
        out_specs=[row, pl.BlockSpec((1, 1), lambda i: (0, 0)), row],
        out_shape=[jax.ShapeDtypeStruct((l, d), F32), jax.ShapeDtypeStruct((1, 1), F32), jax.ShapeDtypeStruct((l, d), F32)],
        compiler_params=_params(("arbitrary",)),
    )(x, y, g, b, target)


def _ln_bwd(dout, h, g, name):
    l, d = h.shape
    tl = min(WIDE_ROWS, l)

    def body(do_ref, h_ref, g_ref, dh_ref, dhm_ref, dg_ref, db_ref):
        @pl.when(pl.program_id(0) == 0)
        def _():
            dg_ref[...] = jnp.zeros_like(dg_ref)
            db_ref[...] = jnp.zeros_like(db_ref)

        hv, do = h_ref[...], do_ref[...]
        mu = jnp.mean(hv, axis=-1, keepdims=True)
        c = hv - mu
        r = lax.rsqrt(jnp.mean(c * c, axis=-1, keepdims=True) + LN_EPS)
        xh = c * r
        dxh = do * g_ref[...]
        m1 = jnp.mean(dxh, axis=-1, keepdims=True)
        m2 = jnp.mean(dxh * xh, axis=-1, keepdims=True)
        dh = r * (dxh - m1 - xh * m2)
        dh_ref[...] = dh
        dhm_ref[...] = dh.astype(dhm_ref.dtype)
        dg_ref[...] += jnp.sum(do * xh, axis=0, keepdims=True)
        db_ref[...] += jnp.sum(do, axis=0, keepdims=True)

    row = pl.BlockSpec((tl, d), lambda i: (i, 0))
    vec = pl.BlockSpec((1, d), lambda i: (0, 0))
    return pl.pallas_call(
        body, name=name, grid=(l // tl,), in_specs=[row, row, vec], out_specs=[row, row, vec, vec],
        out_shape=[jax.ShapeDtypeStruct((l, d), F32), jax.ShapeDtypeStruct((l, d), MXU_DTYPE),
                   jax.ShapeDtypeStruct((1, d), F32), jax.ShapeDtypeStruct((1, d), F32)],
        compiler_params=_params(("arbitrary",)),
    )(dout, h, g)


def _loss_head(y, target):
    l, d = y.shape
    tl = min(WIDE_ROWS, l)

    def body(y_ref, t_ref, loss_ref, dy_ref):
        @pl.when(pl.program_id(0) == 0)
        def _():
            loss_ref[...] = jnp.zeros_like(loss_ref)

        e = y_ref[...] - t_ref[...]
        dy_ref[...] = e * (1.0 / d)
        s = jnp.sum(jnp.sum(e * e, axis=1, keepdims=True), axis=0, keepdims=True)
        loss_ref[...] += s * (0.5 / d)

    row = pl.BlockSpec((tl, d), lambda i: (i, 0))
    return pl.pallas_call(
        body, name="loss_head", grid=(l // tl,), in_specs=[row, row],
        out_specs=[pl.BlockSpec((1, 1), lambda i: (0, 0)), row],
        out_shape=[jax.ShapeDtypeStruct((1, 1), F32), jax.ShapeDtypeStruct((l, d), F32)],
        compiler_params=_params(("arbitrary",)),
    )(y, target)


def _s5_discretize(lam_re, lam_im, log_step, b_re, b_im):
    step = jnp.exp(log_step)[:, None]
    e = jnp.exp(lam_re * step)
    lbr, lbi = e * jnp.cos(lam_im * step), e * jnp.sin(lam_im * step)
    den = lam_re * lam_re + lam_im * lam_im
    qr = ((lbr - 1.0) * lam_re + lbi * lam_im) / den
    qi = (lbi * lam_re - (lbr - 1.0) * lam_im) / den
    bbr = qr[:, :, None] * b_re - qi[:, :, None] * b_im
    bbi = qr[:, :, None] * b_im + qi[:, :, None] * b_re
    return lbr, lbi, bbr, bbi


S5_TILES, S5_SLABS = 4, 8
S5_TILE_W, S5_SLAB_W = GROUP_WIDTH // S5_TILES, S5_NS // S5_TILES
S5_GPT = S5_GROUPS // S5_TILES


def _s5_compact(bbr, bbi, c_re, c_im):
    eye = jnp.eye(S5_GPT, dtype=F32)
    def bd(t):
        return jnp.einsum("tgph,gk->tghkp", t.reshape(S5_TILES, S5_GPT, S5_STATE, S5_CH), eye).reshape(S5_TILES, S5_TILE_W, S5_SLAB_W)
    def cd(t):
        return jnp.einsum("tghp,gk->tgpkh", t.reshape(S5_TILES, S5_GPT, S5_CH, S5_STATE), eye).reshape(S5_TILES, S5_SLAB_W, S5_TILE_W)
    return jnp.concatenate([bd(bbr), bd(bbi)], axis=0), jnp.concatenate([cd(c_re), -cd(c_im)], axis=0)


def _s5_uncompact_b(db):
    eye = jnp.eye(S5_GPT, dtype=F32)[None, :, None, :, None]
    def ex(t):
        d = jnp.sum(t.reshape(S5_TILES, S5_GPT, S5_CH, S5_GPT, S5_STATE) * eye, axis=3)
        return jnp.transpose(d, (0, 1, 3, 2)).reshape(S5_GROUPS, S5_STATE, S5_CH)
    return ex(db[:S5_TILES]), ex(db[S5_TILES:])


def _s5_uncompact_c(dc):
    eye = jnp.eye(S5_GPT, dtype=F32)[None, :, None, :, None]
    def ex(t):
        d = jnp.sum(t.reshape(S5_TILES, S5_GPT, S5_STATE, S5_GPT, S5_CH) * eye, axis=3)
        return jnp.transpose(d, (0, 1, 3, 2)).reshape(S5_GROUPS, S5_CH, S5_STATE)
    return ex(dc[:S5_TILES]), -ex(dc[S5_TILES:])


S5_ROWS = 512


def _s5_tile(j):
    t = j % S5_TILES
    return slice(t * S5_TILE_W, (t + 1) * S5_TILE_W)


def _s5_slab(j):
    return slice(j * S5_SLAB_W, (j + 1) * S5_SLAB_W)


def _s5_recur(src, lam_ref, carry, emit, n_rows, reverse, extra=()):
    ns = S5_NS
    lr, li = lam_ref[:, :ns], lam_ref[:, ns:]

    def step(t, c):
        row = (n_rows - 1 - t) if reverse else t
        cr, ci = c[0], c[1]
        nr = lr * cr - li * ci + src[pl.ds(row, 1), :ns]
        ni = lr * ci + li * cr + src[pl.ds(row, 1), ns:]
        return (nr, ni) + tuple(emit(row, nr, ni, cr, ci, c[2:]))

    fin = lax.fori_loop(0, n_rows, step, (carry[:, :ns], carry[:, ns:]) + tuple(extra))
    carry[:, :ns] = fin[0]
    carry[:, ns:] = fin[1]
    return fin[2:]


def _s5_fwd(proj, b, c, lam, d, name):
    l = proj.shape[0]
    tl = min(S5_ROWS, l)
    w = 2 * S5_NS

    def body(u_ref, b_ref, c_ref, lam_ref, d_ref, hs_ref, y_ref, bu, carry):
        @pl.when(pl.program_id(0) == 0)
        def _():
            carry[...] = jnp.zeros_like(carry)

        u = u_ref[...]
        um = _mx(u)
        for j in range(S5_SLABS):
            bu[:, _s5_slab(j)] = _dot(um[:, _s5_tile(j)], b_ref[j])

        def emit(row, nr, ni, cr, ci, extra):
            hs_ref[pl.ds(row, 1), :S5_NS] = nr
            hs_ref[pl.ds(row, 1), S5_NS:] = ni
            return extra

        _s5_recur(bu, lam_ref, carry, emit, tl, False)
        for t in range(S5_TILES):
            acc = _dot(_mx(hs_ref[:, _s5_slab(t)]), c_ref[t]) + _dot(_mx(hs_ref[:, _s5_slab(S5_TILES + t)]), c_ref[S5_TILES + t])
            y_ref[:, _s5_tile(t)] = acc + d_ref[:, _s5_tile(t)] * u[:, _s5_tile(t)]

    row = lambda width: pl.BlockSpec((tl, width), lambda i: (i, 0))
    full = lambda a: pl.BlockSpec(a.shape, lambda i: (0,) * a.ndim)
    return pl.pallas_call(
        body, name=name, grid=(l // tl,), in_specs=[row(GROUP_WIDTH), full(b), full(c), full(lam), full(d)],
        out_specs=[row(w), row(GROUP_WIDTH)],
        out_shape=[jax.ShapeDtypeStruct((l, w), F32), jax.ShapeDtypeStruct((l, GROUP_WIDTH), F32)],
        scratch_shapes=[pltpu.VMEM((tl, w), F32), pltpu.VMEM((1, w), F32)], compiler_params=_params(("arbitrary",)),
    )(proj, b, c, lam, d)


def _s5_bwd(dy, hs, proj, b, c, lam_conj, d, name):
    l = dy.shape[0]
    tl = min(S5_ROWS, l)
    nb = l // tl
    w = 2 * S5_NS

    def body(dy_ref, hs_ref, u_ref, b_ref, c_ref, lam_ref, d_ref, du_ref, db_ref, dc_ref, dl_ref, dh, adj, carry):
        @pl.when(pl.program_id(0) == 0)
        def _():
            carry[...] = jnp.zeros_like(carry)
            db_ref[...] = jnp.zeros_like(db_ref)
            dc_ref[...] = jnp.zeros_like(dc_ref)
            dl_ref[...] = jnp.zeros_like(dl_ref)

        dyv = dy_ref[...]
        dym, um = _mx(dyv), _mx(u_ref[...])
        for j in range(S5_SLABS):
            dh[:, _s5_slab(j)] = _dot(dym[:, _s5_tile(j)], c_ref[j], "nt")

        def emit(row, nr, ni, cr, ci, extra):
            adj[pl.ds(row, 1), :S5_NS] = nr
            adj[pl.ds(row, 1), S5_NS:] = ni
            hr, hi = hs_ref[pl.ds(row, 1), :S5_NS], hs_ref[pl.ds(row, 1), S5_NS:]
            return extra[0] + cr * hr + ci * hi, extra[1] + ci * hr - cr * hi

        dl = _s5_recur(dh, lam_ref, carry, emit, tl, True, extra=(dl_ref[:, :S5_NS], dl_ref[:, S5_NS:]))
        dl_ref[:, :S5_NS] = dl[0]
        dl_ref[:, S5_NS:] = dl[1]
        for t in range(S5_TILES):
            acc = (_dot(_mx(adj[:, _s5_slab(t)]), b_ref[t], "nt")
                   + _dot(_mx(adj[:, _s5_slab(S5_TILES + t)]), b_ref[S5_TILES + t], "nt"))
            du_ref[:, _s5_tile(t)] = (acc + d_ref[:, _s5_tile(t)] * dyv[:, _s5_tile(t)]).astype(du_ref.dtype)
        for j in range(S5_SLABS):
            dc_ref[j] += _dot(_mx(hs_ref[:, _s5_slab(j)]), dym[:, _s5_tile(j)], "tn")
            db_ref[j] += _dot(um[:, _s5_tile(j)], _mx(adj[:, _s5_slab(j)]), "tn")

    row = lambda width: pl.BlockSpec((tl, width), lambda i: (nb - 1 - i, 0))
    full = lambda a: pl.BlockSpec(a.shape, lambda i: (0,) * a.ndim)
    acc3 = lambda shape: pl.BlockSpec(shape, lambda i: (0, 0, 0))
    return pl.pallas_call(
        body, name=name, grid=(nb,),
        in_specs=[row(GROUP_WIDTH), row(w), row(GROUP_WIDTH), full(b), full(c), full(lam_conj), full(d)],
        out_specs=[row(GROUP_WIDTH), acc3(b.shape), acc3(c.shape), pl.BlockSpec((1, w), lambda i: (0, 0))],
        out_shape=[jax.ShapeDtypeStruct((l, GROUP_WIDTH), BF16), jax.ShapeDtypeStruct(b.shape, F32),
                   jax.ShapeDtypeStruct(c.shape, F32), jax.ShapeDtypeStruct((1, w), F32)],
        scratch_shapes=[pltpu.VMEM((tl, w), F32), pltpu.VMEM((tl, w), F32), pltpu.VMEM((1, w), F32)],
        compiler_params=_params(("arbitrary",)),
    )(dy, hs, proj, b, c, lam_conj, d)


def _s5_glu_fwd(y, glu_w, glu_b, name):
    l, d = y.shape
    tl = min(MIX_ROWS, l)

    def body(y_ref, w_ref, b_ref, o_ref):
        yg = _gelu(y_ref[...])
        z = _mdot(yg, w_ref[...]) + b_ref[...]
        o_ref[...] = (yg * _sigmoid(z)).astype(o_ref.dtype)

    return pl.pallas_call(
        body, name=name, grid=(l // tl,),
        in_specs=[pl.BlockSpec((tl, d), lambda i: (i, 0)), pl.BlockSpec((d, d), lambda i: (0, 0)), pl.BlockSpec((1, d), lambda i: (0, 0))],
        out_specs=pl.BlockSpec((tl, d), lambda i: (i, 0)), out_shape=jax.ShapeDtypeStruct((l, d), BF16),
        compiler_params=_params(("parallel",)),
    )(y, glu_w, glu_b)


def _s5_glu_bwd(dmixed, y, proj, glu_w, glu_b, name):
    l, d = y.shape
    tl = min(MIX_ROWS, l)

    def body(do_ref, y_ref, u_ref, w_ref, b_ref, dy_ref, dz_ref, yg_ref, db_ref, dd_ref):
        @pl.when(pl.program_id(0) == 0)
        def _():
            db_ref[...] = jnp.zeros_like(db_ref)
            dd_ref[...] = jnp.zeros_like(dd_ref)

        yv, do = y_ref[...], do_ref[...]
        yg = _gelu(yv)
        gate = _sigmoid(_mdot(yg, w_ref[...]) + b_ref[...])
        dz = do * yg * gate * (1.0 - gate)
        dyg = do * gate + _mdot(dz, w_ref[...], "nt")
        dy = dyg * _gelu_grad(yv)
        dy_ref[...] = dy
        dz_ref[...] = dz.astype(dz_ref.dtype)
        yg_ref[...] = yg.astype(yg_ref.dtype)
        db_ref[...] += jnp.sum(dz, axis=0, keepdims=True)
        dd_ref[...] += jnp.sum(dy * u_ref[...], axis=0, keepdims=True)

    row = pl.BlockSpec((tl, d), lambda i: (i, 0))
    vec = pl.BlockSpec((1, d), lambda i: (0, 0))
    return pl.pallas_call(
        body, name=name, grid=(l // tl,),
        in_specs=[row, row, row, pl.BlockSpec((d, d), lambda i: (0, 0)), vec],
        out_specs=[row, row, row, vec, vec],
        out_shape=[jax.ShapeDtypeStruct((l, d), F32), jax.ShapeDtypeStruct((l, d), BF16), jax.ShapeDtypeStruct((l, d), BF16),
                   jax.ShapeDtypeStruct((1, d), F32), jax.ShapeDtypeStruct((1, d), F32)],
        compiler_params=_params(("arbitrary",)),
    )(dmixed, y, proj, glu_w, glu_b)


def _sgu_pair(w_ref, x, j, dims):
    lo = lax.broadcasted_iota(jnp.int32, x.shape, 1) < (GROUP_WIDTH // SGU_HEADS)
    xb = _mx(x)
    r0 = _dot(w_ref[2 * j], xb, dims)
    r1 = _dot(w_ref[2 * j + 1], xb, dims)
    return jnp.where(lo, r0, r1)


def _sgu_norm(v, g, b):
    mu = jnp.mean(v, axis=-1, keepdims=True)
    c = v - mu
    r = lax.rsqrt(jnp.mean(c * c, axis=-1, keepdims=True) + LN_EPS)
    return c * r, r


def _sgu_fwd(proj, norm_g, norm_b, wm, bfull, name):
    l = proj.shape[0]
    tl = min(SGU_ROWS, l)
    gw = GROUP_WIDTH

    def body(zu_ref, zv_ref, g_ref, b_ref, w_ref, bf_ref, o_ref):
        for c in range(tl // SGU_CHUNK):
            rows = slice(c * SGU_CHUNK, (c + 1) * SGU_CHUNK)
            u = _gelu(zu_ref[rows, :])
            vh, _ = _sgu_norm(_gelu(zv_ref[rows, :]), None, None)
            vn = vh * g_ref[...] + b_ref[...]
            for j in range(gw // 128):
                cols = slice(j * 128, (j + 1) * 128)
                mixed = _sgu_pair(w_ref, vn[:, cols], j, "nn") + bf_ref[:, cols]
                o_ref[rows, cols] = (u[:, cols] * mixed).astype(o_ref.dtype)

    vec = pl.BlockSpec((1, gw), lambda i: (0, 0))
    return pl.pallas_call(
        body, name=name, grid=(l // tl,),
        in_specs=[pl.BlockSpec((tl, gw), lambda i: (i, 1)), pl.BlockSpec((tl, gw), lambda i: (i, 2)), vec, vec,
                  pl.BlockSpec((SGU_HEADS, SGU_CHUNK, SGU_CHUNK), lambda i: (0, 0, 0)), pl.BlockSpec((SGU_CHUNK, gw), lambda i: (0, 0))],
        out_specs=pl.BlockSpec((tl, gw), lambda i: (i, 0)), out_shape=jax.ShapeDtypeStruct((l, gw), BF16),
        compiler_params=_params(("parallel",)),
    )(proj, proj, norm_g, norm_b, wm, bfull)


def _sgu_bwd(dmixed, proj, norm_g, norm_b, wm, bfull, name):
    l = proj.shape[0]
    tl = min(SGU_ROWS, l)
    gw = GROUP_WIDTH
    hd = gw // SGU_HEADS

    def body(do_ref, zu_ref, zv_ref, g_ref, b_ref, w_ref, bf_ref, dzu_ref, dzv_ref, dw_ref, dbf_ref, dg_ref, dnb_ref):
        @pl.when(pl.program_id(0) == 0)
        def _():
            dw_ref[...] = jnp.zeros_like(dw_ref)
            dbf_ref[...] = jnp.zeros_like(dbf_ref)
            dg_ref[...] = jnp.zeros_like(dg_ref)
            dnb_ref[...] = jnp.zeros_like(dnb_ref)

        for c in range(tl // SGU_CHUNK):
            rows = slice(c * SGU_CHUNK, (c + 1) * SGU_CHUNK)
            zu, zv, do = zu_ref[rows, :], zv_ref[rows, :], do_ref[rows, :]
            u = _gelu(zu)
            vh, r = _sgu_norm(_gelu(zv), None, None)
            vn = vh * g_ref[...] + b_ref[...]
            dvn_parts, mixed_parts = [], []
            for j in range(gw // 128):
                cols = slice(j * 128, (j + 1) * 128)
                vb = vn[:, cols]
                mixed_parts.append(_sgu_pair(w_ref, vb, j, "nn") + bf_ref[:, cols])
                dm = do[:, cols] * u[:, cols]
                dvn_parts.append(_sgu_pair(w_ref, dm, j, "tn"))
                lo = lax.broadcasted_iota(jnp.int32, dm.shape, 1) < hd
                dw_ref[2 * j] += _mdot(jnp.where(lo, dm, 0.0), vb, "nt")
                dw_ref[2 * j + 1] += _mdot(jnp.where(lo, 0.0, dm), vb, "nt")
                dbf_ref[:, cols] += dm
            mixed = jnp.concatenate(mixed_parts, axis=1)
            dvn = jnp.concatenate(dvn_parts, axis=1)
            dzu_ref[rows, :] = (do * mixed * _gelu_grad(zu)).astype(dzu_ref.dtype)
            dg_ref[...] += jnp.sum(dvn * vh, axis=0, keepdims=True)
            dnb_ref[...] += jnp.sum(dvn, axis=0, keepdims=True)
            dvh = dvn * g_ref[...]
            m1 = jnp.mean(dvh, axis=-1, keepdims=True)
            m2 = jnp.mean(dvh * vh, axis=-1, keepdims=True)
            dv = r * (dvh - m1 - vh * m2)
            dzv_ref[rows, :] = (dv * _gelu_grad(zv)).astype(dzv_ref.dtype)

    vec = pl.BlockSpec((1, gw), lambda i: (0, 0))
    row = pl.BlockSpec((tl, gw), lambda i: (i, 0))
    wspec = pl.BlockSpec((SGU_HEADS, SGU_CHUNK, SGU_CHUNK), lambda i: (0, 0, 0))
    bspec = pl.BlockSpec((SGU_CHUNK, gw), lambda i: (0, 0))
    return pl.pallas_call(
        body, name=name, grid=(l // tl,),
        in_specs=[pl.BlockSpec((tl, gw), lambda i: (i, 1)), pl.BlockSpec((tl, gw), lambda i: (i, 1)), pl.BlockSpec((tl, gw), lambda i: (i, 2)),
                  vec, vec, wspec, bspec],
        out_specs=[row, row, wspec, bspec, vec, vec],
        out_shape=[jax.ShapeDtypeStruct((l, gw), BF16), jax.ShapeDtypeStruct((l, gw), BF16),
                   jax.ShapeDtypeStruct((SGU_HEADS, SGU_CHUNK, SGU_CHUNK), F32), jax.ShapeDtypeStruct((SGU_CHUNK, gw), F32),
                   jax.ShapeDtypeStruct((1, gw), F32), jax.ShapeDtypeStruct((1, gw), F32)],
        compiler_params=_params(("arbitrary",)),
    )(dmixed, proj, proj, norm_g, norm_b, wm, bfull)


HALO = 16


def _window_sums(ext, n_rows, forward):
    def sh(x, k):
        return pltpu.roll(x, (n_rows - k) if forward else k, axis=0)
    s2 = ext + sh(ext, 1)
    s4 = s2 + sh(s2, 2)
    s8 = s4 + sh(s4, 4)
    s16 = s8 + sh(s8, 8)
    return (s2, s4, s8, s16)


def _pool_fwd(proj, pool_w, scale, name):
    l = proj.shape[0]
    tl = min(MIX_ROWS, l)
    gw = GROUP_WIDTH
    pg = gw // len(POOL_WINDOWS)

    def body(x_ref, halo_ref, w_ref, s_ref, o_ref, p_ref):
        i = pl.program_id(0)
        x = x_ref[...]
        halo = jnp.where(i > 0, halo_ref[...], 0.0)
        ext = jnp.concatenate([halo, x], axis=0)
        sums = _window_sums(ext, tl + HALO, False)
        t = i * tl + lax.broadcasted_iota(jnp.int32, (tl, pg), 0)
        for gi, win in enumerate(POOL_WINDOWS):
            cols = slice(gi * pg, (gi + 1) * pg)
            cnt = jnp.minimum(t + 1, win).astype(F32)
            pooled = sums[gi][HALO:, cols] / cnt - x[:, cols]
            p_ref[:, cols] = pooled
            o_ref[:, cols] = (_mdot(pooled, w_ref[gi]) * s_ref[:, cols]).astype(o_ref.dtype)

    row = pl.BlockSpec((tl, gw), lambda i: (i, 0))
    return pl.pallas_call(
        body, name=name, grid=(l // tl,),
        in_specs=[pl.BlockSpec((tl, gw), lambda i: (i, 3)),
                  pl.BlockSpec((HALO, gw), lambda i: (jnp.maximum(i * (tl // HALO) - 1, 0), 3)),
                  pl.BlockSpec((len(POOL_WINDOWS), pg, pg), lambda i: (0, 0, 0)), pl.BlockSpec((1, gw), lambda i: (0, 0))],
        out_specs=[row, row], out_shape=[jax.ShapeDtypeStruct((l, gw), BF16), jax.ShapeDtypeStruct((l, gw), F32)],
        compiler_params=_params(("parallel",)),
    )(proj, proj, pool_w, scale)


def _pool_bwd_map(dmixed, pooled, pool_w, scale, name):
    l, gw = pooled.shape
    tl = min(MIX_ROWS, l)
    ng = len(POOL_WINDOWS)
    pg = gw // ng

    def body(do_ref, p_ref, w_ref, s_ref, dp_ref, dw_ref, ds_ref):
        @pl.when(pl.program_id(0) == 0)
        def _():
            dw_ref[...] = jnp.zeros_like(dw_ref)
            ds_ref[...] = jnp.zeros_like(ds_ref)

        for gi in range(ng):
            cols = slice(gi * pg, (gi + 1) * pg)
            do, pooled_g = do_ref[:, cols], p_ref[:, cols]
            mixed = _mdot(pooled_g, w_ref[gi])
            ds_ref[:, cols] += jnp.sum(do * mixed, axis=0, keepdims=True)
            dm = do * s_ref[:, cols]
            dw_ref[gi] += _mdot(pooled_g, dm, "tn")
            dp_ref[:, cols] = _mdot(dm, w_ref[gi], "nt")

    row = pl.BlockSpec((tl, gw), lambda i: (i, 0))
    wspec = pl.BlockSpec((ng, pg, pg), lambda i: (0, 0, 0))
    vec = pl.BlockSpec((1, gw), lambda i: (0, 0))
    return pl.pallas_call(
        body, name=name, grid=(l // tl,),
        in_specs=[pl.BlockSpec((tl, gw), lambda i: (i, 2)), row, wspec, vec], out_specs=[row, wspec, vec],
        out_shape=[jax.ShapeDtypeStruct((l, gw), F32), jax.ShapeDtypeStruct((ng, pg, pg), F32), jax.ShapeDtypeStruct((1, gw), F32)],
        compiler_params=_params(("arbitrary",)),
    )(dmixed, pooled, pool_w, scale)


def _pool_bwd_window(dpooled, name):
    l, gw = dpooled.shape
    tl = min(MIX_ROWS, l)
    nb = l // tl
    pg = gw // len(POOL_WINDOWS)

    def body(d_ref, halo_ref, o_ref):
        i = pl.program_id(0)
        d = d_ref[...]
        halo = jnp.where(i < nb - 1, halo_ref[...], 0.0)
        ext = jnp.concatenate([d, halo], axis=0)
        t = i * tl + lax.broadcasted_iota(jnp.int32, (tl + HALO, pg), 0)
        for gi, win in enumerate(POOL_WINDOWS):
            cols = slice(gi * pg, (gi + 1) * pg)
            cnt = jnp.minimum(t + 1, win).astype(F32)
            sums = _window_sums(ext[:, cols] / cnt, tl + HALO, True)
            o_ref[:, cols] = (sums[gi][:tl, :] - d[:, cols]).astype(o_ref.dtype)

    row = pl.BlockSpec((tl, gw), lambda i: (i, 0))
    return pl.pallas_call(
        body, name=name, grid=(nb,),
        in_specs=[row, pl.BlockSpec((HALO, gw), lambda i: (jnp.minimum((i + 1) * (tl // HALO), l // HALO - 1), 0))],
        out_specs=row, out_shape=jax.ShapeDtypeStruct((l, gw), BF16), compiler_params=_params(("parallel",)),
    )(dpooled, dpooled)


CONV_HALO = 8
QKV_BLK = 4


def _head_sums(x):
    parts = []
    for hd in range(DN_HEADS):
        s = jnp.sum(x[:, hd * DN_HEAD_DIM:(hd + 1) * DN_HEAD_DIM], axis=-1, keepdims=True)
        parts.append(jnp.broadcast_to(s, (x.shape[0], DN_HEAD_DIM)))
    return jnp.concatenate(parts, axis=1)


def _gdn_pre_fwd(proj, proj_ab, conv_w, a_log, dt_bias, name):
    l = proj.shape[0]
    tl = min(MIX_ROWS, l)
    gw = GROUP_WIDTH

    def body(xq, xk, xv, hq, hk, hv, w_ref, ab_ref, al_ref, dt_ref, qn_ref, kn_ref, v_ref, cq_ref, ck_ref, cv_ref, gb_ref):
        i = pl.program_id(0)
        for p, (x_ref, h_ref, c_ref) in enumerate(((xq, hq, cq_ref), (xk, hk, ck_ref), (xv, hv, cv_ref))):
            ext = jnp.concatenate([jnp.where(i > 0, h_ref[...], 0.0), x_ref[...]], axis=0)
            conv = jnp.zeros((tl, gw), F32)
            for j in range(DN_CONV):
                k = DN_CONV - 1 - j
                shifted = ext if k == 0 else pltpu.roll(ext, k, axis=0)
                conv = conv + shifted[CONV_HALO:, :] * w_ref[j:j + 1, p * gw:(p + 1) * gw]
            c_ref[...] = conv
            s = _silu(conv)
            if p == 2:
                v_ref[...] = s
            else:
                r = lax.rsqrt(_head_sums(s * s) + L2_EPS)
                (qn_ref if p == 0 else kn_ref)[...] = s * r * (DN_HEAD_DIM ** -0.5 if p == 0 else 1.0)
        ab = ab_ref[...]
        lane = lax.broadcasted_iota(jnp.int32, ab.shape, 1)
        g = -jnp.exp(al_ref[...]) * _softplus(ab + dt_ref[...])
        gb_ref[...] = jnp.where(lane < DN_HEADS, g, _sigmoid(ab))

    def xs(b):
        return pl.BlockSpec((tl, gw), lambda i: (i, b))

    def hs(b):
        return pl.BlockSpec((CONV_HALO, gw), lambda i: (jnp.maximum(i * (tl // CONV_HALO) - 1, 0), b))

    row = pl.BlockSpec((tl, gw), lambda i: (i, 0))
    abrow = pl.BlockSpec((tl, AB_PAD), lambda i: (i, 0))
    abvec = pl.BlockSpec((1, AB_PAD), lambda i: (0, 0))
    return pl.pallas_call(
        body, name=name, grid=(l // tl,),
        in_specs=[xs(QKV_BLK), xs(QKV_BLK + 1), xs(QKV_BLK + 2), hs(QKV_BLK), hs(QKV_BLK + 1), hs(QKV_BLK + 2),
                  pl.BlockSpec((DN_CONV, 3 * gw), lambda i: (0, 0)), abrow, abvec, abvec],
        out_specs=[row] * 6 + [abrow],
        out_shape=[jax.ShapeDtypeStruct((l, gw), F32)] * 6 + [jax.ShapeDtypeStruct((l, AB_PAD), F32)],
        compiler_params=_params(("parallel",)),
    )(proj, proj, proj, proj, proj, proj, conv_w, proj_ab, a_log, dt_bias)


def _gdn_pre_bwd(dq, dk, dv, cq, ck, cv, dgb, gb, proj_ab, a_log, dt_bias, name):
    l, gw = cq.shape
    tl = min(MIX_ROWS, l)

    def body(dq_ref, dk_ref, dv_ref, cq_ref, ck_ref, cv_ref, dgb_ref, gb_ref, ab_ref, al_ref, dt_ref,
             dcq_ref, dck_ref, dcv_ref, dab_ref, dal_ref, ddt_ref):
        @pl.when(pl.program_id(0) == 0)
        def _():
            dal_ref[...] = jnp.zeros_like(dal_ref)
            ddt_ref[...] = jnp.zeros_like(ddt_ref)

        for p, (d_ref, c_ref, o_ref) in enumerate(((dq_ref, cq_ref, dcq_ref), (dk_ref, ck_ref, dck_ref), (dv_ref, cv_ref, dcv_ref))):
            c, d = c_ref[...], d_ref[...]
            if p == 2:
                ds = d
            else:
                s = _silu(c)
                r = lax.rsqrt(_head_sums(s * s) + L2_EPS)
                ds = (DN_HEAD_DIM ** -0.5 if p == 0 else 1.0) * r * (d - s * r * r * _head_sums(d * s))
            o_ref[...] = ds * _silu_grad(c)
        ab, dgb_v, gb_v = ab_ref[...], dgb_ref[...], gb_ref[...]
        lane = lax.broadcasted_iota(jnp.int32, ab.shape, 1)
        is_g = lane < DN_HEADS
        dpre = dgb_v * (-jnp.exp(al_ref[...])) * _sigmoid(ab + dt_ref[...])
        dab_ref[...] = jnp.where(is_g, dpre, dgb_v * gb_v * (1.0 - gb_v)).astype(dab_ref.dtype)
        dal_ref[...] += jnp.sum(jnp.where(is_g, dgb_v * gb_v, 0.0), axis=0, keepdims=True)
        ddt_ref[...] += jnp.sum(jnp.where(is_g, dpre, 0.0), axis=0, keepdims=True)

    row = pl.BlockSpec((tl, gw), lambda i: (i, 0))
    abrow = pl.BlockSpec((tl, AB_PAD), lambda i: (i, 0))
    abvec = pl.BlockSpec((1, AB_PAD), lambda i: (0, 0))
    return pl.pallas_call(
        body, name=name, grid=(l // tl,),
        in_specs=[row] * 6 + [abrow, abrow, abrow, abvec, abvec],
        out_specs=[row, row, row, abrow, abvec, abvec],
        out_shape=[jax.ShapeDtypeStruct((l, gw), F32)] * 3 + [jax.ShapeDtypeStruct((l, AB_PAD), BF16),
                   jax.ShapeDtypeStruct((1, AB_PAD), F32), jax.ShapeDtypeStruct((1, AB_PAD), F32)],
        compiler_params=_params(("arbitrary",)),
    )(dq, dk, dv, cq, ck, cv, dgb, gb, proj_ab, a_log, dt_bias)


def _conv_bwd(dc, proj, col_blk, w_part, name):
    l, gw = dc.shape
    tl = min(MIX_ROWS, l)
    nb = l // tl

    def body(dc_ref, halo_ref, x_ref, w_ref, dx_ref, dw_ref):
        i = pl.program_id(0)

        @pl.when(i == 0)
        def _():
            dw_ref[...] = jnp.zeros_like(dw_ref)

        ext = jnp.concatenate([dc_ref[...], jnp.where(i < nb - 1, halo_ref[...], 0.0)], axis=0)
        x = x_ref[...]
        dx = jnp.zeros((tl, gw), F32)
        rid = lax.broadcasted_iota(jnp.int32, (8, gw), 0)
        dw = jnp.zeros((8, gw), F32)
        for j in range(DN_CONV):
            k = DN_CONV - 1 - j
            shifted = (ext if k == 0 else pltpu.roll(ext, tl + CONV_HALO - k, axis=0))[:tl, :]
            dx = dx + shifted * w_ref[j:j + 1, :]
            dw = dw + jnp.where(rid == j, jnp.sum(x * shifted, axis=0, keepdims=True), 0.0)
        dx_ref[...] = dx.astype(dx_ref.dtype)
        dw_ref[...] += dw

    row = pl.BlockSpec((tl, gw), lambda i: (i, 0))
    return pl.pallas_call(
        body, name=name, grid=(nb,),
        in_specs=[row, pl.BlockSpec((CONV_HALO, gw), lambda i: (jnp.minimum((i + 1) * (tl // CONV_HALO), l // CONV_HALO - 1), 0)),
                  pl.BlockSpec((tl, gw), lambda i: (i, col_blk)), pl.BlockSpec((DN_CONV, gw), lambda i: (0, 0))],
        out_specs=[row, pl.BlockSpec((8, gw), lambda i: (0, 0))],
        out_shape=[jax.ShapeDtypeStruct((l, gw), BF16), jax.ShapeDtypeStruct((8, gw), F32)],
        compiler_params=_params(("arbitrary",)),
    )(dc, dc, proj, w_part)


TERMS_CHUNKS = 8


def _bdot(a, b, dims="nn", precision=None):
    cd = {"nn": ((2,), (1,)), "nt": ((2,), (2,)), "tn": ((1,), (1,))}[dims]
    return lax.dot_general(a, b, (cd, ((0,), (0,))), preferred_element_type=F32, precision=precision)


def _bmdot(a, b, dims="nn"):
    return _bdot(_mx(a), _mx(b), dims)


def _bdot3(a, b, dims="nn"):
    ah, bh = a.astype(BF16), b.astype(BF16)
    al, bl = (a - ah.astype(F32)).astype(BF16), (b - bh.astype(F32)).astype(BF16)
    return _bdot(ah, bh, dims) + (_bdot(ah, bl, dims) + _bdot(al, bh, dims))


def _wy_terms(q, k, v, gcol, beta, t=None):
    c = DN_CHUNK
    ii = lax.broadcasted_iota(jnp.int32, (1, c, c), 1)
    jj = lax.broadcasted_iota(jnp.int32, (1, c, c), 2)
    tril, strict = ii >= jj, ii > jj
    grow = jnp.sum(jnp.where(ii == jj, gcol, 0.0), axis=1, keepdims=True)
    gc_col = jnp.sum(jnp.where(tril, grow, 0.0), axis=2, keepdims=True)
    gc_row = jnp.sum(jnp.where(ii <= jj, gcol, 0.0), axis=1, keepdims=True)
    dec = jnp.exp(jnp.where(tril, gc_col - gc_row, -1e30))
    kb, vb = k * beta, v * beta
    kk = _bmdot(kb, k, "nt")
    if t is None:
        a = jnp.where(strict, kk * dec, 0.0)
        d = jnp.where((ii >> 3) == (jj >> 3), a, 0.0)
        t = jnp.where(ii == jj, 1.0, 0.0) - d
        p = _bdot(d, d, precision=HI)
        t = t + _bdot(t, p, precision=HI)
        t = t + _bdot(t, _bdot(p, p, precision=HI), precision=HI)
        for sh in (3, 4, 5):
            below = ((ii >> (sh + 1)) == (jj >> (sh + 1))) & ((ii >> sh) > (jj >> sh))
            t = t - _bdot3(t, _bdot3(jnp.where(below, a, 0.0), t))
    eg = jnp.exp(gc_col)
    gc_last = gc_col[:, c - 1:c, :]
    kbg = kb * eg
    qk0 = _bmdot(q, k, "nt")
    e2 = jnp.exp(gc_last - gc_col)
    return dict(ii=ii, jj=jj, tril=tril, strict=strict, dec=dec, kb=kb, vb=vb, kk=kk, t=t, eg=eg, kbg=kbg,
                qk0=qk0, qk=jnp.where(tril, qk0 * dec, 0.0), qg=q * eg, e2=e2, kt=k * e2, gl=jnp.exp(gc_last))


def _to_heads(x, g):
    return jnp.concatenate([x[:, h * DN_HEAD_DIM:(h + 1) * DN_HEAD_DIM].reshape(g, DN_CHUNK, DN_HEAD_DIM)
                            for h in range(DN_HEADS)], axis=0)


def _from_heads(t, ref, g):
    for h in range(DN_HEADS):
        ref[:, h * DN_HEAD_DIM:(h + 1) * DN_HEAD_DIM] = t[h * g:(h + 1) * g].reshape(g * DN_CHUNK, DN_HEAD_DIM).astype(ref.dtype)


def _head_columns(gbv, first_lane, g):
    lane = lax.broadcasted_iota(jnp.int32, gbv.shape, 1)
    return jnp.concatenate([jnp.sum(jnp.where(lane == first_lane + h, gbv, 0.0), axis=1, keepdims=True).reshape(g, DN_CHUNK, 1)
                            for h in range(DN_HEADS)], axis=0)


def _gdn_terms_fwd(qn, kn, v, gb, name):
    l = qn.shape[0]
    n_chunks = l // DN_CHUNK
    g = min(TERMS_CHUNKS, n_chunks)
    rows, c, nh = g * DN_CHUNK, DN_CHUNK, DN_HEADS

    def body(q_ref, k_ref, v_ref, gb_ref, u_ref, w_ref, qg_ref, kt_ref, qk_ref, t_ref, gl_ref):
        gbv = gb_ref[...]
        x = _wy_terms(_to_heads(q_ref[...], g), _to_heads(k_ref[...], g), _to_heads(v_ref[...], g),
                      _head_columns(gbv, 0, g), _head_columns(gbv, nh, g))
        _from_heads(_bmdot(x["t"], x["vb"]), u_ref, g)
        _from_heads(_bmdot(x["t"], x["kbg"]), w_ref, g)
        _from_heads(x["qg"], qg_ref, g)
        _from_heads(x["kt"], kt_ref, g)
        for h in range(nh):
            qk_ref[:, h] = x["qk"][h * g:(h + 1) * g]
            t_ref[:, h] = x["t"][h * g:(h + 1) * g]
            gl_ref[:, h] = jnp.broadcast_to(x["gl"][h * g:(h + 1) * g], (g, 1, 128))

    row = pl.BlockSpec((rows, GROUP_WIDTH), lambda i: (i, 0))
    sq = pl.BlockSpec((g, nh, c, c), lambda i: (i, 0, 0, 0))
    glb = pl.BlockSpec((g, nh, 1, 128), lambda i: (i, 0, 0, 0))
    return pl.pallas_call(
        body, name=name, grid=(n_chunks // g,), in_specs=[row, row, row, pl.BlockSpec((rows, AB_PAD), lambda i: (i, 0))],
        out_specs=[row] * 4 + [sq, sq, glb],
        out_shape=[jax.ShapeDtypeStruct((l, GROUP_WIDTH), F32)] * 4 + [jax.ShapeDtypeStruct((n_chunks, nh, c, c), F32)] * 2
        + [jax.ShapeDtypeStruct((n_chunks, nh, 1, 128), F32)],
        compiler_params=_params(("parallel",)),
    )(qn, kn, v, gb)


REC_CHUNKS = 8


def _rec_specs(n_chunks, reverse):
    c, hd, nh = DN_CHUNK, DN_HEAD_DIM, DN_HEADS
    g = min(REC_CHUNKS, n_chunks)
    nb = n_chunks // g
    blk_of = (lambda n: nb - 1 - n) if reverse else (lambda n: n)
    return g, nb, (pl.BlockSpec((g * c, GROUP_WIDTH), lambda n: (blk_of(n), 0)), pl.BlockSpec((g, nh, c, c), lambda n: (blk_of(n), 0, 0, 0)),
                   pl.BlockSpec((g, nh, 1, 128), lambda n: (blk_of(n), 0, 0, 0)), pl.BlockSpec((g, nh, hd, hd), lambda n: (blk_of(n), 0, 0, 0)))


def _gdn_rec_fwd(u, w, qg, kt, qk, gl, name):
    l = u.shape[0]
    n_chunks = l // DN_CHUNK
    hd, nh, c = DN_HEAD_DIM, DN_HEADS, DN_CHUNK
    g, nb, (blk, sq, glb, st) = _rec_specs(n_chunks, False)
    heads = range(nh)

    def body(u_ref, w_ref, qg_ref, kt_ref, qk_ref, gl_ref, o_ref, vn_ref, s_ref, state):
        @pl.when(pl.program_id(0) == 0)
        def _():
            state[...] = jnp.zeros_like(state)

        def cols(h):
            return slice(h * hd, (h + 1) * hd)
        for ci in range(g):
            rows = slice(ci * c, (ci + 1) * c)
            s = [state[h] for h in heads]
            ws = [_mdot(w_ref[rows, cols(h)], s[h]) for h in heads]
            vn = [u_ref[rows, cols(h)] - ws[h] for h in heads]
            kv = [_mdot(kt_ref[rows, cols(h)], vn[h], "tn") for h in heads]
            for h in heads:
                state[h] = s[h] * gl_ref[ci, h] + kv[h]
            o1 = [_mdot(qg_ref[rows, cols(h)], s[h]) for h in heads]
            o2 = [_mdot(qk_ref[ci, h], vn[h]) for h in heads]
            for h in heads:
                s_ref[ci, h] = s[h]
                o_ref[rows, cols(h)] = o1[h] + o2[h]
                vn_ref[rows, cols(h)] = vn[h]

    return pl.pallas_call(
        body, name=name, grid=(nb,), in_specs=[blk, blk, blk, blk, sq, glb], out_specs=[blk, blk, st],
        out_shape=[jax.ShapeDtypeStruct((l, GROUP_WIDTH), F32)] * 2 + [jax.ShapeDtypeStruct((n_chunks, nh, hd, hd), F32)],
        scratch_shapes=[pltpu.VMEM((nh, hd, hd), F32)], compiler_params=_params(("arbitrary",)),
    )(u, w, qg, kt, qk, gl)


def _gdn_rec_bwd(do, w, qg, kt, vn, qk, gl, states, name):
    l = do.shape[0]
    n_chunks = l // DN_CHUNK
    hd, nh, c = DN_HEAD_DIM, DN_HEADS, DN_CHUNK
    g, nb, (blk, sq, glb, st) = _rec_specs(n_chunks, True)
    heads = range(nh)

    def body(do_ref, w_ref, qg_ref, kt_ref, vn_ref, qk_ref, gl_ref, s_ref, dvn_ref, dw_ref, dkt_ref, dqg_ref, dqk_ref, dgl_ref, dstate):
        @pl.when(pl.program_id(0) == 0)
        def _():
            dstate[...] = jnp.zeros_like(dstate)

        def cols(h):
            return slice(h * hd, (h + 1) * hd)
        tril = lax.broadcasted_iota(jnp.int32, (c, c), 0) >= lax.broadcasted_iota(jnp.int32, (c, c), 1)
        for ci in reversed(range(g)):
            rows = slice(ci * c, (ci + 1) * c)
            ds = [dstate[h] for h in heads]
            dout = [do_ref[rows, cols(h)] for h in heads]
            a1 = [_mdot(qk_ref[ci, h], dout[h], "tn") for h in heads]
            a2 = [_mdot(kt_ref[rows, cols(h)], ds[h]) for h in heads]
            dvn = [a1[h] + a2[h] for h in heads]
            b1 = [_mdot(qg_ref[rows, cols(h)], dout[h], "tn") for h in heads]
            b2 = [_mdot(w_ref[rows, cols(h)], dvn[h], "tn") for h in heads]
            for h in heads:
                dstate[h] = b1[h] + gl_ref[ci, h] * ds[h] - b2[h]
            for h in heads:
                s, vnew = s_ref[ci, h], vn_ref[rows, cols(h)]
                dvn_ref[rows, cols(h)] = dvn[h]
                dw_ref[rows, cols(h)] = -_mdot(dvn[h], s, "nt")
                dkt_ref[rows, cols(h)] = _mdot(vnew, ds[h], "nt")
                dqg_ref[rows, cols(h)] = _mdot(dout[h], s, "nt")
                dqk_ref[ci, h] = jnp.where(tril, _mdot(dout[h], vnew, "nt"), 0.0)
                dgl = jnp.sum(jnp.sum(ds[h] * s, axis=1, keepdims=True), axis=0, keepdims=True)
                dgl_ref[ci, h] = jnp.broadcast_to(dgl, (1, 128))

    return pl.pallas_call(
        body, name=name, grid=(nb,), in_specs=[blk] * 5 + [sq, glb, st], out_specs=[blk] * 4 + [sq, glb],
        out_shape=[jax.ShapeDtypeStruct((l, GROUP_WIDTH), F32)] * 4 + [jax.ShapeDtypeStruct((n_chunks, nh, c, c), F32),
                                                                       jax.ShapeDtypeStruct((n_chunks, nh, 1, 128), F32)],
        scratch_shapes=[pltpu.VMEM((nh, hd, hd), F32)], compiler_params=_params(("arbitrary",)),
    )(do, w, qg, kt, vn, qk, gl, states)


def _gdn_terms_bwd(qn, kn, v, gb, t_inv, dvn, dw, dkt, dqg, dqk, dgl, name):
    l = qn.shape[0]
    n_chunks = l // DN_CHUNK
    g = min(TERMS_CHUNKS, n_chunks)
    rows, c, nh = g * DN_CHUNK, DN_CHUNK, DN_HEADS

    def body(q_ref, k_ref, v_ref, gb_ref, t_ref, dvn_ref, dw_ref, dkt_ref, dqg_ref, dqk_ref, dgl_ref, dq_ref, dk_ref, dv_ref, dgb_ref):
        gbv = gb_ref[...]
        q, k, vv = _to_heads(q_ref[...], g), _to_heads(k_ref[...], g), _to_heads(v_ref[...], g)
        beta = _head_columns(gbv, nh, g)
        t = jnp.concatenate([t_ref[:, h] for h in range(nh)], axis=0)
        x = _wy_terms(q, k, vv, _head_columns(gbv, 0, g), beta, t=t)
        ii, jj, strict = x["ii"], x["jj"], x["strict"]
        du, dwv, dktv, dqgv = (_to_heads(r[...], g) for r in (dvn_ref, dw_ref, dkt_ref, dqg_ref))
        dqkv = jnp.concatenate([dqk_ref[:, h] for h in range(nh)], axis=0)
        dglv = jnp.concatenate([dgl_ref[:, h] for h in range(nh)], axis=0)[:, :, 0:1]
        dt = _bmdot(du, x["vb"], "nt") + _bmdot(dwv, x["kbg"], "nt")
        dvb = _bmdot(t, du, "tn")
        dkbg = _bmdot(t, dwv, "tn")
        da = jnp.where(strict, -_bdot3(_bdot3(t, dt, "tn"), t, "nt"), 0.0)
        dkk = da * x["dec"]
        dqk0 = dqkv * x["dec"]
        e = (da * x["kk"] + dqkv * x["qk0"]) * x["dec"]
        dkb = _bmdot(dkk, k) + dkbg * x["eg"]
        dk = _bmdot(dkk, x["kb"], "tn") + _bmdot(dqk0, q, "tn") + dktv * x["e2"] + dkb * beta
        dq = _bmdot(dqk0, k) + dqgv * x["eg"]
        s_kt = jnp.sum(dktv * x["kt"], axis=2, keepdims=True)
        dgc_c = (jnp.sum(e, axis=2, keepdims=True) + jnp.sum(dqgv * x["qg"], axis=2, keepdims=True) - s_kt
                 + jnp.sum(dkbg * x["kbg"], axis=2, keepdims=True))
        dgc_last = jnp.sum(s_kt, axis=1, keepdims=True) + dglv * x["gl"]
        rid = lax.broadcasted_iota(jnp.int32, (1, c, 1), 1)
        dgc_c = dgc_c + jnp.where(rid == c - 1, dgc_last, 0.0)
        dgc_r = jnp.sum(jnp.where(ii == jj, dgc_c, 0.0), axis=1, keepdims=True) - jnp.sum(e, axis=1, keepdims=True)
        dg = jnp.sum(jnp.where(jj >= ii, dgc_r, 0.0), axis=2, keepdims=True)
        dbeta = jnp.sum(dkb * k, axis=2, keepdims=True) + jnp.sum(dvb * vv, axis=2, keepdims=True)
        _from_heads(dq, dq_ref, g)
        _from_heads(dk, dk_ref, g)
        _from_heads(dvb * beta, dv_ref, g)
        lane = lax.broadcasted_iota(jnp.int32, gbv.shape, 1)
        dgb = jnp.zeros(gbv.shape, F32)
        for h in range(nh):
            dgb = dgb + jnp.where(lane == h, dg[h * g:(h + 1) * g].reshape(rows, 1), 0.0)
            dgb = dgb + jnp.where(lane == nh + h, dbeta[h * g:(h + 1) * g].reshape(rows, 1), 0.0)
        dgb_ref[...] = dgb

    row = pl.BlockSpec((rows, GROUP_WIDTH), lambda i: (i, 0))
    abrow = pl.BlockSpec((rows, AB_PAD), lambda i: (i, 0))
    sq = pl.BlockSpec((g, nh, c, c), lambda i: (i, 0, 0, 0))
    glb = pl.BlockSpec((g, nh, 1, 128), lambda i: (i, 0, 0, 0))
    return pl.pallas_call(
        body, name=name, grid=(n_chunks // g,), in_specs=[row, row, row, abrow, sq, row, row, row, row, sq, glb],
        out_specs=[row, row, row, abrow],
        out_shape=[jax.ShapeDtypeStruct((l, GROUP_WIDTH), F32)] * 3 + [jax.ShapeDtypeStruct((l, AB_PAD), F32)],
        compiler_params=_params(("parallel",)),
    )(qn, kn, v, gb, t_inv, dvn, dw, dkt, dqg, dqk, dgl)


def _gdn_post_fwd(o, proj, norm_g4, name):
    l, gw = o.shape
    tl = min(MIX_ROWS, l)

    def body(o_ref, gate_ref, g_ref, out_ref):
        ov = o_ref[...]
        r = lax.rsqrt(_head_sums(ov * ov) * (1.0 / DN_HEAD_DIM) + RMS_EPS)
        out_ref[...] = (ov * r * g_ref[...] * _silu(gate_ref[...])).astype(out_ref.dtype)

    row = pl.BlockSpec((tl, gw), lambda i: (i, 0))
    return pl.pallas_call(
        body, name=name, grid=(l // tl,),
        in_specs=[row, pl.BlockSpec((tl, gw), lambda i: (i, 7)), pl.BlockSpec((1, gw), lambda i: (0, 0))],
        out_specs=row, out_shape=jax.ShapeDtypeStruct((l, gw), BF16), compiler_params=_params(("parallel",)),
    )(o, proj, norm_g4)


def _gdn_post_bwd(dmixed, o, proj, norm_g4, name):
    l, gw = o.shape
    tl = min(MIX_ROWS, l)

    def body(d_ref, o_ref, gate_ref, g_ref, do_ref, dgate_ref, dng_ref):
        @pl.when(pl.program_id(0) == 0)
        def _():
            dng_ref[...] = jnp.zeros_like(dng_ref)

        ov, gate, d = o_ref[...], gate_ref[...], d_ref[...]
        r = lax.rsqrt(_head_sums(ov * ov) * (1.0 / DN_HEAD_DIM) + RMS_EPS)
        oh = ov * r
        sg = _silu(gate)
        dgate_ref[...] = (d * oh * g_ref[...] * _silu_grad(gate)).astype(dgate_ref.dtype)
        dng_ref[...] += jnp.sum(d * sg * oh, axis=0, keepdims=True)
        doh = d * g_ref[...] * sg
        do_ref[...] = r * (doh - oh * _head_sums(doh * oh) * (1.0 / DN_HEAD_DIM))

    row = pl.BlockSpec((tl, gw), lambda i: (i, 0))
    vec = pl.BlockSpec((1, gw), lambda i: (0, 0))
    return pl.pallas_call(
        body, name=name, grid=(l // tl,),
        in_specs=[pl.BlockSpec((tl, gw), lambda i: (i, 3)), row, pl.BlockSpec((tl, gw), lambda i: (i, 7)), vec],
        out_specs=[row, row, vec],
        out_shape=[jax.ShapeDtypeStruct((l, gw), F32), jax.ShapeDtypeStruct((l, gw), BF16), jax.ShapeDtypeStruct((1, gw), F32)],
        compiler_params=_params(("arbitrary",)),
    )(dmixed, o, proj, norm_g4)


def _run(hosts, name, fn):
    h = hosts.get(name)
    if h is None:
        return fn(None)
    res, outs = fn(h[0]())
    h[1](outs)
    return res


def _layer_fwd(x, xm, w, li, hosts, target=None):
    l = x.shape[0]
    nm = f"l{li}_"
    proj = _run(hosts, nm + "proj", lambda ops: _matmul(
        xm, w["w_main"], mode="nn", tm=1024, tn=1024, tk=2048,out_dtype=F32, name=nm + "proj", comm=ops))
    proj_ab = _matmul(xm, w["w_ab"], mode="nn", tm=1024, tn=AB_PAD, tk=2048, out_dtype=F32, name=nm + "proj_ab")
    hs, y = _s5_fwd(proj, w["s5_b"], w["s5_c"], w["s5_lam"], w["s5_d"], nm + "s5")
    m_s5 = _s5_glu_fwd(y, w["s5_glu_w"], w["s5_glu_b"], nm + "s5_glu")
    m_sgu = _sgu_fwd(proj, w["sgu_norm_g"], w["sgu_norm_b"], w["sgu_wm"], w["sgu_bfull"], nm + "sgu")
    m_pool, pooled = _pool_fwd(proj, w["pool_w"], w["pool_scale"], nm + "pool")
    qn, kn, v, cq, ck, cv, gb = _gdn_pre_fwd(proj, proj_ab, w["dn_conv_w"], w["dn_a_log"], w["dn_dt_bias"], nm + "gdn_pre")
    u, wy, qg, kt, qk, t_inv, gl = _gdn_terms_fwd(qn, kn, v, gb, nm + "gdn_terms")
    o, vn, states = _gdn_rec_fwd(u, wy, qg, kt, qk, gl, nm + "gdn_rec")
    m_dn = _gdn_post_fwd(o, proj, w["dn_norm_g4"], nm + "gdn_post")
    mixed = jnp.concatenate([m_s5, m_sgu, m_pool, m_dn], axis=1)
    y1 = _matmul(mixed, w["w_out"], mode="nn", tm=1024, tn=1024, tk=2048, out_dtype=F32, name=nm + "out_proj")
    h1, x1, x1m = _ln_fwd(x, y1, w["ln1_g"], w["ln1_b"], nm + "ln1")
    r = _run(hosts, nm + "up", lambda ops: _matmul(
        x1m, w["w_up"], mode="nn", tm=1024, tn=1024, tk=2048,out_dtype=BF16, name=nm + "up",
        epi=lambda acc: jnp.maximum(acc, 0.0), b_slab=w["w_up"].shape[2], comm=ops))
    y2 = _run(hosts, nm + "down", lambda ops: _matmul(
        r, w["w_down"], mode="nn", tm=1024, tn=1024, tk=2048,out_dtype=F32, name=nm + "down", a_fn=lambda a: a * a, comm=ops))
    if target is None:
        h2, x2, x2m = _ln_fwd(x1, y2, w["ln2_g"], w["ln2_b"], nm + "ln2")
    else:
        h2, x2, x2m = _ln_loss(x1, y2, w["ln2_g"], w["ln2_b"], target, nm + "ln2_loss")
    saved = dict(xm=xm, proj=proj, proj_ab=proj_ab, hs=hs, y=y, pooled=pooled, qn=qn, kn=kn, v=v, cq=cq, ck=ck, cv=cv, gb=gb,
                 wy=wy, qg=qg, kt=kt, qk=qk, t_inv=t_inv, gl=gl, vn=vn, o=o, states=states, mixed=mixed, h1=h1, x1m=x1m,
                 r=r, h2=h2)
    return x2, x2m, saved


def _layer_bwd(dx2, s, w, small, li, hosts, g):
    nm = f"l{li}b_"
    l = dx2.shape[0]
    gw = GROUP_WIDTH
    wire = MXU_DTYPE
    dh2, dh2m, g["ln2_g"], g["ln2_b"] = _ln_bwd(dx2, s["h2"], w["ln2_g"], nm + "ln2")
    g["w_down"] = _run(hosts, nm + "dw_down", lambda ops: _matmul(
        s["r"], dh2m, mode="tn", tm=1024, tn=1024, tk=2048,out_dtype=wire, name=nm + "dw_down", a_fn=lambda a: a * a,
        comm=ops)).reshape(N_DEV, D_FF // N_DEV, D_MODEL)
    dpre = _run(hosts, nm + "dpre", lambda ops: _matmul(
        dh2m, w["w_down"], mode="nt", tm=1024, tn=1024, tk=2048,out_dtype=BF16, name=nm + "dpre",
        extras=[(s["r"], (None, None), lambda i, j: (i, j))], epi=lambda acc, r: acc * 2.0 * r.astype(F32), comm=ops))
    g["w_up"] = _matmul(s["x1m"], dpre, mode="tn", tm=1024, tn=1024, tk=2048,out_dtype=wire, name=nm + "dw_up",
                        out_slab=D_FF // N_DEV)
    dx1 = _run(hosts, nm + "dx1", lambda ops: _matmul(
        dpre, w["w_up"], mode="nt", tm=1024, tn=1024, tk=2048,out_dtype=F32, name=nm + "dx1",
        extras=[(dh2, (None, None), lambda i, j: (i, j))], epi=lambda acc, e: acc + ALPHA * e,
        b_slab=w["w_up"].shape[2], comm=ops))
    dh1, dh1m, g["ln1_g"], g["ln1_b"] = _ln_bwd(dx1, s["h1"], w["ln1_g"], nm + "ln1")
    g["w_out"] = _matmul(s["mixed"], dh1m, mode="tn", tm=1024, tn=1024, tk=2048,out_dtype=wire,
                         name=nm + "dw_out").reshape(N_DEV, D_MODEL // N_DEV, D_MODEL)
    dmixed = _run(hosts, nm + "dmixed", lambda ops: _matmul(
        dh1m, w["w_out"], mode="nt", tm=1024, tn=1024, tk=2048,out_dtype=F32, name=nm + "dmixed", comm=ops))
    proj, proj_ab = s["proj"], s["proj_ab"]
    dy, dz, yg, g["s5_glu_b"], g["s5_d"] = _s5_glu_bwd(dmixed, s["y"], proj, w["s5_glu_w"], w["s5_glu_b"], nm + "s5_glu")
    g["s5_glu_w"] = _matmul(yg, dz, mode="tn", tm=gw, tn=gw, tk=1024, out_dtype=wire,
                            name=nm + "dw_glu").reshape(N_DEV, gw // N_DEV, gw)
    du_s5, g["s5_b"], g["s5_c"], g["s5_lam"] = _s5_bwd(dy, s["hs"], proj, w["s5_b"], w["s5_c"], w["s5_lam_conj"], w["s5_d"], nm + "s5")
    dzu, dzv, g["sgu_w"], g["sgu_bfull"], g["sgu_norm_g"], g["sgu_norm_b"] = _sgu_bwd(
        dmixed, proj, w["sgu_norm_g"], w["sgu_norm_b"], w["sgu_wm"], w["sgu_bfull"], nm + "sgu")
    dpooled, g["pool_w"], g["pool_scale"] = _pool_bwd_map(dmixed, s["pooled"], w["pool_w"], w["pool_scale"], nm + "pool_map")
    dp = _pool_bwd_window(dpooled, nm + "pool_win")
    do, dgate, g["dn_norm_g4"] = _gdn_post_bwd(dmixed, s["o"], proj, w["dn_norm_g4"], nm + "gdn_post")
    dvn, dwy, dkt, dqg, dqk, dgl = _gdn_rec_bwd(do, s["wy"], s["qg"], s["kt"], s["vn"], s["qk"], s["gl"], s["states"], nm + "gdn_rec")
    dq, dk, dv, dgb = _gdn_terms_bwd(s["qn"], s["kn"], s["v"], s["gb"], s["t_inv"], dvn, dwy, dkt, dqg, dqk, dgl, nm + "gdn_terms")
    dcq, dck, dcv, dab, g["dn_a_log"], g["dn_dt_bias"] = _gdn_pre_bwd(
        dq, dk, dv, s["cq"], s["ck"], s["cv"], dgb, s["gb"], proj_ab, w["dn_a_log"], w["dn_dt_bias"], nm + "gdn_pre")
    dxs, dws = [], []
    for p, dc in enumerate((dcq, dck, dcv)):
        dxp, dwp = _conv_bwd(dc, proj, QKV_BLK + p, w["dn_conv_w"][:, p * gw:(p + 1) * gw], nm + f"conv{p}")
        dxs.append(dxp)
        dws.append(dwp)
    dconv = jnp.concatenate(dws, axis=1)
    g["dn_conv_w"] = jnp.transpose(dconv.reshape(dconv.shape[0], N_DEV, 3 * gw // N_DEV), (1, 0, 2))
    dproj = jnp.concatenate([du_s5, dzu, dzv, dp] + dxs + [dgate], axis=1)
    xm = s["xm"]
    g["small"] = _unprep_grads(g, small)
    dw_main = _run(hosts, nm + "dw_main", lambda ops: _matmul(
        xm, dproj, mode="tn", tm=1024, tn=1024, tk=2048,out_dtype=wire, name=nm + "dw_main", comm=ops))
    dw_ab = _matmul(xm, dab, mode="tn", tm=1024, tn=AB_PAD, tk=1024, out_dtype=wire, name=nm + "dw_ab")
    dw_in = jnp.concatenate([dw_main, dw_ab[:, :2 * DN_HEADS]], axis=1)
    g["w_in"] = jnp.transpose(dw_in.reshape(D_MODEL, N_DEV, dw_in.shape[1] // N_DEV), (1, 0, 2))
    return _run(hosts, nm + "dx", lambda ops: _matmul(
        dproj, w["w_main"], mode="nt", tm=1024, tn=1024, tk=2048, out_dtype=F32, name=nm + "dx",
        extras=[(dh1, (None, None), lambda i, j: (i, j)), (dab, (None, AB_PAD), lambda i, j: (i, 0)),
                (w["w_ab"], ("tn", AB_PAD), lambda i, j: (j, 0))],
        epi=lambda acc, e, da, wab: acc + ALPHA * e + _dot(_mx(da), _mx(wab), "nt"), comm=ops))


SMALL = ("s5_lambda_re", "s5_lambda_im", "s5_log_step", "s5_b_re", "s5_b_im", "s5_c_re", "s5_c_im", "s5_d", "s5_glu_b",
         "sgu_norm_g", "sgu_norm_b", "sgu_w", "sgu_b", "pool_w", "pool_scale", "dn_a_log", "dn_dt_bias", "dn_norm_g",
         "ln1_g", "ln1_b", "ln2_g", "ln2_b")
SHARDED = ("w_in", "s5_glu_w", "dn_conv_w", "w_out", "w_up", "w_down")


def _pad_lanes(v, width=AB_PAD):
    return jnp.pad(v.reshape(1, -1), ((0, 0), (0, width - v.size)))


def _prep_small(p):
    mx = MXU_DTYPE
    lbr, lbi, bbr, bbi = _s5_discretize(p["s5_lambda_re"], p["s5_lambda_im"], p["s5_log_step"], p["s5_b_re"], p["s5_b_im"])
    b_compact, c_compact = _s5_compact(bbr, bbi, p["s5_c_re"], p["s5_c_im"])
    causal = jnp.tril(jnp.ones((SGU_CHUNK, SGU_CHUNK), F32))
    return dict(
        s5_b=b_compact.astype(mx), s5_c=c_compact.astype(mx),
        s5_lam=jnp.concatenate([lbr.reshape(1, -1), lbi.reshape(1, -1)], axis=1),
        s5_lam_conj=jnp.concatenate([lbr.reshape(1, -1), -lbi.reshape(1, -1)], axis=1),
        s5_d=p["s5_d"].reshape(1, -1), s5_glu_b=p["s5_glu_b"].reshape(1, -1),
        sgu_norm_g=p["sgu_norm_g"].reshape(1, -1), sgu_norm_b=p["sgu_norm_b"].reshape(1, -1),
        sgu_wm=(p["sgu_w"] * causal).astype(mx), sgu_bfull=jnp.repeat(p["sgu_b"].T, GROUP_WIDTH // SGU_HEADS, axis=1),
        pool_w=p["pool_w"].astype(mx), pool_scale=p["pool_scale"].reshape(1, -1),
        dn_a_log=_pad_lanes(p["dn_a_log"]), dn_dt_bias=_pad_lanes(p["dn_dt_bias"]),
        dn_norm_g4=jnp.tile(p["dn_norm_g"].reshape(1, -1), (1, DN_HEADS)),
        ln1_g=p["ln1_g"].reshape(1, -1), ln1_b=p["ln1_b"].reshape(1, -1),
        ln2_g=p["ln2_g"].reshape(1, -1), ln2_b=p["ln2_b"].reshape(1, -1),
    )


def _weight_views(name, t):
    if name == "w_in":
        w_in = jnp.transpose(t, (1, 0, 2)).reshape(t.shape[1], N_DEV * t.shape[2])
        pad = AB_PAD - (w_in.shape[1] - MAIN_COLS)
        return dict(w_main=w_in[:, :MAIN_COLS], w_ab=jnp.pad(w_in[:, MAIN_COLS:], ((0, 0), (0, pad))))
    if name == "dn_conv_w":
        return dict(dn_conv_w=jnp.transpose(t, (1, 0, 2)).reshape(t.shape[1], N_DEV * t.shape[2]))
    if name == "w_up":
        return dict(w_up=t)
    return {name: t.reshape(N_DEV * t.shape[1], t.shape[2])}


def _unprep_grads(g, p):
    causal = jnp.tril(jnp.ones((SGU_CHUNK, SGU_CHUNK), F32))
    dbbr, dbbi = _s5_uncompact_b(g["s5_b"])
    dc_re, dc_im = _s5_uncompact_c(g["s5_c"])
    dlbr, dlbi = g["s5_lam"][0, :S5_NS].reshape(S5_GROUPS, S5_STATE), g["s5_lam"][0, S5_NS:].reshape(S5_GROUPS, S5_STATE)
    _, vjp = jax.vjp(_s5_discretize, p["s5_lambda_re"], p["s5_lambda_im"], p["s5_log_step"], p["s5_b_re"], p["s5_b_im"])
    d_lre, d_lim, d_step, d_bre, d_bim = vjp((dlbr, dlbi, dbbr, dbbi))
    hd = GROUP_WIDTH // SGU_HEADS
    return dict(
        s5_lambda_re=d_lre, s5_lambda_im=d_lim, s5_log_step=d_step, s5_b_re=d_bre, s5_b_im=d_bim, s5_c_re=dc_re, s5_c_im=dc_im,
        s5_d=g["s5_d"].reshape(S5_GROUPS, S5_CH), s5_glu_b=g["s5_glu_b"].reshape(-1),
        sgu_norm_g=g["sgu_norm_g"].reshape(-1), sgu_norm_b=g["sgu_norm_b"].reshape(-1), sgu_w=g["sgu_w"] * causal,
        sgu_b=jnp.sum(g["sgu_bfull"].reshape(SGU_CHUNK, SGU_HEADS, hd), axis=2).T,
        pool_w=g["pool_w"], pool_scale=g["pool_scale"].reshape(-1),
        dn_a_log=g["dn_a_log"][0, :DN_HEADS], dn_dt_bias=g["dn_dt_bias"][0, :DN_HEADS],
        dn_norm_g=jnp.sum(g["dn_norm_g4"].reshape(DN_HEADS, DN_HEAD_DIM), axis=0),
        ln1_g=g["ln1_g"].reshape(-1), ln1_b=g["ln1_b"].reshape(-1), ln2_g=g["ln2_g"].reshape(-1), ln2_b=g["ln2_b"].reshape(-1),
    )


def _local_step(x, target, ops, small, fwd_hosts, bwd_hosts, grads):
    saved = []
    h, hm = x, x.astype(MXU_DTYPE)
    for i in range(DEPTH):
        h, hm, s = _layer_fwd(h, hm, ops[i], i, fwd_hosts, target if i == DEPTH - 1 else None)
        saved.append(s)
    loss, dh = h, hm
    for i in reversed(range(DEPTH)):
        dh = _layer_bwd(dh, saved[i], ops[i], small[i], i, bwd_hosts, grads[i])
    return loss, dh


def _adamw(w, gparts, m, v, name):
    rr, c = w.shape
    ng = len(gparts)
    r = rr // ng
    lanes = -(-c // 128) * 128
    tr = r
    while tr * lanes * 4 * N_DEV > (4 << 20) and tr % 16 == 0:
        tr //= 2
    nb = r // tr

    def body(w_ref, *rest):
        g_refs, (m_ref, v_ref, go_ref, d_ref, mo_ref, vo_ref) = rest[:ng], rest[ng:]
        layer = pl.program_id(0)
        g = jnp.zeros(m_ref.shape, F32)
        for li in range(ng):
            gl = g_refs[li][0].astype(F32)
            for s in range(1, N_DEV):
                gl = gl + g_refs[li][s].astype(F32)
            g = jnp.where(layer == li, gl, g)
        mn = ADAM_B1 * m_ref[...] + (1.0 - ADAM_B1) * g
        vn = ADAM_B2 * v_ref[...] + (1.0 - ADAM_B2) * g * g
        m_hat = mn / (1.0 - ADAM_B1 ** ADAM_STEP)
        v_hat = vn / (1.0 - ADAM_B2 ** ADAM_STEP)
        go_ref[...] = g
        d_ref[...] = -ADAM_LR * (m_hat / (jnp.sqrt(v_hat) + ADAM_EPS) + ADAM_WD * w_ref[...])
        mo_ref[...] = mn
        vo_ref[...] = vn

    row = pl.BlockSpec((tr, c), lambda li, i: (li * nb + i, 0))
    part_specs = [pl.BlockSpec((N_DEV, tr, c), functools.partial(lambda li, i, k: (0, jnp.where(li == k, i, 0), 0), k=k))
                  for k in range(ng)]
    return pl.pallas_call(
        body, name=name, grid=(ng, nb), in_specs=[row] + part_specs + [row, row],
        out_specs=[row] * 4, out_shape=[jax.ShapeDtypeStruct((rr, c), F32)] * 4, compiler_params=_params(("arbitrary", "arbitrary")),
    )(w, *gparts, m, v)


PACK_LANES = 128
PACK_ROWS = 8192


PACK_TILE = 8 * PACK_LANES


def _pack_rows(t):
    return -(-t.size // PACK_TILE) * 8


def _pack(vals):
    rows = []
    for t in vals:
        flat = t.reshape(-1)
        n_rows = _pack_rows(t)
        rows.append(jnp.pad(flat, (0, n_rows * PACK_LANES - flat.size)).reshape(n_rows, PACK_LANES))
    used = sum(r.shape[0] for r in rows)
    assert used <= PACK_ROWS, used
    return jnp.concatenate(rows + [jnp.zeros((PACK_ROWS - used, PACK_LANES), F32)], axis=0)


def _unpack(packed, like):
    out, off = [], 0
    for t in like:
        n_rows = _pack_rows(t)
        out.append(packed[off:off + n_rows].reshape(-1)[:t.size].reshape(t.shape))
        off += n_rows
    return out


def kernel(x, w_in, s5_lambda_re, s5_lambda_im, s5_log_step, s5_b_re, s5_b_im, s5_c_re, s5_c_im, s5_d, s5_glu_w, s5_glu_b, sgu_norm_g, sgu_norm_b, sgu_w, sgu_b, pool_w, pool_scale, dn_conv_w, dn_a_log, dn_dt_bias, dn_norm_g, w_out, ln1_g, ln1_b, w_up, w_down, ln2_g, ln2_b, loss_target, m_w_in, m_s5_lambda_re, m_s5_lambda_im, m_s5_log_step, m_s5_b_re, m_s5_b_im, m_s5_c_re, m_s5_c_im, m_s5_d, m_s5_glu_w, m_s5_glu_b, m_sgu_norm_g, m_sgu_norm_b, m_sgu_w, m_sgu_b, m_pool_w, m_pool_scale, m_dn_conv_w, m_dn_a_log, m_dn_dt_bias, m_dn_norm_g, m_w_out, m_ln1_g, m_ln1_b, m_w_up, m_w_down, m_ln2_g, m_ln2_b, v_w_in, v_s5_lambda_re, v_s5_lambda_im, v_s5_log_step, v_s5_b_re, v_s5_b_im, v_s5_c_re, v_s5_c_im, v_s5_d, v_s5_glu_w, v_s5_glu_b, v_sgu_norm_g, v_sgu_norm_b, v_sgu_w, v_sgu_b, v_pool_w, v_pool_scale, v_dn_conv_w, v_dn_a_log, v_dn_dt_bias, v_dn_norm_g, v_w_out, v_ln1_g, v_ln1_b, v_w_up, v_w_down, v_ln2_g, v_ln2_b):
    names = ("w_in", "s5_lambda_re", "s5_lambda_im", "s5_log_step", "s5_b_re", "s5_b_im", "s5_c_re", "s5_c_im", "s5_d", "s5_glu_w",
             "s5_glu_b", "sgu_norm_g", "sgu_norm_b", "sgu_w", "sgu_b", "pool_w", "pool_scale", "dn_conv_w", "dn_a_log", "dn_dt_bias",
             "dn_norm_g", "w_out", "ln1_g", "ln1_b", "w_up", "w_down", "ln2_g", "ln2_b")
    env = locals()
    w = {n: env[n] for n in names}
    m = {n: env["m_" + n] for n in names}
    v = {n: env["v_" + n] for n in names}

    wire = [{n: (w[n][i] if n == "dn_conv_w" else w[n][i].astype(MXU_DTYPE)) for n in SHARDED} for i in range(DEPTH)]
    small = [{n: w[n][i] for n in SMALL} for i in range(DEPTH)]
    ops = [_prep_small(small[i]) for i in range(DEPTH)]
    grads = [{} for _ in range(DEPTH)]
    recv = [{} for _ in range(DEPTH)]
    first = ("w_in", "s5_glu_w", "dn_conv_w", "w_out")

    def gather(layer, group):
        def take(outs):
            for n, t in zip(group, outs):
                ops[layer].update(_weight_views(n, t))
        return (lambda: [(wire[layer][n], False) for n in group]), take

    def scatter(layer, group, with_small=False):
        def make():
            sends = [(grads[layer][n], True) for n in group]
            if with_small:
                sends.append((_pack([jnp.stack([grads[i]["small"][n] for i in range(DEPTH)]) for n in SMALL]), False))
            return sends
        def take(outs):
            recv[layer].update(dict(zip(group + (("small",) if with_small else ()), outs)))
        return make, take

    make, take = gather(0, first)
    take(_exchange(make(), "gather_first"))
    fwd_hosts = {"l0_proj": gather(0, ("w_up",)), "l0_up": gather(0, ("w_down",)), "l0_down": gather(1, first),
                 "l1_proj": gather(1, ("w_up",)), "l1_up": gather(1, ("w_down",))}
    late = ("w_in", "s5_glu_w", "dn_conv_w")
    bwd_hosts = {"l1b_dpre": scatter(1, ("w_down",)), "l1b_dx1": scatter(1, ("w_up",)), "l1b_dmixed": scatter(1, ("w_out",)),
                 "l0b_dw_down": scatter(1, late),
                 "l0b_dpre": scatter(0, ("w_down",)), "l0b_dx1": scatter(0, ("w_up",)), "l0b_dmixed": scatter(0, ("w_out",)),
                 "l0b_dw_main": scatter(0, ("s5_glu_w", "dn_conv_w"), with_small=True), "l0b_dx": scatter(0, ("w_in",))}
    loss, grad_x = _local_step(x[0], loss_target[0], ops, small, fwd_hosts, bwd_hosts, grads)

    g_out, d_out, m_out, v_out = {}, {}, {}, {}
    for n in SHARDED:
        shp = w[n].shape
        pad = (-shp[1]) % 8
        def rows(t):
            return jnp.pad(t, ((0, 0), (0, pad), (0, 0))).reshape(shp[0] * (shp[1] + pad), shp[2])
        res = _adamw(rows(w[n]), [recv[i][n] for i in range(DEPTH)], rows(m[n]), rows(v[n]), "adamw_" + n)
        g_out[n], d_out[n], m_out[n], v_out[n] = (t.reshape(shp[0], shp[1] + pad, shp[2])[:, :shp[1]] for t in res)
    like = [w[n] for n in SMALL]
    res = _adamw(_pack(like), [recv[0]["small"]], _pack([m[n] for n in SMALL]), _pack([v[n] for n in SMALL]), "adamw_small")
    for dst, pk in zip((g_out, d_out, m_out, v_out), res):
        dst.update(dict(zip(SMALL, _unpack(pk, like))))

    total = lax.psum(loss[0, 0], MESH_AXES)
    return (total, grad_x[None], *[g_out[n] for n in names], *[d_out[n] for n in names],
            *[m_out[n] for n in names], *[v_out[n] for n in names])
```
